```python
import math
import jax, jax.numpy as jnp
from jax import lax
import numpy as np

D_MODEL = 1024
BATCH = 2
SEQ = 8192
DEPTH = 1

HEAD_DIM = 64
N_HEADS_FOX = 8
N_HEADS_DIL = 8
D_FOX = N_HEADS_FOX * HEAD_DIM
D_DIL = N_HEADS_DIL * HEAD_DIM
D_MIX = D_FOX + D_DIL
D_PROJ = 3 * D_FOX + 3 * D_DIL + N_HEADS_FOX
DIL_PATTERNS = ((128, 1), (512, 4), (2048, 16))
BLOCK = 128
ROPE_THETA = 10000.0
N_GROUPS = 4
EXPERTS_PER_GROUP = 8
N_EXPERTS = N_GROUPS * EXPERTS_PER_GROUP
TOP_K_INNER = 2
D_EXPERT = 512
EPS = 1e-6
NEG = -1e30
FORGET_BIAS_INIT = 3.0

kernel_name = "hymba_fox_dilated_hmoe_block"


def rmsnorm(x, g):
    xf = x.astype(jnp.float32)
    y = xf * lax.rsqrt(jnp.mean(xf * xf, axis=-1, keepdims=True) + EPS)
    return (y * g.astype(jnp.float32)).astype(x.dtype)


def rope(x, pos):
    dh = x.shape[-1]
    inv_freq = 1.0 / (ROPE_THETA ** (jnp.arange(0, dh, 2, dtype=jnp.float32) / dh))
    ang = pos[:, None] * inv_freq[None, :]
    cos, sin = jnp.cos(ang), jnp.sin(ang)
    xf = x.astype(jnp.float32)
    x1, x2 = xf[..., : dh // 2], xf[..., dh // 2:]
    out = jnp.concatenate([x1 * cos - x2 * sin, x2 * cos + x1 * sin], axis=-1)
    return out.astype(x.dtype)


def forgetting_attention(q, k, v, logf):
    B, H, S, dh = q.shape
    scale = 1.0 / math.sqrt(dh)
    c = jnp.cumsum(logf, axis=-1)
    nb = S // BLOCK
    qb = q.reshape(B, H, nb, BLOCK, dh).transpose(2, 0, 1, 3, 4)
    cb = c.reshape(B, H, nb, BLOCK).transpose(2, 0, 1, 3)
    kpos = jnp.arange(S)

    def one_block(args):
        qi, ci, i = args
        s = jnp.einsum('bhqd,bhkd->bhqk', qi, k).astype(jnp.float32) * scale
        s = s + ci[..., None] - c[:, :, None, :]
        qpos = i * BLOCK + jnp.arange(BLOCK)
        mask = kpos[None, :] <= qpos[:, None]
        s = jnp.where(mask, s, NEG)
        p = jax.nn.softmax(s, axis=-1)
        return jnp.einsum('bhqk,bhkd->bhqd', p.astype(v.dtype), v)

    out = lax.map(one_block, (qb, cb, jnp.arange(nb)))
    return out.transpose(1, 2, 0, 3, 4).reshape(B, H, S, dh)


def dilated_branch(q, k, v, window, dilation):
    B, H, S, dh = q.shape
    scale = 1.0 / math.sqrt(dh)
    wsub = window // dilation
    unit = dilation * BLOCK
    Sp = -(-S // unit) * unit
    L = Sp // dilation
    nb = L // BLOCK
    pad = ((0, 0), (0, 0), (0, Sp - S), (0, 0))

    def split(t):
        t = jnp.pad(t, pad).reshape(B, H, L, dilation, dh).transpose(0, 1, 3, 2, 4)
        return t.reshape(B, H, dilation, nb, BLOCK, dh)

    def with_prev(t):
        prev = jnp.pad(t, ((0, 0), (0, 0), (0, 0), (1, 0), (0, 0), (0, 0)))[:, :, :, :-1]
        return jnp.concatenate([prev, t], axis=4)

    qs = split(q)
    kw = with_prev(split(k))
    vw = with_prev(split(v))
    s = jnp.einsum('bhrnqd,bhrnkd->bhrnqk', qs, kw).astype(jnp.float32) * scale
    ql = jnp.arange(BLOCK)[:, None]
    kl = jnp.arange(2 * BLOCK)[None, :]
    dist = ql + BLOCK - kl
    kpos = jnp.arange(nb)[:, None, None] * BLOCK + kl[None] - BLOCK
    valid = (dist >= 0) & (dist <= wsub) & (kpos >= 0)
    s = jnp.where(valid, s, NEG)
    m = jnp.max(s, axis=-1, keepdims=True)
    p = jnp.exp(s - m)
    l = jnp.sum(p, axis=-1, keepdims=True)
    o = jnp.einsum('bhrnqk,bhrnkd->bhrnqd', (p / l).astype(v.dtype), vw)
    lse = (m + jnp.log(l))[..., 0]
    o = o.reshape(B, H, dilation, L, dh).transpose(0, 1, 3, 2, 4).reshape(B, H, Sp, dh)[:, :, :S]
    lse = lse.reshape(B, H, dilation, L).transpose(0, 1, 3, 2).reshape(B, H, Sp)[:, :, :S]
    return o, lse


def dilated_attention(q, k, v):
    outs, lses = [], []
    for window, dilation in DIL_PATTERNS:
        o, lse = dilated_branch(q, k, v, window, dilation)
        outs.append(o)
        lses.append(lse)
    w = jax.nn.softmax(jnp.stack(lses, axis=0), axis=0)
    o = jnp.sum(w[..., None] * jnp.stack(outs, axis=0).astype(jnp.float32), axis=0)
    return o.astype(q.dtype)


def token_mixer(xn, w_in, b_forget, q_norm_fox, k_norm_fox, q_norm_dil, k_norm_dil,
                out_norm_fox, out_norm_dil, w_out):
    B, S, _ = xn.shape
    proj = xn @ w_in
    o1 = 3 * D_FOX
    o2 = o1 + 3 * D_DIL
    qa, ka, va = jnp.split(proj[..., :o1], 3, axis=-1)
    qb, kb, vb = jnp.split(proj[..., o1:o2], 3, axis=-1)
    fa = proj[..., o2:]

    def heads(t, h):
        return t.reshape(B, S, h, HEAD_DIM).transpose(0, 2, 1, 3)

    qa = rmsnorm(heads(qa, N_HEADS_FOX), q_norm_fox)
    ka = rmsnorm(heads(ka, N_HEADS_FOX), k_norm_fox)
    va = heads(va, N_HEADS_FOX)
    logf = jax.nn.log_sigmoid(fa.astype(jnp.float32) + b_forget.astype(jnp.float32))
    logf = logf.transpose(0, 2, 1)
    oa = forgetting_attention(qa, ka, va, logf)

    pos = jnp.arange(S, dtype=jnp.float32)
    qb = rope(rmsnorm(heads(qb, N_HEADS_DIL), q_norm_dil), pos)
    kb = rope(rmsnorm(heads(kb, N_HEADS_DIL), k_norm_dil), pos)
    vb = heads(vb, N_HEADS_DIL)
    ob = dilated_attention(qb, kb, vb)

    oa = rmsnorm(oa.transpose(0, 2, 1, 3).reshape(B, S, D_FOX), out_norm_fox)
    ob = rmsnorm(ob.transpose(0, 2, 1, 3).reshape(B, S, D_DIL), out_norm_dil)
    return jnp.concatenate([oa, ob], axis=-1) @ w_out


def hierarchical_moe(xn, w_router_group, b_router_group, w_router_expert, b_router_expert,
                     w_gate, w_up, w_down):
    B, S, D = xn.shape
    t = xn.reshape(B * S, D)
    zg = (t @ w_router_group).astype(jnp.float32) + b_router_group.astype(jnp.float32)
    pg = jax.nn.softmax(zg, axis=-1)
    pg_top, g_sel = lax.top_k(pg, 1)
    ze = jnp.einsum('td,gde->tge', t, w_router_expert).astype(jnp.float32)
    ze = ze + b_router_expert.astype(jnp.float32)
    ze_sel = jnp.take_along_axis(ze, g_sel[:, :, None], axis=1)[:, 0]
    v2, i2 = lax.top_k(ze_sel, TOP_K_INNER)
    gates = jax.nn.softmax(v2, axis=-1) * pg_top
    eid = g_sel * EXPERTS_PER_GROUP + i2
    combine = jnp.sum(jax.nn.one_hot(eid, N_EXPERTS, dtype=jnp.float32) * gates[..., None], axis=1)
    y = jnp.zeros((B * S, D), jnp.float32)
    for e in range(N_EXPERTS):
        h = jax.nn.silu(t @ w_gate[e]) * (t @ w_up[e])
        y = y + combine[:, e:e + 1] * (h @ w_down[e]).astype(jnp.float32)
    return y.astype(xn.dtype).reshape(B, S, D)


def setup_inputs(seed: int = 0) -> dict:
    key = jax.random.key(seed)
    ks = jax.random.split(key, 20)
    f32 = jnp.float32

    def nrm(k, shape, scale):
        return jax.random.normal(k, shape, f32) * scale

    def gain(k, shape):
        return 1.0 + 0.02 * jax.random.normal(k, shape, f32)

    return {
        "x": jax.random.normal(ks[0], (BATCH, SEQ, D_MODEL), f32),
        "norm_mix": gain(ks[1], (DEPTH, D_MODEL)),
        "w_in": nrm(ks[2], (DEPTH, D_MODEL, D_PROJ), D_MODEL ** -0.5),
        "b_forget": FORGET_BIAS_INIT + 0.1 * jax.random.normal(ks[3], (DEPTH, N_HEADS_FOX), f32),
        "q_norm_fox": gain(ks[4], (DEPTH, HEAD_DIM)),
        "k_norm_fox": gain(ks[5], (DEPTH, HEAD_DIM)),
        "q_norm_dil": gain(ks[6], (DEPTH, HEAD_DIM)),
        "k_norm_dil": gain(ks[7], (DEPTH, HEAD_DIM)),
        "out_norm_fox": gain(ks[8], (DEPTH, D_FOX)),
        "out_norm_dil": gain(ks[9], (DEPTH, D_DIL)),
        "w_out": nrm(ks[10], (DEPTH, D_MIX, D_MODEL), D_MIX ** -0.5),
        "norm_ffn": gain(ks[11], (DEPTH, D_MODEL)),
        "w_router_group": nrm(ks[12], (DEPTH, D_MODEL, N_GROUPS), D_MODEL ** -0.5),
        "b_router_group": nrm(ks[13], (DEPTH, N_GROUPS), 0.01),
        "w_router_expert": nrm(ks[14], (DEPTH, N_GROUPS, D_MODEL, EXPERTS_PER_GROUP), D_MODEL ** -0.5),
        "b_router_expert": nrm(ks[15], (DEPTH, N_GROUPS, EXPERTS_PER_GROUP), 0.01),
        "w_gate": nrm(ks[16], (DEPTH, N_EXPERTS, D_MODEL, D_EXPERT), D_MODEL ** -0.5),
        "w_up": nrm(ks[17], (DEPTH, N_EXPERTS, D_MODEL, D_EXPERT), D_MODEL ** -0.5),
        "w_down": nrm(ks[18], (DEPTH, N_EXPERTS, D_EXPERT, D_MODEL), D_EXPERT ** -0.5),
    }


def reference(x, norm_mix, w_in, b_forget, q_norm_fox, k_norm_fox, q_norm_dil, k_norm_dil,
              out_norm_fox, out_norm_dil, w_out, norm_ffn, w_router_group, b_router_group,
              w_router_expert, b_router_expert, w_gate, w_up, w_down):
    h = x
    for l in range(DEPTH):
        xn = rmsnorm(h, norm_mix[l])
        h = h + token_mixer(xn, w_in[l], b_forget[l], q_norm_fox[l], k_norm_fox[l],
                            q_norm_dil[l], k_norm_dil[l], out_norm_fox[l], out_norm_dil[l],
                            w_out[l])
        hn = rmsnorm(h, norm_ffn[l])
        h = h + hierarchical_moe(hn, w_router_group[l], b_router_group[l], w_router_expert[l],
                                 b_router_expert[l], w_gate[l], w_up[l], w_down[l])
    return h
```

```python
import functools
import math

import jax
import jax.numpy as jnp
from jax import lax
from jax.experimental import pallas as pl
from jax.experimental.pallas import tpu as pltpu

F32 = jnp.float32
BF16 = jnp.bfloat16
I32 = jnp.int32

D_MODEL = 1024
HEAD_DIM = 64
N_HEADS = 8
D_GRP = N_HEADS * HEAD_DIM
LANES = 128
HEADS_PER_TILE = LANES // HEAD_DIM
N_PAIRS = D_GRP // LANES
DIL_PATTERNS = ((128, 1), (512, 4), (2048, 16))
BLOCK = 128
ROPE_THETA = 10000.0
N_GROUPS = 4
EXPERTS_PER_GROUP = 8
N_EXPERTS = N_GROUPS * EXPERTS_PER_GROUP
D_EXPERT = 512
EPS = 1e-6
NEG = -1e30

TM_IN = 512
TQ = 256
DIL_SPAN = 2048
TM_OUT = 512
TM_ROWS = 256
TILE_E = 256
ROUTER_ROWS = 8 + N_EXPERTS
VMEM_LIMIT = 56 * 1024 * 1024


def _cparams(sem):
    return pltpu.CompilerParams(dimension_semantics=sem, vmem_limit_bytes=VMEM_LIMIT)


def _inproj_kernel(x_ref, gmix_ref, w_ref, wf_ref, bf_ref, gqa_ref, gka_ref, gqb_ref, gkb_ref,
                   bd_ref, cos_ref, sin_ref, tri_ref,
                   qa_ref, ka_ref, va_ref, qb_ref, kb_ref, vb_ref, c_ref, carry_ref):
    @pl.when(pl.program_id(1) == 0)
    def _():
        carry_ref[...] = jnp.zeros_like(carry_ref)

    x = x_ref[0]
    ms = jnp.mean(x * x, axis=-1, keepdims=True)
    xn = (x * lax.rsqrt(ms + EPS) * gmix_ref[...]).astype(BF16)

    def seg(j):
        return jnp.dot(xn, w_ref[:, j * D_GRP:(j + 1) * D_GRP], preferred_element_type=F32)

    def head_norm(y, g_ref, scale):
        ss = jnp.dot((y * y).astype(BF16), bd_ref[...], preferred_element_type=F32) * (1.0 / HEAD_DIM)
        return y * lax.rsqrt(ss + EPS) * (g_ref[...] * scale)

    cos = cos_ref[...]
    sin = sin_ref[...]
    lane = lax.broadcasted_iota(I32, (x.shape[0], LANES), 1)
    first_half = (lane % HEAD_DIM) < (HEAD_DIM // 2)

    def rope(y):
        outs = []
        for j in range(N_PAIRS):
            ys = y[:, j * LANES:(j + 1) * LANES]
            partner = jnp.where(first_half, pltpu.roll(ys, LANES - HEAD_DIM // 2, 1),
                                pltpu.roll(ys, HEAD_DIM // 2, 1))
            outs.append(ys * cos + partner * sin)
        return jnp.concatenate(outs, axis=1)

    scale = 1.0 / math.sqrt(HEAD_DIM)
    qa_ref[0] = head_norm(seg(0), gqa_ref, scale).astype(qa_ref.dtype)
    ka_ref[0] = head_norm(seg(1), gka_ref, 1.0).astype(ka_ref.dtype)
    va_ref[0] = seg(2).astype(va_ref.dtype)
    qb_ref[0] = rope(head_norm(seg(3), gqb_ref, scale)).astype(qb_ref.dtype)
    kb_ref[0] = rope(head_norm(seg(4), gkb_ref, 1.0)).astype(kb_ref.dtype)
    vb_ref[0] = seg(5).astype(vb_ref.dtype)

    fa = jnp.dot(xn, wf_ref[...], preferred_element_type=F32) + bf_ref[...]
    logf = jnp.minimum(fa, 0.0) - jnp.log1p(jnp.exp(-jnp.abs(fa)))
    c = jnp.dot(tri_ref[...], logf, preferred_element_type=F32,
                precision=lax.Precision.HIGHEST) + carry_ref[...]
    carry_ref[...] = c[c.shape[0] - 1:, :]
    c_ref[0] = c[:, :N_HEADS]


def _inproj(x, gmix, w_main, w_f, b_f, gqa, gka, gqb, gkb, bd, cos_t, sin_t, tri):
    B, S, D = x.shape
    tm = TM_IN
    const = lambda shape: pl.BlockSpec(shape, lambda b, i: (0,) * len(shape))
    tok = lambda w, dt: jax.ShapeDtypeStruct((B, S, w), dt)
    tok_spec = lambda w: pl.BlockSpec((1, tm, w), lambda b, i: (b, i, 0))
    return pl.pallas_call(
        _inproj_kernel,
        grid=(B, S // tm),
        in_specs=[tok_spec(D), const((1, D)), const(w_main.shape), const(w_f.shape), const((1, LANES)),
                  const((1, D_GRP)), const((1, D_GRP)), const((1, D_GRP)), const((1, D_GRP)),
                  const((D_GRP, D_GRP)),
                  pl.BlockSpec((tm, LANES), lambda b, i: (i, 0)),
                  pl.BlockSpec((tm, LANES), lambda b, i: (i, 0)),
                  const((tm, tm))],
        out_specs=[tok_spec(D_GRP)] * 6 + [tok_spec(N_HEADS)],
        out_shape=[tok(D_GRP, BF16), tok(D_GRP, BF16), tok(D_GRP, BF16),
                   tok(D_GRP, F32), tok(D_GRP, F32), tok(D_GRP, F32), tok(N_HEADS, F32)],
        scratch_shapes=[pltpu.VMEM((1, LANES), F32)],
        compiler_params=_cparams(("arbitrary", "arbitrary")),
        name="inproj",
    )(x, gmix, w_main, w_f, b_f, gqa, gka, gqb, gkb, bd, cos_t, sin_t, tri)


def _fox_kernel(q_ref, k_ref, v_ref, cq_ref, ck_ref, o_ref):
    hp = pl.program_id(1)
    qi = pl.program_id(2)
    tq = q_ref.shape[1]
    q = q_ref[0]
    cq_all = cq_ref[0]
    lane = lax.broadcasted_iota(I32, (tq, LANES), 1)
    head_lane = lax.broadcasted_iota(I32, cq_all.shape, 1)
    row = lax.broadcasted_iota(I32, (tq, tq), 0)
    col = lax.broadcasted_iota(I32, (tq, tq), 1)
    causal = row >= col
    outs = []
    for j in range(HEADS_PER_TILE):
        h = hp * HEADS_PER_TILE + j
        qj = jnp.where(lane // HEAD_DIM == j, q, jnp.zeros_like(q))
        cq = jnp.sum(jnp.where(head_lane == h, cq_all, 0.0), axis=-1, keepdims=True)

        def step(i, carry, masked, qj=qj, cq=cq, h=h):
            m, l, acc = carry
            start = pl.multiple_of(i * tq, tq)
            k = k_ref[0, pl.ds(start, tq), :]
            v = v_ref[0, pl.ds(start, tq), :]
            ck = ck_ref[0, pl.ds(h, 1), pl.ds(start, tq)]
            s = lax.dot_general(qj, k, (((1,), (1,)), ((), ())), preferred_element_type=F32)
            s = s + cq - ck
            if masked:
                s = jnp.where(causal, s, NEG)
            m_new = jnp.maximum(m, jnp.max(s, axis=-1, keepdims=True))
            alpha = jnp.exp(m - m_new)
            p = jnp.exp(s - m_new)
            l = alpha * l + jnp.sum(p, axis=-1, keepdims=True)
            acc = alpha * acc + jnp.dot(p.astype(BF16), v, preferred_element_type=F32)
            return m_new, l, acc

        init = (jnp.full((tq, 1), NEG, F32), jnp.zeros((tq, 1), F32), jnp.zeros((tq, LANES), F32))
        carry = lax.fori_loop(0, qi, functools.partial(step, masked=False), init)
        m, l, acc = step(qi, carry, masked=True)
        outs.append(acc / l)
    o_ref[0] = jnp.where(lane < HEAD_DIM, outs[0], outs[1]).astype(o_ref.dtype)


def _fox(qa, ka, va, c, c_t):
    B, S, _ = qa.shape
    return pl.pallas_call(
        _fox_kernel,
        grid=(B, N_PAIRS, S // TQ),
        in_specs=[pl.BlockSpec((1, TQ, LANES), lambda b, hp, i: (b, i, hp)),
                  pl.BlockSpec((1, S, LANES), lambda b, hp, i: (b, 0, hp)),
                  pl.BlockSpec((1, S, LANES), lambda b, hp, i: (b, 0, hp)),
                  pl.BlockSpec((1, TQ, N_HEADS), lambda b, hp, i: (b, i, 0)),
                  pl.BlockSpec((1, N_HEADS, S), lambda b, hp, i: (b, 0, 0))],
        out_specs=pl.BlockSpec((1, TQ, LANES), lambda b, hp, i: (b, i, hp)),
        out_shape=jax.ShapeDtypeStruct((B, S, D_GRP), F32),
        compiler_params=_cparams(("arbitrary", "arbitrary", "arbitrary")),
        name="fox",
    )(qa, ka, va, c, c_t)


def _dilated_kernel(q_ref, kp_ref, kc_ref, vp_ref, vc_ref, o_ref, qq, kk, vv, osc, lsc):
    u = pl.program_id(1)
    span = q_ref.shape[1]
    qq[...] = q_ref[0]
    kk[0:span, :] = kp_ref[0]
    kk[span:2 * span, :] = kc_ref[0]
    vv[0:span, :] = vp_ref[0]
    vv[span:2 * span, :] = vc_ref[0]

    lane = lax.broadcasted_iota(I32, (BLOCK, LANES), 1)
    ql = lax.broadcasted_iota(I32, (BLOCK, 2 * BLOCK), 0)
    kl = lax.broadcasted_iota(I32, (BLOCK, 2 * BLOCK), 1)
    dist = ql + BLOCK - kl
    band = (dist >= 0) & (dist <= BLOCK)
    in_prev = kl < BLOCK

    def attend(q_start, k_start, d, first):
        qs = qq[pl.ds(q_start, BLOCK, stride=d), :].astype(BF16)
        ks = kk[pl.ds(k_start, 2 * BLOCK, stride=d), :].astype(BF16)
        vs = vv[pl.ds(k_start, 2 * BLOCK, stride=d), :].astype(BF16)
        valid = band & jnp.logical_not(jnp.logical_and(first, in_prev))
        o_heads, lse_heads = [], []
        for j in range(HEADS_PER_TILE):
            qj = jnp.where(lane // HEAD_DIM == j, qs, jnp.zeros_like(qs))
            s = lax.dot_general(qj, ks, (((1,), (1,)), ((), ())), preferred_element_type=F32)
            s = jnp.where(valid, s, NEG)
            m = jnp.max(s, axis=-1, keepdims=True)
            p = jnp.exp(s - m)
            l = jnp.sum(p, axis=-1, keepdims=True)
            o_heads.append(jnp.dot((p / l).astype(BF16), vs, preferred_element_type=F32))
            lse_heads.append(m + jnp.log(l))
        o = jnp.where(lane < HEAD_DIM, o_heads[0], o_heads[1])
        lse = jnp.where(lane < HEAD_DIM, lse_heads[0], lse_heads[1])
        return o, lse

    for pidx, (window, d) in enumerate(DIL_PATTERNS):
        assert window // d == BLOCK
        unit = d * BLOCK
        n_units = span // unit

        def body(idx, _, pidx=pidx, d=d, unit=unit):
            w = idx // d
            r = idx - w * d
            q_start = w * unit + r
            k_start = span - unit + q_start
            o, lse = attend(q_start, k_start, d, jnp.logical_and(u == 0, w == 0))
            osc[pidx, pl.ds(q_start, BLOCK, stride=d), :] = o
            lsc[pidx, pl.ds(q_start, BLOCK, stride=d), :] = lse
            return 0

        lax.fori_loop(0, n_units * d, body, 0)

    mx = jnp.maximum(jnp.maximum(lsc[0], lsc[1]), lsc[2])
    num = jnp.zeros((span, LANES), F32)
    den = jnp.zeros((span, LANES), F32)
    for pidx in range(len(DIL_PATTERNS)):
        e = jnp.exp(lsc[pidx] - mx)
        num = num + e * osc[pidx]
        den = den + e
    o_ref[0] = (num / den).astype(o_ref.dtype)


def _dilated(qb, kb, vb):
    B, S, _ = qb.shape
    span = DIL_SPAN
    cur = pl.BlockSpec((1, span, LANES), lambda b, u, hp: (b, u, hp))
    prev = pl.BlockSpec((1, span, LANES), lambda b, u, hp: (b, jnp.maximum(u - 1, 0), hp))
    return pl.pallas_call(
        _dilated_kernel,
        grid=(B, S // span, N_PAIRS),
        in_specs=[cur, prev, cur, prev, cur],
        out_specs=cur,
        out_shape=jax.ShapeDtypeStruct((B, S, D_GRP), F32),
        scratch_shapes=[pltpu.VMEM((span, LANES), F32),
                        pltpu.VMEM((2 * span, LANES), F32), pltpu.VMEM((2 * span, LANES), F32),
                        pltpu.VMEM((len(DIL_PATTERNS), span, LANES), F32),
                        pltpu.VMEM((len(DIL_PATTERNS), span, LANES), F32)],
        compiler_params=_cparams(("arbitrary", "arbitrary", "arbitrary")),
        name="dilated",
    )(qb, kb, kb, vb, vb)


def _outproj_kernel(oa_ref, ob_ref, x_ref, gfox_ref, gdil_ref, wo_ref, gffn_ref, wr_ref, br_ref, upper_ref,
                    h_ref, hn_ref, eid_ref, gate_ref, rank_ref, cnt_ref, run_ref):
    @pl.when(pl.program_id(0) == 0)
    def _():
        run_ref[...] = jnp.zeros_like(run_ref)

    def norm(y, g):
        ms = jnp.mean(y * y, axis=-1, keepdims=True)
        return y * lax.rsqrt(ms + EPS) * g

    a = norm(oa_ref[...], gfox_ref[...]).astype(BF16)
    b = norm(ob_ref[...], gdil_ref[...]).astype(BF16)
    mix = (jnp.dot(a, wo_ref[0:D_GRP, :], preferred_element_type=F32)
           + jnp.dot(b, wo_ref[D_GRP:2 * D_GRP, :], preferred_element_type=F32))
    h = x_ref[...] + mix
    h_ref[...] = h
    hn = norm(h, gffn_ref[...])
    hn_ref[...] = hn

    z = lax.dot_general(wr_ref[...], hn, (((1,), (1,)), ((), ())), preferred_element_type=F32,
                        precision=lax.Precision.HIGHEST) + br_ref[...]
    tm = z.shape[1]
    best = z[0:1, :]
    g_sel = jnp.zeros((1, tm), I32)
    for g in range(1, N_GROUPS):
        better = z[g:g + 1, :] > best
        g_sel = jnp.where(better, g, g_sel)
        best = jnp.maximum(best, z[g:g + 1, :])
    den = jnp.zeros((1, tm), F32)
    for g in range(N_GROUPS):
        den = den + jnp.exp(z[g:g + 1, :] - best)
    pg_top = 1.0 / den

    ze = jnp.zeros((EXPERTS_PER_GROUP, tm), F32)
    for g in range(N_GROUPS):
        ze = jnp.where(g_sel == g, z[8 + g * EXPERTS_PER_GROUP:8 + (g + 1) * EXPERTS_PER_GROUP, :], ze)
    e_iota = lax.broadcasted_iota(I32, ze.shape, 0)
    v1 = jnp.max(ze, axis=0, keepdims=True)
    i1 = jnp.min(jnp.where(ze == v1, e_iota, EXPERTS_PER_GROUP), axis=0, keepdims=True)
    ze2 = jnp.where(e_iota == i1, -jnp.inf, ze)
    v2 = jnp.max(ze2, axis=0, keepdims=True)
    i2 = jnp.min(jnp.where(ze2 == v2, e_iota, EXPERTS_PER_GROUP), axis=0, keepdims=True)
    e2 = jnp.exp(v2 - v1)
    inv = 1.0 / (1.0 + e2)
    gate1 = inv * pg_top
    gate2 = e2 * inv * pg_top
    eid1 = g_sel * EXPERTS_PER_GROUP + i1
    eid2 = g_sel * EXPERTS_PER_GROUP + i2

    x_iota = lax.broadcasted_iota(I32, (N_EXPERTS, tm), 0)
    hot1 = x_iota == eid1
    hot2 = x_iota == eid2
    multi = jnp.logical_or(hot1, hot2)
    before = jnp.dot(multi.astype(BF16), upper_ref[...], preferred_element_type=F32)
    slot = before + run_ref[:, 0:1]
    rank1 = jnp.sum(jnp.where(hot1, slot, 0.0), axis=0, keepdims=True)
    rank2 = jnp.sum(jnp.where(hot2, slot, 0.0), axis=0, keepdims=True)
    run_ref[...] = run_ref[...] + jnp.sum(multi.astype(F32), axis=1, keepdims=True)

    eid_ref[...] = jnp.concatenate([eid1, eid2], axis=0)
    gate_ref[...] = jnp.concatenate([gate1, gate2], axis=0)
    rank_ref[...] = jnp.concatenate([rank1, rank2], axis=0).astype(I32)
    cnt_ref[...] = run_ref[...].astype(I32)


def _outproj(oa, ob, x2, gfox, gdil, w_out, gffn, w_r, b_r, upper):
    T, D = x2.shape
    tm = TM_OUT
    const = lambda shape: pl.BlockSpec(shape, lambda i: (0,) * len(shape))
    tok = lambda w: pl.BlockSpec((tm, w), lambda i: (i, 0))
    lanes2 = pl.BlockSpec((2, tm), lambda i: (0, i))
    return pl.pallas_call(
        _outproj_kernel,
        grid=(T // tm,),
        in_specs=[tok(D_GRP), tok(D_GRP), tok(D), const((1, D_GRP)), const((1, D_GRP)), const((D, D)),
                  const((1, D)), const((ROUTER_ROWS, D)), const((ROUTER_ROWS, 1)), const((tm, tm))],
        out_specs=[tok(D), tok(D), lanes2, lanes2, lanes2, const((N_EXPERTS, LANES))],
        out_shape=[jax.ShapeDtypeStruct((T, D), F32), jax.ShapeDtypeStruct((T, D), F32),
                   jax.ShapeDtypeStruct((2, T), I32), jax.ShapeDtypeStruct((2, T), F32),
                   jax.ShapeDtypeStruct((2, T), I32), jax.ShapeDtypeStruct((N_EXPERTS, LANES), I32)],
        scratch_shapes=[pltpu.VMEM((N_EXPERTS, LANES), F32)],
        compiler_params=_cparams(("arbitrary",)),
        name="outproj",
    )(oa, ob, x2, gfox, gdil, w_out, gffn, w_r, b_r, upper)


def _row_copy(src_ref, src_row, dst_ref, dst_row, sem):
    return pltpu.make_async_copy(src_ref.at[pl.ds(src_row, 1), :], dst_ref.at[pl.ds(dst_row, 1), :], sem)


def _scatter_kernel(pos_ref, hn_ref, init_ref, xs_ref, sem):
    del init_ref
    tm = hn_ref.shape[0]

    def start(r, _):
        for k in range(2):
            _row_copy(hn_ref, r, xs_ref, pos_ref[k, r], sem).start()
        return 0

    def wait(r, _):
        for k in range(2):
            _row_copy(hn_ref, r, xs_ref, pos_ref[k, r], sem).wait()
        return 0

    lax.fori_loop(0, tm, start, 0)
    lax.fori_loop(0, tm, wait, 0)


def _scatter(pos, hn, xs_init):
    T, D = hn.shape
    tm = TM_ROWS
    return pl.pallas_call(
        _scatter_kernel,
        grid=(T // tm,),
        in_specs=[pl.BlockSpec((2, tm), lambda i: (0, i), memory_space=pltpu.SMEM),
                  pl.BlockSpec((tm, D), lambda i: (i, 0)),
                  pl.BlockSpec(memory_space=pl.ANY)],
        out_specs=pl.BlockSpec(memory_space=pl.ANY),
        out_shape=jax.ShapeDtypeStruct(xs_init.shape, xs_init.dtype),
        input_output_aliases={2: 0},
        scratch_shapes=[pltpu.SemaphoreType.DMA(())],
        compiler_params=_cparams(("arbitrary",)),
        name="scatter_rows",
    )(pos, hn, xs_init)


def _experts_kernel(te_ref, tv_ref, xs_ref, wg_ref, wu_ref, wd_ref, y_ref):
    i = pl.program_id(0)

    @pl.when(tv_ref[i] != 0)
    def _():
        x = xs_ref[...].astype(BF16)
        g = jnp.dot(x, wg_ref[0].astype(BF16), preferred_element_type=F32)
        up = jnp.dot(x, wu_ref[0].astype(BF16), preferred_element_type=F32)
        hmid = (g * jax.nn.sigmoid(g) * up).astype(BF16)
        y_ref[...] = jnp.dot(hmid, wd_ref[0].astype(BF16), preferred_element_type=F32)

    @pl.when(tv_ref[i] == 0)
    def _():
        y_ref[...] = jnp.zeros_like(y_ref)


def _experts(tile_expert, tile_valid, xs, w_gate, w_up, w_down):
    n_rows, D = xs.shape
    n_tiles = n_rows // TILE_E
    grid_spec = pltpu.PrefetchScalarGridSpec(
        num_scalar_prefetch=2,
        grid=(n_tiles,),
        in_specs=[pl.BlockSpec((TILE_E, D), lambda i, te, tv: (i, 0)),
                  pl.BlockSpec((1, D, D_EXPERT), lambda i, te, tv: (te[i], 0, 0)),
                  pl.BlockSpec((1, D, D_EXPERT), lambda i, te, tv: (te[i], 0, 0)),
                  pl.BlockSpec((1, D_EXPERT, D), lambda i, te, tv: (te[i], 0, 0))],
        out_specs=pl.BlockSpec((TILE_E, D), lambda i, te, tv: (i, 0)),
    )
    return pl.pallas_call(
        _experts_kernel,
        grid_spec=grid_spec,
        out_shape=jax.ShapeDtypeStruct((n_rows, D), F32),
        compiler_params=_cparams(("arbitrary",)),
        name="experts",
    )(tile_expert, tile_valid, xs, w_gate, w_up, w_down)


def _combine_kernel(pos_ref, h_ref, gate_ref, y_ref, o_ref, ybuf, sem):
    tm = h_ref.shape[0]

    def start(r, _):
        for k in range(2):
            _row_copy(y_ref, pos_ref[k, r], ybuf.at[k], r, sem).start()
        return 0

    def wait(r, _):
        for k in range(2):
            _row_copy(y_ref, pos_ref[k, r], ybuf.at[k], r, sem).wait()
        return 0

    lax.fori_loop(0, tm, start, 0)
    lax.fori_loop(0, tm, wait, 0)
    g = gate_ref[...]
    o_ref[...] = h_ref[...] + g[:, 0:1] * ybuf[0] + g[:, 1:2] * ybuf[1]


def _combine(pos, h, gate_t, y):
    T, D = h.shape
    tm = TM_ROWS
    return pl.pallas_call(
        _combine_kernel,
        grid=(T // tm,),
        in_specs=[pl.BlockSpec((2, tm), lambda i: (0, i), memory_space=pltpu.SMEM),
                  pl.BlockSpec((tm, D), lambda i: (i, 0)),
                  pl.BlockSpec((tm, 2), lambda i: (i, 0)),
                  pl.BlockSpec(memory_space=pl.ANY)],
        out_specs=pl.BlockSpec((tm, D), lambda i: (i, 0)),
        out_shape=jax.ShapeDtypeStruct((T, D), F32),
        scratch_shapes=[pltpu.VMEM((2, tm, D), F32), pltpu.SemaphoreType.DMA(())],
        compiler_params=_cparams(("arbitrary",)),
        name="combine",
    )(pos, h, gate_t, y)


def _rope_tables(S):
    inv_freq = 1.0 / (ROPE_THETA ** (jnp.arange(0, HEAD_DIM, 2, dtype=F32) / HEAD_DIM))
    ang = jnp.arange(S, dtype=F32)[:, None] * inv_freq[None, :]
    cos, sin = jnp.cos(ang), jnp.sin(ang)
    cos_t = jnp.tile(cos, (1, 2 * HEADS_PER_TILE))
    sin_t = jnp.tile(jnp.concatenate([-sin, sin], axis=1), (1, HEADS_PER_TILE))
    return cos_t, sin_t


def _layer(x, norm_mix, w_in, b_forget, q_norm_fox, k_norm_fox, q_norm_dil, k_norm_dil,
           out_norm_fox, out_norm_dil, w_out, norm_ffn, w_router_group, b_router_group,
           w_router_expert, b_router_expert, w_gate, w_up, w_down):
    B, S, D = x.shape
    T = B * S
    n_main = 6 * D_GRP

    w_main = w_in[:, :n_main].astype(BF16)
    w_f = jnp.pad(w_in[:, n_main:], ((0, 0), (0, LANES - N_HEADS))).astype(BF16)
    b_f = jnp.pad(b_forget, (0, LANES - N_HEADS))[None, :]
    per_head = lambda g: jnp.tile(g, N_HEADS)[None, :]
    bd = jnp.kron(jnp.eye(N_HEADS, dtype=F32), jnp.ones((HEAD_DIM, HEAD_DIM), F32)).astype(BF16)
    cos_t, sin_t = _rope_tables(S)
    tri = jnp.tril(jnp.ones((TM_IN, TM_IN), F32))
    upper = jnp.triu(jnp.ones((TM_OUT, TM_OUT), F32), k=1).astype(BF16)
    w_r = jnp.concatenate([
        jnp.pad(w_router_group.T, ((0, 8 - N_GROUPS), (0, 0))),
        w_router_expert.transpose(0, 2, 1).reshape(N_EXPERTS, D)], axis=0)
    b_r = jnp.concatenate([jnp.pad(b_router_group, (0, 8 - N_GROUPS)), b_router_expert.reshape(-1)])[:, None]

    qa, ka, va, qb, kb, vb, c = _inproj(
        x, norm_mix[None, :], w_main, w_f, b_f, per_head(q_norm_fox), per_head(k_norm_fox),
        per_head(q_norm_dil), per_head(k_norm_dil), bd, cos_t, sin_t, tri)
    oa = _fox(qa, ka, va, c, c.transpose(0, 2, 1))
    ob = _dilated(qb, kb, vb)

    h, hn, eid, gate, rank, cnt = _outproj(
        oa.reshape(T, D_GRP), ob.reshape(T, D_GRP), x.reshape(T, D), out_norm_fox[None, :],
        out_norm_dil[None, :], w_out.astype(BF16), norm_ffn[None, :], w_r, b_r, upper)

    counts = cnt[:, 0]
    padded = ((counts + TILE_E - 1) // TILE_E) * TILE_E
    ends = jnp.cumsum(padded)
    starts = ends - padded
    pos = starts[eid] + rank
    n_tiles = (2 * T) // TILE_E + N_EXPERTS
    tile_start = jnp.arange(n_tiles, dtype=I32) * TILE_E
    tile_expert = jnp.minimum(jnp.sum((ends[None, :] <= tile_start[:, None]).astype(I32), axis=1), N_EXPERTS - 1)
    tile_valid = (tile_start < ends[-1]).astype(I32)

    xs = _scatter(pos, hn, jnp.zeros((n_tiles * TILE_E, D), F32))
    y = _experts(tile_expert, tile_valid, xs, w_gate, w_up, w_down)
    out = _combine(pos, h, gate.T, y)
    return out.reshape(B, S, D)


def kernel(x, norm_mix, w_in, b_forget, q_norm_fox, k_norm_fox, q_norm_dil, k_norm_dil, out_norm_fox,
           out_norm_dil, w_out, norm_ffn, w_router_group, b_router_group, w_router_expert,
           b_router_expert, w_gate, w_up, w_down):
    h = x
    for l in range(norm_mix.shape[0]):
        h = _layer(h, norm_mix[l], w_in[l], b_forget[l], q_norm_fox[l], k_norm_fox[l], q_norm_dil[l],
                   k_norm_dil[l], out_norm_fox[l], out_norm_dil[l], w_out[l], norm_ffn[l],
                   w_router_group[l], b_router_group[l], w_router_expert[l], b_router_expert[l],
                   w_gate[l], w_up[l], w_down[l])
    return h
```

```python
import functools
import math

import jax
import jax.numpy as jnp
from jax import lax
from jax.experimental import pallas as pl
from jax.experimental.pallas import tpu as pltpu

F32 = jnp.float32
BF16 = jnp.bfloat16
I32 = jnp.int32

D_MODEL = 1024
HEAD_DIM = 64
N_HEADS = 8
D_GRP = N_HEADS * HEAD_DIM
LANES = 128
HEADS_PER_TILE = LANES // HEAD_DIM
N_PAIRS = D_GRP // LANES
DIL_PATTERNS = ((128, 1), (512, 4), (2048, 16))
BLOCK = 128
ROPE_THETA = 10000.0
N_GROUPS = 4
EXPERTS_PER_GROUP = 8
N_EXPERTS = N_GROUPS * EXPERTS_PER_GROUP
D_EXPERT = 512
EPS = 1e-6
NEG = -1e30
LOG2E = 1.4426950408889634

TM_IN = 512
TQ = 512
TK_WIDE = 1024
DIL_SPAN = 2048
DIL_UNROLL = 4
TM_OUT = 512
TM_ROWS = 256
TILE_E = 256
ROUTER_ROWS = 8 + N_EXPERTS
VMEM_LIMIT = 56 * 1024 * 1024


def _cparams(sem):
    return pltpu.CompilerParams(dimension_semantics=sem, vmem_limit_bytes=VMEM_LIMIT)


def _inproj_kernel(x_ref, gmix_ref, w_ref, wf_ref, bf_ref, gqa_ref, gka_ref, gqb_ref, gkb_ref,
                   bd_ref, cos_ref, sin_ref, tri_ref,
                   qa_ref, ka_ref, va_ref, qb_ref, kb_ref, vb_ref, carry_ref):
    @pl.when(pl.program_id(1) == 0)
    def _():
        carry_ref[...] = jnp.zeros_like(carry_ref)

    x = x_ref[0]
    ms = jnp.mean(x * x, axis=-1, keepdims=True)
    xn = (x * lax.rsqrt(ms + EPS) * gmix_ref[...]).astype(BF16)

    def seg(j):
        return jnp.dot(xn, w_ref[:, j * D_GRP:(j + 1) * D_GRP], preferred_element_type=F32)

    def head_norm(y, g_ref, scale):
        ss = jnp.dot((y * y).astype(BF16), bd_ref[...], preferred_element_type=F32) * (1.0 / HEAD_DIM)
        return y * lax.rsqrt(ss + EPS) * (g_ref[...] * scale)

    cos = cos_ref[...]
    sin = sin_ref[...]
    lane = lax.broadcasted_iota(I32, (x.shape[0], LANES), 1)
    first_half = (lane % HEAD_DIM) < (HEAD_DIM // 2)

    def rope(y):
        outs = []
        for j in range(N_PAIRS):
            ys = y[:, j * LANES:(j + 1) * LANES]
            partner = jnp.where(first_half, pltpu.roll(ys, LANES - HEAD_DIM // 2, 1),
                                pltpu.roll(ys, HEAD_DIM // 2, 1))
            outs.append(ys * cos + partner * sin)
        return jnp.concatenate(outs, axis=1)

    scale = 1.0 / math.sqrt(HEAD_DIM)
    va_ref[0] = seg(2).astype(va_ref.dtype)
    qb_ref[0] = rope(head_norm(seg(3), gqb_ref, scale)).astype(qb_ref.dtype)
    kb_ref[0] = rope(head_norm(seg(4), gkb_ref, 1.0)).astype(kb_ref.dtype)
    vb_ref[0] = seg(5).astype(vb_ref.dtype)

    fa = jnp.dot(xn, wf_ref[...], preferred_element_type=F32) + bf_ref[...]
    logf = jnp.minimum(fa, 0.0) - jnp.log1p(jnp.exp(-jnp.abs(fa)))
    c = jnp.dot(tri_ref[...], logf, preferred_element_type=F32,
                precision=lax.Precision.HIGHEST) + carry_ref[...]
    carry_ref[...] = c[c.shape[0] - 1:, :]

    qa = head_norm(seg(0), gqa_ref, scale * LOG2E)
    ka = head_norm(seg(1), gka_ref, 1.0)
    c2 = c * LOG2E
    ones = jnp.where((lane >= HEAD_DIM + 3) & (lane < HEAD_DIM + 6), 1.0, 0.0)
    for h in range(N_HEADS):
        cb = jnp.broadcast_to(c2[:, h:h + 1], (x.shape[0], LANES))
        hi = cb.astype(BF16).astype(F32)
        mid = (cb - hi).astype(BF16).astype(F32)
        lo = cb - hi - mid
        pieces = jnp.where(lane == HEAD_DIM, hi, jnp.where(lane == HEAD_DIM + 1, mid,
                           jnp.where(lane == HEAD_DIM + 2, lo, 0.0)))
        q_extra = pieces + ones
        k_extra = jnp.where((lane >= HEAD_DIM) & (lane < HEAD_DIM + 3), 1.0, 0.0) \
            - pltpu.roll(pieces, 3, 1)
        j, odd = divmod(h, HEADS_PER_TILE)
        qp = qa[:, j * LANES:(j + 1) * LANES]
        kp = ka[:, j * LANES:(j + 1) * LANES]
        if odd:
            qp = pltpu.roll(qp, HEAD_DIM, 1)
            kp = pltpu.roll(kp, HEAD_DIM, 1)
        qa_ref[0, h] = jnp.where(lane < HEAD_DIM, qp, q_extra).astype(qa_ref.dtype)
        ka_ref[0, h] = jnp.where(lane < HEAD_DIM, kp, k_extra).astype(ka_ref.dtype)


def _inproj(x, gmix, w_main, w_f, b_f, gqa, gka, gqb, gkb, bd, cos_t, sin_t, tri):
    B, S, D = x.shape
    tm = TM_IN
    const = lambda shape: pl.BlockSpec(shape, lambda b, i: (0,) * len(shape))
    tok = lambda w, dt: jax.ShapeDtypeStruct((B, S, w), dt)
    tok_spec = lambda w: pl.BlockSpec((1, tm, w), lambda b, i: (b, i, 0))
    head_spec = pl.BlockSpec((1, N_HEADS, tm, LANES), lambda b, i: (b, 0, i, 0))
    head_shape = jax.ShapeDtypeStruct((B, N_HEADS, S, LANES), BF16)
    return pl.pallas_call(
        _inproj_kernel,
        grid=(B, S // tm),
        in_specs=[tok_spec(D), const((1, D)), const(w_main.shape), const(w_f.shape), const((1, LANES)),
                  const((1, D_GRP)), const((1, D_GRP)), const((1, D_GRP)), const((1, D_GRP)),
                  const((D_GRP, D_GRP)),
                  pl.BlockSpec((tm, LANES), lambda b, i: (i, 0)),
                  pl.BlockSpec((tm, LANES), lambda b, i: (i, 0)),
                  const((tm, tm))],
        out_specs=[head_spec, head_spec] + [tok_spec(D_GRP)] * 4,
        out_shape=[head_shape, head_shape, tok(D_GRP, BF16),
                   tok(D_GRP, F32), tok(D_GRP, F32), tok(D_GRP, F32)],
        scratch_shapes=[pltpu.VMEM((1, LANES), F32)],
        compiler_params=_cparams(("arbitrary", "arbitrary")),
        name="inproj",
    )(x, gmix, w_main, w_f, b_f, gqa, gka, gqb, gkb, bd, cos_t, sin_t, tri)


def _fox_kernel(q_ref, k_ref, v_ref, o_ref):
    qi = pl.program_id(2)
    tq = q_ref.shape[2]
    lane = lax.broadcasted_iota(I32, (tq, LANES), 1)

    def step(start, width, carry, diag):
        v = v_ref[0, pl.ds(start, width), :]
        out = []
        for j in range(HEADS_PER_TILE):
            m, l, acc = carry[j]
            k = k_ref[0, j, pl.ds(start, width), :]
            s = lax.dot_general(q_ref[0, j], k, (((1,), (1,)), ((), ())), preferred_element_type=F32)
            if diag:
                row = lax.broadcasted_iota(I32, (tq, width), 0)
                col = lax.broadcasted_iota(I32, (tq, width), 1)
                s = jnp.where(row >= col, s, NEG)
            m_new = jnp.maximum(m, jnp.max(s, axis=-1, keepdims=True))
            alpha = jnp.exp2(m - m_new)
            p = jnp.exp2(s - m_new)
            l = alpha * l + jnp.sum(p, axis=-1, keepdims=True)
            acc = alpha * acc + jnp.dot(p.astype(BF16), v, preferred_element_type=F32)
            out.append((m_new, l, acc))
        return tuple(out)

    init = tuple((jnp.full((tq, 1), NEG, F32), jnp.zeros((tq, 1), F32), jnp.zeros((tq, LANES), F32))
                 for _ in range(HEADS_PER_TILE))
    per_wide = TK_WIDE // tq
    n_wide = qi // per_wide
    carry = lax.fori_loop(
        0, n_wide, lambda i, c: step(pl.multiple_of(i * TK_WIDE, TK_WIDE), TK_WIDE, c, False), init)
    for extra in range(per_wide - 1):
        carry = lax.cond(qi - n_wide * per_wide > extra,
                         lambda c, e=extra: step(pl.multiple_of((n_wide * per_wide + e) * tq, tq), tq, c, False),
                         lambda c: c, carry)
    carry = step(pl.multiple_of(qi * tq, tq), tq, carry, True)
    outs = [acc / l for (_, l, acc) in carry]
    o_ref[0] = jnp.where(lane < HEAD_DIM, outs[0], outs[1]).astype(o_ref.dtype)


def _fox(qa, ka, va):
    B, _, S, _ = qa.shape
    return pl.pallas_call(
        _fox_kernel,
        grid=(B, N_PAIRS, S // TQ),
        in_specs=[pl.BlockSpec((1, HEADS_PER_TILE, TQ, LANES), lambda b, hp, i: (b, hp, i, 0)),
                  pl.BlockSpec((1, HEADS_PER_TILE, S, LANES), lambda b, hp, i: (b, hp, 0, 0)),
                  pl.BlockSpec((1, S, LANES), lambda b, hp, i: (b, 0, hp))],
        out_specs=pl.BlockSpec((1, TQ, LANES), lambda b, hp, i: (b, i, hp)),
        out_shape=jax.ShapeDtypeStruct((B, S, D_GRP), F32),
        compiler_params=_cparams(("arbitrary", "arbitrary", "arbitrary")),
        name="fox",
    )(qa, ka, va)


def _dilated_kernel(q_ref, kp_ref, kc_ref, vp_ref, vc_ref, o_ref, qq, kk, vv, osc, lsc):
    u = pl.program_id(1)
    span = q_ref.shape[1]
    qq[...] = q_ref[0]
    kk[0:span, :] = kp_ref[0]
    kk[span:2 * span, :] = kc_ref[0]
    vv[0:span, :] = vp_ref[0]
    vv[span:2 * span, :] = vc_ref[0]

    lane = lax.broadcasted_iota(I32, (BLOCK, LANES), 1)
    ql = lax.broadcasted_iota(I32, (BLOCK, 2 * BLOCK), 0)
    kl = lax.broadcasted_iota(I32, (BLOCK, 2 * BLOCK), 1)
    dist = ql + BLOCK - kl
    band = (dist >= 0) & (dist <= BLOCK)
    in_prev = kl < BLOCK

    def attend(q_start, k_start, d, first):
        qs = qq[pl.ds(q_start, BLOCK, stride=d), :].astype(BF16)
        ks = kk[pl.ds(k_start, 2 * BLOCK, stride=d), :].astype(BF16)
        vs = vv[pl.ds(k_start, 2 * BLOCK, stride=d), :].astype(BF16)
        valid = band & jnp.logical_not(jnp.logical_and(first, in_prev))
        o_heads, lse_heads = [], []
        for j in range(HEADS_PER_TILE):
            qj = jnp.where(lane // HEAD_DIM == j, qs, jnp.zeros_like(qs))
            s = lax.dot_general(qj, ks, (((1,), (1,)), ((), ())), preferred_element_type=F32)
            s = jnp.where(valid, s, NEG)
            m = jnp.max(s, axis=-1, keepdims=True)
            p = jnp.exp(s - m)
            l = jnp.sum(p, axis=-1, keepdims=True)
            o_heads.append(jnp.dot((p / l).astype(BF16), vs, preferred_element_type=F32))
            lse_heads.append(m + jnp.log(l))
        o = jnp.where(lane < HEAD_DIM, o_heads[0], o_heads[1])
        lse = jnp.where(lane < HEAD_DIM, lse_heads[0], lse_heads[1])
        return o, lse

    for pidx, (window, d) in enumerate(DIL_PATTERNS):
        assert window // d == BLOCK
        unit = d * BLOCK
        n_units = span // unit

        def body(idx, _, pidx=pidx, d=d, unit=unit):
            w = idx // d
            r = idx - w * d
            q_start = w * unit + r
            k_start = span - unit + q_start
            o, lse = attend(q_start, k_start, d, jnp.logical_and(u == 0, w == 0))
            osc[pidx, pl.ds(q_start, BLOCK, stride=d), :] = o
            lsc[pidx, pl.ds(q_start, BLOCK, stride=d), :] = lse
            return 0

        lax.fori_loop(0, n_units * d, body, 0, unroll=DIL_UNROLL)

    mx = jnp.maximum(jnp.maximum(lsc[0], lsc[1]), lsc[2])
    num = jnp.zeros((span, LANES), F32)
    den = jnp.zeros((span, LANES), F32)
    for pidx in range(len(DIL_PATTERNS)):
        e = jnp.exp(lsc[pidx] - mx)
        num = num + e * osc[pidx]
        den = den + e
    o_ref[0] = (num / den).astype(o_ref.dtype)


def _dilated(qb, kb, vb):
    B, S, _ = qb.shape
    span = DIL_SPAN
    cur = pl.BlockSpec((1, span, LANES), lambda b, u, hp: (b, u, hp))
    prev = pl.BlockSpec((1, span, LANES), lambda b, u, hp: (b, jnp.maximum(u - 1, 0), hp))
    return pl.pallas_call(
        _dilated_kernel,
        grid=(B, S // span, N_PAIRS),
        in_specs=[cur, prev, cur, prev, cur],
        out_specs=cur,
        out_shape=jax.ShapeDtypeStruct((B, S, D_GRP), F32),
        scratch_shapes=[pltpu.VMEM((span, LANES), F32),
                        pltpu.VMEM((2 * span, LANES), F32), pltpu.VMEM((2 * span, LANES), F32),
                        pltpu.VMEM((len(DIL_PATTERNS), span, LANES), F32),
                        pltpu.VMEM((len(DIL_PATTERNS), span, LANES), F32)],
        compiler_params=_cparams(("arbitrary", "arbitrary", "arbitrary")),
        name="dilated",
    )(qb, kb, kb, vb, vb)


def _outproj_kernel(oa_ref, ob_ref, x_ref, gfox_ref, gdil_ref, wo_ref, gffn_ref, wr_ref, br_ref, upper_ref,
                    h_ref, hn_ref, eid_ref, gate_ref, rank_ref, cnt_ref, run_ref):
    @pl.when(pl.program_id(0) == 0)
    def _():
        run_ref[...] = jnp.zeros_like(run_ref)

    def norm(y, g):
        ms = jnp.mean(y * y, axis=-1, keepdims=True)
        return y * lax.rsqrt(ms + EPS) * g

    a = norm(oa_ref[...], gfox_ref[...]).astype(BF16)
    b = norm(ob_ref[...], gdil_ref[...]).astype(BF16)
    mix = (jnp.dot(a, wo_ref[0:D_GRP, :], preferred_element_type=F32)
           + jnp.dot(b, wo_ref[D_GRP:2 * D_GRP, :], preferred_element_type=F32))
    h = x_ref[...] + mix
    h_ref[...] = h
    hn = norm(h, gffn_ref[...])
    hn_ref[...] = hn

    z = lax.dot_general(wr_ref[...], hn, (((1,), (1,)), ((), ())), preferred_element_type=F32,
                        precision=lax.Precision.HIGHEST) + br_ref[...]
    tm = z.shape[1]
    best = z[0:1, :]
    g_sel = jnp.zeros((1, tm), I32)
    for g in range(1, N_GROUPS):
        better = z[g:g + 1, :] > best
        g_sel = jnp.where(better, g, g_sel)
        best = jnp.maximum(best, z[g:g + 1, :])
    den = jnp.zeros((1, tm), F32)
    for g in range(N_GROUPS):
        den = den + jnp.exp(z[g:g + 1, :] - best)
    pg_top = 1.0 / den

    ze = jnp.zeros((EXPERTS_PER_GROUP, tm), F32)
    for g in range(N_GROUPS):
        ze = jnp.where(g_sel == g, z[8 + g * EXPERTS_PER_GROUP:8 + (g + 1) * EXPERTS_PER_GROUP, :], ze)
    e_iota = lax.broadcasted_iota(I32, ze.shape, 0)
    v1 = jnp.max(ze, axis=0, keepdims=True)
    i1 = jnp.min(jnp.where(ze == v1, e_iota, EXPERTS_PER_GROUP), axis=0, keepdims=True)
    ze2 = jnp.where(e_iota == i1, -jnp.inf, ze)
    v2 = jnp.max(ze2, axis=0, keepdims=True)
    i2 = jnp.min(jnp.where(ze2 == v2, e_iota, EXPERTS_PER_GROUP), axis=0, keepdims=True)
    e2 = jnp.exp(v2 - v1)
    inv = 1.0 / (1.0 + e2)
    gate1 = inv * pg_top
    gate2 = e2 * inv * pg_top
    eid1 = g_sel * EXPERTS_PER_GROUP + i1
    eid2 = g_sel * EXPERTS_PER_GROUP + i2

    x_iota = lax.broadcasted_iota(I32, (N_EXPERTS, tm), 0)
    hot1 = x_iota == eid1
    hot2 = x_iota == eid2
    multi = jnp.logical_or(hot1, hot2)
    before = jnp.dot(multi.astype(BF16), upper_ref[...], preferred_element_type=F32)
    slot = before + run_ref[:, 0:1]
    rank1 = jnp.sum(jnp.where(hot1, slot, 0.0), axis=0, keepdims=True)
    rank2 = jnp.sum(jnp.where(hot2, slot, 0.0), axis=0, keepdims=True)
    run_ref[...] = run_ref[...] + jnp.sum(multi.astype(F32), axis=1, keepdims=True)

    eid_ref[...] = jnp.concatenate([eid1, eid2], axis=0)
    gate_ref[...] = jnp.concatenate([gate1, gate2], axis=0)
    rank_ref[...] = jnp.concatenate([rank1, rank2], axis=0).astype(I32)
    cnt_ref[...] = run_ref[...].astype(I32)


def _outproj(oa, ob, x2, gfox, gdil, w_out, gffn, w_r, b_r, upper):
    T, D = x2.shape
    tm = TM_OUT
    const = lambda shape: pl.BlockSpec(shape, lambda i: (0,) * len(shape))
    tok = lambda w: pl.BlockSpec((tm, w), lambda i: (i, 0))
    lanes2 = pl.BlockSpec((2, tm), lambda i: (0, i))
    return pl.pallas_call(
        _outproj_kernel,
        grid=(T // tm,),
        in_specs=[tok(D_GRP), tok(D_GRP), tok(D), const((1, D_GRP)), const((1, D_GRP)), const((D, D)),
                  const((1, D)), const((ROUTER_ROWS, D)), const((ROUTER_ROWS, 1)), const((tm, tm))],
        out_specs=[tok(D), tok(D), lanes2, lanes2, lanes2, const((N_EXPERTS, LANES))],
        out_shape=[jax.ShapeDtypeStruct((T, D), F32), jax.ShapeDtypeStruct((T, D), F32),
                   jax.ShapeDtypeStruct((2, T), I32), jax.ShapeDtypeStruct((2, T), F32),
                   jax.ShapeDtypeStruct((2, T), I32), jax.ShapeDtypeStruct((N_EXPERTS, LANES), I32)],
        scratch_shapes=[pltpu.VMEM((N_EXPERTS, LANES), F32)],
        compiler_params=_cparams(("arbitrary",)),
        name="outproj",
    )(oa, ob, x2, gfox, gdil, w_out, gffn, w_r, b_r, upper)


def _row_copy(src_ref, src_row, dst_ref, dst_row, sem):
    return pltpu.make_async_copy(src_ref.at[pl.ds(src_row, 1), :], dst_ref.at[pl.ds(dst_row, 1), :], sem)


def _scatter_kernel(pos_ref, hn_ref, init_ref, xs_ref, sem):
    del init_ref
    tm = hn_ref.shape[0]

    def start(r, _):
        for k in range(2):
            _row_copy(hn_ref, r, xs_ref, pos_ref[k, r], sem).start()
        return 0

    def wait(r, _):
        for k in range(2):
            _row_copy(hn_ref, r, xs_ref, pos_ref[k, r], sem).wait()
        return 0

    lax.fori_loop(0, tm, start, 0)
    lax.fori_loop(0, tm, wait, 0)


def _scatter(pos, hn, xs_init):
    T, D = hn.shape
    tm = TM_ROWS
    return pl.pallas_call(
        _scatter_kernel,
        grid=(T // tm,),
        in_specs=[pl.BlockSpec((2, tm), lambda i: (0, i), memory_space=pltpu.SMEM),
                  pl.BlockSpec((tm, D), lambda i: (i, 0)),
                  pl.BlockSpec(memory_space=pl.ANY)],
        out_specs=pl.BlockSpec(memory_space=pl.ANY),
        out_shape=jax.ShapeDtypeStruct(xs_init.shape, xs_init.dtype),
        input_output_aliases={2: 0},
        scratch_shapes=[pltpu.SemaphoreType.DMA(())],
        compiler_params=_cparams(("arbitrary",)),
        name="scatter_rows",
    )(pos, hn, xs_init)


def _experts_kernel(te_ref, tv_ref, xs_ref, wg_ref, wu_ref, wd_ref, y_ref):
    i = pl.program_id(0)

    @pl.when(tv_ref[i] != 0)
    def _():
        x = xs_ref[...].astype(BF16)
        g = jnp.dot(x, wg_ref[0].astype(BF16), preferred_element_type=F32)
        up = jnp.dot(x, wu_ref[0].astype(BF16), preferred_element_type=F32)
        hmid = (g * jax.nn.sigmoid(g) * up).astype(BF16)
        y_ref[...] = jnp.dot(hmid, wd_ref[0].astype(BF16), preferred_element_type=F32)

    @pl.when(tv_ref[i] == 0)
    def _():
        y_ref[...] = jnp.zeros_like(y_ref)


def _experts(tile_expert, tile_valid, xs, w_gate, w_up, w_down):
    n_rows, D = xs.shape
    n_tiles = n_rows // TILE_E
    grid_spec = pltpu.PrefetchScalarGridSpec(
        num_scalar_prefetch=2,
        grid=(n_tiles,),
        in_specs=[pl.BlockSpec((TILE_E, D), lambda i, te, tv: (i, 0)),
                  pl.BlockSpec((1, D, D_EXPERT), lambda i, te, tv: (te[i], 0, 0)),
                  pl.BlockSpec((1, D, D_EXPERT), lambda i, te, tv: (te[i], 0, 0)),
                  pl.BlockSpec((1, D_EXPERT, D), lambda i, te, tv: (te[i], 0, 0))],
        out_specs=pl.BlockSpec((TILE_E, D), lambda i, te, tv: (i, 0)),
    )
    return pl.pallas_call(
        _experts_kernel,
        grid_spec=grid_spec,
        out_shape=jax.ShapeDtypeStruct((n_rows, D), F32),
        compiler_params=_cparams(("arbitrary",)),
        name="experts",
    )(tile_expert, tile_valid, xs, w_gate, w_up, w_down)


def _combine_kernel(pos_ref, h_ref, gate_ref, y_ref, o_ref, ybuf, sem):
    tm = h_ref.shape[0]

    def start(r, _):
        for k in range(2):
            _row_copy(y_ref, pos_ref[k, r], ybuf.at[k], r, sem).start()
        return 0

    def wait(r, _):
        for k in range(2):
            _row_copy(y_ref, pos_ref[k, r], ybuf.at[k], r, sem).wait()
        return 0

    lax.fori_loop(0, tm, start, 0)
    lax.fori_loop(0, tm, wait, 0)
    g = gate_ref[...]
    o_ref[...] = h_ref[...] + g[:, 0:1] * ybuf[0] + g[:, 1:2] * ybuf[1]


def _combine(pos, h, gate_t, y):
    T, D = h.shape
    tm = TM_ROWS
    return pl.pallas_call(
        _combine_kernel,
        grid=(T // tm,),
        in_specs=[pl.BlockSpec((2, tm), lambda i: (0, i), memory_space=pltpu.SMEM),
                  pl.BlockSpec((tm, D), lambda i: (i, 0)),
                  pl.BlockSpec((tm, 2), lambda i: (i, 0)),
                  pl.BlockSpec(memory_space=pl.ANY)],
        out_specs=pl.BlockSpec((tm, D), lambda i: (i, 0)),
        out_shape=jax.ShapeDtypeStruct((T, D), F32),
        scratch_shapes=[pltpu.VMEM((2, tm, D), F32), pltpu.SemaphoreType.DMA(())],
        compiler_params=_cparams(("arbitrary",)),
        name="combine",
    )(pos, h, gate_t, y)


def _rope_tables(S):
    inv_freq = 1.0 / (ROPE_THETA ** (jnp.arange(0, HEAD_DIM, 2, dtype=F32) / HEAD_DIM))
    ang = jnp.arange(S, dtype=F32)[:, None] * inv_freq[None, :]
    cos, sin = jnp.cos(ang), jnp.sin(ang)
    cos_t = jnp.tile(cos, (1, 2 * HEADS_PER_TILE))
    sin_t = jnp.tile(jnp.concatenate([-sin, sin], axis=1), (1, HEADS_PER_TILE))
    return cos_t, sin_t


def _layer(x, norm_mix, w_in, b_forget, q_norm_fox, k_norm_fox, q_norm_dil, k_norm_dil,
           out_norm_fox, out_norm_dil, w_out, norm_ffn, w_router_group, b_router_group,
           w_router_expert, b_router_expert, w_gate, w_up, w_down):
    B, S, D = x.shape
    T = B * S
    n_main = 6 * D_GRP

    w_main = w_in[:, :n_main].astype(BF16)
    w_f = jnp.pad(w_in[:, n_main:], ((0, 0), (0, LANES - N_HEADS))).astype(BF16)
    b_f = jnp.pad(b_forget, (0, LANES - N_HEADS))[None, :]
    per_head = lambda g: jnp.tile(g, N_HEADS)[None, :]
    bd = jnp.kron(jnp.eye(N_HEADS, dtype=F32), jnp.ones((HEAD_DIM, HEAD_DIM), F32)).astype(BF16)
    cos_t, sin_t = _rope_tables(S)
    tri = jnp.tril(jnp.ones((TM_IN, TM_IN), F32))
    upper = jnp.triu(jnp.ones((TM_OUT, TM_OUT), F32), k=1).astype(BF16)
    w_r = jnp.concatenate([
        jnp.pad(w_router_group.T, ((0, 8 - N_GROUPS), (0, 0))),
        w_router_expert.transpose(0, 2, 1).reshape(N_EXPERTS, D)], axis=0)
    b_r = jnp.concatenate([jnp.pad(b_router_group, (0, 8 - N_GROUPS)), b_router_expert.reshape(-1)])[:, None]

    qa, ka, va, qb, kb, vb = _inproj(
        x, norm_mix[None, :], w_main, w_f, b_f, per_head(q_norm_fox), per_head(k_norm_fox),
        per_head(q_norm_dil), per_head(k_norm_dil), bd, cos_t, sin_t, tri)
    oa = _fox(qa, ka, va)
    ob = _dilated(qb, kb, vb)

    h, hn, eid, gate, rank, cnt = _outproj(
        oa.reshape(T, D_GRP), ob.reshape(T, D_GRP), x.reshape(T, D), out_norm_fox[None, :],
        out_norm_dil[None, :], w_out.astype(BF16), norm_ffn[None, :], w_r, b_r, upper)

    counts = cnt[:, 0]
    padded = ((counts + TILE_E - 1) // TILE_E) * TILE_E
    ends = jnp.cumsum(padded)
    starts = ends - padded
    is_expert = eid[:, :, None] == jnp.arange(N_EXPERTS, dtype=I32)
    pos = jnp.sum(jnp.where(is_expert, starts, 0), axis=-1) + rank
    n_tiles = (2 * T) // TILE_E + N_EXPERTS
    tile_start = jnp.arange(n_tiles, dtype=I32) * TILE_E
    tile_expert = jnp.minimum(jnp.sum((ends[None, :] <= tile_start[:, None]).astype(I32), axis=1), N_EXPERTS - 1)
    tile_valid = (tile_start < ends[-1]).astype(I32)

    xs = _scatter(pos, hn, jnp.zeros((n_tiles * TILE_E, D), F32))
    y = _experts(tile_expert, tile_valid, xs, w_gate, w_up, w_down)
    out = _combine(pos, h, gate.T, y)
    return out.reshape(B, S, D)


def kernel(x, norm_mix, w_in, b_forget, q_norm_fox, k_norm_fox, q_norm_dil, k_norm_dil, out_norm_fox,
           out_norm_dil, w_out, norm_ffn, w_router_group, b_router_group, w_router_expert,
           b_router_expert, w_gate, w_up, w_down):
    h = x
    for l in range(norm_mix.shape[0]):
        h = _layer(h, norm_mix[l], w_in[l], b_forget[l], q_norm_fox[l], k_norm_fox[l], q_norm_dil[l],
                   k_norm_dil[l], out_norm_fox[l], out_norm_dil[l], w_out[l], norm_ffn[l],
                   w_router_group[l], b_router_group[l], w_router_expert[l], b_router_expert[l],
                   w_gate[l], w_up[l], w_down[l])
    return h
```

```python
import functools
import math

import jax
import jax.numpy as jnp
from jax import lax
from jax.experimental import pallas as pl
from jax.experimental.pallas import tpu as pltpu

F32 = jnp.float32
BF16 = jnp.bfloat16
I32 = jnp.int32

D_MODEL = 1024
HEAD_DIM = 64
N_HEADS = 8
D_GRP = N_HEADS * HEAD_DIM
LANES = 128
HEADS_PER_TILE = LANES // HEAD_DIM
N_PAIRS = D_GRP // LANES
DIL_PATTERNS = ((128, 1), (512, 4), (2048, 16))
BLOCK = 128
ROPE_THETA = 10000.0
N_GROUPS = 4
EXPERTS_PER_GROUP = 8
N_EXPERTS = N_GROUPS * EXPERTS_PER_GROUP
D_EXPERT = 512
EPS = 1e-6
NEG = -1e30
LOG2E = 1.4426950408889634

TM_IN = 512
TQ = 512
TK_WIDE = 1024
DIL_SPAN = 2048
DIL_UNROLL = 4
TM_OUT = 512
TM_ROWS = 256
TILE_E = 256
ROUTER_ROWS = 8 + N_EXPERTS
ROW_CHUNKS = D_MODEL // LANES
DMA_UNROLL = 8
VMEM_LIMIT = 56 * 1024 * 1024


def _cparams(sem):
    return pltpu.CompilerParams(dimension_semantics=sem, vmem_limit_bytes=VMEM_LIMIT)


def _inproj_kernel(x_ref, gmix_ref, w_ref, wf_ref, bf_ref, gqa_ref, gka_ref, gqb_ref, gkb_ref,
                   bd_ref, cos_ref, sin_ref, tri_ref,
                   qa_ref, ka_ref, va_ref, qb_ref, kb_ref, vb_ref, carry_ref):
    @pl.when(pl.program_id(1) == 0)
    def _():
        carry_ref[...] = jnp.zeros_like(carry_ref)

    x = x_ref[0]
    ms = jnp.mean(x * x, axis=-1, keepdims=True)
    xn = (x * lax.rsqrt(ms + EPS) * gmix_ref[...]).astype(BF16)

    def seg(j):
        return jnp.dot(xn, w_ref[:, j * D_GRP:(j + 1) * D_GRP], preferred_element_type=F32)

    def head_norm(y, g_ref, scale):
        ss = jnp.dot((y * y).astype(BF16), bd_ref[...], preferred_element_type=F32) * (1.0 / HEAD_DIM)
        return y * lax.rsqrt(ss + EPS) * (g_ref[...] * scale)

    cos = cos_ref[...]
    sin = sin_ref[...]
    lane = lax.broadcasted_iota(I32, (x.shape[0], LANES), 1)
    first_half = (lane % HEAD_DIM) < (HEAD_DIM // 2)

    def rope(y):
        outs = []
        for j in range(N_PAIRS):
            ys = y[:, j * LANES:(j + 1) * LANES]
            partner = jnp.where(first_half, pltpu.roll(ys, LANES - HEAD_DIM // 2, 1),
                                pltpu.roll(ys, HEAD_DIM // 2, 1))
            outs.append(ys * cos + partner * sin)
        return jnp.concatenate(outs, axis=1)

    scale = 1.0 / math.sqrt(HEAD_DIM)
    va_ref[0] = seg(2).astype(va_ref.dtype)
    qb_ref[0] = rope(head_norm(seg(3), gqb_ref, scale)).astype(qb_ref.dtype)
    kb_ref[0] = rope(head_norm(seg(4), gkb_ref, 1.0)).astype(kb_ref.dtype)
    vb_ref[0] = seg(5).astype(vb_ref.dtype)

    fa = jnp.dot(xn, wf_ref[...], preferred_element_type=F32) + bf_ref[...]
    logf = jnp.minimum(fa, 0.0) - jnp.log1p(jnp.exp(-jnp.abs(fa)))
    c = jnp.dot(tri_ref[...], logf, preferred_element_type=F32,
                precision=lax.Precision.HIGHEST) + carry_ref[...]
    carry_ref[...] = c[c.shape[0] - 1:, :]

    qa = head_norm(seg(0), gqa_ref, scale * LOG2E)
    ka = head_norm(seg(1), gka_ref, 1.0)
    c2 = c * LOG2E
    ones = jnp.where((lane >= HEAD_DIM + 3) & (lane < HEAD_DIM + 6), 1.0, 0.0)
    for h in range(N_HEADS):
        cb = jnp.broadcast_to(c2[:, h:h + 1], (x.shape[0], LANES))
        hi = cb.astype(BF16).astype(F32)
        mid = (cb - hi).astype(BF16).astype(F32)
        lo = cb - hi - mid
        pieces = jnp.where(lane == HEAD_DIM, hi, jnp.where(lane == HEAD_DIM + 1, mid,
                           jnp.where(lane == HEAD_DIM + 2, lo, 0.0)))
        q_extra = pieces + ones
        k_extra = jnp.where((lane >= HEAD_DIM) & (lane < HEAD_DIM + 3), 1.0, 0.0) \
            - pltpu.roll(pieces, 3, 1)
        j, odd = divmod(h, HEADS_PER_TILE)
        qp = qa[:, j * LANES:(j + 1) * LANES]
        kp = ka[:, j * LANES:(j + 1) * LANES]
        if odd:
            qp = pltpu.roll(qp, HEAD_DIM, 1)
            kp = pltpu.roll(kp, HEAD_DIM, 1)
        qa_ref[0, h] = jnp.where(lane < HEAD_DIM, qp, q_extra).astype(qa_ref.dtype)
        ka_ref[0, h] = jnp.where(lane < HEAD_DIM, kp, k_extra).astype(ka_ref.dtype)


def _inproj(x, gmix, w_main, w_f, b_f, gqa, gka, gqb, gkb, bd, cos_t, sin_t, tri):
    B, S, D = x.shape
    tm = TM_IN
    const = lambda shape: pl.BlockSpec(shape, lambda b, i: (0,) * len(shape))
    tok = lambda w, dt: jax.ShapeDtypeStruct((B, S, w), dt)
    tok_spec = lambda w: pl.BlockSpec((1, tm, w), lambda b, i: (b, i, 0))
    head_spec = pl.BlockSpec((1, N_HEADS, tm, LANES), lambda b, i: (b, 0, i, 0))
    head_shape = jax.ShapeDtypeStruct((B, N_HEADS, S, LANES), BF16)
    return pl.pallas_call(
        _inproj_kernel,
        grid=(B, S // tm),
        in_specs=[tok_spec(D), const((1, D)), const(w_main.shape), const(w_f.shape), const((1, LANES)),
                  const((1, D_GRP)), const((1, D_GRP)), const((1, D_GRP)), const((1, D_GRP)),
                  const((D_GRP, D_GRP)),
                  pl.BlockSpec((tm, LANES), lambda b, i: (i, 0)),
                  pl.BlockSpec((tm, LANES), lambda b, i: (i, 0)),
                  const((tm, tm))],
        out_specs=[head_spec, head_spec] + [tok_spec(D_GRP)] * 4,
        out_shape=[head_shape, head_shape, tok(D_GRP, BF16),
                   tok(D_GRP, F32), tok(D_GRP, F32), tok(D_GRP, F32)],
        scratch_shapes=[pltpu.VMEM((1, LANES), F32)],
        compiler_params=_cparams(("arbitrary", "arbitrary")),
        name="inproj",
    )(x, gmix, w_main, w_f, b_f, gqa, gka, gqb, gkb, bd, cos_t, sin_t, tri)


def _fox_kernel(q_ref, k_ref, v_ref, o_ref):
    qi = pl.program_id(2)
    tq = q_ref.shape[2]
    lane = lax.broadcasted_iota(I32, (tq, LANES), 1)

    def step(start, width, carry, diag):
        v = v_ref[0, pl.ds(start, width), :]
        out = []
        for j in range(HEADS_PER_TILE):
            m, l, acc = carry[j]
            k = k_ref[0, j, pl.ds(start, width), :]
            s = lax.dot_general(q_ref[0, j], k, (((1,), (1,)), ((), ())), preferred_element_type=F32)
            if diag:
                row = lax.broadcasted_iota(I32, (tq, width), 0)
                col = lax.broadcasted_iota(I32, (tq, width), 1)
                s = jnp.where(row >= col, s, NEG)
            m_new = jnp.maximum(m, jnp.max(s, axis=-1, keepdims=True))
            alpha = jnp.exp2(m - m_new)
            p = jnp.exp2(s - m_new)
            l = alpha * l + jnp.sum(p, axis=-1, keepdims=True)
            acc = alpha * acc + jnp.dot(p.astype(BF16), v, preferred_element_type=F32)
            out.append((m_new, l, acc))
        return tuple(out)

    init = tuple((jnp.full((tq, 1), NEG, F32), jnp.zeros((tq, 1), F32), jnp.zeros((tq, LANES), F32))
                 for _ in range(HEADS_PER_TILE))
    per_wide = TK_WIDE // tq
    n_wide = qi // per_wide
    carry = lax.fori_loop(
        0, n_wide, lambda i, c: step(pl.multiple_of(i * TK_WIDE, TK_WIDE), TK_WIDE, c, False), init)
    for extra in range(per_wide - 1):
        carry = lax.cond(qi - n_wide * per_wide > extra,
                         lambda c, e=extra: step(pl.multiple_of((n_wide * per_wide + e) * tq, tq), tq, c, False),
                         lambda c: c, carry)
    carry = step(pl.multiple_of(qi * tq, tq), tq, carry, True)
    outs = [acc / l for (_, l, acc) in carry]
    o_ref[0] = jnp.where(lane < HEAD_DIM, outs[0], outs[1]).astype(o_ref.dtype)


def _fox(qa, ka, va):
    B, _, S, _ = qa.shape
    return pl.pallas_call(
        _fox_kernel,
        grid=(B, N_PAIRS, S // TQ),
        in_specs=[pl.BlockSpec((1, HEADS_PER_TILE, TQ, LANES), lambda b, hp, i: (b, hp, i, 0)),
                  pl.BlockSpec((1, HEADS_PER_TILE, S, LANES), lambda b, hp, i: (b, hp, 0, 0)),
                  pl.BlockSpec((1, S, LANES), lambda b, hp, i: (b, 0, hp))],
        out_specs=pl.BlockSpec((1, TQ, LANES), lambda b, hp, i: (b, i, hp)),
        out_shape=jax.ShapeDtypeStruct((B, S, D_GRP), F32),
        compiler_params=_cparams(("arbitrary", "arbitrary", "arbitrary")),
        name="fox",
    )(qa, ka, va)


def _dilated_kernel(q_ref, kp_ref, kc_ref, vp_ref, vc_ref, o_ref, qq, kk, vv, osc, lsc):
    u = pl.program_id(1)
    span = q_ref.shape[1]
    qq[...] = q_ref[0]
    kk[0:span, :] = kp_ref[0]
    kk[span:2 * span, :] = kc_ref[0]
    vv[0:span, :] = vp_ref[0]
    vv[span:2 * span, :] = vc_ref[0]

    lane = lax.broadcasted_iota(I32, (BLOCK, LANES), 1)
    ql = lax.broadcasted_iota(I32, (BLOCK, 2 * BLOCK), 0)
    kl = lax.broadcasted_iota(I32, (BLOCK, 2 * BLOCK), 1)
    dist = ql + BLOCK - kl
    band = (dist >= 0) & (dist <= BLOCK)
    in_prev = kl < BLOCK

    def attend(q_start, k_start, d, first):
        qs = qq[pl.ds(q_start, BLOCK, stride=d), :].astype(BF16)
        ks = kk[pl.ds(k_start, 2 * BLOCK, stride=d), :].astype(BF16)
        vs = vv[pl.ds(k_start, 2 * BLOCK, stride=d), :].astype(BF16)
        valid = band & jnp.logical_not(jnp.logical_and(first, in_prev))
        o_heads, lse_heads = [], []
        for j in range(HEADS_PER_TILE):
            qj = jnp.where(lane // HEAD_DIM == j, qs, jnp.zeros_like(qs))
            s = lax.dot_general(qj, ks, (((1,), (1,)), ((), ())), preferred_element_type=F32)
            s = jnp.where(valid, s, NEG)
            m = jnp.max(s, axis=-1, keepdims=True)
            p = jnp.exp(s - m)
            l = jnp.sum(p, axis=-1, keepdims=True)
            o_heads.append(jnp.dot((p / l).astype(BF16), vs, preferred_element_type=F32))
            lse_heads.append(m + jnp.log(l))
        o = jnp.where(lane < HEAD_DIM, o_heads[0], o_heads[1])
        lse = jnp.where(lane < HEAD_DIM, lse_heads[0], lse_heads[1])
        return o, lse

    for pidx, (window, d) in enumerate(DIL_PATTERNS):
        assert window // d == BLOCK
        unit = d * BLOCK
        n_units = span // unit

        def body(idx, _, pidx=pidx, d=d, unit=unit):
            w = idx // d
            r = idx - w * d
            q_start = w * unit + r
            k_start = span - unit + q_start
            o, lse = attend(q_start, k_start, d, jnp.logical_and(u == 0, w == 0))
            osc[pidx, pl.ds(q_start, BLOCK, stride=d), :] = o
            lsc[pidx, pl.ds(q_start, BLOCK, stride=d), :] = lse
            return 0

        lax.fori_loop(0, n_units * d, body, 0, unroll=DIL_UNROLL)

    mx = jnp.maximum(jnp.maximum(lsc[0], lsc[1]), lsc[2])
    num = jnp.zeros((span, LANES), F32)
    den = jnp.zeros((span, LANES), F32)
    for pidx in range(len(DIL_PATTERNS)):
        e = jnp.exp(lsc[pidx] - mx)
        num = num + e * osc[pidx]
        den = den + e
    o_ref[0] = (num / den).astype(o_ref.dtype)


def _dilated(qb, kb, vb):
    B, S, _ = qb.shape
    span = DIL_SPAN
    cur = pl.BlockSpec((1, span, LANES), lambda b, u, hp: (b, u, hp))
    prev = pl.BlockSpec((1, span, LANES), lambda b, u, hp: (b, jnp.maximum(u - 1, 0), hp))
    return pl.pallas_call(
        _dilated_kernel,
        grid=(B, S // span, N_PAIRS),
        in_specs=[cur, prev, cur, prev, cur],
        out_specs=cur,
        out_shape=jax.ShapeDtypeStruct((B, S, D_GRP), F32),
        scratch_shapes=[pltpu.VMEM((span, LANES), F32),
                        pltpu.VMEM((2 * span, LANES), F32), pltpu.VMEM((2 * span, LANES), F32),
                        pltpu.VMEM((len(DIL_PATTERNS), span, LANES), F32),
                        pltpu.VMEM((len(DIL_PATTERNS), span, LANES), F32)],
        compiler_params=_cparams(("arbitrary", "arbitrary", "arbitrary")),
        name="dilated",
    )(qb, kb, kb, vb, vb)


def _store_row_tiles(ref, x):
    n = x.shape[0]
    for c in range(ROW_CHUNKS):
        ref[pl.ds(c, n, stride=ROW_CHUNKS), :] = x[:, c * LANES:(c + 1) * LANES]


def _load_row_tiles(ref, n):
    return jnp.concatenate([ref[pl.ds(c, n, stride=ROW_CHUNKS), :] for c in range(ROW_CHUNKS)], axis=1)


def _row_tile_copy(src_ref, src_row, dst_ref, dst_row, sem):
    src = src_ref.at[pl.ds(pl.multiple_of(src_row * ROW_CHUNKS, ROW_CHUNKS), ROW_CHUNKS), :]
    dst = dst_ref.at[pl.ds(pl.multiple_of(dst_row * ROW_CHUNKS, ROW_CHUNKS), ROW_CHUNKS), :]
    return pltpu.make_async_copy(src, dst, sem)


def _outproj_kernel(oa_ref, ob_ref, x_ref, gfox_ref, gdil_ref, wo_ref, gffn_ref, wr_ref, br_ref, upper_ref,
                    h_ref, hn_ref, eid_ref, gate_ref, rank_ref, cnt_ref, run_ref):
    @pl.when(pl.program_id(0) == 0)
    def _():
        run_ref[...] = jnp.zeros_like(run_ref)

    def norm(y, g):
        ms = jnp.mean(y * y, axis=-1, keepdims=True)
        return y * lax.rsqrt(ms + EPS) * g

    a = norm(oa_ref[...], gfox_ref[...]).astype(BF16)
    b = norm(ob_ref[...], gdil_ref[...]).astype(BF16)
    mix = (jnp.dot(a, wo_ref[0:D_GRP, :], preferred_element_type=F32)
           + jnp.dot(b, wo_ref[D_GRP:2 * D_GRP, :], preferred_element_type=F32))
    h = x_ref[...] + mix
    h_ref[...] = h
    hn = norm(h, gffn_ref[...])
    _store_row_tiles(hn_ref, hn)

    z = lax.dot_general(wr_ref[...], hn, (((1,), (1,)), ((), ())), preferred_element_type=F32,
                        precision=lax.Precision.HIGHEST) + br_ref[...]
    tm = z.shape[1]
    best = z[0:1, :]
    g_sel = jnp.zeros((1, tm), I32)
    for g in range(1, N_GROUPS):
        better = z[g:g + 1, :] > best
        g_sel = jnp.where(better, g, g_sel)
        best = jnp.maximum(best, z[g:g + 1, :])
    den = jnp.zeros((1, tm), F32)
    for g in range(N_GROUPS):
        den = den + jnp.exp(z[g:g + 1, :] - best)
    pg_top = 1.0 / den

    ze = jnp.zeros((EXPERTS_PER_GROUP, tm), F32)
    for g in range(N_GROUPS):
        ze = jnp.where(g_sel == g, z[8 + g * EXPERTS_PER_GROUP:8 + (g + 1) * EXPERTS_PER_GROUP, :], ze)
    e_iota = lax.broadcasted_iota(I32, ze.shape, 0)
    v1 = jnp.max(ze, axis=0, keepdims=True)
    i1 = jnp.min(jnp.where(ze == v1, e_iota, EXPERTS_PER_GROUP), axis=0, keepdims=True)
    ze2 = jnp.where(e_iota == i1, -jnp.inf, ze)
    v2 = jnp.max(ze2, axis=0, keepdims=True)
    i2 = jnp.min(jnp.where(ze2 == v2, e_iota, EXPERTS_PER_GROUP), axis=0, keepdims=True)
    e2 = jnp.exp(v2 - v1)
    inv = 1.0 / (1.0 + e2)
    gate1 = inv * pg_top
    gate2 = e2 * inv * pg_top
    eid1 = g_sel * EXPERTS_PER_GROUP + i1
    eid2 = g_sel * EXPERTS_PER_GROUP + i2

    x_iota = lax.broadcasted_iota(I32, (N_EXPERTS, tm), 0)
    hot1 = x_iota == eid1
    hot2 = x_iota == eid2
    multi = jnp.logical_or(hot1, hot2)
    before = jnp.dot(multi.astype(BF16), upper_ref[...], preferred_element_type=F32)
    slot = before + run_ref[:, 0:1]
    rank1 = jnp.sum(jnp.where(hot1, slot, 0.0), axis=0, keepdims=True)
    rank2 = jnp.sum(jnp.where(hot2, slot, 0.0), axis=0, keepdims=True)
    run_ref[...] = run_ref[...] + jnp.sum(multi.astype(F32), axis=1, keepdims=True)

    eid_ref[...] = jnp.concatenate([eid1, eid2], axis=0)
    gate_ref[...] = jnp.concatenate([gate1, gate2], axis=0)
    rank_ref[...] = jnp.concatenate([rank1, rank2], axis=0).astype(I32)
    cnt_ref[...] = run_ref[...].astype(I32)


def _outproj(oa, ob, x2, gfox, gdil, w_out, gffn, w_r, b_r, upper):
    T, D = x2.shape
    tm = TM_OUT
    const = lambda shape: pl.BlockSpec(shape, lambda i: (0,) * len(shape))
    tok = lambda w: pl.BlockSpec((tm, w), lambda i: (i, 0))
    lanes2 = pl.BlockSpec((2, tm), lambda i: (0, i))
    return pl.pallas_call(
        _outproj_kernel,
        grid=(T // tm,),
        in_specs=[tok(D_GRP), tok(D_GRP), tok(D), const((1, D_GRP)), const((1, D_GRP)), const((D, D)),
                  const((1, D)), const((ROUTER_ROWS, D)), const((ROUTER_ROWS, 1)), const((tm, tm))],
        out_specs=[tok(D), pl.BlockSpec((tm * ROW_CHUNKS, LANES), lambda i: (i, 0)),
                   lanes2, lanes2, lanes2, const((N_EXPERTS, LANES))],
        out_shape=[jax.ShapeDtypeStruct((T, D), F32), jax.ShapeDtypeStruct((T * ROW_CHUNKS, LANES), F32),
                   jax.ShapeDtypeStruct((2, T), I32), jax.ShapeDtypeStruct((2, T), F32),
                   jax.ShapeDtypeStruct((2, T), I32), jax.ShapeDtypeStruct((N_EXPERTS, LANES), I32)],
        scratch_shapes=[pltpu.VMEM((N_EXPERTS, LANES), F32)],
        compiler_params=_cparams(("arbitrary",)),
        name="outproj",
    )(oa, ob, x2, gfox, gdil, w_out, gffn, w_r, b_r, upper)


def _scatter_kernel(pos_ref, hn_ref, init_ref, xs_ref, sem):
    del init_ref
    tm = hn_ref.shape[0] // ROW_CHUNKS

    def start(r, _):
        for k in range(2):
            _row_tile_copy(hn_ref, r, xs_ref, pos_ref[k, r], sem).start()
        return 0

    lax.fori_loop(0, tm, start, 0, unroll=DMA_UNROLL)
    for k in range(2):
        pltpu.make_async_copy(hn_ref, xs_ref.at[pl.ds(0, tm * ROW_CHUNKS), :], sem).wait()


def _scatter(pos, hn, xs_init):
    T = hn.shape[0] // ROW_CHUNKS
    tm = TM_ROWS
    return pl.pallas_call(
        _scatter_kernel,
        grid=(T // tm,),
        in_specs=[pl.BlockSpec((2, tm), lambda i: (0, i), memory_space=pltpu.SMEM),
                  pl.BlockSpec((tm * ROW_CHUNKS, LANES), lambda i: (i, 0)),
                  pl.BlockSpec(memory_space=pl.ANY)],
        out_specs=pl.BlockSpec(memory_space=pl.ANY),
        out_shape=jax.ShapeDtypeStruct(xs_init.shape, xs_init.dtype),
        input_output_aliases={2: 0},
        scratch_shapes=[pltpu.SemaphoreType.DMA(())],
        compiler_params=_cparams(("arbitrary",)),
        name="scatter_rows",
    )(pos, hn, xs_init)


def _experts_kernel(te_ref, tv_ref, xs_ref, wg_ref, wu_ref, wd_ref, y_ref, wg_bf, wu_bf, wd_bf):
    i = pl.program_id(0)
    new_expert = jnp.logical_or(i == 0, te_ref[i] != te_ref[jnp.maximum(i - 1, 0)])

    @pl.when(jnp.logical_and(tv_ref[i] != 0, new_expert))
    def _():
        wg_bf[...] = wg_ref[0].astype(BF16)
        wu_bf[...] = wu_ref[0].astype(BF16)
        wd_bf[...] = wd_ref[0].astype(BF16)

    @pl.when(tv_ref[i] != 0)
    def _():
        x = _load_row_tiles(xs_ref, TILE_E).astype(BF16)
        g = jnp.dot(x, wg_bf[...], preferred_element_type=F32)
        up = jnp.dot(x, wu_bf[...], preferred_element_type=F32)
        hmid = (g * jax.nn.sigmoid(g) * up).astype(BF16)
        _store_row_tiles(y_ref, jnp.dot(hmid, wd_bf[...], preferred_element_type=F32))

    @pl.when(tv_ref[i] == 0)
    def _():
        y_ref[...] = jnp.zeros_like(y_ref)


def _experts(tile_expert, tile_valid, xs, w_gate, w_up, w_down):
    n_tiles = xs.shape[0] // (TILE_E * ROW_CHUNKS)
    D = D_MODEL
    rows_spec = pl.BlockSpec((TILE_E * ROW_CHUNKS, LANES), lambda i, te, tv: (i, 0))
    grid_spec = pltpu.PrefetchScalarGridSpec(
        num_scalar_prefetch=2,
        grid=(n_tiles,),
        in_specs=[rows_spec,
                  pl.BlockSpec((1, D, D_EXPERT), lambda i, te, tv: (te[i], 0, 0)),
                  pl.BlockSpec((1, D, D_EXPERT), lambda i, te, tv: (te[i], 0, 0)),
                  pl.BlockSpec((1, D_EXPERT, D), lambda i, te, tv: (te[i], 0, 0))],
        out_specs=rows_spec,
        scratch_shapes=[pltpu.VMEM((D, D_EXPERT), BF16), pltpu.VMEM((D, D_EXPERT), BF16),
                        pltpu.VMEM((D_EXPERT, D), BF16)],
    )
    return pl.pallas_call(
        _experts_kernel,
        grid_spec=grid_spec,
        out_shape=jax.ShapeDtypeStruct(xs.shape, F32),
        compiler_params=_cparams(("arbitrary",)),
        name="experts",
    )(tile_expert, tile_valid, xs, w_gate, w_up, w_down)


def _combine_kernel(pos_ref, h_ref, gate_ref, y_ref, o_ref, ybuf, sem):
    tm = h_ref.shape[0]

    def start(r, _):
        for k in range(2):
            _row_tile_copy(y_ref, pos_ref[k, r], ybuf.at[k], r, sem).start()
        return 0

    lax.fori_loop(0, tm, start, 0, unroll=DMA_UNROLL)
    for k in range(2):
        pltpu.make_async_copy(y_ref.at[pl.ds(0, tm * ROW_CHUNKS), :], ybuf.at[k], sem).wait()
    g = gate_ref[...]
    o_ref[...] = (h_ref[...] + g[:, 0:1] * _load_row_tiles(ybuf.at[0], tm)
                  + g[:, 1:2] * _load_row_tiles(ybuf.at[1], tm))


def _combine(pos, h, gate_t, y):
    T, D = h.shape
    tm = TM_ROWS
    return pl.pallas_call(
        _combine_kernel,
        grid=(T // tm,),
        in_specs=[pl.BlockSpec((2, tm), lambda i: (0, i), memory_space=pltpu.SMEM),
                  pl.BlockSpec((tm, D), lambda i: (i, 0)),
                  pl.BlockSpec((tm, 2), lambda i: (i, 0)),
                  pl.BlockSpec(memory_space=pl.ANY)],
        out_specs=pl.BlockSpec((tm, D), lambda i: (i, 0)),
        out_shape=jax.ShapeDtypeStruct((T, D), F32),
        scratch_shapes=[pltpu.VMEM((2, tm * ROW_CHUNKS, LANES), F32), pltpu.SemaphoreType.DMA(())],
        compiler_params=_cparams(("arbitrary",)),
        name="combine",
    )(pos, h, gate_t, y)


def _rope_tables(S):
    inv_freq = 1.0 / (ROPE_THETA ** (jnp.arange(0, HEAD_DIM, 2, dtype=F32) / HEAD_DIM))
    ang = jnp.arange(S, dtype=F32)[:, None] * inv_freq[None, :]
    cos, sin = jnp.cos(ang), jnp.sin(ang)
    cos_t = jnp.tile(cos, (1, 2 * HEADS_PER_TILE))
    sin_t = jnp.tile(jnp.concatenate([-sin, sin], axis=1), (1, HEADS_PER_TILE))
    return cos_t, sin_t


def _layer(x, norm_mix, w_in, b_forget, q_norm_fox, k_norm_fox, q_norm_dil, k_norm_dil,
           out_norm_fox, out_norm_dil, w_out, norm_ffn, w_router_group, b_router_group,
           w_router_expert, b_router_expert, w_gate, w_up, w_down):
    B, S, D = x.shape
    T = B * S
    n_main = 6 * D_GRP

    w_main = w_in[:, :n_main].astype(BF16)
    w_f = jnp.pad(w_in[:, n_main:], ((0, 0), (0, LANES - N_HEADS))).astype(BF16)
    b_f = jnp.pad(b_forget, (0, LANES - N_HEADS))[None, :]
    per_head = lambda g: jnp.tile(g, N_HEADS)[None, :]
    bd = jnp.kron(jnp.eye(N_HEADS, dtype=F32), jnp.ones((HEAD_DIM, HEAD_DIM), F32)).astype(BF16)
    cos_t, sin_t = _rope_tables(S)
    tri = jnp.tril(jnp.ones((TM_IN, TM_IN), F32))
    upper = jnp.triu(jnp.ones((TM_OUT, TM_OUT), F32), k=1).astype(BF16)
    w_r = jnp.concatenate([
        jnp.pad(w_router_group.T, ((0, 8 - N_GROUPS), (0, 0))),
        w_router_expert.transpose(0, 2, 1).reshape(N_EXPERTS, D)], axis=0)
    b_r = jnp.concatenate([jnp.pad(b_router_group, (0, 8 - N_GROUPS)), b_router_expert.reshape(-1)])[:, None]

    qa, ka, va, qb, kb, vb = _inproj(
        x, norm_mix[None, :], w_main, w_f, b_f, per_head(q_norm_fox), per_head(k_norm_fox),
        per_head(q_norm_dil), per_head(k_norm_dil), bd, cos_t, sin_t, tri)
    oa = _fox(qa, ka, va)
    ob = _dilated(qb, kb, vb)

    h, hn, eid, gate, rank, cnt = _outproj(
        oa.reshape(T, D_GRP), ob.reshape(T, D_GRP), x.reshape(T, D), out_norm_fox[None, :],
        out_norm_dil[None, :], w_out.astype(BF16), norm_ffn[None, :], w_r, b_r, upper)

    counts = cnt[:, 0]
    padded = ((counts + TILE_E - 1) // TILE_E) * TILE_E
    ends = jnp.cumsum(padded)
    starts = ends - padded
    is_expert = eid[:, :, None] == jnp.arange(N_EXPERTS, dtype=I32)
    pos = jnp.sum(jnp.where(is_expert, starts, 0), axis=-1) + rank
    n_tiles = (2 * T) // TILE_E + N_EXPERTS
    tile_start = jnp.arange(n_tiles, dtype=I32) * TILE_E
    tile_expert = jnp.minimum(jnp.sum((ends[None, :] <= tile_start[:, None]).astype(I32), axis=1), N_EXPERTS - 1)
    tile_valid = (tile_start < ends[-1]).astype(I32)

    xs = _scatter(pos, hn, jnp.zeros((n_tiles * TILE_E * ROW_CHUNKS, LANES), F32))
    y = _experts(tile_expert, tile_valid, xs, w_gate, w_up, w_down)
    out = _combine(pos, h, gate.T, y)
    return out.reshape(B, S, D)


def kernel(x, norm_mix, w_in, b_forget, q_norm_fox, k_norm_fox, q_norm_dil, k_norm_dil, out_norm_fox,
           out_norm_dil, w_out, norm_ffn, w_router_group, b_router_group, w_router_expert,
           b_router_expert, w_gate, w_up, w_down):
    h = x
    for l in range(norm_mix.shape[0]):
        h = _layer(h, norm_mix[l], w_in[l], b_forget[l], q_norm_fox[l], k_norm_fox[l], q_norm_dil[l],
                   k_norm_dil[l], out_norm_fox[l], out_norm_dil[l], w_out[l], norm_ffn[l],
                   w_router_group[l], b_router_group[l], w_router_expert[l], b_router_expert[l],
                   w_gate[l], w_up[l], w_down[l])
    return h
```

```python
import functools
import math

import jax
import jax.numpy as jnp
from jax import lax
from jax.experimental import pallas as pl
from jax.experimental.pallas import tpu as pltpu

F32 = jnp.float32
BF16 = jnp.bfloat16
I32 = jnp.int32

D_MODEL = 1024
HEAD_DIM = 64
N_HEADS = 8
D_GRP = N_HEADS * HEAD_DIM
LANES = 128
HEADS_PER_TILE = LANES // HEAD_DIM
N_PAIRS = D_GRP // LANES
DIL_PATTERNS = ((128, 1), (512, 4), (2048, 16))
BLOCK = 128
ROPE_THETA = 10000.0
N_GROUPS = 4
EXPERTS_PER_GROUP = 8
N_EXPERTS = N_GROUPS * EXPERTS_PER_GROUP
D_EXPERT = 512
EPS = 1e-6
NEG = -1e30
LOG2E = 1.4426950408889634

TM_IN = 512
TQ = 512
TK_WIDE = 1024
TK_SUB = 256
DIL_SPAN = 2048
DIL_UNROLL = 8
TM_OUT = 512
TM_ROWS = 256
TILE_E = 256
ROUTER_ROWS = 8 + N_EXPERTS
ROW_CHUNKS = D_MODEL // LANES
DMA_UNROLL = 8
VMEM_LIMIT = 56 * 1024 * 1024


def _cparams(sem):
    return pltpu.CompilerParams(dimension_semantics=sem, vmem_limit_bytes=VMEM_LIMIT)


def _inproj_kernel(x_ref, gmix_ref, w_ref, wvt_ref, wf_ref, bf_ref, gqa_ref, gka_ref, gqb_ref, gkb_ref,
                   bd_ref, cos_ref, sin_ref, tri_ref,
                   qa_ref, ka_ref, va_ref, qb_ref, kb_ref, vb_ref, carry_ref):
    @pl.when(pl.program_id(1) == 0)
    def _():
        carry_ref[...] = jnp.zeros_like(carry_ref)

    x = x_ref[0]
    ms = jnp.mean(x * x, axis=-1, keepdims=True)
    xn = (x * lax.rsqrt(ms + EPS) * gmix_ref[...]).astype(BF16)

    def seg(j):
        return jnp.dot(xn, w_ref[:, j * D_GRP:(j + 1) * D_GRP], preferred_element_type=F32)

    def head_norm(y, g_ref, scale):
        ss = jnp.dot((y * y).astype(BF16), bd_ref[...], preferred_element_type=F32) * (1.0 / HEAD_DIM)
        return y * lax.rsqrt(ss + EPS) * (g_ref[...] * scale)

    cos = cos_ref[...]
    sin = sin_ref[...]
    lane = lax.broadcasted_iota(I32, (x.shape[0], LANES), 1)
    first_half = (lane % HEAD_DIM) < (HEAD_DIM // 2)

    def rope(y):
        outs = []
        for j in range(N_PAIRS):
            ys = y[:, j * LANES:(j + 1) * LANES]
            partner = jnp.where(first_half, pltpu.roll(ys, LANES - HEAD_DIM // 2, 1),
                                pltpu.roll(ys, HEAD_DIM // 2, 1))
            outs.append(ys * cos + partner * sin)
        return jnp.concatenate(outs, axis=1)

    scale = 1.0 / math.sqrt(HEAD_DIM)
    va_ref[0] = lax.dot_general(wvt_ref[...], xn, (((1,), (1,)), ((), ())),
                                preferred_element_type=F32).astype(va_ref.dtype)
    qb_ref[0] = rope(head_norm(seg(3), gqb_ref, scale)).astype(qb_ref.dtype)
    kb_ref[0] = rope(head_norm(seg(4), gkb_ref, 1.0)).astype(kb_ref.dtype)
    vb_ref[0] = seg(5).astype(vb_ref.dtype)

    fa = jnp.dot(xn, wf_ref[...], preferred_element_type=F32) + bf_ref[...]
    logf = jnp.minimum(fa, 0.0) - jnp.log1p(jnp.exp(-jnp.abs(fa)))
    c = jnp.dot(tri_ref[...], logf, preferred_element_type=F32,
                precision=lax.Precision.HIGHEST) + carry_ref[...]
    carry_ref[...] = c[c.shape[0] - 1:, :]

    qa = head_norm(seg(0), gqa_ref, scale * LOG2E)
    ka = head_norm(seg(1), gka_ref, 1.0)
    c2 = c * LOG2E
    ones = jnp.where((lane >= HEAD_DIM + 3) & (lane < HEAD_DIM + 6), 1.0, 0.0)
    for h in range(N_HEADS):
        cb = jnp.broadcast_to(c2[:, h:h + 1], (x.shape[0], LANES))
        hi = cb.astype(BF16).astype(F32)
        mid = (cb - hi).astype(BF16).astype(F32)
        lo = cb - hi - mid
        pieces = jnp.where(lane == HEAD_DIM, hi, jnp.where(lane == HEAD_DIM + 1, mid,
                           jnp.where(lane == HEAD_DIM + 2, lo, 0.0)))
        q_extra = pieces + ones
        k_extra = jnp.where((lane >= HEAD_DIM) & (lane < HEAD_DIM + 3), 1.0, 0.0) \
            - pltpu.roll(pieces, 3, 1)
        j, odd = divmod(h, HEADS_PER_TILE)
        qp = qa[:, j * LANES:(j + 1) * LANES]
        kp = ka[:, j * LANES:(j + 1) * LANES]
        if odd:
            qp = pltpu.roll(qp, HEAD_DIM, 1)
            kp = pltpu.roll(kp, HEAD_DIM, 1)
        qa_ref[0, h] = jnp.where(lane < HEAD_DIM, qp, q_extra).astype(qa_ref.dtype)
        ka_ref[0, h] = jnp.where(lane < HEAD_DIM, kp, k_extra).astype(ka_ref.dtype)


def _inproj(x, gmix, w_main, w_vt, w_f, b_f, gqa, gka, gqb, gkb, bd, cos_t, sin_t, tri):
    B, S, D = x.shape
    tm = TM_IN
    const = lambda shape: pl.BlockSpec(shape, lambda b, i: (0,) * len(shape))
    tok = lambda w, dt: jax.ShapeDtypeStruct((B, S, w), dt)
    tok_spec = lambda w: pl.BlockSpec((1, tm, w), lambda b, i: (b, i, 0))
    head_spec = pl.BlockSpec((1, N_HEADS, tm, LANES), lambda b, i: (b, 0, i, 0))
    head_shape = jax.ShapeDtypeStruct((B, N_HEADS, S, LANES), BF16)
    return pl.pallas_call(
        _inproj_kernel,
        grid=(B, S // tm),
        in_specs=[tok_spec(D), const((1, D)), const(w_main.shape), const(w_vt.shape), const(w_f.shape),
                  const((1, LANES)),
                  const((1, D_GRP)), const((1, D_GRP)), const((1, D_GRP)), const((1, D_GRP)),
                  const((D_GRP, D_GRP)),
                  pl.BlockSpec((tm, LANES), lambda b, i: (i, 0)),
                  pl.BlockSpec((tm, LANES), lambda b, i: (i, 0)),
                  const((tm, tm))],
        out_specs=[head_spec, head_spec, pl.BlockSpec((1, D_GRP, tm), lambda b, i: (b, 0, i))]
        + [tok_spec(D_GRP)] * 3,
        out_shape=[head_shape, head_shape, jax.ShapeDtypeStruct((B, D_GRP, S), BF16),
                   tok(D_GRP, F32), tok(D_GRP, F32), tok(D_GRP, F32)],
        scratch_shapes=[pltpu.VMEM((1, LANES), F32)],
        compiler_params=_cparams(("arbitrary", "arbitrary")),
        name="inproj",
    )(x, gmix, w_main, w_vt, w_f, b_f, gqa, gka, gqb, gkb, bd, cos_t, sin_t, tri)


def _fox_kernel(q_ref, k_ref, v_ref, o_ref):
    qi = pl.program_id(2)
    tq = q_ref.shape[2]

    def step(start, width, carry, diag):
        carry = list(carry)
        sub = min(TK_SUB, width)
        chunks = [(c, j) for c in range(width // sub) for j in range(HEADS_PER_TILE)]
        scores = {}
        for c, j in chunks:
            k = k_ref[0, j, pl.ds(start + c * sub, sub), :]
            scores[c, j] = lax.dot_general(k, q_ref[0, j], (((1,), (1,)), ((), ())),
                                           preferred_element_type=F32)
        for c, j in chunks:
                m, l, acc = carry[j]
                vt = v_ref[0, :, pl.ds(start + c * sub, sub)]
                s = scores[c, j]
                if diag:
                    key = lax.broadcasted_iota(I32, (sub, tq), 0) + c * sub
                    qry = lax.broadcasted_iota(I32, (sub, tq), 1)
                    s = jnp.where(key <= qry, s, NEG)
                m_new = jnp.maximum(m, jnp.max(s, axis=0, keepdims=True))
                alpha = jnp.exp2(m - m_new)
                p = jnp.exp2(s - m_new)
                l = alpha * l + jnp.sum(p, axis=0, keepdims=True)
                acc = alpha * acc + jnp.dot(vt, p.astype(BF16), preferred_element_type=F32)
                carry[j] = (m_new, l, acc)
        return tuple(carry)

    init = tuple((jnp.full((1, tq), NEG, F32), jnp.zeros((1, tq), F32), jnp.zeros((LANES, tq), F32))
                 for _ in range(HEADS_PER_TILE))
    per_wide = TK_WIDE // tq
    n_wide = qi // per_wide
    carry = lax.fori_loop(
        0, n_wide, lambda i, c: step(pl.multiple_of(i * TK_WIDE, TK_WIDE), TK_WIDE, c, False), init)
    for extra in range(per_wide - 1):
        carry = lax.cond(qi - n_wide * per_wide > extra,
                         lambda c, e=extra: step(pl.multiple_of((n_wide * per_wide + e) * tq, tq), tq, c, False),
                         lambda c: c, carry)
    carry = step(pl.multiple_of(qi * tq, tq), tq, carry, True)
    outs = [acc / l for (_, l, acc) in carry]
    feat = lax.broadcasted_iota(I32, (LANES, tq), 0)
    o_ref[0] = jnp.where(feat < HEAD_DIM, outs[0], outs[1]).T.astype(o_ref.dtype)


def _fox(qa, ka, va_t):
    B, _, S, _ = qa.shape
    return pl.pallas_call(
        _fox_kernel,
        grid=(B, N_PAIRS, S // TQ),
        in_specs=[pl.BlockSpec((1, HEADS_PER_TILE, TQ, LANES), lambda b, hp, i: (b, hp, i, 0)),
                  pl.BlockSpec((1, HEADS_PER_TILE, S, LANES), lambda b, hp, i: (b, hp, 0, 0)),
                  pl.BlockSpec((1, LANES, S), lambda b, hp, i: (b, hp, 0))],
        out_specs=pl.BlockSpec((1, TQ, LANES), lambda b, hp, i: (b, i, hp)),
        out_shape=jax.ShapeDtypeStruct((B, S, D_GRP), F32),
        compiler_params=_cparams(("arbitrary", "arbitrary", "arbitrary")),
        name="fox",
    )(qa, ka, va_t)


def _dilated_kernel(q_ref, kp_ref, kc_ref, vp_ref, vc_ref, o_ref, qq, kk, vv, osc, lsc):
    u = pl.program_id(1)
    span = q_ref.shape[1]
    qq[...] = q_ref[0]
    kk[0:span, :] = kp_ref[0]
    kk[span:2 * span, :] = kc_ref[0]
    vv[0:span, :] = vp_ref[0]
    vv[span:2 * span, :] = vc_ref[0]

    lane = lax.broadcasted_iota(I32, (BLOCK, LANES), 1)
    ql = lax.broadcasted_iota(I32, (BLOCK, 2 * BLOCK), 0)
    kl = lax.broadcasted_iota(I32, (BLOCK, 2 * BLOCK), 1)
    dist = ql + BLOCK - kl
    band = (dist >= 0) & (dist <= BLOCK)
    bias = jnp.where(band, 0.0, NEG)
    bias_first = jnp.where(band & (kl >= BLOCK), 0.0, NEG)

    def attend(q_start, k_start, d, first):
        qs = qq[pl.ds(q_start, BLOCK, stride=d), :].astype(BF16)
        ks = kk[pl.ds(k_start, 2 * BLOCK, stride=d), :].astype(BF16)
        vs = vv[pl.ds(k_start, 2 * BLOCK, stride=d), :].astype(BF16)
        mask = jnp.where(first, bias_first, bias)
        o_heads, lse_heads = [], []
        for j in range(HEADS_PER_TILE):
            qj = jnp.where(lane // HEAD_DIM == j, qs, jnp.zeros_like(qs))
            s = lax.dot_general(qj, ks, (((1,), (1,)), ((), ())), preferred_element_type=F32) + mask
            m = jnp.max(s, axis=-1, keepdims=True)
            p = jnp.exp(s - m)
            l = jnp.sum(p, axis=-1, keepdims=True)
            o_heads.append(jnp.dot((p / l).astype(BF16), vs, preferred_element_type=F32))
            lse_heads.append(m + jnp.log(l))
        o = jnp.where(lane < HEAD_DIM, o_heads[0], o_heads[1])
        lse = jnp.where(lane < HEAD_DIM, lse_heads[0], lse_heads[1])
        return o, lse

    for pidx, (window, d) in enumerate(DIL_PATTERNS):
        assert window // d == BLOCK
        unit = d * BLOCK
        n_units = span // unit

        def body(idx, _, pidx=pidx, d=d, unit=unit):
            w = idx // d
            r = idx - w * d
            q_start = w * unit + r
            k_start = span - unit + q_start
            o, lse = attend(q_start, k_start, d, jnp.logical_and(u == 0, w == 0))
            osc[pidx, pl.ds(q_start, BLOCK, stride=d), :] = o
            lsc[pidx, pl.ds(q_start, BLOCK, stride=d), :] = lse
            return 0

        lax.fori_loop(0, n_units * d, body, 0, unroll=DIL_UNROLL)

    mx = jnp.maximum(jnp.maximum(lsc[0], lsc[1]), lsc[2])
    num = jnp.zeros((span, LANES), F32)
    den = jnp.zeros((span, LANES), F32)
    for pidx in range(len(DIL_PATTERNS)):
        e = jnp.exp(lsc[pidx] - mx)
        num = num + e * osc[pidx]
        den = den + e
    o_ref[0] = (num / den).astype(o_ref.dtype)


def _dilated(qb, kb, vb):
    B, S, _ = qb.shape
    span = DIL_SPAN
    cur = pl.BlockSpec((1, span, LANES), lambda b, u, hp: (b, u, hp))
    prev = pl.BlockSpec((1, span, LANES), lambda b, u, hp: (b, jnp.maximum(u - 1, 0), hp))
    return pl.pallas_call(
        _dilated_kernel,
        grid=(B, S // span, N_PAIRS),
        in_specs=[cur, prev, cur, prev, cur],
        out_specs=cur,
        out_shape=jax.ShapeDtypeStruct((B, S, D_GRP), F32),
        scratch_shapes=[pltpu.VMEM((span, LANES), F32),
                        pltpu.VMEM((2 * span, LANES), F32), pltpu.VMEM((2 * span, LANES), F32),
                        pltpu.VMEM((len(DIL_PATTERNS), span, LANES), F32),
                        pltpu.VMEM((len(DIL_PATTERNS), span, LANES), F32)],
        compiler_params=_cparams(("arbitrary", "arbitrary", "arbitrary")),
        name="dilated",
    )(qb, kb, kb, vb, vb)


def _store_row_tiles(ref, x):
    n = x.shape[0]
    for c in range(ROW_CHUNKS):
        ref[pl.ds(c, n, stride=ROW_CHUNKS), :] = x[:, c * LANES:(c + 1) * LANES]


def _load_row_tiles(ref, n):
    return jnp.concatenate([ref[pl.ds(c, n, stride=ROW_CHUNKS), :] for c in range(ROW_CHUNKS)], axis=1)


def _row_tile_copy(src_ref, src_row, dst_ref, dst_row, sem):
    src = src_ref.at[pl.ds(pl.multiple_of(src_row * ROW_CHUNKS, ROW_CHUNKS), ROW_CHUNKS), :]
    dst = dst_ref.at[pl.ds(pl.multiple_of(dst_row * ROW_CHUNKS, ROW_CHUNKS), ROW_CHUNKS), :]
    return pltpu.make_async_copy(src, dst, sem)


def _outproj_kernel(oa_ref, ob_ref, x_ref, gfox_ref, gdil_ref, wo_ref, gffn_ref, wr_ref, br_ref, upper_ref,
                    h_ref, hn_ref, eid_ref, gate_ref, rank_ref, cnt_ref, run_ref):
    @pl.when(pl.program_id(0) == 0)
    def _():
        run_ref[...] = jnp.zeros_like(run_ref)

    def norm(y, g):
        ms = jnp.mean(y * y, axis=-1, keepdims=True)
        return y * lax.rsqrt(ms + EPS) * g

    a = norm(oa_ref[...], gfox_ref[...]).astype(BF16)
    b = norm(ob_ref[...], gdil_ref[...]).astype(BF16)
    mix = (jnp.dot(a, wo_ref[0:D_GRP, :], preferred_element_type=F32)
           + jnp.dot(b, wo_ref[D_GRP:2 * D_GRP, :], preferred_element_type=F32))
    h = x_ref[...] + mix
    h_ref[...] = h
    hn = norm(h, gffn_ref[...])
    _store_row_tiles(hn_ref, hn)

    z = lax.dot_general(wr_ref[...], hn, (((1,), (1,)), ((), ())), preferred_element_type=F32,
                        precision=lax.Precision.HIGHEST) + br_ref[...]
    tm = z.shape[1]
    best = z[0:1, :]
    g_sel = jnp.zeros((1, tm), I32)
    for g in range(1, N_GROUPS):
        better = z[g:g + 1, :] > best
        g_sel = jnp.where(better, g, g_sel)
        best = jnp.maximum(best, z[g:g + 1, :])
    den = jnp.zeros((1, tm), F32)
    for g in range(N_GROUPS):
        den = den + jnp.exp(z[g:g + 1, :] - best)
    pg_top = 1.0 / den

    ze = jnp.zeros((EXPERTS_PER_GROUP, tm), F32)
    for g in range(N_GROUPS):
        ze = jnp.where(g_sel == g, z[8 + g * EXPERTS_PER_GROUP:8 + (g + 1) * EXPERTS_PER_GROUP, :], ze)
    e_iota = lax.broadcasted_iota(I32, ze.shape, 0)
    v1 = jnp.max(ze, axis=0, keepdims=True)
    i1 = jnp.min(jnp.where(ze == v1, e_iota, EXPERTS_PER_GROUP), axis=0, keepdims=True)
    ze2 = jnp.where(e_iota == i1, -jnp.inf, ze)
    v2 = jnp.max(ze2, axis=0, keepdims=True)
    i2 = jnp.min(jnp.where(ze2 == v2, e_iota, EXPERTS_PER_GROUP), axis=0, keepdims=True)
    e2 = jnp.exp(v2 - v1)
    inv = 1.0 / (1.0 + e2)
    gate1 = inv * pg_top
    gate2 = e2 * inv * pg_top
    eid1 = g_sel * EXPERTS_PER_GROUP + i1
    eid2 = g_sel * EXPERTS_PER_GROUP + i2

    x_iota = lax.broadcasted_iota(I32, (N_EXPERTS, tm), 0)
    hot1 = x_iota == eid1
    hot2 = x_iota == eid2
    multi = jnp.logical_or(hot1, hot2)
    before = jnp.dot(multi.astype(BF16), upper_ref[...], preferred_element_type=F32)
    slot = before + run_ref[:, 0:1]
    rank1 = jnp.sum(jnp.where(hot1, slot, 0.0), axis=0, keepdims=True)
    rank2 = jnp.sum(jnp.where(hot2, slot, 0.0), axis=0, keepdims=True)
    run_ref[...] = run_ref[...] + jnp.sum(multi.astype(F32), axis=1, keepdims=True)

    eid_ref[...] = jnp.concatenate([eid1, eid2], axis=0)
    gate_ref[...] = jnp.concatenate([gate1, gate2], axis=0)
    rank_ref[...] = jnp.concatenate([rank1, rank2], axis=0).astype(I32)
    cnt_ref[...] = run_ref[...].astype(I32)


def _outproj(oa, ob, x2, gfox, gdil, w_out, gffn, w_r, b_r, upper):
    T, D = x2.shape
    tm = TM_OUT
    const = lambda shape: pl.BlockSpec(shape, lambda i: (0,) * len(shape))
    tok = lambda w: pl.BlockSpec((tm, w), lambda i: (i, 0))
    lanes2 = pl.BlockSpec((2, tm), lambda i: (0, i))
    return pl.pallas_call(
        _outproj_kernel,
        grid=(T // tm,),
        in_specs=[tok(D_GRP), tok(D_GRP), tok(D), const((1, D_GRP)), const((1, D_GRP)), const((D, D)),
                  const((1, D)), const((ROUTER_ROWS, D)), const((ROUTER_ROWS, 1)), const((tm, tm))],
        out_specs=[tok(D), pl.BlockSpec((tm * ROW_CHUNKS, LANES), lambda i: (i, 0)),
                   lanes2, lanes2, lanes2, const((N_EXPERTS, LANES))],
        out_shape=[jax.ShapeDtypeStruct((T, D), F32), jax.ShapeDtypeStruct((T * ROW_CHUNKS, LANES), F32),
                   jax.ShapeDtypeStruct((2, T), I32), jax.ShapeDtypeStruct((2, T), F32),
                   jax.ShapeDtypeStruct((2, T), I32), jax.ShapeDtypeStruct((N_EXPERTS, LANES), I32)],
        scratch_shapes=[pltpu.VMEM((N_EXPERTS, LANES), F32)],
        compiler_params=_cparams(("arbitrary",)),
        name="outproj",
    )(oa, ob, x2, gfox, gdil, w_out, gffn, w_r, b_r, upper)


def _scatter_kernel(pos_ref, hn_ref, init_ref, xs_ref, sem):
    del init_ref
    tm = hn_ref.shape[0] // ROW_CHUNKS

    def start(r, _):
        for k in range(2):
            _row_tile_copy(hn_ref, r, xs_ref, pos_ref[k, r], sem).start()
        return 0

    lax.fori_loop(0, tm, start, 0, unroll=DMA_UNROLL)
    for k in range(2):
        pltpu.make_async_copy(hn_ref, xs_ref.at[pl.ds(0, tm * ROW_CHUNKS), :], sem).wait()


def _scatter(pos, hn, xs_init):
    T = hn.shape[0] // ROW_CHUNKS
    tm = TM_ROWS
    return pl.pallas_call(
        _scatter_kernel,
        grid=(T // tm,),
        in_specs=[pl.BlockSpec((2, tm), lambda i: (0, i), memory_space=pltpu.SMEM),
                  pl.BlockSpec((tm * ROW_CHUNKS, LANES), lambda i: (i, 0)),
                  pl.BlockSpec(memory_space=pl.ANY)],
        out_specs=pl.BlockSpec(memory_space=pl.ANY),
        out_shape=jax.ShapeDtypeStruct(xs_init.shape, xs_init.dtype),
        input_output_aliases={2: 0},
        scratch_shapes=[pltpu.SemaphoreType.DMA(())],
        compiler_params=_cparams(("arbitrary",)),
        name="scatter_rows",
    )(pos, hn, xs_init)


def _experts_kernel(te_ref, tv_ref, xs_ref, wg_ref, wu_ref, wd_ref, y_ref, wg_bf, wu_bf, wd_bf):
    i = pl.program_id(0)
    new_expert = jnp.logical_or(i == 0, te_ref[i] != te_ref[jnp.maximum(i - 1, 0)])

    @pl.when(jnp.logical_and(tv_ref[i] != 0, new_expert))
    def _():
        wg_bf[...] = wg_ref[0].astype(BF16)
        wu_bf[...] = wu_ref[0].astype(BF16)
        wd_bf[...] = wd_ref[0].astype(BF16)

    @pl.when(tv_ref[i] != 0)
    def _():
        x = _load_row_tiles(xs_ref, TILE_E).astype(BF16)
        g = jnp.dot(x, wg_bf[...], preferred_element_type=F32)
        up = jnp.dot(x, wu_bf[...], preferred_element_type=F32)
        hmid = (g * jax.nn.sigmoid(g) * up).astype(BF16)
        _store_row_tiles(y_ref, jnp.dot(hmid, wd_bf[...], preferred_element_type=F32))

    @pl.when(tv_ref[i] == 0)
    def _():
        y_ref[...] = jnp.zeros_like(y_ref)


def _experts(tile_expert, tile_valid, xs, w_gate, w_up, w_down):
    n_tiles = xs.shape[0] // (TILE_E * ROW_CHUNKS)
    D = D_MODEL
    rows_spec = pl.BlockSpec((TILE_E * ROW_CHUNKS, LANES), lambda i, te, tv: (i, 0))
    grid_spec = pltpu.PrefetchScalarGridSpec(
        num_scalar_prefetch=2,
        grid=(n_tiles,),
        in_specs=[rows_spec,
                  pl.BlockSpec((1, D, D_EXPERT), lambda i, te, tv: (te[i], 0, 0)),
                  pl.BlockSpec((1, D, D_EXPERT), lambda i, te, tv: (te[i], 0, 0)),
                  pl.BlockSpec((1, D_EXPERT, D), lambda i, te, tv: (te[i], 0, 0))],
        out_specs=rows_spec,
        scratch_shapes=[pltpu.VMEM((D, D_EXPERT), BF16), pltpu.VMEM((D, D_EXPERT), BF16),
                        pltpu.VMEM((D_EXPERT, D), BF16)],
    )
    return pl.pallas_call(
        _experts_kernel,
        grid_spec=grid_spec,
        out_shape=jax.ShapeDtypeStruct(xs.shape, F32),
        compiler_params=_cparams(("arbitrary",)),
        name="experts",
    )(tile_expert, tile_valid, xs, w_gate, w_up, w_down)


def _combine_kernel(pos_ref, h_ref, gate_ref, y_ref, o_ref, ybuf, sem):
    tm = h_ref.shape[0]

    def start(r, _):
        for k in range(2):
            _row_tile_copy(y_ref, pos_ref[k, r], ybuf.at[k], r, sem).start()
        return 0

    lax.fori_loop(0, tm, start, 0, unroll=DMA_UNROLL)
    for k in range(2):
        pltpu.make_async_copy(y_ref.at[pl.ds(0, tm * ROW_CHUNKS), :], ybuf.at[k], sem).wait()
    g = gate_ref[...]
    o_ref[...] = (h_ref[...] + g[:, 0:1] * _load_row_tiles(ybuf.at[0], tm)
                  + g[:, 1:2] * _load_row_tiles(ybuf.at[1], tm))


def _combine(pos, h, gate_t, y):
    T, D = h.shape
    tm = TM_ROWS
    return pl.pallas_call(
        _combine_kernel,
        grid=(T // tm,),
        in_specs=[pl.BlockSpec((2, tm), lambda i: (0, i), memory_space=pltpu.SMEM),
                  pl.BlockSpec((tm, D), lambda i: (i, 0)),
                  pl.BlockSpec((tm, 2), lambda i: (i, 0)),
                  pl.BlockSpec(memory_space=pl.ANY)],
        out_specs=pl.BlockSpec((tm, D), lambda i: (i, 0)),
        out_shape=jax.ShapeDtypeStruct((T, D), F32),
        scratch_shapes=[pltpu.VMEM((2, tm * ROW_CHUNKS, LANES), F32), pltpu.SemaphoreType.DMA(())],
        compiler_params=_cparams(("arbitrary",)),
        name="combine",
    )(pos, h, gate_t, y)


def _rope_tables(S):
    inv_freq = 1.0 / (ROPE_THETA ** (jnp.arange(0, HEAD_DIM, 2, dtype=F32) / HEAD_DIM))
    ang = jnp.arange(S, dtype=F32)[:, None] * inv_freq[None, :]
    cos, sin = jnp.cos(ang), jnp.sin(ang)
    cos_t = jnp.tile(cos, (1, 2 * HEADS_PER_TILE))
    sin_t = jnp.tile(jnp.concatenate([-sin, sin], axis=1), (1, HEADS_PER_TILE))
    return cos_t, sin_t


def _layer(x, norm_mix, w_in, b_forget, q_norm_fox, k_norm_fox, q_norm_dil, k_norm_dil,
           out_norm_fox, out_norm_dil, w_out, norm_ffn, w_router_group, b_router_group,
           w_router_expert, b_router_expert, w_gate, w_up, w_down):
    B, S, D = x.shape
    T = B * S
    n_main = 6 * D_GRP

    w_main = w_in[:, :n_main].astype(BF16)
    w_vt = w_in[:, 2 * D_GRP:3 * D_GRP].T.astype(BF16)
    w_f = jnp.pad(w_in[:, n_main:], ((0, 0), (0, LANES - N_HEADS))).astype(BF16)
    b_f = jnp.pad(b_forget, (0, LANES - N_HEADS))[None, :]
    per_head = lambda g: jnp.tile(g, N_HEADS)[None, :]
    bd = jnp.kron(jnp.eye(N_HEADS, dtype=F32), jnp.ones((HEAD_DIM, HEAD_DIM), F32)).astype(BF16)
    cos_t, sin_t = _rope_tables(S)
    tri = jnp.tril(jnp.ones((TM_IN, TM_IN), F32))
    upper = jnp.triu(jnp.ones((TM_OUT, TM_OUT), F32), k=1).astype(BF16)
    w_r = jnp.concatenate([
        jnp.pad(w_router_group.T, ((0, 8 - N_GROUPS), (0, 0))),
        w_router_expert.transpose(0, 2, 1).reshape(N_EXPERTS, D)], axis=0)
    b_r = jnp.concatenate([jnp.pad(b_router_group, (0, 8 - N_GROUPS)), b_router_expert.reshape(-1)])[:, None]

    qa, ka, va, qb, kb, vb = _inproj(
        x, norm_mix[None, :], w_main, w_vt, w_f, b_f, per_head(q_norm_fox), per_head(k_norm_fox),
        per_head(q_norm_dil), per_head(k_norm_dil), bd, cos_t, sin_t, tri)
    oa = _fox(qa, ka, va)
    ob = _dilated(qb, kb, vb)

    h, hn, eid, gate, rank, cnt = _outproj(
        oa.reshape(T, D_GRP), ob.reshape(T, D_GRP), x.reshape(T, D), out_norm_fox[None, :],
        out_norm_dil[None, :], w_out.astype(BF16), norm_ffn[None, :], w_r, b_r, upper)

    counts = cnt[:, 0]
    padded = ((counts + TILE_E - 1) // TILE_E) * TILE_E
    ends = jnp.cumsum(padded)
    starts = ends - padded
    is_expert = eid[:, :, None] == jnp.arange(N_EXPERTS, dtype=I32)
    pos = jnp.sum(jnp.where(is_expert, starts, 0), axis=-1) + rank
    n_tiles = (2 * T) // TILE_E + N_EXPERTS
    tile_start = jnp.arange(n_tiles, dtype=I32) * TILE_E
    tile_expert = jnp.minimum(jnp.sum((ends[None, :] <= tile_start[:, None]).astype(I32), axis=1), N_EXPERTS - 1)
    tile_valid = (tile_start < ends[-1]).astype(I32)

    xs = _scatter(pos, hn, jnp.zeros((n_tiles * TILE_E * ROW_CHUNKS, LANES), F32))
    y = _experts(tile_expert, tile_valid, xs, w_gate, w_up, w_down)
    out = _combine(pos, h, gate.T, y)
    return out.reshape(B, S, D)


def kernel(x, norm_mix, w_in, b_forget, q_norm_fox, k_norm_fox, q_norm_dil, k_norm_dil, out_norm_fox,
           out_norm_dil, w_out, norm_ffn, w_router_group, b_router_group, w_router_expert,
           b_router_expert, w_gate, w_up, w_down):
    h = x
    for l in range(norm_mix.shape[0]):
        h = _layer(h, norm_mix[l], w_in[l], b_forget[l], q_norm_fox[l], k_norm_fox[l], q_norm_dil[l],
                   k_norm_dil[l], out_norm_fox[l], out_norm_dil[l], w_out[l], norm_ffn[l],
                   w_router_group[l], b_router_group[l], w_router_expert[l], b_router_expert[l],
                   w_gate[l], w_up[l], w_down[l])
    return h
```

```python
import functools
import math

import jax
import jax.numpy as jnp
from jax import lax
from jax.experimental import pallas as pl
from jax.experimental.pallas import tpu as pltpu

F32 = jnp.float32
BF16 = jnp.bfloat16
I32 = jnp.int32

D_MODEL = 1024
HEAD_DIM = 64
N_HEADS = 8
D_GRP = N_HEADS * HEAD_DIM
LANES = 128
HEADS_PER_TILE = LANES // HEAD_DIM
N_PAIRS = D_GRP // LANES
DIL_PATTERNS = ((128, 1), (512, 4), (2048, 16))
BLOCK = 128
ROPE_THETA = 10000.0
N_GROUPS = 4
EXPERTS_PER_GROUP = 8
N_EXPERTS = N_GROUPS * EXPERTS_PER_GROUP
D_EXPERT = 512
EPS = 1e-6
NEG = -1e30
LOG2E = 1.4426950408889634

TM_IN = 512
TQ = 512
TK_WIDE = 1024
TK_SUB = 256
FOX_AHEAD = 4
DIL_SPAN = 2048
DIL_GROUP = 8
DIL_PRE = 4
TM_OUT = 512
TM_ROWS = 256
TILE_E = 256
ROUTER_ROWS = 8 + N_EXPERTS
ROW_CHUNKS = D_MODEL // LANES
DMA_UNROLL = 8
VMEM_LIMIT = 56 * 1024 * 1024


def _cparams(sem, flags=None):
    return pltpu.CompilerParams(dimension_semantics=sem, vmem_limit_bytes=VMEM_LIMIT, flags=flags)


def _inproj_kernel(x_ref, gmix_ref, w_ref, wvt_ref, wf_ref, bf_ref, gqa_ref, gka_ref, gqb_ref, gkb_ref,
                   bd_ref, cos_ref, sin_ref, tri_ref,
                   qa_ref, ka_ref, va_ref, qb_ref, kb_ref, vb_ref, carry_ref):
    @pl.when(pl.program_id(1) == 0)
    def _():
        carry_ref[...] = jnp.zeros_like(carry_ref)

    x = x_ref[0]
    ms = jnp.mean(x * x, axis=-1, keepdims=True)
    xn = (x * lax.rsqrt(ms + EPS) * gmix_ref[...]).astype(BF16)

    def seg(j):
        return jnp.dot(xn, w_ref[:, j * D_GRP:(j + 1) * D_GRP], preferred_element_type=F32)

    def head_norm(y, g_ref, scale):
        ss = jnp.dot((y * y).astype(BF16), bd_ref[...], preferred_element_type=F32) * (1.0 / HEAD_DIM)
        return y * lax.rsqrt(ss + EPS) * (g_ref[...] * scale)

    cos = cos_ref[...]
    sin = sin_ref[...]
    lane = lax.broadcasted_iota(I32, (x.shape[0], LANES), 1)
    first_half = (lane % HEAD_DIM) < (HEAD_DIM // 2)

    def rope(y):
        outs = []
        for j in range(N_PAIRS):
            ys = y[:, j * LANES:(j + 1) * LANES]
            partner = jnp.where(first_half, pltpu.roll(ys, LANES - HEAD_DIM // 2, 1),
                                pltpu.roll(ys, HEAD_DIM // 2, 1))
            outs.append(ys * cos + partner * sin)
        return jnp.concatenate(outs, axis=1)

    scale = 1.0 / math.sqrt(HEAD_DIM)
    va_ref[0] = lax.dot_general(wvt_ref[...], xn, (((1,), (1,)), ((), ())),
                                preferred_element_type=F32).astype(va_ref.dtype)
    qb_ref[0] = rope(head_norm(seg(3), gqb_ref, scale)).astype(qb_ref.dtype)
    kb_ref[0] = rope(head_norm(seg(4), gkb_ref, 1.0)).astype(kb_ref.dtype)
    vb_ref[0] = seg(5).astype(vb_ref.dtype)

    fa = jnp.dot(xn, wf_ref[...], preferred_element_type=F32) + bf_ref[...]
    logf = jnp.minimum(fa, 0.0) - jnp.log1p(jnp.exp(-jnp.abs(fa)))
    c = jnp.dot(tri_ref[...], logf, preferred_element_type=F32,
                precision=lax.Precision.HIGHEST) + carry_ref[...]
    carry_ref[...] = c[c.shape[0] - 1:, :]

    qa = head_norm(seg(0), gqa_ref, scale * LOG2E)
    ka = head_norm(seg(1), gka_ref, 1.0)
    c2 = c * LOG2E
    ones = jnp.where((lane >= HEAD_DIM + 3) & (lane < HEAD_DIM + 6), 1.0, 0.0)
    for h in range(N_HEADS):
        cb = jnp.broadcast_to(c2[:, h:h + 1], (x.shape[0], LANES))
        hi = cb.astype(BF16).astype(F32)
        mid = (cb - hi).astype(BF16).astype(F32)
        lo = cb - hi - mid
        pieces = jnp.where(lane == HEAD_DIM, hi, jnp.where(lane == HEAD_DIM + 1, mid,
                           jnp.where(lane == HEAD_DIM + 2, lo, 0.0)))
        q_extra = pieces + ones
        k_extra = jnp.where((lane >= HEAD_DIM) & (lane < HEAD_DIM + 3), 1.0, 0.0) \
            - pltpu.roll(pieces, 3, 1)
        j, odd = divmod(h, HEADS_PER_TILE)
        qp = qa[:, j * LANES:(j + 1) * LANES]
        kp = ka[:, j * LANES:(j + 1) * LANES]
        if odd:
            qp = pltpu.roll(qp, HEAD_DIM, 1)
            kp = pltpu.roll(kp, HEAD_DIM, 1)
        qa_ref[0, h] = jnp.where(lane < HEAD_DIM, qp, q_extra).astype(qa_ref.dtype)
        ka_ref[0, h] = jnp.where(lane < HEAD_DIM, kp, k_extra).astype(ka_ref.dtype)


def _inproj(x, gmix, w_main, w_vt, w_f, b_f, gqa, gka, gqb, gkb, bd, cos_t, sin_t, tri):
    B, S, D = x.shape
    tm = TM_IN
    const = lambda shape: pl.BlockSpec(shape, lambda b, i: (0,) * len(shape))
    tok = lambda w, dt: jax.ShapeDtypeStruct((B, S, w), dt)
    tok_spec = lambda w: pl.BlockSpec((1, tm, w), lambda b, i: (b, i, 0))
    head_spec = pl.BlockSpec((1, N_HEADS, tm, LANES), lambda b, i: (b, 0, i, 0))
    head_shape = jax.ShapeDtypeStruct((B, N_HEADS, S, LANES), BF16)
    return pl.pallas_call(
        _inproj_kernel,
        grid=(B, S // tm),
        in_specs=[tok_spec(D), const((1, D)), const(w_main.shape), const(w_vt.shape), const(w_f.shape),
                  const((1, LANES)),
                  const((1, D_GRP)), const((1, D_GRP)), const((1, D_GRP)), const((1, D_GRP)),
                  const((D_GRP, D_GRP)),
                  pl.BlockSpec((tm, LANES), lambda b, i: (i, 0)),
                  pl.BlockSpec((tm, LANES), lambda b, i: (i, 0)),
                  const((tm, tm))],
        out_specs=[head_spec, head_spec, pl.BlockSpec((1, D_GRP, tm), lambda b, i: (b, 0, i))]
        + [tok_spec(D_GRP)] * 3,
        out_shape=[head_shape, head_shape, jax.ShapeDtypeStruct((B, D_GRP, S), BF16),
                   tok(D_GRP, F32), tok(D_GRP, F32), tok(D_GRP, F32)],
        scratch_shapes=[pltpu.VMEM((1, LANES), F32)],
        compiler_params=_cparams(("arbitrary", "arbitrary")),
        name="inproj",
    )(x, gmix, w_main, w_vt, w_f, b_f, gqa, gka, gqb, gkb, bd, cos_t, sin_t, tri)


def _fox_kernel(q_ref, k_ref, v_ref, o_ref):
    qi = pl.program_id(2)
    tq = q_ref.shape[2]

    def step(start, width, carry, diag):
        carry = list(carry)
        sub = min(TK_SUB, width)
        chunks = [(c, j) for c in range(width // sub) for j in range(HEADS_PER_TILE)]
        def score(c, j):
            k = k_ref[0, j, pl.ds(start + c * sub, sub), :]
            return lax.dot_general(k, q_ref[0, j], (((1,), (1,)), ((), ())), preferred_element_type=F32)

        scores = {cj: score(*cj) for cj in chunks[:FOX_AHEAD]}
        for n, (c, j) in enumerate(chunks):
                if n + FOX_AHEAD < len(chunks):
                    nxt = chunks[n + FOX_AHEAD]
                    scores[nxt] = score(*nxt)
                m, l, acc = carry[j]
                vt = v_ref[0, :, pl.ds(start + c * sub, sub)]
                s = scores.pop((c, j))
                if diag:
                    key = lax.broadcasted_iota(I32, (sub, tq), 0) + c * sub
                    qry = lax.broadcasted_iota(I32, (sub, tq), 1)
                    s = jnp.where(key <= qry, s, NEG)
                m_new = jnp.maximum(m, jnp.max(s, axis=0, keepdims=True))
                alpha = jnp.exp2(m - m_new)
                p = jnp.exp2(s - m_new)
                l = alpha * l + jnp.sum(p, axis=0, keepdims=True)
                acc = alpha * acc + jnp.dot(vt, p.astype(BF16), preferred_element_type=F32)
                carry[j] = (m_new, l, acc)
        return tuple(carry)

    init = tuple((jnp.full((1, tq), NEG, F32), jnp.zeros((1, tq), F32), jnp.zeros((LANES, tq), F32))
                 for _ in range(HEADS_PER_TILE))
    per_wide = TK_WIDE // tq
    n_wide = qi // per_wide
    carry = lax.fori_loop(
        0, n_wide, lambda i, c: step(pl.multiple_of(i * TK_WIDE, TK_WIDE), TK_WIDE, c, False), init)
    for extra in range(per_wide - 1):
        carry = lax.cond(qi - n_wide * per_wide > extra,
                         lambda c, e=extra: step(pl.multiple_of((n_wide * per_wide + e) * tq, tq), tq, c, False),
                         lambda c: c, carry)
    carry = step(pl.multiple_of(qi * tq, tq), tq, carry, True)
    outs = [acc / l for (_, l, acc) in carry]
    feat = lax.broadcasted_iota(I32, (LANES, tq), 0)
    o_ref[0] = jnp.where(feat < HEAD_DIM, outs[0], outs[1]).T.astype(o_ref.dtype)


def _fox(qa, ka, va_t):
    B, _, S, _ = qa.shape
    return pl.pallas_call(
        _fox_kernel,
        grid=(B, N_PAIRS, S // TQ),
        in_specs=[pl.BlockSpec((1, HEADS_PER_TILE, TQ, LANES), lambda b, hp, i: (b, hp, i, 0)),
                  pl.BlockSpec((1, HEADS_PER_TILE, S, LANES), lambda b, hp, i: (b, hp, 0, 0)),
                  pl.BlockSpec((1, LANES, S), lambda b, hp, i: (b, hp, 0))],
        out_specs=pl.BlockSpec((1, TQ, LANES), lambda b, hp, i: (b, i, hp)),
        out_shape=jax.ShapeDtypeStruct((B, S, D_GRP), F32),
        compiler_params=_cparams(("arbitrary", "arbitrary", "arbitrary")),
        name="fox",
    )(qa, ka, va_t)


def _dilated_kernel(q_ref, kp_ref, kc_ref, vp_ref, vc_ref, o_ref, qq, kk, vv, qq4, kk4, vv4, osc, lsc):
    u = pl.program_id(1)
    span = q_ref.shape[1]
    qq[...] = q_ref[0]
    kk[0:span, :] = kp_ref[0]
    kk[span:2 * span, :] = kc_ref[0]
    vv[0:span, :] = vp_ref[0]
    vv[span:2 * span, :] = vc_ref[0]
    for src, dst in ((qq, qq4), (kk, kk4), (vv, vv4)):
        part = src.shape[0] // DIL_PRE
        for a in range(DIL_PRE):
            dst[a * part:(a + 1) * part, :] = src[pl.ds(a, part, stride=DIL_PRE), :]

    def rows(buf, buf4, start, n, d):
        if d % DIL_PRE:
            return buf[pl.ds(start, n, stride=d), :]
        part = buf4.shape[0] // DIL_PRE
        a = lax.rem(start, DIL_PRE)
        return buf4[pl.ds(a * part + lax.div(start, DIL_PRE), n, stride=d // DIL_PRE), :]

    lane = lax.broadcasted_iota(I32, (BLOCK, LANES), 1)
    ql = lax.broadcasted_iota(I32, (BLOCK, 2 * BLOCK), 0)
    kl = lax.broadcasted_iota(I32, (BLOCK, 2 * BLOCK), 1)
    dist = ql + BLOCK - kl
    band = (dist >= 0) & (dist <= BLOCK)
    bias = jnp.where(band, 0.0, NEG)
    bias_first = jnp.where(band & (kl >= BLOCK), 0.0, NEG)

    def scores(q_start, k_start, d, first):
        qs = rows(qq, qq4, q_start, BLOCK, d).astype(BF16)
        ks = rows(kk, kk4, k_start, 2 * BLOCK, d).astype(BF16)
        mask = jnp.where(first, bias_first, bias)
        out = []
        for j in range(HEADS_PER_TILE):
            qj = jnp.where(lane // HEAD_DIM == j, qs, jnp.zeros_like(qs))
            out.append(lax.dot_general(qj, ks, (((1,), (1,)), ((), ())), preferred_element_type=F32) + mask)
        return out

    def finish(s_heads, k_start, d):
        vs = rows(vv, vv4, k_start, 2 * BLOCK, d).astype(BF16)
        o_heads, lse_heads = [], []
        for s in s_heads:
            m = jnp.max(s, axis=-1, keepdims=True)
            p = jnp.exp(s - m)
            l = jnp.sum(p, axis=-1, keepdims=True)
            o_heads.append(jnp.dot((p / l).astype(BF16), vs, preferred_element_type=F32))
            lse_heads.append(m + jnp.log(l))
        o = jnp.where(lane < HEAD_DIM, o_heads[0], o_heads[1])
        lse = jnp.where(lane < HEAD_DIM, lse_heads[0], lse_heads[1])
        return o, lse

    for pidx, (window, d) in enumerate(DIL_PATTERNS):
        assert window // d == BLOCK
        unit = d * BLOCK
        n_problems = (span // unit) * d
        assert n_problems % DIL_GROUP == 0

        def body(g, _, pidx=pidx, d=d, unit=unit):
            starts, s_all = [], []
            for t in range(DIL_GROUP):
                idx = g * DIL_GROUP + t
                w = idx // d
                q_start = w * unit + (idx - w * d)
                k_start = span - unit + q_start
                starts.append((q_start, k_start))
                s_all.append(scores(q_start, k_start, d, jnp.logical_and(u == 0, w == 0)))
            for (q_start, k_start), s_heads in zip(starts, s_all):
                o, lse = finish(s_heads, k_start, d)
                osc[pidx, pl.ds(q_start, BLOCK, stride=d), :] = o
                lsc[pidx, pl.ds(q_start, BLOCK, stride=d), :] = lse
            return 0

        lax.fori_loop(0, n_problems // DIL_GROUP, body, 0)

    mx = jnp.maximum(jnp.maximum(lsc[0], lsc[1]), lsc[2])
    num = jnp.zeros((span, LANES), F32)
    den = jnp.zeros((span, LANES), F32)
    for pidx in range(len(DIL_PATTERNS)):
        e = jnp.exp(lsc[pidx] - mx)
        num = num + e * osc[pidx]
        den = den + e
    o_ref[0] = (num / den).astype(o_ref.dtype)


def _dilated(qb, kb, vb):
    B, S, _ = qb.shape
    span = DIL_SPAN
    cur = pl.BlockSpec((1, span, LANES), lambda b, u, hp: (b, u, hp))
    prev = pl.BlockSpec((1, span, LANES), lambda b, u, hp: (b, jnp.maximum(u - 1, 0), hp))
    return pl.pallas_call(
        _dilated_kernel,
        grid=(B, S // span, N_PAIRS),
        in_specs=[cur, prev, cur, prev, cur],
        out_specs=cur,
        out_shape=jax.ShapeDtypeStruct((B, S, D_GRP), F32),
        scratch_shapes=[pltpu.VMEM((span, LANES), F32),
                        pltpu.VMEM((2 * span, LANES), F32), pltpu.VMEM((2 * span, LANES), F32),
                        pltpu.VMEM((span, LANES), F32),
                        pltpu.VMEM((2 * span, LANES), F32), pltpu.VMEM((2 * span, LANES), F32),
                        pltpu.VMEM((len(DIL_PATTERNS), span, LANES), F32),
                        pltpu.VMEM((len(DIL_PATTERNS), span, LANES), F32)],
        compiler_params=_cparams(("arbitrary", "arbitrary", "arbitrary")),
        name="dilated",
    )(qb, kb, kb, vb, vb)


def _store_row_tiles(ref, x):
    n = x.shape[0]
    for c in range(ROW_CHUNKS):
        ref[pl.ds(c, n, stride=ROW_CHUNKS), :] = x[:, c * LANES:(c + 1) * LANES]


def _load_row_tiles(ref, n):
    return jnp.concatenate([ref[pl.ds(c, n, stride=ROW_CHUNKS), :] for c in range(ROW_CHUNKS)], axis=1)


def _row_tile_copy(src_ref, src_row, dst_ref, dst_row, sem):
    src = src_ref.at[pl.ds(pl.multiple_of(src_row * ROW_CHUNKS, ROW_CHUNKS), ROW_CHUNKS), :]
    dst = dst_ref.at[pl.ds(pl.multiple_of(dst_row * ROW_CHUNKS, ROW_CHUNKS), ROW_CHUNKS), :]
    return pltpu.make_async_copy(src, dst, sem)


def _outproj_kernel(oa_ref, ob_ref, x_ref, gfox_ref, gdil_ref, wo_ref, gffn_ref, wr_ref, br_ref, upper_ref,
                    h_ref, hn_ref, eid_ref, gate_ref, rank_ref, cnt_ref, run_ref):
    @pl.when(pl.program_id(0) == 0)
    def _():
        run_ref[...] = jnp.zeros_like(run_ref)

    def norm(y, g):
        ms = jnp.mean(y * y, axis=-1, keepdims=True)
        return y * lax.rsqrt(ms + EPS) * g

    a = norm(oa_ref[...], gfox_ref[...]).astype(BF16)
    b = norm(ob_ref[...], gdil_ref[...]).astype(BF16)
    mix = (jnp.dot(a, wo_ref[0:D_GRP, :], preferred_element_type=F32)
           + jnp.dot(b, wo_ref[D_GRP:2 * D_GRP, :], preferred_element_type=F32))
    h = x_ref[...] + mix
    h_ref[...] = h
    hn = norm(h, gffn_ref[...])
    _store_row_tiles(hn_ref, hn)

    z = lax.dot_general(wr_ref[...], hn, (((1,), (1,)), ((), ())), preferred_element_type=F32,
                        precision=lax.Precision.HIGHEST) + br_ref[...]
    tm = z.shape[1]
    best = z[0:1, :]
    g_sel = jnp.zeros((1, tm), I32)
    for g in range(1, N_GROUPS):
        better = z[g:g + 1, :] > best
        g_sel = jnp.where(better, g, g_sel)
        best = jnp.maximum(best, z[g:g + 1, :])
    den = jnp.zeros((1, tm), F32)
    for g in range(N_GROUPS):
        den = den + jnp.exp(z[g:g + 1, :] - best)
    pg_top = 1.0 / den

    ze = jnp.zeros((EXPERTS_PER_GROUP, tm), F32)
    for g in range(N_GROUPS):
        ze = jnp.where(g_sel == g, z[8 + g * EXPERTS_PER_GROUP:8 + (g + 1) * EXPERTS_PER_GROUP, :], ze)
    e_iota = lax.broadcasted_iota(I32, ze.shape, 0)
    v1 = jnp.max(ze, axis=0, keepdims=True)
    i1 = jnp.min(jnp.where(ze == v1, e_iota, EXPERTS_PER_GROUP), axis=0, keepdims=True)
    ze2 = jnp.where(e_iota == i1, -jnp.inf, ze)
    v2 = jnp.max(ze2, axis=0, keepdims=True)
    i2 = jnp.min(jnp.where(ze2 == v2, e_iota, EXPERTS_PER_GROUP), axis=0, keepdims=True)
    e2 = jnp.exp(v2 - v1)
    inv = 1.0 / (1.0 + e2)
    gate1 = inv * pg_top
    gate2 = e2 * inv * pg_top
    eid1 = g_sel * EXPERTS_PER_GROUP + i1
    eid2 = g_sel * EXPERTS_PER_GROUP + i2

    x_iota = lax.broadcasted_iota(I32, (N_EXPERTS, tm), 0)
    hot1 = x_iota == eid1
    hot2 = x_iota == eid2
    multi = jnp.logical_or(hot1, hot2)
    before = jnp.dot(multi.astype(BF16), upper_ref[...], preferred_element_type=F32)
    slot = before + run_ref[:, 0:1]
    rank1 = jnp.sum(jnp.where(hot1, slot, 0.0), axis=0, keepdims=True)
    rank2 = jnp.sum(jnp.where(hot2, slot, 0.0), axis=0, keepdims=True)
    run_ref[...] = run_ref[...] + jnp.sum(multi.astype(F32), axis=1, keepdims=True)

    eid_ref[...] = jnp.concatenate([eid1, eid2], axis=0)
    gate_ref[...] = jnp.concatenate([gate1, gate2], axis=0)
    rank_ref[...] = jnp.concatenate([rank1, rank2], axis=0).astype(I32)
    cnt_ref[...] = run_ref[...].astype(I32)


def _outproj(oa, ob, x2, gfox, gdil, w_out, gffn, w_r, b_r, upper):
    T, D = x2.shape
    tm = TM_OUT
    const = lambda shape: pl.BlockSpec(shape, lambda i: (0,) * len(shape))
    tok = lambda w: pl.BlockSpec((tm, w), lambda i: (i, 0))
    lanes2 = pl.BlockSpec((2, tm), lambda i: (0, i))
    return pl.pallas_call(
        _outproj_kernel,
        grid=(T // tm,),
        in_specs=[tok(D_GRP), tok(D_GRP), tok(D), const((1, D_GRP)), const((1, D_GRP)), const((D, D)),
                  const((1, D)), const((ROUTER_ROWS, D)), const((ROUTER_ROWS, 1)), const((tm, tm))],
        out_specs=[tok(D), pl.BlockSpec((tm * ROW_CHUNKS, LANES), lambda i: (i, 0)),
                   lanes2, lanes2, lanes2, const((N_EXPERTS, LANES))],
        out_shape=[jax.ShapeDtypeStruct((T, D), F32), jax.ShapeDtypeStruct((T * ROW_CHUNKS, LANES), F32),
                   jax.ShapeDtypeStruct((2, T), I32), jax.ShapeDtypeStruct((2, T), F32),
                   jax.ShapeDtypeStruct((2, T), I32), jax.ShapeDtypeStruct((N_EXPERTS, LANES), I32)],
        scratch_shapes=[pltpu.VMEM((N_EXPERTS, LANES), F32)],
        compiler_params=_cparams(("arbitrary",)),
        name="outproj",
    )(oa, ob, x2, gfox, gdil, w_out, gffn, w_r, b_r, upper)


def _scatter_kernel(pos_ref, hn_ref, init_ref, xs_ref, sem):
    del init_ref
    tm = hn_ref.shape[0] // ROW_CHUNKS

    def start(r, _):
        for k in range(2):
            _row_tile_copy(hn_ref, r, xs_ref, pos_ref[k, r], sem).start()
        return 0

    lax.fori_loop(0, tm, start, 0, unroll=DMA_UNROLL)
    for k in range(2):
        pltpu.make_async_copy(hn_ref, xs_ref.at[pl.ds(0, tm * ROW_CHUNKS), :], sem).wait()


def _scatter(pos, hn, xs_init):
    T = hn.shape[0] // ROW_CHUNKS
    tm = TM_ROWS
    return pl.pallas_call(
        _scatter_kernel,
        grid=(T // tm,),
        in_specs=[pl.BlockSpec((2, tm), lambda i: (0, i), memory_space=pltpu.SMEM),
                  pl.BlockSpec((tm * ROW_CHUNKS, LANES), lambda i: (i, 0)),
                  pl.BlockSpec(memory_space=pl.ANY)],
        out_specs=pl.BlockSpec(memory_space=pl.ANY),
        out_shape=jax.ShapeDtypeStruct(xs_init.shape, xs_init.dtype),
        input_output_aliases={2: 0},
        scratch_shapes=[pltpu.SemaphoreType.DMA(())],
        compiler_params=_cparams(("arbitrary",)),
        name="scatter_rows",
    )(pos, hn, xs_init)


def _experts_kernel(te_ref, tv_ref, xs_ref, wg_ref, wu_ref, wd_ref, y_ref, wg_bf, wu_bf, wd_bf):
    i = pl.program_id(0)
    new_expert = jnp.logical_or(i == 0, te_ref[i] != te_ref[jnp.maximum(i - 1, 0)])

    @pl.when(jnp.logical_and(tv_ref[i] != 0, new_expert))
    def _():
        wg_bf[...] = wg_ref[0].astype(BF16)
        wu_bf[...] = wu_ref[0].astype(BF16)
        wd_bf[...] = wd_ref[0].astype(BF16)

    @pl.when(tv_ref[i] != 0)
    def _():
        x = _load_row_tiles(xs_ref, TILE_E).astype(BF16)
        g = jnp.dot(x, wg_bf[...], preferred_element_type=F32)
        up = jnp.dot(x, wu_bf[...], preferred_element_type=F32)
        hmid = (g * jax.nn.sigmoid(g) * up).astype(BF16)
        _store_row_tiles(y_ref, jnp.dot(hmid, wd_bf[...], preferred_element_type=F32))

    @pl.when(tv_ref[i] == 0)
    def _():
        y_ref[...] = jnp.zeros_like(y_ref)


def _experts(tile_expert, tile_valid, xs, w_gate, w_up, w_down):
    n_tiles = xs.shape[0] // (TILE_E * ROW_CHUNKS)
    D = D_MODEL
    rows_spec = pl.BlockSpec((TILE_E * ROW_CHUNKS, LANES), lambda i, te, tv: (i, 0))
    grid_spec = pltpu.PrefetchScalarGridSpec(
        num_scalar_prefetch=2,
        grid=(n_tiles,),
        in_specs=[rows_spec,
                  pl.BlockSpec((1, D, D_EXPERT), lambda i, te, tv: (te[i], 0, 0)),
                  pl.BlockSpec((1, D, D_EXPERT), lambda i, te, tv: (te[i], 0, 0)),
                  pl.BlockSpec((1, D_EXPERT, D), lambda i, te, tv: (te[i], 0, 0))],
        out_specs=rows_spec,
        scratch_shapes=[pltpu.VMEM((D, D_EXPERT), BF16), pltpu.VMEM((D, D_EXPERT), BF16),
                        pltpu.VMEM((D_EXPERT, D), BF16)],
    )
    return pl.pallas_call(
        _experts_kernel,
        grid_spec=grid_spec,
        out_shape=jax.ShapeDtypeStruct(xs.shape, F32),
        compiler_params=_cparams(("arbitrary",)),
        name="experts",
    )(tile_expert, tile_valid, xs, w_gate, w_up, w_down)


def _combine_kernel(pos_ref, h_ref, gate_ref, y_ref, o_ref, ybuf, sem):
    tm = h_ref.shape[0]

    def start(r, _):
        for k in range(2):
            _row_tile_copy(y_ref, pos_ref[k, r], ybuf.at[k], r, sem).start()
        return 0

    lax.fori_loop(0, tm, start, 0, unroll=DMA_UNROLL)
    for k in range(2):
        pltpu.make_async_copy(y_ref.at[pl.ds(0, tm * ROW_CHUNKS), :], ybuf.at[k], sem).wait()
    g = gate_ref[...]
    o_ref[...] = (h_ref[...] + g[:, 0:1] * _load_row_tiles(ybuf.at[0], tm)
                  + g[:, 1:2] * _load_row_tiles(ybuf.at[1], tm))


def _combine(pos, h, gate_t, y):
    T, D = h.shape
    tm = TM_ROWS
    return pl.pallas_call(
        _combine_kernel,
        grid=(T // tm,),
        in_specs=[pl.BlockSpec((2, tm), lambda i: (0, i), memory_space=pltpu.SMEM),
                  pl.BlockSpec((tm, D), lambda i: (i, 0)),
                  pl.BlockSpec((tm, 2), lambda i: (i, 0)),
                  pl.BlockSpec(memory_space=pl.ANY)],
        out_specs=pl.BlockSpec((tm, D), lambda i: (i, 0)),
        out_shape=jax.ShapeDtypeStruct((T, D), F32),
        scratch_shapes=[pltpu.VMEM((2, tm * ROW_CHUNKS, LANES), F32), pltpu.SemaphoreType.DMA(())],
        compiler_params=_cparams(("arbitrary",)),
        name="combine",
    )(pos, h, gate_t, y)


def _rope_tables(S):
    inv_freq = 1.0 / (ROPE_THETA ** (jnp.arange(0, HEAD_DIM, 2, dtype=F32) / HEAD_DIM))
    ang = jnp.arange(S, dtype=F32)[:, None] * inv_freq[None, :]
    cos, sin = jnp.cos(ang), jnp.sin(ang)
    cos_t = jnp.tile(cos, (1, 2 * HEADS_PER_TILE))
    sin_t = jnp.tile(jnp.concatenate([-sin, sin], axis=1), (1, HEADS_PER_TILE))
    return cos_t, sin_t


def _layer(x, norm_mix, w_in, b_forget, q_norm_fox, k_norm_fox, q_norm_dil, k_norm_dil,
           out_norm_fox, out_norm_dil, w_out, norm_ffn, w_router_group, b_router_group,
           w_router_expert, b_router_expert, w_gate, w_up, w_down):
    B, S, D = x.shape
    T = B * S
    n_main = 6 * D_GRP

    w_main = w_in[:, :n_main].astype(BF16)
    w_vt = w_in[:, 2 * D_GRP:3 * D_GRP].T.astype(BF16)
    w_f = jnp.pad(w_in[:, n_main:], ((0, 0), (0, LANES - N_HEADS))).astype(BF16)
    b_f = jnp.pad(b_forget, (0, LANES - N_HEADS))[None, :]
    per_head = lambda g: jnp.tile(g, N_HEADS)[None, :]
    bd = jnp.kron(jnp.eye(N_HEADS, dtype=F32), jnp.ones((HEAD_DIM, HEAD_DIM), F32)).astype(BF16)
    cos_t, sin_t = _rope_tables(S)
    tri = jnp.tril(jnp.ones((TM_IN, TM_IN), F32))
    upper = jnp.triu(jnp.ones((TM_OUT, TM_OUT), F32), k=1).astype(BF16)
    w_r = jnp.concatenate([
        jnp.pad(w_router_group.T, ((0, 8 - N_GROUPS), (0, 0))),
        w_router_expert.transpose(0, 2, 1).reshape(N_EXPERTS, D)], axis=0)
    b_r = jnp.concatenate([jnp.pad(b_router_group, (0, 8 - N_GROUPS)), b_router_expert.reshape(-1)])[:, None]

    qa, ka, va, qb, kb, vb = _inproj(
        x, norm_mix[None, :], w_main, w_vt, w_f, b_f, per_head(q_norm_fox), per_head(k_norm_fox),
        per_head(q_norm_dil), per_head(k_norm_dil), bd, cos_t, sin_t, tri)
    oa = _fox(qa, ka, va)
    ob = _dilated(qb, kb, vb)

    h, hn, eid, gate, rank, cnt = _outproj(
        oa.reshape(T, D_GRP), ob.reshape(T, D_GRP), x.reshape(T, D), out_norm_fox[None, :],
        out_norm_dil[None, :], w_out.astype(BF16), norm_ffn[None, :], w_r, b_r, upper)

    counts = cnt[:, 0]
    padded = ((counts + TILE_E - 1) // TILE_E) * TILE_E
    ends = jnp.cumsum(padded)
    starts = ends - padded
    is_expert = eid[:, :, None] == jnp.arange(N_EXPERTS, dtype=I32)
    pos = jnp.sum(jnp.where(is_expert, starts, 0), axis=-1) + rank
    n_tiles = (2 * T) // TILE_E + N_EXPERTS
    tile_start = jnp.arange(n_tiles, dtype=I32) * TILE_E
    tile_expert = jnp.minimum(jnp.sum((ends[None, :] <= tile_start[:, None]).astype(I32), axis=1), N_EXPERTS - 1)
    tile_valid = (tile_start < ends[-1]).astype(I32)

    xs = _scatter(pos, hn, jnp.zeros((n_tiles * TILE_E * ROW_CHUNKS, LANES), F32))
    y = _experts(tile_expert, tile_valid, xs, w_gate, w_up, w_down)
    out = _combine(pos, h, gate.T, y)
    return out.reshape(B, S, D)


def kernel(x, norm_mix, w_in, b_forget, q_norm_fox, k_norm_fox, q_norm_dil, k_norm_dil, out_norm_fox,
           out_norm_dil, w_out, norm_ffn, w_router_group, b_router_group, w_router_expert,
           b_router_expert, w_gate, w_up, w_down):
    h = x
    for l in range(norm_mix.shape[0]):
        h = _layer(h, norm_mix[l], w_in[l], b_forget[l], q_norm_fox[l], k_norm_fox[l], q_norm_dil[l],
                   k_norm_dil[l], out_norm_fox[l], out_norm_dil[l], w_out[l], norm_ffn[l],
                   w_router_group[l], b_router_group[l], w_router_expert[l], b_router_expert[l],
                   w_gate[l], w_up[l], w_down[l])
    return h
```

```python
import functools
import math

import jax
import jax.numpy as jnp
from jax import lax
from jax.experimental import pallas as pl
from jax.experimental.pallas import tpu as pltpu

F32 = jnp.float32
BF16 = jnp.bfloat16
I32 = jnp.int32

D_MODEL = 1024
HEAD_DIM = 64
N_HEADS = 8
D_GRP = N_HEADS * HEAD_DIM
LANES = 128
HEADS_PER_TILE = LANES // HEAD_DIM
N_PAIRS = D_GRP // LANES
DIL_PATTERNS = ((128, 1), (512, 4), (2048, 16))
BLOCK = 128
ROPE_THETA = 10000.0
N_GROUPS = 4
EXPERTS_PER_GROUP = 8
N_EXPERTS = N_GROUPS * EXPERTS_PER_GROUP
D_EXPERT = 512
EPS = 1e-6
NEG = -1e30
LOG2E = 1.4426950408889634

TM_IN = 512
TQ = 512
TK_WIDE = 1024
TK_SUB = 256
FOX_AHEAD = 4
DIL_SPAN = 2048
DIL_GROUP = 8
DIL_PRE = 4
TM_OUT = 512
TM_ROWS = 256
TILE_E = 256
ROUTER_ROWS = 8 + N_EXPERTS
ROW_CHUNKS = D_MODEL // LANES
DMA_UNROLL = 8
VMEM_LIMIT = 56 * 1024 * 1024


def _cparams(sem, flags=None):
    return pltpu.CompilerParams(dimension_semantics=sem, vmem_limit_bytes=VMEM_LIMIT, flags=flags)


def _inproj_kernel(x_ref, gmix_ref, w_ref, wvt_ref, wf_ref, bf_ref, gqa_ref, gka_ref, gqb_ref, gkb_ref,
                   bd_ref, cos_ref, sin_ref, tri_ref,
                   qa_ref, ka_ref, va_ref, qb_ref, kb_ref, vb_ref, carry_ref):
    @pl.when(pl.program_id(1) == 0)
    def _():
        carry_ref[...] = jnp.zeros_like(carry_ref)

    x = x_ref[0]
    ms = jnp.mean(x * x, axis=-1, keepdims=True)
    xn = (x * lax.rsqrt(ms + EPS) * gmix_ref[...]).astype(BF16)

    def seg(j):
        return jnp.dot(xn, w_ref[:, j * D_GRP:(j + 1) * D_GRP], preferred_element_type=F32)

    def head_norm(y, g_ref, scale):
        ss = jnp.dot((y * y).astype(BF16), bd_ref[...], preferred_element_type=F32) * (1.0 / HEAD_DIM)
        return y * lax.rsqrt(ss + EPS) * (g_ref[...] * scale)

    cos = cos_ref[...]
    sin = sin_ref[...]
    lane = lax.broadcasted_iota(I32, (x.shape[0], LANES), 1)
    first_half = (lane % HEAD_DIM) < (HEAD_DIM // 2)

    def rope(y):
        outs = []
        for j in range(N_PAIRS):
            ys = y[:, j * LANES:(j + 1) * LANES]
            partner = jnp.where(first_half, pltpu.roll(ys, LANES - HEAD_DIM // 2, 1),
                                pltpu.roll(ys, HEAD_DIM // 2, 1))
            outs.append(ys * cos + partner * sin)
        return jnp.concatenate(outs, axis=1)

    scale = 1.0 / math.sqrt(HEAD_DIM)
    va_ref[0] = lax.dot_general(wvt_ref[...], xn, (((1,), (1,)), ((), ())),
                                preferred_element_type=F32).astype(va_ref.dtype)
    qb_ref[0] = rope(head_norm(seg(3), gqb_ref, scale)).astype(qb_ref.dtype)
    kb_ref[0] = rope(head_norm(seg(4), gkb_ref, 1.0)).astype(kb_ref.dtype)
    vb_ref[0] = seg(5).astype(vb_ref.dtype)

    fa = jnp.dot(xn, wf_ref[...], preferred_element_type=F32) + bf_ref[...]
    logf = jnp.minimum(fa, 0.0) - jnp.log1p(jnp.exp(-jnp.abs(fa)))
    c = jnp.dot(tri_ref[...], logf, preferred_element_type=F32,
                precision=lax.Precision.HIGHEST) + carry_ref[...]
    carry_ref[...] = c[c.shape[0] - 1:, :]

    qa = head_norm(seg(0), gqa_ref, scale * LOG2E)
    ka = head_norm(seg(1), gka_ref, 1.0)
    c2 = c * LOG2E
    ones = jnp.where((lane >= HEAD_DIM + 3) & (lane < HEAD_DIM + 6), 1.0, 0.0)
    for h in range(N_HEADS):
        cb = jnp.broadcast_to(c2[:, h:h + 1], (x.shape[0], LANES))
        hi = cb.astype(BF16).astype(F32)
        mid = (cb - hi).astype(BF16).astype(F32)
        lo = cb - hi - mid
        pieces = jnp.where(lane == HEAD_DIM, hi, jnp.where(lane == HEAD_DIM + 1, mid,
                           jnp.where(lane == HEAD_DIM + 2, lo, 0.0)))
        q_extra = pieces + ones
        k_extra = jnp.where((lane >= HEAD_DIM) & (lane < HEAD_DIM + 3), 1.0, 0.0) \
            - pltpu.roll(pieces, 3, 1)
        j, odd = divmod(h, HEADS_PER_TILE)
        qp = qa[:, j * LANES:(j + 1) * LANES]
        kp = ka[:, j * LANES:(j + 1) * LANES]
        if odd:
            qp = pltpu.roll(qp, HEAD_DIM, 1)
            kp = pltpu.roll(kp, HEAD_DIM, 1)
        qa_ref[0, h] = jnp.where(lane < HEAD_DIM, qp, q_extra).astype(qa_ref.dtype)
        ka_ref[0, h] = jnp.where(lane < HEAD_DIM, kp, k_extra).astype(ka_ref.dtype)


def _inproj(x, gmix, w_main, w_vt, w_f, b_f, gqa, gka, gqb, gkb, bd, cos_t, sin_t, tri):
    B, S, D = x.shape
    tm = TM_IN
    const = lambda shape: pl.BlockSpec(shape, lambda b, i: (0,) * len(shape))
    tok = lambda w, dt: jax.ShapeDtypeStruct((B, S, w), dt)
    tok_spec = lambda w: pl.BlockSpec((1, tm, w), lambda b, i: (b, i, 0))
    head_spec = pl.BlockSpec((1, N_HEADS, tm, LANES), lambda b, i: (b, 0, i, 0))
    head_shape = jax.ShapeDtypeStruct((B, N_HEADS, S, LANES), BF16)
    return pl.pallas_call(
        _inproj_kernel,
        grid=(B, S // tm),
        in_specs=[tok_spec(D), const((1, D)), const(w_main.shape), const(w_vt.shape), const(w_f.shape),
                  const((1, LANES)),
                  const((1, D_GRP)), const((1, D_GRP)), const((1, D_GRP)), const((1, D_GRP)),
                  const((D_GRP, D_GRP)),
                  pl.BlockSpec((tm, LANES), lambda b, i: (i, 0)),
                  pl.BlockSpec((tm, LANES), lambda b, i: (i, 0)),
                  const((tm, tm))],
        out_specs=[head_spec, head_spec, pl.BlockSpec((1, D_GRP, tm), lambda b, i: (b, 0, i))]
        + [tok_spec(D_GRP)] * 3,
        out_shape=[head_shape, head_shape, jax.ShapeDtypeStruct((B, D_GRP, S), BF16),
                   tok(D_GRP, F32), tok(D_GRP, F32), tok(D_GRP, F32)],
        scratch_shapes=[pltpu.VMEM((1, LANES), F32)],
        compiler_params=_cparams(("arbitrary", "arbitrary")),
        name="inproj",
    )(x, gmix, w_main, w_vt, w_f, b_f, gqa, gka, gqb, gkb, bd, cos_t, sin_t, tri)


def _fox_kernel(q_ref, k_ref, v_ref, o_ref):
    qi = pl.program_id(2)
    tq = q_ref.shape[2]

    def step(start, width, carry, diag):
        carry = list(carry)
        sub = min(TK_SUB, width)
        chunks = [(c, j) for c in range(width // sub) for j in range(HEADS_PER_TILE)]
        def score(c, j):
            k = k_ref[0, j, pl.ds(start + c * sub, sub), :]
            return lax.dot_general(k, q_ref[0, j], (((1,), (1,)), ((), ())), preferred_element_type=F32)

        scores = {cj: score(*cj) for cj in chunks[:FOX_AHEAD]}
        for n, (c, j) in enumerate(chunks):
                if n + FOX_AHEAD < len(chunks):
                    nxt = chunks[n + FOX_AHEAD]
                    scores[nxt] = score(*nxt)
                m, l, acc = carry[j]
                vt = v_ref[0, :, pl.ds(start + c * sub, sub)]
                s = scores.pop((c, j))
                if diag:
                    key = lax.broadcasted_iota(I32, (sub, tq), 0) + c * sub
                    qry = lax.broadcasted_iota(I32, (sub, tq), 1)
                    s = jnp.where(key <= qry, s, NEG)
                m_new = jnp.maximum(m, jnp.max(s, axis=0, keepdims=True))
                alpha = jnp.exp2(m - m_new)
                p = jnp.exp2(s - m_new)
                l = alpha * l + jnp.sum(p, axis=0, keepdims=True)
                acc = alpha * acc + jnp.dot(vt, p.astype(BF16), preferred_element_type=F32)
                carry[j] = (m_new, l, acc)
        return tuple(carry)

    init = tuple((jnp.full((1, tq), NEG, F32), jnp.zeros((1, tq), F32), jnp.zeros((LANES, tq), F32))
                 for _ in range(HEADS_PER_TILE))
    per_wide = TK_WIDE // tq
    n_wide = qi // per_wide
    carry = lax.fori_loop(
        0, n_wide, lambda i, c: step(pl.multiple_of(i * TK_WIDE, TK_WIDE), TK_WIDE, c, False), init)
    for extra in range(per_wide - 1):
        carry = lax.cond(qi - n_wide * per_wide > extra,
                         lambda c, e=extra: step(pl.multiple_of((n_wide * per_wide + e) * tq, tq), tq, c, False),
                         lambda c: c, carry)
    carry = step(pl.multiple_of(qi * tq, tq), tq, carry, True)
    outs = [acc / l for (_, l, acc) in carry]
    feat = lax.broadcasted_iota(I32, (LANES, tq), 0)
    o_ref[0] = jnp.where(feat < HEAD_DIM, outs[0], outs[1]).T.astype(o_ref.dtype)


def _fox(qa, ka, va_t):
    B, _, S, _ = qa.shape
    return pl.pallas_call(
        _fox_kernel,
        grid=(B, N_PAIRS, S // TQ),
        in_specs=[pl.BlockSpec((1, HEADS_PER_TILE, TQ, LANES), lambda b, hp, i: (b, hp, i, 0)),
                  pl.BlockSpec((1, HEADS_PER_TILE, S, LANES), lambda b, hp, i: (b, hp, 0, 0)),
                  pl.BlockSpec((1, LANES, S), lambda b, hp, i: (b, hp, 0))],
        out_specs=pl.BlockSpec((1, TQ, LANES), lambda b, hp, i: (b, i, hp)),
        out_shape=jax.ShapeDtypeStruct((B, S, D_GRP), F32),
        compiler_params=_cparams(("arbitrary", "arbitrary", "arbitrary")),
        name="fox",
    )(qa, ka, va_t)


def _dilated_kernel(q_ref, kp_ref, kc_ref, vp_ref, vc_ref, o_ref, qq, kk, vv, qq4, kk4, vv4, osc, lsc):
    u = pl.program_id(1)
    span = q_ref.shape[1]
    qq[...] = q_ref[0]
    kk[0:span, :] = kp_ref[0]
    kk[span:2 * span, :] = kc_ref[0]
    vv[0:span, :] = vp_ref[0]
    vv[span:2 * span, :] = vc_ref[0]
    for src, dst in ((qq, qq4), (kk, kk4), (vv, vv4)):
        part = src.shape[0] // DIL_PRE
        for a in range(DIL_PRE):
            dst[a * part:(a + 1) * part, :] = src[pl.ds(a, part, stride=DIL_PRE), :]

    def rows(buf, buf4, start, n, d):
        if d % DIL_PRE:
            return buf[pl.ds(start, n, stride=d), :]
        part = buf4.shape[0] // DIL_PRE
        a = lax.rem(start, DIL_PRE)
        return buf4[pl.ds(a * part + lax.div(start, DIL_PRE), n, stride=d // DIL_PRE), :]

    lane = lax.broadcasted_iota(I32, (BLOCK, LANES), 1)
    ql = lax.broadcasted_iota(I32, (BLOCK, 2 * BLOCK), 0)
    kl = lax.broadcasted_iota(I32, (BLOCK, 2 * BLOCK), 1)
    dist = ql + BLOCK - kl
    band = (dist >= 0) & (dist <= BLOCK)
    bias = jnp.where(band, 0.0, NEG)
    bias_first = jnp.where(band & (kl >= BLOCK), 0.0, NEG)

    def scores(q_start, k_start, d, first):
        qs = rows(qq, qq4, q_start, BLOCK, d).astype(BF16)
        ks = rows(kk, kk4, k_start, 2 * BLOCK, d).astype(BF16)
        mask = jnp.where(first, bias_first, bias)
        out = []
        for j in range(HEADS_PER_TILE):
            qj = jnp.where(lane // HEAD_DIM == j, qs, jnp.zeros_like(qs))
            out.append(lax.dot_general(qj, ks, (((1,), (1,)), ((), ())), preferred_element_type=F32) + mask)
        return out

    def finish(s_heads, k_start, d):
        vs = rows(vv, vv4, k_start, 2 * BLOCK, d).astype(BF16)
        o_heads, lse_heads = [], []
        for s in s_heads:
            m = jnp.max(s, axis=-1, keepdims=True)
            p = jnp.exp(s - m)
            l = jnp.sum(p, axis=-1, keepdims=True)
            o_heads.append(jnp.dot((p / l).astype(BF16), vs, preferred_element_type=F32))
            lse_heads.append(m + jnp.log(l))
        o = jnp.where(lane < HEAD_DIM, o_heads[0], o_heads[1])
        lse = jnp.where(lane < HEAD_DIM, lse_heads[0], lse_heads[1])
        return o, lse

    for pidx, (window, d) in enumerate(DIL_PATTERNS):
        assert window // d == BLOCK
        unit = d * BLOCK
        n_problems = (span // unit) * d
        assert n_problems % DIL_GROUP == 0

        def body(g, _, pidx=pidx, d=d, unit=unit):
            starts, s_all = [], []
            for t in range(DIL_GROUP):
                idx = g * DIL_GROUP + t
                w = idx // d
                q_start = w * unit + (idx - w * d)
                k_start = span - unit + q_start
                starts.append((q_start, k_start))
                s_all.append(scores(q_start, k_start, d, jnp.logical_and(u == 0, w == 0)))
            for (q_start, k_start), s_heads in zip(starts, s_all):
                o, lse = finish(s_heads, k_start, d)
                osc[pidx, pl.ds(q_start, BLOCK, stride=d), :] = o
                lsc[pidx, pl.ds(q_start, BLOCK, stride=d), :] = lse
            return 0

        lax.fori_loop(0, n_problems // DIL_GROUP, body, 0)

    mx = jnp.maximum(jnp.maximum(lsc[0], lsc[1]), lsc[2])
    num = jnp.zeros((span, LANES), F32)
    den = jnp.zeros((span, LANES), F32)
    for pidx in range(len(DIL_PATTERNS)):
        e = jnp.exp(lsc[pidx] - mx)
        num = num + e * osc[pidx]
        den = den + e
    o_ref[0] = (num / den).astype(o_ref.dtype)


def _dilated(qb, kb, vb):
    B, S, _ = qb.shape
    span = DIL_SPAN
    cur = pl.BlockSpec((1, span, LANES), lambda b, u, hp: (b, u, hp))
    prev = pl.BlockSpec((1, span, LANES), lambda b, u, hp: (b, jnp.maximum(u - 1, 0), hp))
    return pl.pallas_call(
        _dilated_kernel,
        grid=(B, S // span, N_PAIRS),
        in_specs=[cur, prev, cur, prev, cur],
        out_specs=cur,
        out_shape=jax.ShapeDtypeStruct((B, S, D_GRP), F32),
        scratch_shapes=[pltpu.VMEM((span, LANES), F32),
                        pltpu.VMEM((2 * span, LANES), F32), pltpu.VMEM((2 * span, LANES), F32),
                        pltpu.VMEM((span, LANES), F32),
                        pltpu.VMEM((2 * span, LANES), F32), pltpu.VMEM((2 * span, LANES), F32),
                        pltpu.VMEM((len(DIL_PATTERNS), span, LANES), F32),
                        pltpu.VMEM((len(DIL_PATTERNS), span, LANES), F32)],
        compiler_params=_cparams(("arbitrary", "arbitrary", "arbitrary")),
        name="dilated",
    )(qb, kb, kb, vb, vb)


def _store_row_tiles(ref, x):
    n = x.shape[0]
    for c in range(ROW_CHUNKS):
        ref[pl.ds(c, n, stride=ROW_CHUNKS), :] = x[:, c * LANES:(c + 1) * LANES]


def _load_row_tiles(ref, n):
    return jnp.concatenate([ref[pl.ds(c, n, stride=ROW_CHUNKS), :] for c in range(ROW_CHUNKS)], axis=1)


def _row_tile_copy(src_ref, src_row, dst_ref, dst_row, sem):
    src = src_ref.at[pl.ds(pl.multiple_of(src_row * ROW_CHUNKS, ROW_CHUNKS), ROW_CHUNKS), :]
    dst = dst_ref.at[pl.ds(pl.multiple_of(dst_row * ROW_CHUNKS, ROW_CHUNKS), ROW_CHUNKS), :]
    return pltpu.make_async_copy(src, dst, sem)


def _outproj_kernel(oa_ref, ob_ref, x_ref, gfox_ref, gdil_ref, wo_ref, gffn_ref, wr_ref, br_ref, upper_ref,
                    h_ref, hn_ref, eid_ref, gate_ref, rank_ref, cnt_ref, run_ref):
    @pl.when(pl.program_id(0) == 0)
    def _():
        run_ref[...] = jnp.zeros_like(run_ref)

    def norm(y, g):
        ms = jnp.mean(y * y, axis=-1, keepdims=True)
        return y * lax.rsqrt(ms + EPS) * g

    a = norm(oa_ref[...], gfox_ref[...]).astype(BF16)
    b = norm(ob_ref[...], gdil_ref[...]).astype(BF16)
    mix = (jnp.dot(a, wo_ref[0:D_GRP, :], preferred_element_type=F32)
           + jnp.dot(b, wo_ref[D_GRP:2 * D_GRP, :], preferred_element_type=F32))
    h = x_ref[...] + mix
    h_ref[...] = h
    hn = norm(h, gffn_ref[...])
    _store_row_tiles(hn_ref, hn)

    z = lax.dot_general(wr_ref[...], hn, (((1,), (1,)), ((), ())), preferred_element_type=F32,
                        precision=lax.Precision.HIGHEST) + br_ref[...]
    tm = z.shape[1]
    best = z[0:1, :]
    g_sel = jnp.zeros((1, tm), I32)
    for g in range(1, N_GROUPS):
        better = z[g:g + 1, :] > best
        g_sel = jnp.where(better, g, g_sel)
        best = jnp.maximum(best, z[g:g + 1, :])
    den = jnp.zeros((1, tm), F32)
    for g in range(N_GROUPS):
        den = den + jnp.exp(z[g:g + 1, :] - best)
    pg_top = 1.0 / den

    ze = jnp.zeros((EXPERTS_PER_GROUP, tm), F32)
    for g in range(N_GROUPS):
        ze = jnp.where(g_sel == g, z[8 + g * EXPERTS_PER_GROUP:8 + (g + 1) * EXPERTS_PER_GROUP, :], ze)
    e_iota = lax.broadcasted_iota(I32, ze.shape, 0)
    v1 = jnp.max(ze, axis=0, keepdims=True)
    i1 = jnp.min(jnp.where(ze == v1, e_iota, EXPERTS_PER_GROUP), axis=0, keepdims=True)
    ze2 = jnp.where(e_iota == i1, -jnp.inf, ze)
    v2 = jnp.max(ze2, axis=0, keepdims=True)
    i2 = jnp.min(jnp.where(ze2 == v2, e_iota, EXPERTS_PER_GROUP), axis=0, keepdims=True)
    e2 = jnp.exp(v2 - v1)
    inv = 1.0 / (1.0 + e2)
    gate1 = inv * pg_top
    gate2 = e2 * inv * pg_top
    eid1 = g_sel * EXPERTS_PER_GROUP + i1
    eid2 = g_sel * EXPERTS_PER_GROUP + i2

    x_iota = lax.broadcasted_iota(I32, (N_EXPERTS, tm), 0)
    hot1 = x_iota == eid1
    hot2 = x_iota == eid2
    multi = jnp.logical_or(hot1, hot2)
    before = jnp.dot(multi.astype(BF16), upper_ref[...], preferred_element_type=F32)
    slot = before + run_ref[:, 0:1]
    rank1 = jnp.sum(jnp.where(hot1, slot, 0.0), axis=0, keepdims=True)
    rank2 = jnp.sum(jnp.where(hot2, slot, 0.0), axis=0, keepdims=True)
    run_ref[...] = run_ref[...] + jnp.sum(multi.astype(F32), axis=1, keepdims=True)

    eid_ref[...] = jnp.concatenate([eid1, eid2], axis=0)
    gate_ref[...] = jnp.concatenate([gate1, gate2], axis=0)
    rank_ref[...] = jnp.concatenate([rank1, rank2], axis=0).astype(I32)
    cnt_ref[...] = run_ref[...].astype(I32)


def _outproj(oa, ob, x2, gfox, gdil, w_out, gffn, w_r, b_r, upper):
    T, D = x2.shape
    tm = TM_OUT
    const = lambda shape: pl.BlockSpec(shape, lambda i: (0,) * len(shape))
    tok = lambda w: pl.BlockSpec((tm, w), lambda i: (i, 0))
    lanes2 = pl.BlockSpec((2, tm), lambda i: (0, i))
    return pl.pallas_call(
        _outproj_kernel,
        grid=(T // tm,),
        in_specs=[tok(D_GRP), tok(D_GRP), tok(D), const((1, D_GRP)), const((1, D_GRP)), const((D, D)),
                  const((1, D)), const((ROUTER_ROWS, D)), const((ROUTER_ROWS, 1)), const((tm, tm))],
        out_specs=[tok(D), pl.BlockSpec((tm * ROW_CHUNKS, LANES), lambda i: (i, 0)),
                   lanes2, lanes2, lanes2, const((N_EXPERTS, LANES))],
        out_shape=[jax.ShapeDtypeStruct((T, D), F32), jax.ShapeDtypeStruct((T * ROW_CHUNKS, LANES), F32),
                   jax.ShapeDtypeStruct((2, T), I32), jax.ShapeDtypeStruct((2, T), F32),
                   jax.ShapeDtypeStruct((2, T), I32), jax.ShapeDtypeStruct((N_EXPERTS, LANES), I32)],
        scratch_shapes=[pltpu.VMEM((N_EXPERTS, LANES), F32)],
        compiler_params=_cparams(("arbitrary",)),
        name="outproj",
    )(oa, ob, x2, gfox, gdil, w_out, gffn, w_r, b_r, upper)


def _scatter_kernel(starts_ref, cnt_ref, pos_ref, hn_ref, xs_ref, ring, zero_tile, sems, zero_sem, *, n_steps):
    i = pl.program_id(0)
    tm = hn_ref.shape[0] // ROW_CHUNKS
    slot = lax.rem(i, 2)

    def wait_slot(s):
        for _ in range(2):
            pltpu.make_async_copy(ring.at[s], xs_ref.at[pl.ds(0, tm * ROW_CHUNKS), :], sems.at[s]).wait()

    @pl.when(i >= 2)
    def _():
        wait_slot(slot)

    ring[slot] = hn_ref[...]

    def start(r, _):
        for k in range(2):
            _row_tile_copy(ring.at[slot], r, xs_ref, pos_ref[k, r], sems.at[slot]).start()
        return 0

    lax.fori_loop(0, tm, start, 0, unroll=DMA_UNROLL)

    @pl.when(i == n_steps - 1)
    def _():
        zero_tile[...] = jnp.zeros_like(zero_tile)

        def pad_expert(e, _):
            n_pad = lax.rem(TILE_E - lax.rem(cnt_ref[e], TILE_E), TILE_E)
            first = starts_ref[e] + cnt_ref[e]

            def zero_copy(r):
                return _row_tile_copy(zero_tile, 0, xs_ref, first + r, zero_sem)

            lax.fori_loop(0, n_pad, lambda r, c: (zero_copy(r).start(), c)[1], 0)
            lax.fori_loop(0, n_pad, lambda r, c: (zero_copy(r).wait(), c)[1], 0)
            return 0

        lax.fori_loop(0, N_EXPERTS, pad_expert, 0)
        wait_slot(slot)
        if n_steps > 1:
            wait_slot(1 - slot)
        last = N_EXPERTS - 1
        used_rows = starts_ref[last] + cnt_ref[last] + lax.rem(TILE_E - lax.rem(cnt_ref[last], TILE_E), TILE_E)
        ring[0] = jnp.zeros((tm * ROW_CHUNKS, LANES), F32)
        n_tail = xs_ref.shape[0] // (tm * ROW_CHUNKS) - used_rows // tm

        def tail_copy(t):
            row0 = pl.multiple_of((used_rows + t * tm) * ROW_CHUNKS, tm * ROW_CHUNKS)
            return pltpu.make_async_copy(ring.at[0], xs_ref.at[pl.ds(row0, tm * ROW_CHUNKS), :], zero_sem)

        lax.fori_loop(0, n_tail, lambda t, c: (tail_copy(t).start(), c)[1], 0)
        lax.fori_loop(0, n_tail, lambda t, c: (tail_copy(t).wait(), c)[1], 0)


def _scatter(starts, cnt, pos, hn, n_rows):
    T = hn.shape[0] // ROW_CHUNKS
    tm = TM_ROWS
    assert tm == TILE_E
    n_steps = T // tm
    grid_spec = pltpu.PrefetchScalarGridSpec(
        num_scalar_prefetch=2,
        grid=(n_steps,),
        in_specs=[pl.BlockSpec((2, tm), lambda i, starts, cnt: (0, i), memory_space=pltpu.SMEM),
                  pl.BlockSpec((tm * ROW_CHUNKS, LANES), lambda i, starts, cnt: (i, 0))],
        out_specs=pl.BlockSpec(memory_space=pl.ANY),
        scratch_shapes=[pltpu.VMEM((2, tm * ROW_CHUNKS, LANES), F32), pltpu.VMEM((ROW_CHUNKS, LANES), F32),
                        pltpu.SemaphoreType.DMA((2,)), pltpu.SemaphoreType.DMA(())],
    )
    return pl.pallas_call(
        functools.partial(_scatter_kernel, n_steps=n_steps),
        grid_spec=grid_spec,
        out_shape=jax.ShapeDtypeStruct((n_rows * ROW_CHUNKS, LANES), F32),
        compiler_params=_cparams(("arbitrary",)),
        name="scatter_rows",
    )(starts, cnt, pos, hn)


def _experts_kernel(te_ref, tv_ref, xs_ref, wg_ref, wu_ref, wd_ref, y_ref, wg_bf, wu_bf, wd_bf):
    i = pl.program_id(0)
    new_expert = jnp.logical_or(i == 0, te_ref[i] != te_ref[jnp.maximum(i - 1, 0)])

    @pl.when(jnp.logical_and(tv_ref[i] != 0, new_expert))
    def _():
        wg_bf[...] = wg_ref[0].astype(BF16)
        wu_bf[...] = wu_ref[0].astype(BF16)
        wd_bf[...] = wd_ref[0].astype(BF16)

    @pl.when(tv_ref[i] != 0)
    def _():
        x = _load_row_tiles(xs_ref, TILE_E).astype(BF16)
        g = jnp.dot(x, wg_bf[...], preferred_element_type=F32)
        up = jnp.dot(x, wu_bf[...], preferred_element_type=F32)
        hmid = (g * jax.nn.sigmoid(g) * up).astype(BF16)
        _store_row_tiles(y_ref, jnp.dot(hmid, wd_bf[...], preferred_element_type=F32))

    @pl.when(tv_ref[i] == 0)
    def _():
        y_ref[...] = jnp.zeros_like(y_ref)


def _experts(tile_expert, tile_valid, xs, w_gate, w_up, w_down):
    n_tiles = tile_expert.shape[0]
    D = D_MODEL
    rows_spec = pl.BlockSpec((TILE_E * ROW_CHUNKS, LANES), lambda i, te, tv: (i, 0))
    grid_spec = pltpu.PrefetchScalarGridSpec(
        num_scalar_prefetch=2,
        grid=(n_tiles,),
        in_specs=[rows_spec,
                  pl.BlockSpec((1, D, D_EXPERT), lambda i, te, tv: (te[i], 0, 0)),
                  pl.BlockSpec((1, D, D_EXPERT), lambda i, te, tv: (te[i], 0, 0)),
                  pl.BlockSpec((1, D_EXPERT, D), lambda i, te, tv: (te[i], 0, 0))],
        out_specs=rows_spec,
        scratch_shapes=[pltpu.VMEM((D, D_EXPERT), BF16), pltpu.VMEM((D, D_EXPERT), BF16),
                        pltpu.VMEM((D_EXPERT, D), BF16)],
    )
    return pl.pallas_call(
        _experts_kernel,
        grid_spec=grid_spec,
        out_shape=jax.ShapeDtypeStruct(xs.shape, F32),
        compiler_params=_cparams(("arbitrary",)),
        name="experts",
    )(tile_expert, tile_valid, xs, w_gate, w_up, w_down)


def _combine_kernel(pos_ref, pos_next_ref, h_ref, gate_ref, y_ref, o_ref, ybuf, sems, *, n_steps):
    i = pl.program_id(0)
    tm = h_ref.shape[0]
    slot = lax.rem(i, 2)

    def gather(p_ref, s):
        def start(r, _):
            for k in range(2):
                _row_tile_copy(y_ref, p_ref[k, r], ybuf.at[s, k], r, sems.at[s]).start()
            return 0

        lax.fori_loop(0, tm, start, 0, unroll=DMA_UNROLL)

    @pl.when(i == 0)
    def _():
        gather(pos_ref, slot)

    @pl.when(i + 1 < n_steps)
    def _():
        gather(pos_next_ref, 1 - slot)

    for k in range(2):
        pltpu.make_async_copy(y_ref.at[pl.ds(0, tm * ROW_CHUNKS), :], ybuf.at[slot, k], sems.at[slot]).wait()
    g = gate_ref[...]
    o_ref[...] = (h_ref[...] + g[:, 0:1] * _load_row_tiles(ybuf.at[slot, 0], tm)
                  + g[:, 1:2] * _load_row_tiles(ybuf.at[slot, 1], tm))


def _combine(pos, h, gate_t, y):
    T, D = h.shape
    tm = TM_ROWS
    n_steps = T // tm
    return pl.pallas_call(
        functools.partial(_combine_kernel, n_steps=n_steps),
        grid=(n_steps,),
        in_specs=[pl.BlockSpec((2, tm), lambda i: (0, i), memory_space=pltpu.SMEM),
                  pl.BlockSpec((2, tm), lambda i: (0, jnp.minimum(i + 1, n_steps - 1)), memory_space=pltpu.SMEM),
                  pl.BlockSpec((tm, D), lambda i: (i, 0)),
                  pl.BlockSpec((tm, 2), lambda i: (i, 0)),
                  pl.BlockSpec(memory_space=pl.ANY)],
        out_specs=pl.BlockSpec((tm, D), lambda i: (i, 0)),
        out_shape=jax.ShapeDtypeStruct((T, D), F32),
        scratch_shapes=[pltpu.VMEM((2, 2, tm * ROW_CHUNKS, LANES), F32), pltpu.SemaphoreType.DMA((2,))],
        compiler_params=_cparams(("arbitrary",)),
        name="combine",
    )(pos, pos, h, gate_t, y)


def _rope_tables(S):
    inv_freq = 1.0 / (ROPE_THETA ** (jnp.arange(0, HEAD_DIM, 2, dtype=F32) / HEAD_DIM))
    ang = jnp.arange(S, dtype=F32)[:, None] * inv_freq[None, :]
    cos, sin = jnp.cos(ang), jnp.sin(ang)
    cos_t = jnp.tile(cos, (1, 2 * HEADS_PER_TILE))
    sin_t = jnp.tile(jnp.concatenate([-sin, sin], axis=1), (1, HEADS_PER_TILE))
    return cos_t, sin_t


def _layer(x, norm_mix, w_in, b_forget, q_norm_fox, k_norm_fox, q_norm_dil, k_norm_dil,
           out_norm_fox, out_norm_dil, w_out, norm_ffn, w_router_group, b_router_group,
           w_router_expert, b_router_expert, w_gate, w_up, w_down):
    B, S, D = x.shape
    T = B * S
    n_main = 6 * D_GRP

    w_main = w_in[:, :n_main].astype(BF16)
    w_vt = w_in[:, 2 * D_GRP:3 * D_GRP].T.astype(BF16)
    w_f = jnp.pad(w_in[:, n_main:], ((0, 0), (0, LANES - N_HEADS))).astype(BF16)
    b_f = jnp.pad(b_forget, (0, LANES - N_HEADS))[None, :]
    per_head = lambda g: jnp.tile(g, N_HEADS)[None, :]
    bd = jnp.kron(jnp.eye(N_HEADS, dtype=F32), jnp.ones((HEAD_DIM, HEAD_DIM), F32)).astype(BF16)
    cos_t, sin_t = _rope_tables(S)
    tri = jnp.tril(jnp.ones((TM_IN, TM_IN), F32))
    upper = jnp.triu(jnp.ones((TM_OUT, TM_OUT), F32), k=1).astype(BF16)
    w_r = jnp.concatenate([
        jnp.pad(w_router_group.T, ((0, 8 - N_GROUPS), (0, 0))),
        w_router_expert.transpose(0, 2, 1).reshape(N_EXPERTS, D)], axis=0)
    b_r = jnp.concatenate([jnp.pad(b_router_group, (0, 8 - N_GROUPS)), b_router_expert.reshape(-1)])[:, None]

    qa, ka, va, qb, kb, vb = _inproj(
        x, norm_mix[None, :], w_main, w_vt, w_f, b_f, per_head(q_norm_fox), per_head(k_norm_fox),
        per_head(q_norm_dil), per_head(k_norm_dil), bd, cos_t, sin_t, tri)
    oa = _fox(qa, ka, va)
    ob = _dilated(qb, kb, vb)

    h, hn, eid, gate, rank, cnt = _outproj(
        oa.reshape(T, D_GRP), ob.reshape(T, D_GRP), x.reshape(T, D), out_norm_fox[None, :],
        out_norm_dil[None, :], w_out.astype(BF16), norm_ffn[None, :], w_r, b_r, upper)

    counts = cnt[:, 0]
    padded = ((counts + TILE_E - 1) // TILE_E) * TILE_E
    ends = jnp.cumsum(padded)
    starts = ends - padded
    is_expert = eid[:, :, None] == jnp.arange(N_EXPERTS, dtype=I32)
    pos = jnp.sum(jnp.where(is_expert, starts, 0), axis=-1) + rank
    n_tiles = (2 * T) // TILE_E + N_EXPERTS
    tile_index = jnp.arange(n_tiles, dtype=I32)
    tile_valid = (tile_index * TILE_E < ends[-1]).astype(I32)
    tile_expert = jnp.minimum(
        jnp.sum((ends[None, :] <= (tile_index * TILE_E)[:, None]).astype(I32), axis=1), N_EXPERTS - 1)

    xs = _scatter(starts, counts, pos, hn, n_tiles * TILE_E)
    y = _experts(tile_expert, tile_valid, xs, w_gate, w_up, w_down)
    out = _combine(pos, h, gate.T, y)
    return out.reshape(B, S, D)


def kernel(x, norm_mix, w_in, b_forget, q_norm_fox, k_norm_fox, q_norm_dil, k_norm_dil, out_norm_fox,
           out_norm_dil, w_out, norm_ffn, w_router_group, b_router_group, w_router_expert,
           b_router_expert, w_gate, w_up, w_down):
    h = x
    for l in range(norm_mix.shape[0]):
        h = _layer(h, norm_mix[l], w_in[l], b_forget[l], q_norm_fox[l], k_norm_fox[l], q_norm_dil[l],
                   k_norm_dil[l], out_norm_fox[l], out_norm_dil[l], w_out[l], norm_ffn[l],
                   w_router_group[l], b_router_group[l], w_router_expert[l], b_router_expert[l],
                   w_gate[l], w_up[l], w_down[l])
    return h
```

```python
import functools
import math

import jax
import jax.numpy as jnp
from jax import lax
from jax.experimental import pallas as pl
from jax.experimental.pallas import tpu as pltpu

F32 = jnp.float32
BF16 = jnp.bfloat16
I32 = jnp.int32

D_MODEL = 1024
HEAD_DIM = 64
N_HEADS = 8
D_GRP = N_HEADS * HEAD_DIM
LANES = 128
HEADS_PER_TILE = LANES // HEAD_DIM
N_PAIRS = D_GRP // LANES
DIL_PATTERNS = ((128, 1), (512, 4), (2048, 16))
BLOCK = 128
ROPE_THETA = 10000.0
N_GROUPS = 4
EXPERTS_PER_GROUP = 8
N_EXPERTS = N_GROUPS * EXPERTS_PER_GROUP
D_EXPERT = 512
EPS = 1e-6
NEG = -1e30
LOG2E = 1.4426950408889634

TM_IN = 512
TQ = 512
TK_WIDE = 1024
TK_SUB = 256
FOX_AHEAD = 4
DIL_SPAN = 2048
DIL_GROUP = 8
DIL_PRE = 4
TM_OUT = 512
TM_ROWS = 256
TILE_E = 512
ROUTER_ROWS = 8 + N_EXPERTS
ROW_CHUNKS = D_MODEL // LANES
DMA_UNROLL = 8
VMEM_LIMIT = 56 * 1024 * 1024


def _cparams(sem, flags=None):
    return pltpu.CompilerParams(dimension_semantics=sem, vmem_limit_bytes=VMEM_LIMIT, flags=flags)


def _inproj_kernel(x_ref, gmix_ref, w_ref, wvt_ref, wf_ref, bf_ref, gqa_ref, gka_ref, gqb_ref, gkb_ref,
                   bd_ref, cos_ref, sin_ref, tri_ref,
                   qa_ref, ka_ref, va_ref, qb_ref, kb_ref, vb_ref, carry_ref):
    @pl.when(pl.program_id(1) == 0)
    def _():
        carry_ref[...] = jnp.zeros_like(carry_ref)

    x = x_ref[0]
    ms = jnp.mean(x * x, axis=-1, keepdims=True)
    xn = (x * lax.rsqrt(ms + EPS) * gmix_ref[...]).astype(BF16)

    def seg(j):
        return jnp.dot(xn, w_ref[:, j * D_GRP:(j + 1) * D_GRP], preferred_element_type=F32)

    def head_norm(y, g_ref, scale):
        ss = jnp.dot((y * y).astype(BF16), bd_ref[...], preferred_element_type=F32) * (1.0 / HEAD_DIM)
        return y * lax.rsqrt(ss + EPS) * (g_ref[...] * scale)

    cos = cos_ref[...]
    sin = sin_ref[...]
    lane = lax.broadcasted_iota(I32, (x.shape[0], LANES), 1)
    first_half = (lane % HEAD_DIM) < (HEAD_DIM // 2)

    def rope(y):
        outs = []
        for j in range(N_PAIRS):
            ys = y[:, j * LANES:(j + 1) * LANES]
            partner = jnp.where(first_half, pltpu.roll(ys, LANES - HEAD_DIM // 2, 1),
                                pltpu.roll(ys, HEAD_DIM // 2, 1))
            outs.append(ys * cos + partner * sin)
        return jnp.concatenate(outs, axis=1)

    scale = 1.0 / math.sqrt(HEAD_DIM)
    va_ref[0] = lax.dot_general(wvt_ref[...], xn, (((1,), (1,)), ((), ())),
                                preferred_element_type=F32).astype(va_ref.dtype)
    qb_ref[0] = rope(head_norm(seg(3), gqb_ref, scale)).astype(qb_ref.dtype)
    kb_ref[0] = rope(head_norm(seg(4), gkb_ref, 1.0)).astype(kb_ref.dtype)
    vb_ref[0] = seg(5).astype(vb_ref.dtype)

    fa = jnp.dot(xn, wf_ref[...], preferred_element_type=F32) + bf_ref[...]
    logf = jnp.minimum(fa, 0.0) - jnp.log1p(jnp.exp(-jnp.abs(fa)))
    hi = logf.astype(BF16)
    mid = (logf - hi.astype(F32)).astype(BF16)
    lo = (logf - hi.astype(F32) - mid.astype(F32)).astype(BF16)
    parts = jnp.dot(tri_ref[...], jnp.concatenate([hi, mid, lo], axis=1), preferred_element_type=F32)
    c = parts[:, :LANES] + parts[:, LANES:2 * LANES] + parts[:, 2 * LANES:] + carry_ref[...]
    carry_ref[...] = c[c.shape[0] - 1:, :]

    qa = head_norm(seg(0), gqa_ref, scale * LOG2E)
    ka = head_norm(seg(1), gka_ref, 1.0)
    c2 = c * LOG2E
    ones = jnp.where((lane >= HEAD_DIM + 3) & (lane < HEAD_DIM + 6), 1.0, 0.0)
    for h in range(N_HEADS):
        cb = jnp.broadcast_to(c2[:, h:h + 1], (x.shape[0], LANES))
        hi = cb.astype(BF16).astype(F32)
        mid = (cb - hi).astype(BF16).astype(F32)
        lo = cb - hi - mid
        pieces = jnp.where(lane == HEAD_DIM, hi, jnp.where(lane == HEAD_DIM + 1, mid,
                           jnp.where(lane == HEAD_DIM + 2, lo, 0.0)))
        q_extra = pieces + ones
        k_extra = jnp.where((lane >= HEAD_DIM) & (lane < HEAD_DIM + 3), 1.0, 0.0) \
            - pltpu.roll(pieces, 3, 1)
        j, odd = divmod(h, HEADS_PER_TILE)
        qp = qa[:, j * LANES:(j + 1) * LANES]
        kp = ka[:, j * LANES:(j + 1) * LANES]
        if odd:
            qp = pltpu.roll(qp, HEAD_DIM, 1)
            kp = pltpu.roll(kp, HEAD_DIM, 1)
        qa_ref[0, h] = jnp.where(lane < HEAD_DIM, qp, q_extra).astype(qa_ref.dtype)
        ka_ref[0, h] = jnp.where(lane < HEAD_DIM, kp, k_extra).astype(ka_ref.dtype)


def _inproj(x, gmix, w_main, w_vt, w_f, b_f, gqa, gka, gqb, gkb, bd, cos_t, sin_t, tri):
    B, S, D = x.shape
    tm = TM_IN
    const = lambda shape: pl.BlockSpec(shape, lambda b, i: (0,) * len(shape))
    tok = lambda w, dt: jax.ShapeDtypeStruct((B, S, w), dt)
    tok_spec = lambda w: pl.BlockSpec((1, tm, w), lambda b, i: (b, i, 0))
    head_spec = pl.BlockSpec((1, N_HEADS, tm, LANES), lambda b, i: (b, 0, i, 0))
    head_shape = jax.ShapeDtypeStruct((B, N_HEADS, S, LANES), BF16)
    return pl.pallas_call(
        _inproj_kernel,
        grid=(B, S // tm),
        in_specs=[tok_spec(D), const((1, D)), const(w_main.shape), const(w_vt.shape), const(w_f.shape),
                  const((1, LANES)),
                  const((1, D_GRP)), const((1, D_GRP)), const((1, D_GRP)), const((1, D_GRP)),
                  const((D_GRP, D_GRP)),
                  pl.BlockSpec((tm, LANES), lambda b, i: (i, 0)),
                  pl.BlockSpec((tm, LANES), lambda b, i: (i, 0)),
                  const((tm, tm))],
        out_specs=[head_spec, head_spec, pl.BlockSpec((1, D_GRP, tm), lambda b, i: (b, 0, i))]
        + [tok_spec(D_GRP)] * 3,
        out_shape=[head_shape, head_shape, jax.ShapeDtypeStruct((B, D_GRP, S), BF16),
                   tok(D_GRP, F32), tok(D_GRP, F32), tok(D_GRP, F32)],
        scratch_shapes=[pltpu.VMEM((1, LANES), F32)],
        compiler_params=_cparams(("arbitrary", "arbitrary")),
        name="inproj",
    )(x, gmix, w_main, w_vt, w_f, b_f, gqa, gka, gqb, gkb, bd, cos_t, sin_t, tri)


def _fox_kernel(q_ref, k_ref, v_ref, o_ref):
    qi = pl.program_id(2)
    tq = q_ref.shape[2]

    def step(start, width, carry, diag):
        carry = list(carry)
        sub = min(TK_SUB, width)
        chunks = [(c, j) for c in range(width // sub) for j in range(HEADS_PER_TILE)]
        def score(c, j):
            k = k_ref[0, j, pl.ds(start + c * sub, sub), :]
            return lax.dot_general(k, q_ref[0, j], (((1,), (1,)), ((), ())), preferred_element_type=F32)

        scores = {cj: score(*cj) for cj in chunks[:FOX_AHEAD]}
        for n, (c, j) in enumerate(chunks):
                if n + FOX_AHEAD < len(chunks):
                    nxt = chunks[n + FOX_AHEAD]
                    scores[nxt] = score(*nxt)
                m, l, acc = carry[j]
                vt = v_ref[0, :, pl.ds(start + c * sub, sub)]
                s = scores.pop((c, j))
                if diag:
                    key = lax.broadcasted_iota(I32, (sub, tq), 0) + c * sub
                    qry = lax.broadcasted_iota(I32, (sub, tq), 1)
                    s = jnp.where(key <= qry, s, NEG)
                m_new = jnp.maximum(m, jnp.max(s, axis=0, keepdims=True))
                alpha = jnp.exp2(m - m_new)
                p = jnp.exp2(s - m_new)
                l = alpha * l + jnp.sum(p, axis=0, keepdims=True)
                acc = alpha * acc + jnp.dot(vt, p.astype(BF16), preferred_element_type=F32)
                carry[j] = (m_new, l, acc)
        return tuple(carry)

    init = tuple((jnp.full((1, tq), NEG, F32), jnp.zeros((1, tq), F32), jnp.zeros((LANES, tq), F32))
                 for _ in range(HEADS_PER_TILE))
    per_wide = TK_WIDE // tq
    n_wide = qi // per_wide
    carry = lax.fori_loop(
        0, n_wide, lambda i, c: step(pl.multiple_of(i * TK_WIDE, TK_WIDE), TK_WIDE, c, False), init)
    for extra in range(per_wide - 1):
        carry = lax.cond(qi - n_wide * per_wide > extra,
                         lambda c, e=extra: step(pl.multiple_of((n_wide * per_wide + e) * tq, tq), tq, c, False),
                         lambda c: c, carry)
    carry = step(pl.multiple_of(qi * tq, tq), tq, carry, True)
    outs = [acc / l for (_, l, acc) in carry]
    feat = lax.broadcasted_iota(I32, (LANES, tq), 0)
    o_ref[0] = jnp.where(feat < HEAD_DIM, outs[0], outs[1]).T.astype(o_ref.dtype)


def _fox(qa, ka, va_t):
    B, _, S, _ = qa.shape
    return pl.pallas_call(
        _fox_kernel,
        grid=(B, N_PAIRS, S // TQ),
        in_specs=[pl.BlockSpec((1, HEADS_PER_TILE, TQ, LANES), lambda b, hp, i: (b, hp, i, 0)),
                  pl.BlockSpec((1, HEADS_PER_TILE, S, LANES), lambda b, hp, i: (b, hp, 0, 0)),
                  pl.BlockSpec((1, LANES, S), lambda b, hp, i: (b, hp, 0))],
        out_specs=pl.BlockSpec((1, TQ, LANES), lambda b, hp, i: (b, i, hp)),
        out_shape=jax.ShapeDtypeStruct((B, S, D_GRP), F32),
        compiler_params=_cparams(("arbitrary", "arbitrary", "arbitrary")),
        name="fox",
    )(qa, ka, va_t)


def _dilated_kernel(q_ref, kp_ref, kc_ref, vp_ref, vc_ref, o_ref, qq, kk, vv, qq4, kk4, vv4, osc, lsc):
    u = pl.program_id(1)
    span = q_ref.shape[1]
    qq[...] = q_ref[0]
    kk[0:span, :] = kp_ref[0]
    kk[span:2 * span, :] = kc_ref[0]
    vv[0:span, :] = vp_ref[0]
    vv[span:2 * span, :] = vc_ref[0]
    for src, dst in ((qq, qq4), (kk, kk4), (vv, vv4)):
        part = src.shape[0] // DIL_PRE
        for a in range(DIL_PRE):
            dst[a * part:(a + 1) * part, :] = src[pl.ds(a, part, stride=DIL_PRE), :]

    def rows(buf, buf4, start, n, d):
        if d % DIL_PRE:
            return buf[pl.ds(start, n, stride=d), :]
        part = buf4.shape[0] // DIL_PRE
        a = lax.rem(start, DIL_PRE)
        return buf4[pl.ds(a * part + lax.div(start, DIL_PRE), n, stride=d // DIL_PRE), :]

    lane = lax.broadcasted_iota(I32, (BLOCK, LANES), 1)
    ql = lax.broadcasted_iota(I32, (BLOCK, 2 * BLOCK), 0)
    kl = lax.broadcasted_iota(I32, (BLOCK, 2 * BLOCK), 1)
    dist = ql + BLOCK - kl
    band = (dist >= 0) & (dist <= BLOCK)
    bias = jnp.where(band, 0.0, NEG)
    bias_first = jnp.where(band & (kl >= BLOCK), 0.0, NEG)

    def scores(q_start, k_start, d, first):
        qs = rows(qq, qq4, q_start, BLOCK, d).astype(BF16)
        ks = rows(kk, kk4, k_start, 2 * BLOCK, d).astype(BF16)
        mask = jnp.where(first, bias_first, bias)
        out = []
        for j in range(HEADS_PER_TILE):
            qj = jnp.where(lane // HEAD_DIM == j, qs, jnp.zeros_like(qs))
            out.append(lax.dot_general(qj, ks, (((1,), (1,)), ((), ())), preferred_element_type=F32) + mask)
        return out

    def finish(s_heads, k_start, d):
        vs = rows(vv, vv4, k_start, 2 * BLOCK, d).astype(BF16)
        o_heads, lse_heads = [], []
        for s in s_heads:
            m = jnp.max(s, axis=-1, keepdims=True)
            p = jnp.exp(s - m)
            l = jnp.sum(p, axis=-1, keepdims=True)
            o_heads.append(jnp.dot((p / l).astype(BF16), vs, preferred_element_type=F32))
            lse_heads.append(m + jnp.log(l))
        o = jnp.where(lane < HEAD_DIM, o_heads[0], o_heads[1])
        lse = jnp.where(lane < HEAD_DIM, lse_heads[0], lse_heads[1])
        return o, lse

    for pidx, (window, d) in enumerate(DIL_PATTERNS):
        assert window // d == BLOCK
        unit = d * BLOCK
        n_problems = (span // unit) * d
        assert n_problems % DIL_GROUP == 0

        def body(g, _, pidx=pidx, d=d, unit=unit):
            starts, s_all = [], []
            for t in range(DIL_GROUP):
                idx = g * DIL_GROUP + t
                w = idx // d
                q_start = w * unit + (idx - w * d)
                k_start = span - unit + q_start
                starts.append((q_start, k_start))
                s_all.append(scores(q_start, k_start, d, jnp.logical_and(u == 0, w == 0)))
            for (q_start, k_start), s_heads in zip(starts, s_all):
                o, lse = finish(s_heads, k_start, d)
                osc[pidx, pl.ds(q_start, BLOCK, stride=d), :] = o
                lsc[pidx, pl.ds(q_start, BLOCK, stride=d), :] = lse
            return 0

        lax.fori_loop(0, n_problems // DIL_GROUP, body, 0)

    mx = jnp.maximum(jnp.maximum(lsc[0], lsc[1]), lsc[2])
    num = jnp.zeros((span, LANES), F32)
    den = jnp.zeros((span, LANES), F32)
    for pidx in range(len(DIL_PATTERNS)):
        e = jnp.exp(lsc[pidx] - mx)
        num = num + e * osc[pidx]
        den = den + e
    o_ref[0] = (num / den).astype(o_ref.dtype)


def _dilated(qb, kb, vb):
    B, S, _ = qb.shape
    span = DIL_SPAN
    cur = pl.BlockSpec((1, span, LANES), lambda b, u, hp: (b, u, hp))
    prev = pl.BlockSpec((1, span, LANES), lambda b, u, hp: (b, jnp.maximum(u - 1, 0), hp))
    return pl.pallas_call(
        _dilated_kernel,
        grid=(B, S // span, N_PAIRS),
        in_specs=[cur, prev, cur, prev, cur],
        out_specs=cur,
        out_shape=jax.ShapeDtypeStruct((B, S, D_GRP), F32),
        scratch_shapes=[pltpu.VMEM((span, LANES), F32),
                        pltpu.VMEM((2 * span, LANES), F32), pltpu.VMEM((2 * span, LANES), F32),
                        pltpu.VMEM((span, LANES), F32),
                        pltpu.VMEM((2 * span, LANES), F32), pltpu.VMEM((2 * span, LANES), F32),
                        pltpu.VMEM((len(DIL_PATTERNS), span, LANES), F32),
                        pltpu.VMEM((len(DIL_PATTERNS), span, LANES), F32)],
        compiler_params=_cparams(("arbitrary", "arbitrary", "arbitrary")),
        name="dilated",
    )(qb, kb, kb, vb, vb)


def _store_row_tiles(ref, x):
    n = x.shape[0]
    for c in range(ROW_CHUNKS):
        ref[pl.ds(c, n, stride=ROW_CHUNKS), :] = x[:, c * LANES:(c + 1) * LANES]


def _load_row_tiles(ref, n):
    return jnp.concatenate([ref[pl.ds(c, n, stride=ROW_CHUNKS), :] for c in range(ROW_CHUNKS)], axis=1)


def _row_tile_copy(src_ref, src_row, dst_ref, dst_row, sem):
    src = src_ref.at[pl.ds(pl.multiple_of(src_row * ROW_CHUNKS, ROW_CHUNKS), ROW_CHUNKS), :]
    dst = dst_ref.at[pl.ds(pl.multiple_of(dst_row * ROW_CHUNKS, ROW_CHUNKS), ROW_CHUNKS), :]
    return pltpu.make_async_copy(src, dst, sem)


def _outproj_kernel(oa_ref, ob_ref, x_ref, gfox_ref, gdil_ref, wo_ref, gffn_ref, wr_ref, br_ref, upper_ref,
                    h_ref, hn_ref, eid_ref, gate_ref, rank_ref, cnt_ref, run_ref):
    @pl.when(pl.program_id(0) == 0)
    def _():
        run_ref[...] = jnp.zeros_like(run_ref)

    def norm(y, g):
        ms = jnp.mean(y * y, axis=-1, keepdims=True)
        return y * lax.rsqrt(ms + EPS) * g

    a = norm(oa_ref[...], gfox_ref[...]).astype(BF16)
    b = norm(ob_ref[...], gdil_ref[...]).astype(BF16)
    mix = (jnp.dot(a, wo_ref[0:D_GRP, :], preferred_element_type=F32)
           + jnp.dot(b, wo_ref[D_GRP:2 * D_GRP, :], preferred_element_type=F32))
    h = x_ref[...] + mix
    h_ref[...] = h
    hn = norm(h, gffn_ref[...])
    _store_row_tiles(hn_ref, hn)

    z = lax.dot_general(wr_ref[...], hn, (((1,), (1,)), ((), ())), preferred_element_type=F32,
                        precision=lax.Precision.HIGHEST) + br_ref[...]
    tm = z.shape[1]
    best = z[0:1, :]
    g_sel = jnp.zeros((1, tm), I32)
    for g in range(1, N_GROUPS):
        better = z[g:g + 1, :] > best
        g_sel = jnp.where(better, g, g_sel)
        best = jnp.maximum(best, z[g:g + 1, :])
    den = jnp.zeros((1, tm), F32)
    for g in range(N_GROUPS):
        den = den + jnp.exp(z[g:g + 1, :] - best)
    pg_top = 1.0 / den

    ze = jnp.zeros((EXPERTS_PER_GROUP, tm), F32)
    for g in range(N_GROUPS):
        ze = jnp.where(g_sel == g, z[8 + g * EXPERTS_PER_GROUP:8 + (g + 1) * EXPERTS_PER_GROUP, :], ze)
    e_iota = lax.broadcasted_iota(I32, ze.shape, 0)
    v1 = jnp.max(ze, axis=0, keepdims=True)
    i1 = jnp.min(jnp.where(ze == v1, e_iota, EXPERTS_PER_GROUP), axis=0, keepdims=True)
    ze2 = jnp.where(e_iota == i1, -jnp.inf, ze)
    v2 = jnp.max(ze2, axis=0, keepdims=True)
    i2 = jnp.min(jnp.where(ze2 == v2, e_iota, EXPERTS_PER_GROUP), axis=0, keepdims=True)
    e2 = jnp.exp(v2 - v1)
    inv = 1.0 / (1.0 + e2)
    gate1 = inv * pg_top
    gate2 = e2 * inv * pg_top
    eid1 = g_sel * EXPERTS_PER_GROUP + i1
    eid2 = g_sel * EXPERTS_PER_GROUP + i2

    x_iota = lax.broadcasted_iota(I32, (N_EXPERTS, tm), 0)
    hot1 = x_iota == eid1
    hot2 = x_iota == eid2
    multi = jnp.logical_or(hot1, hot2)
    before = jnp.dot(multi.astype(BF16), upper_ref[...], preferred_element_type=F32)
    slot = before + run_ref[:, 0:1]
    rank1 = jnp.sum(jnp.where(hot1, slot, 0.0), axis=0, keepdims=True)
    rank2 = jnp.sum(jnp.where(hot2, slot, 0.0), axis=0, keepdims=True)
    run_ref[...] = run_ref[...] + jnp.sum(multi.astype(F32), axis=1, keepdims=True)

    eid_ref[...] = jnp.concatenate([eid1, eid2], axis=0)
    gate_ref[...] = jnp.concatenate([gate1, gate2], axis=0)
    rank_ref[...] = jnp.concatenate([rank1, rank2], axis=0).astype(I32)
    cnt_ref[...] = run_ref[...].astype(I32)


def _outproj(oa, ob, x2, gfox, gdil, w_out, gffn, w_r, b_r, upper):
    T, D = x2.shape
    tm = TM_OUT
    const = lambda shape: pl.BlockSpec(shape, lambda i: (0,) * len(shape))
    tok = lambda w: pl.BlockSpec((tm, w), lambda i: (i, 0))
    lanes2 = pl.BlockSpec((2, tm), lambda i: (0, i))
    return pl.pallas_call(
        _outproj_kernel,
        grid=(T // tm,),
        in_specs=[tok(D_GRP), tok(D_GRP), tok(D), const((1, D_GRP)), const((1, D_GRP)), const((D, D)),
                  const((1, D)), const((ROUTER_ROWS, D)), const((ROUTER_ROWS, 1)), const((tm, tm))],
        out_specs=[tok(D), pl.BlockSpec((tm * ROW_CHUNKS, LANES), lambda i: (i, 0)),
                   lanes2, lanes2, lanes2, const((N_EXPERTS, LANES))],
        out_shape=[jax.ShapeDtypeStruct((T, D), F32), jax.ShapeDtypeStruct((T * ROW_CHUNKS, LANES), F32),
                   jax.ShapeDtypeStruct((2, T), I32), jax.ShapeDtypeStruct((2, T), F32),
                   jax.ShapeDtypeStruct((2, T), I32), jax.ShapeDtypeStruct((N_EXPERTS, LANES), I32)],
        scratch_shapes=[pltpu.VMEM((N_EXPERTS, LANES), F32)],
        compiler_params=_cparams(("arbitrary",)),
        name="outproj",
    )(oa, ob, x2, gfox, gdil, w_out, gffn, w_r, b_r, upper)


def _scatter_kernel(starts_ref, cnt_ref, pos_ref, hn_ref, xs_ref, ring, zero_tile, sems, zero_sem, *, n_steps):
    i = pl.program_id(0)
    tm = hn_ref.shape[0] // ROW_CHUNKS
    slot = lax.rem(i, 2)

    def wait_slot(s):
        for _ in range(2):
            pltpu.make_async_copy(ring.at[s], xs_ref.at[pl.ds(0, tm * ROW_CHUNKS), :], sems.at[s]).wait()

    @pl.when(i >= 2)
    def _():
        wait_slot(slot)

    ring[slot] = hn_ref[...]

    def start(r, _):
        for k in range(2):
            _row_tile_copy(ring.at[slot], r, xs_ref, pos_ref[k, r], sems.at[slot]).start()
        return 0

    lax.fori_loop(0, tm, start, 0, unroll=DMA_UNROLL)

    @pl.when(i == n_steps - 1)
    def _():
        zero_tile[...] = jnp.zeros_like(zero_tile)

        def pad_expert(e, _, wait):
            n_pad = lax.rem(TILE_E - lax.rem(cnt_ref[e], TILE_E), TILE_E)
            first = starts_ref[e] + cnt_ref[e]

            def zero_copy(r, c):
                copy = _row_tile_copy(zero_tile, 0, xs_ref, first + r, zero_sem)
                copy.wait() if wait else copy.start()
                return c

            return lax.fori_loop(0, n_pad, zero_copy, 0)

        lax.fori_loop(0, N_EXPERTS, functools.partial(pad_expert, wait=False), 0)
        lax.fori_loop(0, N_EXPERTS, functools.partial(pad_expert, wait=True), 0)
        wait_slot(slot)
        if n_steps > 1:
            wait_slot(1 - slot)
        last = N_EXPERTS - 1
        used_rows = starts_ref[last] + cnt_ref[last] + lax.rem(TILE_E - lax.rem(cnt_ref[last], TILE_E), TILE_E)
        ring[0] = jnp.zeros((tm * ROW_CHUNKS, LANES), F32)
        n_tail = xs_ref.shape[0] // (tm * ROW_CHUNKS) - used_rows // tm

        def tail_copy(t):
            row0 = pl.multiple_of((used_rows + t * tm) * ROW_CHUNKS, tm * ROW_CHUNKS)
            return pltpu.make_async_copy(ring.at[0], xs_ref.at[pl.ds(row0, tm * ROW_CHUNKS), :], zero_sem)

        lax.fori_loop(0, n_tail, lambda t, c: (tail_copy(t).start(), c)[1], 0)
        lax.fori_loop(0, n_tail, lambda t, c: (tail_copy(t).wait(), c)[1], 0)


def _scatter(starts, cnt, pos, hn, n_rows):
    T = hn.shape[0] // ROW_CHUNKS
    tm = TM_ROWS
    assert TILE_E % tm == 0
    n_steps = T // tm
    grid_spec = pltpu.PrefetchScalarGridSpec(
        num_scalar_prefetch=2,
        grid=(n_steps,),
        in_specs=[pl.BlockSpec((2, tm), lambda i, starts, cnt: (0, i), memory_space=pltpu.SMEM),
                  pl.BlockSpec((tm * ROW_CHUNKS, LANES), lambda i, starts, cnt: (i, 0))],
        out_specs=pl.BlockSpec(memory_space=pl.ANY),
        scratch_shapes=[pltpu.VMEM((2, tm * ROW_CHUNKS, LANES), F32), pltpu.VMEM((ROW_CHUNKS, LANES), F32),
                        pltpu.SemaphoreType.DMA((2,)), pltpu.SemaphoreType.DMA(())],
    )
    return pl.pallas_call(
        functools.partial(_scatter_kernel, n_steps=n_steps),
        grid_spec=grid_spec,
        out_shape=jax.ShapeDtypeStruct((n_rows * ROW_CHUNKS, LANES), F32),
        compiler_params=_cparams(("arbitrary",)),
        name="scatter_rows",
    )(starts, cnt, pos, hn)


def _experts_kernel(te_ref, tv_ref, xs_ref, wg_ref, wu_ref, wd_ref, y_ref, wg_bf, wu_bf, wd_bf):
    i = pl.program_id(0)
    new_expert = jnp.logical_or(i == 0, te_ref[i] != te_ref[jnp.maximum(i - 1, 0)])

    @pl.when(jnp.logical_and(tv_ref[i] != 0, new_expert))
    def _():
        wg_bf[...] = wg_ref[0].astype(BF16)
        wu_bf[...] = wu_ref[0].astype(BF16)
        wd_bf[...] = wd_ref[0].astype(BF16)

    @pl.when(tv_ref[i] != 0)
    def _():
        x = _load_row_tiles(xs_ref, TILE_E).astype(BF16)
        g = jnp.dot(x, wg_bf[...], preferred_element_type=F32)
        up = jnp.dot(x, wu_bf[...], preferred_element_type=F32)
        hmid = (g * jax.nn.sigmoid(g) * up).astype(BF16)
        _store_row_tiles(y_ref, jnp.dot(hmid, wd_bf[...], preferred_element_type=F32))

    @pl.when(tv_ref[i] == 0)
    def _():
        y_ref[...] = jnp.zeros_like(y_ref)


def _experts(tile_expert, tile_valid, xs, w_gate, w_up, w_down):
    n_tiles = tile_expert.shape[0]
    D = D_MODEL
    rows_spec = pl.BlockSpec((TILE_E * ROW_CHUNKS, LANES), lambda i, te, tv: (i, 0))
    grid_spec = pltpu.PrefetchScalarGridSpec(
        num_scalar_prefetch=2,
        grid=(n_tiles,),
        in_specs=[rows_spec,
                  pl.BlockSpec((1, D, D_EXPERT), lambda i, te, tv: (te[i], 0, 0)),
                  pl.BlockSpec((1, D, D_EXPERT), lambda i, te, tv: (te[i], 0, 0)),
                  pl.BlockSpec((1, D_EXPERT, D), lambda i, te, tv: (te[i], 0, 0))],
        out_specs=rows_spec,
        scratch_shapes=[pltpu.VMEM((D, D_EXPERT), BF16), pltpu.VMEM((D, D_EXPERT), BF16),
                        pltpu.VMEM((D_EXPERT, D), BF16)],
    )
    return pl.pallas_call(
        _experts_kernel,
        grid_spec=grid_spec,
        out_shape=jax.ShapeDtypeStruct(xs.shape, F32),
        compiler_params=_cparams(("arbitrary",)),
        name="experts",
    )(tile_expert, tile_valid, xs, w_gate, w_up, w_down)


def _combine_kernel(pos_ref, pos_next_ref, h_ref, gate_ref, y_ref, o_ref, ybuf, sems, *, n_steps):
    i = pl.program_id(0)
    tm = h_ref.shape[0]
    slot = lax.rem(i, 2)

    def gather(p_ref, s):
        def start(r, _):
            for k in range(2):
                _row_tile_copy(y_ref, p_ref[k, r], ybuf.at[s, k], r, sems.at[s]).start()
            return 0

        lax.fori_loop(0, tm, start, 0, unroll=DMA_UNROLL)

    @pl.when(i == 0)
    def _():
        gather(pos_ref, slot)

    @pl.when(i + 1 < n_steps)
    def _():
        gather(pos_next_ref, 1 - slot)

    for k in range(2):
        pltpu.make_async_copy(y_ref.at[pl.ds(0, tm * ROW_CHUNKS), :], ybuf.at[slot, k], sems.at[slot]).wait()
    g = gate_ref[...]
    o_ref[...] = (h_ref[...] + g[:, 0:1] * _load_row_tiles(ybuf.at[slot, 0], tm)
                  + g[:, 1:2] * _load_row_tiles(ybuf.at[slot, 1], tm))


def _combine(pos, h, gate_t, y):
    T, D = h.shape
    tm = TM_ROWS
    n_steps = T // tm
    return pl.pallas_call(
        functools.partial(_combine_kernel, n_steps=n_steps),
        grid=(n_steps,),
        in_specs=[pl.BlockSpec((2, tm), lambda i: (0, i), memory_space=pltpu.SMEM),
                  pl.BlockSpec((2, tm), lambda i: (0, jnp.minimum(i + 1, n_steps - 1)), memory_space=pltpu.SMEM),
                  pl.BlockSpec((tm, D), lambda i: (i, 0)),
                  pl.BlockSpec((tm, 2), lambda i: (i, 0)),
                  pl.BlockSpec(memory_space=pl.ANY)],
        out_specs=pl.BlockSpec((tm, D), lambda i: (i, 0)),
        out_shape=jax.ShapeDtypeStruct((T, D), F32),
        scratch_shapes=[pltpu.VMEM((2, 2, tm * ROW_CHUNKS, LANES), F32), pltpu.SemaphoreType.DMA((2,))],
        compiler_params=_cparams(("arbitrary",)),
        name="combine",
    )(pos, pos, h, gate_t, y)


def _rope_tables(S):
    inv_freq = 1.0 / (ROPE_THETA ** (jnp.arange(0, HEAD_DIM, 2, dtype=F32) / HEAD_DIM))
    ang = jnp.arange(S, dtype=F32)[:, None] * inv_freq[None, :]
    cos, sin = jnp.cos(ang), jnp.sin(ang)
    cos_t = jnp.tile(cos, (1, 2 * HEADS_PER_TILE))
    sin_t = jnp.tile(jnp.concatenate([-sin, sin], axis=1), (1, HEADS_PER_TILE))
    return cos_t, sin_t


def _layer(x, norm_mix, w_in, b_forget, q_norm_fox, k_norm_fox, q_norm_dil, k_norm_dil,
           out_norm_fox, out_norm_dil, w_out, norm_ffn, w_router_group, b_router_group,
           w_router_expert, b_router_expert, w_gate, w_up, w_down):
    B, S, D = x.shape
    T = B * S
    n_main = 6 * D_GRP

    w_main = w_in[:, :n_main].astype(BF16)
    w_vt = w_in[:, 2 * D_GRP:3 * D_GRP].T.astype(BF16)
    w_f = jnp.pad(w_in[:, n_main:], ((0, 0), (0, LANES - N_HEADS))).astype(BF16)
    b_f = jnp.pad(b_forget, (0, LANES - N_HEADS))[None, :]
    per_head = lambda g: jnp.tile(g, N_HEADS)[None, :]
    bd = jnp.kron(jnp.eye(N_HEADS, dtype=F32), jnp.ones((HEAD_DIM, HEAD_DIM), F32)).astype(BF16)
    cos_t, sin_t = _rope_tables(S)
    tri = jnp.tril(jnp.ones((TM_IN, TM_IN), F32)).astype(BF16)
    upper = jnp.triu(jnp.ones((TM_OUT, TM_OUT), F32), k=1).astype(BF16)
    w_r = jnp.concatenate([
        jnp.pad(w_router_group.T, ((0, 8 - N_GROUPS), (0, 0))),
        w_router_expert.transpose(0, 2, 1).reshape(N_EXPERTS, D)], axis=0)
    b_r = jnp.concatenate([jnp.pad(b_router_group, (0, 8 - N_GROUPS)), b_router_expert.reshape(-1)])[:, None]

    qa, ka, va, qb, kb, vb = _inproj(
        x, norm_mix[None, :], w_main, w_vt, w_f, b_f, per_head(q_norm_fox), per_head(k_norm_fox),
        per_head(q_norm_dil), per_head(k_norm_dil), bd, cos_t, sin_t, tri)
    oa = _fox(qa, ka, va)
    ob = _dilated(qb, kb, vb)

    h, hn, eid, gate, rank, cnt = _outproj(
        oa.reshape(T, D_GRP), ob.reshape(T, D_GRP), x.reshape(T, D), out_norm_fox[None, :],
        out_norm_dil[None, :], w_out.astype(BF16), norm_ffn[None, :], w_r, b_r, upper)

    counts = cnt[:, 0]
    padded = ((counts + TILE_E - 1) // TILE_E) * TILE_E
    ends = jnp.cumsum(padded)
    starts = ends - padded
    is_expert = eid[:, :, None] == jnp.arange(N_EXPERTS, dtype=I32)
    pos = jnp.sum(jnp.where(is_expert, starts, 0), axis=-1) + rank
    n_tiles = (2 * T) // TILE_E + N_EXPERTS
    tile_index = jnp.arange(n_tiles, dtype=I32)
    tile_valid = (tile_index * TILE_E < ends[-1]).astype(I32)
    tile_expert = jnp.minimum(
        jnp.sum((ends[None, :] <= (tile_index * TILE_E)[:, None]).astype(I32), axis=1), N_EXPERTS - 1)

    xs = _scatter(starts, counts, pos, hn, n_tiles * TILE_E)
    y = _experts(tile_expert, tile_valid, xs, w_gate, w_up, w_down)
    out = _combine(pos, h, gate.T, y)
    return out.reshape(B, S, D)


def kernel(x, norm_mix, w_in, b_forget, q_norm_fox, k_norm_fox, q_norm_dil, k_norm_dil, out_norm_fox,
           out_norm_dil, w_out, norm_ffn, w_router_group, b_router_group, w_router_expert,
           b_router_expert, w_gate, w_up, w_down):
    h = x
    for l in range(norm_mix.shape[0]):
        h = _layer(h, norm_mix[l], w_in[l], b_forget[l], q_norm_fox[l], k_norm_fox[l], q_norm_dil[l],
                   k_norm_dil[l], out_norm_fox[l], out_norm_dil[l], w_out[l], norm_ffn[l],
                   w_router_group[l], b_router_group[l], w_router_expert[l], b_router_expert[l],
                   w_gate[l], w_up[l], w_down[l])
    return h
```

```python
import functools
import math

import jax
import jax.numpy as jnp
from jax import lax
from jax.experimental import pallas as pl
from jax.experimental.pallas import tpu as pltpu

F32 = jnp.float32
BF16 = jnp.bfloat16
I32 = jnp.int32

D_MODEL = 1024
HEAD_DIM = 64
N_HEADS = 8
D_GRP = N_HEADS * HEAD_DIM
LANES = 128
HEADS_PER_TILE = LANES // HEAD_DIM
N_PAIRS = D_GRP // LANES
DIL_PATTERNS = ((128, 1), (512, 4), (2048, 16))
BLOCK = 128
ROPE_THETA = 10000.0
N_GROUPS = 4
EXPERTS_PER_GROUP = 8
N_EXPERTS = N_GROUPS * EXPERTS_PER_GROUP
D_EXPERT = 512
EPS = 1e-6
NEG = -1e30
LOG2E = 1.4426950408889634

TM_IN = 512
TQ = 512
TK_WIDE = 1024
TK_SUB = 256
FOX_AHEAD = 4
DIL_SPAN = 2048
DIL_GROUP = 8
DIL_PRE = 4
TM_OUT = 512
TM_ROWS = 256
TILE_E = 512
ROUTER_ROWS = 8 + N_EXPERTS
ROW_CHUNKS = D_MODEL // LANES
DMA_UNROLL = 8
VMEM_LIMIT = 56 * 1024 * 1024


def _cparams(sem, flags=None):
    return pltpu.CompilerParams(dimension_semantics=sem, vmem_limit_bytes=VMEM_LIMIT, flags=flags)


def _inproj_kernel(x_ref, gmix_ref, w_ref, wvt_ref, wf_ref, bf_ref, gqa_ref, gka_ref, gqb_ref, gkb_ref,
                   bd_ref, cos_ref, sin_ref, tri_ref,
                   qa_ref, ka_ref, va_ref, qb_ref, kb_ref, vb_ref, carry_ref):
    @pl.when(pl.program_id(1) == 0)
    def _():
        carry_ref[...] = jnp.zeros_like(carry_ref)

    x = x_ref[0]
    ms = jnp.mean(x * x, axis=-1, keepdims=True)
    xn = (x * lax.rsqrt(ms + EPS) * gmix_ref[...]).astype(BF16)

    def seg(j):
        return jnp.dot(xn, w_ref[:, j * D_GRP:(j + 1) * D_GRP], preferred_element_type=F32)

    def head_norm(y, g_ref, scale):
        ss = jnp.dot((y * y).astype(BF16), bd_ref[...], preferred_element_type=F32) * (1.0 / HEAD_DIM)
        return y * lax.rsqrt(ss + EPS) * (g_ref[...] * scale)

    cos = cos_ref[...]
    sin = sin_ref[...]
    lane = lax.broadcasted_iota(I32, (x.shape[0], LANES), 1)
    first_half = (lane % HEAD_DIM) < (HEAD_DIM // 2)

    def rope(y):
        outs = []
        for j in range(N_PAIRS):
            ys = y[:, j * LANES:(j + 1) * LANES]
            partner = jnp.where(first_half, pltpu.roll(ys, LANES - HEAD_DIM // 2, 1),
                                pltpu.roll(ys, HEAD_DIM // 2, 1))
            outs.append(ys * cos + partner * sin)
        return jnp.concatenate(outs, axis=1)

    scale = 1.0 / math.sqrt(HEAD_DIM)
    va_ref[0] = lax.dot_general(wvt_ref[...], xn, (((1,), (1,)), ((), ())),
                                preferred_element_type=F32).astype(va_ref.dtype)
    qb_ref[0] = rope(head_norm(seg(3), gqb_ref, scale)).astype(qb_ref.dtype)
    kb_ref[0] = rope(head_norm(seg(4), gkb_ref, 1.0)).astype(kb_ref.dtype)
    vb_ref[0] = seg(5).astype(vb_ref.dtype)

    fa = jnp.dot(xn, wf_ref[...], preferred_element_type=F32) + bf_ref[...]
    logf = jnp.minimum(fa, 0.0) - jnp.log1p(jnp.exp(-jnp.abs(fa)))
    hi = logf.astype(BF16)
    mid = (logf - hi.astype(F32)).astype(BF16)
    lo = (logf - hi.astype(F32) - mid.astype(F32)).astype(BF16)
    parts = jnp.dot(tri_ref[...], jnp.concatenate([hi, mid, lo], axis=1), preferred_element_type=F32)
    c = parts[:, :LANES] + parts[:, LANES:2 * LANES] + parts[:, 2 * LANES:] + carry_ref[...]
    carry_ref[...] = c[c.shape[0] - 1:, :]

    qa = head_norm(seg(0), gqa_ref, scale * LOG2E)
    ka = head_norm(seg(1), gka_ref, 1.0)
    c2 = c * LOG2E
    ones = jnp.where((lane >= HEAD_DIM + 3) & (lane < HEAD_DIM + 6), 1.0, 0.0)
    for h in range(N_HEADS):
        cb = jnp.broadcast_to(c2[:, h:h + 1], (x.shape[0], LANES))
        hi = cb.astype(BF16).astype(F32)
        mid = (cb - hi).astype(BF16).astype(F32)
        lo = cb - hi - mid
        pieces = jnp.where(lane == HEAD_DIM, hi, jnp.where(lane == HEAD_DIM + 1, mid,
                           jnp.where(lane == HEAD_DIM + 2, lo, 0.0)))
        q_extra = pieces + ones
        k_extra = jnp.where((lane >= HEAD_DIM) & (lane < HEAD_DIM + 3), 1.0, 0.0) \
            - pltpu.roll(pieces, 3, 1)
        j, odd = divmod(h, HEADS_PER_TILE)
        qp = qa[:, j * LANES:(j + 1) * LANES]
        kp = ka[:, j * LANES:(j + 1) * LANES]
        if odd:
            qp = pltpu.roll(qp, HEAD_DIM, 1)
            kp = pltpu.roll(kp, HEAD_DIM, 1)
        qa_ref[0, h] = jnp.where(lane < HEAD_DIM, qp, q_extra).astype(qa_ref.dtype)
        ka_ref[0, h] = jnp.where(lane < HEAD_DIM, kp, k_extra).astype(ka_ref.dtype)


def _inproj(x, gmix, w_main, w_vt, w_f, b_f, gqa, gka, gqb, gkb, bd, cos_t, sin_t, tri):
    B, S, D = x.shape
    tm = TM_IN
    const = lambda shape: pl.BlockSpec(shape, lambda b, i: (0,) * len(shape))
    tok = lambda w, dt: jax.ShapeDtypeStruct((B, S, w), dt)
    tok_spec = lambda w: pl.BlockSpec((1, tm, w), lambda b, i: (b, i, 0))
    head_spec = pl.BlockSpec((1, N_HEADS, tm, LANES), lambda b, i: (b, 0, i, 0))
    head_shape = jax.ShapeDtypeStruct((B, N_HEADS, S, LANES), BF16)
    return pl.pallas_call(
        _inproj_kernel,
        grid=(B, S // tm),
        in_specs=[tok_spec(D), const((1, D)), const(w_main.shape), const(w_vt.shape), const(w_f.shape),
                  const((1, LANES)),
                  const((1, D_GRP)), const((1, D_GRP)), const((1, D_GRP)), const((1, D_GRP)),
                  const((D_GRP, D_GRP)),
                  pl.BlockSpec((tm, LANES), lambda b, i: (i, 0)),
                  pl.BlockSpec((tm, LANES), lambda b, i: (i, 0)),
                  const((tm, tm))],
        out_specs=[head_spec, head_spec, pl.BlockSpec((1, D_GRP, tm), lambda b, i: (b, 0, i))]
        + [tok_spec(D_GRP)] * 3,
        out_shape=[head_shape, head_shape, jax.ShapeDtypeStruct((B, D_GRP, S), BF16),
                   tok(D_GRP, F32), tok(D_GRP, F32), tok(D_GRP, F32)],
        scratch_shapes=[pltpu.VMEM((1, LANES), F32)],
        compiler_params=_cparams(("arbitrary", "arbitrary")),
        name="inproj",
    )(x, gmix, w_main, w_vt, w_f, b_f, gqa, gka, gqb, gkb, bd, cos_t, sin_t, tri)


def _fox_kernel(q_ref, k_ref, v_ref, o_ref):
    qi = pl.program_id(2)
    tq = q_ref.shape[2]

    def step(start, width, carry, diag):
        carry = list(carry)
        sub = min(TK_SUB, width)
        chunks = [(c, j) for c in range(width // sub) for j in range(HEADS_PER_TILE)]
        def score(c, j):
            k = k_ref[0, j, pl.ds(start + c * sub, sub), :]
            return lax.dot_general(k, q_ref[0, j], (((1,), (1,)), ((), ())), preferred_element_type=F32)

        scores = {cj: score(*cj) for cj in chunks[:FOX_AHEAD]}
        for n, (c, j) in enumerate(chunks):
                if n + FOX_AHEAD < len(chunks):
                    nxt = chunks[n + FOX_AHEAD]
                    scores[nxt] = score(*nxt)
                m, l, acc = carry[j]
                vt = v_ref[0, :, pl.ds(start + c * sub, sub)]
                s = scores.pop((c, j))
                if diag:
                    key = lax.broadcasted_iota(I32, (sub, tq), 0) + c * sub
                    qry = lax.broadcasted_iota(I32, (sub, tq), 1)
                    s = jnp.where(key <= qry, s, NEG)
                m_new = jnp.maximum(m, jnp.max(s, axis=0, keepdims=True))
                alpha = jnp.exp2(m - m_new)
                p = jnp.exp2(s - m_new)
                l = alpha * l + jnp.sum(p, axis=0, keepdims=True)
                acc = alpha * acc + jnp.dot(vt, p.astype(BF16), preferred_element_type=F32)
                carry[j] = (m_new, l, acc)
        return tuple(carry)

    init = tuple((jnp.full((1, tq), NEG, F32), jnp.zeros((1, tq), F32), jnp.zeros((LANES, tq), F32))
                 for _ in range(HEADS_PER_TILE))
    per_wide = TK_WIDE // tq
    n_wide = qi // per_wide
    carry = lax.fori_loop(
        0, n_wide, lambda i, c: step(pl.multiple_of(i * TK_WIDE, TK_WIDE), TK_WIDE, c, False), init)
    for extra in range(per_wide - 1):
        carry = lax.cond(qi - n_wide * per_wide > extra,
                         lambda c, e=extra: step(pl.multiple_of((n_wide * per_wide + e) * tq, tq), tq, c, False),
                         lambda c: c, carry)
    carry = step(pl.multiple_of(qi * tq, tq), tq, carry, True)
    outs = [acc / l for (_, l, acc) in carry]
    feat = lax.broadcasted_iota(I32, (LANES, tq), 0)
    o_ref[0] = jnp.where(feat < HEAD_DIM, outs[0], outs[1]).T.astype(o_ref.dtype)


def _fox(qa, ka, va_t):
    B, _, S, _ = qa.shape
    return pl.pallas_call(
        _fox_kernel,
        grid=(B, N_PAIRS, S // TQ),
        in_specs=[pl.BlockSpec((1, HEADS_PER_TILE, TQ, LANES), lambda b, hp, i: (b, hp, i, 0)),
                  pl.BlockSpec((1, HEADS_PER_TILE, S, LANES), lambda b, hp, i: (b, hp, 0, 0)),
                  pl.BlockSpec((1, LANES, S), lambda b, hp, i: (b, hp, 0))],
        out_specs=pl.BlockSpec((1, TQ, LANES), lambda b, hp, i: (b, i, hp)),
        out_shape=jax.ShapeDtypeStruct((B, S, D_GRP), F32),
        compiler_params=_cparams(("arbitrary", "arbitrary", "arbitrary")),
        name="fox",
    )(qa, ka, va_t)


def _dilated_kernel(q_ref, kp_ref, kc_ref, vp_ref, vc_ref, o_ref, qq, kk, vv, qq4, kk4, vv4, osc, lsc):
    u = pl.program_id(1)
    span = q_ref.shape[1]
    qq[...] = q_ref[0]
    kk[0:span, :] = kp_ref[0]
    kk[span:2 * span, :] = kc_ref[0]
    vv[0:span, :] = vp_ref[0]
    vv[span:2 * span, :] = vc_ref[0]
    for src, dst in ((qq, qq4), (kk, kk4), (vv, vv4)):
        part = src.shape[0] // DIL_PRE
        for a in range(DIL_PRE):
            dst[a * part:(a + 1) * part, :] = src[pl.ds(a, part, stride=DIL_PRE), :]

    def rows(buf, buf4, start, n, d):
        if d % DIL_PRE:
            return buf[pl.ds(start, n, stride=d), :]
        part = buf4.shape[0] // DIL_PRE
        a = lax.rem(start, DIL_PRE)
        return buf4[pl.ds(a * part + lax.div(start, DIL_PRE), n, stride=d // DIL_PRE), :]

    lane = lax.broadcasted_iota(I32, (BLOCK, LANES), 1)
    ql = lax.broadcasted_iota(I32, (BLOCK, 2 * BLOCK), 0)
    kl = lax.broadcasted_iota(I32, (BLOCK, 2 * BLOCK), 1)
    dist = ql + BLOCK - kl
    band = (dist >= 0) & (dist <= BLOCK)
    bias = jnp.where(band, 0.0, NEG)
    bias_first = jnp.where(band & (kl >= BLOCK), 0.0, NEG)

    def scores(q_start, k_start, d, first):
        qs = rows(qq, qq4, q_start, BLOCK, d).astype(BF16)
        ks = rows(kk, kk4, k_start, 2 * BLOCK, d).astype(BF16)
        mask = jnp.where(first, bias_first, bias)
        out = []
        for j in range(HEADS_PER_TILE):
            qj = jnp.where(lane // HEAD_DIM == j, qs, jnp.zeros_like(qs))
            out.append(lax.dot_general(qj, ks, (((1,), (1,)), ((), ())), preferred_element_type=F32) + mask)
        return out

    def finish(s_heads, k_start, d):
        vs = rows(vv, vv4, k_start, 2 * BLOCK, d).astype(BF16)
        o_heads, lse_heads = [], []
        for s in s_heads:
            m = jnp.max(s, axis=-1, keepdims=True)
            p = jnp.exp(s - m)
            l = jnp.sum(p, axis=-1, keepdims=True)
            o_heads.append(jnp.dot((p / l).astype(BF16), vs, preferred_element_type=F32))
            lse_heads.append(m + jnp.log(l))
        o = jnp.where(lane < HEAD_DIM, o_heads[0], o_heads[1])
        lse = jnp.where(lane < HEAD_DIM, lse_heads[0], lse_heads[1])
        return o, lse

    for pidx, (window, d) in enumerate(DIL_PATTERNS):
        assert window // d == BLOCK
        unit = d * BLOCK
        n_problems = (span // unit) * d
        assert n_problems % DIL_GROUP == 0

        def body(g, _, pidx=pidx, d=d, unit=unit):
            starts, s_all = [], []
            for t in range(DIL_GROUP):
                idx = g * DIL_GROUP + t
                w = idx // d
                q_start = w * unit + (idx - w * d)
                k_start = span - unit + q_start
                starts.append((q_start, k_start))
                s_all.append(scores(q_start, k_start, d, jnp.logical_and(u == 0, w == 0)))
            for (q_start, k_start), s_heads in zip(starts, s_all):
                o, lse = finish(s_heads, k_start, d)
                osc[pidx, pl.ds(q_start, BLOCK, stride=d), :] = o
                lsc[pidx, pl.ds(q_start, BLOCK, stride=d), :] = lse
            return 0

        lax.fori_loop(0, n_problems // DIL_GROUP, body, 0)

    mx = jnp.maximum(jnp.maximum(lsc[0], lsc[1]), lsc[2])
    num = jnp.zeros((span, LANES), F32)
    den = jnp.zeros((span, LANES), F32)
    for pidx in range(len(DIL_PATTERNS)):
        e = jnp.exp(lsc[pidx] - mx)
        num = num + e * osc[pidx]
        den = den + e
    o_ref[0] = (num / den).astype(o_ref.dtype)


def _dilated(qb, kb, vb):
    B, S, _ = qb.shape
    span = DIL_SPAN
    cur = pl.BlockSpec((1, span, LANES), lambda b, u, hp: (b, u, hp))
    prev = pl.BlockSpec((1, span, LANES), lambda b, u, hp: (b, jnp.maximum(u - 1, 0), hp))
    return pl.pallas_call(
        _dilated_kernel,
        grid=(B, S // span, N_PAIRS),
        in_specs=[cur, prev, cur, prev, cur],
        out_specs=cur,
        out_shape=jax.ShapeDtypeStruct((B, S, D_GRP), F32),
        scratch_shapes=[pltpu.VMEM((span, LANES), F32),
                        pltpu.VMEM((2 * span, LANES), F32), pltpu.VMEM((2 * span, LANES), F32),
                        pltpu.VMEM((span, LANES), F32),
                        pltpu.VMEM((2 * span, LANES), F32), pltpu.VMEM((2 * span, LANES), F32),
                        pltpu.VMEM((len(DIL_PATTERNS), span, LANES), F32),
                        pltpu.VMEM((len(DIL_PATTERNS), span, LANES), F32)],
        compiler_params=_cparams(("arbitrary", "arbitrary", "arbitrary")),
        name="dilated",
    )(qb, kb, kb, vb, vb)


def _store_row_tiles(ref, x):
    n = x.shape[0]
    for c in range(ROW_CHUNKS):
        ref[pl.ds(c, n, stride=ROW_CHUNKS), :] = x[:, c * LANES:(c + 1) * LANES]


def _load_row_tiles(ref, n):
    return jnp.concatenate([ref[pl.ds(c, n, stride=ROW_CHUNKS), :] for c in range(ROW_CHUNKS)], axis=1)


def _row_tile_copy(src_ref, src_row, dst_ref, dst_row, sem):
    src = src_ref.at[pl.ds(pl.multiple_of(src_row * ROW_CHUNKS, ROW_CHUNKS), ROW_CHUNKS), :]
    dst = dst_ref.at[pl.ds(pl.multiple_of(dst_row * ROW_CHUNKS, ROW_CHUNKS), ROW_CHUNKS), :]
    return pltpu.make_async_copy(src, dst, sem)


def _outproj_kernel(oa_ref, ob_ref, x_ref, gfox_ref, gdil_ref, wo_ref, gffn_ref, wr_ref, br_ref, upper_ref,
                    h_ref, hn_ref, eid_ref, gate_ref, rank_ref, cnt_ref, run_ref):
    @pl.when(pl.program_id(0) == 0)
    def _():
        run_ref[...] = jnp.zeros_like(run_ref)

    def norm(y, g):
        ms = jnp.mean(y * y, axis=-1, keepdims=True)
        return y * lax.rsqrt(ms + EPS) * g

    a = norm(oa_ref[...], gfox_ref[...]).astype(BF16)
    b = norm(ob_ref[...], gdil_ref[...]).astype(BF16)
    mix = (jnp.dot(a, wo_ref[0:D_GRP, :], preferred_element_type=F32)
           + jnp.dot(b, wo_ref[D_GRP:2 * D_GRP, :], preferred_element_type=F32))
    h = x_ref[...] + mix
    h_ref[...] = h
    hn = norm(h, gffn_ref[...])
    _store_row_tiles(hn_ref, hn)

    z = lax.dot_general(wr_ref[...], hn, (((1,), (1,)), ((), ())), preferred_element_type=F32,
                        precision=lax.Precision.HIGHEST) + br_ref[...]
    tm = z.shape[1]
    best = z[0:1, :]
    g_sel = jnp.zeros((1, tm), I32)
    for g in range(1, N_GROUPS):
        better = z[g:g + 1, :] > best
        g_sel = jnp.where(better, g, g_sel)
        best = jnp.maximum(best, z[g:g + 1, :])
    den = jnp.zeros((1, tm), F32)
    for g in range(N_GROUPS):
        den = den + jnp.exp(z[g:g + 1, :] - best)
    pg_top = 1.0 / den

    ze = jnp.zeros((EXPERTS_PER_GROUP, tm), F32)
    for g in range(N_GROUPS):
        ze = jnp.where(g_sel == g, z[8 + g * EXPERTS_PER_GROUP:8 + (g + 1) * EXPERTS_PER_GROUP, :], ze)
    e_iota = lax.broadcasted_iota(I32, ze.shape, 0)
    v1 = jnp.max(ze, axis=0, keepdims=True)
    i1 = jnp.min(jnp.where(ze == v1, e_iota, EXPERTS_PER_GROUP), axis=0, keepdims=True)
    ze2 = jnp.where(e_iota == i1, -jnp.inf, ze)
    v2 = jnp.max(ze2, axis=0, keepdims=True)
    i2 = jnp.min(jnp.where(ze2 == v2, e_iota, EXPERTS_PER_GROUP), axis=0, keepdims=True)
    e2 = jnp.exp(v2 - v1)
    inv = 1.0 / (1.0 + e2)
    gate1 = inv * pg_top
    gate2 = e2 * inv * pg_top
    eid1 = g_sel * EXPERTS_PER_GROUP + i1
    eid2 = g_sel * EXPERTS_PER_GROUP + i2

    x_iota = lax.broadcasted_iota(I32, (N_EXPERTS, tm), 0)
    hot1 = x_iota == eid1
    hot2 = x_iota == eid2
    multi = jnp.logical_or(hot1, hot2)
    before = jnp.dot(multi.astype(BF16), upper_ref[...], preferred_element_type=F32)
    slot = before + run_ref[:, 0:1]
    rank1 = jnp.sum(jnp.where(hot1, slot, 0.0), axis=0, keepdims=True)
    rank2 = jnp.sum(jnp.where(hot2, slot, 0.0), axis=0, keepdims=True)
    run_ref[...] = run_ref[...] + jnp.sum(multi.astype(F32), axis=1, keepdims=True)

    eid_ref[...] = jnp.concatenate([eid1, eid2], axis=0)
    gate_ref[...] = jnp.concatenate([gate1, gate2], axis=0)
    rank_ref[...] = jnp.concatenate([rank1, rank2], axis=0).astype(I32)
    cnt_ref[...] = run_ref[...].astype(I32)


def _outproj(oa, ob, x2, gfox, gdil, w_out, gffn, w_r, b_r, upper):
    T, D = x2.shape
    tm = TM_OUT
    const = lambda shape: pl.BlockSpec(shape, lambda i: (0,) * len(shape))
    tok = lambda w: pl.BlockSpec((tm, w), lambda i: (i, 0))
    lanes2 = pl.BlockSpec((2, tm), lambda i: (0, i))
    return pl.pallas_call(
        _outproj_kernel,
        grid=(T // tm,),
        in_specs=[tok(D_GRP), tok(D_GRP), tok(D), const((1, D_GRP)), const((1, D_GRP)), const((D, D)),
                  const((1, D)), const((ROUTER_ROWS, D)), const((ROUTER_ROWS, 1)), const((tm, tm))],
        out_specs=[tok(D), pl.BlockSpec((tm * ROW_CHUNKS, LANES), lambda i: (i, 0)),
                   lanes2, lanes2, lanes2, const((N_EXPERTS, LANES))],
        out_shape=[jax.ShapeDtypeStruct((T, D), F32), jax.ShapeDtypeStruct((T * ROW_CHUNKS, LANES), F32),
                   jax.ShapeDtypeStruct((2, T), I32), jax.ShapeDtypeStruct((2, T), F32),
                   jax.ShapeDtypeStruct((2, T), I32), jax.ShapeDtypeStruct((N_EXPERTS, LANES), I32)],
        scratch_shapes=[pltpu.VMEM((N_EXPERTS, LANES), F32)],
        compiler_params=_cparams(("arbitrary",)),
        name="outproj",
    )(oa, ob, x2, gfox, gdil, w_out, gffn, w_r, b_r, upper)


def _scatter_kernel(starts_ref, cnt_ref, pos_ref, hn_ref, xs_ref, ring, sems, zero_sem, *, n_steps):
    i = pl.program_id(0)
    tm = hn_ref.shape[0] // ROW_CHUNKS
    slot = lax.rem(i, 2)

    def wait_slot(s):
        for _ in range(2):
            pltpu.make_async_copy(ring.at[s], xs_ref.at[pl.ds(0, tm * ROW_CHUNKS), :], sems.at[s]).wait()

    @pl.when(i >= 2)
    def _():
        wait_slot(slot)

    ring[slot] = hn_ref[...]

    def start(r, _):
        for k in range(2):
            _row_tile_copy(ring.at[slot], r, xs_ref, pos_ref[k, r], sems.at[slot]).start()
        return 0

    lax.fori_loop(0, tm, start, 0, unroll=DMA_UNROLL)

    @pl.when(i == n_steps - 1)
    def _():
        wait_slot(slot)
        if n_steps > 1:
            wait_slot(1 - slot)
        ring[0] = jnp.zeros((tm * ROW_CHUNKS, LANES), F32)

        def pad_expert(e, _, wait):
            n_pad = lax.rem(TILE_E - lax.rem(cnt_ref[e], TILE_E), TILE_E)
            first = starts_ref[e] + cnt_ref[e]
            size = TILE_E // 2
            while size >= 1:
                row0 = first + (n_pad & ~(2 * size - 1))

                @pl.when((n_pad & size) != 0)
                def _(size=size, row0=row0):
                    copy = pltpu.make_async_copy(
                        ring.at[0, pl.ds(0, size * ROW_CHUNKS), :],
                        xs_ref.at[pl.ds(pl.multiple_of(row0 * ROW_CHUNKS, ROW_CHUNKS), size * ROW_CHUNKS), :],
                        zero_sem)
                    copy.wait() if wait else copy.start()

                size //= 2
            return 0

        lax.fori_loop(0, N_EXPERTS, functools.partial(pad_expert, wait=False), 0)
        last = N_EXPERTS - 1
        used_rows = starts_ref[last] + cnt_ref[last] + lax.rem(TILE_E - lax.rem(cnt_ref[last], TILE_E), TILE_E)
        n_tail = xs_ref.shape[0] // (tm * ROW_CHUNKS) - used_rows // tm

        def tail_copy(t):
            row0 = pl.multiple_of((used_rows + t * tm) * ROW_CHUNKS, tm * ROW_CHUNKS)
            return pltpu.make_async_copy(ring.at[0], xs_ref.at[pl.ds(row0, tm * ROW_CHUNKS), :], zero_sem)

        lax.fori_loop(0, n_tail, lambda t, c: (tail_copy(t).start(), c)[1], 0)
        lax.fori_loop(0, N_EXPERTS, functools.partial(pad_expert, wait=True), 0)
        lax.fori_loop(0, n_tail, lambda t, c: (tail_copy(t).wait(), c)[1], 0)


def _scatter(starts, cnt, pos, hn, n_rows):
    T = hn.shape[0] // ROW_CHUNKS
    tm = TM_ROWS
    assert TILE_E % tm == 0 and TILE_E // 2 <= tm
    n_steps = T // tm
    grid_spec = pltpu.PrefetchScalarGridSpec(
        num_scalar_prefetch=2,
        grid=(n_steps,),
        in_specs=[pl.BlockSpec((2, tm), lambda i, starts, cnt: (0, i), memory_space=pltpu.SMEM),
                  pl.BlockSpec((tm * ROW_CHUNKS, LANES), lambda i, starts, cnt: (i, 0))],
        out_specs=pl.BlockSpec(memory_space=pl.ANY),
        scratch_shapes=[pltpu.VMEM((2, tm * ROW_CHUNKS, LANES), F32),
                        pltpu.SemaphoreType.DMA((2,)), pltpu.SemaphoreType.DMA(())],
    )
    return pl.pallas_call(
        functools.partial(_scatter_kernel, n_steps=n_steps),
        grid_spec=grid_spec,
        out_shape=jax.ShapeDtypeStruct((n_rows * ROW_CHUNKS, LANES), F32),
        compiler_params=_cparams(("arbitrary",)),
        name="scatter_rows",
    )(starts, cnt, pos, hn)


def _experts_kernel(te_ref, tv_ref, xs_ref, wg_ref, wu_ref, wd_ref, y_ref, wg_bf, wu_bf, wd_bf):
    i = pl.program_id(0)
    new_expert = jnp.logical_or(i == 0, te_ref[i] != te_ref[jnp.maximum(i - 1, 0)])

    @pl.when(jnp.logical_and(tv_ref[i] != 0, new_expert))
    def _():
        wg_bf[...] = wg_ref[0].astype(BF16)
        wu_bf[...] = wu_ref[0].astype(BF16)
        wd_bf[...] = wd_ref[0].astype(BF16)

    @pl.when(tv_ref[i] != 0)
    def _():
        x = _load_row_tiles(xs_ref, TILE_E).astype(BF16)
        g = jnp.dot(x, wg_bf[...], preferred_element_type=F32)
        up = jnp.dot(x, wu_bf[...], preferred_element_type=F32)
        hmid = (g * jax.nn.sigmoid(g) * up).astype(BF16)
        _store_row_tiles(y_ref, jnp.dot(hmid, wd_bf[...], preferred_element_type=F32))

    @pl.when(tv_ref[i] == 0)
    def _():
        y_ref[...] = jnp.zeros_like(y_ref)


def _experts(tile_expert, tile_valid, xs, w_gate, w_up, w_down):
    n_tiles = tile_expert.shape[0]
    D = D_MODEL
    rows_spec = pl.BlockSpec((TILE_E * ROW_CHUNKS, LANES), lambda i, te, tv: (i, 0))
    grid_spec = pltpu.PrefetchScalarGridSpec(
        num_scalar_prefetch=2,
        grid=(n_tiles,),
        in_specs=[rows_spec,
                  pl.BlockSpec((1, D, D_EXPERT), lambda i, te, tv: (te[i], 0, 0)),
                  pl.BlockSpec((1, D, D_EXPERT), lambda i, te, tv: (te[i], 0, 0)),
                  pl.BlockSpec((1, D_EXPERT, D), lambda i, te, tv: (te[i], 0, 0))],
        out_specs=rows_spec,
        scratch_shapes=[pltpu.VMEM((D, D_EXPERT), BF16), pltpu.VMEM((D, D_EXPERT), BF16),
                        pltpu.VMEM((D_EXPERT, D), BF16)],
    )
    return pl.pallas_call(
        _experts_kernel,
        grid_spec=grid_spec,
        out_shape=jax.ShapeDtypeStruct(xs.shape, F32),
        compiler_params=_cparams(("arbitrary",)),
        name="experts",
    )(tile_expert, tile_valid, xs, w_gate, w_up, w_down)


def _combine_kernel(pos_ref, pos_next_ref, h_ref, gate_ref, y_ref, o_ref, ybuf, sems, *, n_steps):
    i = pl.program_id(0)
    tm = h_ref.shape[0]
    slot = lax.rem(i, 2)

    def gather(p_ref, s):
        def start(r, _):
            for k in range(2):
                _row_tile_copy(y_ref, p_ref[k, r], ybuf.at[s, k], r, sems.at[s]).start()
            return 0

        lax.fori_loop(0, tm, start, 0, unroll=DMA_UNROLL)

    @pl.when(i == 0)
    def _():
        gather(pos_ref, slot)

    @pl.when(i + 1 < n_steps)
    def _():
        gather(pos_next_ref, 1 - slot)

    for k in range(2):
        pltpu.make_async_copy(y_ref.at[pl.ds(0, tm * ROW_CHUNKS), :], ybuf.at[slot, k], sems.at[slot]).wait()
    g = gate_ref[...]
    o_ref[...] = (h_ref[...] + g[:, 0:1] * _load_row_tiles(ybuf.at[slot, 0], tm)
                  + g[:, 1:2] * _load_row_tiles(ybuf.at[slot, 1], tm))


def _combine(pos, h, gate_t, y):
    T, D = h.shape
    tm = TM_ROWS
    n_steps = T // tm
    return pl.pallas_call(
        functools.partial(_combine_kernel, n_steps=n_steps),
        grid=(n_steps,),
        in_specs=[pl.BlockSpec((2, tm), lambda i: (0, i), memory_space=pltpu.SMEM),
                  pl.BlockSpec((2, tm), lambda i: (0, jnp.minimum(i + 1, n_steps - 1)), memory_space=pltpu.SMEM),
                  pl.BlockSpec((tm, D), lambda i: (i, 0)),
                  pl.BlockSpec((tm, 2), lambda i: (i, 0)),
                  pl.BlockSpec(memory_space=pl.ANY)],
        out_specs=pl.BlockSpec((tm, D), lambda i: (i, 0)),
        out_shape=jax.ShapeDtypeStruct((T, D), F32),
        scratch_shapes=[pltpu.VMEM((2, 2, tm * ROW_CHUNKS, LANES), F32), pltpu.SemaphoreType.DMA((2,))],
        compiler_params=_cparams(("arbitrary",)),
        name="combine",
    )(pos, pos, h, gate_t, y)


def _rope_tables(S):
    inv_freq = 1.0 / (ROPE_THETA ** (jnp.arange(0, HEAD_DIM, 2, dtype=F32) / HEAD_DIM))
    ang = jnp.arange(S, dtype=F32)[:, None] * inv_freq[None, :]
    cos, sin = jnp.cos(ang), jnp.sin(ang)
    cos_t = jnp.tile(cos, (1, 2 * HEADS_PER_TILE))
    sin_t = jnp.tile(jnp.concatenate([-sin, sin], axis=1), (1, HEADS_PER_TILE))
    return cos_t, sin_t


def _layer(x, norm_mix, w_in, b_forget, q_norm_fox, k_norm_fox, q_norm_dil, k_norm_dil,
           out_norm_fox, out_norm_dil, w_out, norm_ffn, w_router_group, b_router_group,
           w_router_expert, b_router_expert, w_gate, w_up, w_down):
    B, S, D = x.shape
    T = B * S
    n_main = 6 * D_GRP

    w_main = w_in[:, :n_main].astype(BF16)
    w_vt = w_in[:, 2 * D_GRP:3 * D_GRP].T.astype(BF16)
    w_f = jnp.pad(w_in[:, n_main:], ((0, 0), (0, LANES - N_HEADS))).astype(BF16)
    b_f = jnp.pad(b_forget, (0, LANES - N_HEADS))[None, :]
    per_head = lambda g: jnp.tile(g, N_HEADS)[None, :]
    bd = jnp.kron(jnp.eye(N_HEADS, dtype=F32), jnp.ones((HEAD_DIM, HEAD_DIM), F32)).astype(BF16)
    cos_t, sin_t = _rope_tables(S)
    tri = jnp.tril(jnp.ones((TM_IN, TM_IN), F32)).astype(BF16)
    upper = jnp.triu(jnp.ones((TM_OUT, TM_OUT), F32), k=1).astype(BF16)
    w_r = jnp.concatenate([
        jnp.pad(w_router_group.T, ((0, 8 - N_GROUPS), (0, 0))),
        w_router_expert.transpose(0, 2, 1).reshape(N_EXPERTS, D)], axis=0)
    b_r = jnp.concatenate([jnp.pad(b_router_group, (0, 8 - N_GROUPS)), b_router_expert.reshape(-1)])[:, None]

    qa, ka, va, qb, kb, vb = _inproj(
        x, norm_mix[None, :], w_main, w_vt, w_f, b_f, per_head(q_norm_fox), per_head(k_norm_fox),
        per_head(q_norm_dil), per_head(k_norm_dil), bd, cos_t, sin_t, tri)
    oa = _fox(qa, ka, va)
    ob = _dilated(qb, kb, vb)

    h, hn, eid, gate, rank, cnt = _outproj(
        oa.reshape(T, D_GRP), ob.reshape(T, D_GRP), x.reshape(T, D), out_norm_fox[None, :],
        out_norm_dil[None, :], w_out.astype(BF16), norm_ffn[None, :], w_r, b_r, upper)

    counts = cnt[:, 0]
    padded = ((counts + TILE_E - 1) // TILE_E) * TILE_E
    ends = jnp.cumsum(padded)
    starts = ends - padded
    is_expert = eid[:, :, None] == jnp.arange(N_EXPERTS, dtype=I32)
    pos = jnp.sum(jnp.where(is_expert, starts, 0), axis=-1) + rank
    n_tiles = (2 * T) // TILE_E + N_EXPERTS
    tile_index = jnp.arange(n_tiles, dtype=I32)
    tile_valid = (tile_index * TILE_E < ends[-1]).astype(I32)
    tile_expert = jnp.minimum(
        jnp.sum((ends[None, :] <= (tile_index * TILE_E)[:, None]).astype(I32), axis=1), N_EXPERTS - 1)

    xs = _scatter(starts, counts, pos, hn, n_tiles * TILE_E)
    y = _experts(tile_expert, tile_valid, xs, w_gate, w_up, w_down)
    out = _combine(pos, h, gate.T, y)
    return out.reshape(B, S, D)


def kernel(x, norm_mix, w_in, b_forget, q_norm_fox, k_norm_fox, q_norm_dil, k_norm_dil, out_norm_fox,
           out_norm_dil, w_out, norm_ffn, w_router_group, b_router_group, w_router_expert,
           b_router_expert, w_gate, w_up, w_down):
    h = x
    for l in range(norm_mix.shape[0]):
        h = _layer(h, norm_mix[l], w_in[l], b_forget[l], q_norm_fox[l], k_norm_fox[l], q_norm_dil[l],
                   k_norm_dil[l], out_norm_fox[l], out_norm_dil[l], w_out[l], norm_ffn[l],
                   w_router_group[l], b_router_group[l], w_router_expert[l], b_router_expert[l],
                   w_gate[l], w_up[l], w_down[l])
    return h
```

```python
import functools
import math

import jax
import jax.numpy as jnp
from jax import lax
from jax.experimental import pallas as pl
from jax.experimental.pallas import tpu as pltpu

F32 = jnp.float32
BF16 = jnp.bfloat16
I32 = jnp.int32

D_MODEL = 1024
HEAD_DIM = 64
N_HEADS = 8
D_GRP = N_HEADS * HEAD_DIM
LANES = 128
HEADS_PER_TILE = LANES // HEAD_DIM
N_PAIRS = D_GRP // LANES
DIL_PATTERNS = ((128, 1), (512, 4), (2048, 16))
BLOCK = 128
ROPE_THETA = 10000.0
N_GROUPS = 4
EXPERTS_PER_GROUP = 8
N_EXPERTS = N_GROUPS * EXPERTS_PER_GROUP
D_EXPERT = 512
EPS = 1e-6
NEG = -1e30
LOG2E = 1.4426950408889634

TM_IN = 512
TQ = 1024
TK_WIDE = 1024
TK_SUB = 256
FOX_AHEAD = 4
DIL_SPAN = 2048
DIL_GROUP = 8
DIL_PRE = 4
TM_OUT = 512
TM_ROWS = 256
TILE_E = 512
ROUTER_ROWS = 8 + N_EXPERTS
ROW_CHUNKS = D_MODEL // LANES
DMA_UNROLL = 8
VMEM_LIMIT = 56 * 1024 * 1024


def _cparams(sem, flags=None):
    return pltpu.CompilerParams(dimension_semantics=sem, vmem_limit_bytes=VMEM_LIMIT, flags=flags)


def _inproj_kernel(x_ref, gmix_ref, w_ref, wvt_ref, wf_ref, bf_ref, gqa_ref, gka_ref, gqb_ref, gkb_ref,
                   bd_ref, cos_ref, sin_ref, tri_ref,
                   qa_ref, ka_ref, va_ref, qb_ref, kb_ref, vb_ref, carry_ref):
    @pl.when(pl.program_id(1) == 0)
    def _():
        carry_ref[...] = jnp.zeros_like(carry_ref)

    x = x_ref[0]
    ms = jnp.mean(x * x, axis=-1, keepdims=True)
    xn = (x * lax.rsqrt(ms + EPS) * gmix_ref[...]).astype(BF16)

    def seg(j):
        return jnp.dot(xn, w_ref[:, j * D_GRP:(j + 1) * D_GRP], preferred_element_type=F32)

    def head_norm(y, g_ref, scale):
        ss = jnp.dot((y * y).astype(BF16), bd_ref[...], preferred_element_type=F32) * (1.0 / HEAD_DIM)
        return y * lax.rsqrt(ss + EPS) * (g_ref[...] * scale)

    cos = cos_ref[...]
    sin = sin_ref[...]
    lane = lax.broadcasted_iota(I32, (x.shape[0], LANES), 1)
    first_half = (lane % HEAD_DIM) < (HEAD_DIM // 2)

    def rope(y):
        outs = []
        for j in range(N_PAIRS):
            ys = y[:, j * LANES:(j + 1) * LANES]
            partner = jnp.where(first_half, pltpu.roll(ys, LANES - HEAD_DIM // 2, 1),
                                pltpu.roll(ys, HEAD_DIM // 2, 1))
            outs.append(ys * cos + partner * sin)
        return jnp.concatenate(outs, axis=1)

    scale = 1.0 / math.sqrt(HEAD_DIM)
    va_ref[0] = lax.dot_general(wvt_ref[...], xn, (((1,), (1,)), ((), ())),
                                preferred_element_type=F32).astype(va_ref.dtype)
    qb_ref[0] = rope(head_norm(seg(3), gqb_ref, scale)).astype(qb_ref.dtype)
    kb_ref[0] = rope(head_norm(seg(4), gkb_ref, 1.0)).astype(kb_ref.dtype)
    vb_ref[0] = seg(5).astype(vb_ref.dtype)

    fa = jnp.dot(xn, wf_ref[...], preferred_element_type=F32) + bf_ref[...]
    logf = jnp.minimum(fa, 0.0) - jnp.log1p(jnp.exp(-jnp.abs(fa)))
    hi = logf.astype(BF16)
    mid = (logf - hi.astype(F32)).astype(BF16)
    lo = (logf - hi.astype(F32) - mid.astype(F32)).astype(BF16)
    parts = jnp.dot(tri_ref[...], jnp.concatenate([hi, mid, lo], axis=1), preferred_element_type=F32)
    c = parts[:, :LANES] + parts[:, LANES:2 * LANES] + parts[:, 2 * LANES:] + carry_ref[...]
    carry_ref[...] = c[c.shape[0] - 1:, :]

    qa = head_norm(seg(0), gqa_ref, scale * LOG2E)
    ka = head_norm(seg(1), gka_ref, 1.0)
    c2 = c * LOG2E
    ones = jnp.where((lane >= HEAD_DIM + 3) & (lane < HEAD_DIM + 6), 1.0, 0.0)
    for h in range(N_HEADS):
        cb = jnp.broadcast_to(c2[:, h:h + 1], (x.shape[0], LANES))
        hi = cb.astype(BF16).astype(F32)
        mid = (cb - hi).astype(BF16).astype(F32)
        lo = cb - hi - mid
        pieces = jnp.where(lane == HEAD_DIM, hi, jnp.where(lane == HEAD_DIM + 1, mid,
                           jnp.where(lane == HEAD_DIM + 2, lo, 0.0)))
        q_extra = pieces + ones
        k_extra = jnp.where((lane >= HEAD_DIM) & (lane < HEAD_DIM + 3), 1.0, 0.0) \
            - pltpu.roll(pieces, 3, 1)
        j, odd = divmod(h, HEADS_PER_TILE)
        qp = qa[:, j * LANES:(j + 1) * LANES]
        kp = ka[:, j * LANES:(j + 1) * LANES]
        if odd:
            qp = pltpu.roll(qp, HEAD_DIM, 1)
            kp = pltpu.roll(kp, HEAD_DIM, 1)
        qa_ref[0, h] = jnp.where(lane < HEAD_DIM, qp, q_extra).astype(qa_ref.dtype)
        ka_ref[0, h] = jnp.where(lane < HEAD_DIM, kp, k_extra).astype(ka_ref.dtype)


def _inproj(x, gmix, w_main, w_vt, w_f, b_f, gqa, gka, gqb, gkb, bd, cos_t, sin_t, tri):
    B, S, D = x.shape
    tm = TM_IN
    const = lambda shape: pl.BlockSpec(shape, lambda b, i: (0,) * len(shape))
    tok = lambda w, dt: jax.ShapeDtypeStruct((B, S, w), dt)
    tok_spec = lambda w: pl.BlockSpec((1, tm, w), lambda b, i: (b, i, 0))
    head_spec = pl.BlockSpec((1, N_HEADS, tm, LANES), lambda b, i: (b, 0, i, 0))
    head_shape = jax.ShapeDtypeStruct((B, N_HEADS, S, LANES), BF16)
    return pl.pallas_call(
        _inproj_kernel,
        grid=(B, S // tm),
        in_specs=[tok_spec(D), const((1, D)), const(w_main.shape), const(w_vt.shape), const(w_f.shape),
                  const((1, LANES)),
                  const((1, D_GRP)), const((1, D_GRP)), const((1, D_GRP)), const((1, D_GRP)),
                  const((D_GRP, D_GRP)),
                  pl.BlockSpec((tm, LANES), lambda b, i: (i, 0)),
                  pl.BlockSpec((tm, LANES), lambda b, i: (i, 0)),
                  const((tm, tm))],
        out_specs=[head_spec, head_spec, pl.BlockSpec((1, D_GRP, tm), lambda b, i: (b, 0, i))]
        + [tok_spec(D_GRP)] * 3,
        out_shape=[head_shape, head_shape, jax.ShapeDtypeStruct((B, D_GRP, S), BF16),
                   tok(D_GRP, F32), tok(D_GRP, F32), tok(D_GRP, F32)],
        scratch_shapes=[pltpu.VMEM((1, LANES), F32)],
        compiler_params=_cparams(("arbitrary", "arbitrary")),
        name="inproj",
    )(x, gmix, w_main, w_vt, w_f, b_f, gqa, gka, gqb, gkb, bd, cos_t, sin_t, tri)


def _fox_kernel(q_ref, k_ref, v_ref, o_ref):
    qi = pl.program_id(2)
    tq = q_ref.shape[2]

    def step(start, width, carry, diag):
        carry = list(carry)
        sub = min(TK_SUB, width)
        chunks = [(c, j) for c in range(width // sub) for j in range(HEADS_PER_TILE)]
        def score(c, j):
            k = k_ref[0, j, pl.ds(start + c * sub, sub), :]
            return lax.dot_general(k, q_ref[0, j], (((1,), (1,)), ((), ())), preferred_element_type=F32)

        scores = {cj: score(*cj) for cj in chunks[:FOX_AHEAD]}
        for n, (c, j) in enumerate(chunks):
                if n + FOX_AHEAD < len(chunks):
                    nxt = chunks[n + FOX_AHEAD]
                    scores[nxt] = score(*nxt)
                m, l, acc = carry[j]
                vt = v_ref[0, :, pl.ds(start + c * sub, sub)]
                s = scores.pop((c, j))
                if diag:
                    key = lax.broadcasted_iota(I32, (sub, tq), 0) + c * sub
                    qry = lax.broadcasted_iota(I32, (sub, tq), 1)
                    s = jnp.where(key <= qry, s, NEG)
                m_new = jnp.maximum(m, jnp.max(s, axis=0, keepdims=True))
                alpha = jnp.exp2(m - m_new)
                p = jnp.exp2(s - m_new)
                l = alpha * l + jnp.sum(p, axis=0, keepdims=True)
                acc = alpha * acc + jnp.dot(vt, p.astype(BF16), preferred_element_type=F32)
                carry[j] = (m_new, l, acc)
        return tuple(carry)

    init = tuple((jnp.full((1, tq), NEG, F32), jnp.zeros((1, tq), F32), jnp.zeros((LANES, tq), F32))
                 for _ in range(HEADS_PER_TILE))
    per_wide = TK_WIDE // tq
    n_wide = qi // per_wide
    carry = lax.fori_loop(
        0, n_wide, lambda i, c: step(pl.multiple_of(i * TK_WIDE, TK_WIDE), TK_WIDE, c, False), init)
    for extra in range(per_wide - 1):
        carry = lax.cond(qi - n_wide * per_wide > extra,
                         lambda c, e=extra: step(pl.multiple_of((n_wide * per_wide + e) * tq, tq), tq, c, False),
                         lambda c: c, carry)
    carry = step(pl.multiple_of(qi * tq, tq), tq, carry, True)
    outs = [acc / l for (_, l, acc) in carry]
    feat = lax.broadcasted_iota(I32, (LANES, tq), 0)
    o_ref[0] = jnp.where(feat < HEAD_DIM, outs[0], outs[1]).T.astype(o_ref.dtype)


def _fox(qa, ka, va_t):
    B, _, S, _ = qa.shape
    return pl.pallas_call(
        _fox_kernel,
        grid=(B, N_PAIRS, S // TQ),
        in_specs=[pl.BlockSpec((1, HEADS_PER_TILE, TQ, LANES), lambda b, hp, i: (b, hp, i, 0)),
                  pl.BlockSpec((1, HEADS_PER_TILE, S, LANES), lambda b, hp, i: (b, hp, 0, 0)),
                  pl.BlockSpec((1, LANES, S), lambda b, hp, i: (b, hp, 0))],
        out_specs=pl.BlockSpec((1, TQ, LANES), lambda b, hp, i: (b, i, hp)),
        out_shape=jax.ShapeDtypeStruct((B, S, D_GRP), F32),
        compiler_params=_cparams(("arbitrary", "arbitrary", "arbitrary")),
        name="fox",
    )(qa, ka, va_t)


def _dilated_kernel(q_ref, kp_ref, kc_ref, vp_ref, vc_ref, o_ref, qq, kk, vv, qq4, kk4, vv4, osc, lsc):
    u = pl.program_id(1)
    span = q_ref.shape[1]
    qq[...] = q_ref[0]
    kk[0:span, :] = kp_ref[0]
    kk[span:2 * span, :] = kc_ref[0]
    vv[0:span, :] = vp_ref[0]
    vv[span:2 * span, :] = vc_ref[0]
    for src, dst in ((qq, qq4), (kk, kk4), (vv, vv4)):
        part = src.shape[0] // DIL_PRE
        for a in range(DIL_PRE):
            dst[a * part:(a + 1) * part, :] = src[pl.ds(a, part, stride=DIL_PRE), :]

    def rows(buf, buf4, start, n, d):
        if d % DIL_PRE:
            return buf[pl.ds(start, n, stride=d), :]
        part = buf4.shape[0] // DIL_PRE
        a = lax.rem(start, DIL_PRE)
        return buf4[pl.ds(a * part + lax.div(start, DIL_PRE), n, stride=d // DIL_PRE), :]

    lane = lax.broadcasted_iota(I32, (BLOCK, LANES), 1)
    ql = lax.broadcasted_iota(I32, (BLOCK, 2 * BLOCK), 0)
    kl = lax.broadcasted_iota(I32, (BLOCK, 2 * BLOCK), 1)
    dist = ql + BLOCK - kl
    band = (dist >= 0) & (dist <= BLOCK)
    bias = jnp.where(band, 0.0, NEG)
    bias_first = jnp.where(band & (kl >= BLOCK), 0.0, NEG)

    def scores(q_start, k_start, d, first):
        qs = rows(qq, qq4, q_start, BLOCK, d).astype(BF16)
        ks = rows(kk, kk4, k_start, 2 * BLOCK, d).astype(BF16)
        mask = jnp.where(first, bias_first, bias)
        out = []
        for j in range(HEADS_PER_TILE):
            qj = jnp.where(lane // HEAD_DIM == j, qs, jnp.zeros_like(qs))
            out.append(lax.dot_general(qj, ks, (((1,), (1,)), ((), ())), preferred_element_type=F32) + mask)
        return out

    def finish(s_heads, k_start, d):
        vs = rows(vv, vv4, k_start, 2 * BLOCK, d).astype(BF16)
        o_heads, lse_heads = [], []
        for s in s_heads:
            m = jnp.max(s, axis=-1, keepdims=True)
            p = jnp.exp(s - m)
            l = jnp.sum(p, axis=-1, keepdims=True)
            o_heads.append(jnp.dot((p / l).astype(BF16), vs, preferred_element_type=F32))
            lse_heads.append(m + jnp.log(l))
        o = jnp.where(lane < HEAD_DIM, o_heads[0], o_heads[1])
        lse = jnp.where(lane < HEAD_DIM, lse_heads[0], lse_heads[1])
        return o, lse

    for pidx, (window, d) in enumerate(DIL_PATTERNS):
        assert window // d == BLOCK
        unit = d * BLOCK
        n_problems = (span // unit) * d
        assert n_problems % DIL_GROUP == 0

        def body(g, _, pidx=pidx, d=d, unit=unit):
            starts, s_all = [], []
            for t in range(DIL_GROUP):
                idx = g * DIL_GROUP + t
                w = idx // d
                q_start = w * unit + (idx - w * d)
                k_start = span - unit + q_start
                starts.append((q_start, k_start))
                s_all.append(scores(q_start, k_start, d, jnp.logical_and(u == 0, w == 0)))
            for (q_start, k_start), s_heads in zip(starts, s_all):
                o, lse = finish(s_heads, k_start, d)
                osc[pidx, pl.ds(q_start, BLOCK, stride=d), :] = o
                lsc[pidx, pl.ds(q_start, BLOCK, stride=d), :] = lse
            return 0

        lax.fori_loop(0, n_problems // DIL_GROUP, body, 0)

    mx = jnp.maximum(jnp.maximum(lsc[0], lsc[1]), lsc[2])
    num = jnp.zeros((span, LANES), F32)
    den = jnp.zeros((span, LANES), F32)
    for pidx in range(len(DIL_PATTERNS)):
        e = jnp.exp(lsc[pidx] - mx)
        num = num + e * osc[pidx]
        den = den + e
    o_ref[0] = (num / den).astype(o_ref.dtype)


def _dilated(qb, kb, vb):
    B, S, _ = qb.shape
    span = DIL_SPAN
    cur = pl.BlockSpec((1, span, LANES), lambda b, u, hp: (b, u, hp))
    prev = pl.BlockSpec((1, span, LANES), lambda b, u, hp: (b, jnp.maximum(u - 1, 0), hp))
    return pl.pallas_call(
        _dilated_kernel,
        grid=(B, S // span, N_PAIRS),
        in_specs=[cur, prev, cur, prev, cur],
        out_specs=cur,
        out_shape=jax.ShapeDtypeStruct((B, S, D_GRP), F32),
        scratch_shapes=[pltpu.VMEM((span, LANES), F32),
                        pltpu.VMEM((2 * span, LANES), F32), pltpu.VMEM((2 * span, LANES), F32),
                        pltpu.VMEM((span, LANES), F32),
                        pltpu.VMEM((2 * span, LANES), F32), pltpu.VMEM((2 * span, LANES), F32),
                        pltpu.VMEM((len(DIL_PATTERNS), span, LANES), F32),
                        pltpu.VMEM((len(DIL_PATTERNS), span, LANES), F32)],
        compiler_params=_cparams(("arbitrary", "arbitrary", "arbitrary")),
        name="dilated",
    )(qb, kb, kb, vb, vb)


def _store_row_tiles(ref, x):
    n = x.shape[0]
    for c in range(ROW_CHUNKS):
        ref[pl.ds(c, n, stride=ROW_CHUNKS), :] = x[:, c * LANES:(c + 1) * LANES]


def _load_row_tiles(ref, n):
    return jnp.concatenate([ref[pl.ds(c, n, stride=ROW_CHUNKS), :] for c in range(ROW_CHUNKS)], axis=1)


def _row_tile_copy(src_ref, src_row, dst_ref, dst_row, sem):
    src = src_ref.at[pl.ds(pl.multiple_of(src_row * ROW_CHUNKS, ROW_CHUNKS), ROW_CHUNKS), :]
    dst = dst_ref.at[pl.ds(pl.multiple_of(dst_row * ROW_CHUNKS, ROW_CHUNKS), ROW_CHUNKS), :]
    return pltpu.make_async_copy(src, dst, sem)


def _outproj_kernel(oa_ref, ob_ref, x_ref, gfox_ref, gdil_ref, wo_ref, gffn_ref, wr_ref, br_ref, upper_ref,
                    h_ref, hn_ref, eid_ref, gate_ref, rank_ref, cnt_ref, run_ref):
    @pl.when(pl.program_id(0) == 0)
    def _():
        run_ref[...] = jnp.zeros_like(run_ref)

    def norm(y, g):
        ms = jnp.mean(y * y, axis=-1, keepdims=True)
        return y * lax.rsqrt(ms + EPS) * g

    a = norm(oa_ref[...], gfox_ref[...]).astype(BF16)
    b = norm(ob_ref[...], gdil_ref[...]).astype(BF16)
    mix = (jnp.dot(a, wo_ref[0:D_GRP, :], preferred_element_type=F32)
           + jnp.dot(b, wo_ref[D_GRP:2 * D_GRP, :], preferred_element_type=F32))
    h = x_ref[...] + mix
    h_ref[...] = h
    hn = norm(h, gffn_ref[...])
    _store_row_tiles(hn_ref, hn)

    z = lax.dot_general(wr_ref[...], hn, (((1,), (1,)), ((), ())), preferred_element_type=F32,
                        precision=lax.Precision.HIGHEST) + br_ref[...]
    tm = z.shape[1]
    best = z[0:1, :]
    g_sel = jnp.zeros((1, tm), I32)
    for g in range(1, N_GROUPS):
        better = z[g:g + 1, :] > best
        g_sel = jnp.where(better, g, g_sel)
        best = jnp.maximum(best, z[g:g + 1, :])
    den = jnp.zeros((1, tm), F32)
    for g in range(N_GROUPS):
        den = den + jnp.exp(z[g:g + 1, :] - best)
    pg_top = 1.0 / den

    ze = jnp.zeros((EXPERTS_PER_GROUP, tm), F32)
    for g in range(N_GROUPS):
        ze = jnp.where(g_sel == g, z[8 + g * EXPERTS_PER_GROUP:8 + (g + 1) * EXPERTS_PER_GROUP, :], ze)
    e_iota = lax.broadcasted_iota(I32, ze.shape, 0)
    v1 = jnp.max(ze, axis=0, keepdims=True)
    i1 = jnp.min(jnp.where(ze == v1, e_iota, EXPERTS_PER_GROUP), axis=0, keepdims=True)
    ze2 = jnp.where(e_iota == i1, -jnp.inf, ze)
    v2 = jnp.max(ze2, axis=0, keepdims=True)
    i2 = jnp.min(jnp.where(ze2 == v2, e_iota, EXPERTS_PER_GROUP), axis=0, keepdims=True)
    e2 = jnp.exp(v2 - v1)
    inv = 1.0 / (1.0 + e2)
    gate1 = inv * pg_top
    gate2 = e2 * inv * pg_top
    eid1 = g_sel * EXPERTS_PER_GROUP + i1
    eid2 = g_sel * EXPERTS_PER_GROUP + i2

    x_iota = lax.broadcasted_iota(I32, (N_EXPERTS, tm), 0)
    hot1 = x_iota == eid1
    hot2 = x_iota == eid2
    multi = jnp.logical_or(hot1, hot2)
    before = jnp.dot(multi.astype(BF16), upper_ref[...], preferred_element_type=F32)
    slot = before + run_ref[:, 0:1]
    rank1 = jnp.sum(jnp.where(hot1, slot, 0.0), axis=0, keepdims=True)
    rank2 = jnp.sum(jnp.where(hot2, slot, 0.0), axis=0, keepdims=True)
    run_ref[...] = run_ref[...] + jnp.sum(multi.astype(F32), axis=1, keepdims=True)

    eid_ref[...] = jnp.concatenate([eid1, eid2], axis=0)
    gate_ref[...] = jnp.concatenate([gate1, gate2], axis=0)
    rank_ref[...] = jnp.concatenate([rank1, rank2], axis=0).astype(I32)
    cnt_ref[...] = run_ref[...].astype(I32)


def _outproj(oa, ob, x2, gfox, gdil, w_out, gffn, w_r, b_r, upper):
    T, D = x2.shape
    tm = TM_OUT
    const = lambda shape: pl.BlockSpec(shape, lambda i: (0,) * len(shape))
    tok = lambda w: pl.BlockSpec((tm, w), lambda i: (i, 0))
    lanes2 = pl.BlockSpec((2, tm), lambda i: (0, i))
    return pl.pallas_call(
        _outproj_kernel,
        grid=(T // tm,),
        in_specs=[tok(D_GRP), tok(D_GRP), tok(D), const((1, D_GRP)), const((1, D_GRP)), const((D, D)),
                  const((1, D)), const((ROUTER_ROWS, D)), const((ROUTER_ROWS, 1)), const((tm, tm))],
        out_specs=[tok(D), pl.BlockSpec((tm * ROW_CHUNKS, LANES), lambda i: (i, 0)),
                   lanes2, lanes2, lanes2, const((N_EXPERTS, LANES))],
        out_shape=[jax.ShapeDtypeStruct((T, D), F32), jax.ShapeDtypeStruct((T * ROW_CHUNKS, LANES), F32),
                   jax.ShapeDtypeStruct((2, T), I32), jax.ShapeDtypeStruct((2, T), F32),
                   jax.ShapeDtypeStruct((2, T), I32), jax.ShapeDtypeStruct((N_EXPERTS, LANES), I32)],
        scratch_shapes=[pltpu.VMEM((N_EXPERTS, LANES), F32)],
        compiler_params=_cparams(("arbitrary",)),
        name="outproj",
    )(oa, ob, x2, gfox, gdil, w_out, gffn, w_r, b_r, upper)


def _scatter_kernel(starts_ref, cnt_ref, pos_ref, hn_ref, xs_ref, ring, sems, zero_sem, *, n_steps):
    i = pl.program_id(0)
    tm = hn_ref.shape[0] // ROW_CHUNKS
    slot = lax.rem(i, 2)

    def wait_slot(s):
        for _ in range(2):
            pltpu.make_async_copy(ring.at[s], xs_ref.at[pl.ds(0, tm * ROW_CHUNKS), :], sems.at[s]).wait()

    @pl.when(i >= 2)
    def _():
        wait_slot(slot)

    ring[slot] = hn_ref[...]

    def start(r, _):
        for k in range(2):
            _row_tile_copy(ring.at[slot], r, xs_ref, pos_ref[k, r], sems.at[slot]).start()
        return 0

    lax.fori_loop(0, tm, start, 0, unroll=DMA_UNROLL)

    @pl.when(i == n_steps - 1)
    def _():
        wait_slot(slot)
        if n_steps > 1:
            wait_slot(1 - slot)
        ring[0] = jnp.zeros((tm * ROW_CHUNKS, LANES), F32)

        def pad_expert(e, _, wait):
            n_pad = lax.rem(TILE_E - lax.rem(cnt_ref[e], TILE_E), TILE_E)
            first = starts_ref[e] + cnt_ref[e]
            size = TILE_E // 2
            while size >= 1:
                row0 = first + (n_pad & ~(2 * size - 1))

                @pl.when((n_pad & size) != 0)
                def _(size=size, row0=row0):
                    copy = pltpu.make_async_copy(
                        ring.at[0, pl.ds(0, size * ROW_CHUNKS), :],
                        xs_ref.at[pl.ds(pl.multiple_of(row0 * ROW_CHUNKS, ROW_CHUNKS), size * ROW_CHUNKS), :],
                        zero_sem)
                    copy.wait() if wait else copy.start()

                size //= 2
            return 0

        lax.fori_loop(0, N_EXPERTS, functools.partial(pad_expert, wait=False), 0)
        last = N_EXPERTS - 1
        used_rows = starts_ref[last] + cnt_ref[last] + lax.rem(TILE_E - lax.rem(cnt_ref[last], TILE_E), TILE_E)
        n_tail = xs_ref.shape[0] // (tm * ROW_CHUNKS) - used_rows // tm

        def tail_copy(t):
            row0 = pl.multiple_of((used_rows + t * tm) * ROW_CHUNKS, tm * ROW_CHUNKS)
            return pltpu.make_async_copy(ring.at[0], xs_ref.at[pl.ds(row0, tm * ROW_CHUNKS), :], zero_sem)

        lax.fori_loop(0, n_tail, lambda t, c: (tail_copy(t).start(), c)[1], 0)
        lax.fori_loop(0, N_EXPERTS, functools.partial(pad_expert, wait=True), 0)
        lax.fori_loop(0, n_tail, lambda t, c: (tail_copy(t).wait(), c)[1], 0)


def _scatter(starts, cnt, pos, hn, n_rows):
    T = hn.shape[0] // ROW_CHUNKS
    tm = TM_ROWS
    assert TILE_E % tm == 0 and TILE_E // 2 <= tm
    n_steps = T // tm
    grid_spec = pltpu.PrefetchScalarGridSpec(
        num_scalar_prefetch=2,
        grid=(n_steps,),
        in_specs=[pl.BlockSpec((2, tm), lambda i, starts, cnt: (0, i), memory_space=pltpu.SMEM),
                  pl.BlockSpec((tm * ROW_CHUNKS, LANES), lambda i, starts, cnt: (i, 0))],
        out_specs=pl.BlockSpec(memory_space=pl.ANY),
        scratch_shapes=[pltpu.VMEM((2, tm * ROW_CHUNKS, LANES), F32),
                        pltpu.SemaphoreType.DMA((2,)), pltpu.SemaphoreType.DMA(())],
    )
    return pl.pallas_call(
        functools.partial(_scatter_kernel, n_steps=n_steps),
        grid_spec=grid_spec,
        out_shape=jax.ShapeDtypeStruct((n_rows * ROW_CHUNKS, LANES), F32),
        compiler_params=_cparams(("arbitrary",)),
        name="scatter_rows",
    )(starts, cnt, pos, hn)


def _experts_kernel(te_ref, tv_ref, xs_ref, wg_ref, wu_ref, wd_ref, y_ref, wg_bf, wu_bf, wd_bf):
    i = pl.program_id(0)
    new_expert = jnp.logical_or(i == 0, te_ref[i] != te_ref[jnp.maximum(i - 1, 0)])

    @pl.when(jnp.logical_and(tv_ref[i] != 0, new_expert))
    def _():
        wg_bf[...] = wg_ref[0].astype(BF16)
        wu_bf[...] = wu_ref[0].astype(BF16)
        wd_bf[...] = wd_ref[0].astype(BF16)

    @pl.when(tv_ref[i] != 0)
    def _():
        x = _load_row_tiles(xs_ref, TILE_E).astype(BF16)
        g = jnp.dot(x, wg_bf[...], preferred_element_type=F32)
        up = jnp.dot(x, wu_bf[...], preferred_element_type=F32)
        hmid = (g * jax.nn.sigmoid(g) * up).astype(BF16)
        _store_row_tiles(y_ref, jnp.dot(hmid, wd_bf[...], preferred_element_type=F32))

    @pl.when(tv_ref[i] == 0)
    def _():
        y_ref[...] = jnp.zeros_like(y_ref)


def _experts(tile_expert, tile_valid, xs, w_gate, w_up, w_down):
    n_tiles = tile_expert.shape[0]
    D = D_MODEL
    rows_spec = pl.BlockSpec((TILE_E * ROW_CHUNKS, LANES), lambda i, te, tv: (i, 0))
    grid_spec = pltpu.PrefetchScalarGridSpec(
        num_scalar_prefetch=2,
        grid=(n_tiles,),
        in_specs=[rows_spec,
                  pl.BlockSpec((1, D, D_EXPERT), lambda i, te, tv: (te[i], 0, 0)),
                  pl.BlockSpec((1, D, D_EXPERT), lambda i, te, tv: (te[i], 0, 0)),
                  pl.BlockSpec((1, D_EXPERT, D), lambda i, te, tv: (te[i], 0, 0))],
        out_specs=rows_spec,
        scratch_shapes=[pltpu.VMEM((D, D_EXPERT), BF16), pltpu.VMEM((D, D_EXPERT), BF16),
                        pltpu.VMEM((D_EXPERT, D), BF16)],
    )
    return pl.pallas_call(
        _experts_kernel,
        grid_spec=grid_spec,
        out_shape=jax.ShapeDtypeStruct(xs.shape, F32),
        compiler_params=_cparams(("arbitrary",)),
        name="experts",
    )(tile_expert, tile_valid, xs, w_gate, w_up, w_down)


def _combine_kernel(pos_ref, pos_next_ref, h_ref, gate_ref, y_ref, o_ref, ybuf, sems, *, n_steps):
    i = pl.program_id(0)
    tm = h_ref.shape[0]
    slot = lax.rem(i, 2)

    def gather(p_ref, s):
        def start(r, _):
            for k in range(2):
                _row_tile_copy(y_ref, p_ref[k, r], ybuf.at[s, k], r, sems.at[s]).start()
            return 0

        lax.fori_loop(0, tm, start, 0, unroll=DMA_UNROLL)

    @pl.when(i == 0)
    def _():
        gather(pos_ref, slot)

    @pl.when(i + 1 < n_steps)
    def _():
        gather(pos_next_ref, 1 - slot)

    for k in range(2):
        pltpu.make_async_copy(y_ref.at[pl.ds(0, tm * ROW_CHUNKS), :], ybuf.at[slot, k], sems.at[slot]).wait()
    g = gate_ref[...]
    o_ref[...] = (h_ref[...] + g[:, 0:1] * _load_row_tiles(ybuf.at[slot, 0], tm)
                  + g[:, 1:2] * _load_row_tiles(ybuf.at[slot, 1], tm))


def _combine(pos, h, gate_t, y):
    T, D = h.shape
    tm = TM_ROWS
    n_steps = T // tm
    return pl.pallas_call(
        functools.partial(_combine_kernel, n_steps=n_steps),
        grid=(n_steps,),
        in_specs=[pl.BlockSpec((2, tm), lambda i: (0, i), memory_space=pltpu.SMEM),
                  pl.BlockSpec((2, tm), lambda i: (0, jnp.minimum(i + 1, n_steps - 1)), memory_space=pltpu.SMEM),
                  pl.BlockSpec((tm, D), lambda i: (i, 0)),
                  pl.BlockSpec((tm, 2), lambda i: (i, 0)),
                  pl.BlockSpec(memory_space=pl.ANY)],
        out_specs=pl.BlockSpec((tm, D), lambda i: (i, 0)),
        out_shape=jax.ShapeDtypeStruct((T, D), F32),
        scratch_shapes=[pltpu.VMEM((2, 2, tm * ROW_CHUNKS, LANES), F32), pltpu.SemaphoreType.DMA((2,))],
        compiler_params=_cparams(("arbitrary",)),
        name="combine",
    )(pos, pos, h, gate_t, y)


def _rope_tables(S):
    inv_freq = 1.0 / (ROPE_THETA ** (jnp.arange(0, HEAD_DIM, 2, dtype=F32) / HEAD_DIM))
    ang = jnp.arange(S, dtype=F32)[:, None] * inv_freq[None, :]
    cos, sin = jnp.cos(ang), jnp.sin(ang)
    cos_t = jnp.tile(cos, (1, 2 * HEADS_PER_TILE))
    sin_t = jnp.tile(jnp.concatenate([-sin, sin], axis=1), (1, HEADS_PER_TILE))
    return cos_t, sin_t


def _layer(x, norm_mix, w_in, b_forget, q_norm_fox, k_norm_fox, q_norm_dil, k_norm_dil,
           out_norm_fox, out_norm_dil, w_out, norm_ffn, w_router_group, b_router_group,
           w_router_expert, b_router_expert, w_gate, w_up, w_down):
    B, S, D = x.shape
    T = B * S
    n_main = 6 * D_GRP

    w_main = w_in[:, :n_main].astype(BF16)
    w_vt = w_in[:, 2 * D_GRP:3 * D_GRP].T.astype(BF16)
    w_f = jnp.pad(w_in[:, n_main:], ((0, 0), (0, LANES - N_HEADS))).astype(BF16)
    b_f = jnp.pad(b_forget, (0, LANES - N_HEADS))[None, :]
    per_head = lambda g: jnp.tile(g, N_HEADS)[None, :]
    bd = jnp.kron(jnp.eye(N_HEADS, dtype=F32), jnp.ones((HEAD_DIM, HEAD_DIM), F32)).astype(BF16)
    cos_t, sin_t = _rope_tables(S)
    tri = jnp.tril(jnp.ones((TM_IN, TM_IN), F32)).astype(BF16)
    upper = jnp.triu(jnp.ones((TM_OUT, TM_OUT), F32), k=1).astype(BF16)
    w_r = jnp.concatenate([
        jnp.pad(w_router_group.T, ((0, 8 - N_GROUPS), (0, 0))),
        w_router_expert.transpose(0, 2, 1).reshape(N_EXPERTS, D)], axis=0)
    b_r = jnp.concatenate([jnp.pad(b_router_group, (0, 8 - N_GROUPS)), b_router_expert.reshape(-1)])[:, None]

    qa, ka, va, qb, kb, vb = _inproj(
        x, norm_mix[None, :], w_main, w_vt, w_f, b_f, per_head(q_norm_fox), per_head(k_norm_fox),
        per_head(q_norm_dil), per_head(k_norm_dil), bd, cos_t, sin_t, tri)
    oa = _fox(qa, ka, va)
    ob = _dilated(qb, kb, vb)

    h, hn, eid, gate, rank, cnt = _outproj(
        oa.reshape(T, D_GRP), ob.reshape(T, D_GRP), x.reshape(T, D), out_norm_fox[None, :],
        out_norm_dil[None, :], w_out.astype(BF16), norm_ffn[None, :], w_r, b_r, upper)

    counts = cnt[:, 0]
    padded = ((counts + TILE_E - 1) // TILE_E) * TILE_E
    ends = jnp.cumsum(padded)
    starts = ends - padded
    is_expert = eid[:, :, None] == jnp.arange(N_EXPERTS, dtype=I32)
    pos = jnp.sum(jnp.where(is_expert, starts, 0), axis=-1) + rank
    n_tiles = (2 * T) // TILE_E + N_EXPERTS
    tile_index = jnp.arange(n_tiles, dtype=I32)
    tile_valid = (tile_index * TILE_E < ends[-1]).astype(I32)
    tile_expert = jnp.minimum(
        jnp.sum((ends[None, :] <= (tile_index * TILE_E)[:, None]).astype(I32), axis=1), N_EXPERTS - 1)

    xs = _scatter(starts, counts, pos, hn, n_tiles * TILE_E)
    y = _experts(tile_expert, tile_valid, xs, w_gate, w_up, w_down)
    out = _combine(pos, h, gate.T, y)
    return out.reshape(B, S, D)


def kernel(x, norm_mix, w_in, b_forget, q_norm_fox, k_norm_fox, q_norm_dil, k_norm_dil, out_norm_fox,
           out_norm_dil, w_out, norm_ffn, w_router_group, b_router_group, w_router_expert,
           b_router_expert, w_gate, w_up, w_down):
    h = x
    for l in range(norm_mix.shape[0]):
        h = _layer(h, norm_mix[l], w_in[l], b_forget[l], q_norm_fox[l], k_norm_fox[l], q_norm_dil[l],
                   k_norm_dil[l], out_norm_fox[l], out_norm_dil[l], w_out[l], norm_ffn[l],
                   w_router_group[l], b_router_group[l], w_router_expert[l], b_router_expert[l],
                   w_gate[l], w_up[l], w_down[l])
    return h
```

```python
import functools
import math

import jax
import jax.numpy as jnp
from jax import lax
from jax.experimental import pallas as pl
from jax.experimental.pallas import tpu as pltpu

F32 = jnp.float32
BF16 = jnp.bfloat16
I32 = jnp.int32

D_MODEL = 1024
HEAD_DIM = 64
N_HEADS = 8
D_GRP = N_HEADS * HEAD_DIM
LANES = 128
HEADS_PER_TILE = LANES // HEAD_DIM
N_PAIRS = D_GRP // LANES
DIL_PATTERNS = ((128, 1), (512, 4), (2048, 16))
BLOCK = 128
ROPE_THETA = 10000.0
N_GROUPS = 4
EXPERTS_PER_GROUP = 8
N_EXPERTS = N_GROUPS * EXPERTS_PER_GROUP
D_EXPERT = 512
EPS = 1e-6
NEG = -1e30
LOG2E = 1.4426950408889634

TM_IN = 512
TQ = 1024
TK_WIDE = 1024
TK_SUB = 256
FOX_AHEAD = 4
FOX_BOUND_SLACK = 1.02
FOX_SAFE_SPAN = 100.0
DIL_SPAN = 2048
DIL_GROUP = 8
DIL_PRE = 4
TM_OUT = 512
TM_ROWS = 256
TILE_E = 512
ROUTER_ROWS = 8 + N_EXPERTS
ROW_CHUNKS = D_MODEL // LANES
DMA_UNROLL = 8
VMEM_LIMIT = 56 * 1024 * 1024


def _cparams(sem, flags=None):
    return pltpu.CompilerParams(dimension_semantics=sem, vmem_limit_bytes=VMEM_LIMIT, flags=flags)


def _inproj_kernel(x_ref, gmix_ref, w_ref, wvt_ref, wf_ref, bf_ref, shift_ref, gqa_ref, gka_ref, gqb_ref, gkb_ref,
                   bd_ref, cos_ref, sin_ref, tri_ref,
                   qa_ref, ka_ref, va_ref, qb_ref, kb_ref, vb_ref, carry_ref):
    @pl.when(pl.program_id(1) == 0)
    def _():
        carry_ref[...] = jnp.zeros_like(carry_ref)

    x = x_ref[0]
    ms = jnp.mean(x * x, axis=-1, keepdims=True)
    xn = (x * lax.rsqrt(ms + EPS) * gmix_ref[...]).astype(BF16)

    def seg(j):
        return jnp.dot(xn, w_ref[:, j * D_GRP:(j + 1) * D_GRP], preferred_element_type=F32)

    def head_norm(y, g_ref, scale):
        ss = jnp.dot((y * y).astype(BF16), bd_ref[...], preferred_element_type=F32) * (1.0 / HEAD_DIM)
        return y * lax.rsqrt(ss + EPS) * (g_ref[...] * scale)

    cos = cos_ref[...]
    sin = sin_ref[...]
    lane = lax.broadcasted_iota(I32, (x.shape[0], LANES), 1)
    first_half = (lane % HEAD_DIM) < (HEAD_DIM // 2)

    def rope(y):
        outs = []
        for j in range(N_PAIRS):
            ys = y[:, j * LANES:(j + 1) * LANES]
            partner = jnp.where(first_half, pltpu.roll(ys, LANES - HEAD_DIM // 2, 1),
                                pltpu.roll(ys, HEAD_DIM // 2, 1))
            outs.append(ys * cos + partner * sin)
        return jnp.concatenate(outs, axis=1)

    scale = 1.0 / math.sqrt(HEAD_DIM)
    va_ref[0] = lax.dot_general(wvt_ref[...], xn, (((1,), (1,)), ((), ())),
                                preferred_element_type=F32).astype(va_ref.dtype)
    qb_ref[0] = rope(head_norm(seg(3), gqb_ref, scale)).astype(qb_ref.dtype)
    kb_ref[0] = rope(head_norm(seg(4), gkb_ref, 1.0)).astype(kb_ref.dtype)
    vb_ref[0] = seg(5).astype(vb_ref.dtype)

    fa = jnp.dot(xn, wf_ref[...], preferred_element_type=F32) + bf_ref[...]
    logf = jnp.minimum(fa, 0.0) - jnp.log1p(jnp.exp(-jnp.abs(fa)))
    hi = logf.astype(BF16)
    mid = (logf - hi.astype(F32)).astype(BF16)
    lo = (logf - hi.astype(F32) - mid.astype(F32)).astype(BF16)
    parts = jnp.dot(tri_ref[...], jnp.concatenate([hi, mid, lo], axis=1), preferred_element_type=F32)
    c = parts[:, :LANES] + parts[:, LANES:2 * LANES] + parts[:, 2 * LANES:] + carry_ref[...]
    carry_ref[...] = c[c.shape[0] - 1:, :]

    qa = head_norm(seg(0), gqa_ref, scale * LOG2E)
    ka = head_norm(seg(1), gka_ref, 1.0)
    c2 = c * LOG2E
    ones = (jnp.where((lane >= HEAD_DIM + 3) & (lane < HEAD_DIM + 6), 1.0, 0.0)
            - jnp.where(lane == HEAD_DIM + 6, shift_ref[...], 0.0))
    for h in range(N_HEADS):
        cb = jnp.broadcast_to(c2[:, h:h + 1], (x.shape[0], LANES))
        hi = cb.astype(BF16).astype(F32)
        mid = (cb - hi).astype(BF16).astype(F32)
        lo = cb - hi - mid
        pieces = jnp.where(lane == HEAD_DIM, hi, jnp.where(lane == HEAD_DIM + 1, mid,
                           jnp.where(lane == HEAD_DIM + 2, lo, 0.0)))
        q_extra = pieces + ones
        k_extra = jnp.where(((lane >= HEAD_DIM) & (lane < HEAD_DIM + 3)) | (lane == HEAD_DIM + 6), 1.0, 0.0) \
            - pltpu.roll(pieces, 3, 1)
        j, odd = divmod(h, HEADS_PER_TILE)
        qp = qa[:, j * LANES:(j + 1) * LANES]
        kp = ka[:, j * LANES:(j + 1) * LANES]
        if odd:
            qp = pltpu.roll(qp, HEAD_DIM, 1)
            kp = pltpu.roll(kp, HEAD_DIM, 1)
        qa_ref[0, h] = jnp.where(lane < HEAD_DIM, qp, q_extra).astype(qa_ref.dtype)
        ka_ref[0, h] = jnp.where(lane < HEAD_DIM, kp, k_extra).astype(ka_ref.dtype)


def _inproj(x, gmix, w_main, w_vt, w_f, b_f, shift, gqa, gka, gqb, gkb, bd, cos_t, sin_t, tri):
    B, S, D = x.shape
    tm = TM_IN
    const = lambda shape: pl.BlockSpec(shape, lambda b, i: (0,) * len(shape))
    tok = lambda w, dt: jax.ShapeDtypeStruct((B, S, w), dt)
    tok_spec = lambda w: pl.BlockSpec((1, tm, w), lambda b, i: (b, i, 0))
    head_spec = pl.BlockSpec((1, N_HEADS, tm, LANES), lambda b, i: (b, 0, i, 0))
    head_shape = jax.ShapeDtypeStruct((B, N_HEADS, S, LANES), BF16)
    return pl.pallas_call(
        _inproj_kernel,
        grid=(B, S // tm),
        in_specs=[tok_spec(D), const((1, D)), const(w_main.shape), const(w_vt.shape), const(w_f.shape),
                  const((1, LANES)), const((1, LANES)),
                  const((1, D_GRP)), const((1, D_GRP)), const((1, D_GRP)), const((1, D_GRP)),
                  const((D_GRP, D_GRP)),
                  pl.BlockSpec((tm, LANES), lambda b, i: (i, 0)),
                  pl.BlockSpec((tm, LANES), lambda b, i: (i, 0)),
                  const((tm, tm))],
        out_specs=[head_spec, head_spec, pl.BlockSpec((1, D_GRP, tm), lambda b, i: (b, 0, i))]
        + [tok_spec(D_GRP)] * 3,
        out_shape=[head_shape, head_shape, jax.ShapeDtypeStruct((B, D_GRP, S), BF16),
                   tok(D_GRP, F32), tok(D_GRP, F32), tok(D_GRP, F32)],
        scratch_shapes=[pltpu.VMEM((1, LANES), F32)],
        compiler_params=_cparams(("arbitrary", "arbitrary")),
        name="inproj",
    )(x, gmix, w_main, w_vt, w_f, b_f, shift, gqa, gka, gqb, gkb, bd, cos_t, sin_t, tri)


def _fox_kernel(q_ref, k_ref, v_ref, o_ref, *, online):
    qi = pl.program_id(2)
    tq = q_ref.shape[2]

    def step(start, width, carry, diag):
        carry = list(carry)
        sub = min(TK_SUB, width)
        chunks = [(c, j) for c in range(width // sub) for j in range(HEADS_PER_TILE)]
        def score(c, j):
            k = k_ref[0, j, pl.ds(start + c * sub, sub), :]
            return lax.dot_general(k, q_ref[0, j], (((1,), (1,)), ((), ())), preferred_element_type=F32)

        scores = {cj: score(*cj) for cj in chunks[:FOX_AHEAD]}
        for n, (c, j) in enumerate(chunks):
                if n + FOX_AHEAD < len(chunks):
                    nxt = chunks[n + FOX_AHEAD]
                    scores[nxt] = score(*nxt)
                m, l, acc = carry[j]
                vt = v_ref[0, :, pl.ds(start + c * sub, sub)]
                s = scores.pop((c, j))
                if diag:
                    key = lax.broadcasted_iota(I32, (sub, tq), 0) + c * sub
                    qry = lax.broadcasted_iota(I32, (sub, tq), 1)
                    s = jnp.where(key <= qry, s, NEG)
                if online:
                    m_new = jnp.maximum(m, jnp.max(s, axis=0, keepdims=True))
                    alpha = jnp.exp2(m - m_new)
                    p = jnp.exp2(s - m_new)
                    l = alpha * l + jnp.sum(p, axis=0, keepdims=True)
                    acc = alpha * acc + jnp.dot(vt, p.astype(BF16), preferred_element_type=F32)
                else:
                    m_new = m
                    p = jnp.exp2(s)
                    l = l + jnp.sum(p, axis=0, keepdims=True)
                    acc = acc + jnp.dot(vt, p.astype(BF16), preferred_element_type=F32)
                carry[j] = (m_new, l, acc)
        return tuple(carry)

    init = tuple((jnp.full((1, tq), NEG, F32), jnp.zeros((1, tq), F32), jnp.zeros((LANES, tq), F32))
                 for _ in range(HEADS_PER_TILE))
    per_wide = TK_WIDE // tq
    n_wide = qi // per_wide
    carry = lax.fori_loop(
        0, n_wide, lambda i, c: step(pl.multiple_of(i * TK_WIDE, TK_WIDE), TK_WIDE, c, False), init)
    for extra in range(per_wide - 1):
        carry = lax.cond(qi - n_wide * per_wide > extra,
                         lambda c, e=extra: step(pl.multiple_of((n_wide * per_wide + e) * tq, tq), tq, c, False),
                         lambda c: c, carry)
    carry = step(pl.multiple_of(qi * tq, tq), tq, carry, True)
    outs = [acc / l for (_, l, acc) in carry]
    feat = lax.broadcasted_iota(I32, (LANES, tq), 0)
    o_ref[0] = jnp.where(feat < HEAD_DIM, outs[0], outs[1]).T.astype(o_ref.dtype)


def _fox(qa, ka, va_t, online):
    B, _, S, _ = qa.shape
    return pl.pallas_call(
        functools.partial(_fox_kernel, online=online),
        grid=(B, N_PAIRS, S // TQ),
        in_specs=[pl.BlockSpec((1, HEADS_PER_TILE, TQ, LANES), lambda b, hp, i: (b, hp, i, 0)),
                  pl.BlockSpec((1, HEADS_PER_TILE, S, LANES), lambda b, hp, i: (b, hp, 0, 0)),
                  pl.BlockSpec((1, LANES, S), lambda b, hp, i: (b, hp, 0))],
        out_specs=pl.BlockSpec((1, TQ, LANES), lambda b, hp, i: (b, i, hp)),
        out_shape=jax.ShapeDtypeStruct((B, S, D_GRP), F32),
        compiler_params=_cparams(("arbitrary", "arbitrary", "arbitrary")),
        name="fox_online" if online else "fox",
    )(qa, ka, va_t)


def _dilated_kernel(q_ref, kp_ref, kc_ref, vp_ref, vc_ref, o_ref, qq, kk, vv, qq4, kk4, vv4, osc, lsc):
    u = pl.program_id(1)
    span = q_ref.shape[1]
    qq[...] = q_ref[0]
    kk[0:span, :] = kp_ref[0]
    kk[span:2 * span, :] = kc_ref[0]
    vv[0:span, :] = vp_ref[0]
    vv[span:2 * span, :] = vc_ref[0]
    for src, dst in ((qq, qq4), (kk, kk4), (vv, vv4)):
        part = src.shape[0] // DIL_PRE
        for a in range(DIL_PRE):
            dst[a * part:(a + 1) * part, :] = src[pl.ds(a, part, stride=DIL_PRE), :]

    def rows(buf, buf4, start, n, d):
        if d % DIL_PRE:
            return buf[pl.ds(start, n, stride=d), :]
        part = buf4.shape[0] // DIL_PRE
        a = lax.rem(start, DIL_PRE)
        return buf4[pl.ds(a * part + lax.div(start, DIL_PRE), n, stride=d // DIL_PRE), :]

    lane = lax.broadcasted_iota(I32, (BLOCK, LANES), 1)
    ql = lax.broadcasted_iota(I32, (BLOCK, 2 * BLOCK), 0)
    kl = lax.broadcasted_iota(I32, (BLOCK, 2 * BLOCK), 1)
    dist = ql + BLOCK - kl
    band = (dist >= 0) & (dist <= BLOCK)
    bias = jnp.where(band, 0.0, NEG)
    bias_first = jnp.where(band & (kl >= BLOCK), 0.0, NEG)

    def scores(q_start, k_start, d, first):
        qs = rows(qq, qq4, q_start, BLOCK, d).astype(BF16)
        ks = rows(kk, kk4, k_start, 2 * BLOCK, d).astype(BF16)
        mask = jnp.where(first, bias_first, bias)
        out = []
        for j in range(HEADS_PER_TILE):
            qj = jnp.where(lane // HEAD_DIM == j, qs, jnp.zeros_like(qs))
            out.append(lax.dot_general(qj, ks, (((1,), (1,)), ((), ())), preferred_element_type=F32) + mask)
        return out

    def finish(s_heads, k_start, d):
        vs = rows(vv, vv4, k_start, 2 * BLOCK, d).astype(BF16)
        o_heads, lse_heads = [], []
        for s in s_heads:
            m = jnp.max(s, axis=-1, keepdims=True)
            p = jnp.exp(s - m)
            l = jnp.sum(p, axis=-1, keepdims=True)
            o_heads.append(jnp.dot((p / l).astype(BF16), vs, preferred_element_type=F32))
            lse_heads.append(m + jnp.log(l))
        o = jnp.where(lane < HEAD_DIM, o_heads[0], o_heads[1])
        lse = jnp.where(lane < HEAD_DIM, lse_heads[0], lse_heads[1])
        return o, lse

    for pidx, (window, d) in enumerate(DIL_PATTERNS):
        assert window // d == BLOCK
        unit = d * BLOCK
        n_problems = (span // unit) * d
        assert n_problems % DIL_GROUP == 0

        def body(g, _, pidx=pidx, d=d, unit=unit):
            starts, s_all = [], []
            for t in range(DIL_GROUP):
                idx = g * DIL_GROUP + t
                w = idx // d
                q_start = w * unit + (idx - w * d)
                k_start = span - unit + q_start
                starts.append((q_start, k_start))
                s_all.append(scores(q_start, k_start, d, jnp.logical_and(u == 0, w == 0)))
            for (q_start, k_start), s_heads in zip(starts, s_all):
                o, lse = finish(s_heads, k_start, d)
                osc[pidx, pl.ds(q_start, BLOCK, stride=d), :] = o
                lsc[pidx, pl.ds(q_start, BLOCK, stride=d), :] = lse
            return 0

        lax.fori_loop(0, n_problems // DIL_GROUP, body, 0)

    mx = jnp.maximum(jnp.maximum(lsc[0], lsc[1]), lsc[2])
    num = jnp.zeros((span, LANES), F32)
    den = jnp.zeros((span, LANES), F32)
    for pidx in range(len(DIL_PATTERNS)):
        e = jnp.exp(lsc[pidx] - mx)
        num = num + e * osc[pidx]
        den = den + e
    o_ref[0] = (num / den).astype(o_ref.dtype)


def _dilated(qb, kb, vb):
    B, S, _ = qb.shape
    span = DIL_SPAN
    cur = pl.BlockSpec((1, span, LANES), lambda b, u, hp: (b, u, hp))
    prev = pl.BlockSpec((1, span, LANES), lambda b, u, hp: (b, jnp.maximum(u - 1, 0), hp))
    return pl.pallas_call(
        _dilated_kernel,
        grid=(B, S // span, N_PAIRS),
        in_specs=[cur, prev, cur, prev, cur],
        out_specs=cur,
        out_shape=jax.ShapeDtypeStruct((B, S, D_GRP), F32),
        scratch_shapes=[pltpu.VMEM((span, LANES), F32),
                        pltpu.VMEM((2 * span, LANES), F32), pltpu.VMEM((2 * span, LANES), F32),
                        pltpu.VMEM((span, LANES), F32),
                        pltpu.VMEM((2 * span, LANES), F32), pltpu.VMEM((2 * span, LANES), F32),
                        pltpu.VMEM((len(DIL_PATTERNS), span, LANES), F32),
                        pltpu.VMEM((len(DIL_PATTERNS), span, LANES), F32)],
        compiler_params=_cparams(("arbitrary", "arbitrary", "arbitrary")),
        name="dilated",
    )(qb, kb, kb, vb, vb)


def _store_row_tiles(ref, x):
    n = x.shape[0]
    for c in range(ROW_CHUNKS):
        ref[pl.ds(c, n, stride=ROW_CHUNKS), :] = x[:, c * LANES:(c + 1) * LANES]


def _load_row_tiles(ref, n):
    return jnp.concatenate([ref[pl.ds(c, n, stride=ROW_CHUNKS), :] for c in range(ROW_CHUNKS)], axis=1)


def _row_tile_copy(src_ref, src_row, dst_ref, dst_row, sem):
    src = src_ref.at[pl.ds(pl.multiple_of(src_row * ROW_CHUNKS, ROW_CHUNKS), ROW_CHUNKS), :]
    dst = dst_ref.at[pl.ds(pl.multiple_of(dst_row * ROW_CHUNKS, ROW_CHUNKS), ROW_CHUNKS), :]
    return pltpu.make_async_copy(src, dst, sem)


def _outproj_kernel(oa_ref, ob_ref, x_ref, gfox_ref, gdil_ref, wo_ref, gffn_ref, wr_ref, br_ref, upper_ref,
                    h_ref, hn_ref, eid_ref, gate_ref, rank_ref, cnt_ref, run_ref):
    @pl.when(pl.program_id(0) == 0)
    def _():
        run_ref[...] = jnp.zeros_like(run_ref)

    def norm(y, g):
        ms = jnp.mean(y * y, axis=-1, keepdims=True)
        return y * lax.rsqrt(ms + EPS) * g

    a = norm(oa_ref[...], gfox_ref[...]).astype(BF16)
    b = norm(ob_ref[...], gdil_ref[...]).astype(BF16)
    mix = (jnp.dot(a, wo_ref[0:D_GRP, :], preferred_element_type=F32)
           + jnp.dot(b, wo_ref[D_GRP:2 * D_GRP, :], preferred_element_type=F32))
    h = x_ref[...] + mix
    h_ref[...] = h
    hn = norm(h, gffn_ref[...])
    _store_row_tiles(hn_ref, hn)

    z = lax.dot_general(wr_ref[...], hn, (((1,), (1,)), ((), ())), preferred_element_type=F32,
                        precision=lax.Precision.HIGHEST) + br_ref[...]
    tm = z.shape[1]
    best = z[0:1, :]
    g_sel = jnp.zeros((1, tm), I32)
    for g in range(1, N_GROUPS):
        better = z[g:g + 1, :] > best
        g_sel = jnp.where(better, g, g_sel)
        best = jnp.maximum(best, z[g:g + 1, :])
    den = jnp.zeros((1, tm), F32)
    for g in range(N_GROUPS):
        den = den + jnp.exp(z[g:g + 1, :] - best)
    pg_top = 1.0 / den

    ze = jnp.zeros((EXPERTS_PER_GROUP, tm), F32)
    for g in range(N_GROUPS):
        ze = jnp.where(g_sel == g, z[8 + g * EXPERTS_PER_GROUP:8 + (g + 1) * EXPERTS_PER_GROUP, :], ze)
    e_iota = lax.broadcasted_iota(I32, ze.shape, 0)
    v1 = jnp.max(ze, axis=0, keepdims=True)
    i1 = jnp.min(jnp.where(ze == v1, e_iota, EXPERTS_PER_GROUP), axis=0, keepdims=True)
    ze2 = jnp.where(e_iota == i1, -jnp.inf, ze)
    v2 = jnp.max(ze2, axis=0, keepdims=True)
    i2 = jnp.min(jnp.where(ze2 == v2, e_iota, EXPERTS_PER_GROUP), axis=0, keepdims=True)
    e2 = jnp.exp(v2 - v1)
    inv = 1.0 / (1.0 + e2)
    gate1 = inv * pg_top
    gate2 = e2 * inv * pg_top
    eid1 = g_sel * EXPERTS_PER_GROUP + i1
    eid2 = g_sel * EXPERTS_PER_GROUP + i2

    x_iota = lax.broadcasted_iota(I32, (N_EXPERTS, tm), 0)
    hot1 = x_iota == eid1
    hot2 = x_iota == eid2
    multi = jnp.logical_or(hot1, hot2)
    before = jnp.dot(multi.astype(BF16), upper_ref[...], preferred_element_type=F32)
    slot = before + run_ref[:, 0:1]
    rank1 = jnp.sum(jnp.where(hot1, slot, 0.0), axis=0, keepdims=True)
    rank2 = jnp.sum(jnp.where(hot2, slot, 0.0), axis=0, keepdims=True)
    run_ref[...] = run_ref[...] + jnp.sum(multi.astype(F32), axis=1, keepdims=True)

    eid_ref[...] = jnp.concatenate([eid1, eid2], axis=0)
    gate_ref[...] = jnp.concatenate([gate1, gate2], axis=0)
    rank_ref[...] = jnp.concatenate([rank1, rank2], axis=0).astype(I32)
    cnt_ref[...] = run_ref[...].astype(I32)


def _outproj(oa, ob, x2, gfox, gdil, w_out, gffn, w_r, b_r, upper):
    T, D = x2.shape
    tm = TM_OUT
    const = lambda shape: pl.BlockSpec(shape, lambda i: (0,) * len(shape))
    tok = lambda w: pl.BlockSpec((tm, w), lambda i: (i, 0))
    lanes2 = pl.BlockSpec((2, tm), lambda i: (0, i))
    return pl.pallas_call(
        _outproj_kernel,
        grid=(T // tm,),
        in_specs=[tok(D_GRP), tok(D_GRP), tok(D), const((1, D_GRP)), const((1, D_GRP)), const((D, D)),
                  const((1, D)), const((ROUTER_ROWS, D)), const((ROUTER_ROWS, 1)), const((tm, tm))],
        out_specs=[tok(D), pl.BlockSpec((tm * ROW_CHUNKS, LANES), lambda i: (i, 0)),
                   lanes2, lanes2, lanes2, const((N_EXPERTS, LANES))],
        out_shape=[jax.ShapeDtypeStruct((T, D), F32), jax.ShapeDtypeStruct((T * ROW_CHUNKS, LANES), F32),
                   jax.ShapeDtypeStruct((2, T), I32), jax.ShapeDtypeStruct((2, T), F32),
                   jax.ShapeDtypeStruct((2, T), I32), jax.ShapeDtypeStruct((N_EXPERTS, LANES), I32)],
        scratch_shapes=[pltpu.VMEM((N_EXPERTS, LANES), F32)],
        compiler_params=_cparams(("arbitrary",)),
        name="outproj",
    )(oa, ob, x2, gfox, gdil, w_out, gffn, w_r, b_r, upper)


def _scatter_kernel(starts_ref, cnt_ref, pos_ref, hn_ref, xs_ref, ring, sems, zero_sem, *, n_steps):
    i = pl.program_id(0)
    tm = hn_ref.shape[0] // ROW_CHUNKS
    slot = lax.rem(i, 2)

    def wait_slot(s):
        for _ in range(2):
            pltpu.make_async_copy(ring.at[s], xs_ref.at[pl.ds(0, tm * ROW_CHUNKS), :], sems.at[s]).wait()

    @pl.when(i >= 2)
    def _():
        wait_slot(slot)

    ring[slot] = hn_ref[...]

    def start(r, _):
        for k in range(2):
            _row_tile_copy(ring.at[slot], r, xs_ref, pos_ref[k, r], sems.at[slot]).start()
        return 0

    lax.fori_loop(0, tm, start, 0, unroll=DMA_UNROLL)

    @pl.when(i == n_steps - 1)
    def _():
        wait_slot(slot)
        if n_steps > 1:
            wait_slot(1 - slot)
        ring[0] = jnp.zeros((tm * ROW_CHUNKS, LANES), F32)

        def pad_expert(e, _, wait):
            n_pad = lax.rem(TILE_E - lax.rem(cnt_ref[e], TILE_E), TILE_E)
            first = starts_ref[e] + cnt_ref[e]
            size = TILE_E // 2
            while size >= 1:
                row0 = first + (n_pad & ~(2 * size - 1))

                @pl.when((n_pad & size) != 0)
                def _(size=size, row0=row0):
                    copy = pltpu.make_async_copy(
                        ring.at[0, pl.ds(0, size * ROW_CHUNKS), :],
                        xs_ref.at[pl.ds(pl.multiple_of(row0 * ROW_CHUNKS, ROW_CHUNKS), size * ROW_CHUNKS), :],
                        zero_sem)
                    copy.wait() if wait else copy.start()

                size //= 2
            return 0

        lax.fori_loop(0, N_EXPERTS, functools.partial(pad_expert, wait=False), 0)
        last = N_EXPERTS - 1
        used_rows = starts_ref[last] + cnt_ref[last] + lax.rem(TILE_E - lax.rem(cnt_ref[last], TILE_E), TILE_E)
        n_tail = xs_ref.shape[0] // (tm * ROW_CHUNKS) - used_rows // tm

        def tail_copy(t):
            row0 = pl.multiple_of((used_rows + t * tm) * ROW_CHUNKS, tm * ROW_CHUNKS)
            return pltpu.make_async_copy(ring.at[0], xs_ref.at[pl.ds(row0, tm * ROW_CHUNKS), :], zero_sem)

        lax.fori_loop(0, n_tail, lambda t, c: (tail_copy(t).start(), c)[1], 0)
        lax.fori_loop(0, N_EXPERTS, functools.partial(pad_expert, wait=True), 0)
        lax.fori_loop(0, n_tail, lambda t, c: (tail_copy(t).wait(), c)[1], 0)


def _scatter(starts, cnt, pos, hn, n_rows):
    T = hn.shape[0] // ROW_CHUNKS
    tm = TM_ROWS
    assert TILE_E % tm == 0 and TILE_E // 2 <= tm
    n_steps = T // tm
    grid_spec = pltpu.PrefetchScalarGridSpec(
        num_scalar_prefetch=2,
        grid=(n_steps,),
        in_specs=[pl.BlockSpec((2, tm), lambda i, starts, cnt: (0, i), memory_space=pltpu.SMEM),
                  pl.BlockSpec((tm * ROW_CHUNKS, LANES), lambda i, starts, cnt: (i, 0))],
        out_specs=pl.BlockSpec(memory_space=pl.ANY),
        scratch_shapes=[pltpu.VMEM((2, tm * ROW_CHUNKS, LANES), F32),
                        pltpu.SemaphoreType.DMA((2,)), pltpu.SemaphoreType.DMA(())],
    )
    return pl.pallas_call(
        functools.partial(_scatter_kernel, n_steps=n_steps),
        grid_spec=grid_spec,
        out_shape=jax.ShapeDtypeStruct((n_rows * ROW_CHUNKS, LANES), F32),
        compiler_params=_cparams(("arbitrary",)),
        name="scatter_rows",
    )(starts, cnt, pos, hn)


def _experts_kernel(te_ref, tv_ref, xs_ref, wg_ref, wu_ref, wd_ref, y_ref, wg_bf, wu_bf, wd_bf):
    i = pl.program_id(0)
    new_expert = jnp.logical_or(i == 0, te_ref[i] != te_ref[jnp.maximum(i - 1, 0)])

    @pl.when(jnp.logical_and(tv_ref[i] != 0, new_expert))
    def _():
        wg_bf[...] = wg_ref[0].astype(BF16)
        wu_bf[...] = wu_ref[0].astype(BF16)
        wd_bf[...] = wd_ref[0].astype(BF16)

    @pl.when(tv_ref[i] != 0)
    def _():
        x = _load_row_tiles(xs_ref, TILE_E).astype(BF16)
        g = jnp.dot(x, wg_bf[...], preferred_element_type=F32)
        up = jnp.dot(x, wu_bf[...], preferred_element_type=F32)
        hmid = (g * jax.nn.sigmoid(g) * up).astype(BF16)
        _store_row_tiles(y_ref, jnp.dot(hmid, wd_bf[...], preferred_element_type=F32))

    @pl.when(tv_ref[i] == 0)
    def _():
        y_ref[...] = jnp.zeros_like(y_ref)


def _experts(tile_expert, tile_valid, xs, w_gate, w_up, w_down):
    n_tiles = tile_expert.shape[0]
    D = D_MODEL
    rows_spec = pl.BlockSpec((TILE_E * ROW_CHUNKS, LANES), lambda i, te, tv: (i, 0))
    grid_spec = pltpu.PrefetchScalarGridSpec(
        num_scalar_prefetch=2,
        grid=(n_tiles,),
        in_specs=[rows_spec,
                  pl.BlockSpec((1, D, D_EXPERT), lambda i, te, tv: (te[i], 0, 0)),
                  pl.BlockSpec((1, D, D_EXPERT), lambda i, te, tv: (te[i], 0, 0)),
                  pl.BlockSpec((1, D_EXPERT, D), lambda i, te, tv: (te[i], 0, 0))],
        out_specs=rows_spec,
        scratch_shapes=[pltpu.VMEM((D, D_EXPERT), BF16), pltpu.VMEM((D, D_EXPERT), BF16),
                        pltpu.VMEM((D_EXPERT, D), BF16)],
    )
    return pl.pallas_call(
        _experts_kernel,
        grid_spec=grid_spec,
        out_shape=jax.ShapeDtypeStruct(xs.shape, F32),
        compiler_params=_cparams(("arbitrary",)),
        name="experts",
    )(tile_expert, tile_valid, xs, w_gate, w_up, w_down)


def _combine_kernel(pos_ref, pos_next_ref, h_ref, gate_ref, y_ref, o_ref, ybuf, sems, *, n_steps):
    i = pl.program_id(0)
    tm = h_ref.shape[0]
    slot = lax.rem(i, 2)

    def gather(p_ref, s):
        def start(r, _):
            for k in range(2):
                _row_tile_copy(y_ref, p_ref[k, r], ybuf.at[s, k], r, sems.at[s]).start()
            return 0

        lax.fori_loop(0, tm, start, 0, unroll=DMA_UNROLL)

    @pl.when(i == 0)
    def _():
        gather(pos_ref, slot)

    @pl.when(i + 1 < n_steps)
    def _():
        gather(pos_next_ref, 1 - slot)

    for k in range(2):
        pltpu.make_async_copy(y_ref.at[pl.ds(0, tm * ROW_CHUNKS), :], ybuf.at[slot, k], sems.at[slot]).wait()
    g = gate_ref[...]
    o_ref[...] = (h_ref[...] + g[:, 0:1] * _load_row_tiles(ybuf.at[slot, 0], tm)
                  + g[:, 1:2] * _load_row_tiles(ybuf.at[slot, 1], tm))


def _combine(pos, h, gate_t, y):
    T, D = h.shape
    tm = TM_ROWS
    n_steps = T // tm
    return pl.pallas_call(
        functools.partial(_combine_kernel, n_steps=n_steps),
        grid=(n_steps,),
        in_specs=[pl.BlockSpec((2, tm), lambda i: (0, i), memory_space=pltpu.SMEM),
                  pl.BlockSpec((2, tm), lambda i: (0, jnp.minimum(i + 1, n_steps - 1)), memory_space=pltpu.SMEM),
                  pl.BlockSpec((tm, D), lambda i: (i, 0)),
                  pl.BlockSpec((tm, 2), lambda i: (i, 0)),
                  pl.BlockSpec(memory_space=pl.ANY)],
        out_specs=pl.BlockSpec((tm, D), lambda i: (i, 0)),
        out_shape=jax.ShapeDtypeStruct((T, D), F32),
        scratch_shapes=[pltpu.VMEM((2, 2, tm * ROW_CHUNKS, LANES), F32), pltpu.SemaphoreType.DMA((2,))],
        compiler_params=_cparams(("arbitrary",)),
        name="combine",
    )(pos, pos, h, gate_t, y)


def _rope_tables(S):
    inv_freq = 1.0 / (ROPE_THETA ** (jnp.arange(0, HEAD_DIM, 2, dtype=F32) / HEAD_DIM))
    ang = jnp.arange(S, dtype=F32)[:, None] * inv_freq[None, :]
    cos, sin = jnp.cos(ang), jnp.sin(ang)
    cos_t = jnp.tile(cos, (1, 2 * HEADS_PER_TILE))
    sin_t = jnp.tile(jnp.concatenate([-sin, sin], axis=1), (1, HEADS_PER_TILE))
    return cos_t, sin_t


def _layer(x, norm_mix, w_in, b_forget, q_norm_fox, k_norm_fox, q_norm_dil, k_norm_dil,
           out_norm_fox, out_norm_dil, w_out, norm_ffn, w_router_group, b_router_group,
           w_router_expert, b_router_expert, w_gate, w_up, w_down):
    B, S, D = x.shape
    T = B * S
    n_main = 6 * D_GRP

    w_main = w_in[:, :n_main].astype(BF16)
    w_vt = w_in[:, 2 * D_GRP:3 * D_GRP].T.astype(BF16)
    w_f = jnp.pad(w_in[:, n_main:], ((0, 0), (0, LANES - N_HEADS))).astype(BF16)
    b_f = jnp.pad(b_forget, (0, LANES - N_HEADS))[None, :]
    per_head = lambda g: jnp.tile(g, N_HEADS)[None, :]
    bd = jnp.kron(jnp.eye(N_HEADS, dtype=F32), jnp.ones((HEAD_DIM, HEAD_DIM), F32)).astype(BF16)
    cos_t, sin_t = _rope_tables(S)
    tri = jnp.tril(jnp.ones((TM_IN, TM_IN), F32)).astype(BF16)
    upper = jnp.triu(jnp.ones((TM_OUT, TM_OUT), F32), k=1).astype(BF16)
    w_r = jnp.concatenate([
        jnp.pad(w_router_group.T, ((0, 8 - N_GROUPS), (0, 0))),
        w_router_expert.transpose(0, 2, 1).reshape(N_EXPERTS, D)], axis=0)
    b_r = jnp.concatenate([jnp.pad(b_router_group, (0, 8 - N_GROUPS)), b_router_expert.reshape(-1)])[:, None]

    bound = (HEAD_DIM / math.sqrt(HEAD_DIM)) * LOG2E * jnp.max(jnp.abs(q_norm_fox)) * jnp.max(jnp.abs(k_norm_fox))
    shift = FOX_BOUND_SLACK * bound + 1.0
    qa, ka, va, qb, kb, vb = _inproj(
        x, norm_mix[None, :], w_main, w_vt, w_f, b_f, jnp.full((1, LANES), shift, F32),
        per_head(q_norm_fox), per_head(k_norm_fox),
        per_head(q_norm_dil), per_head(k_norm_dil), bd, cos_t, sin_t, tri)
    oa = lax.cond(2.0 * shift <= FOX_SAFE_SPAN,
                  lambda: _fox(qa, ka, va, online=False), lambda: _fox(qa, ka, va, online=True))
    ob = _dilated(qb, kb, vb)

    h, hn, eid, gate, rank, cnt = _outproj(
        oa.reshape(T, D_GRP), ob.reshape(T, D_GRP), x.reshape(T, D), out_norm_fox[None, :],
        out_norm_dil[None, :], w_out.astype(BF16), norm_ffn[None, :], w_r, b_r, upper)

    counts = cnt[:, 0]
    padded = ((counts + TILE_E - 1) // TILE_E) * TILE_E
    ends = jnp.cumsum(padded)
    starts = ends - padded
    is_expert = eid[:, :, None] == jnp.arange(N_EXPERTS, dtype=I32)
    pos = jnp.sum(jnp.where(is_expert, starts, 0), axis=-1) + rank
    n_tiles = (2 * T) // TILE_E + N_EXPERTS
    tile_index = jnp.arange(n_tiles, dtype=I32)
    tile_valid = (tile_index * TILE_E < ends[-1]).astype(I32)
    tile_expert = jnp.minimum(
        jnp.sum((ends[None, :] <= (tile_index * TILE_E)[:, None]).astype(I32), axis=1), N_EXPERTS - 1)

    xs = _scatter(starts, counts, pos, hn, n_tiles * TILE_E)
    y = _experts(tile_expert, tile_valid, xs, w_gate, w_up, w_down)
    out = _combine(pos, h, gate.T, y)
    return out.reshape(B, S, D)


def kernel(x, norm_mix, w_in, b_forget, q_norm_fox, k_norm_fox, q_norm_dil, k_norm_dil, out_norm_fox,
           out_norm_dil, w_out, norm_ffn, w_router_group, b_router_group, w_router_expert,
           b_router_expert, w_gate, w_up, w_down):
    h = x
    for l in range(norm_mix.shape[0]):
        h = _layer(h, norm_mix[l], w_in[l], b_forget[l], q_norm_fox[l], k_norm_fox[l], q_norm_dil[l],
                   k_norm_dil[l], out_norm_fox[l], out_norm_dil[l], w_out[l], norm_ffn[l],
                   w_router_group[l], b_router_group[l], w_router_expert[l], b_router_expert[l],
                   w_gate[l], w_up[l], w_down[l])
    return h
```

```python
import functools
import math

import jax
import jax.numpy as jnp
from jax import lax
from jax.experimental import pallas as pl
from jax.experimental.pallas import tpu as pltpu

F32 = jnp.float32
BF16 = jnp.bfloat16
I32 = jnp.int32

D_MODEL = 1024
HEAD_DIM = 64
N_HEADS = 8
D_GRP = N_HEADS * HEAD_DIM
LANES = 128
HEADS_PER_TILE = LANES // HEAD_DIM
N_PAIRS = D_GRP // LANES
DIL_PATTERNS = ((128, 1), (512, 4), (2048, 16))
BLOCK = 128
ROPE_THETA = 10000.0
N_GROUPS = 4
EXPERTS_PER_GROUP = 8
N_EXPERTS = N_GROUPS * EXPERTS_PER_GROUP
D_EXPERT = 512
EPS = 1e-6
NEG = -1e30
LOG2E = 1.4426950408889634

TM_IN = 512
TQ = 1024
TK_WIDE = 1024
TK_SUB = 256
FOX_AHEAD = 4
FOX_BOUND_SLACK = 1.02
DIL_SAFE_SPAN = 69.0
FOX_SAFE_SPAN = 100.0
DIL_SPAN = 2048
DIL_GROUP = 8
DIL_PRE = 4
TM_OUT = 512
TM_ROWS = 256
TILE_E = 512
ROUTER_ROWS = 8 + N_EXPERTS
ROW_CHUNKS = D_MODEL // LANES
DMA_UNROLL = 8
VMEM_LIMIT = 56 * 1024 * 1024


def _cparams(sem, flags=None):
    return pltpu.CompilerParams(dimension_semantics=sem, vmem_limit_bytes=VMEM_LIMIT, flags=flags)


def _inproj_kernel(x_ref, gmix_ref, w_ref, wvt_ref, wf_ref, bf_ref, shift_ref, gqa_ref, gka_ref, gqb_ref, gkb_ref,
                   bd_ref, cos_ref, sin_ref, tri_ref,
                   qa_ref, ka_ref, va_ref, qb_ref, kb_ref, vb_ref, carry_ref):
    @pl.when(pl.program_id(1) == 0)
    def _():
        carry_ref[...] = jnp.zeros_like(carry_ref)

    x = x_ref[0]
    ms = jnp.mean(x * x, axis=-1, keepdims=True)
    xn = (x * lax.rsqrt(ms + EPS) * gmix_ref[...]).astype(BF16)

    def seg(j):
        return jnp.dot(xn, w_ref[:, j * D_GRP:(j + 1) * D_GRP], preferred_element_type=F32)

    def head_norm(y, g_ref, scale):
        ss = jnp.dot((y * y).astype(BF16), bd_ref[...], preferred_element_type=F32) * (1.0 / HEAD_DIM)
        return y * lax.rsqrt(ss + EPS) * (g_ref[...] * scale)

    cos = cos_ref[...]
    sin = sin_ref[...]
    lane = lax.broadcasted_iota(I32, (x.shape[0], LANES), 1)
    first_half = (lane % HEAD_DIM) < (HEAD_DIM // 2)

    def rope(y):
        outs = []
        for j in range(N_PAIRS):
            ys = y[:, j * LANES:(j + 1) * LANES]
            partner = jnp.where(first_half, pltpu.roll(ys, LANES - HEAD_DIM // 2, 1),
                                pltpu.roll(ys, HEAD_DIM // 2, 1))
            outs.append(ys * cos + partner * sin)
        return jnp.concatenate(outs, axis=1)

    scale = 1.0 / math.sqrt(HEAD_DIM)
    va_ref[0] = lax.dot_general(wvt_ref[...], xn, (((1,), (1,)), ((), ())),
                                preferred_element_type=F32).astype(va_ref.dtype)
    qb_ref[0] = rope(head_norm(seg(3), gqb_ref, scale)).astype(qb_ref.dtype)
    kb_ref[0] = rope(head_norm(seg(4), gkb_ref, 1.0)).astype(kb_ref.dtype)
    vb_ref[0] = seg(5).astype(vb_ref.dtype)

    fa = jnp.dot(xn, wf_ref[...], preferred_element_type=F32) + bf_ref[...]
    logf = jnp.minimum(fa, 0.0) - jnp.log1p(jnp.exp(-jnp.abs(fa)))
    hi = logf.astype(BF16)
    mid = (logf - hi.astype(F32)).astype(BF16)
    lo = (logf - hi.astype(F32) - mid.astype(F32)).astype(BF16)
    parts = jnp.dot(tri_ref[...], jnp.concatenate([hi, mid, lo], axis=1), preferred_element_type=F32)
    c = parts[:, :LANES] + parts[:, LANES:2 * LANES] + parts[:, 2 * LANES:] + carry_ref[...]
    carry_ref[...] = c[c.shape[0] - 1:, :]

    qa = head_norm(seg(0), gqa_ref, scale * LOG2E)
    ka = head_norm(seg(1), gka_ref, 1.0)
    c2 = c * LOG2E
    ones = (jnp.where((lane >= HEAD_DIM + 3) & (lane < HEAD_DIM + 6), 1.0, 0.0)
            - jnp.where(lane == HEAD_DIM + 6, shift_ref[...], 0.0))
    for h in range(N_HEADS):
        cb = jnp.broadcast_to(c2[:, h:h + 1], (x.shape[0], LANES))
        hi = cb.astype(BF16).astype(F32)
        mid = (cb - hi).astype(BF16).astype(F32)
        lo = cb - hi - mid
        pieces = jnp.where(lane == HEAD_DIM, hi, jnp.where(lane == HEAD_DIM + 1, mid,
                           jnp.where(lane == HEAD_DIM + 2, lo, 0.0)))
        q_extra = pieces + ones
        k_extra = jnp.where(((lane >= HEAD_DIM) & (lane < HEAD_DIM + 3)) | (lane == HEAD_DIM + 6), 1.0, 0.0) \
            - pltpu.roll(pieces, 3, 1)
        j, odd = divmod(h, HEADS_PER_TILE)
        qp = qa[:, j * LANES:(j + 1) * LANES]
        kp = ka[:, j * LANES:(j + 1) * LANES]
        if odd:
            qp = pltpu.roll(qp, HEAD_DIM, 1)
            kp = pltpu.roll(kp, HEAD_DIM, 1)
        qa_ref[0, h] = jnp.where(lane < HEAD_DIM, qp, q_extra).astype(qa_ref.dtype)
        ka_ref[0, h] = jnp.where(lane < HEAD_DIM, kp, k_extra).astype(ka_ref.dtype)


def _inproj(x, gmix, w_main, w_vt, w_f, b_f, shift, gqa, gka, gqb, gkb, bd, cos_t, sin_t, tri):
    B, S, D = x.shape
    tm = TM_IN
    const = lambda shape: pl.BlockSpec(shape, lambda b, i: (0,) * len(shape))
    tok = lambda w, dt: jax.ShapeDtypeStruct((B, S, w), dt)
    tok_spec = lambda w: pl.BlockSpec((1, tm, w), lambda b, i: (b, i, 0))
    head_spec = pl.BlockSpec((1, N_HEADS, tm, LANES), lambda b, i: (b, 0, i, 0))
    head_shape = jax.ShapeDtypeStruct((B, N_HEADS, S, LANES), BF16)
    return pl.pallas_call(
        _inproj_kernel,
        grid=(B, S // tm),
        in_specs=[tok_spec(D), const((1, D)), const(w_main.shape), const(w_vt.shape), const(w_f.shape),
                  const((1, LANES)), const((1, LANES)),
                  const((1, D_GRP)), const((1, D_GRP)), const((1, D_GRP)), const((1, D_GRP)),
                  const((D_GRP, D_GRP)),
                  pl.BlockSpec((tm, LANES), lambda b, i: (i, 0)),
                  pl.BlockSpec((tm, LANES), lambda b, i: (i, 0)),
                  const((tm, tm))],
        out_specs=[head_spec, head_spec, pl.BlockSpec((1, D_GRP, tm), lambda b, i: (b, 0, i))]
        + [tok_spec(D_GRP)] * 3,
        out_shape=[head_shape, head_shape, jax.ShapeDtypeStruct((B, D_GRP, S), BF16),
                   tok(D_GRP, F32), tok(D_GRP, F32), tok(D_GRP, F32)],
        scratch_shapes=[pltpu.VMEM((1, LANES), F32)],
        compiler_params=_cparams(("arbitrary", "arbitrary")),
        name="inproj",
    )(x, gmix, w_main, w_vt, w_f, b_f, shift, gqa, gka, gqb, gkb, bd, cos_t, sin_t, tri)


def _fox_kernel(q_ref, k_ref, v_ref, o_ref, *, online):
    qi = pl.program_id(2)
    tq = q_ref.shape[2]

    def step(start, width, carry, diag):
        carry = list(carry)
        sub = min(TK_SUB, width)
        chunks = [(c, j) for c in range(width // sub) for j in range(HEADS_PER_TILE)]
        def score(c, j):
            k = k_ref[0, j, pl.ds(start + c * sub, sub), :]
            return lax.dot_general(k, q_ref[0, j], (((1,), (1,)), ((), ())), preferred_element_type=F32)

        scores = {cj: score(*cj) for cj in chunks[:FOX_AHEAD]}
        for n, (c, j) in enumerate(chunks):
                if n + FOX_AHEAD < len(chunks):
                    nxt = chunks[n + FOX_AHEAD]
                    scores[nxt] = score(*nxt)
                m, l, acc = carry[j]
                vt = v_ref[0, :, pl.ds(start + c * sub, sub)]
                s = scores.pop((c, j))
                if diag:
                    key = lax.broadcasted_iota(I32, (sub, tq), 0) + c * sub
                    qry = lax.broadcasted_iota(I32, (sub, tq), 1)
                    s = jnp.where(key <= qry, s, NEG)
                if online:
                    m_new = jnp.maximum(m, jnp.max(s, axis=0, keepdims=True))
                    alpha = jnp.exp2(m - m_new)
                    p = jnp.exp2(s - m_new)
                    l = alpha * l + jnp.sum(p, axis=0, keepdims=True)
                    acc = alpha * acc + jnp.dot(vt, p.astype(BF16), preferred_element_type=F32)
                else:
                    m_new = m
                    p = jnp.exp2(s)
                    l = l + jnp.sum(p, axis=0, keepdims=True)
                    acc = acc + jnp.dot(vt, p.astype(BF16), preferred_element_type=F32)
                carry[j] = (m_new, l, acc)
        return tuple(carry)

    init = tuple((jnp.full((1, tq), NEG, F32), jnp.zeros((1, tq), F32), jnp.zeros((LANES, tq), F32))
                 for _ in range(HEADS_PER_TILE))
    per_wide = TK_WIDE // tq
    n_wide = qi // per_wide
    carry = lax.fori_loop(
        0, n_wide, lambda i, c: step(pl.multiple_of(i * TK_WIDE, TK_WIDE), TK_WIDE, c, False), init)
    for extra in range(per_wide - 1):
        carry = lax.cond(qi - n_wide * per_wide > extra,
                         lambda c, e=extra: step(pl.multiple_of((n_wide * per_wide + e) * tq, tq), tq, c, False),
                         lambda c: c, carry)
    carry = step(pl.multiple_of(qi * tq, tq), tq, carry, True)
    outs = [acc / l for (_, l, acc) in carry]
    feat = lax.broadcasted_iota(I32, (LANES, tq), 0)
    o_ref[0] = jnp.where(feat < HEAD_DIM, outs[0], outs[1]).T.astype(o_ref.dtype)


def _fox(qa, ka, va_t, online):
    B, _, S, _ = qa.shape
    return pl.pallas_call(
        functools.partial(_fox_kernel, online=online),
        grid=(B, N_PAIRS, S // TQ),
        in_specs=[pl.BlockSpec((1, HEADS_PER_TILE, TQ, LANES), lambda b, hp, i: (b, hp, i, 0)),
                  pl.BlockSpec((1, HEADS_PER_TILE, S, LANES), lambda b, hp, i: (b, hp, 0, 0)),
                  pl.BlockSpec((1, LANES, S), lambda b, hp, i: (b, hp, 0))],
        out_specs=pl.BlockSpec((1, TQ, LANES), lambda b, hp, i: (b, i, hp)),
        out_shape=jax.ShapeDtypeStruct((B, S, D_GRP), F32),
        compiler_params=_cparams(("arbitrary", "arbitrary", "arbitrary")),
        name="fox_online" if online else "fox",
    )(qa, ka, va_t)


def _dilated_kernel(shift_ref, q_ref, kp_ref, kc_ref, vp_ref, vc_ref, o_ref, qq, kk, vv, qq4, kk4, vv4, osc, lsc,
                    *, bounded):
    u = pl.program_id(1)
    span = q_ref.shape[1]
    qq[...] = q_ref[0]
    kk[0:span, :] = kp_ref[0]
    kk[span:2 * span, :] = kc_ref[0]
    vv[0:span, :] = vp_ref[0]
    vv[span:2 * span, :] = vc_ref[0]
    for src, dst in ((qq, qq4), (kk, kk4), (vv, vv4)):
        part = src.shape[0] // DIL_PRE
        for a in range(DIL_PRE):
            dst[a * part:(a + 1) * part, :] = src[pl.ds(a, part, stride=DIL_PRE), :]

    def rows(buf, buf4, start, n, d):
        if d % DIL_PRE:
            return buf[pl.ds(start, n, stride=d), :]
        part = buf4.shape[0] // DIL_PRE
        a = lax.rem(start, DIL_PRE)
        return buf4[pl.ds(a * part + lax.div(start, DIL_PRE), n, stride=d // DIL_PRE), :]

    lane = lax.broadcasted_iota(I32, (BLOCK, LANES), 1)
    ql = lax.broadcasted_iota(I32, (BLOCK, 2 * BLOCK), 0)
    kl = lax.broadcasted_iota(I32, (BLOCK, 2 * BLOCK), 1)
    dist = ql + BLOCK - kl
    band = (dist >= 0) & (dist <= BLOCK)
    live = -shift_ref[0, 0] if bounded else 0.0
    bias = jnp.where(band, live, NEG)
    bias_first = jnp.where(band & (kl >= BLOCK), live, NEG)

    def scores(q_start, k_start, d, first):
        qs = rows(qq, qq4, q_start, BLOCK, d).astype(BF16)
        ks = rows(kk, kk4, k_start, 2 * BLOCK, d).astype(BF16)
        mask = jnp.where(first, bias_first, bias)
        out = []
        for j in range(HEADS_PER_TILE):
            qj = jnp.where(lane // HEAD_DIM == j, qs, jnp.zeros_like(qs))
            out.append(lax.dot_general(qj, ks, (((1,), (1,)), ((), ())), preferred_element_type=F32) + mask)
        return out

    def finish(s_heads, k_start, d):
        vs = rows(vv, vv4, k_start, 2 * BLOCK, d).astype(BF16)
        o_heads, lse_heads = [], []
        for s in s_heads:
            if bounded:
                p = jnp.exp(s)
                o_heads.append(jnp.dot(p.astype(BF16), vs, preferred_element_type=F32))
                lse_heads.append(jnp.sum(p, axis=-1, keepdims=True))
            else:
                m = jnp.max(s, axis=-1, keepdims=True)
                p = jnp.exp(s - m)
                l = jnp.sum(p, axis=-1, keepdims=True)
                o_heads.append(jnp.dot((p / l).astype(BF16), vs, preferred_element_type=F32))
                lse_heads.append(m + jnp.log(l))
        o = jnp.where(lane < HEAD_DIM, o_heads[0], o_heads[1])
        lse = jnp.where(lane < HEAD_DIM, lse_heads[0], lse_heads[1])
        return o, lse

    for pidx, (window, d) in enumerate(DIL_PATTERNS):
        assert window // d == BLOCK
        unit = d * BLOCK
        n_problems = (span // unit) * d
        assert n_problems % DIL_GROUP == 0

        def body(g, _, pidx=pidx, d=d, unit=unit):
            starts, s_all = [], []
            for t in range(DIL_GROUP):
                idx = g * DIL_GROUP + t
                w = idx // d
                q_start = w * unit + (idx - w * d)
                k_start = span - unit + q_start
                starts.append((q_start, k_start))
                s_all.append(scores(q_start, k_start, d, jnp.logical_and(u == 0, w == 0)))
            for (q_start, k_start), s_heads in zip(starts, s_all):
                o, lse = finish(s_heads, k_start, d)
                osc[pidx, pl.ds(q_start, BLOCK, stride=d), :] = o
                lsc[pidx, pl.ds(q_start, BLOCK, stride=d), :] = lse
            return 0

        lax.fori_loop(0, n_problems // DIL_GROUP, body, 0)

    if bounded:
        num = osc[0] + osc[1] + osc[2]
        den = lsc[0] + lsc[1] + lsc[2]
    else:
        mx = jnp.maximum(jnp.maximum(lsc[0], lsc[1]), lsc[2])
        num = jnp.zeros((span, LANES), F32)
        den = jnp.zeros((span, LANES), F32)
        for pidx in range(len(DIL_PATTERNS)):
            e = jnp.exp(lsc[pidx] - mx)
            num = num + e * osc[pidx]
            den = den + e
    o_ref[0] = (num / den).astype(o_ref.dtype)


def _dilated(shift, qb, kb, vb, bounded):
    B, S, _ = qb.shape
    span = DIL_SPAN
    cur = pl.BlockSpec((1, span, LANES), lambda b, u, hp: (b, u, hp))
    prev = pl.BlockSpec((1, span, LANES), lambda b, u, hp: (b, jnp.maximum(u - 1, 0), hp))
    return pl.pallas_call(
        functools.partial(_dilated_kernel, bounded=bounded),
        grid=(B, S // span, N_PAIRS),
        in_specs=[pl.BlockSpec(memory_space=pltpu.SMEM), cur, prev, cur, prev, cur],
        out_specs=cur,
        out_shape=jax.ShapeDtypeStruct((B, S, D_GRP), F32),
        scratch_shapes=[pltpu.VMEM((span, LANES), F32),
                        pltpu.VMEM((2 * span, LANES), F32), pltpu.VMEM((2 * span, LANES), F32),
                        pltpu.VMEM((span, LANES), F32),
                        pltpu.VMEM((2 * span, LANES), F32), pltpu.VMEM((2 * span, LANES), F32),
                        pltpu.VMEM((len(DIL_PATTERNS), span, LANES), F32),
                        pltpu.VMEM((len(DIL_PATTERNS), span, LANES), F32)],
        compiler_params=_cparams(("arbitrary", "arbitrary", "arbitrary")),
        name="dilated" if bounded else "dilated_exact",
    )(shift, qb, kb, kb, vb, vb)


def _store_row_tiles(ref, x):
    n = x.shape[0]
    for c in range(ROW_CHUNKS):
        ref[pl.ds(c, n, stride=ROW_CHUNKS), :] = x[:, c * LANES:(c + 1) * LANES]


def _load_row_tiles(ref, n):
    return jnp.concatenate([ref[pl.ds(c, n, stride=ROW_CHUNKS), :] for c in range(ROW_CHUNKS)], axis=1)


def _row_tile_copy(src_ref, src_row, dst_ref, dst_row, sem):
    src = src_ref.at[pl.ds(pl.multiple_of(src_row * ROW_CHUNKS, ROW_CHUNKS), ROW_CHUNKS), :]
    dst = dst_ref.at[pl.ds(pl.multiple_of(dst_row * ROW_CHUNKS, ROW_CHUNKS), ROW_CHUNKS), :]
    return pltpu.make_async_copy(src, dst, sem)


def _outproj_kernel(oa_ref, ob_ref, x_ref, gfox_ref, gdil_ref, wo_ref, gffn_ref, wr_ref, br_ref, upper_ref,
                    h_ref, hn_ref, eid_ref, gate_ref, rank_ref, cnt_ref, run_ref):
    @pl.when(pl.program_id(0) == 0)
    def _():
        run_ref[...] = jnp.zeros_like(run_ref)

    def norm(y, g):
        ms = jnp.mean(y * y, axis=-1, keepdims=True)
        return y * lax.rsqrt(ms + EPS) * g

    a = norm(oa_ref[...], gfox_ref[...]).astype(BF16)
    b = norm(ob_ref[...], gdil_ref[...]).astype(BF16)
    mix = (jnp.dot(a, wo_ref[0:D_GRP, :], preferred_element_type=F32)
           + jnp.dot(b, wo_ref[D_GRP:2 * D_GRP, :], preferred_element_type=F32))
    h = x_ref[...] + mix
    h_ref[...] = h
    hn = norm(h, gffn_ref[...])
    _store_row_tiles(hn_ref, hn)

    z = lax.dot_general(wr_ref[...], hn, (((1,), (1,)), ((), ())), preferred_element_type=F32,
                        precision=lax.Precision.HIGHEST) + br_ref[...]
    tm = z.shape[1]
    best = z[0:1, :]
    g_sel = jnp.zeros((1, tm), I32)
    for g in range(1, N_GROUPS):
        better = z[g:g + 1, :] > best
        g_sel = jnp.where(better, g, g_sel)
        best = jnp.maximum(best, z[g:g + 1, :])
    den = jnp.zeros((1, tm), F32)
    for g in range(N_GROUPS):
        den = den + jnp.exp(z[g:g + 1, :] - best)
    pg_top = 1.0 / den

    ze = jnp.zeros((EXPERTS_PER_GROUP, tm), F32)
    for g in range(N_GROUPS):
        ze = jnp.where(g_sel == g, z[8 + g * EXPERTS_PER_GROUP:8 + (g + 1) * EXPERTS_PER_GROUP, :], ze)
    e_iota = lax.broadcasted_iota(I32, ze.shape, 0)
    v1 = jnp.max(ze, axis=0, keepdims=True)
    i1 = jnp.min(jnp.where(ze == v1, e_iota, EXPERTS_PER_GROUP), axis=0, keepdims=True)
    ze2 = jnp.where(e_iota == i1, -jnp.inf, ze)
    v2 = jnp.max(ze2, axis=0, keepdims=True)
    i2 = jnp.min(jnp.where(ze2 == v2, e_iota, EXPERTS_PER_GROUP), axis=0, keepdims=True)
    e2 = jnp.exp(v2 - v1)
    inv = 1.0 / (1.0 + e2)
    gate1 = inv * pg_top
    gate2 = e2 * inv * pg_top
    eid1 = g_sel * EXPERTS_PER_GROUP + i1
    eid2 = g_sel * EXPERTS_PER_GROUP + i2

    x_iota = lax.broadcasted_iota(I32, (N_EXPERTS, tm), 0)
    hot1 = x_iota == eid1
    hot2 = x_iota == eid2
    multi = jnp.logical_or(hot1, hot2)
    before = jnp.dot(multi.astype(BF16), upper_ref[...], preferred_element_type=F32)
    slot = before + run_ref[:, 0:1]
    rank1 = jnp.sum(jnp.where(hot1, slot, 0.0), axis=0, keepdims=True)
    rank2 = jnp.sum(jnp.where(hot2, slot, 0.0), axis=0, keepdims=True)
    run_ref[...] = run_ref[...] + jnp.sum(multi.astype(F32), axis=1, keepdims=True)

    eid_ref[...] = jnp.concatenate([eid1, eid2], axis=0)
    gate_ref[...] = jnp.concatenate([gate1, gate2], axis=0)
    rank_ref[...] = jnp.concatenate([rank1, rank2], axis=0).astype(I32)
    cnt_ref[...] = run_ref[...].astype(I32)


def _outproj(oa, ob, x2, gfox, gdil, w_out, gffn, w_r, b_r, upper):
    T, D = x2.shape
    tm = TM_OUT
    const = lambda shape: pl.BlockSpec(shape, lambda i: (0,) * len(shape))
    tok = lambda w: pl.BlockSpec((tm, w), lambda i: (i, 0))
    lanes2 = pl.BlockSpec((2, tm), lambda i: (0, i))
    return pl.pallas_call(
        _outproj_kernel,
        grid=(T // tm,),
        in_specs=[tok(D_GRP), tok(D_GRP), tok(D), const((1, D_GRP)), const((1, D_GRP)), const((D, D)),
                  const((1, D)), const((ROUTER_ROWS, D)), const((ROUTER_ROWS, 1)), const((tm, tm))],
        out_specs=[tok(D), pl.BlockSpec((tm * ROW_CHUNKS, LANES), lambda i: (i, 0)),
                   lanes2, lanes2, lanes2, const((N_EXPERTS, LANES))],
        out_shape=[jax.ShapeDtypeStruct((T, D), F32), jax.ShapeDtypeStruct((T * ROW_CHUNKS, LANES), F32),
                   jax.ShapeDtypeStruct((2, T), I32), jax.ShapeDtypeStruct((2, T), F32),
                   jax.ShapeDtypeStruct((2, T), I32), jax.ShapeDtypeStruct((N_EXPERTS, LANES), I32)],
        scratch_shapes=[pltpu.VMEM((N_EXPERTS, LANES), F32)],
        compiler_params=_cparams(("arbitrary",)),
        name="outproj",
    )(oa, ob, x2, gfox, gdil, w_out, gffn, w_r, b_r, upper)


def _scatter_kernel(starts_ref, cnt_ref, pos_ref, hn_ref, xs_ref, ring, sems, zero_sem, *, n_steps):
    i = pl.program_id(0)
    tm = hn_ref.shape[0] // ROW_CHUNKS
    slot = lax.rem(i, 2)

    def wait_slot(s):
        for _ in range(2):
            pltpu.make_async_copy(ring.at[s], xs_ref.at[pl.ds(0, tm * ROW_CHUNKS), :], sems.at[s]).wait()

    @pl.when(i >= 2)
    def _():
        wait_slot(slot)

    ring[slot] = hn_ref[...]

    def start(r, _):
        for k in range(2):
            _row_tile_copy(ring.at[slot], r, xs_ref, pos_ref[k, r], sems.at[slot]).start()
        return 0

    lax.fori_loop(0, tm, start, 0, unroll=DMA_UNROLL)

    @pl.when(i == n_steps - 1)
    def _():
        wait_slot(slot)
        if n_steps > 1:
            wait_slot(1 - slot)
        ring[0] = jnp.zeros((tm * ROW_CHUNKS, LANES), F32)

        def pad_expert(e, _, wait):
            n_pad = lax.rem(TILE_E - lax.rem(cnt_ref[e], TILE_E), TILE_E)
            first = starts_ref[e] + cnt_ref[e]
            size = TILE_E // 2
            while size >= 1:
                row0 = first + (n_pad & ~(2 * size - 1))

                @pl.when((n_pad & size) != 0)
                def _(size=size, row0=row0):
                    copy = pltpu.make_async_copy(
                        ring.at[0, pl.ds(0, size * ROW_CHUNKS), :],
                        xs_ref.at[pl.ds(pl.multiple_of(row0 * ROW_CHUNKS, ROW_CHUNKS), size * ROW_CHUNKS), :],
                        zero_sem)
                    copy.wait() if wait else copy.start()

                size //= 2
            return 0

        lax.fori_loop(0, N_EXPERTS, functools.partial(pad_expert, wait=False), 0)
        last = N_EXPERTS - 1
        used_rows = starts_ref[last] + cnt_ref[last] + lax.rem(TILE_E - lax.rem(cnt_ref[last], TILE_E), TILE_E)
        n_tail = xs_ref.shape[0] // (tm * ROW_CHUNKS) - used_rows // tm

        def tail_copy(t):
            row0 = pl.multiple_of((used_rows + t * tm) * ROW_CHUNKS, tm * ROW_CHUNKS)
            return pltpu.make_async_copy(ring.at[0], xs_ref.at[pl.ds(row0, tm * ROW_CHUNKS), :], zero_sem)

        lax.fori_loop(0, n_tail, lambda t, c: (tail_copy(t).start(), c)[1], 0)
        lax.fori_loop(0, N_EXPERTS, functools.partial(pad_expert, wait=True), 0)
        lax.fori_loop(0, n_tail, lambda t, c: (tail_copy(t).wait(), c)[1], 0)


def _scatter(starts, cnt, pos, hn, n_rows):
    T = hn.shape[0] // ROW_CHUNKS
    tm = TM_ROWS
    assert TILE_E % tm == 0 and TILE_E // 2 <= tm
    n_steps = T // tm
    grid_spec = pltpu.PrefetchScalarGridSpec(
        num_scalar_prefetch=2,
        grid=(n_steps,),
        in_specs=[pl.BlockSpec((2, tm), lambda i, starts, cnt: (0, i), memory_space=pltpu.SMEM),
                  pl.BlockSpec((tm * ROW_CHUNKS, LANES), lambda i, starts, cnt: (i, 0))],
        out_specs=pl.BlockSpec(memory_space=pl.ANY),
        scratch_shapes=[pltpu.VMEM((2, tm * ROW_CHUNKS, LANES), F32),
                        pltpu.SemaphoreType.DMA((2,)), pltpu.SemaphoreType.DMA(())],
    )
    return pl.pallas_call(
        functools.partial(_scatter_kernel, n_steps=n_steps),
        grid_spec=grid_spec,
        out_shape=jax.ShapeDtypeStruct((n_rows * ROW_CHUNKS, LANES), F32),
        compiler_params=_cparams(("arbitrary",)),
        name="scatter_rows",
    )(starts, cnt, pos, hn)


def _experts_kernel(te_ref, tv_ref, xs_ref, wg_ref, wu_ref, wd_ref, y_ref, wg_bf, wu_bf, wd_bf):
    i = pl.program_id(0)
    new_expert = jnp.logical_or(i == 0, te_ref[i] != te_ref[jnp.maximum(i - 1, 0)])

    @pl.when(jnp.logical_and(tv_ref[i] != 0, new_expert))
    def _():
        wg_bf[...] = wg_ref[0].astype(BF16)
        wu_bf[...] = wu_ref[0].astype(BF16)
        wd_bf[...] = wd_ref[0].astype(BF16)

    @pl.when(tv_ref[i] != 0)
    def _():
        x = _load_row_tiles(xs_ref, TILE_E).astype(BF16)
        g = jnp.dot(x, wg_bf[...], preferred_element_type=F32)
        up = jnp.dot(x, wu_bf[...], preferred_element_type=F32)
        hmid = (g * jax.nn.sigmoid(g) * up).astype(BF16)
        _store_row_tiles(y_ref, jnp.dot(hmid, wd_bf[...], preferred_element_type=F32))

    @pl.when(tv_ref[i] == 0)
    def _():
        y_ref[...] = jnp.zeros_like(y_ref)


def _experts(tile_expert, tile_valid, xs, w_gate, w_up, w_down):
    n_tiles = tile_expert.shape[0]
    D = D_MODEL
    rows_spec = pl.BlockSpec((TILE_E * ROW_CHUNKS, LANES), lambda i, te, tv: (i, 0))
    grid_spec = pltpu.PrefetchScalarGridSpec(
        num_scalar_prefetch=2,
        grid=(n_tiles,),
        in_specs=[rows_spec,
                  pl.BlockSpec((1, D, D_EXPERT), lambda i, te, tv: (te[i], 0, 0)),
                  pl.BlockSpec((1, D, D_EXPERT), lambda i, te, tv: (te[i], 0, 0)),
                  pl.BlockSpec((1, D_EXPERT, D), lambda i, te, tv: (te[i], 0, 0))],
        out_specs=rows_spec,
        scratch_shapes=[pltpu.VMEM((D, D_EXPERT), BF16), pltpu.VMEM((D, D_EXPERT), BF16),
                        pltpu.VMEM((D_EXPERT, D), BF16)],
    )
    return pl.pallas_call(
        _experts_kernel,
        grid_spec=grid_spec,
        out_shape=jax.ShapeDtypeStruct(xs.shape, F32),
        compiler_params=_cparams(("arbitrary",)),
        name="experts",
    )(tile_expert, tile_valid, xs, w_gate, w_up, w_down)


def _combine_kernel(pos_ref, pos_next_ref, h_ref, gate_ref, y_ref, o_ref, ybuf, sems, *, n_steps):
    i = pl.program_id(0)
    tm = h_ref.shape[0]
    slot = lax.rem(i, 2)

    def gather(p_ref, s):
        def start(r, _):
            for k in range(2):
                _row_tile_copy(y_ref, p_ref[k, r], ybuf.at[s, k], r, sems.at[s]).start()
            return 0

        lax.fori_loop(0, tm, start, 0, unroll=DMA_UNROLL)

    @pl.when(i == 0)
    def _():
        gather(pos_ref, slot)

    @pl.when(i + 1 < n_steps)
    def _():
        gather(pos_next_ref, 1 - slot)

    for k in range(2):
        pltpu.make_async_copy(y_ref.at[pl.ds(0, tm * ROW_CHUNKS), :], ybuf.at[slot, k], sems.at[slot]).wait()
    g = gate_ref[...]
    o_ref[...] = (h_ref[...] + g[:, 0:1] * _load_row_tiles(ybuf.at[slot, 0], tm)
                  + g[:, 1:2] * _load_row_tiles(ybuf.at[slot, 1], tm))


def _combine(pos, h, gate_t, y):
    T, D = h.shape
    tm = TM_ROWS
    n_steps = T // tm
    return pl.pallas_call(
        functools.partial(_combine_kernel, n_steps=n_steps),
        grid=(n_steps,),
        in_specs=[pl.BlockSpec((2, tm), lambda i: (0, i), memory_space=pltpu.SMEM),
                  pl.BlockSpec((2, tm), lambda i: (0, jnp.minimum(i + 1, n_steps - 1)), memory_space=pltpu.SMEM),
                  pl.BlockSpec((tm, D), lambda i: (i, 0)),
                  pl.BlockSpec((tm, 2), lambda i: (i, 0)),
                  pl.BlockSpec(memory_space=pl.ANY)],
        out_specs=pl.BlockSpec((tm, D), lambda i: (i, 0)),
        out_shape=jax.ShapeDtypeStruct((T, D), F32),
        scratch_shapes=[pltpu.VMEM((2, 2, tm * ROW_CHUNKS, LANES), F32), pltpu.SemaphoreType.DMA((2,))],
        compiler_params=_cparams(("arbitrary",)),
        name="combine",
    )(pos, pos, h, gate_t, y)


def _rope_tables(S):
    inv_freq = 1.0 / (ROPE_THETA ** (jnp.arange(0, HEAD_DIM, 2, dtype=F32) / HEAD_DIM))
    ang = jnp.arange(S, dtype=F32)[:, None] * inv_freq[None, :]
    cos, sin = jnp.cos(ang), jnp.sin(ang)
    cos_t = jnp.tile(cos, (1, 2 * HEADS_PER_TILE))
    sin_t = jnp.tile(jnp.concatenate([-sin, sin], axis=1), (1, HEADS_PER_TILE))
    return cos_t, sin_t


def _layer(x, norm_mix, w_in, b_forget, q_norm_fox, k_norm_fox, q_norm_dil, k_norm_dil,
           out_norm_fox, out_norm_dil, w_out, norm_ffn, w_router_group, b_router_group,
           w_router_expert, b_router_expert, w_gate, w_up, w_down):
    B, S, D = x.shape
    T = B * S
    n_main = 6 * D_GRP

    w_main = w_in[:, :n_main].astype(BF16)
    w_vt = w_in[:, 2 * D_GRP:3 * D_GRP].T.astype(BF16)
    w_f = jnp.pad(w_in[:, n_main:], ((0, 0), (0, LANES - N_HEADS))).astype(BF16)
    b_f = jnp.pad(b_forget, (0, LANES - N_HEADS))[None, :]
    per_head = lambda g: jnp.tile(g, N_HEADS)[None, :]
    bd = jnp.kron(jnp.eye(N_HEADS, dtype=F32), jnp.ones((HEAD_DIM, HEAD_DIM), F32)).astype(BF16)
    cos_t, sin_t = _rope_tables(S)
    tri = jnp.tril(jnp.ones((TM_IN, TM_IN), F32)).astype(BF16)
    upper = jnp.triu(jnp.ones((TM_OUT, TM_OUT), F32), k=1).astype(BF16)
    w_r = jnp.concatenate([
        jnp.pad(w_router_group.T, ((0, 8 - N_GROUPS), (0, 0))),
        w_router_expert.transpose(0, 2, 1).reshape(N_EXPERTS, D)], axis=0)
    b_r = jnp.concatenate([jnp.pad(b_router_group, (0, 8 - N_GROUPS)), b_router_expert.reshape(-1)])[:, None]

    bound = (HEAD_DIM / math.sqrt(HEAD_DIM)) * LOG2E * jnp.max(jnp.abs(q_norm_fox)) * jnp.max(jnp.abs(k_norm_fox))
    shift = FOX_BOUND_SLACK * bound + 1.0
    qa, ka, va, qb, kb, vb = _inproj(
        x, norm_mix[None, :], w_main, w_vt, w_f, b_f, jnp.full((1, LANES), shift, F32),
        per_head(q_norm_fox), per_head(k_norm_fox),
        per_head(q_norm_dil), per_head(k_norm_dil), bd, cos_t, sin_t, tri)
    oa = lax.cond(2.0 * shift <= FOX_SAFE_SPAN,
                  lambda: _fox(qa, ka, va, online=False), lambda: _fox(qa, ka, va, online=True))
    dil_shift = (FOX_BOUND_SLACK * (HEAD_DIM / math.sqrt(HEAD_DIM))
                 * jnp.max(jnp.abs(q_norm_dil)) * jnp.max(jnp.abs(k_norm_dil)) + 1.0)
    dil_shift_arr = jnp.full((1, 1), dil_shift, F32)
    ob = lax.cond(2.0 * dil_shift <= DIL_SAFE_SPAN,
                  lambda: _dilated(dil_shift_arr, qb, kb, vb, bounded=True),
                  lambda: _dilated(dil_shift_arr, qb, kb, vb, bounded=False))

    h, hn, eid, gate, rank, cnt = _outproj(
        oa.reshape(T, D_GRP), ob.reshape(T, D_GRP), x.reshape(T, D), out_norm_fox[None, :],
        out_norm_dil[None, :], w_out.astype(BF16), norm_ffn[None, :], w_r, b_r, upper)

    counts = cnt[:, 0]
    padded = ((counts + TILE_E - 1) // TILE_E) * TILE_E
    ends = jnp.cumsum(padded)
    starts = ends - padded
    is_expert = eid[:, :, None] == jnp.arange(N_EXPERTS, dtype=I32)
    pos = jnp.sum(jnp.where(is_expert, starts, 0), axis=-1) + rank
    n_tiles = (2 * T) // TILE_E + N_EXPERTS
    tile_index = jnp.arange(n_tiles, dtype=I32)
    tile_valid = (tile_index * TILE_E < ends[-1]).astype(I32)
    tile_expert = jnp.minimum(
        jnp.sum((ends[None, :] <= (tile_index * TILE_E)[:, None]).astype(I32), axis=1), N_EXPERTS - 1)

    xs = _scatter(starts, counts, pos, hn, n_tiles * TILE_E)
    y = _experts(tile_expert, tile_valid, xs, w_gate, w_up, w_down)
    out = _combine(pos, h, gate.T, y)
    return out.reshape(B, S, D)


def kernel(x, norm_mix, w_in, b_forget, q_norm_fox, k_norm_fox, q_norm_dil, k_norm_dil, out_norm_fox,
           out_norm_dil, w_out, norm_ffn, w_router_group, b_router_group, w_router_expert,
           b_router_expert, w_gate, w_up, w_down):
    h = x
    for l in range(norm_mix.shape[0]):
        h = _layer(h, norm_mix[l], w_in[l], b_forget[l], q_norm_fox[l], k_norm_fox[l], q_norm_dil[l],
                   k_norm_dil[l], out_norm_fox[l], out_norm_dil[l], w_out[l], norm_ffn[l],
                   w_router_group[l], b_router_group[l], w_router_expert[l], b_router_expert[l],
                   w_gate[l], w_up[l], w_down[l])
    return h
```

```python
import functools
import math

import jax
import jax.numpy as jnp
from jax import lax
from jax.experimental import pallas as pl
from jax.experimental.pallas import tpu as pltpu

F32 = jnp.float32
BF16 = jnp.bfloat16
I32 = jnp.int32

D_MODEL = 1024
HEAD_DIM = 64
N_HEADS = 8
D_GRP = N_HEADS * HEAD_DIM
LANES = 128
HEADS_PER_TILE = LANES // HEAD_DIM
N_PAIRS = D_GRP // LANES
DIL_PATTERNS = ((128, 1), (512, 4), (2048, 16))
BLOCK = 128
ROPE_THETA = 10000.0
N_GROUPS = 4
EXPERTS_PER_GROUP = 8
N_EXPERTS = N_GROUPS * EXPERTS_PER_GROUP
D_EXPERT = 512
EPS = 1e-6
NEG = -1e30
LOG2E = 1.4426950408889634

TM_IN = 512
TQ = 1024
TK_WIDE = 1024
TK_SUB = 256
FOX_AHEAD = 4
FOX_BOUND_SLACK = 1.02
DIL_SAFE_SPAN = 69.0
FOX_SAFE_SPAN = 100.0
DIL_SPAN = 2048
DIL_GROUP = 8
DIL_PRE = 4
TM_OUT = 512
TM_ROWS = 256
TILE_E = 512
ROUTER_ROWS = 8 + N_EXPERTS
ROW_CHUNKS = D_MODEL // LANES
DMA_UNROLL = 8
VMEM_LIMIT = 56 * 1024 * 1024


def _cparams(sem, flags=None):
    return pltpu.CompilerParams(dimension_semantics=sem, vmem_limit_bytes=VMEM_LIMIT, flags=flags)


def _inproj_kernel(x_ref, gmix_ref, w_ref, wvt_ref, wf_ref, bf_ref, shift_ref, gqa_ref, gka_ref, gqb_ref, gkb_ref,
                   bd_ref, cos_ref, sin_ref, tri_ref,
                   qa_ref, ka_ref, va_ref, qb_ref, kb_ref, vb_ref, carry_ref):
    @pl.when(pl.program_id(1) == 0)
    def _():
        carry_ref[...] = jnp.zeros_like(carry_ref)

    x = x_ref[0]
    ms = jnp.mean(x * x, axis=-1, keepdims=True)
    xn = (x * lax.rsqrt(ms + EPS) * gmix_ref[...]).astype(BF16)

    def seg(j):
        return jnp.dot(xn, w_ref[:, j * D_GRP:(j + 1) * D_GRP], preferred_element_type=F32)

    def head_norm(y, g_ref, scale):
        ss = jnp.dot((y * y).astype(BF16), bd_ref[...], preferred_element_type=F32) * (1.0 / HEAD_DIM)
        return y * lax.rsqrt(ss + EPS) * (g_ref[...] * scale)

    cos = cos_ref[...]
    sin = sin_ref[...]
    lane = lax.broadcasted_iota(I32, (x.shape[0], LANES), 1)
    first_half = (lane % HEAD_DIM) < (HEAD_DIM // 2)

    def rope(y):
        outs = []
        for j in range(N_PAIRS):
            ys = y[:, j * LANES:(j + 1) * LANES]
            partner = jnp.where(first_half, pltpu.roll(ys, LANES - HEAD_DIM // 2, 1),
                                pltpu.roll(ys, HEAD_DIM // 2, 1))
            outs.append(ys * cos + partner * sin)
        return jnp.concatenate(outs, axis=1)

    scale = 1.0 / math.sqrt(HEAD_DIM)
    va_ref[0] = lax.dot_general(wvt_ref[...], xn, (((1,), (1,)), ((), ())),
                                preferred_element_type=F32).astype(va_ref.dtype)
    qb_ref[0] = rope(head_norm(seg(3), gqb_ref, scale)).astype(qb_ref.dtype)
    kb_ref[0] = rope(head_norm(seg(4), gkb_ref, 1.0)).astype(kb_ref.dtype)
    vb_ref[0] = seg(5).astype(vb_ref.dtype)

    fa = jnp.dot(xn, wf_ref[...], preferred_element_type=F32) + bf_ref[...]
    logf = jnp.minimum(fa, 0.0) - jnp.log1p(jnp.exp(-jnp.abs(fa)))
    hi = logf.astype(BF16)
    mid = (logf - hi.astype(F32)).astype(BF16)
    lo = (logf - hi.astype(F32) - mid.astype(F32)).astype(BF16)
    parts = jnp.dot(tri_ref[...], jnp.concatenate([hi, mid, lo], axis=1), preferred_element_type=F32)
    c = parts[:, :LANES] + parts[:, LANES:2 * LANES] + parts[:, 2 * LANES:] + carry_ref[...]
    carry_ref[...] = c[c.shape[0] - 1:, :]

    qa = head_norm(seg(0), gqa_ref, scale * LOG2E)
    ka = head_norm(seg(1), gka_ref, 1.0)
    c2 = c * LOG2E
    ones = (jnp.where((lane >= HEAD_DIM + 3) & (lane < HEAD_DIM + 6), 1.0, 0.0)
            - jnp.where(lane == HEAD_DIM + 6, shift_ref[...], 0.0))
    for h in range(N_HEADS):
        cb = jnp.broadcast_to(c2[:, h:h + 1], (x.shape[0], LANES))
        hi = cb.astype(BF16).astype(F32)
        mid = (cb - hi).astype(BF16).astype(F32)
        lo = cb - hi - mid
        pieces = jnp.where(lane == HEAD_DIM, hi, jnp.where(lane == HEAD_DIM + 1, mid,
                           jnp.where(lane == HEAD_DIM + 2, lo, 0.0)))
        q_extra = pieces + ones
        k_extra = jnp.where(((lane >= HEAD_DIM) & (lane < HEAD_DIM + 3)) | (lane == HEAD_DIM + 6), 1.0, 0.0) \
            - pltpu.roll(pieces, 3, 1)
        j, odd = divmod(h, HEADS_PER_TILE)
        qp = qa[:, j * LANES:(j + 1) * LANES]
        kp = ka[:, j * LANES:(j + 1) * LANES]
        if odd:
            qp = pltpu.roll(qp, HEAD_DIM, 1)
            kp = pltpu.roll(kp, HEAD_DIM, 1)
        qa_ref[0, h] = jnp.where(lane < HEAD_DIM, qp, q_extra).astype(qa_ref.dtype)
        ka_ref[0, h] = jnp.where(lane < HEAD_DIM, kp, k_extra).astype(ka_ref.dtype)


def _inproj(x, gmix, w_main, w_vt, w_f, b_f, shift, gqa, gka, gqb, gkb, bd, cos_t, sin_t, tri):
    B, S, D = x.shape
    tm = TM_IN
    const = lambda shape: pl.BlockSpec(shape, lambda b, i: (0,) * len(shape))
    tok = lambda w, dt: jax.ShapeDtypeStruct((B, S, w), dt)
    tok_spec = lambda w: pl.BlockSpec((1, tm, w), lambda b, i: (b, i, 0))
    head_spec = pl.BlockSpec((1, N_HEADS, tm, LANES), lambda b, i: (b, 0, i, 0))
    head_shape = jax.ShapeDtypeStruct((B, N_HEADS, S, LANES), BF16)
    return pl.pallas_call(
        _inproj_kernel,
        grid=(B, S // tm),
        in_specs=[tok_spec(D), const((1, D)), const(w_main.shape), const(w_vt.shape), const(w_f.shape),
                  const((1, LANES)), const((1, LANES)),
                  const((1, D_GRP)), const((1, D_GRP)), const((1, D_GRP)), const((1, D_GRP)),
                  const((D_GRP, D_GRP)),
                  pl.BlockSpec((tm, LANES), lambda b, i: (i, 0)),
                  pl.BlockSpec((tm, LANES), lambda b, i: (i, 0)),
                  const((tm, tm))],
        out_specs=[head_spec, head_spec, pl.BlockSpec((1, D_GRP, tm), lambda b, i: (b, 0, i))]
        + [tok_spec(D_GRP)] * 3,
        out_shape=[head_shape, head_shape, jax.ShapeDtypeStruct((B, D_GRP, S), BF16),
                   tok(D_GRP, F32), tok(D_GRP, F32), tok(D_GRP, F32)],
        scratch_shapes=[pltpu.VMEM((1, LANES), F32)],
        compiler_params=_cparams(("arbitrary", "arbitrary")),
        name="inproj",
    )(x, gmix, w_main, w_vt, w_f, b_f, shift, gqa, gka, gqb, gkb, bd, cos_t, sin_t, tri)


def _fox_kernel(q_ref, k_ref, v_ref, o_ref, *, online):
    qi = pl.program_id(2)
    tq = q_ref.shape[2]

    def step(start, width, carry, diag):
        carry = list(carry)
        sub = min(TK_SUB, width)
        chunks = [(c, j) for c in range(width // sub) for j in range(HEADS_PER_TILE)]
        def first_query(c):
            return c * sub if diag else 0

        def score(c, j):
            k = k_ref[0, j, pl.ds(start + c * sub, sub), :]
            q = q_ref[0, j, first_query(c):, :]
            return lax.dot_general(k, q, (((1,), (1,)), ((), ())), preferred_element_type=F32)

        def tail(full, lo, new):
            return new if lo == 0 else jnp.concatenate([full[:, :lo], new], axis=1)

        scores = {cj: score(*cj) for cj in chunks[:FOX_AHEAD]}
        for n, (c, j) in enumerate(chunks):
                if n + FOX_AHEAD < len(chunks):
                    nxt = chunks[n + FOX_AHEAD]
                    scores[nxt] = score(*nxt)
                m, l, acc = carry[j]
                lo = first_query(c)
                vt = v_ref[0, :, pl.ds(start + c * sub, sub)]
                s = scores.pop((c, j))
                if diag:
                    key = lax.broadcasted_iota(I32, s.shape, 0)
                    qry = lax.broadcasted_iota(I32, s.shape, 1)
                    s = jnp.where(key <= qry, s, NEG)
                if online:
                    m_new = jnp.maximum(m[:, lo:], jnp.max(s, axis=0, keepdims=True))
                    alpha = jnp.exp2(m[:, lo:] - m_new)
                    p = jnp.exp2(s - m_new)
                    l_new = alpha * l[:, lo:] + jnp.sum(p, axis=0, keepdims=True)
                    acc_new = alpha * acc[:, lo:] + jnp.dot(vt, p.astype(BF16), preferred_element_type=F32)
                    m = tail(m, lo, m_new)
                else:
                    p = jnp.exp2(s)
                    l_new = l[:, lo:] + jnp.sum(p, axis=0, keepdims=True)
                    acc_new = acc[:, lo:] + jnp.dot(vt, p.astype(BF16), preferred_element_type=F32)
                carry[j] = (m, tail(l, lo, l_new), tail(acc, lo, acc_new))
        return tuple(carry)

    init = tuple((jnp.full((1, tq), NEG, F32), jnp.zeros((1, tq), F32), jnp.zeros((LANES, tq), F32))
                 for _ in range(HEADS_PER_TILE))
    per_wide = TK_WIDE // tq
    n_wide = qi // per_wide
    carry = lax.fori_loop(
        0, n_wide, lambda i, c: step(pl.multiple_of(i * TK_WIDE, TK_WIDE), TK_WIDE, c, False), init)
    for extra in range(per_wide - 1):
        carry = lax.cond(qi - n_wide * per_wide > extra,
                         lambda c, e=extra: step(pl.multiple_of((n_wide * per_wide + e) * tq, tq), tq, c, False),
                         lambda c: c, carry)
    carry = step(pl.multiple_of(qi * tq, tq), tq, carry, True)
    outs = [acc / l for (_, l, acc) in carry]
    feat = lax.broadcasted_iota(I32, (LANES, tq), 0)
    o_ref[0] = jnp.where(feat < HEAD_DIM, outs[0], outs[1]).T.astype(o_ref.dtype)


def _fox(qa, ka, va_t, online):
    B, _, S, _ = qa.shape
    return pl.pallas_call(
        functools.partial(_fox_kernel, online=online),
        grid=(B, N_PAIRS, S // TQ),
        in_specs=[pl.BlockSpec((1, HEADS_PER_TILE, TQ, LANES), lambda b, hp, i: (b, hp, i, 0)),
                  pl.BlockSpec((1, HEADS_PER_TILE, S, LANES), lambda b, hp, i: (b, hp, 0, 0)),
                  pl.BlockSpec((1, LANES, S), lambda b, hp, i: (b, hp, 0))],
        out_specs=pl.BlockSpec((1, TQ, LANES), lambda b, hp, i: (b, i, hp)),
        out_shape=jax.ShapeDtypeStruct((B, S, D_GRP), F32),
        compiler_params=_cparams(("arbitrary", "arbitrary", "arbitrary")),
        name="fox_online" if online else "fox",
    )(qa, ka, va_t)


def _dilated_kernel(shift_ref, q_ref, kp_ref, kc_ref, vp_ref, vc_ref, o_ref, qq, kk, vv, qq4, kk4, vv4, osc, lsc,
                    *, bounded):
    u = pl.program_id(1)
    span = q_ref.shape[1]
    qq[...] = q_ref[0]
    kk[0:span, :] = kp_ref[0]
    kk[span:2 * span, :] = kc_ref[0]
    vv[0:span, :] = vp_ref[0]
    vv[span:2 * span, :] = vc_ref[0]
    for src, dst in ((qq, qq4), (kk, kk4), (vv, vv4)):
        part = src.shape[0] // DIL_PRE
        for a in range(DIL_PRE):
            dst[a * part:(a + 1) * part, :] = src[pl.ds(a, part, stride=DIL_PRE), :]

    def rows(buf, buf4, start, n, d):
        if d % DIL_PRE:
            return buf[pl.ds(start, n, stride=d), :]
        part = buf4.shape[0] // DIL_PRE
        a = lax.rem(start, DIL_PRE)
        return buf4[pl.ds(a * part + lax.div(start, DIL_PRE), n, stride=d // DIL_PRE), :]

    lane = lax.broadcasted_iota(I32, (BLOCK, LANES), 1)
    ql = lax.broadcasted_iota(I32, (BLOCK, 2 * BLOCK), 0)
    kl = lax.broadcasted_iota(I32, (BLOCK, 2 * BLOCK), 1)
    dist = ql + BLOCK - kl
    band = (dist >= 0) & (dist <= BLOCK)
    live = -shift_ref[0, 0] if bounded else 0.0
    bias = jnp.where(band, live, NEG)
    bias_first = jnp.where(band & (kl >= BLOCK), live, NEG)

    def scores(q_start, k_start, d, first):
        qs = rows(qq, qq4, q_start, BLOCK, d).astype(BF16)
        ks = rows(kk, kk4, k_start, 2 * BLOCK, d).astype(BF16)
        mask = jnp.where(first, bias_first, bias)
        out = []
        for j in range(HEADS_PER_TILE):
            qj = jnp.where(lane // HEAD_DIM == j, qs, jnp.zeros_like(qs))
            out.append(lax.dot_general(qj, ks, (((1,), (1,)), ((), ())), preferred_element_type=F32) + mask)
        return out

    def finish(s_heads, k_start, d):
        vs = rows(vv, vv4, k_start, 2 * BLOCK, d).astype(BF16)
        o_heads, lse_heads = [], []
        for s in s_heads:
            if bounded:
                p = jnp.exp(s)
                o_heads.append(jnp.dot(p.astype(BF16), vs, preferred_element_type=F32))
                lse_heads.append(jnp.sum(p, axis=-1, keepdims=True))
            else:
                m = jnp.max(s, axis=-1, keepdims=True)
                p = jnp.exp(s - m)
                l = jnp.sum(p, axis=-1, keepdims=True)
                o_heads.append(jnp.dot((p / l).astype(BF16), vs, preferred_element_type=F32))
                lse_heads.append(m + jnp.log(l))
        o = jnp.where(lane < HEAD_DIM, o_heads[0], o_heads[1])
        lse = jnp.where(lane < HEAD_DIM, lse_heads[0], lse_heads[1])
        return o, lse

    for pidx, (window, d) in enumerate(DIL_PATTERNS):
        assert window // d == BLOCK
        unit = d * BLOCK
        n_problems = (span // unit) * d
        assert n_problems % DIL_GROUP == 0

        def body(g, _, pidx=pidx, d=d, unit=unit):
            starts, s_all = [], []
            for t in range(DIL_GROUP):
                idx = g * DIL_GROUP + t
                w = idx // d
                q_start = w * unit + (idx - w * d)
                k_start = span - unit + q_start
                starts.append((q_start, k_start))
                s_all.append(scores(q_start, k_start, d, jnp.logical_and(u == 0, w == 0)))
            for (q_start, k_start), s_heads in zip(starts, s_all):
                o, lse = finish(s_heads, k_start, d)
                osc[pidx, pl.ds(q_start, BLOCK, stride=d), :] = o
                lsc[pidx, pl.ds(q_start, BLOCK, stride=d), :] = lse
            return 0

        lax.fori_loop(0, n_problems // DIL_GROUP, body, 0)

    if bounded:
        num = osc[0] + osc[1] + osc[2]
        den = lsc[0] + lsc[1] + lsc[2]
    else:
        mx = jnp.maximum(jnp.maximum(lsc[0], lsc[1]), lsc[2])
        num = jnp.zeros((span, LANES), F32)
        den = jnp.zeros((span, LANES), F32)
        for pidx in range(len(DIL_PATTERNS)):
            e = jnp.exp(lsc[pidx] - mx)
            num = num + e * osc[pidx]
            den = den + e
    o_ref[0] = (num / den).astype(o_ref.dtype)


def _dilated(shift, qb, kb, vb, bounded):
    B, S, _ = qb.shape
    span = DIL_SPAN
    cur = pl.BlockSpec((1, span, LANES), lambda b, u, hp: (b, u, hp))
    prev = pl.BlockSpec((1, span, LANES), lambda b, u, hp: (b, jnp.maximum(u - 1, 0), hp))
    return pl.pallas_call(
        functools.partial(_dilated_kernel, bounded=bounded),
        grid=(B, S // span, N_PAIRS),
        in_specs=[pl.BlockSpec(memory_space=pltpu.SMEM), cur, prev, cur, prev, cur],
        out_specs=cur,
        out_shape=jax.ShapeDtypeStruct((B, S, D_GRP), F32),
        scratch_shapes=[pltpu.VMEM((span, LANES), F32),
                        pltpu.VMEM((2 * span, LANES), F32), pltpu.VMEM((2 * span, LANES), F32),
                        pltpu.VMEM((span, LANES), F32),
                        pltpu.VMEM((2 * span, LANES), F32), pltpu.VMEM((2 * span, LANES), F32),
                        pltpu.VMEM((len(DIL_PATTERNS), span, LANES), F32),
                        pltpu.VMEM((len(DIL_PATTERNS), span, LANES), F32)],
        compiler_params=_cparams(("arbitrary", "arbitrary", "arbitrary")),
        name="dilated" if bounded else "dilated_exact",
    )(shift, qb, kb, kb, vb, vb)


def _store_row_tiles(ref, x):
    n = x.shape[0]
    for c in range(ROW_CHUNKS):
        ref[pl.ds(c, n, stride=ROW_CHUNKS), :] = x[:, c * LANES:(c + 1) * LANES]


def _load_row_tiles(ref, n):
    return jnp.concatenate([ref[pl.ds(c, n, stride=ROW_CHUNKS), :] for c in range(ROW_CHUNKS)], axis=1)


def _row_tile_copy(src_ref, src_row, dst_ref, dst_row, sem):
    src = src_ref.at[pl.ds(pl.multiple_of(src_row * ROW_CHUNKS, ROW_CHUNKS), ROW_CHUNKS), :]
    dst = dst_ref.at[pl.ds(pl.multiple_of(dst_row * ROW_CHUNKS, ROW_CHUNKS), ROW_CHUNKS), :]
    return pltpu.make_async_copy(src, dst, sem)


def _outproj_kernel(oa_ref, ob_ref, x_ref, gfox_ref, gdil_ref, wo_ref, gffn_ref, wr_ref, br_ref, upper_ref,
                    h_ref, hn_ref, eid_ref, gate_ref, rank_ref, cnt_ref, run_ref):
    @pl.when(pl.program_id(0) == 0)
    def _():
        run_ref[...] = jnp.zeros_like(run_ref)

    def norm(y, g):
        ms = jnp.mean(y * y, axis=-1, keepdims=True)
        return y * lax.rsqrt(ms + EPS) * g

    a = norm(oa_ref[...], gfox_ref[...]).astype(BF16)
    b = norm(ob_ref[...], gdil_ref[...]).astype(BF16)
    mix = (jnp.dot(a, wo_ref[0:D_GRP, :], preferred_element_type=F32)
           + jnp.dot(b, wo_ref[D_GRP:2 * D_GRP, :], preferred_element_type=F32))
    h = x_ref[...] + mix
    h_ref[...] = h
    hn = norm(h, gffn_ref[...])
    _store_row_tiles(hn_ref, hn)

    z = lax.dot_general(wr_ref[...], hn, (((1,), (1,)), ((), ())), preferred_element_type=F32,
                        precision=lax.Precision.HIGHEST) + br_ref[...]
    tm = z.shape[1]
    best = z[0:1, :]
    g_sel = jnp.zeros((1, tm), I32)
    for g in range(1, N_GROUPS):
        better = z[g:g + 1, :] > best
        g_sel = jnp.where(better, g, g_sel)
        best = jnp.maximum(best, z[g:g + 1, :])
    den = jnp.zeros((1, tm), F32)
    for g in range(N_GROUPS):
        den = den + jnp.exp(z[g:g + 1, :] - best)
    pg_top = 1.0 / den

    ze = jnp.zeros((EXPERTS_PER_GROUP, tm), F32)
    for g in range(N_GROUPS):
        ze = jnp.where(g_sel == g, z[8 + g * EXPERTS_PER_GROUP:8 + (g + 1) * EXPERTS_PER_GROUP, :], ze)
    e_iota = lax.broadcasted_iota(I32, ze.shape, 0)
    v1 = jnp.max(ze, axis=0, keepdims=True)
    i1 = jnp.min(jnp.where(ze == v1, e_iota, EXPERTS_PER_GROUP), axis=0, keepdims=True)
    ze2 = jnp.where(e_iota == i1, -jnp.inf, ze)
    v2 = jnp.max(ze2, axis=0, keepdims=True)
    i2 = jnp.min(jnp.where(ze2 == v2, e_iota, EXPERTS_PER_GROUP), axis=0, keepdims=True)
    e2 = jnp.exp(v2 - v1)
    inv = 1.0 / (1.0 + e2)
    gate1 = inv * pg_top
    gate2 = e2 * inv * pg_top
    eid1 = g_sel * EXPERTS_PER_GROUP + i1
    eid2 = g_sel * EXPERTS_PER_GROUP + i2

    x_iota = lax.broadcasted_iota(I32, (N_EXPERTS, tm), 0)
    hot1 = x_iota == eid1
    hot2 = x_iota == eid2
    multi = jnp.logical_or(hot1, hot2)
    before = jnp.dot(multi.astype(BF16), upper_ref[...], preferred_element_type=F32)
    slot = before + run_ref[:, 0:1]
    rank1 = jnp.sum(jnp.where(hot1, slot, 0.0), axis=0, keepdims=True)
    rank2 = jnp.sum(jnp.where(hot2, slot, 0.0), axis=0, keepdims=True)
    run_ref[...] = run_ref[...] + jnp.sum(multi.astype(F32), axis=1, keepdims=True)

    eid_ref[...] = jnp.concatenate([eid1, eid2], axis=0)
    gate_ref[...] = jnp.concatenate([gate1, gate2], axis=0)
    rank_ref[...] = jnp.concatenate([rank1, rank2], axis=0).astype(I32)
    cnt_ref[...] = run_ref[...].astype(I32)


def _outproj(oa, ob, x2, gfox, gdil, w_out, gffn, w_r, b_r, upper):
    T, D = x2.shape
    tm = TM_OUT
    const = lambda shape: pl.BlockSpec(shape, lambda i: (0,) * len(shape))
    tok = lambda w: pl.BlockSpec((tm, w), lambda i: (i, 0))
    lanes2 = pl.BlockSpec((2, tm), lambda i: (0, i))
    return pl.pallas_call(
        _outproj_kernel,
        grid=(T // tm,),
        in_specs=[tok(D_GRP), tok(D_GRP), tok(D), const((1, D_GRP)), const((1, D_GRP)), const((D, D)),
                  const((1, D)), const((ROUTER_ROWS, D)), const((ROUTER_ROWS, 1)), const((tm, tm))],
        out_specs=[tok(D), pl.BlockSpec((tm * ROW_CHUNKS, LANES), lambda i: (i, 0)),
                   lanes2, lanes2, lanes2, const((N_EXPERTS, LANES))],
        out_shape=[jax.ShapeDtypeStruct((T, D), F32), jax.ShapeDtypeStruct((T * ROW_CHUNKS, LANES), F32),
                   jax.ShapeDtypeStruct((2, T), I32), jax.ShapeDtypeStruct((2, T), F32),
                   jax.ShapeDtypeStruct((2, T), I32), jax.ShapeDtypeStruct((N_EXPERTS, LANES), I32)],
        scratch_shapes=[pltpu.VMEM((N_EXPERTS, LANES), F32)],
        compiler_params=_cparams(("arbitrary",)),
        name="outproj",
    )(oa, ob, x2, gfox, gdil, w_out, gffn, w_r, b_r, upper)


def _scatter_kernel(starts_ref, cnt_ref, pos_ref, hn_ref, xs_ref, ring, sems, zero_sem, *, n_steps):
    i = pl.program_id(0)
    tm = hn_ref.shape[0] // ROW_CHUNKS
    slot = lax.rem(i, 2)

    def wait_slot(s):
        for _ in range(2):
            pltpu.make_async_copy(ring.at[s], xs_ref.at[pl.ds(0, tm * ROW_CHUNKS), :], sems.at[s]).wait()

    @pl.when(i >= 2)
    def _():
        wait_slot(slot)

    ring[slot] = hn_ref[...]

    def start(r, _):
        for k in range(2):
            _row_tile_copy(ring.at[slot], r, xs_ref, pos_ref[2 * r + k], sems.at[slot]).start()
        return 0

    lax.fori_loop(0, tm, start, 0, unroll=DMA_UNROLL)

    @pl.when(i == n_steps - 1)
    def _():
        wait_slot(slot)
        if n_steps > 1:
            wait_slot(1 - slot)
        ring[0] = jnp.zeros((tm * ROW_CHUNKS, LANES), F32)

        def pad_expert(e, _, wait):
            n_pad = lax.rem(TILE_E - lax.rem(cnt_ref[e], TILE_E), TILE_E)
            first = starts_ref[e] + cnt_ref[e]
            size = TILE_E // 2
            while size >= 1:
                row0 = first + (n_pad & ~(2 * size - 1))

                @pl.when((n_pad & size) != 0)
                def _(size=size, row0=row0):
                    copy = pltpu.make_async_copy(
                        ring.at[0, pl.ds(0, size * ROW_CHUNKS), :],
                        xs_ref.at[pl.ds(pl.multiple_of(row0 * ROW_CHUNKS, ROW_CHUNKS), size * ROW_CHUNKS), :],
                        zero_sem)
                    copy.wait() if wait else copy.start()

                size //= 2
            return 0

        lax.fori_loop(0, N_EXPERTS, functools.partial(pad_expert, wait=False), 0)
        last = N_EXPERTS - 1
        used_rows = starts_ref[last] + cnt_ref[last] + lax.rem(TILE_E - lax.rem(cnt_ref[last], TILE_E), TILE_E)
        n_tail = xs_ref.shape[0] // (tm * ROW_CHUNKS) - used_rows // tm

        def tail_copy(t):
            row0 = pl.multiple_of((used_rows + t * tm) * ROW_CHUNKS, tm * ROW_CHUNKS)
            return pltpu.make_async_copy(ring.at[0], xs_ref.at[pl.ds(row0, tm * ROW_CHUNKS), :], zero_sem)

        lax.fori_loop(0, n_tail, lambda t, c: (tail_copy(t).start(), c)[1], 0)
        lax.fori_loop(0, N_EXPERTS, functools.partial(pad_expert, wait=True), 0)
        lax.fori_loop(0, n_tail, lambda t, c: (tail_copy(t).wait(), c)[1], 0)


def _scatter(starts, cnt, pos, hn, n_rows):
    T = hn.shape[0] // ROW_CHUNKS
    tm = TM_ROWS
    assert TILE_E % tm == 0 and TILE_E // 2 <= tm
    n_steps = T // tm
    grid_spec = pltpu.PrefetchScalarGridSpec(
        num_scalar_prefetch=2,
        grid=(n_steps,),
        in_specs=[pl.BlockSpec((2 * tm,), lambda i, starts, cnt: (i,), memory_space=pltpu.SMEM),
                  pl.BlockSpec((tm * ROW_CHUNKS, LANES), lambda i, starts, cnt: (i, 0))],
        out_specs=pl.BlockSpec(memory_space=pl.ANY),
        scratch_shapes=[pltpu.VMEM((2, tm * ROW_CHUNKS, LANES), F32),
                        pltpu.SemaphoreType.DMA((2,)), pltpu.SemaphoreType.DMA(())],
    )
    return pl.pallas_call(
        functools.partial(_scatter_kernel, n_steps=n_steps),
        grid_spec=grid_spec,
        out_shape=jax.ShapeDtypeStruct((n_rows * ROW_CHUNKS, LANES), F32),
        compiler_params=_cparams(("arbitrary",)),
        name="scatter_rows",
    )(starts, cnt, pos, hn)


def _experts_kernel(te_ref, tv_ref, xs_ref, wg_ref, wu_ref, wd_ref, y_ref, wg_bf, wu_bf, wd_bf):
    i = pl.program_id(0)
    new_expert = jnp.logical_or(i == 0, te_ref[i] != te_ref[jnp.maximum(i - 1, 0)])

    @pl.when(jnp.logical_and(tv_ref[i] != 0, new_expert))
    def _():
        wg_bf[...] = wg_ref[0].astype(BF16)
        wu_bf[...] = wu_ref[0].astype(BF16)
        wd_bf[...] = wd_ref[0].astype(BF16)

    @pl.when(tv_ref[i] != 0)
    def _():
        x = _load_row_tiles(xs_ref, TILE_E).astype(BF16)
        g = jnp.dot(x, wg_bf[...], preferred_element_type=F32)
        up = jnp.dot(x, wu_bf[...], preferred_element_type=F32)
        hmid = (g * jax.nn.sigmoid(g) * up).astype(BF16)
        _store_row_tiles(y_ref, jnp.dot(hmid, wd_bf[...], preferred_element_type=F32))

    @pl.when(tv_ref[i] == 0)
    def _():
        y_ref[...] = jnp.zeros_like(y_ref)


def _experts(tile_expert, tile_valid, xs, w_gate, w_up, w_down):
    n_tiles = tile_expert.shape[0]
    D = D_MODEL
    rows_spec = pl.BlockSpec((TILE_E * ROW_CHUNKS, LANES), lambda i, te, tv: (i, 0))
    grid_spec = pltpu.PrefetchScalarGridSpec(
        num_scalar_prefetch=2,
        grid=(n_tiles,),
        in_specs=[rows_spec,
                  pl.BlockSpec((1, D, D_EXPERT), lambda i, te, tv: (te[i], 0, 0)),
                  pl.BlockSpec((1, D, D_EXPERT), lambda i, te, tv: (te[i], 0, 0)),
                  pl.BlockSpec((1, D_EXPERT, D), lambda i, te, tv: (te[i], 0, 0))],
        out_specs=rows_spec,
        scratch_shapes=[pltpu.VMEM((D, D_EXPERT), BF16), pltpu.VMEM((D, D_EXPERT), BF16),
                        pltpu.VMEM((D_EXPERT, D), BF16)],
    )
    return pl.pallas_call(
        _experts_kernel,
        grid_spec=grid_spec,
        out_shape=jax.ShapeDtypeStruct(xs.shape, F32),
        compiler_params=_cparams(("arbitrary",)),
        name="experts",
    )(tile_expert, tile_valid, xs, w_gate, w_up, w_down)


def _combine_kernel(pos_ref, pos_next_ref, h_ref, gate_ref, y_ref, o_ref, ybuf, sems, *, n_steps):
    i = pl.program_id(0)
    tm = h_ref.shape[0]
    slot = lax.rem(i, 2)

    def gather(p_ref, s):
        def start(r, _):
            for k in range(2):
                _row_tile_copy(y_ref, p_ref[2 * r + k], ybuf.at[s, k], r, sems.at[s]).start()
            return 0

        lax.fori_loop(0, tm, start, 0, unroll=DMA_UNROLL)

    @pl.when(i == 0)
    def _():
        gather(pos_ref, slot)

    @pl.when(i + 1 < n_steps)
    def _():
        gather(pos_next_ref, 1 - slot)

    for k in range(2):
        pltpu.make_async_copy(y_ref.at[pl.ds(0, tm * ROW_CHUNKS), :], ybuf.at[slot, k], sems.at[slot]).wait()
    g = gate_ref[...]
    o_ref[...] = (h_ref[...] + g[:, 0:1] * _load_row_tiles(ybuf.at[slot, 0], tm)
                  + g[:, 1:2] * _load_row_tiles(ybuf.at[slot, 1], tm))


def _combine(pos, h, gate_t, y):
    T, D = h.shape
    tm = TM_ROWS
    n_steps = T // tm
    return pl.pallas_call(
        functools.partial(_combine_kernel, n_steps=n_steps),
        grid=(n_steps,),
        in_specs=[pl.BlockSpec((2 * tm,), lambda i: (i,), memory_space=pltpu.SMEM),
                  pl.BlockSpec((2 * tm,), lambda i: (jnp.minimum(i + 1, n_steps - 1),), memory_space=pltpu.SMEM),
                  pl.BlockSpec((tm, D), lambda i: (i, 0)),
                  pl.BlockSpec((tm, 2), lambda i: (i, 0)),
                  pl.BlockSpec(memory_space=pl.ANY)],
        out_specs=pl.BlockSpec((tm, D), lambda i: (i, 0)),
        out_shape=jax.ShapeDtypeStruct((T, D), F32),
        scratch_shapes=[pltpu.VMEM((2, 2, tm * ROW_CHUNKS, LANES), F32), pltpu.SemaphoreType.DMA((2,))],
        compiler_params=_cparams(("arbitrary",)),
        name="combine",
    )(pos, pos, h, gate_t, y)


def _rope_tables(S):
    inv_freq = 1.0 / (ROPE_THETA ** (jnp.arange(0, HEAD_DIM, 2, dtype=F32) / HEAD_DIM))
    ang = jnp.arange(S, dtype=F32)[:, None] * inv_freq[None, :]
    cos, sin = jnp.cos(ang), jnp.sin(ang)
    cos_t = jnp.tile(cos, (1, 2 * HEADS_PER_TILE))
    sin_t = jnp.tile(jnp.concatenate([-sin, sin], axis=1), (1, HEADS_PER_TILE))
    return cos_t, sin_t


def _layer(x, norm_mix, w_in, b_forget, q_norm_fox, k_norm_fox, q_norm_dil, k_norm_dil,
           out_norm_fox, out_norm_dil, w_out, norm_ffn, w_router_group, b_router_group,
           w_router_expert, b_router_expert, w_gate, w_up, w_down):
    B, S, D = x.shape
    T = B * S
    n_main = 6 * D_GRP

    w_main = w_in[:, :n_main].astype(BF16)
    w_vt = w_in[:, 2 * D_GRP:3 * D_GRP].T.astype(BF16)
    w_f = jnp.pad(w_in[:, n_main:], ((0, 0), (0, LANES - N_HEADS))).astype(BF16)
    b_f = jnp.pad(b_forget, (0, LANES - N_HEADS))[None, :]
    per_head = lambda g: jnp.tile(g, N_HEADS)[None, :]
    bd = jnp.kron(jnp.eye(N_HEADS, dtype=F32), jnp.ones((HEAD_DIM, HEAD_DIM), F32)).astype(BF16)
    cos_t, sin_t = _rope_tables(S)
    tri = jnp.tril(jnp.ones((TM_IN, TM_IN), F32)).astype(BF16)
    upper = jnp.triu(jnp.ones((TM_OUT, TM_OUT), F32), k=1).astype(BF16)
    w_r = jnp.concatenate([
        jnp.pad(w_router_group.T, ((0, 8 - N_GROUPS), (0, 0))),
        w_router_expert.transpose(0, 2, 1).reshape(N_EXPERTS, D)], axis=0)
    b_r = jnp.concatenate([jnp.pad(b_router_group, (0, 8 - N_GROUPS)), b_router_expert.reshape(-1)])[:, None]

    bound = (HEAD_DIM / math.sqrt(HEAD_DIM)) * LOG2E * jnp.max(jnp.abs(q_norm_fox)) * jnp.max(jnp.abs(k_norm_fox))
    shift = FOX_BOUND_SLACK * bound + 1.0
    qa, ka, va, qb, kb, vb = _inproj(
        x, norm_mix[None, :], w_main, w_vt, w_f, b_f, jnp.full((1, LANES), shift, F32),
        per_head(q_norm_fox), per_head(k_norm_fox),
        per_head(q_norm_dil), per_head(k_norm_dil), bd, cos_t, sin_t, tri)
    oa = lax.cond(2.0 * shift <= FOX_SAFE_SPAN,
                  lambda: _fox(qa, ka, va, online=False), lambda: _fox(qa, ka, va, online=True))
    dil_shift = (FOX_BOUND_SLACK * (HEAD_DIM / math.sqrt(HEAD_DIM))
                 * jnp.max(jnp.abs(q_norm_dil)) * jnp.max(jnp.abs(k_norm_dil)) + 1.0)
    dil_shift_arr = jnp.full((1, 1), dil_shift, F32)
    ob = lax.cond(2.0 * dil_shift <= DIL_SAFE_SPAN,
                  lambda: _dilated(dil_shift_arr, qb, kb, vb, bounded=True),
                  lambda: _dilated(dil_shift_arr, qb, kb, vb, bounded=False))

    h, hn, eid, gate, rank, cnt = _outproj(
        oa.reshape(T, D_GRP), ob.reshape(T, D_GRP), x.reshape(T, D), out_norm_fox[None, :],
        out_norm_dil[None, :], w_out.astype(BF16), norm_ffn[None, :], w_r, b_r, upper)

    counts = cnt[:, 0]
    padded = ((counts + TILE_E - 1) // TILE_E) * TILE_E
    ends = jnp.cumsum(padded)
    starts = ends - padded
    is_expert = eid[:, :, None] == jnp.arange(N_EXPERTS, dtype=I32)
    pos = jnp.sum(jnp.where(is_expert, starts, 0), axis=-1) + rank
    n_tiles = (2 * T) // TILE_E + N_EXPERTS
    tile_index = jnp.arange(n_tiles, dtype=I32)
    tile_valid = (tile_index * TILE_E < ends[-1]).astype(I32)
    tile_expert = jnp.minimum(
        jnp.sum((ends[None, :] <= (tile_index * TILE_E)[:, None]).astype(I32), axis=1), N_EXPERTS - 1)

    pos_flat = pos.T.reshape(-1)
    xs = _scatter(starts, counts, pos_flat, hn, n_tiles * TILE_E)
    y = _experts(tile_expert, tile_valid, xs, w_gate, w_up, w_down)
    out = _combine(pos_flat, h, gate.T, y)
    return out.reshape(B, S, D)


def kernel(x, norm_mix, w_in, b_forget, q_norm_fox, k_norm_fox, q_norm_dil, k_norm_dil, out_norm_fox,
           out_norm_dil, w_out, norm_ffn, w_router_group, b_router_group, w_router_expert,
           b_router_expert, w_gate, w_up, w_down):
    h = x
    for l in range(norm_mix.shape[0]):
        h = _layer(h, norm_mix[l], w_in[l], b_forget[l], q_norm_fox[l], k_norm_fox[l], q_norm_dil[l],
                   k_norm_dil[l], out_norm_fox[l], out_norm_dil[l], w_out[l], norm_ffn[l],
                   w_router_group[l], b_router_group[l], w_router_expert[l], b_router_expert[l],
                   w_gate[l], w_up[l], w_down[l])
    return h
```

```python
import functools
import math

import jax
import jax.numpy as jnp
from jax import lax
from jax.experimental import pallas as pl
from jax.experimental.pallas import tpu as pltpu

F32 = jnp.float32
BF16 = jnp.bfloat16
I32 = jnp.int32

D_MODEL = 1024
HEAD_DIM = 64
N_HEADS = 8
D_GRP = N_HEADS * HEAD_DIM
LANES = 128
HEADS_PER_TILE = LANES // HEAD_DIM
N_PAIRS = D_GRP // LANES
DIL_PATTERNS = ((128, 1), (512, 4), (2048, 16))
BLOCK = 128
ROPE_THETA = 10000.0
N_GROUPS = 4
EXPERTS_PER_GROUP = 8
N_EXPERTS = N_GROUPS * EXPERTS_PER_GROUP
D_EXPERT = 512
EPS = 1e-6
NEG = -1e30
LOG2E = 1.4426950408889634

TM_IN = 512
TQ = 1024
TK_WIDE = 1024
TK_SUB = 256
FOX_AHEAD = 4
FOX_BOUND_SLACK = 1.02
DIL_SAFE_SPAN = 69.0
FOX_SAFE_SPAN = 100.0
DIL_SPAN = 2048
DIL_GROUP = 8
DIL_PRE = 4
TM_OUT = 512
TM_ROWS = 256
TILE_E = 512
ROUTER_ROWS = 8 + N_EXPERTS
ROW_CHUNKS = D_MODEL // LANES
DMA_UNROLL = 8
VMEM_LIMIT = 56 * 1024 * 1024


def _cparams(sem, flags=None):
    return pltpu.CompilerParams(dimension_semantics=sem, vmem_limit_bytes=VMEM_LIMIT, flags=flags)


def _inproj_kernel(x_ref, gmix_ref, w_ref, wvt_ref, wf_ref, bf_ref, shift_ref, gqa_ref, gka_ref, gqb_ref, gkb_ref,
                   bd_ref, cos_ref, sin_ref, tri_ref,
                   qa_ref, ka_ref, va_ref, qb_ref, kb_ref, vb_ref, carry_ref):
    @pl.when(pl.program_id(1) == 0)
    def _():
        carry_ref[...] = jnp.zeros_like(carry_ref)

    x = x_ref[0]
    ms = jnp.mean(x * x, axis=-1, keepdims=True)
    xn = (x * lax.rsqrt(ms + EPS) * gmix_ref[...]).astype(BF16)

    def seg(j):
        return jnp.dot(xn, w_ref[:, j * D_GRP:(j + 1) * D_GRP], preferred_element_type=F32)

    def head_norm(y, g_ref, scale):
        ss = jnp.dot((y * y).astype(BF16), bd_ref[...], preferred_element_type=F32) * (1.0 / HEAD_DIM)
        return y * lax.rsqrt(ss + EPS) * (g_ref[...] * scale)

    cos = cos_ref[...]
    sin = sin_ref[...]
    lane = lax.broadcasted_iota(I32, (x.shape[0], LANES), 1)
    first_half = (lane % HEAD_DIM) < (HEAD_DIM // 2)

    def rope(y):
        outs = []
        for j in range(N_PAIRS):
            ys = y[:, j * LANES:(j + 1) * LANES]
            partner = jnp.where(first_half, pltpu.roll(ys, LANES - HEAD_DIM // 2, 1),
                                pltpu.roll(ys, HEAD_DIM // 2, 1))
            outs.append(ys * cos + partner * sin)
        return jnp.concatenate(outs, axis=1)

    scale = 1.0 / math.sqrt(HEAD_DIM)
    va_ref[0] = lax.dot_general(wvt_ref[...], xn, (((1,), (1,)), ((), ())),
                                preferred_element_type=F32).astype(va_ref.dtype)
    qb_ref[0] = rope(head_norm(seg(3), gqb_ref, scale)).astype(qb_ref.dtype)
    kb_ref[0] = rope(head_norm(seg(4), gkb_ref, 1.0)).astype(kb_ref.dtype)
    vb_ref[0] = seg(5).astype(vb_ref.dtype)

    fa = jnp.dot(xn, wf_ref[...], preferred_element_type=F32) + bf_ref[...]
    logf = jnp.minimum(fa, 0.0) - jnp.log1p(jnp.exp(-jnp.abs(fa)))
    hi = logf.astype(BF16)
    mid = (logf - hi.astype(F32)).astype(BF16)
    lo = (logf - hi.astype(F32) - mid.astype(F32)).astype(BF16)
    parts = jnp.dot(tri_ref[...], jnp.concatenate([hi, mid, lo], axis=1), preferred_element_type=F32)
    c = parts[:, :LANES] + parts[:, LANES:2 * LANES] + parts[:, 2 * LANES:] + carry_ref[...]
    carry_ref[...] = c[c.shape[0] - 1:, :]

    qa = head_norm(seg(0), gqa_ref, scale * LOG2E)
    ka = head_norm(seg(1), gka_ref, 1.0)
    c2 = c * LOG2E
    ones = (jnp.where((lane >= HEAD_DIM + 3) & (lane < HEAD_DIM + 6), 1.0, 0.0)
            - jnp.where(lane == HEAD_DIM + 6, shift_ref[...], 0.0))
    for h in range(N_HEADS):
        cb = jnp.broadcast_to(c2[:, h:h + 1], (x.shape[0], LANES))
        hi = cb.astype(BF16).astype(F32)
        mid = (cb - hi).astype(BF16).astype(F32)
        lo = cb - hi - mid
        pieces = jnp.where(lane == HEAD_DIM, hi, jnp.where(lane == HEAD_DIM + 1, mid,
                           jnp.where(lane == HEAD_DIM + 2, lo, 0.0)))
        q_extra = pieces + ones
        k_extra = jnp.where(((lane >= HEAD_DIM) & (lane < HEAD_DIM + 3)) | (lane == HEAD_DIM + 6), 1.0, 0.0) \
            - pltpu.roll(pieces, 3, 1)
        j, odd = divmod(h, HEADS_PER_TILE)
        qp = qa[:, j * LANES:(j + 1) * LANES]
        kp = ka[:, j * LANES:(j + 1) * LANES]
        if odd:
            qp = pltpu.roll(qp, HEAD_DIM, 1)
            kp = pltpu.roll(kp, HEAD_DIM, 1)
        qa_ref[0, h] = jnp.where(lane < HEAD_DIM, qp, q_extra).astype(qa_ref.dtype)
        ka_ref[0, h] = jnp.where(lane < HEAD_DIM, kp, k_extra).astype(ka_ref.dtype)


def _inproj(x, gmix, w_main, w_vt, w_f, b_f, shift, gqa, gka, gqb, gkb, bd, cos_t, sin_t, tri):
    B, S, D = x.shape
    tm = TM_IN
    const = lambda shape: pl.BlockSpec(shape, lambda b, i: (0,) * len(shape))
    tok = lambda w, dt: jax.ShapeDtypeStruct((B, S, w), dt)
    tok_spec = lambda w: pl.BlockSpec((1, tm, w), lambda b, i: (b, i, 0))
    head_spec = pl.BlockSpec((1, N_HEADS, tm, LANES), lambda b, i: (b, 0, i, 0))
    head_shape = jax.ShapeDtypeStruct((B, N_HEADS, S, LANES), BF16)
    return pl.pallas_call(
        _inproj_kernel,
        grid=(B, S // tm),
        in_specs=[tok_spec(D), const((1, D)), const(w_main.shape), const(w_vt.shape), const(w_f.shape),
                  const((1, LANES)), const((1, LANES)),
                  const((1, D_GRP)), const((1, D_GRP)), const((1, D_GRP)), const((1, D_GRP)),
                  const((D_GRP, D_GRP)),
                  pl.BlockSpec((tm, LANES), lambda b, i: (i, 0)),
                  pl.BlockSpec((tm, LANES), lambda b, i: (i, 0)),
                  const((tm, tm))],
        out_specs=[head_spec, head_spec, pl.BlockSpec((1, D_GRP, tm), lambda b, i: (b, 0, i))]
        + [tok_spec(D_GRP)] * 3,
        out_shape=[head_shape, head_shape, jax.ShapeDtypeStruct((B, D_GRP, S), BF16),
                   tok(D_GRP, F32), tok(D_GRP, F32), tok(D_GRP, F32)],
        scratch_shapes=[pltpu.VMEM((1, LANES), F32)],
        compiler_params=_cparams(("arbitrary", "arbitrary")),
        name="inproj",
    )(x, gmix, w_main, w_vt, w_f, b_f, shift, gqa, gka, gqb, gkb, bd, cos_t, sin_t, tri)


def _fox_kernel(q_ref, k_ref, v_ref, o_ref, *, online):
    qi = pl.program_id(2)
    tq = q_ref.shape[2]

    def step(start, width, carry, diag):
        carry = list(carry)
        sub = min(TK_SUB, width)
        chunks = [(c, j) for c in range(width // sub) for j in range(HEADS_PER_TILE)]
        def first_query(c):
            return c * sub if diag else 0

        def score(c, j):
            k = k_ref[0, j, pl.ds(start + c * sub, sub), :]
            q = q_ref[0, j, first_query(c):, :]
            return lax.dot_general(k, q, (((1,), (1,)), ((), ())), preferred_element_type=F32)

        def tail(full, lo, new):
            return new if lo == 0 else jnp.concatenate([full[:, :lo], new], axis=1)

        scores = {cj: score(*cj) for cj in chunks[:FOX_AHEAD]}
        for n, (c, j) in enumerate(chunks):
                if n + FOX_AHEAD < len(chunks):
                    nxt = chunks[n + FOX_AHEAD]
                    scores[nxt] = score(*nxt)
                m, l, acc = carry[j]
                lo = first_query(c)
                vt = v_ref[0, :, pl.ds(start + c * sub, sub)]
                s = scores.pop((c, j))
                if diag:
                    key = lax.broadcasted_iota(I32, s.shape, 0)
                    qry = lax.broadcasted_iota(I32, s.shape, 1)
                    s = jnp.where(key <= qry, s, NEG)
                if online:
                    m_new = jnp.maximum(m[:, lo:], jnp.max(s, axis=0, keepdims=True))
                    alpha = jnp.exp2(m[:, lo:] - m_new)
                    p = jnp.exp2(s - m_new)
                    l_new = alpha * l[:, lo:] + jnp.sum(p, axis=0, keepdims=True)
                    acc_new = alpha * acc[:, lo:] + jnp.dot(vt, p.astype(BF16), preferred_element_type=F32)
                    m = tail(m, lo, m_new)
                else:
                    p = jnp.exp2(s)
                    l_new = l[:, lo:] + jnp.sum(p, axis=0, keepdims=True)
                    acc_new = acc[:, lo:] + jnp.dot(vt, p.astype(BF16), preferred_element_type=F32)
                carry[j] = (m, tail(l, lo, l_new), tail(acc, lo, acc_new))
        return tuple(carry)

    init = tuple((jnp.full((1, tq), NEG, F32), jnp.zeros((1, tq), F32), jnp.zeros((LANES, tq), F32))
                 for _ in range(HEADS_PER_TILE))
    per_wide = TK_WIDE // tq
    n_wide = qi // per_wide
    carry = lax.fori_loop(
        0, n_wide, lambda i, c: step(pl.multiple_of(i * TK_WIDE, TK_WIDE), TK_WIDE, c, False), init)
    for extra in range(per_wide - 1):
        carry = lax.cond(qi - n_wide * per_wide > extra,
                         lambda c, e=extra: step(pl.multiple_of((n_wide * per_wide + e) * tq, tq), tq, c, False),
                         lambda c: c, carry)
    carry = step(pl.multiple_of(qi * tq, tq), tq, carry, True)
    outs = [acc / l for (_, l, acc) in carry]
    feat = lax.broadcasted_iota(I32, (LANES, tq), 0)
    o_ref[0] = jnp.where(feat < HEAD_DIM, outs[0], outs[1]).T.astype(o_ref.dtype)


def _fox(qa, ka, va_t, online):
    B, _, S, _ = qa.shape
    return pl.pallas_call(
        functools.partial(_fox_kernel, online=online),
        grid=(B, N_PAIRS, S // TQ),
        in_specs=[pl.BlockSpec((1, HEADS_PER_TILE, TQ, LANES), lambda b, hp, i: (b, hp, i, 0)),
                  pl.BlockSpec((1, HEADS_PER_TILE, S, LANES), lambda b, hp, i: (b, hp, 0, 0)),
                  pl.BlockSpec((1, LANES, S), lambda b, hp, i: (b, hp, 0))],
        out_specs=pl.BlockSpec((1, TQ, LANES), lambda b, hp, i: (b, i, hp)),
        out_shape=jax.ShapeDtypeStruct((B, S, D_GRP), F32),
        compiler_params=_cparams(("arbitrary", "arbitrary", "arbitrary")),
        name="fox_online" if online else "fox",
    )(qa, ka, va_t)


def _dilated_kernel(shift_ref, q_ref, kp_ref, kc_ref, vp_ref, vc_ref, o_ref, qq, kk, vv, qq4, kk4, vv4, osc, lsc,
                    *, bounded):
    u = pl.program_id(1)
    span = q_ref.shape[1]
    qq[...] = q_ref[0]
    kk[0:span, :] = kp_ref[0]
    kk[span:2 * span, :] = kc_ref[0]
    vv[0:span, :] = vp_ref[0]
    vv[span:2 * span, :] = vc_ref[0]
    for src, dst in ((qq, qq4), (kk, kk4), (vv, vv4)):
        part = src.shape[0] // DIL_PRE
        for a in range(DIL_PRE):
            dst[a * part:(a + 1) * part, :] = src[pl.ds(a, part, stride=DIL_PRE), :]

    def rows(buf, buf4, start, n, d):
        if d % DIL_PRE:
            return buf[pl.ds(start, n, stride=d), :]
        part = buf4.shape[0] // DIL_PRE
        a = lax.rem(start, DIL_PRE)
        return buf4[pl.ds(a * part + lax.div(start, DIL_PRE), n, stride=d // DIL_PRE), :]

    lane = lax.broadcasted_iota(I32, (BLOCK, LANES), 1)
    ql = lax.broadcasted_iota(I32, (BLOCK, 2 * BLOCK), 0)
    kl = lax.broadcasted_iota(I32, (BLOCK, 2 * BLOCK), 1)
    dist = ql + BLOCK - kl
    band = (dist >= 0) & (dist <= BLOCK)
    live = -shift_ref[0, 0] if bounded else 0.0
    bias = jnp.where(band, live, NEG)
    bias_first = jnp.where(band & (kl >= BLOCK), live, NEG)

    def scores(q_start, k_start, d, first):
        qs = rows(qq, qq4, q_start, BLOCK, d).astype(BF16)
        ks = rows(kk, kk4, k_start, 2 * BLOCK, d).astype(BF16)
        mask = jnp.where(first, bias_first, bias)
        out = []
        for j in range(HEADS_PER_TILE):
            qj = jnp.where(lane // HEAD_DIM == j, qs, jnp.zeros_like(qs))
            out.append(lax.dot_general(qj, ks, (((1,), (1,)), ((), ())), preferred_element_type=F32) + mask)
        return out

    def finish(s_heads, k_start, d):
        vs = rows(vv, vv4, k_start, 2 * BLOCK, d).astype(BF16)
        o_heads, lse_heads = [], []
        for s in s_heads:
            if bounded:
                p = jnp.exp(s)
                o_heads.append(jnp.dot(p.astype(BF16), vs, preferred_element_type=F32))
                lse_heads.append(jnp.sum(p, axis=-1, keepdims=True))
            else:
                m = jnp.max(s, axis=-1, keepdims=True)
                p = jnp.exp(s - m)
                l = jnp.sum(p, axis=-1, keepdims=True)
                o_heads.append(jnp.dot((p / l).astype(BF16), vs, preferred_element_type=F32))
                lse_heads.append(m + jnp.log(l))
        o = jnp.where(lane < HEAD_DIM, o_heads[0], o_heads[1])
        lse = jnp.where(lane < HEAD_DIM, lse_heads[0], lse_heads[1])
        return o, lse

    for pidx, (window, d) in enumerate(DIL_PATTERNS):
        assert window // d == BLOCK
        unit = d * BLOCK
        n_problems = (span // unit) * d
        assert n_problems % DIL_GROUP == 0

        def body(g, _, pidx=pidx, d=d, unit=unit):
            starts, s_all = [], []
            for t in range(DIL_GROUP):
                idx = g * DIL_GROUP + t
                w = idx // d
                q_start = w * unit + (idx - w * d)
                k_start = span - unit + q_start
                starts.append((q_start, k_start))
                s_all.append(scores(q_start, k_start, d, jnp.logical_and(u == 0, w == 0)))
            for (q_start, k_start), s_heads in zip(starts, s_all):
                o, lse = finish(s_heads, k_start, d)
                osc[pidx, pl.ds(q_start, BLOCK, stride=d), :] = o
                lsc[pidx, pl.ds(q_start, BLOCK, stride=d), :] = lse
            return 0

        lax.fori_loop(0, n_problems // DIL_GROUP, body, 0)

    if bounded:
        num = osc[0] + osc[1] + osc[2]
        den = lsc[0] + lsc[1] + lsc[2]
    else:
        mx = jnp.maximum(jnp.maximum(lsc[0], lsc[1]), lsc[2])
        num = jnp.zeros((span, LANES), F32)
        den = jnp.zeros((span, LANES), F32)
        for pidx in range(len(DIL_PATTERNS)):
            e = jnp.exp(lsc[pidx] - mx)
            num = num + e * osc[pidx]
            den = den + e
    o_ref[0] = (num / den).astype(o_ref.dtype)


def _dilated(shift, qb, kb, vb, bounded):
    B, S, _ = qb.shape
    span = DIL_SPAN
    cur = pl.BlockSpec((1, span, LANES), lambda b, u, hp: (b, u, hp))
    prev = pl.BlockSpec((1, span, LANES), lambda b, u, hp: (b, jnp.maximum(u - 1, 0), hp))
    return pl.pallas_call(
        functools.partial(_dilated_kernel, bounded=bounded),
        grid=(B, S // span, N_PAIRS),
        in_specs=[pl.BlockSpec(memory_space=pltpu.SMEM), cur, prev, cur, prev, cur],
        out_specs=cur,
        out_shape=jax.ShapeDtypeStruct((B, S, D_GRP), F32),
        scratch_shapes=[pltpu.VMEM((span, LANES), F32),
                        pltpu.VMEM((2 * span, LANES), F32), pltpu.VMEM((2 * span, LANES), F32),
                        pltpu.VMEM((span, LANES), F32),
                        pltpu.VMEM((2 * span, LANES), F32), pltpu.VMEM((2 * span, LANES), F32),
                        pltpu.VMEM((len(DIL_PATTERNS), span, LANES), F32),
                        pltpu.VMEM((len(DIL_PATTERNS), span, LANES), F32)],
        compiler_params=_cparams(("arbitrary", "arbitrary", "arbitrary")),
        name="dilated" if bounded else "dilated_exact",
    )(shift, qb, kb, kb, vb, vb)


def _store_row_tiles(ref, x):
    n = x.shape[0]
    for c in range(ROW_CHUNKS):
        ref[pl.ds(c, n, stride=ROW_CHUNKS), :] = x[:, c * LANES:(c + 1) * LANES]


def _load_row_tiles(ref, n):
    return jnp.concatenate([ref[pl.ds(c, n, stride=ROW_CHUNKS), :] for c in range(ROW_CHUNKS)], axis=1)


def _row_tile_copy(src_ref, src_row, dst_ref, dst_row, sem):
    src = src_ref.at[pl.ds(pl.multiple_of(src_row * ROW_CHUNKS, ROW_CHUNKS), ROW_CHUNKS), :]
    dst = dst_ref.at[pl.ds(pl.multiple_of(dst_row * ROW_CHUNKS, ROW_CHUNKS), ROW_CHUNKS), :]
    return pltpu.make_async_copy(src, dst, sem)


def _outproj_kernel(oa_ref, ob_ref, x_ref, gfox_ref, gdil_ref, wo_ref, gffn_ref, wr_ref, br_ref, upper_ref,
                    h_ref, hn_ref, eid_ref, gate_ref, rank_ref, cnt_ref, run_ref):
    @pl.when(pl.program_id(0) == 0)
    def _():
        run_ref[...] = jnp.zeros_like(run_ref)

    def norm(y, g):
        ms = jnp.mean(y * y, axis=-1, keepdims=True)
        return y * lax.rsqrt(ms + EPS) * g

    a = norm(oa_ref[...], gfox_ref[...]).astype(BF16)
    b = norm(ob_ref[...], gdil_ref[...]).astype(BF16)
    mix = (jnp.dot(a, wo_ref[0:D_GRP, :], preferred_element_type=F32)
           + jnp.dot(b, wo_ref[D_GRP:2 * D_GRP, :], preferred_element_type=F32))
    h = x_ref[...] + mix
    h_ref[...] = h
    hn = norm(h, gffn_ref[...])
    _store_row_tiles(hn_ref, hn)

    z = lax.dot_general(wr_ref[...], hn, (((1,), (1,)), ((), ())), preferred_element_type=F32,
                        precision=lax.Precision.HIGHEST) + br_ref[...]
    tm = z.shape[1]
    best = z[0:1, :]
    g_sel = jnp.zeros((1, tm), I32)
    for g in range(1, N_GROUPS):
        better = z[g:g + 1, :] > best
        g_sel = jnp.where(better, g, g_sel)
        best = jnp.maximum(best, z[g:g + 1, :])
    den = jnp.zeros((1, tm), F32)
    for g in range(N_GROUPS):
        den = den + jnp.exp(z[g:g + 1, :] - best)
    pg_top = 1.0 / den

    ze = jnp.zeros((EXPERTS_PER_GROUP, tm), F32)
    for g in range(N_GROUPS):
        ze = jnp.where(g_sel == g, z[8 + g * EXPERTS_PER_GROUP:8 + (g + 1) * EXPERTS_PER_GROUP, :], ze)
    e_iota = lax.broadcasted_iota(I32, ze.shape, 0)
    v1 = jnp.max(ze, axis=0, keepdims=True)
    i1 = jnp.min(jnp.where(ze == v1, e_iota, EXPERTS_PER_GROUP), axis=0, keepdims=True)
    ze2 = jnp.where(e_iota == i1, -jnp.inf, ze)
    v2 = jnp.max(ze2, axis=0, keepdims=True)
    i2 = jnp.min(jnp.where(ze2 == v2, e_iota, EXPERTS_PER_GROUP), axis=0, keepdims=True)
    e2 = jnp.exp(v2 - v1)
    inv = 1.0 / (1.0 + e2)
    gate1 = inv * pg_top
    gate2 = e2 * inv * pg_top
    eid1 = g_sel * EXPERTS_PER_GROUP + i1
    eid2 = g_sel * EXPERTS_PER_GROUP + i2

    x_iota = lax.broadcasted_iota(I32, (N_EXPERTS, tm), 0)
    hot1 = x_iota == eid1
    hot2 = x_iota == eid2
    multi = jnp.logical_or(hot1, hot2)
    before = jnp.dot(multi.astype(BF16), upper_ref[...], preferred_element_type=F32)
    slot = before + run_ref[:, 0:1]
    rank1 = jnp.sum(jnp.where(hot1, slot, 0.0), axis=0, keepdims=True)
    rank2 = jnp.sum(jnp.where(hot2, slot, 0.0), axis=0, keepdims=True)
    run_ref[...] = run_ref[...] + jnp.sum(multi.astype(F32), axis=1, keepdims=True)

    eid_ref[...] = jnp.concatenate([eid1, eid2], axis=0)
    gate_ref[...] = jnp.concatenate([gate1, gate2], axis=0)
    rank_ref[...] = jnp.concatenate([rank1, rank2], axis=0).astype(I32)
    cnt_ref[...] = run_ref[...].astype(I32)


def _outproj(oa, ob, x2, gfox, gdil, w_out, gffn, w_r, b_r, upper):
    T, D = x2.shape
    tm = TM_OUT
    const = lambda shape: pl.BlockSpec(shape, lambda i: (0,) * len(shape))
    tok = lambda w: pl.BlockSpec((tm, w), lambda i: (i, 0))
    lanes2 = pl.BlockSpec((2, tm), lambda i: (0, i))
    return pl.pallas_call(
        _outproj_kernel,
        grid=(T // tm,),
        in_specs=[tok(D_GRP), tok(D_GRP), tok(D), const((1, D_GRP)), const((1, D_GRP)), const((D, D)),
                  const((1, D)), const((ROUTER_ROWS, D)), const((ROUTER_ROWS, 1)), const((tm, tm))],
        out_specs=[tok(D), pl.BlockSpec((tm * ROW_CHUNKS, LANES), lambda i: (i, 0)),
                   lanes2, lanes2, lanes2, const((N_EXPERTS, LANES))],
        out_shape=[jax.ShapeDtypeStruct((T, D), F32), jax.ShapeDtypeStruct((T * ROW_CHUNKS, LANES), F32),
                   jax.ShapeDtypeStruct((2, T), I32), jax.ShapeDtypeStruct((2, T), F32),
                   jax.ShapeDtypeStruct((2, T), I32), jax.ShapeDtypeStruct((N_EXPERTS, LANES), I32)],
        scratch_shapes=[pltpu.VMEM((N_EXPERTS, LANES), F32)],
        compiler_params=_cparams(("arbitrary",)),
        name="outproj",
    )(oa, ob, x2, gfox, gdil, w_out, gffn, w_r, b_r, upper)


def _scatter_kernel(starts_ref, cnt_ref, pos_ref, hn_ref, xs_ref, ring, sems, zero_sem, *, n_steps):
    i = pl.program_id(0)
    tm = hn_ref.shape[0] // ROW_CHUNKS
    slot = lax.rem(i, 2)

    def wait_slot(s):
        for _ in range(2):
            pltpu.make_async_copy(ring.at[s], xs_ref.at[pl.ds(0, tm * ROW_CHUNKS), :], sems.at[s]).wait()

    @pl.when(i >= 2)
    def _():
        wait_slot(slot)

    ring[slot] = hn_ref[...]

    def start(r, _):
        for k in range(2):
            _row_tile_copy(ring.at[slot], r, xs_ref, pos_ref[2 * r + k], sems.at[slot]).start(priority=k)
        return 0

    lax.fori_loop(0, tm, start, 0, unroll=DMA_UNROLL)

    @pl.when(i == n_steps - 1)
    def _():
        wait_slot(slot)
        if n_steps > 1:
            wait_slot(1 - slot)
        ring[0] = jnp.zeros((tm * ROW_CHUNKS, LANES), F32)

        def pad_expert(e, _, wait):
            n_pad = lax.rem(TILE_E - lax.rem(cnt_ref[e], TILE_E), TILE_E)
            first = starts_ref[e] + cnt_ref[e]
            size = TILE_E // 2
            while size >= 1:
                row0 = first + (n_pad & ~(2 * size - 1))

                @pl.when((n_pad & size) != 0)
                def _(size=size, row0=row0):
                    copy = pltpu.make_async_copy(
                        ring.at[0, pl.ds(0, size * ROW_CHUNKS), :],
                        xs_ref.at[pl.ds(pl.multiple_of(row0 * ROW_CHUNKS, ROW_CHUNKS), size * ROW_CHUNKS), :],
                        zero_sem)
                    copy.wait() if wait else copy.start()

                size //= 2
            return 0

        lax.fori_loop(0, N_EXPERTS, functools.partial(pad_expert, wait=False), 0)
        last = N_EXPERTS - 1
        used_rows = starts_ref[last] + cnt_ref[last] + lax.rem(TILE_E - lax.rem(cnt_ref[last], TILE_E), TILE_E)
        n_tail = xs_ref.shape[0] // (tm * ROW_CHUNKS) - used_rows // tm

        def tail_copy(t):
            row0 = pl.multiple_of((used_rows + t * tm) * ROW_CHUNKS, tm * ROW_CHUNKS)
            return pltpu.make_async_copy(ring.at[0], xs_ref.at[pl.ds(row0, tm * ROW_CHUNKS), :], zero_sem)

        lax.fori_loop(0, n_tail, lambda t, c: (tail_copy(t).start(), c)[1], 0)
        lax.fori_loop(0, N_EXPERTS, functools.partial(pad_expert, wait=True), 0)
        lax.fori_loop(0, n_tail, lambda t, c: (tail_copy(t).wait(), c)[1], 0)


def _scatter(starts, cnt, pos, hn, n_rows):
    T = hn.shape[0] // ROW_CHUNKS
    tm = TM_ROWS
    assert TILE_E % tm == 0 and TILE_E // 2 <= tm
    n_steps = T // tm
    grid_spec = pltpu.PrefetchScalarGridSpec(
        num_scalar_prefetch=2,
        grid=(n_steps,),
        in_specs=[pl.BlockSpec((2 * tm,), lambda i, starts, cnt: (i,), memory_space=pltpu.SMEM),
                  pl.BlockSpec((tm * ROW_CHUNKS, LANES), lambda i, starts, cnt: (i, 0))],
        out_specs=pl.BlockSpec(memory_space=pl.ANY),
        scratch_shapes=[pltpu.VMEM((2, tm * ROW_CHUNKS, LANES), F32),
                        pltpu.SemaphoreType.DMA((2,)), pltpu.SemaphoreType.DMA(())],
    )
    return pl.pallas_call(
        functools.partial(_scatter_kernel, n_steps=n_steps),
        grid_spec=grid_spec,
        out_shape=jax.ShapeDtypeStruct((n_rows * ROW_CHUNKS, LANES), F32),
        compiler_params=_cparams(("arbitrary",)),
        name="scatter_rows",
    )(starts, cnt, pos, hn)


def _experts_kernel(te_ref, tv_ref, xs_ref, wg_ref, wu_ref, wd_ref, y_ref, wg_bf, wu_bf, wd_bf):
    i = pl.program_id(0)
    new_expert = jnp.logical_or(i == 0, te_ref[i] != te_ref[jnp.maximum(i - 1, 0)])

    @pl.when(jnp.logical_and(tv_ref[i] != 0, new_expert))
    def _():
        wg_bf[...] = wg_ref[0].astype(BF16)
        wu_bf[...] = wu_ref[0].astype(BF16)
        wd_bf[...] = wd_ref[0].astype(BF16)

    @pl.when(tv_ref[i] != 0)
    def _():
        x = _load_row_tiles(xs_ref, TILE_E).astype(BF16)
        g = jnp.dot(x, wg_bf[...], preferred_element_type=F32)
        up = jnp.dot(x, wu_bf[...], preferred_element_type=F32)
        hmid = (g * jax.nn.sigmoid(g) * up).astype(BF16)
        _store_row_tiles(y_ref, jnp.dot(hmid, wd_bf[...], preferred_element_type=F32))

    @pl.when(tv_ref[i] == 0)
    def _():
        y_ref[...] = jnp.zeros_like(y_ref)


def _experts(tile_expert, tile_valid, xs, w_gate, w_up, w_down):
    n_tiles = tile_expert.shape[0]
    D = D_MODEL
    rows_spec = pl.BlockSpec((TILE_E * ROW_CHUNKS, LANES), lambda i, te, tv: (i, 0))
    grid_spec = pltpu.PrefetchScalarGridSpec(
        num_scalar_prefetch=2,
        grid=(n_tiles,),
        in_specs=[rows_spec,
                  pl.BlockSpec((1, D, D_EXPERT), lambda i, te, tv: (te[i], 0, 0)),
                  pl.BlockSpec((1, D, D_EXPERT), lambda i, te, tv: (te[i], 0, 0)),
                  pl.BlockSpec((1, D_EXPERT, D), lambda i, te, tv: (te[i], 0, 0))],
        out_specs=rows_spec,
        scratch_shapes=[pltpu.VMEM((D, D_EXPERT), BF16), pltpu.VMEM((D, D_EXPERT), BF16),
                        pltpu.VMEM((D_EXPERT, D), BF16)],
    )
    return pl.pallas_call(
        _experts_kernel,
        grid_spec=grid_spec,
        out_shape=jax.ShapeDtypeStruct(xs.shape, F32),
        compiler_params=_cparams(("arbitrary",)),
        name="experts",
    )(tile_expert, tile_valid, xs, w_gate, w_up, w_down)


def _combine_kernel(pos_ref, pos_next_ref, h_ref, gate_ref, y_ref, o_ref, ybuf, sems, *, n_steps):
    i = pl.program_id(0)
    tm = h_ref.shape[0]
    slot = lax.rem(i, 2)

    def gather(p_ref, s):
        def start(r, _):
            for k in range(2):
                _row_tile_copy(y_ref, p_ref[2 * r + k], ybuf.at[s, k], r, sems.at[s]).start(priority=k)
            return 0

        lax.fori_loop(0, tm, start, 0, unroll=DMA_UNROLL)

    @pl.when(i == 0)
    def _():
        gather(pos_ref, slot)

    @pl.when(i + 1 < n_steps)
    def _():
        gather(pos_next_ref, 1 - slot)

    for k in range(2):
        pltpu.make_async_copy(y_ref.at[pl.ds(0, tm * ROW_CHUNKS), :], ybuf.at[slot, k], sems.at[slot]).wait()
    g = gate_ref[...]
    o_ref[...] = (h_ref[...] + g[:, 0:1] * _load_row_tiles(ybuf.at[slot, 0], tm)
                  + g[:, 1:2] * _load_row_tiles(ybuf.at[slot, 1], tm))


def _combine(pos, h, gate_t, y):
    T, D = h.shape
    tm = TM_ROWS
    n_steps = T // tm
    return pl.pallas_call(
        functools.partial(_combine_kernel, n_steps=n_steps),
        grid=(n_steps,),
        in_specs=[pl.BlockSpec((2 * tm,), lambda i: (i,), memory_space=pltpu.SMEM),
                  pl.BlockSpec((2 * tm,), lambda i: (jnp.minimum(i + 1, n_steps - 1),), memory_space=pltpu.SMEM),
                  pl.BlockSpec((tm, D), lambda i: (i, 0)),
                  pl.BlockSpec((tm, 2), lambda i: (i, 0)),
                  pl.BlockSpec(memory_space=pl.ANY)],
        out_specs=pl.BlockSpec((tm, D), lambda i: (i, 0)),
        out_shape=jax.ShapeDtypeStruct((T, D), F32),
        scratch_shapes=[pltpu.VMEM((2, 2, tm * ROW_CHUNKS, LANES), F32), pltpu.SemaphoreType.DMA((2,))],
        compiler_params=_cparams(("arbitrary",)),
        name="combine",
    )(pos, pos, h, gate_t, y)


def _rope_tables(S):
    inv_freq = 1.0 / (ROPE_THETA ** (jnp.arange(0, HEAD_DIM, 2, dtype=F32) / HEAD_DIM))
    ang = jnp.arange(S, dtype=F32)[:, None] * inv_freq[None, :]
    cos, sin = jnp.cos(ang), jnp.sin(ang)
    cos_t = jnp.tile(cos, (1, 2 * HEADS_PER_TILE))
    sin_t = jnp.tile(jnp.concatenate([-sin, sin], axis=1), (1, HEADS_PER_TILE))
    return cos_t, sin_t


def _layer(x, norm_mix, w_in, b_forget, q_norm_fox, k_norm_fox, q_norm_dil, k_norm_dil,
           out_norm_fox, out_norm_dil, w_out, norm_ffn, w_router_group, b_router_group,
           w_router_expert, b_router_expert, w_gate, w_up, w_down):
    B, S, D = x.shape
    T = B * S
    n_main = 6 * D_GRP

    w_main = w_in[:, :n_main].astype(BF16)
    w_vt = w_in[:, 2 * D_GRP:3 * D_GRP].T.astype(BF16)
    w_f = jnp.pad(w_in[:, n_main:], ((0, 0), (0, LANES - N_HEADS))).astype(BF16)
    b_f = jnp.pad(b_forget, (0, LANES - N_HEADS))[None, :]
    per_head = lambda g: jnp.tile(g, N_HEADS)[None, :]
    bd = jnp.kron(jnp.eye(N_HEADS, dtype=F32), jnp.ones((HEAD_DIM, HEAD_DIM), F32)).astype(BF16)
    cos_t, sin_t = _rope_tables(S)
    tri = jnp.tril(jnp.ones((TM_IN, TM_IN), F32)).astype(BF16)
    upper = jnp.triu(jnp.ones((TM_OUT, TM_OUT), F32), k=1).astype(BF16)
    w_r = jnp.concatenate([
        jnp.pad(w_router_group.T, ((0, 8 - N_GROUPS), (0, 0))),
        w_router_expert.transpose(0, 2, 1).reshape(N_EXPERTS, D)], axis=0)
    b_r = jnp.concatenate([jnp.pad(b_router_group, (0, 8 - N_GROUPS)), b_router_expert.reshape(-1)])[:, None]

    bound = (HEAD_DIM / math.sqrt(HEAD_DIM)) * LOG2E * jnp.max(jnp.abs(q_norm_fox)) * jnp.max(jnp.abs(k_norm_fox))
    shift = FOX_BOUND_SLACK * bound + 1.0
    qa, ka, va, qb, kb, vb = _inproj(
        x, norm_mix[None, :], w_main, w_vt, w_f, b_f, jnp.full((1, LANES), shift, F32),
        per_head(q_norm_fox), per_head(k_norm_fox),
        per_head(q_norm_dil), per_head(k_norm_dil), bd, cos_t, sin_t, tri)
    oa = lax.cond(2.0 * shift <= FOX_SAFE_SPAN,
                  lambda: _fox(qa, ka, va, online=False), lambda: _fox(qa, ka, va, online=True))
    dil_shift = (FOX_BOUND_SLACK * (HEAD_DIM / math.sqrt(HEAD_DIM))
                 * jnp.max(jnp.abs(q_norm_dil)) * jnp.max(jnp.abs(k_norm_dil)) + 1.0)
    dil_shift_arr = jnp.full((1, 1), dil_shift, F32)
    ob = lax.cond(2.0 * dil_shift <= DIL_SAFE_SPAN,
                  lambda: _dilated(dil_shift_arr, qb, kb, vb, bounded=True),
                  lambda: _dilated(dil_shift_arr, qb, kb, vb, bounded=False))

    h, hn, eid, gate, rank, cnt = _outproj(
        oa.reshape(T, D_GRP), ob.reshape(T, D_GRP), x.reshape(T, D), out_norm_fox[None, :],
        out_norm_dil[None, :], w_out.astype(BF16), norm_ffn[None, :], w_r, b_r, upper)

    counts = cnt[:, 0]
    padded = ((counts + TILE_E - 1) // TILE_E) * TILE_E
    ends = jnp.cumsum(padded)
    starts = ends - padded
    is_expert = eid[:, :, None] == jnp.arange(N_EXPERTS, dtype=I32)
    pos = jnp.sum(jnp.where(is_expert, starts, 0), axis=-1) + rank
    n_tiles = (2 * T) // TILE_E + N_EXPERTS
    tile_index = jnp.arange(n_tiles, dtype=I32)
    tile_valid = (tile_index * TILE_E < ends[-1]).astype(I32)
    tile_expert = jnp.minimum(
        jnp.sum((ends[None, :] <= (tile_index * TILE_E)[:, None]).astype(I32), axis=1), N_EXPERTS - 1)

    pos_flat = pos.T.reshape(-1)
    xs = _scatter(starts, counts, pos_flat, hn, n_tiles * TILE_E)
    y = _experts(tile_expert, tile_valid, xs, w_gate, w_up, w_down)
    out = _combine(pos_flat, h, gate.T, y)
    return out.reshape(B, S, D)


def kernel(x, norm_mix, w_in, b_forget, q_norm_fox, k_norm_fox, q_norm_dil, k_norm_dil, out_norm_fox,
           out_norm_dil, w_out, norm_ffn, w_router_group, b_router_group, w_router_expert,
           b_router_expert, w_gate, w_up, w_down):
    h = x
    for l in range(norm_mix.shape[0]):
        h = _layer(h, norm_mix[l], w_in[l], b_forget[l], q_norm_fox[l], k_norm_fox[l], q_norm_dil[l],
                   k_norm_dil[l], out_norm_fox[l], out_norm_dil[l], w_out[l], norm_ffn[l],
                   w_router_group[l], b_router_group[l], w_router_expert[l], b_router_expert[l],
                   w_gate[l], w_up[l], w_down[l])
    return h
```

```python
import functools
import math

import jax
import jax.numpy as jnp
from jax import lax
from jax.experimental import pallas as pl
from jax.experimental.pallas import tpu as pltpu

F32 = jnp.float32
BF16 = jnp.bfloat16
I32 = jnp.int32

D_MODEL = 1024
HEAD_DIM = 64
N_HEADS = 8
D_GRP = N_HEADS * HEAD_DIM
LANES = 128
HEADS_PER_TILE = LANES // HEAD_DIM
N_PAIRS = D_GRP // LANES
DIL_PATTERNS = ((128, 1), (512, 4), (2048, 16))
BLOCK = 128
ROPE_THETA = 10000.0
N_GROUPS = 4
EXPERTS_PER_GROUP = 8
N_EXPERTS = N_GROUPS * EXPERTS_PER_GROUP
D_EXPERT = 512
EPS = 1e-6
NEG = -1e30
LOG2E = 1.4426950408889634

TM_IN = 512
TQ = 1024
TK_WIDE = 1024
TK_SUB = 256
FOX_AHEAD = 4
FOX_BOUND_SLACK = 1.02
DIL_SAFE_SPAN = 69.0
FOX_SAFE_SPAN = 100.0
DIL_SPAN = 2048
DIL_GROUP = 8
DIL_PRE = 4
TM_OUT = 512
TM_ROWS = 256
TILE_E = 512
ROUTER_ROWS = 8 + N_EXPERTS
ROW_CHUNKS = D_MODEL // LANES
DMA_UNROLL = 8
VMEM_LIMIT = 56 * 1024 * 1024


def _cparams(sem, flags=None):
    return pltpu.CompilerParams(dimension_semantics=sem, vmem_limit_bytes=VMEM_LIMIT, flags=flags)


def _inproj_kernel(x_ref, gmix_ref, w_ref, wvt_ref, wf_ref, bf_ref, shift_ref, gqa_ref, gka_ref, gqb_ref, gkb_ref,
                   bd_ref, cos_ref, sin_ref, tri_ref,
                   qa_ref, ka_ref, va_ref, qb_ref, kb_ref, vb_ref, carry_ref):
    @pl.when(pl.program_id(1) == 0)
    def _():
        carry_ref[...] = jnp.zeros_like(carry_ref)

    x = x_ref[0]
    ms = jnp.mean(x * x, axis=-1, keepdims=True)
    xn = (x * lax.rsqrt(ms + EPS) * gmix_ref[...]).astype(BF16)

    def seg(j):
        return jnp.dot(xn, w_ref[:, j * D_GRP:(j + 1) * D_GRP], preferred_element_type=F32)

    def head_norm(y, g_ref, scale):
        ss = jnp.dot((y * y).astype(BF16), bd_ref[...], preferred_element_type=F32) * (1.0 / HEAD_DIM)
        return y * lax.rsqrt(ss + EPS) * (g_ref[...] * scale)

    cos = cos_ref[...]
    sin = sin_ref[...]
    lane = lax.broadcasted_iota(I32, (x.shape[0], LANES), 1)
    first_half = (lane % HEAD_DIM) < (HEAD_DIM // 2)

    def rope(y):
        outs = []
        for j in range(N_PAIRS):
            ys = y[:, j * LANES:(j + 1) * LANES]
            partner = jnp.where(first_half, pltpu.roll(ys, LANES - HEAD_DIM // 2, 1),
                                pltpu.roll(ys, HEAD_DIM // 2, 1))
            outs.append(ys * cos + partner * sin)
        return jnp.concatenate(outs, axis=1)

    scale = 1.0 / math.sqrt(HEAD_DIM)
    va_ref[0] = lax.dot_general(wvt_ref[...], xn, (((1,), (1,)), ((), ())),
                                preferred_element_type=F32).astype(va_ref.dtype)
    qb_ref[0] = rope(head_norm(seg(3), gqb_ref, scale)).astype(qb_ref.dtype)
    kb_ref[0] = rope(head_norm(seg(4), gkb_ref, 1.0)).astype(kb_ref.dtype)
    vb_ref[0] = seg(5).astype(vb_ref.dtype)

    fa = jnp.dot(xn, wf_ref[...], preferred_element_type=F32) + bf_ref[...]
    logf = jnp.minimum(fa, 0.0) - jnp.log1p(jnp.exp(-jnp.abs(fa)))
    hi = logf.astype(BF16)
    mid = (logf - hi.astype(F32)).astype(BF16)
    lo = (logf - hi.astype(F32) - mid.astype(F32)).astype(BF16)
    parts = jnp.dot(tri_ref[...], jnp.concatenate([hi, mid, lo], axis=1), preferred_element_type=F32)
    c = parts[:, :LANES] + parts[:, LANES:2 * LANES] + parts[:, 2 * LANES:] + carry_ref[...]
    carry_ref[...] = c[c.shape[0] - 1:, :]

    qa = head_norm(seg(0), gqa_ref, scale * LOG2E)
    ka = head_norm(seg(1), gka_ref, 1.0)
    c2 = c * LOG2E
    ones = (jnp.where((lane >= HEAD_DIM + 3) & (lane < HEAD_DIM + 6), 1.0, 0.0)
            - jnp.where(lane == HEAD_DIM + 6, shift_ref[...], 0.0))
    for h in range(N_HEADS):
        cb = jnp.broadcast_to(c2[:, h:h + 1], (x.shape[0], LANES))
        hi = cb.astype(BF16).astype(F32)
        mid = (cb - hi).astype(BF16).astype(F32)
        lo = cb - hi - mid
        pieces = jnp.where(lane == HEAD_DIM, hi, jnp.where(lane == HEAD_DIM + 1, mid,
                           jnp.where(lane == HEAD_DIM + 2, lo, 0.0)))
        q_extra = pieces + ones
        k_extra = jnp.where(((lane >= HEAD_DIM) & (lane < HEAD_DIM + 3)) | (lane == HEAD_DIM + 6), 1.0, 0.0) \
            - pltpu.roll(pieces, 3, 1)
        j, odd = divmod(h, HEADS_PER_TILE)
        qp = qa[:, j * LANES:(j + 1) * LANES]
        kp = ka[:, j * LANES:(j + 1) * LANES]
        if odd:
            qp = pltpu.roll(qp, HEAD_DIM, 1)
            kp = pltpu.roll(kp, HEAD_DIM, 1)
        qa_ref[0, h] = jnp.where(lane < HEAD_DIM, qp, q_extra).astype(qa_ref.dtype)
        ka_ref[0, h] = jnp.where(lane < HEAD_DIM, kp, k_extra).astype(ka_ref.dtype)


def _inproj(x, gmix, w_main, w_vt, w_f, b_f, shift, gqa, gka, gqb, gkb, bd, cos_t, sin_t, tri):
    B, S, D = x.shape
    tm = TM_IN
    const = lambda shape: pl.BlockSpec(shape, lambda b, i: (0,) * len(shape))
    tok = lambda w, dt: jax.ShapeDtypeStruct((B, S, w), dt)
    tok_spec = lambda w: pl.BlockSpec((1, tm, w), lambda b, i: (b, i, 0))
    head_spec = pl.BlockSpec((1, N_HEADS, tm, LANES), lambda b, i: (b, 0, i, 0))
    head_shape = jax.ShapeDtypeStruct((B, N_HEADS, S, LANES), BF16)
    return pl.pallas_call(
        _inproj_kernel,
        grid=(B, S // tm),
        in_specs=[tok_spec(D), const((1, D)), const(w_main.shape), const(w_vt.shape), const(w_f.shape),
                  const((1, LANES)), const((1, LANES)),
                  const((1, D_GRP)), const((1, D_GRP)), const((1, D_GRP)), const((1, D_GRP)),
                  const((D_GRP, D_GRP)),
                  pl.BlockSpec((tm, LANES), lambda b, i: (i, 0)),
                  pl.BlockSpec((tm, LANES), lambda b, i: (i, 0)),
                  const((tm, tm))],
        out_specs=[head_spec, head_spec, pl.BlockSpec((1, D_GRP, tm), lambda b, i: (b, 0, i))]
        + [tok_spec(D_GRP)] * 3,
        out_shape=[head_shape, head_shape, jax.ShapeDtypeStruct((B, D_GRP, S), BF16),
                   tok(D_GRP, F32), tok(D_GRP, F32), tok(D_GRP, F32)],
        scratch_shapes=[pltpu.VMEM((1, LANES), F32)],
        compiler_params=_cparams(("arbitrary", "arbitrary")),
        name="inproj",
    )(x, gmix, w_main, w_vt, w_f, b_f, shift, gqa, gka, gqb, gkb, bd, cos_t, sin_t, tri)


def _fox_kernel(q_ref, k_ref, v_ref, o_ref, *, online):
    qi = pl.program_id(2)
    tq = q_ref.shape[2]

    def step(start, width, carry, diag):
        carry = list(carry)
        sub = min(TK_SUB, width)
        chunks = [(c, j) for c in range(width // sub) for j in range(HEADS_PER_TILE)]
        def first_query(c):
            return c * sub if diag else 0

        def score(c, j):
            k = k_ref[0, j, pl.ds(start + c * sub, sub), :]
            q = q_ref[0, j, first_query(c):, :]
            return lax.dot_general(k, q, (((1,), (1,)), ((), ())), preferred_element_type=F32)

        def tail(full, lo, new):
            return new if lo == 0 else jnp.concatenate([full[:, :lo], new], axis=1)

        scores = {cj: score(*cj) for cj in chunks[:FOX_AHEAD]}
        for n, (c, j) in enumerate(chunks):
                if n + FOX_AHEAD < len(chunks):
                    nxt = chunks[n + FOX_AHEAD]
                    scores[nxt] = score(*nxt)
                m, l, acc = carry[j]
                lo = first_query(c)
                vt = v_ref[0, :, pl.ds(start + c * sub, sub)]
                s = scores.pop((c, j))
                if diag:
                    key = lax.broadcasted_iota(I32, s.shape, 0)
                    qry = lax.broadcasted_iota(I32, s.shape, 1)
                    s = jnp.where(key <= qry, s, NEG)
                if online:
                    m_new = jnp.maximum(m[:, lo:], jnp.max(s, axis=0, keepdims=True))
                    alpha = jnp.exp2(m[:, lo:] - m_new)
                    p = jnp.exp2(s - m_new)
                    l_new = alpha * l[:, lo:] + jnp.sum(p, axis=0, keepdims=True)
                    acc_new = alpha * acc[:, lo:] + jnp.dot(vt, p.astype(BF16), preferred_element_type=F32)
                    m = tail(m, lo, m_new)
                else:
                    p = jnp.exp2(s)
                    l_new = l[:, lo:] + jnp.sum(p, axis=0, keepdims=True)
                    acc_new = acc[:, lo:] + jnp.dot(vt, p.astype(BF16), preferred_element_type=F32)
                carry[j] = (m, tail(l, lo, l_new), tail(acc, lo, acc_new))
        return tuple(carry)

    init = tuple((jnp.full((1, tq), NEG, F32), jnp.zeros((1, tq), F32), jnp.zeros((LANES, tq), F32))
                 for _ in range(HEADS_PER_TILE))
    per_wide = TK_WIDE // tq
    n_wide = qi // per_wide
    carry = lax.fori_loop(
        0, n_wide, lambda i, c: step(pl.multiple_of(i * TK_WIDE, TK_WIDE), TK_WIDE, c, False), init)
    for extra in range(per_wide - 1):
        carry = lax.cond(qi - n_wide * per_wide > extra,
                         lambda c, e=extra: step(pl.multiple_of((n_wide * per_wide + e) * tq, tq), tq, c, False),
                         lambda c: c, carry)
    carry = step(pl.multiple_of(qi * tq, tq), tq, carry, True)
    outs = [acc / l for (_, l, acc) in carry]
    feat = lax.broadcasted_iota(I32, (LANES, tq), 0)
    o_ref[0] = jnp.where(feat < HEAD_DIM, outs[0], outs[1]).T.astype(o_ref.dtype)


def _fox(qa, ka, va_t, online):
    B, _, S, _ = qa.shape
    return pl.pallas_call(
        functools.partial(_fox_kernel, online=online),
        grid=(B, N_PAIRS, S // TQ),
        in_specs=[pl.BlockSpec((1, HEADS_PER_TILE, TQ, LANES), lambda b, hp, i: (b, hp, i, 0)),
                  pl.BlockSpec((1, HEADS_PER_TILE, S, LANES), lambda b, hp, i: (b, hp, 0, 0)),
                  pl.BlockSpec((1, LANES, S), lambda b, hp, i: (b, hp, 0))],
        out_specs=pl.BlockSpec((1, TQ, LANES), lambda b, hp, i: (b, i, hp)),
        out_shape=jax.ShapeDtypeStruct((B, S, D_GRP), F32),
        compiler_params=_cparams(("arbitrary", "arbitrary", "arbitrary")),
        name="fox_online" if online else "fox",
    )(qa, ka, va_t)


def _dilated_kernel(shift_ref, q_ref, kp_ref, kc_ref, vp_ref, vc_ref, o_ref, qq, kk, vv, qq4, kk4, vv4, osc, lsc,
                    *, bounded):
    u = pl.program_id(1)
    span = q_ref.shape[1]
    qq[...] = q_ref[0]
    kk[0:span, :] = kp_ref[0]
    kk[span:2 * span, :] = kc_ref[0]
    vv[0:span, :] = vp_ref[0]
    vv[span:2 * span, :] = vc_ref[0]
    for src, dst in ((qq, qq4), (kk, kk4), (vv, vv4)):
        part = src.shape[0] // DIL_PRE
        for a in range(DIL_PRE):
            dst[a * part:(a + 1) * part, :] = src[pl.ds(a, part, stride=DIL_PRE), :]

    def rows(buf, buf4, start, n, d):
        if d % DIL_PRE:
            return buf[pl.ds(start, n, stride=d), :]
        part = buf4.shape[0] // DIL_PRE
        a = lax.rem(start, DIL_PRE)
        return buf4[pl.ds(a * part + lax.div(start, DIL_PRE), n, stride=d // DIL_PRE), :]

    lane = lax.broadcasted_iota(I32, (BLOCK, LANES), 1)
    ql = lax.broadcasted_iota(I32, (BLOCK, 2 * BLOCK), 0)
    kl = lax.broadcasted_iota(I32, (BLOCK, 2 * BLOCK), 1)
    dist = ql + BLOCK - kl
    band = (dist >= 0) & (dist <= BLOCK)
    live = -shift_ref[0, 0] if bounded else 0.0
    bias = jnp.where(band, live, NEG)
    bias_first = jnp.where(band & (kl >= BLOCK), live, NEG)

    def scores(q_start, k_start, d, first):
        qs = rows(qq, qq4, q_start, BLOCK, d).astype(BF16)
        ks = rows(kk, kk4, k_start, 2 * BLOCK, d).astype(BF16)
        mask = jnp.where(first, bias_first, bias)
        out = []
        for j in range(HEADS_PER_TILE):
            qj = jnp.where(lane // HEAD_DIM == j, qs, jnp.zeros_like(qs))
            out.append(lax.dot_general(qj, ks, (((1,), (1,)), ((), ())), preferred_element_type=F32) + mask)
        return out

    def finish(s_heads, k_start, d):
        vs = rows(vv, vv4, k_start, 2 * BLOCK, d).astype(BF16)
        o_heads, lse_heads = [], []
        for s in s_heads:
            if bounded:
                p = jnp.exp(s)
                o_heads.append(jnp.dot(p.astype(BF16), vs, preferred_element_type=F32))
                lse_heads.append(jnp.sum(p, axis=-1, keepdims=True))
            else:
                m = jnp.max(s, axis=-1, keepdims=True)
                p = jnp.exp(s - m)
                l = jnp.sum(p, axis=-1, keepdims=True)
                o_heads.append(jnp.dot((p / l).astype(BF16), vs, preferred_element_type=F32))
                lse_heads.append(m + jnp.log(l))
        o = jnp.where(lane < HEAD_DIM, o_heads[0], o_heads[1])
        lse = jnp.where(lane < HEAD_DIM, lse_heads[0], lse_heads[1])
        return o, lse

    for pidx, (window, d) in enumerate(DIL_PATTERNS):
        assert window // d == BLOCK
        unit = d * BLOCK
        n_problems = (span // unit) * d
        assert n_problems % DIL_GROUP == 0

        def body(g, _, pidx=pidx, d=d, unit=unit):
            starts, s_all = [], []
            for t in range(DIL_GROUP):
                idx = g * DIL_GROUP + t
                w = idx // d
                q_start = w * unit + (idx - w * d)
                k_start = span - unit + q_start
                starts.append((q_start, k_start))
                s_all.append(scores(q_start, k_start, d, jnp.logical_and(u == 0, w == 0)))
            for (q_start, k_start), s_heads in zip(starts, s_all):
                o, lse = finish(s_heads, k_start, d)
                osc[pidx, pl.ds(q_start, BLOCK, stride=d), :] = o
                lsc[pidx, pl.ds(q_start, BLOCK, stride=d), :] = lse
            return 0

        lax.fori_loop(0, n_problems // DIL_GROUP, body, 0)

    if bounded:
        num = osc[0] + osc[1] + osc[2]
        den = lsc[0] + lsc[1] + lsc[2]
    else:
        mx = jnp.maximum(jnp.maximum(lsc[0], lsc[1]), lsc[2])
        num = jnp.zeros((span, LANES), F32)
        den = jnp.zeros((span, LANES), F32)
        for pidx in range(len(DIL_PATTERNS)):
            e = jnp.exp(lsc[pidx] - mx)
            num = num + e * osc[pidx]
            den = den + e
    o_ref[0] = (num / den).astype(o_ref.dtype)


def _dilated(shift, qb, kb, vb, bounded):
    B, S, _ = qb.shape
    span = DIL_SPAN
    cur = pl.BlockSpec((1, span, LANES), lambda b, u, hp: (b, u, hp))
    prev = pl.BlockSpec((1, span, LANES), lambda b, u, hp: (b, jnp.maximum(u - 1, 0), hp))
    return pl.pallas_call(
        functools.partial(_dilated_kernel, bounded=bounded),
        grid=(B, S // span, N_PAIRS),
        in_specs=[pl.BlockSpec(memory_space=pltpu.SMEM), cur, prev, cur, prev, cur],
        out_specs=cur,
        out_shape=jax.ShapeDtypeStruct((B, S, D_GRP), F32),
        scratch_shapes=[pltpu.VMEM((span, LANES), F32),
                        pltpu.VMEM((2 * span, LANES), F32), pltpu.VMEM((2 * span, LANES), F32),
                        pltpu.VMEM((span, LANES), F32),
                        pltpu.VMEM((2 * span, LANES), F32), pltpu.VMEM((2 * span, LANES), F32),
                        pltpu.VMEM((len(DIL_PATTERNS), span, LANES), F32),
                        pltpu.VMEM((len(DIL_PATTERNS), span, LANES), F32)],
        compiler_params=_cparams(("arbitrary", "arbitrary", "arbitrary")),
        name="dilated" if bounded else "dilated_exact",
    )(shift, qb, kb, kb, vb, vb)


def _store_row_tiles(ref, x, first_row=0):
    n = x.shape[0]
    for c in range(ROW_CHUNKS):
        ref[pl.ds(first_row * ROW_CHUNKS + c, n, stride=ROW_CHUNKS), :] = x[:, c * LANES:(c + 1) * LANES]


def _load_row_tiles(ref, n, first_row=0):
    return jnp.concatenate([ref[pl.ds(first_row * ROW_CHUNKS + c, n, stride=ROW_CHUNKS), :]
                            for c in range(ROW_CHUNKS)], axis=1)


def _row_tile_copy(src_ref, src_row, dst_ref, dst_row, sem):
    src = src_ref.at[pl.ds(pl.multiple_of(src_row * ROW_CHUNKS, ROW_CHUNKS), ROW_CHUNKS), :]
    dst = dst_ref.at[pl.ds(pl.multiple_of(dst_row * ROW_CHUNKS, ROW_CHUNKS), ROW_CHUNKS), :]
    return pltpu.make_async_copy(src, dst, sem)


def _outproj_kernel(oa_ref, ob_ref, x_ref, gfox_ref, gdil_ref, wo_ref, gffn_ref, wr_ref, br_ref, upper_ref,
                    h_ref, hn_ref, eid_ref, gate_ref, rank_ref, cnt_ref, run_ref):
    @pl.when(pl.program_id(0) == 0)
    def _():
        run_ref[...] = jnp.zeros_like(run_ref)

    def norm(y, g):
        ms = jnp.mean(y * y, axis=-1, keepdims=True)
        return y * lax.rsqrt(ms + EPS) * g

    a = norm(oa_ref[...], gfox_ref[...]).astype(BF16)
    b = norm(ob_ref[...], gdil_ref[...]).astype(BF16)
    mix = (jnp.dot(a, wo_ref[0:D_GRP, :], preferred_element_type=F32)
           + jnp.dot(b, wo_ref[D_GRP:2 * D_GRP, :], preferred_element_type=F32))
    h = x_ref[...] + mix
    h_ref[...] = h
    hn = norm(h, gffn_ref[...])
    _store_row_tiles(hn_ref, hn)

    z = lax.dot_general(wr_ref[...], hn, (((1,), (1,)), ((), ())), preferred_element_type=F32,
                        precision=lax.Precision.HIGHEST) + br_ref[...]
    tm = z.shape[1]
    best = z[0:1, :]
    g_sel = jnp.zeros((1, tm), I32)
    for g in range(1, N_GROUPS):
        better = z[g:g + 1, :] > best
        g_sel = jnp.where(better, g, g_sel)
        best = jnp.maximum(best, z[g:g + 1, :])
    den = jnp.zeros((1, tm), F32)
    for g in range(N_GROUPS):
        den = den + jnp.exp(z[g:g + 1, :] - best)
    pg_top = 1.0 / den

    ze = jnp.zeros((EXPERTS_PER_GROUP, tm), F32)
    for g in range(N_GROUPS):
        ze = jnp.where(g_sel == g, z[8 + g * EXPERTS_PER_GROUP:8 + (g + 1) * EXPERTS_PER_GROUP, :], ze)
    e_iota = lax.broadcasted_iota(I32, ze.shape, 0)
    v1 = jnp.max(ze, axis=0, keepdims=True)
    i1 = jnp.min(jnp.where(ze == v1, e_iota, EXPERTS_PER_GROUP), axis=0, keepdims=True)
    ze2 = jnp.where(e_iota == i1, -jnp.inf, ze)
    v2 = jnp.max(ze2, axis=0, keepdims=True)
    i2 = jnp.min(jnp.where(ze2 == v2, e_iota, EXPERTS_PER_GROUP), axis=0, keepdims=True)
    e2 = jnp.exp(v2 - v1)
    inv = 1.0 / (1.0 + e2)
    gate1 = inv * pg_top
    gate2 = e2 * inv * pg_top
    eid1 = g_sel * EXPERTS_PER_GROUP + i1
    eid2 = g_sel * EXPERTS_PER_GROUP + i2

    x_iota = lax.broadcasted_iota(I32, (N_EXPERTS, tm), 0)
    hot1 = x_iota == eid1
    hot2 = x_iota == eid2
    multi = jnp.logical_or(hot1, hot2)
    before = jnp.dot(multi.astype(BF16), upper_ref[...], preferred_element_type=F32)
    slot = before + run_ref[:, 0:1]
    rank1 = jnp.sum(jnp.where(hot1, slot, 0.0), axis=0, keepdims=True)
    rank2 = jnp.sum(jnp.where(hot2, slot, 0.0), axis=0, keepdims=True)
    run_ref[...] = run_ref[...] + jnp.sum(multi.astype(F32), axis=1, keepdims=True)

    eid_ref[...] = jnp.concatenate([eid1, eid2], axis=0)
    gate_ref[...] = jnp.concatenate([gate1, gate2], axis=0)
    rank_ref[...] = jnp.concatenate([rank1, rank2], axis=0).astype(I32)
    cnt_ref[...] = run_ref[...].astype(I32)


def _outproj(oa, ob, x2, gfox, gdil, w_out, gffn, w_r, b_r, upper):
    T, D = x2.shape
    tm = TM_OUT
    const = lambda shape: pl.BlockSpec(shape, lambda i: (0,) * len(shape))
    tok = lambda w: pl.BlockSpec((tm, w), lambda i: (i, 0))
    lanes2 = pl.BlockSpec((2, tm), lambda i: (0, i))
    return pl.pallas_call(
        _outproj_kernel,
        grid=(T // tm,),
        in_specs=[tok(D_GRP), tok(D_GRP), tok(D), const((1, D_GRP)), const((1, D_GRP)), const((D, D)),
                  const((1, D)), const((ROUTER_ROWS, D)), const((ROUTER_ROWS, 1)), const((tm, tm))],
        out_specs=[tok(D), pl.BlockSpec((tm * ROW_CHUNKS, LANES), lambda i: (i, 0)),
                   lanes2, lanes2, lanes2, const((N_EXPERTS, LANES))],
        out_shape=[jax.ShapeDtypeStruct((T, D), F32), jax.ShapeDtypeStruct((T * ROW_CHUNKS, LANES), F32),
                   jax.ShapeDtypeStruct((2, T), I32), jax.ShapeDtypeStruct((2, T), F32),
                   jax.ShapeDtypeStruct((2, T), I32), jax.ShapeDtypeStruct((N_EXPERTS, LANES), I32)],
        scratch_shapes=[pltpu.VMEM((N_EXPERTS, LANES), F32)],
        compiler_params=_cparams(("arbitrary",)),
        name="outproj",
    )(oa, ob, x2, gfox, gdil, w_out, gffn, w_r, b_r, upper)


def _scatter_kernel(starts_ref, cnt_ref, pos_ref, hn_ref, xs_ref, ring, sems, zero_sem, *, n_steps):
    i = pl.program_id(0)
    tm = hn_ref.shape[0] // ROW_CHUNKS
    slot = lax.rem(i, 2)

    def wait_slot(s):
        for _ in range(2):
            pltpu.make_async_copy(ring.at[s], xs_ref.at[pl.ds(0, tm * ROW_CHUNKS), :], sems.at[s]).wait()

    @pl.when(i >= 2)
    def _():
        wait_slot(slot)

    ring[slot] = hn_ref[...]

    def start(r, _):
        for k in range(2):
            _row_tile_copy(ring.at[slot], r, xs_ref, pos_ref[2 * r + k], sems.at[slot]).start(priority=k)
        return 0

    lax.fori_loop(0, tm, start, 0, unroll=DMA_UNROLL)

    @pl.when(i == n_steps - 1)
    def _():
        wait_slot(slot)
        if n_steps > 1:
            wait_slot(1 - slot)
        ring[0] = jnp.zeros((tm * ROW_CHUNKS, LANES), F32)

        def pad_expert(e, _, wait):
            n_pad = lax.rem(TILE_E - lax.rem(cnt_ref[e], TILE_E), TILE_E)
            first = starts_ref[e] + cnt_ref[e]
            size = TILE_E // 2
            while size >= 1:
                row0 = first + (n_pad & ~(2 * size - 1))

                @pl.when((n_pad & size) != 0)
                def _(size=size, row0=row0):
                    copy = pltpu.make_async_copy(
                        ring.at[0, pl.ds(0, size * ROW_CHUNKS), :],
                        xs_ref.at[pl.ds(pl.multiple_of(row0 * ROW_CHUNKS, ROW_CHUNKS), size * ROW_CHUNKS), :],
                        zero_sem)
                    copy.wait() if wait else copy.start()

                size //= 2
            return 0

        lax.fori_loop(0, N_EXPERTS, functools.partial(pad_expert, wait=False), 0)
        last = N_EXPERTS - 1
        used_rows = starts_ref[last] + cnt_ref[last] + lax.rem(TILE_E - lax.rem(cnt_ref[last], TILE_E), TILE_E)
        n_tail = xs_ref.shape[0] // (tm * ROW_CHUNKS) - used_rows // tm

        def tail_copy(t):
            row0 = pl.multiple_of((used_rows + t * tm) * ROW_CHUNKS, tm * ROW_CHUNKS)
            return pltpu.make_async_copy(ring.at[0], xs_ref.at[pl.ds(row0, tm * ROW_CHUNKS), :], zero_sem)

        lax.fori_loop(0, n_tail, lambda t, c: (tail_copy(t).start(), c)[1], 0)
        lax.fori_loop(0, N_EXPERTS, functools.partial(pad_expert, wait=True), 0)
        lax.fori_loop(0, n_tail, lambda t, c: (tail_copy(t).wait(), c)[1], 0)


def _scatter(starts, cnt, pos, hn, n_rows):
    T = hn.shape[0] // ROW_CHUNKS
    tm = TM_ROWS
    assert TILE_E % tm == 0 and TILE_E // 2 <= tm
    n_steps = T // tm
    grid_spec = pltpu.PrefetchScalarGridSpec(
        num_scalar_prefetch=2,
        grid=(n_steps,),
        in_specs=[pl.BlockSpec((2 * tm,), lambda i, starts, cnt: (i,), memory_space=pltpu.SMEM),
                  pl.BlockSpec((tm * ROW_CHUNKS, LANES), lambda i, starts, cnt: (i, 0))],
        out_specs=pl.BlockSpec(memory_space=pl.ANY),
        scratch_shapes=[pltpu.VMEM((2, tm * ROW_CHUNKS, LANES), F32),
                        pltpu.SemaphoreType.DMA((2,)), pltpu.SemaphoreType.DMA(())],
    )
    return pl.pallas_call(
        functools.partial(_scatter_kernel, n_steps=n_steps),
        grid_spec=grid_spec,
        out_shape=jax.ShapeDtypeStruct((n_rows * ROW_CHUNKS, LANES), F32),
        compiler_params=_cparams(("arbitrary",)),
        name="scatter_rows",
    )(starts, cnt, pos, hn)


def _experts_kernel(te_ref, tb_ref, tr_ref, xs_ref, wg_ref, wu_ref, wd_ref, y_ref, wg_bf, wu_bf, wd_bf):
    del tb_ref
    i = pl.program_id(0)
    rows = tr_ref[i]
    new_expert = jnp.logical_or(i == 0, te_ref[i] != te_ref[jnp.maximum(i - 1, 0)])

    @pl.when(jnp.logical_and(rows > 0, new_expert))
    def _():
        wg_bf[...] = wg_ref[0].astype(BF16)
        wu_bf[...] = wu_ref[0].astype(BF16)
        wd_bf[...] = wd_ref[0].astype(BF16)

    half = TILE_E // 2
    for part in range(2):
        @pl.when(rows > part * half)
        def _(part=part):
            x = _load_row_tiles(xs_ref, half, part * half).astype(BF16)
            g = jnp.dot(x, wg_bf[...], preferred_element_type=F32)
            up = jnp.dot(x, wu_bf[...], preferred_element_type=F32)
            hmid = (g * jax.nn.sigmoid(g) * up).astype(BF16)
            _store_row_tiles(y_ref, jnp.dot(hmid, wd_bf[...], preferred_element_type=F32), part * half)

    @pl.when(jnp.logical_and(rows > 0, rows <= half))
    def _():
        y_ref[half * ROW_CHUNKS:, :] = jnp.zeros((half * ROW_CHUNKS, LANES), F32)


def _experts(tile_expert, tile_block, tile_rows, xs, w_gate, w_up, w_down):
    n_tiles = tile_expert.shape[0]
    D = D_MODEL
    rows_spec = pl.BlockSpec((TILE_E * ROW_CHUNKS, LANES), lambda i, te, tb, tr: (tb[i], 0))
    weights = lambda shape: pl.BlockSpec(shape, lambda i, te, tb, tr: (te[i], 0, 0))
    grid_spec = pltpu.PrefetchScalarGridSpec(
        num_scalar_prefetch=3,
        grid=(n_tiles,),
        in_specs=[rows_spec, weights((1, D, D_EXPERT)), weights((1, D, D_EXPERT)), weights((1, D_EXPERT, D))],
        out_specs=rows_spec,
        scratch_shapes=[pltpu.VMEM((D, D_EXPERT), BF16), pltpu.VMEM((D, D_EXPERT), BF16),
                        pltpu.VMEM((D_EXPERT, D), BF16)],
    )
    return pl.pallas_call(
        _experts_kernel,
        grid_spec=grid_spec,
        out_shape=jax.ShapeDtypeStruct(xs.shape, F32),
        input_output_aliases={3: 0},
        compiler_params=_cparams(("arbitrary",)),
        name="experts",
    )(tile_expert, tile_block, tile_rows, xs, w_gate, w_up, w_down)


def _combine_kernel(pos_ref, pos_next_ref, h_ref, gate_ref, y_ref, o_ref, ybuf, sems, *, n_steps):
    i = pl.program_id(0)
    tm = h_ref.shape[0]
    slot = lax.rem(i, 2)

    def gather(p_ref, s):
        def start(r, _):
            for k in range(2):
                _row_tile_copy(y_ref, p_ref[2 * r + k], ybuf.at[s, k], r, sems.at[s]).start(priority=k)
            return 0

        lax.fori_loop(0, tm, start, 0, unroll=DMA_UNROLL)

    @pl.when(i == 0)
    def _():
        gather(pos_ref, slot)

    @pl.when(i + 1 < n_steps)
    def _():
        gather(pos_next_ref, 1 - slot)

    for k in range(2):
        pltpu.make_async_copy(y_ref.at[pl.ds(0, tm * ROW_CHUNKS), :], ybuf.at[slot, k], sems.at[slot]).wait()
    g = gate_ref[...]
    o_ref[...] = (h_ref[...] + g[:, 0:1] * _load_row_tiles(ybuf.at[slot, 0], tm)
                  + g[:, 1:2] * _load_row_tiles(ybuf.at[slot, 1], tm))


def _combine(pos, h, gate_t, y):
    T, D = h.shape
    tm = TM_ROWS
    n_steps = T // tm
    return pl.pallas_call(
        functools.partial(_combine_kernel, n_steps=n_steps),
        grid=(n_steps,),
        in_specs=[pl.BlockSpec((2 * tm,), lambda i: (i,), memory_space=pltpu.SMEM),
                  pl.BlockSpec((2 * tm,), lambda i: (jnp.minimum(i + 1, n_steps - 1),), memory_space=pltpu.SMEM),
                  pl.BlockSpec((tm, D), lambda i: (i, 0)),
                  pl.BlockSpec((tm, 2), lambda i: (i, 0)),
                  pl.BlockSpec(memory_space=pl.ANY)],
        out_specs=pl.BlockSpec((tm, D), lambda i: (i, 0)),
        out_shape=jax.ShapeDtypeStruct((T, D), F32),
        scratch_shapes=[pltpu.VMEM((2, 2, tm * ROW_CHUNKS, LANES), F32), pltpu.SemaphoreType.DMA((2,))],
        compiler_params=_cparams(("arbitrary",)),
        name="combine",
    )(pos, pos, h, gate_t, y)


def _rope_tables(S):
    inv_freq = 1.0 / (ROPE_THETA ** (jnp.arange(0, HEAD_DIM, 2, dtype=F32) / HEAD_DIM))
    ang = jnp.arange(S, dtype=F32)[:, None] * inv_freq[None, :]
    cos, sin = jnp.cos(ang), jnp.sin(ang)
    cos_t = jnp.tile(cos, (1, 2 * HEADS_PER_TILE))
    sin_t = jnp.tile(jnp.concatenate([-sin, sin], axis=1), (1, HEADS_PER_TILE))
    return cos_t, sin_t


def _layer(x, norm_mix, w_in, b_forget, q_norm_fox, k_norm_fox, q_norm_dil, k_norm_dil,
           out_norm_fox, out_norm_dil, w_out, norm_ffn, w_router_group, b_router_group,
           w_router_expert, b_router_expert, w_gate, w_up, w_down):
    B, S, D = x.shape
    T = B * S
    n_main = 6 * D_GRP

    w_main = w_in[:, :n_main].astype(BF16)
    w_vt = w_in[:, 2 * D_GRP:3 * D_GRP].T.astype(BF16)
    w_f = jnp.pad(w_in[:, n_main:], ((0, 0), (0, LANES - N_HEADS))).astype(BF16)
    b_f = jnp.pad(b_forget, (0, LANES - N_HEADS))[None, :]
    per_head = lambda g: jnp.tile(g, N_HEADS)[None, :]
    bd = jnp.kron(jnp.eye(N_HEADS, dtype=F32), jnp.ones((HEAD_DIM, HEAD_DIM), F32)).astype(BF16)
    cos_t, sin_t = _rope_tables(S)
    tri = jnp.tril(jnp.ones((TM_IN, TM_IN), F32)).astype(BF16)
    upper = jnp.triu(jnp.ones((TM_OUT, TM_OUT), F32), k=1).astype(BF16)
    w_r = jnp.concatenate([
        jnp.pad(w_router_group.T, ((0, 8 - N_GROUPS), (0, 0))),
        w_router_expert.transpose(0, 2, 1).reshape(N_EXPERTS, D)], axis=0)
    b_r = jnp.concatenate([jnp.pad(b_router_group, (0, 8 - N_GROUPS)), b_router_expert.reshape(-1)])[:, None]

    bound = (HEAD_DIM / math.sqrt(HEAD_DIM)) * LOG2E * jnp.max(jnp.abs(q_norm_fox)) * jnp.max(jnp.abs(k_norm_fox))
    shift = FOX_BOUND_SLACK * bound + 1.0
    qa, ka, va, qb, kb, vb = _inproj(
        x, norm_mix[None, :], w_main, w_vt, w_f, b_f, jnp.full((1, LANES), shift, F32),
        per_head(q_norm_fox), per_head(k_norm_fox),
        per_head(q_norm_dil), per_head(k_norm_dil), bd, cos_t, sin_t, tri)
    oa = lax.cond(2.0 * shift <= FOX_SAFE_SPAN,
                  lambda: _fox(qa, ka, va, online=False), lambda: _fox(qa, ka, va, online=True))
    dil_shift = (FOX_BOUND_SLACK * (HEAD_DIM / math.sqrt(HEAD_DIM))
                 * jnp.max(jnp.abs(q_norm_dil)) * jnp.max(jnp.abs(k_norm_dil)) + 1.0)
    dil_shift_arr = jnp.full((1, 1), dil_shift, F32)
    ob = lax.cond(2.0 * dil_shift <= DIL_SAFE_SPAN,
                  lambda: _dilated(dil_shift_arr, qb, kb, vb, bounded=True),
                  lambda: _dilated(dil_shift_arr, qb, kb, vb, bounded=False))

    h, hn, eid, gate, rank, cnt = _outproj(
        oa.reshape(T, D_GRP), ob.reshape(T, D_GRP), x.reshape(T, D), out_norm_fox[None, :],
        out_norm_dil[None, :], w_out.astype(BF16), norm_ffn[None, :], w_r, b_r, upper)

    counts = cnt[:, 0]
    padded = ((counts + TILE_E - 1) // TILE_E) * TILE_E
    ends = jnp.cumsum(padded)
    starts = ends - padded
    is_expert = eid[:, :, None] == jnp.arange(N_EXPERTS, dtype=I32)
    pos = jnp.sum(jnp.where(is_expert, starts, 0), axis=-1) + rank
    n_tiles = (2 * T) // TILE_E + N_EXPERTS
    tile_index = jnp.arange(n_tiles, dtype=I32)
    tile_valid = tile_index * TILE_E < ends[-1]
    tile_block = jnp.minimum(tile_index, ends[-1] // TILE_E - 1)
    in_region = ends[None, :] <= (tile_block * TILE_E)[:, None]
    tile_expert = jnp.sum(in_region.astype(I32), axis=1)
    expert_of_tile = tile_expert[:, None] == jnp.arange(N_EXPERTS, dtype=I32)
    rows_before = tile_block * TILE_E - jnp.sum(jnp.where(expert_of_tile, starts, 0), axis=1)
    tile_rows = jnp.clip(jnp.sum(jnp.where(expert_of_tile, counts, 0), axis=1) - rows_before, 0, TILE_E)
    tile_rows = jnp.where(tile_valid, tile_rows, 0).astype(I32)

    pos_flat = pos.T.reshape(-1)
    xs = _scatter(starts, counts, pos_flat, hn, n_tiles * TILE_E)
    y = _experts(tile_expert, tile_block, tile_rows, xs, w_gate, w_up, w_down)
    out = _combine(pos_flat, h, gate.T, y)
    return out.reshape(B, S, D)


def kernel(x, norm_mix, w_in, b_forget, q_norm_fox, k_norm_fox, q_norm_dil, k_norm_dil, out_norm_fox,
           out_norm_dil, w_out, norm_ffn, w_router_group, b_router_group, w_router_expert,
           b_router_expert, w_gate, w_up, w_down):
    h = x
    for l in range(norm_mix.shape[0]):
        h = _layer(h, norm_mix[l], w_in[l], b_forget[l], q_norm_fox[l], k_norm_fox[l], q_norm_dil[l],
                   k_norm_dil[l], out_norm_fox[l], out_norm_dil[l], w_out[l], norm_ffn[l],
                   w_router_group[l], b_router_group[l], w_router_expert[l], b_router_expert[l],
                   w_gate[l], w_up[l], w_down[l])
    return h
```

```python
import functools
import math

import jax
import jax.numpy as jnp
from jax import lax
from jax.experimental import pallas as pl
from jax.experimental.pallas import tpu as pltpu

F32 = jnp.float32
BF16 = jnp.bfloat16
I32 = jnp.int32

D_MODEL = 1024
HEAD_DIM = 64
N_HEADS = 8
D_GRP = N_HEADS * HEAD_DIM
LANES = 128
HEADS_PER_TILE = LANES // HEAD_DIM
N_PAIRS = D_GRP // LANES
DIL_PATTERNS = ((128, 1), (512, 4), (2048, 16))
BLOCK = 128
ROPE_THETA = 10000.0
N_GROUPS = 4
EXPERTS_PER_GROUP = 8
N_EXPERTS = N_GROUPS * EXPERTS_PER_GROUP
D_EXPERT = 512
EPS = 1e-6
NEG = -1e30
LOG2E = 1.4426950408889634

TM_IN = 1024
TQ = 1024
TK_WIDE = 1024
TK_SUB = 256
FOX_AHEAD = 4
FOX_BOUND_SLACK = 1.02
DIL_SAFE_SPAN = 69.0
FOX_SAFE_SPAN = 100.0
DIL_SPAN = 2048
DIL_GROUP = 8
DIL_PRE = 4
TM_OUT = 512
TM_ROWS = 512
TILE_E = 512
ROUTER_ROWS = 8 + N_EXPERTS
ROW_CHUNKS = D_MODEL // LANES
DMA_UNROLL = 8
VMEM_LIMIT = 56 * 1024 * 1024


def _cparams(sem, flags=None):
    return pltpu.CompilerParams(dimension_semantics=sem, vmem_limit_bytes=VMEM_LIMIT, flags=flags)


def _inproj_kernel(x_ref, gmix_ref, w_ref, wvt_ref, wf_ref, bf_ref, shift_ref, gqa_ref, gka_ref, gqb_ref, gkb_ref,
                   bd_ref, cos_ref, sin_ref, tri_ref,
                   qa_ref, ka_ref, va_ref, qb_ref, kb_ref, vb_ref, carry_ref):
    @pl.when(pl.program_id(1) == 0)
    def _():
        carry_ref[...] = jnp.zeros_like(carry_ref)

    x = x_ref[0]
    ms = jnp.mean(x * x, axis=-1, keepdims=True)
    xn = (x * lax.rsqrt(ms + EPS) * gmix_ref[...]).astype(BF16)

    def seg(j):
        return jnp.dot(xn, w_ref[:, j * D_GRP:(j + 1) * D_GRP], preferred_element_type=F32)

    def head_norm(y, g_ref, scale):
        ss = jnp.dot((y * y).astype(BF16), bd_ref[...], preferred_element_type=F32) * (1.0 / HEAD_DIM)
        return y * lax.rsqrt(ss + EPS) * (g_ref[...] * scale)

    cos = cos_ref[...]
    sin = sin_ref[...]
    lane = lax.broadcasted_iota(I32, (x.shape[0], LANES), 1)
    first_half = (lane % HEAD_DIM) < (HEAD_DIM // 2)

    def rope(y):
        outs = []
        for j in range(N_PAIRS):
            ys = y[:, j * LANES:(j + 1) * LANES]
            partner = jnp.where(first_half, pltpu.roll(ys, LANES - HEAD_DIM // 2, 1),
                                pltpu.roll(ys, HEAD_DIM // 2, 1))
            outs.append(ys * cos + partner * sin)
        return jnp.concatenate(outs, axis=1)

    scale = 1.0 / math.sqrt(HEAD_DIM)
    va_ref[0] = lax.dot_general(wvt_ref[...], xn, (((1,), (1,)), ((), ())),
                                preferred_element_type=F32).astype(va_ref.dtype)
    qb_ref[0] = rope(head_norm(seg(3), gqb_ref, scale)).astype(qb_ref.dtype)
    kb_ref[0] = rope(head_norm(seg(4), gkb_ref, 1.0)).astype(kb_ref.dtype)
    vb_ref[0] = seg(5).astype(vb_ref.dtype)

    fa = jnp.dot(xn, wf_ref[...], preferred_element_type=F32) + bf_ref[...]
    logf = jnp.minimum(fa, 0.0) - jnp.log1p(jnp.exp(-jnp.abs(fa)))
    hi = logf.astype(BF16)
    mid = (logf - hi.astype(F32)).astype(BF16)
    lo = (logf - hi.astype(F32) - mid.astype(F32)).astype(BF16)
    parts = jnp.dot(tri_ref[...], jnp.concatenate([hi, mid, lo], axis=1), preferred_element_type=F32)
    c = parts[:, :LANES] + parts[:, LANES:2 * LANES] + parts[:, 2 * LANES:] + carry_ref[...]
    carry_ref[...] = c[c.shape[0] - 1:, :]

    qa = head_norm(seg(0), gqa_ref, scale * LOG2E)
    ka = head_norm(seg(1), gka_ref, 1.0)
    c2 = c * LOG2E
    ones = (jnp.where((lane >= HEAD_DIM + 3) & (lane < HEAD_DIM + 6), 1.0, 0.0)
            - jnp.where(lane == HEAD_DIM + 6, shift_ref[...], 0.0))
    for h in range(N_HEADS):
        cb = jnp.broadcast_to(c2[:, h:h + 1], (x.shape[0], LANES))
        hi = cb.astype(BF16).astype(F32)
        mid = (cb - hi).astype(BF16).astype(F32)
        lo = cb - hi - mid
        pieces = jnp.where(lane == HEAD_DIM, hi, jnp.where(lane == HEAD_DIM + 1, mid,
                           jnp.where(lane == HEAD_DIM + 2, lo, 0.0)))
        q_extra = pieces + ones
        k_extra = jnp.where(((lane >= HEAD_DIM) & (lane < HEAD_DIM + 3)) | (lane == HEAD_DIM + 6), 1.0, 0.0) \
            - pltpu.roll(pieces, 3, 1)
        j, odd = divmod(h, HEADS_PER_TILE)
        qp = qa[:, j * LANES:(j + 1) * LANES]
        kp = ka[:, j * LANES:(j + 1) * LANES]
        if odd:
            qp = pltpu.roll(qp, HEAD_DIM, 1)
            kp = pltpu.roll(kp, HEAD_DIM, 1)
        qa_ref[0, h] = jnp.where(lane < HEAD_DIM, qp, q_extra).astype(qa_ref.dtype)
        ka_ref[0, h] = jnp.where(lane < HEAD_DIM, kp, k_extra).astype(ka_ref.dtype)


def _inproj(x, gmix, w_main, w_vt, w_f, b_f, shift, gqa, gka, gqb, gkb, bd, cos_t, sin_t, tri):
    B, S, D = x.shape
    tm = TM_IN
    const = lambda shape: pl.BlockSpec(shape, lambda b, i: (0,) * len(shape))
    tok = lambda w, dt: jax.ShapeDtypeStruct((B, S, w), dt)
    tok_spec = lambda w: pl.BlockSpec((1, tm, w), lambda b, i: (b, i, 0))
    head_spec = pl.BlockSpec((1, N_HEADS, tm, LANES), lambda b, i: (b, 0, i, 0))
    head_shape = jax.ShapeDtypeStruct((B, N_HEADS, S, LANES), BF16)
    return pl.pallas_call(
        _inproj_kernel,
        grid=(B, S // tm),
        in_specs=[tok_spec(D), const((1, D)), const(w_main.shape), const(w_vt.shape), const(w_f.shape),
                  const((1, LANES)), const((1, LANES)),
                  const((1, D_GRP)), const((1, D_GRP)), const((1, D_GRP)), const((1, D_GRP)),
                  const((D_GRP, D_GRP)),
                  pl.BlockSpec((tm, LANES), lambda b, i: (i, 0)),
                  pl.BlockSpec((tm, LANES), lambda b, i: (i, 0)),
                  const((tm, tm))],
        out_specs=[head_spec, head_spec, pl.BlockSpec((1, D_GRP, tm), lambda b, i: (b, 0, i))]
        + [tok_spec(D_GRP)] * 3,
        out_shape=[head_shape, head_shape, jax.ShapeDtypeStruct((B, D_GRP, S), BF16),
                   tok(D_GRP, F32), tok(D_GRP, F32), tok(D_GRP, F32)],
        scratch_shapes=[pltpu.VMEM((1, LANES), F32)],
        compiler_params=_cparams(("arbitrary", "arbitrary")),
        name="inproj",
    )(x, gmix, w_main, w_vt, w_f, b_f, shift, gqa, gka, gqb, gkb, bd, cos_t, sin_t, tri)


def _fox_kernel(q_ref, k_ref, v_ref, o_ref, *, online):
    qi = pl.program_id(2)
    tq = q_ref.shape[2]

    def step(start, width, carry, diag):
        carry = list(carry)
        sub = min(TK_SUB, width)
        chunks = [(c, j) for c in range(width // sub) for j in range(HEADS_PER_TILE)]
        def first_query(c):
            return c * sub if diag else 0

        def score(c, j):
            k = k_ref[0, j, pl.ds(start + c * sub, sub), :]
            q = q_ref[0, j, first_query(c):, :]
            return lax.dot_general(k, q, (((1,), (1,)), ((), ())), preferred_element_type=F32)

        def tail(full, lo, new):
            return new if lo == 0 else jnp.concatenate([full[:, :lo], new], axis=1)

        scores = {cj: score(*cj) for cj in chunks[:FOX_AHEAD]}
        for n, (c, j) in enumerate(chunks):
                if n + FOX_AHEAD < len(chunks):
                    nxt = chunks[n + FOX_AHEAD]
                    scores[nxt] = score(*nxt)
                m, l, acc = carry[j]
                lo = first_query(c)
                vt = v_ref[0, :, pl.ds(start + c * sub, sub)]
                s = scores.pop((c, j))
                if diag:
                    key = lax.broadcasted_iota(I32, s.shape, 0)
                    qry = lax.broadcasted_iota(I32, s.shape, 1)
                    s = jnp.where(key <= qry, s, NEG)
                if online:
                    m_new = jnp.maximum(m[:, lo:], jnp.max(s, axis=0, keepdims=True))
                    alpha = jnp.exp2(m[:, lo:] - m_new)
                    p = jnp.exp2(s - m_new)
                    l_new = alpha * l[:, lo:] + jnp.sum(p, axis=0, keepdims=True)
                    acc_new = alpha * acc[:, lo:] + jnp.dot(vt, p.astype(BF16), preferred_element_type=F32)
                    m = tail(m, lo, m_new)
                else:
                    p = jnp.exp2(s)
                    l_new = l[:, lo:] + jnp.sum(p, axis=0, keepdims=True)
                    acc_new = acc[:, lo:] + jnp.dot(vt, p.astype(BF16), preferred_element_type=F32)
                carry[j] = (m, tail(l, lo, l_new), tail(acc, lo, acc_new))
        return tuple(carry)

    init = tuple((jnp.full((1, tq), NEG, F32), jnp.zeros((1, tq), F32), jnp.zeros((LANES, tq), F32))
                 for _ in range(HEADS_PER_TILE))
    per_wide = TK_WIDE // tq
    n_wide = qi // per_wide
    carry = lax.fori_loop(
        0, n_wide, lambda i, c: step(pl.multiple_of(i * TK_WIDE, TK_WIDE), TK_WIDE, c, False), init)
    for extra in range(per_wide - 1):
        carry = lax.cond(qi - n_wide * per_wide > extra,
                         lambda c, e=extra: step(pl.multiple_of((n_wide * per_wide + e) * tq, tq), tq, c, False),
                         lambda c: c, carry)
    carry = step(pl.multiple_of(qi * tq, tq), tq, carry, True)
    outs = [acc / l for (_, l, acc) in carry]
    feat = lax.broadcasted_iota(I32, (LANES, tq), 0)
    o_ref[0] = jnp.where(feat < HEAD_DIM, outs[0], outs[1]).T.astype(o_ref.dtype)


def _fox(qa, ka, va_t, online):
    B, _, S, _ = qa.shape
    return pl.pallas_call(
        functools.partial(_fox_kernel, online=online),
        grid=(B, N_PAIRS, S // TQ),
        in_specs=[pl.BlockSpec((1, HEADS_PER_TILE, TQ, LANES), lambda b, hp, i: (b, hp, i, 0)),
                  pl.BlockSpec((1, HEADS_PER_TILE, S, LANES), lambda b, hp, i: (b, hp, 0, 0)),
                  pl.BlockSpec((1, LANES, S), lambda b, hp, i: (b, hp, 0))],
        out_specs=pl.BlockSpec((1, TQ, LANES), lambda b, hp, i: (b, i, hp)),
        out_shape=jax.ShapeDtypeStruct((B, S, D_GRP), F32),
        compiler_params=_cparams(("arbitrary", "arbitrary", "arbitrary")),
        name="fox_online" if online else "fox",
    )(qa, ka, va_t)


def _dilated_kernel(shift_ref, q_ref, kp_ref, kc_ref, vp_ref, vc_ref, o_ref, qq, kk, vv, qq4, kk4, vv4, osc, lsc,
                    *, bounded):
    u = pl.program_id(1)
    span = q_ref.shape[1]
    qq[...] = q_ref[0]
    kk[0:span, :] = kp_ref[0]
    kk[span:2 * span, :] = kc_ref[0]
    vv[0:span, :] = vp_ref[0]
    vv[span:2 * span, :] = vc_ref[0]
    for src, dst in ((qq, qq4), (kk, kk4), (vv, vv4)):
        part = src.shape[0] // DIL_PRE
        for a in range(DIL_PRE):
            dst[a * part:(a + 1) * part, :] = src[pl.ds(a, part, stride=DIL_PRE), :]

    def rows(buf, buf4, start, n, d):
        if d % DIL_PRE:
            return buf[pl.ds(start, n, stride=d), :]
        part = buf4.shape[0] // DIL_PRE
        a = lax.rem(start, DIL_PRE)
        return buf4[pl.ds(a * part + lax.div(start, DIL_PRE), n, stride=d // DIL_PRE), :]

    lane = lax.broadcasted_iota(I32, (BLOCK, LANES), 1)
    ql = lax.broadcasted_iota(I32, (BLOCK, 2 * BLOCK), 0)
    kl = lax.broadcasted_iota(I32, (BLOCK, 2 * BLOCK), 1)
    dist = ql + BLOCK - kl
    band = (dist >= 0) & (dist <= BLOCK)
    live = -shift_ref[0, 0] if bounded else 0.0
    bias = jnp.where(band, live, NEG)
    bias_first = jnp.where(band & (kl >= BLOCK), live, NEG)

    def scores(q_start, k_start, d, first):
        qs = rows(qq, qq4, q_start, BLOCK, d).astype(BF16)
        ks = rows(kk, kk4, k_start, 2 * BLOCK, d).astype(BF16)
        mask = jnp.where(first, bias_first, bias)
        out = []
        for j in range(HEADS_PER_TILE):
            qj = jnp.where(lane // HEAD_DIM == j, qs, jnp.zeros_like(qs))
            out.append(lax.dot_general(qj, ks, (((1,), (1,)), ((), ())), preferred_element_type=F32) + mask)
        return out

    def finish(s_heads, k_start, d):
        vs = rows(vv, vv4, k_start, 2 * BLOCK, d).astype(BF16)
        o_heads, lse_heads = [], []
        for s in s_heads:
            if bounded:
                p = jnp.exp(s)
                o_heads.append(jnp.dot(p.astype(BF16), vs, preferred_element_type=F32))
                lse_heads.append(jnp.sum(p, axis=-1, keepdims=True))
            else:
                m = jnp.max(s, axis=-1, keepdims=True)
                p = jnp.exp(s - m)
                l = jnp.sum(p, axis=-1, keepdims=True)
                o_heads.append(jnp.dot((p / l).astype(BF16), vs, preferred_element_type=F32))
                lse_heads.append(m + jnp.log(l))
        o = jnp.where(lane < HEAD_DIM, o_heads[0], o_heads[1])
        lse = jnp.where(lane < HEAD_DIM, lse_heads[0], lse_heads[1])
        return o, lse

    for pidx, (window, d) in enumerate(DIL_PATTERNS):
        assert window // d == BLOCK
        unit = d * BLOCK
        n_problems = (span // unit) * d
        assert n_problems % DIL_GROUP == 0

        def body(g, _, pidx=pidx, d=d, unit=unit):
            starts, s_all = [], []
            for t in range(DIL_GROUP):
                idx = g * DIL_GROUP + t
                w = idx // d
                q_start = w * unit + (idx - w * d)
                k_start = span - unit + q_start
                starts.append((q_start, k_start))
                s_all.append(scores(q_start, k_start, d, jnp.logical_and(u == 0, w == 0)))
            for (q_start, k_start), s_heads in zip(starts, s_all):
                o, lse = finish(s_heads, k_start, d)
                osc[pidx, pl.ds(q_start, BLOCK, stride=d), :] = o
                lsc[pidx, pl.ds(q_start, BLOCK, stride=d), :] = lse
            return 0

        lax.fori_loop(0, n_problems // DIL_GROUP, body, 0)

    if bounded:
        num = osc[0] + osc[1] + osc[2]
        den = lsc[0] + lsc[1] + lsc[2]
    else:
        mx = jnp.maximum(jnp.maximum(lsc[0], lsc[1]), lsc[2])
        num = jnp.zeros((span, LANES), F32)
        den = jnp.zeros((span, LANES), F32)
        for pidx in range(len(DIL_PATTERNS)):
            e = jnp.exp(lsc[pidx] - mx)
            num = num + e * osc[pidx]
            den = den + e
    o_ref[0] = (num / den).astype(o_ref.dtype)


def _dilated(shift, qb, kb, vb, bounded):
    B, S, _ = qb.shape
    span = DIL_SPAN
    cur = pl.BlockSpec((1, span, LANES), lambda b, u, hp: (b, u, hp))
    prev = pl.BlockSpec((1, span, LANES), lambda b, u, hp: (b, jnp.maximum(u - 1, 0), hp))
    return pl.pallas_call(
        functools.partial(_dilated_kernel, bounded=bounded),
        grid=(B, S // span, N_PAIRS),
        in_specs=[pl.BlockSpec(memory_space=pltpu.SMEM), cur, prev, cur, prev, cur],
        out_specs=cur,
        out_shape=jax.ShapeDtypeStruct((B, S, D_GRP), F32),
        scratch_shapes=[pltpu.VMEM((span, LANES), F32),
                        pltpu.VMEM((2 * span, LANES), F32), pltpu.VMEM((2 * span, LANES), F32),
                        pltpu.VMEM((span, LANES), F32),
                        pltpu.VMEM((2 * span, LANES), F32), pltpu.VMEM((2 * span, LANES), F32),
                        pltpu.VMEM((len(DIL_PATTERNS), span, LANES), F32),
                        pltpu.VMEM((len(DIL_PATTERNS), span, LANES), F32)],
        compiler_params=_cparams(("arbitrary", "arbitrary", "arbitrary")),
        name="dilated" if bounded else "dilated_exact",
    )(shift, qb, kb, kb, vb, vb)


def _store_row_tiles(ref, x, first_row=0):
    n = x.shape[0]
    for c in range(ROW_CHUNKS):
        ref[pl.ds(first_row * ROW_CHUNKS + c, n, stride=ROW_CHUNKS), :] = x[:, c * LANES:(c + 1) * LANES]


def _load_row_tiles(ref, n, first_row=0):
    return jnp.concatenate([ref[pl.ds(first_row * ROW_CHUNKS + c, n, stride=ROW_CHUNKS), :]
                            for c in range(ROW_CHUNKS)], axis=1)


def _row_tile_copy(src_ref, src_row, dst_ref, dst_row, sem):
    src = src_ref.at[pl.ds(pl.multiple_of(src_row * ROW_CHUNKS, ROW_CHUNKS), ROW_CHUNKS), :]
    dst = dst_ref.at[pl.ds(pl.multiple_of(dst_row * ROW_CHUNKS, ROW_CHUNKS), ROW_CHUNKS), :]
    return pltpu.make_async_copy(src, dst, sem)


def _outproj_kernel(oa_ref, ob_ref, x_ref, gfox_ref, gdil_ref, wo_ref, gffn_ref, wr_ref, br_ref, upper_ref,
                    h_ref, hn_ref, eid_ref, gate_ref, rank_ref, cnt_ref, run_ref):
    @pl.when(pl.program_id(0) == 0)
    def _():
        run_ref[...] = jnp.zeros_like(run_ref)

    def norm(y, g):
        ms = jnp.mean(y * y, axis=-1, keepdims=True)
        return y * lax.rsqrt(ms + EPS) * g

    a = norm(oa_ref[...], gfox_ref[...]).astype(BF16)
    b = norm(ob_ref[...], gdil_ref[...]).astype(BF16)
    mix = (jnp.dot(a, wo_ref[0:D_GRP, :], preferred_element_type=F32)
           + jnp.dot(b, wo_ref[D_GRP:2 * D_GRP, :], preferred_element_type=F32))
    h = x_ref[...] + mix
    h_ref[...] = h
    hn = norm(h, gffn_ref[...])
    _store_row_tiles(hn_ref, hn)

    z = lax.dot_general(wr_ref[...], hn, (((1,), (1,)), ((), ())), preferred_element_type=F32,
                        precision=lax.Precision.HIGHEST) + br_ref[...]
    tm = z.shape[1]
    best = z[0:1, :]
    g_sel = jnp.zeros((1, tm), I32)
    for g in range(1, N_GROUPS):
        better = z[g:g + 1, :] > best
        g_sel = jnp.where(better, g, g_sel)
        best = jnp.maximum(best, z[g:g + 1, :])
    den = jnp.zeros((1, tm), F32)
    for g in range(N_GROUPS):
        den = den + jnp.exp(z[g:g + 1, :] - best)
    pg_top = 1.0 / den

    ze = jnp.zeros((EXPERTS_PER_GROUP, tm), F32)
    for g in range(N_GROUPS):
        ze = jnp.where(g_sel == g, z[8 + g * EXPERTS_PER_GROUP:8 + (g + 1) * EXPERTS_PER_GROUP, :], ze)
    e_iota = lax.broadcasted_iota(I32, ze.shape, 0)
    v1 = jnp.max(ze, axis=0, keepdims=True)
    i1 = jnp.min(jnp.where(ze == v1, e_iota, EXPERTS_PER_GROUP), axis=0, keepdims=True)
    ze2 = jnp.where(e_iota == i1, -jnp.inf, ze)
    v2 = jnp.max(ze2, axis=0, keepdims=True)
    i2 = jnp.min(jnp.where(ze2 == v2, e_iota, EXPERTS_PER_GROUP), axis=0, keepdims=True)
    e2 = jnp.exp(v2 - v1)
    inv = 1.0 / (1.0 + e2)
    gate1 = inv * pg_top
    gate2 = e2 * inv * pg_top
    eid1 = g_sel * EXPERTS_PER_GROUP + i1
    eid2 = g_sel * EXPERTS_PER_GROUP + i2

    x_iota = lax.broadcasted_iota(I32, (N_EXPERTS, tm), 0)
    hot1 = x_iota == eid1
    hot2 = x_iota == eid2
    multi = jnp.logical_or(hot1, hot2)
    before = jnp.dot(multi.astype(BF16), upper_ref[...], preferred_element_type=F32)
    slot = before + run_ref[:, 0:1]
    rank1 = jnp.sum(jnp.where(hot1, slot, 0.0), axis=0, keepdims=True)
    rank2 = jnp.sum(jnp.where(hot2, slot, 0.0), axis=0, keepdims=True)
    run_ref[...] = run_ref[...] + jnp.sum(multi.astype(F32), axis=1, keepdims=True)

    eid_ref[...] = jnp.concatenate([eid1, eid2], axis=0)
    gate_ref[...] = jnp.concatenate([gate1, gate2], axis=0)
    rank_ref[...] = jnp.concatenate([rank1, rank2], axis=0).astype(I32)
    cnt_ref[...] = run_ref[...].astype(I32)


def _outproj(oa, ob, x2, gfox, gdil, w_out, gffn, w_r, b_r, upper):
    T, D = x2.shape
    tm = TM_OUT
    const = lambda shape: pl.BlockSpec(shape, lambda i: (0,) * len(shape))
    tok = lambda w: pl.BlockSpec((tm, w), lambda i: (i, 0))
    lanes2 = pl.BlockSpec((2, tm), lambda i: (0, i))
    return pl.pallas_call(
        _outproj_kernel,
        grid=(T // tm,),
        in_specs=[tok(D_GRP), tok(D_GRP), tok(D), const((1, D_GRP)), const((1, D_GRP)), const((D, D)),
                  const((1, D)), const((ROUTER_ROWS, D)), const((ROUTER_ROWS, 1)), const((tm, tm))],
        out_specs=[tok(D), pl.BlockSpec((tm * ROW_CHUNKS, LANES), lambda i: (i, 0)),
                   lanes2, lanes2, lanes2, const((N_EXPERTS, LANES))],
        out_shape=[jax.ShapeDtypeStruct((T, D), F32), jax.ShapeDtypeStruct((T * ROW_CHUNKS, LANES), F32),
                   jax.ShapeDtypeStruct((2, T), I32), jax.ShapeDtypeStruct((2, T), F32),
                   jax.ShapeDtypeStruct((2, T), I32), jax.ShapeDtypeStruct((N_EXPERTS, LANES), I32)],
        scratch_shapes=[pltpu.VMEM((N_EXPERTS, LANES), F32)],
        compiler_params=_cparams(("arbitrary",)),
        name="outproj",
    )(oa, ob, x2, gfox, gdil, w_out, gffn, w_r, b_r, upper)


def _scatter_kernel(starts_ref, cnt_ref, pos0_ref, pos1_ref, hn_ref, xs_ref, ring, sems, zero_sem, *, n_steps):
    i = pl.program_id(0)
    tm = hn_ref.shape[0] // ROW_CHUNKS
    slot = lax.rem(i, 2)

    def wait_slot(s):
        for _ in range(2):
            pltpu.make_async_copy(ring.at[s], xs_ref.at[pl.ds(0, tm * ROW_CHUNKS), :], sems.at[s]).wait()

    @pl.when(i >= 2)
    def _():
        wait_slot(slot)

    ring[slot] = hn_ref[...]

    def start(r, _):
        for k, pos_ref in enumerate((pos0_ref, pos1_ref)):
            _row_tile_copy(ring.at[slot], r, xs_ref, pos_ref[r], sems.at[slot]).start(priority=k)
        return 0

    lax.fori_loop(0, tm, start, 0, unroll=DMA_UNROLL)

    @pl.when(i == n_steps - 1)
    def _():
        wait_slot(slot)
        if n_steps > 1:
            wait_slot(1 - slot)
        ring[0] = jnp.zeros((tm * ROW_CHUNKS, LANES), F32)

        def pad_expert(e, _, wait):
            n_pad = lax.rem(TILE_E - lax.rem(cnt_ref[e], TILE_E), TILE_E)
            first = starts_ref[e] + cnt_ref[e]
            size = TILE_E // 2
            while size >= 1:
                row0 = first + (n_pad & ~(2 * size - 1))

                @pl.when((n_pad & size) != 0)
                def _(size=size, row0=row0):
                    copy = pltpu.make_async_copy(
                        ring.at[0, pl.ds(0, size * ROW_CHUNKS), :],
                        xs_ref.at[pl.ds(pl.multiple_of(row0 * ROW_CHUNKS, ROW_CHUNKS), size * ROW_CHUNKS), :],
                        zero_sem)
                    copy.wait() if wait else copy.start()

                size //= 2
            return 0

        lax.fori_loop(0, N_EXPERTS, functools.partial(pad_expert, wait=False), 0)
        last = N_EXPERTS - 1
        used_rows = starts_ref[last] + cnt_ref[last] + lax.rem(TILE_E - lax.rem(cnt_ref[last], TILE_E), TILE_E)
        n_tail = xs_ref.shape[0] // (tm * ROW_CHUNKS) - used_rows // tm

        def tail_copy(t):
            row0 = pl.multiple_of((used_rows + t * tm) * ROW_CHUNKS, tm * ROW_CHUNKS)
            return pltpu.make_async_copy(ring.at[0], xs_ref.at[pl.ds(row0, tm * ROW_CHUNKS), :], zero_sem)

        lax.fori_loop(0, n_tail, lambda t, c: (tail_copy(t).start(), c)[1], 0)
        lax.fori_loop(0, N_EXPERTS, functools.partial(pad_expert, wait=True), 0)
        lax.fori_loop(0, n_tail, lambda t, c: (tail_copy(t).wait(), c)[1], 0)


def _scatter(starts, cnt, pos, hn, n_rows):
    T = hn.shape[0] // ROW_CHUNKS
    tm = TM_ROWS
    assert TILE_E % tm == 0 and TILE_E // 2 <= tm
    n_steps = T // tm
    grid_spec = pltpu.PrefetchScalarGridSpec(
        num_scalar_prefetch=2,
        grid=(n_steps,),
        in_specs=[pl.BlockSpec((tm,), lambda i, starts, cnt: (i,), memory_space=pltpu.SMEM),
                  pl.BlockSpec((tm,), lambda i, starts, cnt: (i,), memory_space=pltpu.SMEM),
                  pl.BlockSpec((tm * ROW_CHUNKS, LANES), lambda i, starts, cnt: (i, 0))],
        out_specs=pl.BlockSpec(memory_space=pl.ANY),
        scratch_shapes=[pltpu.VMEM((2, tm * ROW_CHUNKS, LANES), F32),
                        pltpu.SemaphoreType.DMA((2,)), pltpu.SemaphoreType.DMA(())],
    )
    return pl.pallas_call(
        functools.partial(_scatter_kernel, n_steps=n_steps),
        grid_spec=grid_spec,
        out_shape=jax.ShapeDtypeStruct((n_rows * ROW_CHUNKS, LANES), F32),
        compiler_params=_cparams(("arbitrary",)),
        name="scatter_rows",
    )(starts, cnt, pos[0], pos[1], hn)


def _experts_kernel(te_ref, tb_ref, tr_ref, xs_ref, wg_ref, wu_ref, wd_ref, y_ref, wg_bf, wu_bf, wd_bf):
    del tb_ref
    i = pl.program_id(0)
    rows = tr_ref[i]
    new_expert = jnp.logical_or(i == 0, te_ref[i] != te_ref[jnp.maximum(i - 1, 0)])

    @pl.when(jnp.logical_and(rows > 0, new_expert))
    def _():
        wg_bf[...] = wg_ref[0].astype(BF16)
        wu_bf[...] = wu_ref[0].astype(BF16)
        wd_bf[...] = wd_ref[0].astype(BF16)

    half = TILE_E // 2
    for part in range(2):
        @pl.when(rows > part * half)
        def _(part=part):
            x = _load_row_tiles(xs_ref, half, part * half).astype(BF16)
            g = jnp.dot(x, wg_bf[...], preferred_element_type=F32)
            up = jnp.dot(x, wu_bf[...], preferred_element_type=F32)
            hmid = (g * jax.nn.sigmoid(g) * up).astype(BF16)
            _store_row_tiles(y_ref, jnp.dot(hmid, wd_bf[...], preferred_element_type=F32), part * half)

    @pl.when(jnp.logical_and(rows > 0, rows <= half))
    def _():
        y_ref[half * ROW_CHUNKS:, :] = jnp.zeros((half * ROW_CHUNKS, LANES), F32)


def _experts(tile_expert, tile_block, tile_rows, xs, w_gate, w_up, w_down):
    n_tiles = tile_expert.shape[0]
    D = D_MODEL
    rows_spec = pl.BlockSpec((TILE_E * ROW_CHUNKS, LANES), lambda i, te, tb, tr: (tb[i], 0))
    weights = lambda shape: pl.BlockSpec(shape, lambda i, te, tb, tr: (te[i], 0, 0))
    grid_spec = pltpu.PrefetchScalarGridSpec(
        num_scalar_prefetch=3,
        grid=(n_tiles,),
        in_specs=[rows_spec, weights((1, D, D_EXPERT)), weights((1, D, D_EXPERT)), weights((1, D_EXPERT, D))],
        out_specs=rows_spec,
        scratch_shapes=[pltpu.VMEM((D, D_EXPERT), BF16), pltpu.VMEM((D, D_EXPERT), BF16),
                        pltpu.VMEM((D_EXPERT, D), BF16)],
    )
    return pl.pallas_call(
        _experts_kernel,
        grid_spec=grid_spec,
        out_shape=jax.ShapeDtypeStruct(xs.shape, F32),
        input_output_aliases={3: 0},
        compiler_params=_cparams(("arbitrary",)),
        name="experts",
    )(tile_expert, tile_block, tile_rows, xs, w_gate, w_up, w_down)


def _combine_kernel(pos0_ref, pos1_ref, next0_ref, next1_ref, h_ref, gate_ref, y_ref, o_ref, ybuf, sems, *, n_steps):
    i = pl.program_id(0)
    tm = h_ref.shape[0]
    slot = lax.rem(i, 2)

    def gather(p_refs, s):
        def start(r, _):
            for k, p_ref in enumerate(p_refs):
                _row_tile_copy(y_ref, p_ref[r], ybuf.at[s, k], r, sems.at[s]).start(priority=k)
            return 0

        lax.fori_loop(0, tm, start, 0, unroll=DMA_UNROLL)

    @pl.when(i == 0)
    def _():
        gather((pos0_ref, pos1_ref), slot)

    @pl.when(i + 1 < n_steps)
    def _():
        gather((next0_ref, next1_ref), 1 - slot)

    for k in range(2):
        pltpu.make_async_copy(y_ref.at[pl.ds(0, tm * ROW_CHUNKS), :], ybuf.at[slot, k], sems.at[slot]).wait()
    g = gate_ref[...]
    o_ref[...] = (h_ref[...] + g[:, 0:1] * _load_row_tiles(ybuf.at[slot, 0], tm)
                  + g[:, 1:2] * _load_row_tiles(ybuf.at[slot, 1], tm))


def _combine(pos, h, gate_t, y):
    T, D = h.shape
    tm = TM_ROWS
    n_steps = T // tm
    return pl.pallas_call(
        functools.partial(_combine_kernel, n_steps=n_steps),
        grid=(n_steps,),
        in_specs=[pl.BlockSpec((tm,), lambda i: (i,), memory_space=pltpu.SMEM),
                  pl.BlockSpec((tm,), lambda i: (i,), memory_space=pltpu.SMEM),
                  pl.BlockSpec((tm,), lambda i: (jnp.minimum(i + 1, n_steps - 1),), memory_space=pltpu.SMEM),
                  pl.BlockSpec((tm,), lambda i: (jnp.minimum(i + 1, n_steps - 1),), memory_space=pltpu.SMEM),
                  pl.BlockSpec((tm, D), lambda i: (i, 0)),
                  pl.BlockSpec((tm, 2), lambda i: (i, 0)),
                  pl.BlockSpec(memory_space=pl.ANY)],
        out_specs=pl.BlockSpec((tm, D), lambda i: (i, 0)),
        out_shape=jax.ShapeDtypeStruct((T, D), F32),
        scratch_shapes=[pltpu.VMEM((2, 2, tm * ROW_CHUNKS, LANES), F32), pltpu.SemaphoreType.DMA((2,))],
        compiler_params=_cparams(("arbitrary",)),
        name="combine",
    )(pos[0], pos[1], pos[0], pos[1], h, gate_t, y)


def _rope_tables(S):
    lane = jnp.arange(LANES)
    inv_freq = 1.0 / (ROPE_THETA ** (jnp.arange(0, HEAD_DIM, 2, dtype=F32) / HEAD_DIM))
    ang = jnp.arange(S, dtype=F32)[:, None] * inv_freq[lane % (HEAD_DIM // 2)][None, :]
    sign = jnp.where(lane % HEAD_DIM < HEAD_DIM // 2, -1.0, 1.0).astype(F32)
    return jnp.cos(ang), jnp.sin(ang) * sign[None, :]


def _layer(x, norm_mix, w_in, b_forget, q_norm_fox, k_norm_fox, q_norm_dil, k_norm_dil,
           out_norm_fox, out_norm_dil, w_out, norm_ffn, w_router_group, b_router_group,
           w_router_expert, b_router_expert, w_gate, w_up, w_down):
    B, S, D = x.shape
    T = B * S
    n_main = 6 * D_GRP

    w_main = w_in.astype(BF16)
    w_vt = w_main[:, 2 * D_GRP:3 * D_GRP].T
    w_f = jnp.pad(w_main[:, n_main:], ((0, 0), (0, LANES - N_HEADS)))
    b_f = jnp.pad(b_forget, (0, LANES - N_HEADS))[None, :]
    per_head = lambda g: jnp.tile(g, N_HEADS)[None, :]
    bd = jnp.kron(jnp.eye(N_HEADS, dtype=F32), jnp.ones((HEAD_DIM, HEAD_DIM), F32)).astype(BF16)
    cos_t, sin_t = _rope_tables(S)
    tri = jnp.tril(jnp.ones((TM_IN, TM_IN), F32)).astype(BF16)
    upper = jnp.triu(jnp.ones((TM_OUT, TM_OUT), F32), k=1).astype(BF16)
    w_r = jnp.concatenate([
        jnp.pad(w_router_group.T, ((0, 8 - N_GROUPS), (0, 0))),
        w_router_expert.transpose(0, 2, 1).reshape(N_EXPERTS, D)], axis=0)
    b_r = jnp.concatenate([jnp.pad(b_router_group, (0, 8 - N_GROUPS)), b_router_expert.reshape(-1)])[:, None]

    bound = (HEAD_DIM / math.sqrt(HEAD_DIM)) * LOG2E * jnp.max(jnp.abs(q_norm_fox)) * jnp.max(jnp.abs(k_norm_fox))
    shift = FOX_BOUND_SLACK * bound + 1.0
    qa, ka, va, qb, kb, vb = _inproj(
        x, norm_mix[None, :], w_main, w_vt, w_f, b_f, jnp.full((1, LANES), shift, F32),
        per_head(q_norm_fox), per_head(k_norm_fox),
        per_head(q_norm_dil), per_head(k_norm_dil), bd, cos_t, sin_t, tri)
    oa = lax.cond(2.0 * shift <= FOX_SAFE_SPAN,
                  lambda: _fox(qa, ka, va, online=False), lambda: _fox(qa, ka, va, online=True))
    dil_shift = (FOX_BOUND_SLACK * (HEAD_DIM / math.sqrt(HEAD_DIM))
                 * jnp.max(jnp.abs(q_norm_dil)) * jnp.max(jnp.abs(k_norm_dil)) + 1.0)
    dil_shift_arr = jnp.full((1, 1), dil_shift, F32)
    ob = lax.cond(2.0 * dil_shift <= DIL_SAFE_SPAN,
                  lambda: _dilated(dil_shift_arr, qb, kb, vb, bounded=True),
                  lambda: _dilated(dil_shift_arr, qb, kb, vb, bounded=False))

    h, hn, eid, gate, rank, cnt = _outproj(
        oa.reshape(T, D_GRP), ob.reshape(T, D_GRP), x.reshape(T, D), out_norm_fox[None, :],
        out_norm_dil[None, :], w_out.astype(BF16), norm_ffn[None, :], w_r, b_r, upper)

    counts = cnt[:, 0]
    padded = ((counts + TILE_E - 1) // TILE_E) * TILE_E
    ends = jnp.cumsum(padded)
    starts = ends - padded
    is_expert = eid[:, :, None] == jnp.arange(N_EXPERTS, dtype=I32)
    pos = jnp.sum(jnp.where(is_expert, starts, 0), axis=-1) + rank
    n_tiles = (2 * T) // TILE_E + N_EXPERTS
    tile_index = jnp.arange(n_tiles, dtype=I32)
    tile_valid = tile_index * TILE_E < ends[-1]
    tile_block = jnp.minimum(tile_index, ends[-1] // TILE_E - 1)
    in_region = ends[None, :] <= (tile_block * TILE_E)[:, None]
    tile_expert = jnp.sum(in_region.astype(I32), axis=1)
    expert_of_tile = tile_expert[:, None] == jnp.arange(N_EXPERTS, dtype=I32)
    rows_before = tile_block * TILE_E - jnp.sum(jnp.where(expert_of_tile, starts, 0), axis=1)
    tile_rows = jnp.clip(jnp.sum(jnp.where(expert_of_tile, counts, 0), axis=1) - rows_before, 0, TILE_E)
    tile_rows = jnp.where(tile_valid, tile_rows, 0).astype(I32)

    xs = _scatter(starts, counts, pos, hn, n_tiles * TILE_E)
    y = _experts(tile_expert, tile_block, tile_rows, xs, w_gate, w_up, w_down)
    out = _combine(pos, h, gate.T, y)
    return out.reshape(B, S, D)


def kernel(x, norm_mix, w_in, b_forget, q_norm_fox, k_norm_fox, q_norm_dil, k_norm_dil, out_norm_fox,
           out_norm_dil, w_out, norm_ffn, w_router_group, b_router_group, w_router_expert,
           b_router_expert, w_gate, w_up, w_down):
    h = x
    for l in range(norm_mix.shape[0]):
        h = _layer(h, norm_mix[l], w_in[l], b_forget[l], q_norm_fox[l], k_norm_fox[l], q_norm_dil[l],
                   k_norm_dil[l], out_norm_fox[l], out_norm_dil[l], w_out[l], norm_ffn[l],
                   w_router_group[l], b_router_group[l], w_router_expert[l], b_router_expert[l],
                   w_gate[l], w_up[l], w_down[l])
    return h
```

```python
import functools
import math

import jax
import jax.numpy as jnp
from jax import lax
from jax.experimental import pallas as pl
from jax.experimental.pallas import tpu as pltpu

F32 = jnp.float32
BF16 = jnp.bfloat16
I32 = jnp.int32

D_MODEL = 1024
HEAD_DIM = 64
N_HEADS = 8
D_GRP = N_HEADS * HEAD_DIM
LANES = 128
HEADS_PER_TILE = LANES // HEAD_DIM
N_PAIRS = D_GRP // LANES
DIL_PATTERNS = ((128, 1), (512, 4), (2048, 16))
BLOCK = 128
ROPE_THETA = 10000.0
N_GROUPS = 4
EXPERTS_PER_GROUP = 8
N_EXPERTS = N_GROUPS * EXPERTS_PER_GROUP
D_EXPERT = 512
EPS = 1e-6
NEG = -1e30
LOG2E = 1.4426950408889634

TM_IN = 1024
TQ = 1024
TK_WIDE = 1024
TK_SUB = 256
FOX_AHEAD = 4
FOX_BOUND_SLACK = 1.02
DIL_SAFE_SPAN = 69.0
FOX_SAFE_SPAN = 100.0
DIL_SPAN = 2048
DIL_GROUP = 16
DIL_PRE = 4
TM_OUT = 512
TM_ROWS = 512
TILE_E = 512
ROUTER_ROWS = 8 + N_EXPERTS
ROW_CHUNKS = D_MODEL // LANES
DMA_UNROLL = 8
VMEM_LIMIT = 56 * 1024 * 1024


def _cparams(sem, flags=None):
    return pltpu.CompilerParams(dimension_semantics=sem, vmem_limit_bytes=VMEM_LIMIT, flags=flags)


def _inproj_kernel(x_ref, gmix_ref, w_ref, wvt_ref, wf_ref, bf_ref, shift_ref, gqa_ref, gka_ref, gqb_ref, gkb_ref,
                   bd_ref, cos_ref, sin_ref, tri_ref,
                   qa_ref, ka_ref, va_ref, qb_ref, kb_ref, vb_ref, carry_ref):
    @pl.when(pl.program_id(1) == 0)
    def _():
        carry_ref[...] = jnp.zeros_like(carry_ref)

    x = x_ref[0]
    ms = jnp.mean(x * x, axis=-1, keepdims=True)
    xn = (x * lax.rsqrt(ms + EPS) * gmix_ref[...]).astype(BF16)

    def seg(j):
        return jnp.dot(xn, w_ref[:, j * D_GRP:(j + 1) * D_GRP], preferred_element_type=F32)

    def head_norm(y, g_ref, scale):
        ss = jnp.dot((y * y).astype(BF16), bd_ref[...], preferred_element_type=F32) * (1.0 / HEAD_DIM)
        return y * lax.rsqrt(ss + EPS) * (g_ref[...] * scale)

    lane = lax.broadcasted_iota(I32, (x.shape[0], LANES), 1)
    first_half = (lane % HEAD_DIM) < (HEAD_DIM // 2)

    def spread(t):
        t = t + pltpu.roll(t, HEAD_DIM // 2, 1)
        return t + pltpu.roll(t, HEAD_DIM, 1)

    cos = spread(cos_ref[...])
    sin = jnp.where(first_half, -1.0, 1.0) * spread(sin_ref[...])

    def rope(y):
        outs = []
        for j in range(N_PAIRS):
            ys = y[:, j * LANES:(j + 1) * LANES]
            partner = jnp.where(first_half, pltpu.roll(ys, LANES - HEAD_DIM // 2, 1),
                                pltpu.roll(ys, HEAD_DIM // 2, 1))
            outs.append(ys * cos + partner * sin)
        return jnp.concatenate(outs, axis=1)

    scale = 1.0 / math.sqrt(HEAD_DIM)
    va_ref[0] = lax.dot_general(wvt_ref[...], xn, (((1,), (1,)), ((), ())),
                                preferred_element_type=F32).astype(va_ref.dtype)
    qb_ref[0] = rope(head_norm(seg(3), gqb_ref, scale)).astype(qb_ref.dtype)
    kb_ref[0] = rope(head_norm(seg(4), gkb_ref, 1.0)).astype(kb_ref.dtype)
    vb_ref[0] = seg(5).astype(vb_ref.dtype)

    fa = jnp.dot(xn, wf_ref[...], preferred_element_type=F32) + bf_ref[...]
    logf = jnp.minimum(fa, 0.0) - jnp.log1p(jnp.exp(-jnp.abs(fa)))
    hi = logf.astype(BF16)
    mid = (logf - hi.astype(F32)).astype(BF16)
    lo = (logf - hi.astype(F32) - mid.astype(F32)).astype(BF16)
    parts = jnp.dot(tri_ref[...], jnp.concatenate([hi, mid, lo], axis=1), preferred_element_type=F32)
    c = parts[:, :LANES] + parts[:, LANES:2 * LANES] + parts[:, 2 * LANES:] + carry_ref[...]
    carry_ref[...] = c[c.shape[0] - 1:, :]

    qa = head_norm(seg(0), gqa_ref, scale * LOG2E)
    ka = head_norm(seg(1), gka_ref, 1.0)
    c2 = c * LOG2E
    ones = (jnp.where((lane >= HEAD_DIM + 3) & (lane < HEAD_DIM + 6), 1.0, 0.0)
            - jnp.where(lane == HEAD_DIM + 6, shift_ref[...], 0.0))
    for h in range(N_HEADS):
        cb = jnp.broadcast_to(c2[:, h:h + 1], (x.shape[0], LANES))
        hi = cb.astype(BF16).astype(F32)
        mid = (cb - hi).astype(BF16).astype(F32)
        lo = cb - hi - mid
        pieces = jnp.where(lane == HEAD_DIM, hi, jnp.where(lane == HEAD_DIM + 1, mid,
                           jnp.where(lane == HEAD_DIM + 2, lo, 0.0)))
        q_extra = pieces + ones
        k_extra = jnp.where(((lane >= HEAD_DIM) & (lane < HEAD_DIM + 3)) | (lane == HEAD_DIM + 6), 1.0, 0.0) \
            - pltpu.roll(pieces, 3, 1)
        j, odd = divmod(h, HEADS_PER_TILE)
        qp = qa[:, j * LANES:(j + 1) * LANES]
        kp = ka[:, j * LANES:(j + 1) * LANES]
        if odd:
            qp = pltpu.roll(qp, HEAD_DIM, 1)
            kp = pltpu.roll(kp, HEAD_DIM, 1)
        qa_ref[0, h] = jnp.where(lane < HEAD_DIM, qp, q_extra).astype(qa_ref.dtype)
        ka_ref[0, h] = jnp.where(lane < HEAD_DIM, kp, k_extra).astype(ka_ref.dtype)


def _inproj(x, gmix, w_main, w_vt, w_f, b_f, shift, gqa, gka, gqb, gkb, bd, cos_t, sin_t, tri):
    B, S, D = x.shape
    tm = TM_IN
    const = lambda shape: pl.BlockSpec(shape, lambda b, i: (0,) * len(shape))
    tok = lambda w, dt: jax.ShapeDtypeStruct((B, S, w), dt)
    tok_spec = lambda w: pl.BlockSpec((1, tm, w), lambda b, i: (b, i, 0))
    head_spec = pl.BlockSpec((1, N_HEADS, tm, LANES), lambda b, i: (b, 0, i, 0))
    head_shape = jax.ShapeDtypeStruct((B, N_HEADS, S, LANES), BF16)
    return pl.pallas_call(
        _inproj_kernel,
        grid=(B, S // tm),
        in_specs=[tok_spec(D), const((1, D)), const(w_main.shape), const(w_vt.shape), const(w_f.shape),
                  const((1, LANES)), const((1, LANES)),
                  const((1, D_GRP)), const((1, D_GRP)), const((1, D_GRP)), const((1, D_GRP)),
                  const((D_GRP, D_GRP)),
                  pl.BlockSpec((tm, LANES), lambda b, i: (i, 0)),
                  pl.BlockSpec((tm, LANES), lambda b, i: (i, 0)),
                  const((tm, tm))],
        out_specs=[head_spec, head_spec, pl.BlockSpec((1, D_GRP, tm), lambda b, i: (b, 0, i))]
        + [tok_spec(D_GRP)] * 3,
        out_shape=[head_shape, head_shape, jax.ShapeDtypeStruct((B, D_GRP, S), BF16),
                   tok(D_GRP, F32), tok(D_GRP, F32), tok(D_GRP, F32)],
        scratch_shapes=[pltpu.VMEM((1, LANES), F32)],
        compiler_params=_cparams(("arbitrary", "arbitrary")),
        name="inproj",
    )(x, gmix, w_main, w_vt, w_f, b_f, shift, gqa, gka, gqb, gkb, bd, cos_t, sin_t, tri)


def _fox_kernel(q_ref, k_ref, v_ref, o_ref, *, online):
    qi = pl.program_id(2)
    tq = q_ref.shape[2]

    def step(start, width, carry, diag):
        carry = list(carry)
        sub = min(TK_SUB, width)
        chunks = [(c, j) for c in range(width // sub) for j in range(HEADS_PER_TILE)]
        def first_query(c):
            return c * sub if diag else 0

        def score(c, j):
            k = k_ref[0, j, pl.ds(start + c * sub, sub), :]
            q = q_ref[0, j, first_query(c):, :]
            return lax.dot_general(k, q, (((1,), (1,)), ((), ())), preferred_element_type=F32)

        def tail(full, lo, new):
            return new if lo == 0 else jnp.concatenate([full[:, :lo], new], axis=1)

        scores = {cj: score(*cj) for cj in chunks[:FOX_AHEAD]}
        for n, (c, j) in enumerate(chunks):
                if n + FOX_AHEAD < len(chunks):
                    nxt = chunks[n + FOX_AHEAD]
                    scores[nxt] = score(*nxt)
                m, l, acc = carry[j]
                lo = first_query(c)
                vt = v_ref[0, :, pl.ds(start + c * sub, sub)]
                s = scores.pop((c, j))
                if diag:
                    key = lax.broadcasted_iota(I32, s.shape, 0)
                    qry = lax.broadcasted_iota(I32, s.shape, 1)
                    s = jnp.where(key <= qry, s, NEG)
                if online:
                    m_new = jnp.maximum(m[:, lo:], jnp.max(s, axis=0, keepdims=True))
                    alpha = jnp.exp2(m[:, lo:] - m_new)
                    p = jnp.exp2(s - m_new)
                    l_new = alpha * l[:, lo:] + jnp.sum(p, axis=0, keepdims=True)
                    acc_new = alpha * acc[:, lo:] + jnp.dot(vt, p.astype(BF16), preferred_element_type=F32)
                    m = tail(m, lo, m_new)
                else:
                    p = jnp.exp2(s)
                    l_new = l[:, lo:] + jnp.sum(p, axis=0, keepdims=True)
                    acc_new = acc[:, lo:] + jnp.dot(vt, p.astype(BF16), preferred_element_type=F32)
                carry[j] = (m, tail(l, lo, l_new), tail(acc, lo, acc_new))
        return tuple(carry)

    init = tuple((jnp.full((1, tq), NEG, F32), jnp.zeros((1, tq), F32), jnp.zeros((LANES, tq), F32))
                 for _ in range(HEADS_PER_TILE))
    per_wide = TK_WIDE // tq
    n_wide = qi // per_wide
    carry = lax.fori_loop(
        0, n_wide, lambda i, c: step(pl.multiple_of(i * TK_WIDE, TK_WIDE), TK_WIDE, c, False), init)
    for extra in range(per_wide - 1):
        carry = lax.cond(qi - n_wide * per_wide > extra,
                         lambda c, e=extra: step(pl.multiple_of((n_wide * per_wide + e) * tq, tq), tq, c, False),
                         lambda c: c, carry)
    carry = step(pl.multiple_of(qi * tq, tq), tq, carry, True)
    outs = [acc / l for (_, l, acc) in carry]
    feat = lax.broadcasted_iota(I32, (LANES, tq), 0)
    o_ref[0] = jnp.where(feat < HEAD_DIM, outs[0], outs[1]).T.astype(o_ref.dtype)


def _fox(qa, ka, va_t, online):
    B, _, S, _ = qa.shape
    return pl.pallas_call(
        functools.partial(_fox_kernel, online=online),
        grid=(B, N_PAIRS, S // TQ),
        in_specs=[pl.BlockSpec((1, HEADS_PER_TILE, TQ, LANES), lambda b, hp, i: (b, hp, i, 0)),
                  pl.BlockSpec((1, HEADS_PER_TILE, S, LANES), lambda b, hp, i: (b, hp, 0, 0)),
                  pl.BlockSpec((1, LANES, S), lambda b, hp, i: (b, hp, 0))],
        out_specs=pl.BlockSpec((1, TQ, LANES), lambda b, hp, i: (b, i, hp)),
        out_shape=jax.ShapeDtypeStruct((B, S, D_GRP), F32),
        compiler_params=_cparams(("arbitrary", "arbitrary", "arbitrary")),
        name="fox_online" if online else "fox",
    )(qa, ka, va_t)


def _dilated_kernel(shift_ref, q_ref, kp_ref, kc_ref, vp_ref, vc_ref, o_ref, qq, kk, vv, qq4, kk4, vv4, osc, lsc,
                    *, bounded):
    u = pl.program_id(1)
    span = q_ref.shape[1]
    qq[...] = q_ref[0]
    kk[0:span, :] = kp_ref[0]
    kk[span:2 * span, :] = kc_ref[0]
    vv[0:span, :] = vp_ref[0]
    vv[span:2 * span, :] = vc_ref[0]
    for src, dst in ((qq, qq4), (kk, kk4), (vv, vv4)):
        part = src.shape[0] // DIL_PRE
        for a in range(DIL_PRE):
            dst[a * part:(a + 1) * part, :] = src[pl.ds(a, part, stride=DIL_PRE), :]

    def rows(buf, buf4, start, n, d):
        if d % DIL_PRE:
            return buf[pl.ds(start, n, stride=d), :]
        part = buf4.shape[0] // DIL_PRE
        a = lax.rem(start, DIL_PRE)
        return buf4[pl.ds(a * part + lax.div(start, DIL_PRE), n, stride=d // DIL_PRE), :]

    lane = lax.broadcasted_iota(I32, (BLOCK, LANES), 1)
    ql = lax.broadcasted_iota(I32, (BLOCK, 2 * BLOCK), 0)
    kl = lax.broadcasted_iota(I32, (BLOCK, 2 * BLOCK), 1)
    dist = ql + BLOCK - kl
    band = (dist >= 0) & (dist <= BLOCK)
    live = -shift_ref[0, 0] if bounded else 0.0
    bias = jnp.where(band, live, NEG)
    bias_first = jnp.where(band & (kl >= BLOCK), live, NEG)

    def scores(q_start, k_start, d, first):
        qs = rows(qq, qq4, q_start, BLOCK, d).astype(BF16)
        ks = rows(kk, kk4, k_start, 2 * BLOCK, d).astype(BF16)
        mask = jnp.where(first, bias_first, bias)
        out = []
        for j in range(HEADS_PER_TILE):
            qj = jnp.where(lane // HEAD_DIM == j, qs, jnp.zeros_like(qs))
            out.append(lax.dot_general(qj, ks, (((1,), (1,)), ((), ())), preferred_element_type=F32) + mask)
        return out

    def finish(s_heads, k_start, d):
        vs = rows(vv, vv4, k_start, 2 * BLOCK, d).astype(BF16)
        o_heads, lse_heads = [], []
        for s in s_heads:
            if bounded:
                p = jnp.exp(s)
                o_heads.append(jnp.dot(p.astype(BF16), vs, preferred_element_type=F32))
                lse_heads.append(jnp.sum(p, axis=-1, keepdims=True))
            else:
                m = jnp.max(s, axis=-1, keepdims=True)
                p = jnp.exp(s - m)
                l = jnp.sum(p, axis=-1, keepdims=True)
                o_heads.append(jnp.dot((p / l).astype(BF16), vs, preferred_element_type=F32))
                lse_heads.append(m + jnp.log(l))
        o = jnp.where(lane < HEAD_DIM, o_heads[0], o_heads[1])
        lse = jnp.where(lane < HEAD_DIM, lse_heads[0], lse_heads[1])
        return o, lse

    for pidx, (window, d) in enumerate(DIL_PATTERNS):
        assert window // d == BLOCK
        unit = d * BLOCK
        n_problems = (span // unit) * d
        assert n_problems % DIL_GROUP == 0

        def body(g, _, pidx=pidx, d=d, unit=unit):
            starts, s_all = [], []
            for t in range(DIL_GROUP):
                idx = g * DIL_GROUP + t
                w = idx // d
                q_start = w * unit + (idx - w * d)
                k_start = span - unit + q_start
                starts.append((q_start, k_start))
                s_all.append(scores(q_start, k_start, d, jnp.logical_and(u == 0, w == 0)))
            for (q_start, k_start), s_heads in zip(starts, s_all):
                o, lse = finish(s_heads, k_start, d)
                osc[pidx, pl.ds(q_start, BLOCK, stride=d), :] = o
                lsc[pidx, pl.ds(q_start, BLOCK, stride=d), :] = lse
            return 0

        lax.fori_loop(0, n_problems // DIL_GROUP, body, 0)

    if bounded:
        num = osc[0] + osc[1] + osc[2]
        den = lsc[0] + lsc[1] + lsc[2]
    else:
        mx = jnp.maximum(jnp.maximum(lsc[0], lsc[1]), lsc[2])
        num = jnp.zeros((span, LANES), F32)
        den = jnp.zeros((span, LANES), F32)
        for pidx in range(len(DIL_PATTERNS)):
            e = jnp.exp(lsc[pidx] - mx)
            num = num + e * osc[pidx]
            den = den + e
    o_ref[0] = (num / den).astype(o_ref.dtype)


def _dilated(shift, qb, kb, vb, bounded):
    B, S, _ = qb.shape
    span = DIL_SPAN
    cur = pl.BlockSpec((1, span, LANES), lambda b, u, hp: (b, u, hp))
    prev = pl.BlockSpec((1, span, LANES), lambda b, u, hp: (b, jnp.maximum(u - 1, 0), hp))
    return pl.pallas_call(
        functools.partial(_dilated_kernel, bounded=bounded),
        grid=(B, S // span, N_PAIRS),
        in_specs=[pl.BlockSpec(memory_space=pltpu.SMEM), cur, prev, cur, prev, cur],
        out_specs=cur,
        out_shape=jax.ShapeDtypeStruct((B, S, D_GRP), F32),
        scratch_shapes=[pltpu.VMEM((span, LANES), F32),
                        pltpu.VMEM((2 * span, LANES), F32), pltpu.VMEM((2 * span, LANES), F32),
                        pltpu.VMEM((span, LANES), F32),
                        pltpu.VMEM((2 * span, LANES), F32), pltpu.VMEM((2 * span, LANES), F32),
                        pltpu.VMEM((len(DIL_PATTERNS), span, LANES), F32),
                        pltpu.VMEM((len(DIL_PATTERNS), span, LANES), F32)],
        compiler_params=_cparams(("arbitrary", "arbitrary", "arbitrary")),
        name="dilated" if bounded else "dilated_exact",
    )(shift, qb, kb, kb, vb, vb)


def _store_row_tiles(ref, x, first_row=0):
    n = x.shape[0]
    for c in range(ROW_CHUNKS):
        ref[pl.ds(first_row * ROW_CHUNKS + c, n, stride=ROW_CHUNKS), :] = x[:, c * LANES:(c + 1) * LANES]


def _load_row_tiles(ref, n, first_row=0):
    return jnp.concatenate([ref[pl.ds(first_row * ROW_CHUNKS + c, n, stride=ROW_CHUNKS), :]
                            for c in range(ROW_CHUNKS)], axis=1)


def _row_tile_copy(src_ref, src_row, dst_ref, dst_row, sem):
    src = src_ref.at[pl.ds(pl.multiple_of(src_row * ROW_CHUNKS, ROW_CHUNKS), ROW_CHUNKS), :]
    dst = dst_ref.at[pl.ds(pl.multiple_of(dst_row * ROW_CHUNKS, ROW_CHUNKS), ROW_CHUNKS), :]
    return pltpu.make_async_copy(src, dst, sem)


def _outproj_kernel(oa_ref, ob_ref, x_ref, gfox_ref, gdil_ref, wo_ref, gffn_ref, wr_ref, br_ref, upper_ref,
                    h_ref, hn_ref, eid_ref, gate_ref, rank_ref, cnt_ref, run_ref):
    @pl.when(pl.program_id(0) == 0)
    def _():
        run_ref[...] = jnp.zeros_like(run_ref)

    def norm(y, g):
        ms = jnp.mean(y * y, axis=-1, keepdims=True)
        return y * lax.rsqrt(ms + EPS) * g

    a = norm(oa_ref[...], gfox_ref[...]).astype(BF16)
    b = norm(ob_ref[...], gdil_ref[...]).astype(BF16)
    mix = (jnp.dot(a, wo_ref[0:D_GRP, :], preferred_element_type=F32)
           + jnp.dot(b, wo_ref[D_GRP:2 * D_GRP, :], preferred_element_type=F32))
    h = x_ref[...] + mix
    h_ref[...] = h
    hn = norm(h, gffn_ref[...])
    _store_row_tiles(hn_ref, hn)

    z = lax.dot_general(wr_ref[...], hn, (((1,), (1,)), ((), ())), preferred_element_type=F32,
                        precision=lax.Precision.HIGHEST) + br_ref[...]
    tm = z.shape[1]
    best = z[0:1, :]
    g_sel = jnp.zeros((1, tm), I32)
    for g in range(1, N_GROUPS):
        better = z[g:g + 1, :] > best
        g_sel = jnp.where(better, g, g_sel)
        best = jnp.maximum(best, z[g:g + 1, :])
    den = jnp.zeros((1, tm), F32)
    for g in range(N_GROUPS):
        den = den + jnp.exp(z[g:g + 1, :] - best)
    pg_top = 1.0 / den

    ze = jnp.zeros((EXPERTS_PER_GROUP, tm), F32)
    for g in range(N_GROUPS):
        ze = jnp.where(g_sel == g, z[8 + g * EXPERTS_PER_GROUP:8 + (g + 1) * EXPERTS_PER_GROUP, :], ze)
    e_iota = lax.broadcasted_iota(I32, ze.shape, 0)
    v1 = jnp.max(ze, axis=0, keepdims=True)
    i1 = jnp.min(jnp.where(ze == v1, e_iota, EXPERTS_PER_GROUP), axis=0, keepdims=True)
    ze2 = jnp.where(e_iota == i1, -jnp.inf, ze)
    v2 = jnp.max(ze2, axis=0, keepdims=True)
    i2 = jnp.min(jnp.where(ze2 == v2, e_iota, EXPERTS_PER_GROUP), axis=0, keepdims=True)
    e2 = jnp.exp(v2 - v1)
    inv = 1.0 / (1.0 + e2)
    gate1 = inv * pg_top
    gate2 = e2 * inv * pg_top
    eid1 = g_sel * EXPERTS_PER_GROUP + i1
    eid2 = g_sel * EXPERTS_PER_GROUP + i2

    x_iota = lax.broadcasted_iota(I32, (N_EXPERTS, tm), 0)
    hot1 = x_iota == eid1
    hot2 = x_iota == eid2
    multi = jnp.logical_or(hot1, hot2)
    before = jnp.dot(multi.astype(BF16), upper_ref[...], preferred_element_type=F32)
    slot = before + run_ref[:, 0:1]
    rank1 = jnp.sum(jnp.where(hot1, slot, 0.0), axis=0, keepdims=True)
    rank2 = jnp.sum(jnp.where(hot2, slot, 0.0), axis=0, keepdims=True)
    run_ref[...] = run_ref[...] + jnp.sum(multi.astype(F32), axis=1, keepdims=True)

    eid_ref[...] = jnp.concatenate([eid1, eid2], axis=0)
    gate_ref[...] = jnp.concatenate([gate1, gate2], axis=0)
    rank_ref[...] = jnp.concatenate([rank1, rank2], axis=0).astype(I32)
    cnt_ref[...] = run_ref[...].astype(I32)


def _outproj(oa, ob, x2, gfox, gdil, w_out, gffn, w_r, b_r, upper):
    T, D = x2.shape
    tm = TM_OUT
    const = lambda shape: pl.BlockSpec(shape, lambda i: (0,) * len(shape))
    tok = lambda w: pl.BlockSpec((tm, w), lambda i: (i, 0))
    lanes2 = pl.BlockSpec((2, tm), lambda i: (0, i))
    return pl.pallas_call(
        _outproj_kernel,
        grid=(T // tm,),
        in_specs=[tok(D_GRP), tok(D_GRP), tok(D), const((1, D_GRP)), const((1, D_GRP)), const((D, D)),
                  const((1, D)), const((ROUTER_ROWS, D)), const((ROUTER_ROWS, 1)), const((tm, tm))],
        out_specs=[tok(D), pl.BlockSpec((tm * ROW_CHUNKS, LANES), lambda i: (i, 0)),
                   lanes2, lanes2, lanes2, const((N_EXPERTS, LANES))],
        out_shape=[jax.ShapeDtypeStruct((T, D), F32), jax.ShapeDtypeStruct((T * ROW_CHUNKS, LANES), F32),
                   jax.ShapeDtypeStruct((2, T), I32), jax.ShapeDtypeStruct((2, T), F32),
                   jax.ShapeDtypeStruct((2, T), I32), jax.ShapeDtypeStruct((N_EXPERTS, LANES), I32)],
        scratch_shapes=[pltpu.VMEM((N_EXPERTS, LANES), F32)],
        compiler_params=_cparams(("arbitrary",)),
        name="outproj",
    )(oa, ob, x2, gfox, gdil, w_out, gffn, w_r, b_r, upper)


def _scatter_kernel(starts_ref, cnt_ref, pos0_ref, pos1_ref, hn_ref, xs_ref, ring, sems, zero_sem, *, n_steps):
    i = pl.program_id(0)
    tm = hn_ref.shape[0] // ROW_CHUNKS
    slot = lax.rem(i, 2)

    def wait_slot(s):
        for _ in range(2):
            pltpu.make_async_copy(ring.at[s], xs_ref.at[pl.ds(0, tm * ROW_CHUNKS), :], sems.at[s]).wait()

    @pl.when(i >= 2)
    def _():
        wait_slot(slot)

    ring[slot] = hn_ref[...]

    def start(r, _):
        for k, pos_ref in enumerate((pos0_ref, pos1_ref)):
            _row_tile_copy(ring.at[slot], r, xs_ref, pos_ref[r], sems.at[slot]).start(priority=k)
        return 0

    lax.fori_loop(0, tm, start, 0, unroll=DMA_UNROLL)

    @pl.when(i == n_steps - 1)
    def _():
        wait_slot(slot)
        if n_steps > 1:
            wait_slot(1 - slot)
        ring[0] = jnp.zeros((tm * ROW_CHUNKS, LANES), F32)

        def pad_expert(e, _, wait):
            n_pad = lax.rem(TILE_E - lax.rem(cnt_ref[e], TILE_E), TILE_E)
            first = starts_ref[e] + cnt_ref[e]
            size = TILE_E // 2
            while size >= 1:
                row0 = first + (n_pad & ~(2 * size - 1))

                @pl.when((n_pad & size) != 0)
                def _(size=size, row0=row0):
                    copy = pltpu.make_async_copy(
                        ring.at[0, pl.ds(0, size * ROW_CHUNKS), :],
                        xs_ref.at[pl.ds(pl.multiple_of(row0 * ROW_CHUNKS, ROW_CHUNKS), size * ROW_CHUNKS), :],
                        zero_sem)
                    copy.wait() if wait else copy.start()

                size //= 2
            return 0

        lax.fori_loop(0, N_EXPERTS, functools.partial(pad_expert, wait=False), 0)
        last = N_EXPERTS - 1
        used_rows = starts_ref[last] + cnt_ref[last] + lax.rem(TILE_E - lax.rem(cnt_ref[last], TILE_E), TILE_E)
        n_tail = xs_ref.shape[0] // (tm * ROW_CHUNKS) - used_rows // tm

        def tail_copy(t):
            row0 = pl.multiple_of((used_rows + t * tm) * ROW_CHUNKS, tm * ROW_CHUNKS)
            return pltpu.make_async_copy(ring.at[0], xs_ref.at[pl.ds(row0, tm * ROW_CHUNKS), :], zero_sem)

        lax.fori_loop(0, n_tail, lambda t, c: (tail_copy(t).start(), c)[1], 0)
        lax.fori_loop(0, N_EXPERTS, functools.partial(pad_expert, wait=True), 0)
        lax.fori_loop(0, n_tail, lambda t, c: (tail_copy(t).wait(), c)[1], 0)


def _scatter(starts, cnt, pos, hn, n_rows):
    T = hn.shape[0] // ROW_CHUNKS
    tm = TM_ROWS
    assert TILE_E % tm == 0 and TILE_E // 2 <= tm
    n_steps = T // tm
    grid_spec = pltpu.PrefetchScalarGridSpec(
        num_scalar_prefetch=2,
        grid=(n_steps,),
        in_specs=[pl.BlockSpec((tm,), lambda i, starts, cnt: (i,), memory_space=pltpu.SMEM),
                  pl.BlockSpec((tm,), lambda i, starts, cnt: (i,), memory_space=pltpu.SMEM),
                  pl.BlockSpec((tm * ROW_CHUNKS, LANES), lambda i, starts, cnt: (i, 0))],
        out_specs=pl.BlockSpec(memory_space=pl.ANY),
        scratch_shapes=[pltpu.VMEM((2, tm * ROW_CHUNKS, LANES), F32),
                        pltpu.SemaphoreType.DMA((2,)), pltpu.SemaphoreType.DMA(())],
    )
    return pl.pallas_call(
        functools.partial(_scatter_kernel, n_steps=n_steps),
        grid_spec=grid_spec,
        out_shape=jax.ShapeDtypeStruct((n_rows * ROW_CHUNKS, LANES), F32),
        compiler_params=_cparams(("arbitrary",)),
        name="scatter_rows",
    )(starts, cnt, pos[0], pos[1], hn)


def _experts_kernel(te_ref, tb_ref, tr_ref, xs_ref, wg_ref, wu_ref, wd_ref, y_ref, wg_bf, wu_bf, wd_bf):
    del tb_ref
    i = pl.program_id(0)
    rows = tr_ref[i]
    new_expert = jnp.logical_or(i == 0, te_ref[i] != te_ref[jnp.maximum(i - 1, 0)])

    @pl.when(jnp.logical_and(rows > 0, new_expert))
    def _():
        wg_bf[...] = wg_ref[0].astype(BF16)
        wu_bf[...] = wu_ref[0].astype(BF16)
        wd_bf[...] = wd_ref[0].astype(BF16)

    half = TILE_E // 2
    for part in range(2):
        @pl.when(rows > part * half)
        def _(part=part):
            x = _load_row_tiles(xs_ref, half, part * half).astype(BF16)
            g = jnp.dot(x, wg_bf[...], preferred_element_type=F32)
            up = jnp.dot(x, wu_bf[...], preferred_element_type=F32)
            hmid = (g * jax.nn.sigmoid(g) * up).astype(BF16)
            _store_row_tiles(y_ref, jnp.dot(hmid, wd_bf[...], preferred_element_type=F32), part * half)

    @pl.when(jnp.logical_and(rows > 0, rows <= half))
    def _():
        y_ref[half * ROW_CHUNKS:, :] = jnp.zeros((half * ROW_CHUNKS, LANES), F32)


def _experts(tile_expert, tile_block, tile_rows, xs, w_gate, w_up, w_down):
    n_tiles = tile_expert.shape[0]
    D = D_MODEL
    rows_spec = pl.BlockSpec((TILE_E * ROW_CHUNKS, LANES), lambda i, te, tb, tr: (tb[i], 0))
    weights = lambda shape: pl.BlockSpec(shape, lambda i, te, tb, tr: (te[i], 0, 0))
    grid_spec = pltpu.PrefetchScalarGridSpec(
        num_scalar_prefetch=3,
        grid=(n_tiles,),
        in_specs=[rows_spec, weights((1, D, D_EXPERT)), weights((1, D, D_EXPERT)), weights((1, D_EXPERT, D))],
        out_specs=rows_spec,
        scratch_shapes=[pltpu.VMEM((D, D_EXPERT), BF16), pltpu.VMEM((D, D_EXPERT), BF16),
                        pltpu.VMEM((D_EXPERT, D), BF16)],
    )
    return pl.pallas_call(
        _experts_kernel,
        grid_spec=grid_spec,
        out_shape=jax.ShapeDtypeStruct(xs.shape, F32),
        input_output_aliases={3: 0},
        compiler_params=_cparams(("arbitrary",)),
        name="experts",
    )(tile_expert, tile_block, tile_rows, xs, w_gate, w_up, w_down)


def _combine_kernel(pos0_ref, pos1_ref, next0_ref, next1_ref, h_ref, gate_ref, y_ref, o_ref, ybuf, sems, *, n_steps):
    i = pl.program_id(0)
    tm = h_ref.shape[0]
    slot = lax.rem(i, 2)

    def gather(p_refs, s):
        def start(r, _):
            for k, p_ref in enumerate(p_refs):
                _row_tile_copy(y_ref, p_ref[r], ybuf.at[s, k], r, sems.at[s]).start(priority=k)
            return 0

        lax.fori_loop(0, tm, start, 0, unroll=DMA_UNROLL)

    @pl.when(i == 0)
    def _():
        gather((pos0_ref, pos1_ref), slot)

    @pl.when(i + 1 < n_steps)
    def _():
        gather((next0_ref, next1_ref), 1 - slot)

    for k in range(2):
        pltpu.make_async_copy(y_ref.at[pl.ds(0, tm * ROW_CHUNKS), :], ybuf.at[slot, k], sems.at[slot]).wait()
    g = gate_ref[...]
    o_ref[...] = (h_ref[...] + g[:, 0:1] * _load_row_tiles(ybuf.at[slot, 0], tm)
                  + g[:, 1:2] * _load_row_tiles(ybuf.at[slot, 1], tm))


def _combine(pos, h, gate_t, y):
    T, D = h.shape
    tm = TM_ROWS
    n_steps = T // tm
    return pl.pallas_call(
        functools.partial(_combine_kernel, n_steps=n_steps),
        grid=(n_steps,),
        in_specs=[pl.BlockSpec((tm,), lambda i: (i,), memory_space=pltpu.SMEM),
                  pl.BlockSpec((tm,), lambda i: (i,), memory_space=pltpu.SMEM),
                  pl.BlockSpec((tm,), lambda i: (jnp.minimum(i + 1, n_steps - 1),), memory_space=pltpu.SMEM),
                  pl.BlockSpec((tm,), lambda i: (jnp.minimum(i + 1, n_steps - 1),), memory_space=pltpu.SMEM),
                  pl.BlockSpec((tm, D), lambda i: (i, 0)),
                  pl.BlockSpec((tm, 2), lambda i: (i, 0)),
                  pl.BlockSpec(memory_space=pl.ANY)],
        out_specs=pl.BlockSpec((tm, D), lambda i: (i, 0)),
        out_shape=jax.ShapeDtypeStruct((T, D), F32),
        scratch_shapes=[pltpu.VMEM((2, 2, tm * ROW_CHUNKS, LANES), F32), pltpu.SemaphoreType.DMA((2,))],
        compiler_params=_cparams(("arbitrary",)),
        name="combine",
    )(pos[0], pos[1], pos[0], pos[1], h, gate_t, y)


def _rope_tables(S):
    inv_freq = 1.0 / (ROPE_THETA ** (jnp.arange(0, HEAD_DIM, 2, dtype=F32) / HEAD_DIM))
    ang = jnp.arange(S, dtype=F32)[:, None] * inv_freq[None, :]
    widen = lambda t: jnp.pad(t, ((0, 0), (0, LANES - HEAD_DIM // 2)))
    return widen(jnp.cos(ang)), widen(jnp.sin(ang))


def _layer(x, norm_mix, w_in, b_forget, q_norm_fox, k_norm_fox, q_norm_dil, k_norm_dil,
           out_norm_fox, out_norm_dil, w_out, norm_ffn, w_router_group, b_router_group,
           w_router_expert, b_router_expert, w_gate, w_up, w_down):
    B, S, D = x.shape
    T = B * S
    n_main = 6 * D_GRP

    w_main = w_in.astype(BF16)
    w_vt = w_main[:, 2 * D_GRP:3 * D_GRP].T
    w_f = jnp.pad(w_main[:, n_main:], ((0, 0), (0, LANES - N_HEADS)))
    b_f = jnp.pad(b_forget, (0, LANES - N_HEADS))[None, :]
    per_head = lambda g: jnp.tile(g, N_HEADS)[None, :]
    bd = jnp.kron(jnp.eye(N_HEADS, dtype=F32), jnp.ones((HEAD_DIM, HEAD_DIM), F32)).astype(BF16)
    cos_t, sin_t = _rope_tables(S)
    tri = jnp.tril(jnp.ones((TM_IN, TM_IN), F32)).astype(BF16)
    upper = jnp.triu(jnp.ones((TM_OUT, TM_OUT), F32), k=1).astype(BF16)
    w_r = jnp.concatenate([
        jnp.pad(w_router_group.T, ((0, 8 - N_GROUPS), (0, 0))),
        w_router_expert.transpose(0, 2, 1).reshape(N_EXPERTS, D)], axis=0)
    b_r = jnp.concatenate([jnp.pad(b_router_group, (0, 8 - N_GROUPS)), b_router_expert.reshape(-1)])[:, None]

    bound = (HEAD_DIM / math.sqrt(HEAD_DIM)) * LOG2E * jnp.max(jnp.abs(q_norm_fox)) * jnp.max(jnp.abs(k_norm_fox))
    shift = FOX_BOUND_SLACK * bound + 1.0
    qa, ka, va, qb, kb, vb = _inproj(
        x, norm_mix[None, :], w_main, w_vt, w_f, b_f, jnp.full((1, LANES), shift, F32),
        per_head(q_norm_fox), per_head(k_norm_fox),
        per_head(q_norm_dil), per_head(k_norm_dil), bd, cos_t, sin_t, tri)
    oa = lax.cond(2.0 * shift <= FOX_SAFE_SPAN,
                  lambda: _fox(qa, ka, va, online=False), lambda: _fox(qa, ka, va, online=True))
    dil_shift = (FOX_BOUND_SLACK * (HEAD_DIM / math.sqrt(HEAD_DIM))
                 * jnp.max(jnp.abs(q_norm_dil)) * jnp.max(jnp.abs(k_norm_dil)) + 1.0)
    dil_shift_arr = jnp.full((1, 1), dil_shift, F32)
    ob = lax.cond(2.0 * dil_shift <= DIL_SAFE_SPAN,
                  lambda: _dilated(dil_shift_arr, qb, kb, vb, bounded=True),
                  lambda: _dilated(dil_shift_arr, qb, kb, vb, bounded=False))

    h, hn, eid, gate, rank, cnt = _outproj(
        oa.reshape(T, D_GRP), ob.reshape(T, D_GRP), x.reshape(T, D), out_norm_fox[None, :],
        out_norm_dil[None, :], w_out.astype(BF16), norm_ffn[None, :], w_r, b_r, upper)

    counts = cnt[:, 0]
    padded = ((counts + TILE_E - 1) // TILE_E) * TILE_E
    ends = jnp.cumsum(padded)
    starts = ends - padded
    is_expert = eid[:, :, None] == jnp.arange(N_EXPERTS, dtype=I32)
    pos = jnp.sum(jnp.where(is_expert, starts, 0), axis=-1) + rank
    n_tiles = (2 * T) // TILE_E + N_EXPERTS
    tile_index = jnp.arange(n_tiles, dtype=I32)
    tile_valid = tile_index * TILE_E < ends[-1]
    tile_block = jnp.minimum(tile_index, ends[-1] // TILE_E - 1)
    in_region = ends[None, :] <= (tile_block * TILE_E)[:, None]
    tile_expert = jnp.sum(in_region.astype(I32), axis=1)
    expert_of_tile = tile_expert[:, None] == jnp.arange(N_EXPERTS, dtype=I32)
    rows_before = tile_block * TILE_E - jnp.sum(jnp.where(expert_of_tile, starts, 0), axis=1)
    tile_rows = jnp.clip(jnp.sum(jnp.where(expert_of_tile, counts, 0), axis=1) - rows_before, 0, TILE_E)
    tile_rows = jnp.where(tile_valid, tile_rows, 0).astype(I32)

    xs = _scatter(starts, counts, pos, hn, n_tiles * TILE_E)
    y = _experts(tile_expert, tile_block, tile_rows, xs, w_gate, w_up, w_down)
    out = _combine(pos, h, gate.T, y)
    return out.reshape(B, S, D)


def kernel(x, norm_mix, w_in, b_forget, q_norm_fox, k_norm_fox, q_norm_dil, k_norm_dil, out_norm_fox,
           out_norm_dil, w_out, norm_ffn, w_router_group, b_router_group, w_router_expert,
           b_router_expert, w_gate, w_up, w_down):
    h = x
    for l in range(norm_mix.shape[0]):
        h = _layer(h, norm_mix[l], w_in[l], b_forget[l], q_norm_fox[l], k_norm_fox[l], q_norm_dil[l],
                   k_norm_dil[l], out_norm_fox[l], out_norm_dil[l], w_out[l], norm_ffn[l],
                   w_router_group[l], b_router_group[l], w_router_expert[l], b_router_expert[l],
                   w_gate[l], w_up[l], w_down[l])
    return h
```

```python
import functools
import math

import jax
import jax.numpy as jnp
from jax import lax
from jax.experimental import pallas as pl
from jax.experimental.pallas import tpu as pltpu

F32 = jnp.float32
BF16 = jnp.bfloat16
I32 = jnp.int32

D_MODEL = 1024
HEAD_DIM = 64
N_HEADS = 8
D_GRP = N_HEADS * HEAD_DIM
LANES = 128
HEADS_PER_TILE = LANES // HEAD_DIM
N_PAIRS = D_GRP // LANES
DIL_PATTERNS = ((128, 1), (512, 4), (2048, 16))
BLOCK = 128
ROPE_THETA = 10000.0
N_GROUPS = 4
EXPERTS_PER_GROUP = 8
N_EXPERTS = N_GROUPS * EXPERTS_PER_GROUP
D_EXPERT = 512
EPS = 1e-6
NEG = -1e30
LOG2E = 1.4426950408889634

TM_IN = 1024
TQ = 1024
TK_WIDE = 1024
TK_SUB = 256
FOX_AHEAD = 4
FOX_BOUND_SLACK = 1.02
DIL_SAFE_SPAN = 69.0
FOX_SAFE_SPAN = 100.0
DIL_SPAN = 2048
DIL_GROUP = 16
DIL_PRE = 4
TM_OUT = 512
TM_ROWS = 512
TILE_E = 512
ROUTER_ROWS = 8 + N_EXPERTS
ROW_CHUNKS = D_MODEL // LANES
DMA_UNROLL = 8
VMEM_LIMIT = 56 * 1024 * 1024


def _cparams(sem, flags=None):
    return pltpu.CompilerParams(dimension_semantics=sem, vmem_limit_bytes=VMEM_LIMIT, flags=flags)


def _inproj_kernel(x_ref, gmix_ref, w_ref, wvt_ref, wf_ref, bf_ref, shift_ref, gqa_ref, gka_ref, gqb_ref, gkb_ref,
                   bd_ref, cos_ref, sin_ref, tri_ref,
                   qa_ref, ka_ref, va_ref, qb_ref, kb_ref, vb_ref, carry_ref):
    @pl.when(pl.program_id(1) == 0)
    def _():
        carry_ref[...] = jnp.zeros_like(carry_ref)

    x = x_ref[0]
    ms = jnp.mean(x * x, axis=-1, keepdims=True)
    xn = (x * lax.rsqrt(ms + EPS) * gmix_ref[...]).astype(BF16)

    def seg(j):
        return jnp.dot(xn, w_ref[:, j * D_GRP:(j + 1) * D_GRP], preferred_element_type=F32)

    def head_norm(y, g_ref, scale):
        ss = jnp.dot((y * y).astype(BF16), bd_ref[...], preferred_element_type=F32) * (1.0 / HEAD_DIM)
        return y * lax.rsqrt(ss + EPS) * (g_ref[...] * scale)

    lane = lax.broadcasted_iota(I32, (x.shape[0], LANES), 1)
    first_half = (lane % HEAD_DIM) < (HEAD_DIM // 2)

    def spread(t):
        t = t + pltpu.roll(t, HEAD_DIM // 2, 1)
        return t + pltpu.roll(t, HEAD_DIM, 1)

    cos = spread(cos_ref[...])
    sin = jnp.where(first_half, -1.0, 1.0) * spread(sin_ref[...])

    def rope(y):
        outs = []
        for j in range(N_PAIRS):
            ys = y[:, j * LANES:(j + 1) * LANES]
            partner = jnp.where(first_half, pltpu.roll(ys, LANES - HEAD_DIM // 2, 1),
                                pltpu.roll(ys, HEAD_DIM // 2, 1))
            outs.append(ys * cos + partner * sin)
        return jnp.concatenate(outs, axis=1)

    scale = 1.0 / math.sqrt(HEAD_DIM)
    va_ref[0] = lax.dot_general(wvt_ref[...], xn, (((1,), (1,)), ((), ())),
                                preferred_element_type=F32).astype(va_ref.dtype)
    qb_ref[0] = rope(head_norm(seg(3), gqb_ref, scale)).astype(qb_ref.dtype)
    kb_ref[0] = rope(head_norm(seg(4), gkb_ref, 1.0)).astype(kb_ref.dtype)
    vb_ref[0] = seg(5).astype(vb_ref.dtype)

    fa = jnp.dot(xn, wf_ref[...], preferred_element_type=F32) + bf_ref[...]
    logf = jnp.minimum(fa, 0.0) - jnp.log1p(jnp.exp(-jnp.abs(fa)))
    hi = logf.astype(BF16)
    mid = (logf - hi.astype(F32)).astype(BF16)
    lo = (logf - hi.astype(F32) - mid.astype(F32)).astype(BF16)
    parts = jnp.dot(tri_ref[...], jnp.concatenate([hi, mid, lo], axis=1), preferred_element_type=F32)
    c = parts[:, :LANES] + parts[:, LANES:2 * LANES] + parts[:, 2 * LANES:] + carry_ref[...]
    carry_ref[...] = c[c.shape[0] - 1:, :]

    qa = head_norm(seg(0), gqa_ref, scale * LOG2E)
    ka = head_norm(seg(1), gka_ref, 1.0)
    c2 = c * LOG2E
    ones = (jnp.where((lane >= HEAD_DIM + 3) & (lane < HEAD_DIM + 6), 1.0, 0.0)
            - jnp.where(lane == HEAD_DIM + 6, shift_ref[...], 0.0))
    for h in range(N_HEADS):
        cb = jnp.broadcast_to(c2[:, h:h + 1], (x.shape[0], LANES))
        hi = cb.astype(BF16).astype(F32)
        mid = (cb - hi).astype(BF16).astype(F32)
        lo = cb - hi - mid
        pieces = jnp.where(lane == HEAD_DIM, hi, jnp.where(lane == HEAD_DIM + 1, mid,
                           jnp.where(lane == HEAD_DIM + 2, lo, 0.0)))
        q_extra = pieces + ones
        k_extra = jnp.where(((lane >= HEAD_DIM) & (lane < HEAD_DIM + 3)) | (lane == HEAD_DIM + 6), 1.0, 0.0) \
            - pltpu.roll(pieces, 3, 1)
        j, odd = divmod(h, HEADS_PER_TILE)
        qp = qa[:, j * LANES:(j + 1) * LANES]
        kp = ka[:, j * LANES:(j + 1) * LANES]
        if odd:
            qp = pltpu.roll(qp, HEAD_DIM, 1)
            kp = pltpu.roll(kp, HEAD_DIM, 1)
        qa_ref[0, h] = jnp.where(lane < HEAD_DIM, qp, q_extra).astype(qa_ref.dtype)
        ka_ref[0, h] = jnp.where(lane < HEAD_DIM, kp, k_extra).astype(ka_ref.dtype)


def _inproj(x, gmix, w_main, w_vt, w_f, b_f, shift, gqa, gka, gqb, gkb, bd, cos_t, sin_t, tri):
    B, S, D = x.shape
    tm = TM_IN
    const = lambda shape: pl.BlockSpec(shape, lambda b, i: (0,) * len(shape))
    tok = lambda w, dt: jax.ShapeDtypeStruct((B, S, w), dt)
    tok_spec = lambda w: pl.BlockSpec((1, tm, w), lambda b, i: (b, i, 0))
    head_spec = pl.BlockSpec((1, N_HEADS, tm, LANES), lambda b, i: (b, 0, i, 0))
    head_shape = jax.ShapeDtypeStruct((B, N_HEADS, S, LANES), BF16)
    return pl.pallas_call(
        _inproj_kernel,
        grid=(B, S // tm),
        in_specs=[tok_spec(D), const((1, D)), const(w_main.shape), const(w_vt.shape), const(w_f.shape),
                  const((1, LANES)), const((1, LANES)),
                  const((1, D_GRP)), const((1, D_GRP)), const((1, D_GRP)), const((1, D_GRP)),
                  const((D_GRP, D_GRP)),
                  pl.BlockSpec((tm, LANES), lambda b, i: (i, 0)),
                  pl.BlockSpec((tm, LANES), lambda b, i: (i, 0)),
                  const((tm, tm))],
        out_specs=[head_spec, head_spec, pl.BlockSpec((1, D_GRP, tm), lambda b, i: (b, 0, i))]
        + [tok_spec(D_GRP)] * 3,
        out_shape=[head_shape, head_shape, jax.ShapeDtypeStruct((B, D_GRP, S), BF16),
                   tok(D_GRP, F32), tok(D_GRP, F32), tok(D_GRP, F32)],
        scratch_shapes=[pltpu.VMEM((1, LANES), F32)],
        compiler_params=_cparams(("arbitrary", "arbitrary")),
        name="inproj",
    )(x, gmix, w_main, w_vt, w_f, b_f, shift, gqa, gka, gqb, gkb, bd, cos_t, sin_t, tri)


def _fox_kernel(q_ref, k_ref, v_ref, o_ref, *, online):
    qi = pl.program_id(2)
    tq = q_ref.shape[2]

    def step(start, width, carry, diag):
        carry = list(carry)
        sub = min(TK_SUB, width)
        chunks = [(c, j) for c in range(width // sub) for j in range(HEADS_PER_TILE)]
        def first_query(c):
            return c * sub if diag else 0

        def score(c, j):
            k = k_ref[0, j, pl.ds(start + c * sub, sub), :]
            q = q_ref[0, j, first_query(c):, :]
            return lax.dot_general(k, q, (((1,), (1,)), ((), ())), preferred_element_type=F32)

        def tail(full, lo, new):
            return new if lo == 0 else jnp.concatenate([full[:, :lo], new], axis=1)

        scores = {cj: score(*cj) for cj in chunks[:FOX_AHEAD]}
        for n, (c, j) in enumerate(chunks):
                if n + FOX_AHEAD < len(chunks):
                    nxt = chunks[n + FOX_AHEAD]
                    scores[nxt] = score(*nxt)
                m, l, acc = carry[j]
                lo = first_query(c)
                vt = v_ref[0, :, pl.ds(start + c * sub, sub)]
                s = scores.pop((c, j))
                if diag:
                    key = lax.broadcasted_iota(I32, s.shape, 0)
                    qry = lax.broadcasted_iota(I32, s.shape, 1)
                    s = jnp.where(key <= qry, s, NEG)
                if online:
                    m_new = jnp.maximum(m[:, lo:], jnp.max(s, axis=0, keepdims=True))
                    alpha = jnp.exp2(m[:, lo:] - m_new)
                    p = jnp.exp2(s - m_new)
                    l_new = alpha * l[:, lo:] + jnp.sum(p, axis=0, keepdims=True)
                    acc_new = alpha * acc[:, lo:] + jnp.dot(vt, p.astype(BF16), preferred_element_type=F32)
                    m = tail(m, lo, m_new)
                else:
                    p = jnp.exp2(s)
                    l_new = l[:, lo:] + jnp.sum(p, axis=0, keepdims=True)
                    acc_new = acc[:, lo:] + jnp.dot(vt, p.astype(BF16), preferred_element_type=F32)
                carry[j] = (m, tail(l, lo, l_new), tail(acc, lo, acc_new))
        return tuple(carry)

    init = tuple((jnp.full((1, tq), NEG, F32), jnp.zeros((1, tq), F32), jnp.zeros((LANES, tq), F32))
                 for _ in range(HEADS_PER_TILE))
    per_wide = TK_WIDE // tq
    n_wide = qi // per_wide
    carry = lax.fori_loop(
        0, n_wide, lambda i, c: step(pl.multiple_of(i * TK_WIDE, TK_WIDE), TK_WIDE, c, False), init)
    for extra in range(per_wide - 1):
        carry = lax.cond(qi - n_wide * per_wide > extra,
                         lambda c, e=extra: step(pl.multiple_of((n_wide * per_wide + e) * tq, tq), tq, c, False),
                         lambda c: c, carry)
    carry = step(pl.multiple_of(qi * tq, tq), tq, carry, True)
    outs = [acc / l for (_, l, acc) in carry]
    feat = lax.broadcasted_iota(I32, (LANES, tq), 0)
    o_ref[0] = jnp.where(feat < HEAD_DIM, outs[0], outs[1]).T.astype(o_ref.dtype)


def _fox(qa, ka, va_t, online):
    B, _, S, _ = qa.shape
    return pl.pallas_call(
        functools.partial(_fox_kernel, online=online),
        grid=(B, N_PAIRS, S // TQ),
        in_specs=[pl.BlockSpec((1, HEADS_PER_TILE, TQ, LANES), lambda b, hp, i: (b, hp, i, 0)),
                  pl.BlockSpec((1, HEADS_PER_TILE, S, LANES), lambda b, hp, i: (b, hp, 0, 0)),
                  pl.BlockSpec((1, LANES, S), lambda b, hp, i: (b, hp, 0))],
        out_specs=pl.BlockSpec((1, TQ, LANES), lambda b, hp, i: (b, i, hp)),
        out_shape=jax.ShapeDtypeStruct((B, S, D_GRP), F32),
        compiler_params=_cparams(("arbitrary", "arbitrary", "arbitrary")),
        name="fox_online" if online else "fox",
    )(qa, ka, va_t)


def _dilated_kernel(shift_ref, q_ref, kp_ref, kc_ref, vp_ref, vc_ref, o_ref, qq, kk, vv, qq4, kk4, vv4, osc, lsc,
                    *, bounded):
    u = pl.program_id(1)
    span = q_ref.shape[1]
    qq[...] = q_ref[0]
    kk[0:span, :] = kp_ref[0]
    kk[span:2 * span, :] = kc_ref[0]
    vv[0:span, :] = vp_ref[0]
    vv[span:2 * span, :] = vc_ref[0]
    for src, dst in ((qq, qq4), (kk, kk4), (vv, vv4)):
        part = src.shape[0] // DIL_PRE
        for a in range(DIL_PRE):
            dst[a * part:(a + 1) * part, :] = src[pl.ds(a, part, stride=DIL_PRE), :]

    def rows(buf, buf4, start, n, d):
        if d % DIL_PRE:
            return buf[pl.ds(start, n, stride=d), :]
        part = buf4.shape[0] // DIL_PRE
        a = lax.rem(start, DIL_PRE)
        return buf4[pl.ds(a * part + lax.div(start, DIL_PRE), n, stride=d // DIL_PRE), :]

    lane = lax.broadcasted_iota(I32, (BLOCK, LANES), 1)
    ql = lax.broadcasted_iota(I32, (BLOCK, 2 * BLOCK), 0)
    kl = lax.broadcasted_iota(I32, (BLOCK, 2 * BLOCK), 1)
    dist = ql + BLOCK - kl
    band = (dist >= 0) & (dist <= BLOCK)
    live = -shift_ref[0, 0] if bounded else 0.0
    bias = jnp.where(band, live, NEG)
    bias_first = jnp.where(band & (kl >= BLOCK), live, NEG)

    def scores(q_start, k_start, d, first):
        qs = rows(qq, qq4, q_start, BLOCK, d).astype(BF16)
        ks = rows(kk, kk4, k_start, 2 * BLOCK, d).astype(BF16)
        mask = jnp.where(first, bias_first, bias)
        out = []
        for j in range(HEADS_PER_TILE):
            qj = jnp.where(lane // HEAD_DIM == j, qs, jnp.zeros_like(qs))
            out.append(lax.dot_general(qj, ks, (((1,), (1,)), ((), ())), preferred_element_type=F32) + mask)
        return out

    def finish(s_heads, k_start, d):
        vs = rows(vv, vv4, k_start, 2 * BLOCK, d).astype(BF16)
        o_heads, lse_heads = [], []
        for s in s_heads:
            if bounded:
                p = jnp.exp(s)
                o_heads.append(jnp.dot(p.astype(BF16), vs, preferred_element_type=F32))
                lse_heads.append(jnp.sum(p, axis=-1, keepdims=True))
            else:
                m = jnp.max(s, axis=-1, keepdims=True)
                p = jnp.exp(s - m)
                l = jnp.sum(p, axis=-1, keepdims=True)
                o_heads.append(jnp.dot((p / l).astype(BF16), vs, preferred_element_type=F32))
                lse_heads.append(m + jnp.log(l))
        o = jnp.where(lane < HEAD_DIM, o_heads[0], o_heads[1])
        lse = jnp.where(lane < HEAD_DIM, lse_heads[0], lse_heads[1])
        return o, lse

    for pidx, (window, d) in enumerate(DIL_PATTERNS):
        assert window // d == BLOCK
        unit = d * BLOCK
        n_problems = (span // unit) * d
        assert n_problems % DIL_GROUP == 0

        def body(g, _, pidx=pidx, d=d, unit=unit):
            starts, s_all = [], []
            for t in range(DIL_GROUP):
                idx = g * DIL_GROUP + t
                w = idx // d
                q_start = w * unit + (idx - w * d)
                k_start = span - unit + q_start
                starts.append((q_start, k_start))
                s_all.append(scores(q_start, k_start, d, jnp.logical_and(u == 0, w == 0)))
            for (q_start, k_start), s_heads in zip(starts, s_all):
                o, lse = finish(s_heads, k_start, d)
                osc[pidx, pl.ds(q_start, BLOCK, stride=d), :] = o
                lsc[pidx, pl.ds(q_start, BLOCK, stride=d), :] = lse
            return 0

        lax.fori_loop(0, n_problems // DIL_GROUP, body, 0)

    if bounded:
        num = osc[0] + osc[1] + osc[2]
        den = lsc[0] + lsc[1] + lsc[2]
    else:
        mx = jnp.maximum(jnp.maximum(lsc[0], lsc[1]), lsc[2])
        num = jnp.zeros((span, LANES), F32)
        den = jnp.zeros((span, LANES), F32)
        for pidx in range(len(DIL_PATTERNS)):
            e = jnp.exp(lsc[pidx] - mx)
            num = num + e * osc[pidx]
            den = den + e
    o_ref[0] = (num / den).astype(o_ref.dtype)


def _dilated(shift, qb, kb, vb, bounded):
    B, S, _ = qb.shape
    span = DIL_SPAN
    cur = pl.BlockSpec((1, span, LANES), lambda b, u, hp: (b, u, hp))
    prev = pl.BlockSpec((1, span, LANES), lambda b, u, hp: (b, jnp.maximum(u - 1, 0), hp))
    return pl.pallas_call(
        functools.partial(_dilated_kernel, bounded=bounded),
        grid=(B, S // span, N_PAIRS),
        in_specs=[pl.BlockSpec(memory_space=pltpu.SMEM), cur, prev, cur, prev, cur],
        out_specs=cur,
        out_shape=jax.ShapeDtypeStruct((B, S, D_GRP), F32),
        scratch_shapes=[pltpu.VMEM((span, LANES), F32),
                        pltpu.VMEM((2 * span, LANES), F32), pltpu.VMEM((2 * span, LANES), F32),
                        pltpu.VMEM((span, LANES), F32),
                        pltpu.VMEM((2 * span, LANES), F32), pltpu.VMEM((2 * span, LANES), F32),
                        pltpu.VMEM((len(DIL_PATTERNS), span, LANES), F32),
                        pltpu.VMEM((len(DIL_PATTERNS), span, LANES), F32)],
        compiler_params=_cparams(("arbitrary", "arbitrary", "arbitrary")),
        name="dilated" if bounded else "dilated_exact",
    )(shift, qb, kb, kb, vb, vb)


def _store_row_tiles(ref, stage_ref, x, first_row=0):
    n = x.shape[0]
    for c in range(ROW_CHUNKS):
        stage_ref[pl.ds(first_row * ROW_CHUNKS + c, n, stride=ROW_CHUNKS), :] = x[:, c * LANES:(c + 1) * LANES]
    rows = slice(first_row * ROW_CHUNKS, (first_row + n) * ROW_CHUNKS)
    ref[rows, :] = stage_ref[rows, :].astype(ref.dtype)


def _load_row_tiles(ref, stage_ref, n, first_row=0):
    rows = slice(first_row * ROW_CHUNKS, (first_row + n) * ROW_CHUNKS)
    stage_ref[rows, :] = ref[rows, :].astype(F32)
    return jnp.concatenate([stage_ref[pl.ds(first_row * ROW_CHUNKS + c, n, stride=ROW_CHUNKS), :]
                            for c in range(ROW_CHUNKS)], axis=1)


def _row_tile_copy(src_ref, src_row, dst_ref, dst_row, sem):
    src = src_ref.at[pl.ds(pl.multiple_of(src_row * ROW_CHUNKS, ROW_CHUNKS), ROW_CHUNKS), :]
    dst = dst_ref.at[pl.ds(pl.multiple_of(dst_row * ROW_CHUNKS, ROW_CHUNKS), ROW_CHUNKS), :]
    return pltpu.make_async_copy(src, dst, sem)


def _outproj_kernel(oa_ref, ob_ref, x_ref, gfox_ref, gdil_ref, wo_ref, gffn_ref, wr_ref, br_ref, upper_ref,
                    h_ref, hn_ref, eid_ref, gate_ref, rank_ref, cnt_ref, run_ref, stage_ref):
    @pl.when(pl.program_id(0) == 0)
    def _():
        run_ref[...] = jnp.zeros_like(run_ref)

    def norm(y, g):
        ms = jnp.mean(y * y, axis=-1, keepdims=True)
        return y * lax.rsqrt(ms + EPS) * g

    a = norm(oa_ref[...], gfox_ref[...]).astype(BF16)
    b = norm(ob_ref[...], gdil_ref[...]).astype(BF16)
    mix = (jnp.dot(a, wo_ref[0:D_GRP, :], preferred_element_type=F32)
           + jnp.dot(b, wo_ref[D_GRP:2 * D_GRP, :], preferred_element_type=F32))
    h = x_ref[...] + mix
    h_ref[...] = h
    hn = norm(h, gffn_ref[...])
    _store_row_tiles(hn_ref, stage_ref, hn)

    z = lax.dot_general(wr_ref[...], hn, (((1,), (1,)), ((), ())), preferred_element_type=F32,
                        precision=lax.Precision.HIGHEST) + br_ref[...]
    tm = z.shape[1]
    best = z[0:1, :]
    g_sel = jnp.zeros((1, tm), I32)
    for g in range(1, N_GROUPS):
        better = z[g:g + 1, :] > best
        g_sel = jnp.where(better, g, g_sel)
        best = jnp.maximum(best, z[g:g + 1, :])
    den = jnp.zeros((1, tm), F32)
    for g in range(N_GROUPS):
        den = den + jnp.exp(z[g:g + 1, :] - best)
    pg_top = 1.0 / den

    ze = jnp.zeros((EXPERTS_PER_GROUP, tm), F32)
    for g in range(N_GROUPS):
        ze = jnp.where(g_sel == g, z[8 + g * EXPERTS_PER_GROUP:8 + (g + 1) * EXPERTS_PER_GROUP, :], ze)
    e_iota = lax.broadcasted_iota(I32, ze.shape, 0)
    v1 = jnp.max(ze, axis=0, keepdims=True)
    i1 = jnp.min(jnp.where(ze == v1, e_iota, EXPERTS_PER_GROUP), axis=0, keepdims=True)
    ze2 = jnp.where(e_iota == i1, -jnp.inf, ze)
    v2 = jnp.max(ze2, axis=0, keepdims=True)
    i2 = jnp.min(jnp.where(ze2 == v2, e_iota, EXPERTS_PER_GROUP), axis=0, keepdims=True)
    e2 = jnp.exp(v2 - v1)
    inv = 1.0 / (1.0 + e2)
    gate1 = inv * pg_top
    gate2 = e2 * inv * pg_top
    eid1 = g_sel * EXPERTS_PER_GROUP + i1
    eid2 = g_sel * EXPERTS_PER_GROUP + i2

    x_iota = lax.broadcasted_iota(I32, (N_EXPERTS, tm), 0)
    hot1 = x_iota == eid1
    hot2 = x_iota == eid2
    multi = jnp.logical_or(hot1, hot2)
    before = jnp.dot(multi.astype(BF16), upper_ref[...], preferred_element_type=F32)
    slot = before + run_ref[:, 0:1]
    rank1 = jnp.sum(jnp.where(hot1, slot, 0.0), axis=0, keepdims=True)
    rank2 = jnp.sum(jnp.where(hot2, slot, 0.0), axis=0, keepdims=True)
    run_ref[...] = run_ref[...] + jnp.sum(multi.astype(F32), axis=1, keepdims=True)

    eid_ref[...] = jnp.concatenate([eid1, eid2], axis=0)
    gate_ref[...] = jnp.concatenate([gate1, gate2], axis=0)
    rank_ref[...] = jnp.concatenate([rank1, rank2], axis=0).astype(I32)
    cnt_ref[...] = run_ref[...].astype(I32)


def _outproj(oa, ob, x2, gfox, gdil, w_out, gffn, w_r, b_r, upper):
    T, D = x2.shape
    tm = TM_OUT
    const = lambda shape: pl.BlockSpec(shape, lambda i: (0,) * len(shape))
    tok = lambda w: pl.BlockSpec((tm, w), lambda i: (i, 0))
    lanes2 = pl.BlockSpec((2, tm), lambda i: (0, i))
    return pl.pallas_call(
        _outproj_kernel,
        grid=(T // tm,),
        in_specs=[tok(D_GRP), tok(D_GRP), tok(D), const((1, D_GRP)), const((1, D_GRP)), const((D, D)),
                  const((1, D)), const((ROUTER_ROWS, D)), const((ROUTER_ROWS, 1)), const((tm, tm))],
        out_specs=[tok(D), pl.BlockSpec((tm * ROW_CHUNKS, LANES), lambda i: (i, 0)),
                   lanes2, lanes2, lanes2, const((N_EXPERTS, LANES))],
        out_shape=[jax.ShapeDtypeStruct((T, D), F32), jax.ShapeDtypeStruct((T * ROW_CHUNKS, LANES), BF16),
                   jax.ShapeDtypeStruct((2, T), I32), jax.ShapeDtypeStruct((2, T), F32),
                   jax.ShapeDtypeStruct((2, T), I32), jax.ShapeDtypeStruct((N_EXPERTS, LANES), I32)],
        scratch_shapes=[pltpu.VMEM((N_EXPERTS, LANES), F32), pltpu.VMEM((tm * ROW_CHUNKS, LANES), F32)],
        compiler_params=_cparams(("arbitrary",)),
        name="outproj",
    )(oa, ob, x2, gfox, gdil, w_out, gffn, w_r, b_r, upper)


def _scatter_kernel(starts_ref, cnt_ref, pos0_ref, pos1_ref, hn_ref, xs_ref, ring, sems, zero_sem, *, n_steps):
    i = pl.program_id(0)
    tm = hn_ref.shape[0] // ROW_CHUNKS
    slot = lax.rem(i, 2)

    def wait_slot(s):
        for _ in range(2):
            pltpu.make_async_copy(ring.at[s], xs_ref.at[pl.ds(0, tm * ROW_CHUNKS), :], sems.at[s]).wait()

    @pl.when(i >= 2)
    def _():
        wait_slot(slot)

    ring[slot] = hn_ref[...]

    def start(r, _):
        for k, pos_ref in enumerate((pos0_ref, pos1_ref)):
            _row_tile_copy(ring.at[slot], r, xs_ref, pos_ref[r], sems.at[slot]).start(priority=k)
        return 0

    lax.fori_loop(0, tm, start, 0, unroll=DMA_UNROLL)

    @pl.when(i == n_steps - 1)
    def _():
        wait_slot(slot)
        if n_steps > 1:
            wait_slot(1 - slot)
        ring[0] = jnp.zeros((tm * ROW_CHUNKS, LANES), ring.dtype)

        def pad_expert(e, _, wait):
            n_pad = lax.rem(TILE_E - lax.rem(cnt_ref[e], TILE_E), TILE_E)
            first = starts_ref[e] + cnt_ref[e]
            size = TILE_E // 2
            while size >= 1:
                row0 = first + (n_pad & ~(2 * size - 1))

                @pl.when((n_pad & size) != 0)
                def _(size=size, row0=row0):
                    copy = pltpu.make_async_copy(
                        ring.at[0, pl.ds(0, size * ROW_CHUNKS), :],
                        xs_ref.at[pl.ds(pl.multiple_of(row0 * ROW_CHUNKS, ROW_CHUNKS), size * ROW_CHUNKS), :],
                        zero_sem)
                    copy.wait() if wait else copy.start()

                size //= 2
            return 0

        lax.fori_loop(0, N_EXPERTS, functools.partial(pad_expert, wait=False), 0)
        last = N_EXPERTS - 1
        used_rows = starts_ref[last] + cnt_ref[last] + lax.rem(TILE_E - lax.rem(cnt_ref[last], TILE_E), TILE_E)
        n_tail = xs_ref.shape[0] // (tm * ROW_CHUNKS) - used_rows // tm

        def tail_copy(t):
            row0 = pl.multiple_of((used_rows + t * tm) * ROW_CHUNKS, tm * ROW_CHUNKS)
            return pltpu.make_async_copy(ring.at[0], xs_ref.at[pl.ds(row0, tm * ROW_CHUNKS), :], zero_sem)

        lax.fori_loop(0, n_tail, lambda t, c: (tail_copy(t).start(), c)[1], 0)
        lax.fori_loop(0, N_EXPERTS, functools.partial(pad_expert, wait=True), 0)
        lax.fori_loop(0, n_tail, lambda t, c: (tail_copy(t).wait(), c)[1], 0)


def _scatter(starts, cnt, pos, hn, n_rows):
    T = hn.shape[0] // ROW_CHUNKS
    tm = TM_ROWS
    assert TILE_E % tm == 0 and TILE_E // 2 <= tm
    n_steps = T // tm
    grid_spec = pltpu.PrefetchScalarGridSpec(
        num_scalar_prefetch=2,
        grid=(n_steps,),
        in_specs=[pl.BlockSpec((tm,), lambda i, starts, cnt: (i,), memory_space=pltpu.SMEM),
                  pl.BlockSpec((tm,), lambda i, starts, cnt: (i,), memory_space=pltpu.SMEM),
                  pl.BlockSpec((tm * ROW_CHUNKS, LANES), lambda i, starts, cnt: (i, 0))],
        out_specs=pl.BlockSpec(memory_space=pl.ANY),
        scratch_shapes=[pltpu.VMEM((2, tm * ROW_CHUNKS, LANES), hn.dtype),
                        pltpu.SemaphoreType.DMA((2,)), pltpu.SemaphoreType.DMA(())],
    )
    return pl.pallas_call(
        functools.partial(_scatter_kernel, n_steps=n_steps),
        grid_spec=grid_spec,
        out_shape=jax.ShapeDtypeStruct((n_rows * ROW_CHUNKS, LANES), hn.dtype),
        compiler_params=_cparams(("arbitrary",)),
        name="scatter_rows",
    )(starts, cnt, pos[0], pos[1], hn)


def _experts_kernel(te_ref, tb_ref, tr_ref, xs_ref, wg_ref, wu_ref, wd_ref, y_ref, wg_bf, wu_bf, wd_bf, stage_ref):
    del tb_ref
    i = pl.program_id(0)
    rows = tr_ref[i]
    new_expert = jnp.logical_or(i == 0, te_ref[i] != te_ref[jnp.maximum(i - 1, 0)])

    @pl.when(jnp.logical_and(rows > 0, new_expert))
    def _():
        wg_bf[...] = wg_ref[0].astype(BF16)
        wu_bf[...] = wu_ref[0].astype(BF16)
        wd_bf[...] = wd_ref[0].astype(BF16)

    half = TILE_E // 2
    for part in range(2):
        @pl.when(rows > part * half)
        def _(part=part):
            x = _load_row_tiles(xs_ref, stage_ref, half, part * half).astype(BF16)
            g = jnp.dot(x, wg_bf[...], preferred_element_type=F32)
            up = jnp.dot(x, wu_bf[...], preferred_element_type=F32)
            hmid = (g * jax.nn.sigmoid(g) * up).astype(BF16)
            _store_row_tiles(y_ref, stage_ref, jnp.dot(hmid, wd_bf[...], preferred_element_type=F32), part * half)

    @pl.when(jnp.logical_and(rows > 0, rows <= half))
    def _():
        y_ref[half * ROW_CHUNKS:, :] = jnp.zeros((half * ROW_CHUNKS, LANES), y_ref.dtype)


def _experts(tile_expert, tile_block, tile_rows, xs, w_gate, w_up, w_down):
    n_tiles = tile_expert.shape[0]
    D = D_MODEL
    rows_spec = pl.BlockSpec((TILE_E * ROW_CHUNKS, LANES), lambda i, te, tb, tr: (tb[i], 0))
    weights = lambda shape: pl.BlockSpec(shape, lambda i, te, tb, tr: (te[i], 0, 0))
    grid_spec = pltpu.PrefetchScalarGridSpec(
        num_scalar_prefetch=3,
        grid=(n_tiles,),
        in_specs=[rows_spec, weights((1, D, D_EXPERT)), weights((1, D, D_EXPERT)), weights((1, D_EXPERT, D))],
        out_specs=rows_spec,
        scratch_shapes=[pltpu.VMEM((D, D_EXPERT), BF16), pltpu.VMEM((D, D_EXPERT), BF16),
                        pltpu.VMEM((D_EXPERT, D), BF16), pltpu.VMEM((TILE_E * ROW_CHUNKS, LANES), F32)],
    )
    return pl.pallas_call(
        _experts_kernel,
        grid_spec=grid_spec,
        out_shape=jax.ShapeDtypeStruct(xs.shape, xs.dtype),
        input_output_aliases={3: 0},
        compiler_params=_cparams(("arbitrary",)),
        name="experts",
    )(tile_expert, tile_block, tile_rows, xs, w_gate, w_up, w_down)


def _combine_kernel(pos0_ref, pos1_ref, next0_ref, next1_ref, h_ref, gate_ref, y_ref, o_ref, ybuf, sems, stage_ref,
                    *, n_steps):
    i = pl.program_id(0)
    tm = h_ref.shape[0]
    slot = lax.rem(i, 2)

    def gather(p_refs, s):
        def start(r, _):
            for k, p_ref in enumerate(p_refs):
                _row_tile_copy(y_ref, p_ref[r], ybuf.at[s, k], r, sems.at[s]).start(priority=k)
            return 0

        lax.fori_loop(0, tm, start, 0, unroll=DMA_UNROLL)

    @pl.when(i == 0)
    def _():
        gather((pos0_ref, pos1_ref), slot)

    @pl.when(i + 1 < n_steps)
    def _():
        gather((next0_ref, next1_ref), 1 - slot)

    for k in range(2):
        pltpu.make_async_copy(y_ref.at[pl.ds(0, tm * ROW_CHUNKS), :], ybuf.at[slot, k], sems.at[slot]).wait()
    g = gate_ref[...]
    acc = h_ref[...] + g[:, 0:1] * _load_row_tiles(ybuf.at[slot, 0], stage_ref, tm)
    o_ref[...] = acc + g[:, 1:2] * _load_row_tiles(ybuf.at[slot, 1], stage_ref, tm)


def _combine(pos, h, gate_t, y):
    T, D = h.shape
    tm = TM_ROWS
    n_steps = T // tm
    return pl.pallas_call(
        functools.partial(_combine_kernel, n_steps=n_steps),
        grid=(n_steps,),
        in_specs=[pl.BlockSpec((tm,), lambda i: (i,), memory_space=pltpu.SMEM),
                  pl.BlockSpec((tm,), lambda i: (i,), memory_space=pltpu.SMEM),
                  pl.BlockSpec((tm,), lambda i: (jnp.minimum(i + 1, n_steps - 1),), memory_space=pltpu.SMEM),
                  pl.BlockSpec((tm,), lambda i: (jnp.minimum(i + 1, n_steps - 1),), memory_space=pltpu.SMEM),
                  pl.BlockSpec((tm, D), lambda i: (i, 0)),
                  pl.BlockSpec((tm, 2), lambda i: (i, 0)),
                  pl.BlockSpec(memory_space=pl.ANY)],
        out_specs=pl.BlockSpec((tm, D), lambda i: (i, 0)),
        out_shape=jax.ShapeDtypeStruct((T, D), F32),
        scratch_shapes=[pltpu.VMEM((2, 2, tm * ROW_CHUNKS, LANES), y.dtype), pltpu.SemaphoreType.DMA((2,)),
                        pltpu.VMEM((tm * ROW_CHUNKS, LANES), F32)],
        compiler_params=_cparams(("arbitrary",)),
        name="combine",
    )(pos[0], pos[1], pos[0], pos[1], h, gate_t, y)


def _rope_tables(S):
    inv_freq = 1.0 / (ROPE_THETA ** (jnp.arange(0, HEAD_DIM, 2, dtype=F32) / HEAD_DIM))
    ang = jnp.arange(S, dtype=F32)[:, None] * inv_freq[None, :]
    widen = lambda t: jnp.pad(t, ((0, 0), (0, LANES - HEAD_DIM // 2)))
    return widen(jnp.cos(ang)), widen(jnp.sin(ang))


def _layer(x, norm_mix, w_in, b_forget, q_norm_fox, k_norm_fox, q_norm_dil, k_norm_dil,
           out_norm_fox, out_norm_dil, w_out, norm_ffn, w_router_group, b_router_group,
           w_router_expert, b_router_expert, w_gate, w_up, w_down):
    B, S, D = x.shape
    T = B * S
    n_main = 6 * D_GRP

    w_main = w_in.astype(BF16)
    w_vt = w_main[:, 2 * D_GRP:3 * D_GRP].T
    w_f = jnp.pad(w_main[:, n_main:], ((0, 0), (0, LANES - N_HEADS)))
    b_f = jnp.pad(b_forget, (0, LANES - N_HEADS))[None, :]
    per_head = lambda g: jnp.tile(g, N_HEADS)[None, :]
    bd = jnp.kron(jnp.eye(N_HEADS, dtype=F32), jnp.ones((HEAD_DIM, HEAD_DIM), F32)).astype(BF16)
    cos_t, sin_t = _rope_tables(S)
    tri = jnp.tril(jnp.ones((TM_IN, TM_IN), F32)).astype(BF16)
    upper = jnp.triu(jnp.ones((TM_OUT, TM_OUT), F32), k=1).astype(BF16)
    w_r = jnp.concatenate([
        jnp.pad(w_router_group.T, ((0, 8 - N_GROUPS), (0, 0))),
        w_router_expert.transpose(0, 2, 1).reshape(N_EXPERTS, D)], axis=0)
    b_r = jnp.concatenate([jnp.pad(b_router_group, (0, 8 - N_GROUPS)), b_router_expert.reshape(-1)])[:, None]

    bound = (HEAD_DIM / math.sqrt(HEAD_DIM)) * LOG2E * jnp.max(jnp.abs(q_norm_fox)) * jnp.max(jnp.abs(k_norm_fox))
    shift = FOX_BOUND_SLACK * bound + 1.0
    qa, ka, va, qb, kb, vb = _inproj(
        x, norm_mix[None, :], w_main, w_vt, w_f, b_f, jnp.full((1, LANES), shift, F32),
        per_head(q_norm_fox), per_head(k_norm_fox),
        per_head(q_norm_dil), per_head(k_norm_dil), bd, cos_t, sin_t, tri)
    oa = lax.cond(2.0 * shift <= FOX_SAFE_SPAN,
                  lambda: _fox(qa, ka, va, online=False), lambda: _fox(qa, ka, va, online=True))
    dil_shift = (FOX_BOUND_SLACK * (HEAD_DIM / math.sqrt(HEAD_DIM))
                 * jnp.max(jnp.abs(q_norm_dil)) * jnp.max(jnp.abs(k_norm_dil)) + 1.0)
    dil_shift_arr = jnp.full((1, 1), dil_shift, F32)
    ob = lax.cond(2.0 * dil_shift <= DIL_SAFE_SPAN,
                  lambda: _dilated(dil_shift_arr, qb, kb, vb, bounded=True),
                  lambda: _dilated(dil_shift_arr, qb, kb, vb, bounded=False))

    h, hn, eid, gate, rank, cnt = _outproj(
        oa.reshape(T, D_GRP), ob.reshape(T, D_GRP), x.reshape(T, D), out_norm_fox[None, :],
        out_norm_dil[None, :], w_out.astype(BF16), norm_ffn[None, :], w_r, b_r, upper)

    counts = cnt[:, 0]
    padded = ((counts + TILE_E - 1) // TILE_E) * TILE_E
    ends = jnp.cumsum(padded)
    starts = ends - padded
    is_expert = eid[:, :, None] == jnp.arange(N_EXPERTS, dtype=I32)
    pos = jnp.sum(jnp.where(is_expert, starts, 0), axis=-1) + rank
    n_tiles = (2 * T) // TILE_E + N_EXPERTS
    tile_index = jnp.arange(n_tiles, dtype=I32)
    tile_valid = tile_index * TILE_E < ends[-1]
    tile_block = jnp.minimum(tile_index, ends[-1] // TILE_E - 1)
    in_region = ends[None, :] <= (tile_block * TILE_E)[:, None]
    tile_expert = jnp.sum(in_region.astype(I32), axis=1)
    expert_of_tile = tile_expert[:, None] == jnp.arange(N_EXPERTS, dtype=I32)
    rows_before = tile_block * TILE_E - jnp.sum(jnp.where(expert_of_tile, starts, 0), axis=1)
    tile_rows = jnp.clip(jnp.sum(jnp.where(expert_of_tile, counts, 0), axis=1) - rows_before, 0, TILE_E)
    tile_rows = jnp.where(tile_valid, tile_rows, 0).astype(I32)

    xs = _scatter(starts, counts, pos, hn, n_tiles * TILE_E)
    y = _experts(tile_expert, tile_block, tile_rows, xs, w_gate, w_up, w_down)
    out = _combine(pos, h, gate.T, y)
    return out.reshape(B, S, D)


def kernel(x, norm_mix, w_in, b_forget, q_norm_fox, k_norm_fox, q_norm_dil, k_norm_dil, out_norm_fox,
           out_norm_dil, w_out, norm_ffn, w_router_group, b_router_group, w_router_expert,
           b_router_expert, w_gate, w_up, w_down):
    h = x
    for l in range(norm_mix.shape[0]):
        h = _layer(h, norm_mix[l], w_in[l], b_forget[l], q_norm_fox[l], k_norm_fox[l], q_norm_dil[l],
                   k_norm_dil[l], out_norm_fox[l], out_norm_dil[l], w_out[l], norm_ffn[l],
                   w_router_group[l], b_router_group[l], w_router_expert[l], b_router_expert[l],
                   w_gate[l], w_up[l], w_down[l])
    return h
```

```python
import functools
import math

import jax
import jax.numpy as jnp
from jax import lax
from jax.experimental import pallas as pl
from jax.experimental.pallas import tpu as pltpu

F32 = jnp.float32
BF16 = jnp.bfloat16
I32 = jnp.int32

D_MODEL = 1024
HEAD_DIM = 64
N_HEADS = 8
D_GRP = N_HEADS * HEAD_DIM
LANES = 128
HEADS_PER_TILE = LANES // HEAD_DIM
N_PAIRS = D_GRP // LANES
DIL_PATTERNS = ((128, 1), (512, 4), (2048, 16))
BLOCK = 128
ROPE_THETA = 10000.0
N_GROUPS = 4
EXPERTS_PER_GROUP = 8
N_EXPERTS = N_GROUPS * EXPERTS_PER_GROUP
D_EXPERT = 512
EPS = 1e-6
NEG = -1e30
LOG2E = 1.4426950408889634

TM_IN = 1024
TQ = 1024
TK_WIDE = 1024
TK_SUB = 256
FOX_AHEAD = 4
FOX_ZERO_BITS = 160.0
FOX_BOUND_SLACK = 1.02
DIL_SAFE_SPAN = 69.0
FOX_SAFE_SPAN = 100.0
DIL_SPAN = 2048
DIL_GROUP = 16
DIL_PRE = 4
TM_OUT = 512
TM_ROWS = 512
TILE_E = 512
ROUTER_ROWS = 8 + N_EXPERTS
ROW_CHUNKS = D_MODEL // LANES
DMA_UNROLL = 8
VMEM_LIMIT = 56 * 1024 * 1024


def _cparams(sem, flags=None):
    return pltpu.CompilerParams(dimension_semantics=sem, vmem_limit_bytes=VMEM_LIMIT, flags=flags)


def _inproj_kernel(x_ref, gmix_ref, w_ref, wvt_ref, wf_ref, bf_ref, shift_ref, gqa_ref, gka_ref, gqb_ref, gkb_ref,
                   bd_ref, cos_ref, sin_ref, tri_ref,
                   qa_ref, ka_ref, va_ref, qb_ref, kb_ref, vb_ref, c_ref, carry_ref):
    @pl.when(pl.program_id(1) == 0)
    def _():
        carry_ref[...] = jnp.zeros_like(carry_ref)

    x = x_ref[0]
    ms = jnp.mean(x * x, axis=-1, keepdims=True)
    xn = (x * lax.rsqrt(ms + EPS) * gmix_ref[...]).astype(BF16)

    def seg(j):
        return jnp.dot(xn, w_ref[:, j * D_GRP:(j + 1) * D_GRP], preferred_element_type=F32)

    def head_norm(y, g_ref, scale):
        ss = jnp.dot((y * y).astype(BF16), bd_ref[...], preferred_element_type=F32) * (1.0 / HEAD_DIM)
        return y * lax.rsqrt(ss + EPS) * (g_ref[...] * scale)

    lane = lax.broadcasted_iota(I32, (x.shape[0], LANES), 1)
    first_half = (lane % HEAD_DIM) < (HEAD_DIM // 2)

    def spread(t):
        t = t + pltpu.roll(t, HEAD_DIM // 2, 1)
        return t + pltpu.roll(t, HEAD_DIM, 1)

    cos = spread(cos_ref[...])
    sin = jnp.where(first_half, -1.0, 1.0) * spread(sin_ref[...])

    def rope(y):
        outs = []
        for j in range(N_PAIRS):
            ys = y[:, j * LANES:(j + 1) * LANES]
            partner = jnp.where(first_half, pltpu.roll(ys, LANES - HEAD_DIM // 2, 1),
                                pltpu.roll(ys, HEAD_DIM // 2, 1))
            outs.append(ys * cos + partner * sin)
        return jnp.concatenate(outs, axis=1)

    scale = 1.0 / math.sqrt(HEAD_DIM)
    va_ref[0] = lax.dot_general(wvt_ref[...], xn, (((1,), (1,)), ((), ())),
                                preferred_element_type=F32).astype(va_ref.dtype)
    qb_ref[0] = rope(head_norm(seg(3), gqb_ref, scale)).astype(qb_ref.dtype)
    kb_ref[0] = rope(head_norm(seg(4), gkb_ref, 1.0)).astype(kb_ref.dtype)
    vb_ref[0] = seg(5).astype(vb_ref.dtype)

    fa = jnp.dot(xn, wf_ref[...], preferred_element_type=F32) + bf_ref[...]
    logf = jnp.minimum(fa, 0.0) - jnp.log1p(jnp.exp(-jnp.abs(fa)))
    hi = logf.astype(BF16)
    mid = (logf - hi.astype(F32)).astype(BF16)
    lo = (logf - hi.astype(F32) - mid.astype(F32)).astype(BF16)
    parts = jnp.dot(tri_ref[...], jnp.concatenate([hi, mid, lo], axis=1), preferred_element_type=F32)
    c = parts[:, :LANES] + parts[:, LANES:2 * LANES] + parts[:, 2 * LANES:] + carry_ref[...]
    carry_ref[...] = c[c.shape[0] - 1:, :]

    qa = head_norm(seg(0), gqa_ref, scale * LOG2E)
    ka = head_norm(seg(1), gka_ref, 1.0)
    c2 = c * LOG2E
    c_ref[0] = c2[:, :N_HEADS]
    ones = (jnp.where((lane >= HEAD_DIM + 3) & (lane < HEAD_DIM + 6), 1.0, 0.0)
            - jnp.where(lane == HEAD_DIM + 6, shift_ref[...], 0.0))
    for h in range(N_HEADS):
        cb = jnp.broadcast_to(c2[:, h:h + 1], (x.shape[0], LANES))
        hi = cb.astype(BF16).astype(F32)
        mid = (cb - hi).astype(BF16).astype(F32)
        lo = cb - hi - mid
        pieces = jnp.where(lane == HEAD_DIM, hi, jnp.where(lane == HEAD_DIM + 1, mid,
                           jnp.where(lane == HEAD_DIM + 2, lo, 0.0)))
        q_extra = pieces + ones
        k_extra = jnp.where(((lane >= HEAD_DIM) & (lane < HEAD_DIM + 3)) | (lane == HEAD_DIM + 6), 1.0, 0.0) \
            - pltpu.roll(pieces, 3, 1)
        j, odd = divmod(h, HEADS_PER_TILE)
        qp = qa[:, j * LANES:(j + 1) * LANES]
        kp = ka[:, j * LANES:(j + 1) * LANES]
        if odd:
            qp = pltpu.roll(qp, HEAD_DIM, 1)
            kp = pltpu.roll(kp, HEAD_DIM, 1)
        qa_ref[0, h] = jnp.where(lane < HEAD_DIM, qp, q_extra).astype(qa_ref.dtype)
        ka_ref[0, h] = jnp.where(lane < HEAD_DIM, kp, k_extra).astype(ka_ref.dtype)


def _inproj(x, gmix, w_main, w_vt, w_f, b_f, shift, gqa, gka, gqb, gkb, bd, cos_t, sin_t, tri):
    B, S, D = x.shape
    tm = TM_IN
    const = lambda shape: pl.BlockSpec(shape, lambda b, i: (0,) * len(shape))
    tok = lambda w, dt: jax.ShapeDtypeStruct((B, S, w), dt)
    tok_spec = lambda w: pl.BlockSpec((1, tm, w), lambda b, i: (b, i, 0))
    head_spec = pl.BlockSpec((1, N_HEADS, tm, LANES), lambda b, i: (b, 0, i, 0))
    head_shape = jax.ShapeDtypeStruct((B, N_HEADS, S, LANES), BF16)
    return pl.pallas_call(
        _inproj_kernel,
        grid=(B, S // tm),
        in_specs=[tok_spec(D), const((1, D)), const(w_main.shape), const(w_vt.shape), const(w_f.shape),
                  const((1, LANES)), const((1, LANES)),
                  const((1, D_GRP)), const((1, D_GRP)), const((1, D_GRP)), const((1, D_GRP)),
                  const((D_GRP, D_GRP)),
                  pl.BlockSpec((tm, LANES), lambda b, i: (i, 0)),
                  pl.BlockSpec((tm, LANES), lambda b, i: (i, 0)),
                  const((tm, tm))],
        out_specs=[head_spec, head_spec, pl.BlockSpec((1, D_GRP, tm), lambda b, i: (b, 0, i))]
        + [tok_spec(D_GRP)] * 3 + [tok_spec(N_HEADS)],
        out_shape=[head_shape, head_shape, jax.ShapeDtypeStruct((B, D_GRP, S), BF16),
                   tok(D_GRP, F32), tok(D_GRP, F32), tok(D_GRP, F32), tok(N_HEADS, F32)],
        scratch_shapes=[pltpu.VMEM((1, LANES), F32)],
        compiler_params=_cparams(("arbitrary", "arbitrary")),
        name="inproj",
    )(x, gmix, w_main, w_vt, w_f, b_f, shift, gqa, gka, gqb, gkb, bd, cos_t, sin_t, tri)


def _fox_kernel(first_ref, q_ref, k_ref, v_ref, o_ref, *, online):
    qi = pl.program_id(2)
    tq = q_ref.shape[2]

    def step(start, width, carry, diag):
        carry = list(carry)
        sub = min(TK_SUB, width)
        chunks = [(c, j) for c in range(width // sub) for j in range(HEADS_PER_TILE)]
        def first_query(c):
            return c * sub if diag else 0

        def score(c, j):
            k = k_ref[0, j, pl.ds(start + c * sub, sub), :]
            q = q_ref[0, j, first_query(c):, :]
            return lax.dot_general(k, q, (((1,), (1,)), ((), ())), preferred_element_type=F32)

        def tail(full, lo, new):
            return new if lo == 0 else jnp.concatenate([full[:, :lo], new], axis=1)

        scores = {cj: score(*cj) for cj in chunks[:FOX_AHEAD]}
        for n, (c, j) in enumerate(chunks):
                if n + FOX_AHEAD < len(chunks):
                    nxt = chunks[n + FOX_AHEAD]
                    scores[nxt] = score(*nxt)
                m, l, acc = carry[j]
                lo = first_query(c)
                vt = v_ref[0, :, pl.ds(start + c * sub, sub)]
                s = scores.pop((c, j))
                if diag:
                    key = lax.broadcasted_iota(I32, s.shape, 0)
                    qry = lax.broadcasted_iota(I32, s.shape, 1)
                    s = jnp.where(key <= qry, s, NEG)
                if online:
                    m_new = jnp.maximum(m[:, lo:], jnp.max(s, axis=0, keepdims=True))
                    alpha = jnp.exp2(m[:, lo:] - m_new)
                    p = jnp.exp2(s - m_new)
                    l_new = alpha * l[:, lo:] + jnp.sum(p, axis=0, keepdims=True)
                    acc_new = alpha * acc[:, lo:] + jnp.dot(vt, p.astype(BF16), preferred_element_type=F32)
                    m = tail(m, lo, m_new)
                else:
                    p = jnp.exp2(s)
                    l_new = l[:, lo:] + jnp.sum(p, axis=0, keepdims=True)
                    acc_new = acc[:, lo:] + jnp.dot(vt, p.astype(BF16), preferred_element_type=F32)
                carry[j] = (m, tail(l, lo, l_new), tail(acc, lo, acc_new))
        return tuple(carry)

    init = tuple((jnp.full((1, tq), NEG, F32), jnp.zeros((1, tq), F32), jnp.zeros((LANES, tq), F32))
                 for _ in range(HEADS_PER_TILE))
    per_wide = TK_WIDE // tq
    n_wide = qi // per_wide
    first = first_ref[(pl.program_id(0) * pl.num_programs(1) + pl.program_id(1)) * pl.num_programs(2) + qi]
    carry = lax.fori_loop(
        first, n_wide, lambda i, c: step(pl.multiple_of(i * TK_WIDE, TK_WIDE), TK_WIDE, c, False), init)
    for extra in range(per_wide - 1):
        carry = lax.cond(qi - n_wide * per_wide > extra,
                         lambda c, e=extra: step(pl.multiple_of((n_wide * per_wide + e) * tq, tq), tq, c, False),
                         lambda c: c, carry)
    carry = step(pl.multiple_of(qi * tq, tq), tq, carry, True)
    outs = [acc / l for (_, l, acc) in carry]
    feat = lax.broadcasted_iota(I32, (LANES, tq), 0)
    o_ref[0] = jnp.where(feat < HEAD_DIM, outs[0], outs[1]).T.astype(o_ref.dtype)


def _fox(first_tile, qa, ka, va_t, online):
    B, _, S, _ = qa.shape
    grid_spec = pltpu.PrefetchScalarGridSpec(
        num_scalar_prefetch=1,
        grid=(B, N_PAIRS, S // TQ),
        in_specs=[pl.BlockSpec((1, HEADS_PER_TILE, TQ, LANES), lambda b, hp, i, first: (b, hp, i, 0)),
                  pl.BlockSpec((1, HEADS_PER_TILE, S, LANES), lambda b, hp, i, first: (b, hp, 0, 0)),
                  pl.BlockSpec((1, LANES, S), lambda b, hp, i, first: (b, hp, 0))],
        out_specs=pl.BlockSpec((1, TQ, LANES), lambda b, hp, i, first: (b, i, hp)),
    )
    return pl.pallas_call(
        functools.partial(_fox_kernel, online=online),
        grid_spec=grid_spec,
        out_shape=jax.ShapeDtypeStruct((B, S, D_GRP), F32),
        compiler_params=_cparams(("arbitrary", "arbitrary", "arbitrary")),
        name="fox_online" if online else "fox",
    )(first_tile, qa, ka, va_t)


def _dilated_kernel(shift_ref, q_ref, kp_ref, kc_ref, vp_ref, vc_ref, o_ref, qq, kk, vv, qq4, kk4, vv4, osc, lsc,
                    *, bounded):
    u = pl.program_id(1)
    span = q_ref.shape[1]
    qq[...] = q_ref[0]
    kk[0:span, :] = kp_ref[0]
    kk[span:2 * span, :] = kc_ref[0]
    vv[0:span, :] = vp_ref[0]
    vv[span:2 * span, :] = vc_ref[0]
    for src, dst in ((qq, qq4), (kk, kk4), (vv, vv4)):
        part = src.shape[0] // DIL_PRE
        for a in range(DIL_PRE):
            dst[a * part:(a + 1) * part, :] = src[pl.ds(a, part, stride=DIL_PRE), :]

    def rows(buf, buf4, start, n, d):
        if d % DIL_PRE:
            return buf[pl.ds(start, n, stride=d), :]
        part = buf4.shape[0] // DIL_PRE
        a = lax.rem(start, DIL_PRE)
        return buf4[pl.ds(a * part + lax.div(start, DIL_PRE), n, stride=d // DIL_PRE), :]

    lane = lax.broadcasted_iota(I32, (BLOCK, LANES), 1)
    ql = lax.broadcasted_iota(I32, (BLOCK, 2 * BLOCK), 0)
    kl = lax.broadcasted_iota(I32, (BLOCK, 2 * BLOCK), 1)
    dist = ql + BLOCK - kl
    band = (dist >= 0) & (dist <= BLOCK)
    live = -shift_ref[0, 0] if bounded else 0.0
    bias = jnp.where(band, live, NEG)
    bias_first = jnp.where(band & (kl >= BLOCK), live, NEG)

    def scores(q_start, k_start, d, first):
        qs = rows(qq, qq4, q_start, BLOCK, d).astype(BF16)
        ks = rows(kk, kk4, k_start, 2 * BLOCK, d).astype(BF16)
        mask = jnp.where(first, bias_first, bias)
        out = []
        for j in range(HEADS_PER_TILE):
            qj = jnp.where(lane // HEAD_DIM == j, qs, jnp.zeros_like(qs))
            out.append(lax.dot_general(qj, ks, (((1,), (1,)), ((), ())), preferred_element_type=F32) + mask)
        return out

    def finish(s_heads, k_start, d):
        vs = rows(vv, vv4, k_start, 2 * BLOCK, d).astype(BF16)
        o_heads, lse_heads = [], []
        for s in s_heads:
            if bounded:
                p = jnp.exp(s)
                o_heads.append(jnp.dot(p.astype(BF16), vs, preferred_element_type=F32))
                lse_heads.append(jnp.sum(p, axis=-1, keepdims=True))
            else:
                m = jnp.max(s, axis=-1, keepdims=True)
                p = jnp.exp(s - m)
                l = jnp.sum(p, axis=-1, keepdims=True)
                o_heads.append(jnp.dot((p / l).astype(BF16), vs, preferred_element_type=F32))
                lse_heads.append(m + jnp.log(l))
        o = jnp.where(lane < HEAD_DIM, o_heads[0], o_heads[1])
        lse = jnp.where(lane < HEAD_DIM, lse_heads[0], lse_heads[1])
        return o, lse

    for pidx, (window, d) in enumerate(DIL_PATTERNS):
        assert window // d == BLOCK
        unit = d * BLOCK
        n_problems = (span // unit) * d
        assert n_problems % DIL_GROUP == 0

        def body(g, _, pidx=pidx, d=d, unit=unit):
            starts, s_all = [], []
            for t in range(DIL_GROUP):
                idx = g * DIL_GROUP + t
                w = idx // d
                q_start = w * unit + (idx - w * d)
                k_start = span - unit + q_start
                starts.append((q_start, k_start))
                s_all.append(scores(q_start, k_start, d, jnp.logical_and(u == 0, w == 0)))
            for (q_start, k_start), s_heads in zip(starts, s_all):
                o, lse = finish(s_heads, k_start, d)
                osc[pidx, pl.ds(q_start, BLOCK, stride=d), :] = o
                lsc[pidx, pl.ds(q_start, BLOCK, stride=d), :] = lse
            return 0

        lax.fori_loop(0, n_problems // DIL_GROUP, body, 0)

    if bounded:
        num = osc[0] + osc[1] + osc[2]
        den = lsc[0] + lsc[1] + lsc[2]
    else:
        mx = jnp.maximum(jnp.maximum(lsc[0], lsc[1]), lsc[2])
        num = jnp.zeros((span, LANES), F32)
        den = jnp.zeros((span, LANES), F32)
        for pidx in range(len(DIL_PATTERNS)):
            e = jnp.exp(lsc[pidx] - mx)
            num = num + e * osc[pidx]
            den = den + e
    o_ref[0] = (num / den).astype(o_ref.dtype)


def _dilated(shift, qb, kb, vb, bounded):
    B, S, _ = qb.shape
    span = DIL_SPAN
    cur = pl.BlockSpec((1, span, LANES), lambda b, u, hp: (b, u, hp))
    prev = pl.BlockSpec((1, span, LANES), lambda b, u, hp: (b, jnp.maximum(u - 1, 0), hp))
    return pl.pallas_call(
        functools.partial(_dilated_kernel, bounded=bounded),
        grid=(B, S // span, N_PAIRS),
        in_specs=[pl.BlockSpec(memory_space=pltpu.SMEM), cur, prev, cur, prev, cur],
        out_specs=cur,
        out_shape=jax.ShapeDtypeStruct((B, S, D_GRP), F32),
        scratch_shapes=[pltpu.VMEM((span, LANES), F32),
                        pltpu.VMEM((2 * span, LANES), F32), pltpu.VMEM((2 * span, LANES), F32),
                        pltpu.VMEM((span, LANES), F32),
                        pltpu.VMEM((2 * span, LANES), F32), pltpu.VMEM((2 * span, LANES), F32),
                        pltpu.VMEM((len(DIL_PATTERNS), span, LANES), F32),
                        pltpu.VMEM((len(DIL_PATTERNS), span, LANES), F32)],
        compiler_params=_cparams(("arbitrary", "arbitrary", "arbitrary")),
        name="dilated" if bounded else "dilated_exact",
    )(shift, qb, kb, kb, vb, vb)


def _store_row_tiles(ref, stage_ref, x, first_row=0):
    n = x.shape[0]
    for c in range(ROW_CHUNKS):
        stage_ref[pl.ds(first_row * ROW_CHUNKS + c, n, stride=ROW_CHUNKS), :] = x[:, c * LANES:(c + 1) * LANES]
    rows = slice(first_row * ROW_CHUNKS, (first_row + n) * ROW_CHUNKS)
    ref[rows, :] = stage_ref[rows, :].astype(ref.dtype)


def _load_row_tiles(ref, stage_ref, n, first_row=0):
    rows = slice(first_row * ROW_CHUNKS, (first_row + n) * ROW_CHUNKS)
    stage_ref[rows, :] = ref[rows, :].astype(F32)
    return jnp.concatenate([stage_ref[pl.ds(first_row * ROW_CHUNKS + c, n, stride=ROW_CHUNKS), :]
                            for c in range(ROW_CHUNKS)], axis=1)


def _row_tile_copy(src_ref, src_row, dst_ref, dst_row, sem):
    src = src_ref.at[pl.ds(pl.multiple_of(src_row * ROW_CHUNKS, ROW_CHUNKS), ROW_CHUNKS), :]
    dst = dst_ref.at[pl.ds(pl.multiple_of(dst_row * ROW_CHUNKS, ROW_CHUNKS), ROW_CHUNKS), :]
    return pltpu.make_async_copy(src, dst, sem)


def _outproj_kernel(oa_ref, ob_ref, x_ref, gfox_ref, gdil_ref, wo_ref, gffn_ref, wr_ref, br_ref, upper_ref,
                    h_ref, hn_ref, eid_ref, gate_ref, rank_ref, cnt_ref, run_ref, stage_ref):
    @pl.when(pl.program_id(0) == 0)
    def _():
        run_ref[...] = jnp.zeros_like(run_ref)

    def norm(y, g):
        ms = jnp.mean(y * y, axis=-1, keepdims=True)
        return y * lax.rsqrt(ms + EPS) * g

    a = norm(oa_ref[...], gfox_ref[...]).astype(BF16)
    b = norm(ob_ref[...], gdil_ref[...]).astype(BF16)
    mix = (jnp.dot(a, wo_ref[0:D_GRP, :], preferred_element_type=F32)
           + jnp.dot(b, wo_ref[D_GRP:2 * D_GRP, :], preferred_element_type=F32))
    h = x_ref[...] + mix
    h_ref[...] = h
    hn = norm(h, gffn_ref[...])
    _store_row_tiles(hn_ref, stage_ref, hn)

    z = lax.dot_general(wr_ref[...], hn, (((1,), (1,)), ((), ())), preferred_element_type=F32,
                        precision=lax.Precision.HIGHEST) + br_ref[...]
    tm = z.shape[1]
    best = z[0:1, :]
    g_sel = jnp.zeros((1, tm), I32)
    for g in range(1, N_GROUPS):
        better = z[g:g + 1, :] > best
        g_sel = jnp.where(better, g, g_sel)
        best = jnp.maximum(best, z[g:g + 1, :])
    den = jnp.zeros((1, tm), F32)
    for g in range(N_GROUPS):
        den = den + jnp.exp(z[g:g + 1, :] - best)
    pg_top = 1.0 / den

    ze = jnp.zeros((EXPERTS_PER_GROUP, tm), F32)
    for g in range(N_GROUPS):
        ze = jnp.where(g_sel == g, z[8 + g * EXPERTS_PER_GROUP:8 + (g + 1) * EXPERTS_PER_GROUP, :], ze)
    e_iota = lax.broadcasted_iota(I32, ze.shape, 0)
    v1 = jnp.max(ze, axis=0, keepdims=True)
    i1 = jnp.min(jnp.where(ze == v1, e_iota, EXPERTS_PER_GROUP), axis=0, keepdims=True)
    ze2 = jnp.where(e_iota == i1, -jnp.inf, ze)
    v2 = jnp.max(ze2, axis=0, keepdims=True)
    i2 = jnp.min(jnp.where(ze2 == v2, e_iota, EXPERTS_PER_GROUP), axis=0, keepdims=True)
    e2 = jnp.exp(v2 - v1)
    inv = 1.0 / (1.0 + e2)
    gate1 = inv * pg_top
    gate2 = e2 * inv * pg_top
    eid1 = g_sel * EXPERTS_PER_GROUP + i1
    eid2 = g_sel * EXPERTS_PER_GROUP + i2

    x_iota = lax.broadcasted_iota(I32, (N_EXPERTS, tm), 0)
    hot1 = x_iota == eid1
    hot2 = x_iota == eid2
    multi = jnp.logical_or(hot1, hot2)
    before = jnp.dot(multi.astype(BF16), upper_ref[...], preferred_element_type=F32)
    slot = before + run_ref[:, 0:1]
    rank1 = jnp.sum(jnp.where(hot1, slot, 0.0), axis=0, keepdims=True)
    rank2 = jnp.sum(jnp.where(hot2, slot, 0.0), axis=0, keepdims=True)
    run_ref[...] = run_ref[...] + jnp.sum(multi.astype(F32), axis=1, keepdims=True)

    eid_ref[...] = jnp.concatenate([eid1, eid2], axis=0)
    gate_ref[...] = jnp.concatenate([gate1, gate2], axis=0)
    rank_ref[...] = jnp.concatenate([rank1, rank2], axis=0).astype(I32)
    cnt_ref[...] = run_ref[...].astype(I32)


def _outproj(oa, ob, x2, gfox, gdil, w_out, gffn, w_r, b_r, upper):
    T, D = x2.shape
    tm = TM_OUT
    const = lambda shape: pl.BlockSpec(shape, lambda i: (0,) * len(shape))
    tok = lambda w: pl.BlockSpec((tm, w), lambda i: (i, 0))
    lanes2 = pl.BlockSpec((2, tm), lambda i: (0, i))
    return pl.pallas_call(
        _outproj_kernel,
        grid=(T // tm,),
        in_specs=[tok(D_GRP), tok(D_GRP), tok(D), const((1, D_GRP)), const((1, D_GRP)), const((D, D)),
                  const((1, D)), const((ROUTER_ROWS, D)), const((ROUTER_ROWS, 1)), const((tm, tm))],
        out_specs=[tok(D), pl.BlockSpec((tm * ROW_CHUNKS, LANES), lambda i: (i, 0)),
                   lanes2, lanes2, lanes2, const((N_EXPERTS, LANES))],
        out_shape=[jax.ShapeDtypeStruct((T, D), F32), jax.ShapeDtypeStruct((T * ROW_CHUNKS, LANES), BF16),
                   jax.ShapeDtypeStruct((2, T), I32), jax.ShapeDtypeStruct((2, T), F32),
                   jax.ShapeDtypeStruct((2, T), I32), jax.ShapeDtypeStruct((N_EXPERTS, LANES), I32)],
        scratch_shapes=[pltpu.VMEM((N_EXPERTS, LANES), F32), pltpu.VMEM((tm * ROW_CHUNKS, LANES), F32)],
        compiler_params=_cparams(("arbitrary",)),
        name="outproj",
    )(oa, ob, x2, gfox, gdil, w_out, gffn, w_r, b_r, upper)


def _scatter_kernel(starts_ref, cnt_ref, pos0_ref, pos1_ref, hn_ref, xs_ref, ring, sems, zero_sem, *, n_steps):
    i = pl.program_id(0)
    tm = hn_ref.shape[0] // ROW_CHUNKS
    slot = lax.rem(i, 2)

    def wait_slot(s):
        for _ in range(2):
            pltpu.make_async_copy(ring.at[s], xs_ref.at[pl.ds(0, tm * ROW_CHUNKS), :], sems.at[s]).wait()

    @pl.when(i >= 2)
    def _():
        wait_slot(slot)

    ring[slot] = hn_ref[...]

    def start(r, _):
        for k, pos_ref in enumerate((pos0_ref, pos1_ref)):
            _row_tile_copy(ring.at[slot], r, xs_ref, pos_ref[r], sems.at[slot]).start(priority=k)
        return 0

    lax.fori_loop(0, tm, start, 0, unroll=DMA_UNROLL)

    @pl.when(i == n_steps - 1)
    def _():
        wait_slot(slot)
        if n_steps > 1:
            wait_slot(1 - slot)
        ring[0] = jnp.zeros((tm * ROW_CHUNKS, LANES), ring.dtype)

        def pad_expert(e, _, wait):
            n_pad = lax.rem(TILE_E - lax.rem(cnt_ref[e], TILE_E), TILE_E)
            first = starts_ref[e] + cnt_ref[e]
            size = TILE_E // 2
            while size >= 1:
                row0 = first + (n_pad & ~(2 * size - 1))

                @pl.when((n_pad & size) != 0)
                def _(size=size, row0=row0):
                    copy = pltpu.make_async_copy(
                        ring.at[0, pl.ds(0, size * ROW_CHUNKS), :],
                        xs_ref.at[pl.ds(pl.multiple_of(row0 * ROW_CHUNKS, ROW_CHUNKS), size * ROW_CHUNKS), :],
                        zero_sem)
                    copy.wait() if wait else copy.start()

                size //= 2
            return 0

        lax.fori_loop(0, N_EXPERTS, functools.partial(pad_expert, wait=False), 0)
        last = N_EXPERTS - 1
        used_rows = starts_ref[last] + cnt_ref[last] + lax.rem(TILE_E - lax.rem(cnt_ref[last], TILE_E), TILE_E)
        n_tail = xs_ref.shape[0] // (tm * ROW_CHUNKS) - used_rows // tm

        def tail_copy(t):
            row0 = pl.multiple_of((used_rows + t * tm) * ROW_CHUNKS, tm * ROW_CHUNKS)
            return pltpu.make_async_copy(ring.at[0], xs_ref.at[pl.ds(row0, tm * ROW_CHUNKS), :], zero_sem)

        lax.fori_loop(0, n_tail, lambda t, c: (tail_copy(t).start(), c)[1], 0)
        lax.fori_loop(0, N_EXPERTS, functools.partial(pad_expert, wait=True), 0)
        lax.fori_loop(0, n_tail, lambda t, c: (tail_copy(t).wait(), c)[1], 0)


def _scatter(starts, cnt, pos, hn, n_rows):
    T = hn.shape[0] // ROW_CHUNKS
    tm = TM_ROWS
    assert TILE_E % tm == 0 and TILE_E // 2 <= tm
    n_steps = T // tm
    grid_spec = pltpu.PrefetchScalarGridSpec(
        num_scalar_prefetch=2,
        grid=(n_steps,),
        in_specs=[pl.BlockSpec((tm,), lambda i, starts, cnt: (i,), memory_space=pltpu.SMEM),
                  pl.BlockSpec((tm,), lambda i, starts, cnt: (i,), memory_space=pltpu.SMEM),
                  pl.BlockSpec((tm * ROW_CHUNKS, LANES), lambda i, starts, cnt: (i, 0))],
        out_specs=pl.BlockSpec(memory_space=pl.ANY),
        scratch_shapes=[pltpu.VMEM((2, tm * ROW_CHUNKS, LANES), hn.dtype),
                        pltpu.SemaphoreType.DMA((2,)), pltpu.SemaphoreType.DMA(())],
    )
    return pl.pallas_call(
        functools.partial(_scatter_kernel, n_steps=n_steps),
        grid_spec=grid_spec,
        out_shape=jax.ShapeDtypeStruct((n_rows * ROW_CHUNKS, LANES), hn.dtype),
        compiler_params=_cparams(("arbitrary",)),
        name="scatter_rows",
    )(starts, cnt, pos[0], pos[1], hn)


def _experts_kernel(te_ref, tb_ref, tr_ref, xs_ref, wg_ref, wu_ref, wd_ref, y_ref, wg_bf, wu_bf, wd_bf, stage_ref):
    del tb_ref
    i = pl.program_id(0)
    rows = tr_ref[i]
    new_expert = jnp.logical_or(i == 0, te_ref[i] != te_ref[jnp.maximum(i - 1, 0)])

    @pl.when(jnp.logical_and(rows > 0, new_expert))
    def _():
        wg_bf[...] = wg_ref[0].astype(BF16)
        wu_bf[...] = wu_ref[0].astype(BF16)
        wd_bf[...] = wd_ref[0].astype(BF16)

    half = TILE_E // 2
    for part in range(2):
        @pl.when(rows > part * half)
        def _(part=part):
            x = _load_row_tiles(xs_ref, stage_ref, half, part * half).astype(BF16)
            g = jnp.dot(x, wg_bf[...], preferred_element_type=F32)
            up = jnp.dot(x, wu_bf[...], preferred_element_type=F32)
            hmid = (g * jax.nn.sigmoid(g) * up).astype(BF16)
            _store_row_tiles(y_ref, stage_ref, jnp.dot(hmid, wd_bf[...], preferred_element_type=F32), part * half)

    @pl.when(jnp.logical_and(rows > 0, rows <= half))
    def _():
        y_ref[half * ROW_CHUNKS:, :] = jnp.zeros((half * ROW_CHUNKS, LANES), y_ref.dtype)


def _experts(tile_expert, tile_block, tile_rows, xs, w_gate, w_up, w_down):
    n_tiles = tile_expert.shape[0]
    D = D_MODEL
    rows_spec = pl.BlockSpec((TILE_E * ROW_CHUNKS, LANES), lambda i, te, tb, tr: (tb[i], 0))
    weights = lambda shape: pl.BlockSpec(shape, lambda i, te, tb, tr: (te[i], 0, 0))
    grid_spec = pltpu.PrefetchScalarGridSpec(
        num_scalar_prefetch=3,
        grid=(n_tiles,),
        in_specs=[rows_spec, weights((1, D, D_EXPERT)), weights((1, D, D_EXPERT)), weights((1, D_EXPERT, D))],
        out_specs=rows_spec,
        scratch_shapes=[pltpu.VMEM((D, D_EXPERT), BF16), pltpu.VMEM((D, D_EXPERT), BF16),
                        pltpu.VMEM((D_EXPERT, D), BF16), pltpu.VMEM((TILE_E * ROW_CHUNKS, LANES), F32)],
    )
    return pl.pallas_call(
        _experts_kernel,
        grid_spec=grid_spec,
        out_shape=jax.ShapeDtypeStruct(xs.shape, xs.dtype),
        input_output_aliases={3: 0},
        compiler_params=_cparams(("arbitrary",)),
        name="experts",
    )(tile_expert, tile_block, tile_rows, xs, w_gate, w_up, w_down)


def _combine_kernel(pos0_ref, pos1_ref, next0_ref, next1_ref, h_ref, gate_ref, y_ref, o_ref, ybuf, sems, stage_ref,
                    *, n_steps):
    i = pl.program_id(0)
    tm = h_ref.shape[0]
    slot = lax.rem(i, 2)

    def gather(p_refs, s):
        def start(r, _):
            for k, p_ref in enumerate(p_refs):
                _row_tile_copy(y_ref, p_ref[r], ybuf.at[s, k], r, sems.at[s]).start(priority=k)
            return 0

        lax.fori_loop(0, tm, start, 0, unroll=DMA_UNROLL)

    @pl.when(i == 0)
    def _():
        gather((pos0_ref, pos1_ref), slot)

    @pl.when(i + 1 < n_steps)
    def _():
        gather((next0_ref, next1_ref), 1 - slot)

    for k in range(2):
        pltpu.make_async_copy(y_ref.at[pl.ds(0, tm * ROW_CHUNKS), :], ybuf.at[slot, k], sems.at[slot]).wait()
    g = gate_ref[...]
    acc = h_ref[...] + g[:, 0:1] * _load_row_tiles(ybuf.at[slot, 0], stage_ref, tm)
    o_ref[...] = acc + g[:, 1:2] * _load_row_tiles(ybuf.at[slot, 1], stage_ref, tm)


def _combine(pos, h, gate_t, y):
    T, D = h.shape
    tm = TM_ROWS
    n_steps = T // tm
    return pl.pallas_call(
        functools.partial(_combine_kernel, n_steps=n_steps),
        grid=(n_steps,),
        in_specs=[pl.BlockSpec((tm,), lambda i: (i,), memory_space=pltpu.SMEM),
                  pl.BlockSpec((tm,), lambda i: (i,), memory_space=pltpu.SMEM),
                  pl.BlockSpec((tm,), lambda i: (jnp.minimum(i + 1, n_steps - 1),), memory_space=pltpu.SMEM),
                  pl.BlockSpec((tm,), lambda i: (jnp.minimum(i + 1, n_steps - 1),), memory_space=pltpu.SMEM),
                  pl.BlockSpec((tm, D), lambda i: (i, 0)),
                  pl.BlockSpec((tm, 2), lambda i: (i, 0)),
                  pl.BlockSpec(memory_space=pl.ANY)],
        out_specs=pl.BlockSpec((tm, D), lambda i: (i, 0)),
        out_shape=jax.ShapeDtypeStruct((T, D), F32),
        scratch_shapes=[pltpu.VMEM((2, 2, tm * ROW_CHUNKS, LANES), y.dtype), pltpu.SemaphoreType.DMA((2,)),
                        pltpu.VMEM((tm * ROW_CHUNKS, LANES), F32)],
        compiler_params=_cparams(("arbitrary",)),
        name="combine",
    )(pos[0], pos[1], pos[0], pos[1], h, gate_t, y)


def _rope_tables(S):
    inv_freq = 1.0 / (ROPE_THETA ** (jnp.arange(0, HEAD_DIM, 2, dtype=F32) / HEAD_DIM))
    ang = jnp.arange(S, dtype=F32)[:, None] * inv_freq[None, :]
    widen = lambda t: jnp.pad(t, ((0, 0), (0, LANES - HEAD_DIM // 2)))
    return widen(jnp.cos(ang)), widen(jnp.sin(ang))


def _layer(x, norm_mix, w_in, b_forget, q_norm_fox, k_norm_fox, q_norm_dil, k_norm_dil,
           out_norm_fox, out_norm_dil, w_out, norm_ffn, w_router_group, b_router_group,
           w_router_expert, b_router_expert, w_gate, w_up, w_down):
    B, S, D = x.shape
    T = B * S
    n_main = 6 * D_GRP

    w_main = w_in.astype(BF16)
    w_vt = w_main[:, 2 * D_GRP:3 * D_GRP].T
    w_f = jnp.pad(w_main[:, n_main:], ((0, 0), (0, LANES - N_HEADS)))
    b_f = jnp.pad(b_forget, (0, LANES - N_HEADS))[None, :]
    per_head = lambda g: jnp.tile(g, N_HEADS)[None, :]
    bd = jnp.kron(jnp.eye(N_HEADS, dtype=F32), jnp.ones((HEAD_DIM, HEAD_DIM), F32)).astype(BF16)
    cos_t, sin_t = _rope_tables(S)
    tri = jnp.tril(jnp.ones((TM_IN, TM_IN), F32)).astype(BF16)
    upper = jnp.triu(jnp.ones((TM_OUT, TM_OUT), F32), k=1).astype(BF16)
    w_r = jnp.concatenate([
        jnp.pad(w_router_group.T, ((0, 8 - N_GROUPS), (0, 0))),
        w_router_expert.transpose(0, 2, 1).reshape(N_EXPERTS, D)], axis=0)
    b_r = jnp.concatenate([jnp.pad(b_router_group, (0, 8 - N_GROUPS)), b_router_expert.reshape(-1)])[:, None]

    bound = (HEAD_DIM / math.sqrt(HEAD_DIM)) * LOG2E * jnp.max(jnp.abs(q_norm_fox)) * jnp.max(jnp.abs(k_norm_fox))
    shift = FOX_BOUND_SLACK * bound + 1.0
    qa, ka, va, qb, kb, vb, c2 = _inproj(
        x, norm_mix[None, :], w_main, w_vt, w_f, b_f, jnp.full((1, LANES), shift, F32),
        per_head(q_norm_fox), per_head(k_norm_fox),
        per_head(q_norm_dil), per_head(k_norm_dil), bd, cos_t, sin_t, tri)
    n_q, n_k = S // TQ, S // TK_WIDE
    c_first = c2[:, ::TQ, :].reshape(B, n_q, 1, N_PAIRS, HEADS_PER_TILE)
    c_last = c2[:, TK_WIDE - 1::TK_WIDE, :].reshape(B, 1, n_k, N_PAIRS, HEADS_PER_TILE)
    all_zero = jnp.all(c_first - c_last < -FOX_ZERO_BITS, axis=-1)
    leading = jnp.cumprod(all_zero.astype(I32), axis=2)
    before = (jnp.arange(n_k) * TK_WIDE + TK_WIDE <= jnp.arange(n_q)[:, None] * TQ)
    first_tile = jnp.sum(leading * before[None, :, :, None].astype(I32), axis=2)
    first_tile = first_tile.transpose(0, 2, 1).reshape(-1).astype(I32)
    oa = lax.cond(2.0 * shift <= FOX_SAFE_SPAN,
                  lambda: _fox(first_tile, qa, ka, va, online=False),
                  lambda: _fox(jnp.zeros_like(first_tile), qa, ka, va, online=True))
    dil_shift = (FOX_BOUND_SLACK * (HEAD_DIM / math.sqrt(HEAD_DIM))
                 * jnp.max(jnp.abs(q_norm_dil)) * jnp.max(jnp.abs(k_norm_dil)) + 1.0)
    dil_shift_arr = jnp.full((1, 1), dil_shift, F32)
    ob = lax.cond(2.0 * dil_shift <= DIL_SAFE_SPAN,
                  lambda: _dilated(dil_shift_arr, qb, kb, vb, bounded=True),
                  lambda: _dilated(dil_shift_arr, qb, kb, vb, bounded=False))

    h, hn, eid, gate, rank, cnt = _outproj(
        oa.reshape(T, D_GRP), ob.reshape(T, D_GRP), x.reshape(T, D), out_norm_fox[None, :],
        out_norm_dil[None, :], w_out.astype(BF16), norm_ffn[None, :], w_r, b_r, upper)

    counts = cnt[:, 0]
    padded = ((counts + TILE_E - 1) // TILE_E) * TILE_E
    ends = jnp.cumsum(padded)
    starts = ends - padded
    is_expert = eid[:, :, None] == jnp.arange(N_EXPERTS, dtype=I32)
    pos = jnp.sum(jnp.where(is_expert, starts, 0), axis=-1) + rank
    n_tiles = (2 * T) // TILE_E + N_EXPERTS
    tile_index = jnp.arange(n_tiles, dtype=I32)
    tile_valid = tile_index * TILE_E < ends[-1]
    tile_block = jnp.minimum(tile_index, ends[-1] // TILE_E - 1)
    in_region = ends[None, :] <= (tile_block * TILE_E)[:, None]
    tile_expert = jnp.sum(in_region.astype(I32), axis=1)
    expert_of_tile = tile_expert[:, None] == jnp.arange(N_EXPERTS, dtype=I32)
    rows_before = tile_block * TILE_E - jnp.sum(jnp.where(expert_of_tile, starts, 0), axis=1)
    tile_rows = jnp.clip(jnp.sum(jnp.where(expert_of_tile, counts, 0), axis=1) - rows_before, 0, TILE_E)
    tile_rows = jnp.where(tile_valid, tile_rows, 0).astype(I32)

    xs = _scatter(starts, counts, pos, hn, n_tiles * TILE_E)
    y = _experts(tile_expert, tile_block, tile_rows, xs, w_gate, w_up, w_down)
    out = _combine(pos, h, gate.T, y)
    return out.reshape(B, S, D)


def kernel(x, norm_mix, w_in, b_forget, q_norm_fox, k_norm_fox, q_norm_dil, k_norm_dil, out_norm_fox,
           out_norm_dil, w_out, norm_ffn, w_router_group, b_router_group, w_router_expert,
           b_router_expert, w_gate, w_up, w_down):
    h = x
    for l in range(norm_mix.shape[0]):
        h = _layer(h, norm_mix[l], w_in[l], b_forget[l], q_norm_fox[l], k_norm_fox[l], q_norm_dil[l],
                   k_norm_dil[l], out_norm_fox[l], out_norm_dil[l], w_out[l], norm_ffn[l],
                   w_router_group[l], b_router_group[l], w_router_expert[l], b_router_expert[l],
                   w_gate[l], w_up[l], w_down[l])
    return h
```

```python
import functools
import math

import jax
import jax.numpy as jnp
from jax import lax
from jax.experimental import pallas as pl
from jax.experimental.pallas import tpu as pltpu

F32 = jnp.float32
BF16 = jnp.bfloat16
I32 = jnp.int32

D_MODEL = 1024
HEAD_DIM = 64
N_HEADS = 8
D_GRP = N_HEADS * HEAD_DIM
LANES = 128
HEADS_PER_TILE = LANES // HEAD_DIM
N_PAIRS = D_GRP // LANES
DIL_PATTERNS = ((128, 1), (512, 4), (2048, 16))
BLOCK = 128
ROPE_THETA = 10000.0
N_GROUPS = 4
EXPERTS_PER_GROUP = 8
N_EXPERTS = N_GROUPS * EXPERTS_PER_GROUP
D_EXPERT = 512
EPS = 1e-6
NEG = -1e30
LOG2E = 1.4426950408889634

TM_IN = 1024
TQ = 1024
TK_WIDE = 1024
TK_SUB = 256
FOX_AHEAD = 4
FOX_ZERO_BITS = 160.0
FOX_BOUND_SLACK = 1.02
DIL_SAFE_SPAN = 69.0
FOX_SAFE_SPAN = 100.0
DIL_SPAN = 2048
DIL_GROUP = 16
DIL_PRE = 4
TM_OUT = 512
TM_ROWS = 512
TILE_E = 512
ROUTER_ROWS = 8 + N_EXPERTS
ROW_CHUNKS = D_MODEL // LANES
DMA_UNROLL = 8
VMEM_LIMIT = 56 * 1024 * 1024


def _cparams(sem, flags=None):
    return pltpu.CompilerParams(dimension_semantics=sem, vmem_limit_bytes=VMEM_LIMIT, flags=flags)


def _inproj_kernel(x_ref, gmix_ref, w_ref, wvt_ref, wf_ref, bf_ref, shift_ref, gqa_ref, gka_ref, gqb_ref, gkb_ref,
                   bd_ref, cos_ref, sin_ref, tri_ref,
                   qa_ref, ka_ref, va_ref, qb_ref, kb_ref, vb_ref, c_ref, carry_ref):
    @pl.when(pl.program_id(1) == 0)
    def _():
        carry_ref[...] = jnp.zeros_like(carry_ref)

    x = x_ref[0]
    ms = jnp.mean(x * x, axis=-1, keepdims=True)
    xn = (x * lax.rsqrt(ms + EPS) * gmix_ref[...]).astype(BF16)

    def seg(j):
        return jnp.dot(xn, w_ref[:, j * D_GRP:(j + 1) * D_GRP], preferred_element_type=F32)

    def head_norm(y, g_ref, scale):
        ss = jnp.dot((y * y).astype(BF16), bd_ref[...], preferred_element_type=F32) * (1.0 / HEAD_DIM)
        return y * lax.rsqrt(ss + EPS) * (g_ref[...] * scale)

    lane = lax.broadcasted_iota(I32, (x.shape[0], LANES), 1)
    first_half = (lane % HEAD_DIM) < (HEAD_DIM // 2)

    def spread(t):
        t = t + pltpu.roll(t, HEAD_DIM // 2, 1)
        return t + pltpu.roll(t, HEAD_DIM, 1)

    cos = spread(cos_ref[...])
    sin = jnp.where(first_half, -1.0, 1.0) * spread(sin_ref[...])

    def rope(y):
        outs = []
        for j in range(N_PAIRS):
            ys = y[:, j * LANES:(j + 1) * LANES]
            partner = jnp.where(first_half, pltpu.roll(ys, LANES - HEAD_DIM // 2, 1),
                                pltpu.roll(ys, HEAD_DIM // 2, 1))
            outs.append(ys * cos + partner * sin)
        return jnp.concatenate(outs, axis=1)

    scale = 1.0 / math.sqrt(HEAD_DIM)
    va_ref[0] = lax.dot_general(wvt_ref[...], xn, (((1,), (1,)), ((), ())),
                                preferred_element_type=F32).astype(va_ref.dtype)
    qb_ref[0] = rope(head_norm(seg(3), gqb_ref, scale)).astype(qb_ref.dtype)
    kb_ref[0] = rope(head_norm(seg(4), gkb_ref, 1.0)).astype(kb_ref.dtype)
    vb_ref[0] = seg(5).astype(vb_ref.dtype)

    fa = jnp.dot(xn, wf_ref[...], preferred_element_type=F32) + bf_ref[...]
    logf = jnp.minimum(fa, 0.0) - jnp.log1p(jnp.exp(-jnp.abs(fa)))
    hi = logf.astype(BF16)
    mid = (logf - hi.astype(F32)).astype(BF16)
    lo = (logf - hi.astype(F32) - mid.astype(F32)).astype(BF16)
    parts = jnp.dot(tri_ref[...], jnp.concatenate([hi, mid, lo], axis=1), preferred_element_type=F32)
    c = parts[:, :LANES] + parts[:, LANES:2 * LANES] + parts[:, 2 * LANES:] + carry_ref[...]
    carry_ref[...] = c[c.shape[0] - 1:, :]

    qa = head_norm(seg(0), gqa_ref, scale * LOG2E)
    ka = head_norm(seg(1), gka_ref, 1.0)
    c2 = c * LOG2E
    c_ref[0] = c2[:, :N_HEADS]
    ones = (jnp.where((lane >= HEAD_DIM + 3) & (lane < HEAD_DIM + 6), 1.0, 0.0)
            - jnp.where(lane == HEAD_DIM + 6, shift_ref[...], 0.0))
    for h in range(N_HEADS):
        cb = jnp.broadcast_to(c2[:, h:h + 1], (x.shape[0], LANES))
        hi = cb.astype(BF16).astype(F32)
        mid = (cb - hi).astype(BF16).astype(F32)
        lo = cb - hi - mid
        pieces = jnp.where(lane == HEAD_DIM, hi, jnp.where(lane == HEAD_DIM + 1, mid,
                           jnp.where(lane == HEAD_DIM + 2, lo, 0.0)))
        q_extra = pieces + ones
        k_extra = jnp.where(((lane >= HEAD_DIM) & (lane < HEAD_DIM + 3)) | (lane == HEAD_DIM + 6), 1.0, 0.0) \
            - pltpu.roll(pieces, 3, 1)
        j, odd = divmod(h, HEADS_PER_TILE)
        qp = qa[:, j * LANES:(j + 1) * LANES]
        kp = ka[:, j * LANES:(j + 1) * LANES]
        if odd:
            qp = pltpu.roll(qp, HEAD_DIM, 1)
            kp = pltpu.roll(kp, HEAD_DIM, 1)
        qa_ref[0, h] = jnp.where(lane < HEAD_DIM, qp, q_extra).astype(qa_ref.dtype)
        ka_ref[0, h] = jnp.where(lane < HEAD_DIM, kp, k_extra).astype(ka_ref.dtype)


def _inproj(x, gmix, w_main, w_vt, w_f, b_f, shift, gqa, gka, gqb, gkb, bd, cos_t, sin_t, tri):
    B, S, D = x.shape
    tm = TM_IN
    const = lambda shape: pl.BlockSpec(shape, lambda b, i: (0,) * len(shape))
    tok = lambda w, dt: jax.ShapeDtypeStruct((B, S, w), dt)
    tok_spec = lambda w: pl.BlockSpec((1, tm, w), lambda b, i: (b, i, 0))
    head_spec = pl.BlockSpec((1, N_HEADS, tm, LANES), lambda b, i: (b, 0, i, 0))
    head_shape = jax.ShapeDtypeStruct((B, N_HEADS, S, LANES), BF16)
    return pl.pallas_call(
        _inproj_kernel,
        grid=(B, S // tm),
        in_specs=[tok_spec(D), const((1, D)), const(w_main.shape), const(w_vt.shape), const(w_f.shape),
                  const((1, LANES)), const((1, LANES)),
                  const((1, D_GRP)), const((1, D_GRP)), const((1, D_GRP)), const((1, D_GRP)),
                  const((D_GRP, D_GRP)),
                  pl.BlockSpec((tm, LANES), lambda b, i: (i, 0)),
                  pl.BlockSpec((tm, LANES), lambda b, i: (i, 0)),
                  const((tm, tm))],
        out_specs=[head_spec, head_spec, pl.BlockSpec((1, D_GRP, tm), lambda b, i: (b, 0, i))]
        + [tok_spec(D_GRP)] * 3 + [tok_spec(N_HEADS)],
        out_shape=[head_shape, head_shape, jax.ShapeDtypeStruct((B, D_GRP, S), BF16),
                   tok(D_GRP, F32), tok(D_GRP, F32), tok(D_GRP, F32), tok(N_HEADS, F32)],
        scratch_shapes=[pltpu.VMEM((1, LANES), F32)],
        compiler_params=_cparams(("arbitrary", "arbitrary")),
        name="inproj",
    )(x, gmix, w_main, w_vt, w_f, b_f, shift, gqa, gka, gqb, gkb, bd, cos_t, sin_t, tri)


def _fox_kernel(first_ref, q_ref, k_ref, v_ref, o_ref, *, online):
    qi = pl.program_id(2)
    tq = q_ref.shape[2]

    def step(start, width, carry, diag):
        carry = list(carry)
        sub = min(TK_SUB, width)
        chunks = [(c, j) for c in range(width // sub) for j in range(HEADS_PER_TILE)]
        def first_query(c):
            return c * sub if diag else 0

        def score(c, j):
            k = k_ref[0, j, pl.ds(start + c * sub, sub), :]
            q = q_ref[0, j, first_query(c):, :]
            return lax.dot_general(k, q, (((1,), (1,)), ((), ())), preferred_element_type=F32)

        def tail(full, lo, new):
            return new if lo == 0 else jnp.concatenate([full[:, :lo], new], axis=1)

        scores = {cj: score(*cj) for cj in chunks[:FOX_AHEAD]}
        for n, (c, j) in enumerate(chunks):
                if n + FOX_AHEAD < len(chunks):
                    nxt = chunks[n + FOX_AHEAD]
                    scores[nxt] = score(*nxt)
                m, l, acc = carry[j]
                lo = first_query(c)
                vt = v_ref[0, :, pl.ds(start + c * sub, sub)]
                s = scores.pop((c, j))
                if diag:
                    key = lax.broadcasted_iota(I32, s.shape, 0)
                    qry = lax.broadcasted_iota(I32, s.shape, 1)
                    s = jnp.where(key <= qry, s, NEG)
                if online:
                    m_new = jnp.maximum(m[:, lo:], jnp.max(s, axis=0, keepdims=True))
                    alpha = jnp.exp2(m[:, lo:] - m_new)
                    p = jnp.exp2(s - m_new)
                    l_new = alpha * l[:, lo:] + jnp.sum(p, axis=0, keepdims=True)
                    acc_new = alpha * acc[:, lo:] + jnp.dot(vt, p.astype(BF16), preferred_element_type=F32)
                    m = tail(m, lo, m_new)
                else:
                    p = jnp.exp2(s)
                    l_new = l[:, lo:] + jnp.sum(p, axis=0, keepdims=True)
                    acc_new = acc[:, lo:] + jnp.dot(vt, p.astype(BF16), preferred_element_type=F32)
                carry[j] = (m, tail(l, lo, l_new), tail(acc, lo, acc_new))
        return tuple(carry)

    init = tuple((jnp.full((1, tq), NEG, F32), jnp.zeros((1, tq), F32), jnp.zeros((LANES, tq), F32))
                 for _ in range(HEADS_PER_TILE))
    assert TK_WIDE == tq and TK_WIDE % TK_SUB == 0
    per_wide = TK_WIDE // TK_SUB
    first = first_ref[(pl.program_id(0) * pl.num_programs(1) + pl.program_id(1)) * pl.num_programs(2) + qi]
    live = qi * per_wide - first
    n_narrow = lax.rem(live, per_wide)
    narrow_start = pl.multiple_of(first * TK_SUB, TK_SUB)
    carry = lax.switch(
        n_narrow,
        [lambda c: c] + [lambda c, w=w: step(narrow_start, w * TK_SUB, c, False) for w in range(1, per_wide)],
        init)
    wide_start = (first + n_narrow) * TK_SUB
    carry = lax.fori_loop(
        0, lax.div(live, per_wide),
        lambda i, c: step(pl.multiple_of(wide_start + i * TK_WIDE, TK_SUB), TK_WIDE, c, False), carry)
    carry = step(pl.multiple_of(qi * tq, tq), tq, carry, True)
    outs = [acc / l for (_, l, acc) in carry]
    feat = lax.broadcasted_iota(I32, (LANES, tq), 0)
    o_ref[0] = jnp.where(feat < HEAD_DIM, outs[0], outs[1]).T.astype(o_ref.dtype)


def _fox(first_tile, qa, ka, va_t, online):
    B, _, S, _ = qa.shape
    grid_spec = pltpu.PrefetchScalarGridSpec(
        num_scalar_prefetch=1,
        grid=(B, N_PAIRS, S // TQ),
        in_specs=[pl.BlockSpec((1, HEADS_PER_TILE, TQ, LANES), lambda b, hp, i, first: (b, hp, i, 0)),
                  pl.BlockSpec((1, HEADS_PER_TILE, S, LANES), lambda b, hp, i, first: (b, hp, 0, 0)),
                  pl.BlockSpec((1, LANES, S), lambda b, hp, i, first: (b, hp, 0))],
        out_specs=pl.BlockSpec((1, TQ, LANES), lambda b, hp, i, first: (b, i, hp)),
    )
    return pl.pallas_call(
        functools.partial(_fox_kernel, online=online),
        grid_spec=grid_spec,
        out_shape=jax.ShapeDtypeStruct((B, S, D_GRP), F32),
        compiler_params=_cparams(("arbitrary", "arbitrary", "arbitrary")),
        name="fox_online" if online else "fox",
    )(first_tile, qa, ka, va_t)


def _dilated_kernel(shift_ref, q_ref, kp_ref, kc_ref, vp_ref, vc_ref, o_ref, qq, kk, vv, qq4, kk4, vv4, osc, lsc,
                    *, bounded):
    u = pl.program_id(1)
    span = q_ref.shape[1]
    qq[...] = q_ref[0]
    kk[0:span, :] = kp_ref[0]
    kk[span:2 * span, :] = kc_ref[0]
    vv[0:span, :] = vp_ref[0]
    vv[span:2 * span, :] = vc_ref[0]
    for src, dst in ((qq, qq4), (kk, kk4), (vv, vv4)):
        part = src.shape[0] // DIL_PRE
        for a in range(DIL_PRE):
            dst[a * part:(a + 1) * part, :] = src[pl.ds(a, part, stride=DIL_PRE), :]

    def rows(buf, buf4, start, n, d):
        if d % DIL_PRE:
            return buf[pl.ds(start, n, stride=d), :]
        part = buf4.shape[0] // DIL_PRE
        a = lax.rem(start, DIL_PRE)
        return buf4[pl.ds(a * part + lax.div(start, DIL_PRE), n, stride=d // DIL_PRE), :]

    lane = lax.broadcasted_iota(I32, (BLOCK, LANES), 1)
    ql = lax.broadcasted_iota(I32, (BLOCK, 2 * BLOCK), 0)
    kl = lax.broadcasted_iota(I32, (BLOCK, 2 * BLOCK), 1)
    dist = ql + BLOCK - kl
    band = (dist >= 0) & (dist <= BLOCK)
    live = -shift_ref[0, 0] if bounded else 0.0
    bias = jnp.where(band, live, NEG)
    bias_first = jnp.where(band & (kl >= BLOCK), live, NEG)

    def scores(q_start, k_start, d, first):
        qs = rows(qq, qq4, q_start, BLOCK, d).astype(BF16)
        ks = rows(kk, kk4, k_start, 2 * BLOCK, d).astype(BF16)
        mask = jnp.where(first, bias_first, bias)
        out = []
        for j in range(HEADS_PER_TILE):
            qj = jnp.where(lane // HEAD_DIM == j, qs, jnp.zeros_like(qs))
            out.append(lax.dot_general(qj, ks, (((1,), (1,)), ((), ())), preferred_element_type=F32) + mask)
        return out

    def finish(s_heads, k_start, d):
        vs = rows(vv, vv4, k_start, 2 * BLOCK, d).astype(BF16)
        o_heads, lse_heads = [], []
        for s in s_heads:
            if bounded:
                p = jnp.exp(s)
                o_heads.append(jnp.dot(p.astype(BF16), vs, preferred_element_type=F32))
                lse_heads.append(jnp.sum(p, axis=-1, keepdims=True))
            else:
                m = jnp.max(s, axis=-1, keepdims=True)
                p = jnp.exp(s - m)
                l = jnp.sum(p, axis=-1, keepdims=True)
                o_heads.append(jnp.dot((p / l).astype(BF16), vs, preferred_element_type=F32))
                lse_heads.append(m + jnp.log(l))
        o = jnp.where(lane < HEAD_DIM, o_heads[0], o_heads[1])
        lse = jnp.where(lane < HEAD_DIM, lse_heads[0], lse_heads[1])
        return o, lse

    for pidx, (window, d) in enumerate(DIL_PATTERNS):
        assert window // d == BLOCK
        unit = d * BLOCK
        n_problems = (span // unit) * d
        assert n_problems % DIL_GROUP == 0

        def body(g, _, pidx=pidx, d=d, unit=unit):
            starts, s_all = [], []
            for t in range(DIL_GROUP):
                idx = g * DIL_GROUP + t
                w = idx // d
                q_start = w * unit + (idx - w * d)
                k_start = span - unit + q_start
                starts.append((q_start, k_start))
                s_all.append(scores(q_start, k_start, d, jnp.logical_and(u == 0, w == 0)))
            for (q_start, k_start), s_heads in zip(starts, s_all):
                o, lse = finish(s_heads, k_start, d)
                osc[pidx, pl.ds(q_start, BLOCK, stride=d), :] = o
                lsc[pidx, pl.ds(q_start, BLOCK, stride=d), :] = lse
            return 0

        lax.fori_loop(0, n_problems // DIL_GROUP, body, 0)

    if bounded:
        num = osc[0] + osc[1] + osc[2]
        den = lsc[0] + lsc[1] + lsc[2]
    else:
        mx = jnp.maximum(jnp.maximum(lsc[0], lsc[1]), lsc[2])
        num = jnp.zeros((span, LANES), F32)
        den = jnp.zeros((span, LANES), F32)
        for pidx in range(len(DIL_PATTERNS)):
            e = jnp.exp(lsc[pidx] - mx)
            num = num + e * osc[pidx]
            den = den + e
    o_ref[0] = (num / den).astype(o_ref.dtype)


def _dilated(shift, qb, kb, vb, bounded):
    B, S, _ = qb.shape
    span = DIL_SPAN
    cur = pl.BlockSpec((1, span, LANES), lambda b, u, hp: (b, u, hp))
    prev = pl.BlockSpec((1, span, LANES), lambda b, u, hp: (b, jnp.maximum(u - 1, 0), hp))
    return pl.pallas_call(
        functools.partial(_dilated_kernel, bounded=bounded),
        grid=(B, S // span, N_PAIRS),
        in_specs=[pl.BlockSpec(memory_space=pltpu.SMEM), cur, prev, cur, prev, cur],
        out_specs=cur,
        out_shape=jax.ShapeDtypeStruct((B, S, D_GRP), F32),
        scratch_shapes=[pltpu.VMEM((span, LANES), F32),
                        pltpu.VMEM((2 * span, LANES), F32), pltpu.VMEM((2 * span, LANES), F32),
                        pltpu.VMEM((span, LANES), F32),
                        pltpu.VMEM((2 * span, LANES), F32), pltpu.VMEM((2 * span, LANES), F32),
                        pltpu.VMEM((len(DIL_PATTERNS), span, LANES), F32),
                        pltpu.VMEM((len(DIL_PATTERNS), span, LANES), F32)],
        compiler_params=_cparams(("arbitrary", "arbitrary", "arbitrary")),
        name="dilated" if bounded else "dilated_exact",
    )(shift, qb, kb, kb, vb, vb)


def _store_row_tiles(ref, stage_ref, x, first_row=0):
    n = x.shape[0]
    for c in range(ROW_CHUNKS):
        stage_ref[pl.ds(first_row * ROW_CHUNKS + c, n, stride=ROW_CHUNKS), :] = x[:, c * LANES:(c + 1) * LANES]
    rows = slice(first_row * ROW_CHUNKS, (first_row + n) * ROW_CHUNKS)
    ref[rows, :] = stage_ref[rows, :].astype(ref.dtype)


def _load_row_tiles(ref, stage_ref, n, first_row=0):
    rows = slice(first_row * ROW_CHUNKS, (first_row + n) * ROW_CHUNKS)
    stage_ref[rows, :] = ref[rows, :].astype(F32)
    return jnp.concatenate([stage_ref[pl.ds(first_row * ROW_CHUNKS + c, n, stride=ROW_CHUNKS), :]
                            for c in range(ROW_CHUNKS)], axis=1)


def _row_tile_copy(src_ref, src_row, dst_ref, dst_row, sem):
    src = src_ref.at[pl.ds(pl.multiple_of(src_row * ROW_CHUNKS, ROW_CHUNKS), ROW_CHUNKS), :]
    dst = dst_ref.at[pl.ds(pl.multiple_of(dst_row * ROW_CHUNKS, ROW_CHUNKS), ROW_CHUNKS), :]
    return pltpu.make_async_copy(src, dst, sem)


def _outproj_kernel(oa_ref, ob_ref, x_ref, gfox_ref, gdil_ref, wo_ref, gffn_ref, wr_ref, br_ref, upper_ref,
                    h_ref, hn_ref, eid_ref, gate_ref, rank_ref, cnt_ref, run_ref, stage_ref):
    @pl.when(pl.program_id(0) == 0)
    def _():
        run_ref[...] = jnp.zeros_like(run_ref)

    def norm(y, g):
        ms = jnp.mean(y * y, axis=-1, keepdims=True)
        return y * lax.rsqrt(ms + EPS) * g

    a = norm(oa_ref[...], gfox_ref[...]).astype(BF16)
    b = norm(ob_ref[...], gdil_ref[...]).astype(BF16)
    mix = (jnp.dot(a, wo_ref[0:D_GRP, :], preferred_element_type=F32)
           + jnp.dot(b, wo_ref[D_GRP:2 * D_GRP, :], preferred_element_type=F32))
    h = x_ref[...] + mix
    h_ref[...] = h
    hn = norm(h, gffn_ref[...])
    _store_row_tiles(hn_ref, stage_ref, hn)

    z = lax.dot_general(wr_ref[...], hn, (((1,), (1,)), ((), ())), preferred_element_type=F32,
                        precision=lax.Precision.HIGHEST) + br_ref[...]
    tm = z.shape[1]
    best = z[0:1, :]
    g_sel = jnp.zeros((1, tm), I32)
    for g in range(1, N_GROUPS):
        better = z[g:g + 1, :] > best
        g_sel = jnp.where(better, g, g_sel)
        best = jnp.maximum(best, z[g:g + 1, :])
    den = jnp.zeros((1, tm), F32)
    for g in range(N_GROUPS):
        den = den + jnp.exp(z[g:g + 1, :] - best)
    pg_top = 1.0 / den

    ze = jnp.zeros((EXPERTS_PER_GROUP, tm), F32)
    for g in range(N_GROUPS):
        ze = jnp.where(g_sel == g, z[8 + g * EXPERTS_PER_GROUP:8 + (g + 1) * EXPERTS_PER_GROUP, :], ze)
    e_iota = lax.broadcasted_iota(I32, ze.shape, 0)
    v1 = jnp.max(ze, axis=0, keepdims=True)
    i1 = jnp.min(jnp.where(ze == v1, e_iota, EXPERTS_PER_GROUP), axis=0, keepdims=True)
    ze2 = jnp.where(e_iota == i1, -jnp.inf, ze)
    v2 = jnp.max(ze2, axis=0, keepdims=True)
    i2 = jnp.min(jnp.where(ze2 == v2, e_iota, EXPERTS_PER_GROUP), axis=0, keepdims=True)
    e2 = jnp.exp(v2 - v1)
    inv = 1.0 / (1.0 + e2)
    gate1 = inv * pg_top
    gate2 = e2 * inv * pg_top
    eid1 = g_sel * EXPERTS_PER_GROUP + i1
    eid2 = g_sel * EXPERTS_PER_GROUP + i2

    x_iota = lax.broadcasted_iota(I32, (N_EXPERTS, tm), 0)
    hot1 = x_iota == eid1
    hot2 = x_iota == eid2
    multi = jnp.logical_or(hot1, hot2)
    before = jnp.dot(multi.astype(BF16), upper_ref[...], preferred_element_type=F32)
    slot = before + run_ref[:, 0:1]
    rank1 = jnp.sum(jnp.where(hot1, slot, 0.0), axis=0, keepdims=True)
    rank2 = jnp.sum(jnp.where(hot2, slot, 0.0), axis=0, keepdims=True)
    run_ref[...] = run_ref[...] + jnp.sum(multi.astype(F32), axis=1, keepdims=True)

    eid_ref[...] = jnp.concatenate([eid1, eid2], axis=0)
    gate_ref[...] = jnp.concatenate([gate1, gate2], axis=0)
    rank_ref[...] = jnp.concatenate([rank1, rank2], axis=0).astype(I32)
    cnt_ref[...] = run_ref[...].astype(I32)


def _outproj(oa, ob, x2, gfox, gdil, w_out, gffn, w_r, b_r, upper):
    T, D = x2.shape
    tm = TM_OUT
    const = lambda shape: pl.BlockSpec(shape, lambda i: (0,) * len(shape))
    tok = lambda w: pl.BlockSpec((tm, w), lambda i: (i, 0))
    lanes2 = pl.BlockSpec((2, tm), lambda i: (0, i))
    return pl.pallas_call(
        _outproj_kernel,
        grid=(T // tm,),
        in_specs=[tok(D_GRP), tok(D_GRP), tok(D), const((1, D_GRP)), const((1, D_GRP)), const((D, D)),
                  const((1, D)), const((ROUTER_ROWS, D)), const((ROUTER_ROWS, 1)), const((tm, tm))],
        out_specs=[tok(D), pl.BlockSpec((tm * ROW_CHUNKS, LANES), lambda i: (i, 0)),
                   lanes2, lanes2, lanes2, const((N_EXPERTS, LANES))],
        out_shape=[jax.ShapeDtypeStruct((T, D), F32), jax.ShapeDtypeStruct((T * ROW_CHUNKS, LANES), BF16),
                   jax.ShapeDtypeStruct((2, T), I32), jax.ShapeDtypeStruct((2, T), F32),
                   jax.ShapeDtypeStruct((2, T), I32), jax.ShapeDtypeStruct((N_EXPERTS, LANES), I32)],
        scratch_shapes=[pltpu.VMEM((N_EXPERTS, LANES), F32), pltpu.VMEM((tm * ROW_CHUNKS, LANES), F32)],
        compiler_params=_cparams(("arbitrary",)),
        name="outproj",
    )(oa, ob, x2, gfox, gdil, w_out, gffn, w_r, b_r, upper)


def _scatter_kernel(starts_ref, cnt_ref, pos0_ref, pos1_ref, hn_ref, xs_ref, ring, sems, zero_sem, *, n_steps):
    i = pl.program_id(0)
    tm = hn_ref.shape[0] // ROW_CHUNKS
    slot = lax.rem(i, 2)

    def wait_slot(s):
        for _ in range(2):
            pltpu.make_async_copy(ring.at[s], xs_ref.at[pl.ds(0, tm * ROW_CHUNKS), :], sems.at[s]).wait()

    @pl.when(i >= 2)
    def _():
        wait_slot(slot)

    ring[slot] = hn_ref[...]

    def start(r, _):
        for k, pos_ref in enumerate((pos0_ref, pos1_ref)):
            _row_tile_copy(ring.at[slot], r, xs_ref, pos_ref[r], sems.at[slot]).start(priority=k)
        return 0

    lax.fori_loop(0, tm, start, 0, unroll=DMA_UNROLL)

    @pl.when(i == n_steps - 1)
    def _():
        wait_slot(slot)
        if n_steps > 1:
            wait_slot(1 - slot)
        ring[0] = jnp.zeros((tm * ROW_CHUNKS, LANES), ring.dtype)

        def pad_expert(e, _, wait):
            n_pad = lax.rem(TILE_E - lax.rem(cnt_ref[e], TILE_E), TILE_E)
            first = starts_ref[e] + cnt_ref[e]
            size = TILE_E // 2
            while size >= 1:
                row0 = first + (n_pad & ~(2 * size - 1))

                @pl.when((n_pad & size) != 0)
                def _(size=size, row0=row0):
                    copy = pltpu.make_async_copy(
                        ring.at[0, pl.ds(0, size * ROW_CHUNKS), :],
                        xs_ref.at[pl.ds(pl.multiple_of(row0 * ROW_CHUNKS, ROW_CHUNKS), size * ROW_CHUNKS), :],
                        zero_sem)
                    copy.wait() if wait else copy.start()

                size //= 2
            return 0

        lax.fori_loop(0, N_EXPERTS, functools.partial(pad_expert, wait=False), 0)
        last = N_EXPERTS - 1
        used_rows = starts_ref[last] + cnt_ref[last] + lax.rem(TILE_E - lax.rem(cnt_ref[last], TILE_E), TILE_E)
        n_tail = xs_ref.shape[0] // (tm * ROW_CHUNKS) - used_rows // tm

        def tail_copy(t):
            row0 = pl.multiple_of((used_rows + t * tm) * ROW_CHUNKS, tm * ROW_CHUNKS)
            return pltpu.make_async_copy(ring.at[0], xs_ref.at[pl.ds(row0, tm * ROW_CHUNKS), :], zero_sem)

        lax.fori_loop(0, n_tail, lambda t, c: (tail_copy(t).start(), c)[1], 0)
        lax.fori_loop(0, N_EXPERTS, functools.partial(pad_expert, wait=True), 0)
        lax.fori_loop(0, n_tail, lambda t, c: (tail_copy(t).wait(), c)[1], 0)


def _scatter(starts, cnt, pos, hn, n_rows):
    T = hn.shape[0] // ROW_CHUNKS
    tm = TM_ROWS
    assert TILE_E % tm == 0 and TILE_E // 2 <= tm
    n_steps = T // tm
    grid_spec = pltpu.PrefetchScalarGridSpec(
        num_scalar_prefetch=2,
        grid=(n_steps,),
        in_specs=[pl.BlockSpec((tm,), lambda i, starts, cnt: (i,), memory_space=pltpu.SMEM),
                  pl.BlockSpec((tm,), lambda i, starts, cnt: (i,), memory_space=pltpu.SMEM),
                  pl.BlockSpec((tm * ROW_CHUNKS, LANES), lambda i, starts, cnt: (i, 0))],
        out_specs=pl.BlockSpec(memory_space=pl.ANY),
        scratch_shapes=[pltpu.VMEM((2, tm * ROW_CHUNKS, LANES), hn.dtype),
                        pltpu.SemaphoreType.DMA((2,)), pltpu.SemaphoreType.DMA(())],
    )
    return pl.pallas_call(
        functools.partial(_scatter_kernel, n_steps=n_steps),
        grid_spec=grid_spec,
        out_shape=jax.ShapeDtypeStruct((n_rows * ROW_CHUNKS, LANES), hn.dtype),
        compiler_params=_cparams(("arbitrary",)),
        name="scatter_rows",
    )(starts, cnt, pos[0], pos[1], hn)


def _experts_kernel(te_ref, tb_ref, tr_ref, xs_ref, wg_ref, wu_ref, wd_ref, y_ref, wg_bf, wu_bf, wd_bf, stage_ref):
    del tb_ref
    i = pl.program_id(0)
    rows = tr_ref[i]
    new_expert = jnp.logical_or(i == 0, te_ref[i] != te_ref[jnp.maximum(i - 1, 0)])

    @pl.when(jnp.logical_and(rows > 0, new_expert))
    def _():
        wg_bf[...] = wg_ref[0].astype(BF16)
        wu_bf[...] = wu_ref[0].astype(BF16)
        wd_bf[...] = wd_ref[0].astype(BF16)

    half = TILE_E // 2
    for part in range(2):
        @pl.when(rows > part * half)
        def _(part=part):
            x = _load_row_tiles(xs_ref, stage_ref, half, part * half).astype(BF16)
            g = jnp.dot(x, wg_bf[...], preferred_element_type=F32)
            up = jnp.dot(x, wu_bf[...], preferred_element_type=F32)
            hmid = (g * jax.nn.sigmoid(g) * up).astype(BF16)
            _store_row_tiles(y_ref, stage_ref, jnp.dot(hmid, wd_bf[...], preferred_element_type=F32), part * half)

    @pl.when(jnp.logical_and(rows > 0, rows <= half))
    def _():
        y_ref[half * ROW_CHUNKS:, :] = jnp.zeros((half * ROW_CHUNKS, LANES), y_ref.dtype)


def _experts(tile_expert, tile_block, tile_rows, xs, w_gate, w_up, w_down):
    n_tiles = tile_expert.shape[0]
    D = D_MODEL
    rows_spec = pl.BlockSpec((TILE_E * ROW_CHUNKS, LANES), lambda i, te, tb, tr: (tb[i], 0))
    weights = lambda shape: pl.BlockSpec(shape, lambda i, te, tb, tr: (te[i], 0, 0))
    grid_spec = pltpu.PrefetchScalarGridSpec(
        num_scalar_prefetch=3,
        grid=(n_tiles,),
        in_specs=[rows_spec, weights((1, D, D_EXPERT)), weights((1, D, D_EXPERT)), weights((1, D_EXPERT, D))],
        out_specs=rows_spec,
        scratch_shapes=[pltpu.VMEM((D, D_EXPERT), BF16), pltpu.VMEM((D, D_EXPERT), BF16),
                        pltpu.VMEM((D_EXPERT, D), BF16), pltpu.VMEM((TILE_E * ROW_CHUNKS, LANES), F32)],
    )
    return pl.pallas_call(
        _experts_kernel,
        grid_spec=grid_spec,
        out_shape=jax.ShapeDtypeStruct(xs.shape, xs.dtype),
        input_output_aliases={3: 0},
        compiler_params=_cparams(("arbitrary",)),
        name="experts",
    )(tile_expert, tile_block, tile_rows, xs, w_gate, w_up, w_down)


def _combine_kernel(pos0_ref, pos1_ref, next0_ref, next1_ref, h_ref, gate_ref, y_ref, o_ref, ybuf, sems, stage_ref,
                    *, n_steps):
    i = pl.program_id(0)
    tm = h_ref.shape[0]
    slot = lax.rem(i, 2)

    def gather(p_refs, s):
        def start(r, _):
            for k, p_ref in enumerate(p_refs):
                _row_tile_copy(y_ref, p_ref[r], ybuf.at[s, k], r, sems.at[s]).start(priority=k)
            return 0

        lax.fori_loop(0, tm, start, 0, unroll=DMA_UNROLL)

    @pl.when(i == 0)
    def _():
        gather((pos0_ref, pos1_ref), slot)

    @pl.when(i + 1 < n_steps)
    def _():
        gather((next0_ref, next1_ref), 1 - slot)

    for k in range(2):
        pltpu.make_async_copy(y_ref.at[pl.ds(0, tm * ROW_CHUNKS), :], ybuf.at[slot, k], sems.at[slot]).wait()
    g = gate_ref[...]
    acc = h_ref[...] + g[:, 0:1] * _load_row_tiles(ybuf.at[slot, 0], stage_ref, tm)
    o_ref[...] = acc + g[:, 1:2] * _load_row_tiles(ybuf.at[slot, 1], stage_ref, tm)


def _combine(pos, h, gate_t, y):
    T, D = h.shape
    tm = TM_ROWS
    n_steps = T // tm
    return pl.pallas_call(
        functools.partial(_combine_kernel, n_steps=n_steps),
        grid=(n_steps,),
        in_specs=[pl.BlockSpec((tm,), lambda i: (i,), memory_space=pltpu.SMEM),
                  pl.BlockSpec((tm,), lambda i: (i,), memory_space=pltpu.SMEM),
                  pl.BlockSpec((tm,), lambda i: (jnp.minimum(i + 1, n_steps - 1),), memory_space=pltpu.SMEM),
                  pl.BlockSpec((tm,), lambda i: (jnp.minimum(i + 1, n_steps - 1),), memory_space=pltpu.SMEM),
                  pl.BlockSpec((tm, D), lambda i: (i, 0)),
                  pl.BlockSpec((tm, 2), lambda i: (i, 0)),
                  pl.BlockSpec(memory_space=pl.ANY)],
        out_specs=pl.BlockSpec((tm, D), lambda i: (i, 0)),
        out_shape=jax.ShapeDtypeStruct((T, D), F32),
        scratch_shapes=[pltpu.VMEM((2, 2, tm * ROW_CHUNKS, LANES), y.dtype), pltpu.SemaphoreType.DMA((2,)),
                        pltpu.VMEM((tm * ROW_CHUNKS, LANES), F32)],
        compiler_params=_cparams(("arbitrary",)),
        name="combine",
    )(pos[0], pos[1], pos[0], pos[1], h, gate_t, y)


def _rope_tables(S):
    inv_freq = 1.0 / (ROPE_THETA ** (jnp.arange(0, HEAD_DIM, 2, dtype=F32) / HEAD_DIM))
    ang = jnp.arange(S, dtype=F32)[:, None] * inv_freq[None, :]
    widen = lambda t: jnp.pad(t, ((0, 0), (0, LANES - HEAD_DIM // 2)))
    return widen(jnp.cos(ang)), widen(jnp.sin(ang))


def _layer(x, norm_mix, w_in, b_forget, q_norm_fox, k_norm_fox, q_norm_dil, k_norm_dil,
           out_norm_fox, out_norm_dil, w_out, norm_ffn, w_router_group, b_router_group,
           w_router_expert, b_router_expert, w_gate, w_up, w_down):
    B, S, D = x.shape
    T = B * S
    n_main = 6 * D_GRP

    w_main = w_in.astype(BF16)
    w_vt = w_main[:, 2 * D_GRP:3 * D_GRP].T
    w_f = jnp.pad(w_main[:, n_main:], ((0, 0), (0, LANES - N_HEADS)))
    b_f = jnp.pad(b_forget, (0, LANES - N_HEADS))[None, :]
    per_head = lambda g: jnp.tile(g, N_HEADS)[None, :]
    bd = jnp.kron(jnp.eye(N_HEADS, dtype=F32), jnp.ones((HEAD_DIM, HEAD_DIM), F32)).astype(BF16)
    cos_t, sin_t = _rope_tables(S)
    tri = jnp.tril(jnp.ones((TM_IN, TM_IN), F32)).astype(BF16)
    upper = jnp.triu(jnp.ones((TM_OUT, TM_OUT), F32), k=1).astype(BF16)
    w_r = jnp.concatenate([
        jnp.pad(w_router_group.T, ((0, 8 - N_GROUPS), (0, 0))),
        w_router_expert.transpose(0, 2, 1).reshape(N_EXPERTS, D)], axis=0)
    b_r = jnp.concatenate([jnp.pad(b_router_group, (0, 8 - N_GROUPS)), b_router_expert.reshape(-1)])[:, None]

    bound = (HEAD_DIM / math.sqrt(HEAD_DIM)) * LOG2E * jnp.max(jnp.abs(q_norm_fox)) * jnp.max(jnp.abs(k_norm_fox))
    shift = FOX_BOUND_SLACK * bound + 1.0
    qa, ka, va, qb, kb, vb, c2 = _inproj(
        x, norm_mix[None, :], w_main, w_vt, w_f, b_f, jnp.full((1, LANES), shift, F32),
        per_head(q_norm_fox), per_head(k_norm_fox),
        per_head(q_norm_dil), per_head(k_norm_dil), bd, cos_t, sin_t, tri)
    n_q, n_k = S // TQ, S // TK_SUB
    c_first = c2[:, ::TQ, :].reshape(B, n_q, 1, N_PAIRS, HEADS_PER_TILE)
    c_last = c2[:, TK_SUB - 1::TK_SUB, :].reshape(B, 1, n_k, N_PAIRS, HEADS_PER_TILE)
    all_zero = jnp.all(c_first - c_last < -FOX_ZERO_BITS, axis=-1)
    live_so_far = jnp.einsum("bqkp,kj->bqjp", 1 - all_zero.astype(I32),
                             jnp.triu(jnp.ones((n_k, n_k), I32)))
    leading = (live_so_far == 0).astype(I32)
    before = (jnp.arange(n_k) * TK_SUB + TK_SUB <= jnp.arange(n_q)[:, None] * TQ)
    first_tile = jnp.sum(leading * before[None, :, :, None].astype(I32), axis=2)
    first_tile = first_tile.transpose(0, 2, 1).reshape(-1).astype(I32)
    oa = lax.cond(2.0 * shift <= FOX_SAFE_SPAN,
                  lambda: _fox(first_tile, qa, ka, va, online=False),
                  lambda: _fox(jnp.zeros_like(first_tile), qa, ka, va, online=True))
    dil_shift = (FOX_BOUND_SLACK * (HEAD_DIM / math.sqrt(HEAD_DIM))
                 * jnp.max(jnp.abs(q_norm_dil)) * jnp.max(jnp.abs(k_norm_dil)) + 1.0)
    dil_shift_arr = jnp.full((1, 1), dil_shift, F32)
    ob = lax.cond(2.0 * dil_shift <= DIL_SAFE_SPAN,
                  lambda: _dilated(dil_shift_arr, qb, kb, vb, bounded=True),
                  lambda: _dilated(dil_shift_arr, qb, kb, vb, bounded=False))

    h, hn, eid, gate, rank, cnt = _outproj(
        oa.reshape(T, D_GRP), ob.reshape(T, D_GRP), x.reshape(T, D), out_norm_fox[None, :],
        out_norm_dil[None, :], w_out.astype(BF16), norm_ffn[None, :], w_r, b_r, upper)

    counts = cnt[:, 0]
    padded = ((counts + TILE_E - 1) // TILE_E) * TILE_E
    ends = jnp.cumsum(padded)
    starts = ends - padded
    is_expert = eid[:, :, None] == jnp.arange(N_EXPERTS, dtype=I32)
    pos = jnp.sum(jnp.where(is_expert, starts, 0), axis=-1) + rank
    n_tiles = (2 * T) // TILE_E + N_EXPERTS
    tile_index = jnp.arange(n_tiles, dtype=I32)
    tile_valid = tile_index * TILE_E < ends[-1]
    tile_block = jnp.minimum(tile_index, ends[-1] // TILE_E - 1)
    in_region = ends[None, :] <= (tile_block * TILE_E)[:, None]
    tile_expert = jnp.sum(in_region.astype(I32), axis=1)
    expert_of_tile = tile_expert[:, None] == jnp.arange(N_EXPERTS, dtype=I32)
    rows_before = tile_block * TILE_E - jnp.sum(jnp.where(expert_of_tile, starts, 0), axis=1)
    tile_rows = jnp.clip(jnp.sum(jnp.where(expert_of_tile, counts, 0), axis=1) - rows_before, 0, TILE_E)
    tile_rows = jnp.where(tile_valid, tile_rows, 0).astype(I32)

    xs = _scatter(starts, counts, pos, hn, n_tiles * TILE_E)
    y = _experts(tile_expert, tile_block, tile_rows, xs, w_gate, w_up, w_down)
    out = _combine(pos, h, gate.T, y)
    return out.reshape(B, S, D)


def kernel(x, norm_mix, w_in, b_forget, q_norm_fox, k_norm_fox, q_norm_dil, k_norm_dil, out_norm_fox,
           out_norm_dil, w_out, norm_ffn, w_router_group, b_router_group, w_router_expert,
           b_router_expert, w_gate, w_up, w_down):
    h = x
    for l in range(norm_mix.shape[0]):
        h = _layer(h, norm_mix[l], w_in[l], b_forget[l], q_norm_fox[l], k_norm_fox[l], q_norm_dil[l],
                   k_norm_dil[l], out_norm_fox[l], out_norm_dil[l], w_out[l], norm_ffn[l],
                   w_router_group[l], b_router_group[l], w_router_expert[l], b_router_expert[l],
                   w_gate[l], w_up[l], w_down[l])
    return h
```

```python
import functools
import math

import jax
import jax.numpy as jnp
import numpy as np
from jax import lax
from jax.experimental import pallas as pl
from jax.experimental.pallas import tpu as pltpu

F32 = jnp.float32
BF16 = jnp.bfloat16
I32 = jnp.int32

D_MODEL = 1024
HEAD_DIM = 64
N_HEADS = 8
D_GRP = N_HEADS * HEAD_DIM
LANES = 128
HEADS_PER_TILE = LANES // HEAD_DIM
N_PAIRS = D_GRP // LANES
DIL_PATTERNS = ((128, 1), (512, 4), (2048, 16))
BLOCK = 128
ROPE_THETA = 10000.0
N_GROUPS = 4
EXPERTS_PER_GROUP = 8
N_EXPERTS = N_GROUPS * EXPERTS_PER_GROUP
D_EXPERT = 512
EPS = 1e-6
NEG = -1e30
LOG2E = 1.4426950408889634

TM_IN = 1024
TRI_ROWS = 256
TQ = 1024
TK_WIDE = 1024
TK_SUB = 256
FOX_AHEAD = 4
FOX_ZERO_BITS = 160.0
FOX_BOUND_SLACK = 1.02
DIL_SAFE_SPAN = 69.0
FOX_SAFE_SPAN = 100.0
DIL_SPAN = 2048
DIL_GROUP = 16
DIL_PRE = 4
TM_OUT = 512
TM_ROWS = 512
TILE_E = 512
ROUTER_ROWS = 8 + N_EXPERTS
ROW_CHUNKS = D_MODEL // LANES
DMA_UNROLL = 8
VMEM_LIMIT = 56 * 1024 * 1024


def _cparams(sem, flags=None):
    return pltpu.CompilerParams(dimension_semantics=sem, vmem_limit_bytes=VMEM_LIMIT, flags=flags)


def _inproj_kernel(x_ref, gmix_ref, w_ref, wvt_ref, wf_ref, bf_ref, shift_ref, gqa_ref, gka_ref, gqb_ref, gkb_ref,
                   bd_ref, cos_ref, sin_ref, tri_ref,
                   qa_ref, ka_ref, va_ref, qb_ref, kb_ref, vb_ref, c_ref, carry_ref):
    @pl.when(pl.program_id(1) == 0)
    def _():
        carry_ref[...] = jnp.zeros_like(carry_ref)

    x = x_ref[0]
    ms = jnp.mean(x * x, axis=-1, keepdims=True)
    xn = (x * lax.rsqrt(ms + EPS) * gmix_ref[...]).astype(BF16)

    def seg(j):
        return jnp.dot(xn, w_ref[:, j * D_GRP:(j + 1) * D_GRP], preferred_element_type=F32)

    def head_norm(y, g_ref, scale):
        ss = jnp.dot((y * y).astype(BF16), bd_ref[...], preferred_element_type=F32) * (1.0 / HEAD_DIM)
        return y * lax.rsqrt(ss + EPS) * (g_ref[...] * scale)

    lane = lax.broadcasted_iota(I32, (x.shape[0], LANES), 1)
    first_half = (lane % HEAD_DIM) < (HEAD_DIM // 2)

    def spread(t):
        t = t + pltpu.roll(t, HEAD_DIM // 2, 1)
        return t + pltpu.roll(t, HEAD_DIM, 1)

    cos = spread(cos_ref[...])
    sin = jnp.where(first_half, -1.0, 1.0) * spread(sin_ref[...])

    def rope(y):
        outs = []
        for j in range(N_PAIRS):
            ys = y[:, j * LANES:(j + 1) * LANES]
            partner = jnp.where(first_half, pltpu.roll(ys, LANES - HEAD_DIM // 2, 1),
                                pltpu.roll(ys, HEAD_DIM // 2, 1))
            outs.append(ys * cos + partner * sin)
        return jnp.concatenate(outs, axis=1)

    scale = 1.0 / math.sqrt(HEAD_DIM)
    va_ref[0] = lax.dot_general(wvt_ref[...], xn, (((1,), (1,)), ((), ())),
                                preferred_element_type=F32).astype(va_ref.dtype)
    qb_ref[0] = rope(head_norm(seg(3), gqb_ref, scale)).astype(qb_ref.dtype)
    kb_ref[0] = rope(head_norm(seg(4), gkb_ref, 1.0)).astype(kb_ref.dtype)
    vb_ref[0] = seg(5).astype(vb_ref.dtype)

    fa = jnp.dot(xn, wf_ref[...], preferred_element_type=F32) + bf_ref[...]
    logf = jnp.minimum(fa, 0.0) - jnp.log1p(jnp.exp(-jnp.abs(fa)))
    hi = logf.astype(BF16)
    mid = (logf - hi.astype(F32)).astype(BF16)
    lo = (logf - hi.astype(F32) - mid.astype(F32)).astype(BF16)
    pieces = jnp.concatenate([hi, mid, lo], axis=1)
    carry = carry_ref[...]
    blocks = []
    for j in range(x.shape[0] // TRI_ROWS):
        parts = jnp.dot(tri_ref[...], pieces[j * TRI_ROWS:(j + 1) * TRI_ROWS, :], preferred_element_type=F32)
        blk = parts[:, :LANES] + parts[:, LANES:2 * LANES] + parts[:, 2 * LANES:] + carry
        carry = blk[TRI_ROWS - 1:, :]
        blocks.append(blk)
    c = jnp.concatenate(blocks, axis=0)
    carry_ref[...] = carry

    qa = head_norm(seg(0), gqa_ref, scale * LOG2E)
    ka = head_norm(seg(1), gka_ref, 1.0)
    c2 = c * LOG2E
    c_ref[0] = c2[:, :N_HEADS]
    ones = (jnp.where((lane >= HEAD_DIM + 3) & (lane < HEAD_DIM + 6), 1.0, 0.0)
            - jnp.where(lane == HEAD_DIM + 6, shift_ref[...], 0.0))
    for h in range(N_HEADS):
        cb = jnp.broadcast_to(c2[:, h:h + 1], (x.shape[0], LANES))
        hi = cb.astype(BF16).astype(F32)
        mid = (cb - hi).astype(BF16).astype(F32)
        lo = cb - hi - mid
        pieces = jnp.where(lane == HEAD_DIM, hi, jnp.where(lane == HEAD_DIM + 1, mid,
                           jnp.where(lane == HEAD_DIM + 2, lo, 0.0)))
        q_extra = pieces + ones
        k_extra = jnp.where(((lane >= HEAD_DIM) & (lane < HEAD_DIM + 3)) | (lane == HEAD_DIM + 6), 1.0, 0.0) \
            - pltpu.roll(pieces, 3, 1)
        j, odd = divmod(h, HEADS_PER_TILE)
        qp = qa[:, j * LANES:(j + 1) * LANES]
        kp = ka[:, j * LANES:(j + 1) * LANES]
        if odd:
            qp = pltpu.roll(qp, HEAD_DIM, 1)
            kp = pltpu.roll(kp, HEAD_DIM, 1)
        qa_ref[0, h] = jnp.where(lane < HEAD_DIM, qp, q_extra).astype(qa_ref.dtype)
        ka_ref[0, h] = jnp.where(lane < HEAD_DIM, kp, k_extra).astype(ka_ref.dtype)


def _inproj(x, gmix, w_main, w_vt, w_f, b_f, shift, gqa, gka, gqb, gkb, bd, cos_t, sin_t, tri):
    B, S, D = x.shape
    tm = TM_IN
    const = lambda shape: pl.BlockSpec(shape, lambda b, i: (0,) * len(shape))
    tok = lambda w, dt: jax.ShapeDtypeStruct((B, S, w), dt)
    tok_spec = lambda w: pl.BlockSpec((1, tm, w), lambda b, i: (b, i, 0))
    head_spec = pl.BlockSpec((1, N_HEADS, tm, LANES), lambda b, i: (b, 0, i, 0))
    head_shape = jax.ShapeDtypeStruct((B, N_HEADS, S, LANES), BF16)
    return pl.pallas_call(
        _inproj_kernel,
        grid=(B, S // tm),
        in_specs=[tok_spec(D), const((1, D)), const(w_main.shape), const(w_vt.shape), const(w_f.shape),
                  const((1, LANES)), const((1, LANES)),
                  const((1, D_GRP)), const((1, D_GRP)), const((1, D_GRP)), const((1, D_GRP)),
                  const((D_GRP, D_GRP)),
                  pl.BlockSpec((tm, LANES), lambda b, i: (i, 0)),
                  pl.BlockSpec((tm, LANES), lambda b, i: (i, 0)),
                  const(tri.shape)],
        out_specs=[head_spec, head_spec, pl.BlockSpec((1, D_GRP, tm), lambda b, i: (b, 0, i))]
        + [tok_spec(D_GRP)] * 3 + [tok_spec(N_HEADS)],
        out_shape=[head_shape, head_shape, jax.ShapeDtypeStruct((B, D_GRP, S), BF16),
                   tok(D_GRP, F32), tok(D_GRP, F32), tok(D_GRP, F32), tok(N_HEADS, F32)],
        scratch_shapes=[pltpu.VMEM((1, LANES), F32)],
        compiler_params=_cparams(("arbitrary", "arbitrary")),
        name="inproj",
    )(x, gmix, w_main, w_vt, w_f, b_f, shift, gqa, gka, gqb, gkb, bd, cos_t, sin_t, tri)


def _fox_kernel(first_ref, q_ref, k_ref, v_ref, o_ref, *, online):
    qi = pl.program_id(2)
    tq = q_ref.shape[2]

    def step(start, width, carry, diag):
        carry = list(carry)
        sub = min(TK_SUB, width)
        chunks = [(c, j) for c in range(width // sub) for j in range(HEADS_PER_TILE)]
        def first_query(c):
            return c * sub if diag else 0

        def score(c, j):
            k = k_ref[0, j, pl.ds(start + c * sub, sub), :]
            q = q_ref[0, j, first_query(c):, :]
            return lax.dot_general(k, q, (((1,), (1,)), ((), ())), preferred_element_type=F32)

        def tail(full, lo, new):
            return new if lo == 0 else jnp.concatenate([full[:, :lo], new], axis=1)

        scores = {cj: score(*cj) for cj in chunks[:FOX_AHEAD]}
        for n, (c, j) in enumerate(chunks):
                if n + FOX_AHEAD < len(chunks):
                    nxt = chunks[n + FOX_AHEAD]
                    scores[nxt] = score(*nxt)
                m, l, acc = carry[j]
                lo = first_query(c)
                vt = v_ref[0, :, pl.ds(start + c * sub, sub)]
                s = scores.pop((c, j))
                if diag:
                    key = lax.broadcasted_iota(I32, s.shape, 0)
                    qry = lax.broadcasted_iota(I32, s.shape, 1)
                    s = jnp.where(key <= qry, s, NEG)
                if online:
                    m_new = jnp.maximum(m[:, lo:], jnp.max(s, axis=0, keepdims=True))
                    alpha = jnp.exp2(m[:, lo:] - m_new)
                    p = jnp.exp2(s - m_new)
                    l_new = alpha * l[:, lo:] + jnp.sum(p, axis=0, keepdims=True)
                    acc_new = alpha * acc[:, lo:] + jnp.dot(vt, p.astype(BF16), preferred_element_type=F32)
                    m = tail(m, lo, m_new)
                else:
                    p = jnp.exp2(s)
                    l_new = l[:, lo:] + jnp.sum(p, axis=0, keepdims=True)
                    acc_new = acc[:, lo:] + jnp.dot(vt, p.astype(BF16), preferred_element_type=F32)
                carry[j] = (m, tail(l, lo, l_new), tail(acc, lo, acc_new))
        return tuple(carry)

    init = tuple((jnp.full((1, tq), NEG, F32), jnp.zeros((1, tq), F32), jnp.zeros((LANES, tq), F32))
                 for _ in range(HEADS_PER_TILE))
    assert TK_WIDE == tq and TK_WIDE % TK_SUB == 0
    per_wide = TK_WIDE // TK_SUB
    first = first_ref[(pl.program_id(0) * pl.num_programs(1) + pl.program_id(1)) * pl.num_programs(2) + qi]
    live = qi * per_wide - first
    n_narrow = lax.rem(live, per_wide)
    narrow_start = pl.multiple_of(first * TK_SUB, TK_SUB)
    carry = lax.switch(
        n_narrow,
        [lambda c: c] + [lambda c, w=w: step(narrow_start, w * TK_SUB, c, False) for w in range(1, per_wide)],
        init)
    wide_start = (first + n_narrow) * TK_SUB
    carry = lax.fori_loop(
        0, lax.div(live, per_wide),
        lambda i, c: step(pl.multiple_of(wide_start + i * TK_WIDE, TK_SUB), TK_WIDE, c, False), carry)
    carry = step(pl.multiple_of(qi * tq, tq), tq, carry, True)
    outs = [acc / l for (_, l, acc) in carry]
    feat = lax.broadcasted_iota(I32, (LANES, tq), 0)
    o_ref[0] = jnp.where(feat < HEAD_DIM, outs[0], outs[1]).T.astype(o_ref.dtype)


def _fox(first_tile, qa, ka, va_t, online):
    B, _, S, _ = qa.shape
    grid_spec = pltpu.PrefetchScalarGridSpec(
        num_scalar_prefetch=1,
        grid=(B, N_PAIRS, S // TQ),
        in_specs=[pl.BlockSpec((1, HEADS_PER_TILE, TQ, LANES), lambda b, hp, i, first: (b, hp, i, 0)),
                  pl.BlockSpec((1, HEADS_PER_TILE, S, LANES), lambda b, hp, i, first: (b, hp, 0, 0)),
                  pl.BlockSpec((1, LANES, S), lambda b, hp, i, first: (b, hp, 0))],
        out_specs=pl.BlockSpec((1, TQ, LANES), lambda b, hp, i, first: (b, i, hp)),
    )
    return pl.pallas_call(
        functools.partial(_fox_kernel, online=online),
        grid_spec=grid_spec,
        out_shape=jax.ShapeDtypeStruct((B, S, D_GRP), F32),
        compiler_params=_cparams(("arbitrary", "arbitrary", "arbitrary")),
        name="fox_online" if online else "fox",
    )(first_tile, qa, ka, va_t)


def _dilated_kernel(shift_ref, q_ref, kp_ref, kc_ref, vp_ref, vc_ref, o_ref, qq, kk, vv, qq4, kk4, vv4, osc, lsc,
                    *, bounded):
    u = pl.program_id(1)
    span = q_ref.shape[1]
    qq[...] = q_ref[0]
    kk[0:span, :] = kp_ref[0]
    kk[span:2 * span, :] = kc_ref[0]
    vv[0:span, :] = vp_ref[0]
    vv[span:2 * span, :] = vc_ref[0]
    for src, dst in ((qq, qq4), (kk, kk4), (vv, vv4)):
        part = src.shape[0] // DIL_PRE
        for a in range(DIL_PRE):
            dst[a * part:(a + 1) * part, :] = src[pl.ds(a, part, stride=DIL_PRE), :]

    def rows(buf, buf4, start, n, d):
        if d % DIL_PRE:
            return buf[pl.ds(start, n, stride=d), :]
        part = buf4.shape[0] // DIL_PRE
        a = lax.rem(start, DIL_PRE)
        return buf4[pl.ds(a * part + lax.div(start, DIL_PRE), n, stride=d // DIL_PRE), :]

    lane = lax.broadcasted_iota(I32, (BLOCK, LANES), 1)
    ql = lax.broadcasted_iota(I32, (BLOCK, 2 * BLOCK), 0)
    kl = lax.broadcasted_iota(I32, (BLOCK, 2 * BLOCK), 1)
    dist = ql + BLOCK - kl
    band = (dist >= 0) & (dist <= BLOCK)
    live = -shift_ref[0, 0] if bounded else 0.0
    bias = jnp.where(band, live, NEG)
    bias_first = jnp.where(band & (kl >= BLOCK), live, NEG)

    def scores(q_start, k_start, d, first):
        qs = rows(qq, qq4, q_start, BLOCK, d).astype(BF16)
        ks = rows(kk, kk4, k_start, 2 * BLOCK, d).astype(BF16)
        mask = jnp.where(first, bias_first, bias)
        out = []
        for j in range(HEADS_PER_TILE):
            qj = jnp.where(lane // HEAD_DIM == j, qs, jnp.zeros_like(qs))
            out.append(lax.dot_general(qj, ks, (((1,), (1,)), ((), ())), preferred_element_type=F32) + mask)
        return out

    def finish(s_heads, k_start, d):
        vs = rows(vv, vv4, k_start, 2 * BLOCK, d).astype(BF16)
        o_heads, lse_heads = [], []
        for s in s_heads:
            if bounded:
                p = jnp.exp(s)
                o_heads.append(jnp.dot(p.astype(BF16), vs, preferred_element_type=F32))
                lse_heads.append(jnp.sum(p, axis=-1, keepdims=True))
            else:
                m = jnp.max(s, axis=-1, keepdims=True)
                p = jnp.exp(s - m)
                l = jnp.sum(p, axis=-1, keepdims=True)
                o_heads.append(jnp.dot((p / l).astype(BF16), vs, preferred_element_type=F32))
                lse_heads.append(m + jnp.log(l))
        o = jnp.where(lane < HEAD_DIM, o_heads[0], o_heads[1])
        lse = jnp.where(lane < HEAD_DIM, lse_heads[0], lse_heads[1])
        return o, lse

    for pidx, (window, d) in enumerate(DIL_PATTERNS):
        assert window // d == BLOCK
        unit = d * BLOCK
        n_problems = (span // unit) * d
        assert n_problems % DIL_GROUP == 0

        def body(g, _, pidx=pidx, d=d, unit=unit):
            starts, s_all = [], []
            for t in range(DIL_GROUP):
                idx = g * DIL_GROUP + t
                w = idx // d
                q_start = w * unit + (idx - w * d)
                k_start = span - unit + q_start
                starts.append((q_start, k_start))
                s_all.append(scores(q_start, k_start, d, jnp.logical_and(u == 0, w == 0)))
            for (q_start, k_start), s_heads in zip(starts, s_all):
                o, lse = finish(s_heads, k_start, d)
                osc[pidx, pl.ds(q_start, BLOCK, stride=d), :] = o
                lsc[pidx, pl.ds(q_start, BLOCK, stride=d), :] = lse
            return 0

        lax.fori_loop(0, n_problems // DIL_GROUP, body, 0)

    if bounded:
        num = osc[0] + osc[1] + osc[2]
        den = lsc[0] + lsc[1] + lsc[2]
    else:
        mx = jnp.maximum(jnp.maximum(lsc[0], lsc[1]), lsc[2])
        num = jnp.zeros((span, LANES), F32)
        den = jnp.zeros((span, LANES), F32)
        for pidx in range(len(DIL_PATTERNS)):
            e = jnp.exp(lsc[pidx] - mx)
            num = num + e * osc[pidx]
            den = den + e
    o_ref[0] = (num / den).astype(o_ref.dtype)


def _dilated(shift, qb, kb, vb, bounded):
    B, S, _ = qb.shape
    span = DIL_SPAN
    cur = pl.BlockSpec((1, span, LANES), lambda b, u, hp: (b, u, hp))
    prev = pl.BlockSpec((1, span, LANES), lambda b, u, hp: (b, jnp.maximum(u - 1, 0), hp))
    return pl.pallas_call(
        functools.partial(_dilated_kernel, bounded=bounded),
        grid=(B, S // span, N_PAIRS),
        in_specs=[pl.BlockSpec(memory_space=pltpu.SMEM), cur, prev, cur, prev, cur],
        out_specs=cur,
        out_shape=jax.ShapeDtypeStruct((B, S, D_GRP), F32),
        scratch_shapes=[pltpu.VMEM((span, LANES), F32),
                        pltpu.VMEM((2 * span, LANES), F32), pltpu.VMEM((2 * span, LANES), F32),
                        pltpu.VMEM((span, LANES), F32),
                        pltpu.VMEM((2 * span, LANES), F32), pltpu.VMEM((2 * span, LANES), F32),
                        pltpu.VMEM((len(DIL_PATTERNS), span, LANES), F32),
                        pltpu.VMEM((len(DIL_PATTERNS), span, LANES), F32)],
        compiler_params=_cparams(("arbitrary", "arbitrary", "arbitrary")),
        name="dilated" if bounded else "dilated_exact",
    )(shift, qb, kb, kb, vb, vb)


def _store_row_tiles(ref, stage_ref, x, first_row=0):
    n = x.shape[0]
    for c in range(ROW_CHUNKS):
        stage_ref[pl.ds(first_row * ROW_CHUNKS + c, n, stride=ROW_CHUNKS), :] = x[:, c * LANES:(c + 1) * LANES]
    rows = slice(first_row * ROW_CHUNKS, (first_row + n) * ROW_CHUNKS)
    ref[rows, :] = stage_ref[rows, :].astype(ref.dtype)


def _load_row_tiles(ref, stage_ref, n, first_row=0):
    rows = slice(first_row * ROW_CHUNKS, (first_row + n) * ROW_CHUNKS)
    stage_ref[rows, :] = ref[rows, :].astype(F32)
    return jnp.concatenate([stage_ref[pl.ds(first_row * ROW_CHUNKS + c, n, stride=ROW_CHUNKS), :]
                            for c in range(ROW_CHUNKS)], axis=1)


def _row_tile_copy(src_ref, src_row, dst_ref, dst_row, sem):
    src = src_ref.at[pl.ds(pl.multiple_of(src_row * ROW_CHUNKS, ROW_CHUNKS), ROW_CHUNKS), :]
    dst = dst_ref.at[pl.ds(pl.multiple_of(dst_row * ROW_CHUNKS, ROW_CHUNKS), ROW_CHUNKS), :]
    return pltpu.make_async_copy(src, dst, sem)


def _outproj_kernel(oa_ref, ob_ref, x_ref, gfox_ref, gdil_ref, wo_ref, gffn_ref, wr_ref, br_ref, upper_ref,
                    h_ref, hn_ref, eid_ref, gate_ref, rank_ref, cnt_ref, run_ref, stage_ref):
    @pl.when(pl.program_id(0) == 0)
    def _():
        run_ref[...] = jnp.zeros_like(run_ref)

    def norm(y, g):
        ms = jnp.mean(y * y, axis=-1, keepdims=True)
        return y * lax.rsqrt(ms + EPS) * g

    a = norm(oa_ref[...], gfox_ref[...]).astype(BF16)
    b = norm(ob_ref[...], gdil_ref[...]).astype(BF16)
    mix = (jnp.dot(a, wo_ref[0:D_GRP, :], preferred_element_type=F32)
           + jnp.dot(b, wo_ref[D_GRP:2 * D_GRP, :], preferred_element_type=F32))
    h = x_ref[...] + mix
    h_ref[...] = h
    hn = norm(h, gffn_ref[...])
    _store_row_tiles(hn_ref, stage_ref, hn)

    z = lax.dot_general(wr_ref[...], hn, (((1,), (1,)), ((), ())), preferred_element_type=F32,
                        precision=lax.Precision.HIGHEST) + br_ref[...]
    tm = z.shape[1]
    best = z[0:1, :]
    g_sel = jnp.zeros((1, tm), I32)
    for g in range(1, N_GROUPS):
        better = z[g:g + 1, :] > best
        g_sel = jnp.where(better, g, g_sel)
        best = jnp.maximum(best, z[g:g + 1, :])
    den = jnp.zeros((1, tm), F32)
    for g in range(N_GROUPS):
        den = den + jnp.exp(z[g:g + 1, :] - best)
    pg_top = 1.0 / den

    ze = jnp.zeros((EXPERTS_PER_GROUP, tm), F32)
    for g in range(N_GROUPS):
        ze = jnp.where(g_sel == g, z[8 + g * EXPERTS_PER_GROUP:8 + (g + 1) * EXPERTS_PER_GROUP, :], ze)
    e_iota = lax.broadcasted_iota(I32, ze.shape, 0)
    v1 = jnp.max(ze, axis=0, keepdims=True)
    i1 = jnp.min(jnp.where(ze == v1, e_iota, EXPERTS_PER_GROUP), axis=0, keepdims=True)
    ze2 = jnp.where(e_iota == i1, -jnp.inf, ze)
    v2 = jnp.max(ze2, axis=0, keepdims=True)
    i2 = jnp.min(jnp.where(ze2 == v2, e_iota, EXPERTS_PER_GROUP), axis=0, keepdims=True)
    e2 = jnp.exp(v2 - v1)
    inv = 1.0 / (1.0 + e2)
    gate1 = inv * pg_top
    gate2 = e2 * inv * pg_top
    eid1 = g_sel * EXPERTS_PER_GROUP + i1
    eid2 = g_sel * EXPERTS_PER_GROUP + i2

    x_iota = lax.broadcasted_iota(I32, (N_EXPERTS, tm), 0)
    hot1 = x_iota == eid1
    hot2 = x_iota == eid2
    multi = jnp.logical_or(hot1, hot2)
    before = jnp.dot(multi.astype(BF16), upper_ref[...], preferred_element_type=F32)
    slot = before + run_ref[:, 0:1]
    rank1 = jnp.sum(jnp.where(hot1, slot, 0.0), axis=0, keepdims=True)
    rank2 = jnp.sum(jnp.where(hot2, slot, 0.0), axis=0, keepdims=True)
    run_ref[...] = run_ref[...] + jnp.sum(multi.astype(F32), axis=1, keepdims=True)

    eid_ref[...] = jnp.concatenate([eid1, eid2], axis=0)
    gate_ref[...] = jnp.concatenate([gate1, gate2], axis=0)
    rank_ref[...] = jnp.concatenate([rank1, rank2], axis=0).astype(I32)
    cnt_ref[...] = run_ref[...].astype(I32)


def _outproj(oa, ob, x2, gfox, gdil, w_out, gffn, w_r, b_r, upper):
    T, D = x2.shape
    tm = TM_OUT
    const = lambda shape: pl.BlockSpec(shape, lambda i: (0,) * len(shape))
    tok = lambda w: pl.BlockSpec((tm, w), lambda i: (i, 0))
    lanes2 = pl.BlockSpec((2, tm), lambda i: (0, i))
    return pl.pallas_call(
        _outproj_kernel,
        grid=(T // tm,),
        in_specs=[tok(D_GRP), tok(D_GRP), tok(D), const((1, D_GRP)), const((1, D_GRP)), const((D, D)),
                  const((1, D)), const((ROUTER_ROWS, D)), const((ROUTER_ROWS, 1)), const((tm, tm))],
        out_specs=[tok(D), pl.BlockSpec((tm * ROW_CHUNKS, LANES), lambda i: (i, 0)),
                   lanes2, lanes2, lanes2, const((N_EXPERTS, LANES))],
        out_shape=[jax.ShapeDtypeStruct((T, D), F32), jax.ShapeDtypeStruct((T * ROW_CHUNKS, LANES), BF16),
                   jax.ShapeDtypeStruct((2, T), I32), jax.ShapeDtypeStruct((2, T), F32),
                   jax.ShapeDtypeStruct((2, T), I32), jax.ShapeDtypeStruct((N_EXPERTS, LANES), I32)],
        scratch_shapes=[pltpu.VMEM((N_EXPERTS, LANES), F32), pltpu.VMEM((tm * ROW_CHUNKS, LANES), F32)],
        compiler_params=_cparams(("arbitrary",)),
        name="outproj",
    )(oa, ob, x2, gfox, gdil, w_out, gffn, w_r, b_r, upper)


def _scatter_kernel(starts_ref, cnt_ref, pos0_ref, pos1_ref, hn_ref, xs_ref, ring, sems, zero_sem, *, n_steps):
    i = pl.program_id(0)
    tm = hn_ref.shape[0] // ROW_CHUNKS
    slot = lax.rem(i, 2)

    def wait_slot(s):
        for _ in range(2):
            pltpu.make_async_copy(ring.at[s], xs_ref.at[pl.ds(0, tm * ROW_CHUNKS), :], sems.at[s]).wait()

    @pl.when(i >= 2)
    def _():
        wait_slot(slot)

    ring[slot] = hn_ref[...]

    def start(r, _):
        for k, pos_ref in enumerate((pos0_ref, pos1_ref)):
            _row_tile_copy(ring.at[slot], r, xs_ref, pos_ref[r], sems.at[slot]).start(priority=k)
        return 0

    lax.fori_loop(0, tm, start, 0, unroll=DMA_UNROLL)

    @pl.when(i == n_steps - 1)
    def _():
        wait_slot(slot)
        if n_steps > 1:
            wait_slot(1 - slot)
        ring[0] = jnp.zeros((tm * ROW_CHUNKS, LANES), ring.dtype)

        def pad_expert(e, _, wait):
            n_pad = lax.rem(TILE_E - lax.rem(cnt_ref[e], TILE_E), TILE_E)
            first = starts_ref[e] + cnt_ref[e]
            size = TILE_E // 2
            while size >= 1:
                row0 = first + (n_pad & ~(2 * size - 1))

                @pl.when((n_pad & size) != 0)
                def _(size=size, row0=row0):
                    copy = pltpu.make_async_copy(
                        ring.at[0, pl.ds(0, size * ROW_CHUNKS), :],
                        xs_ref.at[pl.ds(pl.multiple_of(row0 * ROW_CHUNKS, ROW_CHUNKS), size * ROW_CHUNKS), :],
                        zero_sem)
                    copy.wait() if wait else copy.start()

                size //= 2
            return 0

        lax.fori_loop(0, N_EXPERTS, functools.partial(pad_expert, wait=False), 0)
        last = N_EXPERTS - 1
        used_rows = starts_ref[last] + cnt_ref[last] + lax.rem(TILE_E - lax.rem(cnt_ref[last], TILE_E), TILE_E)
        n_tail = xs_ref.shape[0] // (tm * ROW_CHUNKS) - used_rows // tm

        def tail_copy(t):
            row0 = pl.multiple_of((used_rows + t * tm) * ROW_CHUNKS, tm * ROW_CHUNKS)
            return pltpu.make_async_copy(ring.at[0], xs_ref.at[pl.ds(row0, tm * ROW_CHUNKS), :], zero_sem)

        lax.fori_loop(0, n_tail, lambda t, c: (tail_copy(t).start(), c)[1], 0)
        lax.fori_loop(0, N_EXPERTS, functools.partial(pad_expert, wait=True), 0)
        lax.fori_loop(0, n_tail, lambda t, c: (tail_copy(t).wait(), c)[1], 0)


def _scatter(starts, cnt, pos, hn, n_rows):
    T = hn.shape[0] // ROW_CHUNKS
    tm = TM_ROWS
    assert TILE_E % tm == 0 and TILE_E // 2 <= tm
    n_steps = T // tm
    grid_spec = pltpu.PrefetchScalarGridSpec(
        num_scalar_prefetch=2,
        grid=(n_steps,),
        in_specs=[pl.BlockSpec((tm,), lambda i, starts, cnt: (i,), memory_space=pltpu.SMEM),
                  pl.BlockSpec((tm,), lambda i, starts, cnt: (i,), memory_space=pltpu.SMEM),
                  pl.BlockSpec((tm * ROW_CHUNKS, LANES), lambda i, starts, cnt: (i, 0))],
        out_specs=pl.BlockSpec(memory_space=pl.ANY),
        scratch_shapes=[pltpu.VMEM((2, tm * ROW_CHUNKS, LANES), hn.dtype),
                        pltpu.SemaphoreType.DMA((2,)), pltpu.SemaphoreType.DMA(())],
    )
    return pl.pallas_call(
        functools.partial(_scatter_kernel, n_steps=n_steps),
        grid_spec=grid_spec,
        out_shape=jax.ShapeDtypeStruct((n_rows * ROW_CHUNKS, LANES), hn.dtype),
        compiler_params=_cparams(("arbitrary",)),
        name="scatter_rows",
    )(starts, cnt, pos[0], pos[1], hn)


def _experts_kernel(te_ref, tb_ref, tr_ref, xs_ref, wg_ref, wu_ref, wd_ref, y_ref, wg_bf, wu_bf, wd_bf, stage_ref):
    del tb_ref
    i = pl.program_id(0)
    rows = tr_ref[i]
    new_expert = jnp.logical_or(i == 0, te_ref[i] != te_ref[jnp.maximum(i - 1, 0)])

    @pl.when(jnp.logical_and(rows > 0, new_expert))
    def _():
        wg_bf[...] = wg_ref[0].astype(BF16)
        wu_bf[...] = wu_ref[0].astype(BF16)
        wd_bf[...] = wd_ref[0].astype(BF16)

    half = TILE_E // 2
    for part in range(2):
        @pl.when(rows > part * half)
        def _(part=part):
            x = _load_row_tiles(xs_ref, stage_ref, half, part * half).astype(BF16)
            g = jnp.dot(x, wg_bf[...], preferred_element_type=F32)
            up = jnp.dot(x, wu_bf[...], preferred_element_type=F32)
            hmid = (g * jax.nn.sigmoid(g) * up).astype(BF16)
            _store_row_tiles(y_ref, stage_ref, jnp.dot(hmid, wd_bf[...], preferred_element_type=F32), part * half)

    @pl.when(jnp.logical_and(rows > 0, rows <= half))
    def _():
        y_ref[half * ROW_CHUNKS:, :] = jnp.zeros((half * ROW_CHUNKS, LANES), y_ref.dtype)


def _experts(tile_expert, tile_block, tile_rows, xs, w_gate, w_up, w_down):
    n_tiles = tile_expert.shape[0]
    D = D_MODEL
    rows_spec = pl.BlockSpec((TILE_E * ROW_CHUNKS, LANES), lambda i, te, tb, tr: (tb[i], 0))
    weights = lambda shape: pl.BlockSpec(shape, lambda i, te, tb, tr: (te[i], 0, 0))
    grid_spec = pltpu.PrefetchScalarGridSpec(
        num_scalar_prefetch=3,
        grid=(n_tiles,),
        in_specs=[rows_spec, weights((1, D, D_EXPERT)), weights((1, D, D_EXPERT)), weights((1, D_EXPERT, D))],
        out_specs=rows_spec,
        scratch_shapes=[pltpu.VMEM((D, D_EXPERT), BF16), pltpu.VMEM((D, D_EXPERT), BF16),
                        pltpu.VMEM((D_EXPERT, D), BF16), pltpu.VMEM((TILE_E * ROW_CHUNKS, LANES), F32)],
    )
    return pl.pallas_call(
        _experts_kernel,
        grid_spec=grid_spec,
        out_shape=jax.ShapeDtypeStruct(xs.shape, xs.dtype),
        input_output_aliases={3: 0},
        compiler_params=_cparams(("arbitrary",)),
        name="experts",
    )(tile_expert, tile_block, tile_rows, xs, w_gate, w_up, w_down)


def _combine_kernel(pos0_ref, pos1_ref, next0_ref, next1_ref, h_ref, gate_ref, y_ref, o_ref, ybuf, sems, stage_ref,
                    *, n_steps):
    i = pl.program_id(0)
    tm = h_ref.shape[0]
    slot = lax.rem(i, 2)

    def gather(p_refs, s):
        def start(r, _):
            for k, p_ref in enumerate(p_refs):
                _row_tile_copy(y_ref, p_ref[r], ybuf.at[s, k], r, sems.at[s]).start(priority=k)
            return 0

        lax.fori_loop(0, tm, start, 0, unroll=DMA_UNROLL)

    @pl.when(i == 0)
    def _():
        gather((pos0_ref, pos1_ref), slot)

    @pl.when(i + 1 < n_steps)
    def _():
        gather((next0_ref, next1_ref), 1 - slot)

    for k in range(2):
        pltpu.make_async_copy(y_ref.at[pl.ds(0, tm * ROW_CHUNKS), :], ybuf.at[slot, k], sems.at[slot]).wait()
    g = gate_ref[...]
    acc = h_ref[...] + g[:, 0:1] * _load_row_tiles(ybuf.at[slot, 0], stage_ref, tm)
    o_ref[...] = acc + g[:, 1:2] * _load_row_tiles(ybuf.at[slot, 1], stage_ref, tm)


def _combine(pos, h, gate_t, y):
    T, D = h.shape
    tm = TM_ROWS
    n_steps = T // tm
    return pl.pallas_call(
        functools.partial(_combine_kernel, n_steps=n_steps),
        grid=(n_steps,),
        in_specs=[pl.BlockSpec((tm,), lambda i: (i,), memory_space=pltpu.SMEM),
                  pl.BlockSpec((tm,), lambda i: (i,), memory_space=pltpu.SMEM),
                  pl.BlockSpec((tm,), lambda i: (jnp.minimum(i + 1, n_steps - 1),), memory_space=pltpu.SMEM),
                  pl.BlockSpec((tm,), lambda i: (jnp.minimum(i + 1, n_steps - 1),), memory_space=pltpu.SMEM),
                  pl.BlockSpec((tm, D), lambda i: (i, 0)),
                  pl.BlockSpec((tm, 2), lambda i: (i, 0)),
                  pl.BlockSpec(memory_space=pl.ANY)],
        out_specs=pl.BlockSpec((tm, D), lambda i: (i, 0)),
        out_shape=jax.ShapeDtypeStruct((T, D), F32),
        scratch_shapes=[pltpu.VMEM((2, 2, tm * ROW_CHUNKS, LANES), y.dtype), pltpu.SemaphoreType.DMA((2,)),
                        pltpu.VMEM((tm * ROW_CHUNKS, LANES), F32)],
        compiler_params=_cparams(("arbitrary",)),
        name="combine",
    )(pos[0], pos[1], pos[0], pos[1], h, gate_t, y)


def _rope_tables(S):
    inv_freq = 1.0 / (ROPE_THETA ** (np.arange(0, HEAD_DIM, 2, dtype=np.float64) / HEAD_DIM))
    ang = np.arange(S, dtype=np.float64)[:, None] * inv_freq[None, :]
    widen = lambda t: jnp.asarray(np.pad(t, ((0, 0), (0, LANES - HEAD_DIM // 2))).astype(np.float32))
    return widen(np.cos(ang)), widen(np.sin(ang))


def _layer(x, norm_mix, w_in, b_forget, q_norm_fox, k_norm_fox, q_norm_dil, k_norm_dil,
           out_norm_fox, out_norm_dil, w_out, norm_ffn, w_router_group, b_router_group,
           w_router_expert, b_router_expert, w_gate, w_up, w_down):
    B, S, D = x.shape
    T = B * S
    n_main = 6 * D_GRP

    w_main = w_in.astype(BF16)
    w_vt = w_main[:, 2 * D_GRP:3 * D_GRP].T
    w_f = jnp.pad(w_main[:, n_main:], ((0, 0), (0, LANES - N_HEADS)))
    b_f = jnp.pad(b_forget, (0, LANES - N_HEADS))[None, :]
    per_head = lambda g: jnp.tile(g, N_HEADS)[None, :]
    bd = jnp.kron(jnp.eye(N_HEADS, dtype=F32), jnp.ones((HEAD_DIM, HEAD_DIM), F32)).astype(BF16)
    cos_t, sin_t = _rope_tables(S)
    tri = jnp.tril(jnp.ones((TRI_ROWS, TRI_ROWS), F32)).astype(BF16)
    upper = jnp.triu(jnp.ones((TM_OUT, TM_OUT), F32), k=1).astype(BF16)
    w_r = jnp.concatenate([
        jnp.pad(w_router_group.T, ((0, 8 - N_GROUPS), (0, 0))),
        w_router_expert.transpose(0, 2, 1).reshape(N_EXPERTS, D)], axis=0)
    b_r = jnp.concatenate([jnp.pad(b_router_group, (0, 8 - N_GROUPS)), b_router_expert.reshape(-1)])[:, None]

    bound = (HEAD_DIM / math.sqrt(HEAD_DIM)) * LOG2E * jnp.max(jnp.abs(q_norm_fox)) * jnp.max(jnp.abs(k_norm_fox))
    shift = FOX_BOUND_SLACK * bound + 1.0
    qa, ka, va, qb, kb, vb, c2 = _inproj(
        x, norm_mix[None, :], w_main, w_vt, w_f, b_f, jnp.full((1, LANES), shift, F32),
        per_head(q_norm_fox), per_head(k_norm_fox),
        per_head(q_norm_dil), per_head(k_norm_dil), bd, cos_t, sin_t, tri)
    n_q, n_k = S // TQ, S // TK_SUB
    c_first = c2[:, ::TQ, :].reshape(B, n_q, 1, N_PAIRS, HEADS_PER_TILE)
    c_last = c2[:, TK_SUB - 1::TK_SUB, :].reshape(B, 1, n_k, N_PAIRS, HEADS_PER_TILE)
    all_zero = jnp.all(c_first - c_last < -FOX_ZERO_BITS, axis=-1)
    live_so_far = jnp.einsum("bqkp,kj->bqjp", 1 - all_zero.astype(I32),
                             jnp.triu(jnp.ones((n_k, n_k), I32)))
    leading = (live_so_far == 0).astype(I32)
    before = (jnp.arange(n_k) * TK_SUB + TK_SUB <= jnp.arange(n_q)[:, None] * TQ)
    first_tile = jnp.sum(leading * before[None, :, :, None].astype(I32), axis=2)
    first_tile = first_tile.transpose(0, 2, 1).reshape(-1).astype(I32)
    oa = lax.cond(2.0 * shift <= FOX_SAFE_SPAN,
                  lambda: _fox(first_tile, qa, ka, va, online=False),
                  lambda: _fox(jnp.zeros_like(first_tile), qa, ka, va, online=True))
    dil_shift = (FOX_BOUND_SLACK * (HEAD_DIM / math.sqrt(HEAD_DIM))
                 * jnp.max(jnp.abs(q_norm_dil)) * jnp.max(jnp.abs(k_norm_dil)) + 1.0)
    dil_shift_arr = jnp.full((1, 1), dil_shift, F32)
    ob = lax.cond(2.0 * dil_shift <= DIL_SAFE_SPAN,
                  lambda: _dilated(dil_shift_arr, qb, kb, vb, bounded=True),
                  lambda: _dilated(dil_shift_arr, qb, kb, vb, bounded=False))

    h, hn, eid, gate, rank, cnt = _outproj(
        oa.reshape(T, D_GRP), ob.reshape(T, D_GRP), x.reshape(T, D), out_norm_fox[None, :],
        out_norm_dil[None, :], w_out.astype(BF16), norm_ffn[None, :], w_r, b_r, upper)

    counts = cnt[:, 0]
    padded = ((counts + TILE_E - 1) // TILE_E) * TILE_E
    ends = jnp.cumsum(padded)
    starts = ends - padded
    is_expert = eid[:, :, None] == jnp.arange(N_EXPERTS, dtype=I32)
    pos = jnp.sum(jnp.where(is_expert, starts, 0), axis=-1) + rank
    n_tiles = (2 * T) // TILE_E + N_EXPERTS
    tile_index = jnp.arange(n_tiles, dtype=I32)
    tile_valid = tile_index * TILE_E < ends[-1]
    tile_block = jnp.minimum(tile_index, ends[-1] // TILE_E - 1)
    in_region = ends[None, :] <= (tile_block * TILE_E)[:, None]
    tile_expert = jnp.sum(in_region.astype(I32), axis=1)
    expert_of_tile = tile_expert[:, None] == jnp.arange(N_EXPERTS, dtype=I32)
    rows_before = tile_block * TILE_E - jnp.sum(jnp.where(expert_of_tile, starts, 0), axis=1)
    tile_rows = jnp.clip(jnp.sum(jnp.where(expert_of_tile, counts, 0), axis=1) - rows_before, 0, TILE_E)
    tile_rows = jnp.where(tile_valid, tile_rows, 0).astype(I32)

    xs = _scatter(starts, counts, pos, hn, n_tiles * TILE_E)
    y = _experts(tile_expert, tile_block, tile_rows, xs, w_gate, w_up, w_down)
    out = _combine(pos, h, gate.T, y)
    return out.reshape(B, S, D)


def kernel(x, norm_mix, w_in, b_forget, q_norm_fox, k_norm_fox, q_norm_dil, k_norm_dil, out_norm_fox,
           out_norm_dil, w_out, norm_ffn, w_router_group, b_router_group, w_router_expert,
           b_router_expert, w_gate, w_up, w_down):
    h = x
    for l in range(norm_mix.shape[0]):
        h = _layer(h, norm_mix[l], w_in[l], b_forget[l], q_norm_fox[l], k_norm_fox[l], q_norm_dil[l],
                   k_norm_dil[l], out_norm_fox[l], out_norm_dil[l], w_out[l], norm_ffn[l],
                   w_router_group[l], b_router_group[l], w_router_expert[l], b_router_expert[l],
                   w_gate[l], w_up[l], w_down[l])
    return h
```

```python
import functools
import math

import jax
import jax.numpy as jnp
import numpy as np
from jax import lax
from jax.experimental import pallas as pl
from jax.experimental.pallas import tpu as pltpu

F32 = jnp.float32
BF16 = jnp.bfloat16
I32 = jnp.int32

D_MODEL = 1024
HEAD_DIM = 64
N_HEADS = 8
D_GRP = N_HEADS * HEAD_DIM
LANES = 128
HEADS_PER_TILE = LANES // HEAD_DIM
N_PAIRS = D_GRP // LANES
DIL_PATTERNS = ((128, 1), (512, 4), (2048, 16))
BLOCK = 128
ROPE_THETA = 10000.0
N_GROUPS = 4
EXPERTS_PER_GROUP = 8
N_EXPERTS = N_GROUPS * EXPERTS_PER_GROUP
D_EXPERT = 512
EPS = 1e-6
NEG = -1e30
LOG2E = 1.4426950408889634

TM_IN = 1024
TRI_ROWS = 256
TQ = 1024
TK_WIDE = 1024
TK_SUB = 256
FOX_AHEAD = 4
FOX_ZERO_BITS = 160.0
FOX_BOUND_SLACK = 1.02
DIL_SAFE_SPAN = 69.0
FOX_SAFE_SPAN = 100.0
DIL_SPAN = 2048
DIL_GROUP = 16
DIL_PRE = 4
TM_OUT = 512
TM_ROWS = 512
TILE_E = 512
ROUTER_ROWS = 8 + N_EXPERTS
ROW_CHUNKS = D_MODEL // LANES
DMA_UNROLL = 8
VMEM_LIMIT = 56 * 1024 * 1024


def _cparams(sem, flags=None):
    return pltpu.CompilerParams(dimension_semantics=sem, vmem_limit_bytes=VMEM_LIMIT, flags=flags)


def _inproj_kernel(x_ref, gmix_ref, w_ref, wvt_ref, wf_ref, bf_ref, shift_ref, gqa_ref, gka_ref, gqb_ref, gkb_ref,
                   bd_ref, cos_ref, sin_ref, tri_ref,
                   qa_ref, ka_ref, va_ref, qb_ref, kb_ref, vb_ref, c_ref, carry_ref):
    @pl.when(pl.program_id(1) == 0)
    def _():
        carry_ref[...] = jnp.zeros_like(carry_ref)

    x = x_ref[0]
    ms = jnp.mean(x * x, axis=-1, keepdims=True)
    xn = (x * lax.rsqrt(ms + EPS) * gmix_ref[...]).astype(BF16)

    def seg(j):
        return jnp.dot(xn, w_ref[:, j * D_GRP:(j + 1) * D_GRP], preferred_element_type=F32)

    def head_norm(y, g_ref, scale):
        ss = jnp.dot((y * y).astype(BF16), bd_ref[...], preferred_element_type=F32) * (1.0 / HEAD_DIM)
        return y * lax.rsqrt(ss + EPS) * (g_ref[...] * scale)

    lane = lax.broadcasted_iota(I32, (x.shape[0], LANES), 1)
    first_half = (lane % HEAD_DIM) < (HEAD_DIM // 2)

    def spread(t):
        t = t + pltpu.roll(t, HEAD_DIM // 2, 1)
        return t + pltpu.roll(t, HEAD_DIM, 1)

    cos = spread(cos_ref[...])
    sin = jnp.where(first_half, -1.0, 1.0) * spread(sin_ref[...])

    def rope(y):
        outs = []
        for j in range(N_PAIRS):
            ys = y[:, j * LANES:(j + 1) * LANES]
            partner = jnp.where(first_half, pltpu.roll(ys, LANES - HEAD_DIM // 2, 1),
                                pltpu.roll(ys, HEAD_DIM // 2, 1))
            outs.append(ys * cos + partner * sin)
        return jnp.concatenate(outs, axis=1)

    scale = 1.0 / math.sqrt(HEAD_DIM)
    va_ref[0] = lax.dot_general(wvt_ref[...], xn, (((1,), (1,)), ((), ())),
                                preferred_element_type=F32).astype(va_ref.dtype)
    qb_ref[0] = rope(head_norm(seg(3), gqb_ref, scale)).astype(qb_ref.dtype)
    kb_ref[0] = rope(head_norm(seg(4), gkb_ref, 1.0)).astype(kb_ref.dtype)
    vb_ref[0] = seg(5).astype(vb_ref.dtype)

    fa = jnp.dot(xn, wf_ref[...], preferred_element_type=F32) + bf_ref[...]
    logf = jnp.minimum(fa, 0.0) - jnp.log1p(jnp.exp(-jnp.abs(fa)))
    hi = logf.astype(BF16)
    mid = (logf - hi.astype(F32)).astype(BF16)
    lo = (logf - hi.astype(F32) - mid.astype(F32)).astype(BF16)
    pieces = jnp.concatenate([hi, mid, lo], axis=1)
    carry = carry_ref[...]
    blocks = []
    for j in range(x.shape[0] // TRI_ROWS):
        parts = jnp.dot(tri_ref[...], pieces[j * TRI_ROWS:(j + 1) * TRI_ROWS, :], preferred_element_type=F32)
        blk = parts[:, :LANES] + parts[:, LANES:2 * LANES] + parts[:, 2 * LANES:] + carry
        carry = blk[TRI_ROWS - 1:, :]
        blocks.append(blk)
    c = jnp.concatenate(blocks, axis=0)
    carry_ref[...] = carry

    qa = head_norm(seg(0), gqa_ref, scale * LOG2E)
    ka = head_norm(seg(1), gka_ref, 1.0)
    c2 = c * LOG2E
    c_ref[0] = c2[:, :N_HEADS]
    ones = (jnp.where((lane >= HEAD_DIM + 3) & (lane < HEAD_DIM + 6), 1.0, 0.0)
            - jnp.where(lane == HEAD_DIM + 6, shift_ref[...], 0.0))
    for h in range(N_HEADS):
        cb = jnp.broadcast_to(c2[:, h:h + 1], (x.shape[0], LANES))
        hi = cb.astype(BF16).astype(F32)
        mid = (cb - hi).astype(BF16).astype(F32)
        lo = cb - hi - mid
        pieces = jnp.where(lane == HEAD_DIM, hi, jnp.where(lane == HEAD_DIM + 1, mid,
                           jnp.where(lane == HEAD_DIM + 2, lo, 0.0)))
        q_extra = pieces + ones
        k_extra = jnp.where(((lane >= HEAD_DIM) & (lane < HEAD_DIM + 3)) | (lane == HEAD_DIM + 6), 1.0, 0.0) \
            - pltpu.roll(pieces, 3, 1)
        j, odd = divmod(h, HEADS_PER_TILE)
        qp = qa[:, j * LANES:(j + 1) * LANES]
        kp = ka[:, j * LANES:(j + 1) * LANES]
        if odd:
            qp = pltpu.roll(qp, HEAD_DIM, 1)
            kp = pltpu.roll(kp, HEAD_DIM, 1)
        qa_ref[0, h] = jnp.where(lane < HEAD_DIM, qp, q_extra).astype(qa_ref.dtype)
        ka_ref[0, h] = jnp.where(lane < HEAD_DIM, kp, k_extra).astype(ka_ref.dtype)


def _inproj(x, gmix, w_main, w_vt, w_f, b_f, shift, gqa, gka, gqb, gkb, bd, cos_t, sin_t, tri):
    B, S, D = x.shape
    tm = TM_IN
    const = lambda shape: pl.BlockSpec(shape, lambda b, i: (0,) * len(shape))
    tok = lambda w, dt: jax.ShapeDtypeStruct((B, S, w), dt)
    tok_spec = lambda w: pl.BlockSpec((1, tm, w), lambda b, i: (b, i, 0))
    head_spec = pl.BlockSpec((1, N_HEADS, tm, LANES), lambda b, i: (b, 0, i, 0))
    head_shape = jax.ShapeDtypeStruct((B, N_HEADS, S, LANES), BF16)
    return pl.pallas_call(
        _inproj_kernel,
        grid=(B, S // tm),
        in_specs=[tok_spec(D), const((1, D)), const(w_main.shape), const(w_vt.shape), const(w_f.shape),
                  const((1, LANES)), const((1, LANES)),
                  const((1, D_GRP)), const((1, D_GRP)), const((1, D_GRP)), const((1, D_GRP)),
                  const((D_GRP, D_GRP)),
                  pl.BlockSpec((tm, LANES), lambda b, i: (i, 0)),
                  pl.BlockSpec((tm, LANES), lambda b, i: (i, 0)),
                  const(tri.shape)],
        out_specs=[head_spec, head_spec, pl.BlockSpec((1, D_GRP, tm), lambda b, i: (b, 0, i))]
        + [tok_spec(D_GRP)] * 3 + [tok_spec(N_HEADS)],
        out_shape=[head_shape, head_shape, jax.ShapeDtypeStruct((B, D_GRP, S), BF16),
                   tok(D_GRP, F32), tok(D_GRP, F32), tok(D_GRP, F32), tok(N_HEADS, F32)],
        scratch_shapes=[pltpu.VMEM((1, LANES), F32)],
        compiler_params=_cparams(("arbitrary", "arbitrary")),
        name="inproj",
    )(x, gmix, w_main, w_vt, w_f, b_f, shift, gqa, gka, gqb, gkb, bd, cos_t, sin_t, tri)


def _fox_kernel(first_ref, q_ref, k_ref, v_ref, o_ref, *, online):
    qi = pl.program_id(2)
    tq = q_ref.shape[2]

    def step(start, width, carry, diag):
        carry = list(carry)
        sub = min(TK_SUB, width)
        chunks = [(c, j) for c in range(width // sub) for j in range(HEADS_PER_TILE)]
        def first_query(c):
            return c * sub if diag else 0

        def score(c, j):
            k = k_ref[0, j, pl.ds(start + c * sub, sub), :]
            q = q_ref[0, j, first_query(c):, :]
            return lax.dot_general(k, q, (((1,), (1,)), ((), ())), preferred_element_type=F32)

        def tail(full, lo, new):
            return new if lo == 0 else jnp.concatenate([full[:, :lo], new], axis=1)

        scores = {cj: score(*cj) for cj in chunks[:FOX_AHEAD]}
        for n, (c, j) in enumerate(chunks):
                if n + FOX_AHEAD < len(chunks):
                    nxt = chunks[n + FOX_AHEAD]
                    scores[nxt] = score(*nxt)
                m, l, acc = carry[j]
                lo = first_query(c)
                vt = v_ref[0, :, pl.ds(start + c * sub, sub)]
                s = scores.pop((c, j))
                if diag:
                    key = lax.broadcasted_iota(I32, s.shape, 0)
                    qry = lax.broadcasted_iota(I32, s.shape, 1)
                    s = jnp.where(key <= qry, s, NEG)
                if online:
                    m_new = jnp.maximum(m[:, lo:], jnp.max(s, axis=0, keepdims=True))
                    alpha = jnp.exp2(m[:, lo:] - m_new)
                    p = jnp.exp2(s - m_new)
                    l_new = alpha * l[:, lo:] + jnp.sum(p, axis=0, keepdims=True)
                    acc_new = alpha * acc[:, lo:] + jnp.dot(vt, p.astype(BF16), preferred_element_type=F32)
                    m = tail(m, lo, m_new)
                else:
                    p = jnp.exp2(s)
                    l_new = l[:, lo:] + jnp.sum(p, axis=0, keepdims=True)
                    acc_new = acc[:, lo:] + jnp.dot(vt, p.astype(BF16), preferred_element_type=F32)
                carry[j] = (m, tail(l, lo, l_new), tail(acc, lo, acc_new))
        return tuple(carry)

    init = tuple((jnp.full((1, tq), NEG, F32), jnp.zeros((1, tq), F32), jnp.zeros((LANES, tq), F32))
                 for _ in range(HEADS_PER_TILE))
    assert TK_WIDE == tq and TK_WIDE % TK_SUB == 0
    per_wide = TK_WIDE // TK_SUB
    first = first_ref[(pl.program_id(0) * pl.num_programs(1) + pl.program_id(1)) * pl.num_programs(2) + qi]
    live = qi * per_wide - first
    n_narrow = lax.rem(live, per_wide)
    narrow_start = pl.multiple_of(first * TK_SUB, TK_SUB)
    carry = lax.switch(
        n_narrow,
        [lambda c: c] + [lambda c, w=w: step(narrow_start, w * TK_SUB, c, False) for w in range(1, per_wide)],
        init)
    wide_start = (first + n_narrow) * TK_SUB
    carry = lax.fori_loop(
        0, lax.div(live, per_wide),
        lambda i, c: step(pl.multiple_of(wide_start + i * TK_WIDE, TK_SUB), TK_WIDE, c, False), carry)
    carry = step(pl.multiple_of(qi * tq, tq), tq, carry, True)
    outs = [acc / l for (_, l, acc) in carry]
    feat = lax.broadcasted_iota(I32, (LANES, tq), 0)
    o_ref[0] = jnp.where(feat < HEAD_DIM, outs[0], outs[1]).T.astype(o_ref.dtype)


def _fox(first_tile, qa, ka, va_t, online):
    B, _, S, _ = qa.shape
    grid_spec = pltpu.PrefetchScalarGridSpec(
        num_scalar_prefetch=1,
        grid=(B, N_PAIRS, S // TQ),
        in_specs=[pl.BlockSpec((1, HEADS_PER_TILE, TQ, LANES), lambda b, hp, i, first: (b, hp, i, 0)),
                  pl.BlockSpec((1, HEADS_PER_TILE, S, LANES), lambda b, hp, i, first: (b, hp, 0, 0)),
                  pl.BlockSpec((1, LANES, S), lambda b, hp, i, first: (b, hp, 0))],
        out_specs=pl.BlockSpec((1, TQ, LANES), lambda b, hp, i, first: (b, i, hp)),
    )
    return pl.pallas_call(
        functools.partial(_fox_kernel, online=online),
        grid_spec=grid_spec,
        out_shape=jax.ShapeDtypeStruct((B, S, D_GRP), F32),
        compiler_params=_cparams(("arbitrary", "arbitrary", "arbitrary")),
        name="fox_online" if online else "fox",
    )(first_tile, qa, ka, va_t)


def _dilated_kernel(shift_ref, q_ref, kp_ref, kc_ref, vp_ref, vc_ref, o_ref, qq, kk, vv, qq4, kk4, vv4, osc, lsc,
                    *, bounded):
    u = pl.program_id(1)
    span = q_ref.shape[1]
    qq[...] = q_ref[0]
    kk[0:span, :] = kp_ref[0]
    kk[span:2 * span, :] = kc_ref[0]
    vv[0:span, :] = vp_ref[0]
    vv[span:2 * span, :] = vc_ref[0]
    for src, dst in ((qq, qq4), (kk, kk4), (vv, vv4)):
        part = src.shape[0] // DIL_PRE
        for a in range(DIL_PRE):
            dst[a * part:(a + 1) * part, :] = src[pl.ds(a, part, stride=DIL_PRE), :]

    def rows(buf, buf4, start, n, d):
        if d % DIL_PRE:
            return buf[pl.ds(start, n, stride=d), :]
        part = buf4.shape[0] // DIL_PRE
        a = lax.rem(start, DIL_PRE)
        return buf4[pl.ds(a * part + lax.div(start, DIL_PRE), n, stride=d // DIL_PRE), :]

    lane = lax.broadcasted_iota(I32, (BLOCK, LANES), 1)
    ql = lax.broadcasted_iota(I32, (BLOCK, 2 * BLOCK), 0)
    kl = lax.broadcasted_iota(I32, (BLOCK, 2 * BLOCK), 1)
    dist = ql + BLOCK - kl
    band = (dist >= 0) & (dist <= BLOCK)
    live = -shift_ref[0, 0] if bounded else 0.0
    bias = jnp.where(band, live, NEG)
    bias_first = jnp.where(band & (kl >= BLOCK), live, NEG)

    def scores(q_start, k_start, d, first):
        qs = rows(qq, qq4, q_start, BLOCK, d).astype(BF16)
        ks = rows(kk, kk4, k_start, 2 * BLOCK, d).astype(BF16)
        mask = jnp.where(first, bias_first, bias)
        out = []
        for j in range(HEADS_PER_TILE):
            qj = jnp.where(lane // HEAD_DIM == j, qs, jnp.zeros_like(qs))
            out.append(lax.dot_general(qj, ks, (((1,), (1,)), ((), ())), preferred_element_type=F32) + mask)
        return out

    def finish(s_heads, k_start, d):
        vs = rows(vv, vv4, k_start, 2 * BLOCK, d).astype(BF16)
        o_heads, lse_heads = [], []
        for s in s_heads:
            if bounded:
                p = jnp.exp(s)
                o_heads.append(jnp.dot(p.astype(BF16), vs, preferred_element_type=F32))
                lse_heads.append(jnp.sum(p, axis=-1, keepdims=True))
            else:
                m = jnp.max(s, axis=-1, keepdims=True)
                p = jnp.exp(s - m)
                l = jnp.sum(p, axis=-1, keepdims=True)
                o_heads.append(jnp.dot((p / l).astype(BF16), vs, preferred_element_type=F32))
                lse_heads.append(m + jnp.log(l))
        o = jnp.where(lane < HEAD_DIM, o_heads[0], o_heads[1])
        lse = jnp.where(lane < HEAD_DIM, lse_heads[0], lse_heads[1])
        return o, lse

    for pidx, (window, d) in enumerate(DIL_PATTERNS):
        assert window // d == BLOCK
        unit = d * BLOCK
        n_problems = (span // unit) * d
        assert n_problems % DIL_GROUP == 0

        def body(g, _, pidx=pidx, d=d, unit=unit):
            starts, s_all = [], []
            for t in range(DIL_GROUP):
                idx = g * DIL_GROUP + t
                w = idx // d
                q_start = w * unit + (idx - w * d)
                k_start = span - unit + q_start
                starts.append((q_start, k_start))
                s_all.append(scores(q_start, k_start, d, jnp.logical_and(u == 0, w == 0)))
            for (q_start, k_start), s_heads in zip(starts, s_all):
                o, lse = finish(s_heads, k_start, d)
                osc[pidx, pl.ds(q_start, BLOCK, stride=d), :] = o
                lsc[pidx, pl.ds(q_start, BLOCK, stride=d), :] = lse
            return 0

        lax.fori_loop(0, n_problems // DIL_GROUP, body, 0)

    if bounded:
        num = osc[0] + osc[1] + osc[2]
        den = lsc[0] + lsc[1] + lsc[2]
    else:
        mx = jnp.maximum(jnp.maximum(lsc[0], lsc[1]), lsc[2])
        num = jnp.zeros((span, LANES), F32)
        den = jnp.zeros((span, LANES), F32)
        for pidx in range(len(DIL_PATTERNS)):
            e = jnp.exp(lsc[pidx] - mx)
            num = num + e * osc[pidx]
            den = den + e
    o_ref[0] = (num / den).astype(o_ref.dtype)


def _dilated(shift, qb, kb, vb, bounded):
    B, S, _ = qb.shape
    span = DIL_SPAN
    cur = pl.BlockSpec((1, span, LANES), lambda b, u, hp: (b, u, hp))
    prev = pl.BlockSpec((1, span, LANES), lambda b, u, hp: (b, jnp.maximum(u - 1, 0), hp))
    return pl.pallas_call(
        functools.partial(_dilated_kernel, bounded=bounded),
        grid=(B, S // span, N_PAIRS),
        in_specs=[pl.BlockSpec(memory_space=pltpu.SMEM), cur, prev, cur, prev, cur],
        out_specs=cur,
        out_shape=jax.ShapeDtypeStruct((B, S, D_GRP), F32),
        scratch_shapes=[pltpu.VMEM((span, LANES), F32),
                        pltpu.VMEM((2 * span, LANES), F32), pltpu.VMEM((2 * span, LANES), F32),
                        pltpu.VMEM((span, LANES), F32),
                        pltpu.VMEM((2 * span, LANES), F32), pltpu.VMEM((2 * span, LANES), F32),
                        pltpu.VMEM((len(DIL_PATTERNS), span, LANES), F32),
                        pltpu.VMEM((len(DIL_PATTERNS), span, LANES), F32)],
        compiler_params=_cparams(("arbitrary", "arbitrary", "arbitrary")),
        name="dilated" if bounded else "dilated_exact",
    )(shift, qb, kb, kb, vb, vb)


def _store_row_tiles(ref, stage_ref, x, first_row=0):
    n = x.shape[0]
    target = ref if stage_ref is None else stage_ref
    for c in range(ROW_CHUNKS):
        target[pl.ds(first_row * ROW_CHUNKS + c, n, stride=ROW_CHUNKS), :] = x[:, c * LANES:(c + 1) * LANES]
    if stage_ref is not None:
        rows = slice(first_row * ROW_CHUNKS, (first_row + n) * ROW_CHUNKS)
        ref[rows, :] = stage_ref[rows, :].astype(ref.dtype)


def _load_row_tiles(ref, stage_ref, n, first_row=0):
    source = ref
    if stage_ref is not None:
        rows = slice(first_row * ROW_CHUNKS, (first_row + n) * ROW_CHUNKS)
        stage_ref[rows, :] = ref[rows, :].astype(F32)
        source = stage_ref
    return jnp.concatenate([source[pl.ds(first_row * ROW_CHUNKS + c, n, stride=ROW_CHUNKS), :]
                            for c in range(ROW_CHUNKS)], axis=1)


def _row_tile_copy(src_ref, src_row, dst_ref, dst_row, sem):
    src = src_ref.at[pl.ds(pl.multiple_of(src_row * ROW_CHUNKS, ROW_CHUNKS), ROW_CHUNKS), :]
    dst = dst_ref.at[pl.ds(pl.multiple_of(dst_row * ROW_CHUNKS, ROW_CHUNKS), ROW_CHUNKS), :]
    return pltpu.make_async_copy(src, dst, sem)


def _outproj_kernel(oa_ref, ob_ref, x_ref, gfox_ref, gdil_ref, wo_ref, gffn_ref, wr_ref, br_ref, upper_ref,
                    h_ref, hn_ref, eid_ref, gate_ref, rank_ref, cnt_ref, run_ref, stage_ref):
    @pl.when(pl.program_id(0) == 0)
    def _():
        run_ref[...] = jnp.zeros_like(run_ref)

    def norm(y, g):
        ms = jnp.mean(y * y, axis=-1, keepdims=True)
        return y * lax.rsqrt(ms + EPS) * g

    a = norm(oa_ref[...], gfox_ref[...]).astype(BF16)
    b = norm(ob_ref[...], gdil_ref[...]).astype(BF16)
    mix = (jnp.dot(a, wo_ref[0:D_GRP, :], preferred_element_type=F32)
           + jnp.dot(b, wo_ref[D_GRP:2 * D_GRP, :], preferred_element_type=F32))
    h = x_ref[...] + mix
    h_ref[...] = h
    hn = norm(h, gffn_ref[...])
    _store_row_tiles(hn_ref, stage_ref, hn)

    z = lax.dot_general(wr_ref[...], hn, (((1,), (1,)), ((), ())), preferred_element_type=F32,
                        precision=lax.Precision.HIGHEST) + br_ref[...]
    tm = z.shape[1]
    best = z[0:1, :]
    g_sel = jnp.zeros((1, tm), I32)
    for g in range(1, N_GROUPS):
        better = z[g:g + 1, :] > best
        g_sel = jnp.where(better, g, g_sel)
        best = jnp.maximum(best, z[g:g + 1, :])
    den = jnp.zeros((1, tm), F32)
    for g in range(N_GROUPS):
        den = den + jnp.exp(z[g:g + 1, :] - best)
    pg_top = 1.0 / den

    ze = jnp.zeros((EXPERTS_PER_GROUP, tm), F32)
    for g in range(N_GROUPS):
        ze = jnp.where(g_sel == g, z[8 + g * EXPERTS_PER_GROUP:8 + (g + 1) * EXPERTS_PER_GROUP, :], ze)
    e_iota = lax.broadcasted_iota(I32, ze.shape, 0)
    v1 = jnp.max(ze, axis=0, keepdims=True)
    i1 = jnp.min(jnp.where(ze == v1, e_iota, EXPERTS_PER_GROUP), axis=0, keepdims=True)
    ze2 = jnp.where(e_iota == i1, -jnp.inf, ze)
    v2 = jnp.max(ze2, axis=0, keepdims=True)
    i2 = jnp.min(jnp.where(ze2 == v2, e_iota, EXPERTS_PER_GROUP), axis=0, keepdims=True)
    e2 = jnp.exp(v2 - v1)
    inv = 1.0 / (1.0 + e2)
    gate1 = inv * pg_top
    gate2 = e2 * inv * pg_top
    eid1 = g_sel * EXPERTS_PER_GROUP + i1
    eid2 = g_sel * EXPERTS_PER_GROUP + i2

    x_iota = lax.broadcasted_iota(I32, (N_EXPERTS, tm), 0)
    hot1 = x_iota == eid1
    hot2 = x_iota == eid2
    multi = jnp.logical_or(hot1, hot2)
    before = jnp.dot(multi.astype(BF16), upper_ref[...], preferred_element_type=F32)
    slot = before + run_ref[:, 0:1]
    rank1 = jnp.sum(jnp.where(hot1, slot, 0.0), axis=0, keepdims=True)
    rank2 = jnp.sum(jnp.where(hot2, slot, 0.0), axis=0, keepdims=True)
    run_ref[...] = run_ref[...] + jnp.sum(multi.astype(F32), axis=1, keepdims=True)

    eid_ref[...] = jnp.concatenate([eid1, eid2], axis=0)
    gate_ref[...] = jnp.concatenate([gate1, gate2], axis=0)
    rank_ref[...] = jnp.concatenate([rank1, rank2], axis=0).astype(I32)
    cnt_ref[...] = run_ref[...].astype(I32)


def _outproj(oa, ob, x2, gfox, gdil, w_out, gffn, w_r, b_r, upper):
    T, D = x2.shape
    tm = TM_OUT
    const = lambda shape: pl.BlockSpec(shape, lambda i: (0,) * len(shape))
    tok = lambda w: pl.BlockSpec((tm, w), lambda i: (i, 0))
    lanes2 = pl.BlockSpec((2, tm), lambda i: (0, i))
    return pl.pallas_call(
        _outproj_kernel,
        grid=(T // tm,),
        in_specs=[tok(D_GRP), tok(D_GRP), tok(D), const((1, D_GRP)), const((1, D_GRP)), const((D, D)),
                  const((1, D)), const((ROUTER_ROWS, D)), const((ROUTER_ROWS, 1)), const((tm, tm))],
        out_specs=[tok(D), pl.BlockSpec((tm * ROW_CHUNKS, LANES), lambda i: (i, 0)),
                   lanes2, lanes2, lanes2, const((N_EXPERTS, LANES))],
        out_shape=[jax.ShapeDtypeStruct((T, D), F32), jax.ShapeDtypeStruct((T * ROW_CHUNKS, LANES), BF16),
                   jax.ShapeDtypeStruct((2, T), I32), jax.ShapeDtypeStruct((2, T), F32),
                   jax.ShapeDtypeStruct((2, T), I32), jax.ShapeDtypeStruct((N_EXPERTS, LANES), I32)],
        scratch_shapes=[pltpu.VMEM((N_EXPERTS, LANES), F32), pltpu.VMEM((tm * ROW_CHUNKS, LANES), F32)],
        compiler_params=_cparams(("arbitrary",)),
        name="outproj",
    )(oa, ob, x2, gfox, gdil, w_out, gffn, w_r, b_r, upper)


def _scatter_kernel(starts_ref, cnt_ref, pos0_ref, pos1_ref, hn_ref, xs_ref, ring, sems, zero_sem, *, n_steps):
    i = pl.program_id(0)
    tm = hn_ref.shape[0] // ROW_CHUNKS
    slot = lax.rem(i, 2)

    def wait_slot(s):
        for _ in range(2):
            pltpu.make_async_copy(ring.at[s], xs_ref.at[pl.ds(0, tm * ROW_CHUNKS), :], sems.at[s]).wait()

    @pl.when(i >= 2)
    def _():
        wait_slot(slot)

    ring[slot] = hn_ref[...]

    def start(r, _):
        for k, pos_ref in enumerate((pos0_ref, pos1_ref)):
            _row_tile_copy(ring.at[slot], r, xs_ref, pos_ref[r], sems.at[slot]).start(priority=k)
        return 0

    lax.fori_loop(0, tm, start, 0, unroll=DMA_UNROLL)

    @pl.when(i == n_steps - 1)
    def _():
        wait_slot(slot)
        if n_steps > 1:
            wait_slot(1 - slot)
        ring[0] = jnp.zeros((tm * ROW_CHUNKS, LANES), ring.dtype)

        def pad_expert(e, _, wait):
            n_pad = lax.rem(TILE_E - lax.rem(cnt_ref[e], TILE_E), TILE_E)
            first = starts_ref[e] + cnt_ref[e]
            size = TILE_E // 2
            while size >= 1:
                row0 = first + (n_pad & ~(2 * size - 1))

                @pl.when((n_pad & size) != 0)
                def _(size=size, row0=row0):
                    copy = pltpu.make_async_copy(
                        ring.at[0, pl.ds(0, size * ROW_CHUNKS), :],
                        xs_ref.at[pl.ds(pl.multiple_of(row0 * ROW_CHUNKS, ROW_CHUNKS), size * ROW_CHUNKS), :],
                        zero_sem)
                    copy.wait() if wait else copy.start()

                size //= 2
            return 0

        lax.fori_loop(0, N_EXPERTS, functools.partial(pad_expert, wait=False), 0)
        last = N_EXPERTS - 1
        used_rows = starts_ref[last] + cnt_ref[last] + lax.rem(TILE_E - lax.rem(cnt_ref[last], TILE_E), TILE_E)
        n_tail = xs_ref.shape[0] // (tm * ROW_CHUNKS) - used_rows // tm

        def tail_copy(t):
            row0 = pl.multiple_of((used_rows + t * tm) * ROW_CHUNKS, tm * ROW_CHUNKS)
            return pltpu.make_async_copy(ring.at[0], xs_ref.at[pl.ds(row0, tm * ROW_CHUNKS), :], zero_sem)

        lax.fori_loop(0, n_tail, lambda t, c: (tail_copy(t).start(), c)[1], 0)
        lax.fori_loop(0, N_EXPERTS, functools.partial(pad_expert, wait=True), 0)
        lax.fori_loop(0, n_tail, lambda t, c: (tail_copy(t).wait(), c)[1], 0)


def _scatter(starts, cnt, pos, hn, n_rows):
    T = hn.shape[0] // ROW_CHUNKS
    tm = TM_ROWS
    assert TILE_E % tm == 0 and TILE_E // 2 <= tm
    n_steps = T // tm
    grid_spec = pltpu.PrefetchScalarGridSpec(
        num_scalar_prefetch=2,
        grid=(n_steps,),
        in_specs=[pl.BlockSpec((tm,), lambda i, starts, cnt: (i,), memory_space=pltpu.SMEM),
                  pl.BlockSpec((tm,), lambda i, starts, cnt: (i,), memory_space=pltpu.SMEM),
                  pl.BlockSpec((tm * ROW_CHUNKS, LANES), lambda i, starts, cnt: (i, 0))],
        out_specs=pl.BlockSpec(memory_space=pl.ANY),
        scratch_shapes=[pltpu.VMEM((2, tm * ROW_CHUNKS, LANES), hn.dtype),
                        pltpu.SemaphoreType.DMA((2,)), pltpu.SemaphoreType.DMA(())],
    )
    return pl.pallas_call(
        functools.partial(_scatter_kernel, n_steps=n_steps),
        grid_spec=grid_spec,
        out_shape=jax.ShapeDtypeStruct((n_rows * ROW_CHUNKS, LANES), hn.dtype),
        compiler_params=_cparams(("arbitrary",)),
        name="scatter_rows",
    )(starts, cnt, pos[0], pos[1], hn)


def _experts_kernel(te_ref, tb_ref, tr_ref, xs_ref, wg_ref, wu_ref, wd_ref, y_ref, wg_bf, wu_bf, wd_bf, stage_ref):
    del tb_ref
    i = pl.program_id(0)
    rows = tr_ref[i]
    new_expert = jnp.logical_or(i == 0, te_ref[i] != te_ref[jnp.maximum(i - 1, 0)])

    @pl.when(jnp.logical_and(rows > 0, new_expert))
    def _():
        wg_bf[...] = wg_ref[0].astype(BF16)
        wu_bf[...] = wu_ref[0].astype(BF16)
        wd_bf[...] = wd_ref[0].astype(BF16)

    half = TILE_E // 2
    for part in range(2):
        @pl.when(rows > part * half)
        def _(part=part):
            x = _load_row_tiles(xs_ref, stage_ref, half, part * half).astype(BF16)
            g = jnp.dot(x, wg_bf[...], preferred_element_type=F32)
            up = jnp.dot(x, wu_bf[...], preferred_element_type=F32)
            hmid = (g * jax.nn.sigmoid(g) * up).astype(BF16)
            _store_row_tiles(y_ref, None, jnp.dot(hmid, wd_bf[...], preferred_element_type=F32), part * half)

        @pl.when(rows <= part * half)
        def _(part=part):
            y_ref[part * half * ROW_CHUNKS:(part + 1) * half * ROW_CHUNKS, :] = (
                jnp.zeros((half * ROW_CHUNKS, LANES), y_ref.dtype))


def _experts(tile_expert, tile_block, tile_rows, xs, w_gate, w_up, w_down):
    n_tiles = tile_expert.shape[0]
    D = D_MODEL
    rows_spec = pl.BlockSpec((TILE_E * ROW_CHUNKS, LANES), lambda i, te, tb, tr: (tb[i], 0))
    weights = lambda shape: pl.BlockSpec(shape, lambda i, te, tb, tr: (te[i], 0, 0))
    grid_spec = pltpu.PrefetchScalarGridSpec(
        num_scalar_prefetch=3,
        grid=(n_tiles,),
        in_specs=[rows_spec, weights((1, D, D_EXPERT)), weights((1, D, D_EXPERT)), weights((1, D_EXPERT, D))],
        out_specs=pl.BlockSpec((TILE_E * ROW_CHUNKS, LANES), lambda i, te, tb, tr: (i, 0)),
        scratch_shapes=[pltpu.VMEM((D, D_EXPERT), BF16), pltpu.VMEM((D, D_EXPERT), BF16),
                        pltpu.VMEM((D_EXPERT, D), BF16), pltpu.VMEM((TILE_E * ROW_CHUNKS, LANES), F32)],
    )
    return pl.pallas_call(
        _experts_kernel,
        grid_spec=grid_spec,
        out_shape=jax.ShapeDtypeStruct(xs.shape, F32),
        compiler_params=_cparams(("arbitrary",)),
        name="experts",
    )(tile_expert, tile_block, tile_rows, xs, w_gate, w_up, w_down)


def _combine_kernel(pos0_ref, pos1_ref, next0_ref, next1_ref, h_ref, gate_ref, y_ref, o_ref, ybuf, sems, *, n_steps):
    i = pl.program_id(0)
    tm = h_ref.shape[0]
    slot = lax.rem(i, 2)

    def gather(p_refs, s):
        def start(r, _):
            for k, p_ref in enumerate(p_refs):
                _row_tile_copy(y_ref, p_ref[r], ybuf.at[s, k], r, sems.at[s]).start(priority=k)
            return 0

        lax.fori_loop(0, tm, start, 0, unroll=DMA_UNROLL)

    @pl.when(i == 0)
    def _():
        gather((pos0_ref, pos1_ref), slot)

    @pl.when(i + 1 < n_steps)
    def _():
        gather((next0_ref, next1_ref), 1 - slot)

    for k in range(2):
        pltpu.make_async_copy(y_ref.at[pl.ds(0, tm * ROW_CHUNKS), :], ybuf.at[slot, k], sems.at[slot]).wait()
    g = gate_ref[...]
    o_ref[...] = (h_ref[...] + g[:, 0:1] * _load_row_tiles(ybuf.at[slot, 0], None, tm)
                  + g[:, 1:2] * _load_row_tiles(ybuf.at[slot, 1], None, tm))


def _combine(pos, h, gate_t, y):
    T, D = h.shape
    tm = TM_ROWS
    n_steps = T // tm
    return pl.pallas_call(
        functools.partial(_combine_kernel, n_steps=n_steps),
        grid=(n_steps,),
        in_specs=[pl.BlockSpec((tm,), lambda i: (i,), memory_space=pltpu.SMEM),
                  pl.BlockSpec((tm,), lambda i: (i,), memory_space=pltpu.SMEM),
                  pl.BlockSpec((tm,), lambda i: (jnp.minimum(i + 1, n_steps - 1),), memory_space=pltpu.SMEM),
                  pl.BlockSpec((tm,), lambda i: (jnp.minimum(i + 1, n_steps - 1),), memory_space=pltpu.SMEM),
                  pl.BlockSpec((tm, D), lambda i: (i, 0)),
                  pl.BlockSpec((tm, 2), lambda i: (i, 0)),
                  pl.BlockSpec(memory_space=pl.ANY)],
        out_specs=pl.BlockSpec((tm, D), lambda i: (i, 0)),
        out_shape=jax.ShapeDtypeStruct((T, D), F32),
        scratch_shapes=[pltpu.VMEM((2, 2, tm * ROW_CHUNKS, LANES), y.dtype), pltpu.SemaphoreType.DMA((2,))],
        compiler_params=_cparams(("arbitrary",)),
        name="combine",
    )(pos[0], pos[1], pos[0], pos[1], h, gate_t, y)


def _rope_tables(S):
    inv_freq = 1.0 / (ROPE_THETA ** (np.arange(0, HEAD_DIM, 2, dtype=np.float64) / HEAD_DIM))
    ang = np.arange(S, dtype=np.float64)[:, None] * inv_freq[None, :]
    widen = lambda t: jnp.asarray(np.pad(t, ((0, 0), (0, LANES - HEAD_DIM // 2))).astype(np.float32))
    return widen(np.cos(ang)), widen(np.sin(ang))


def _layer(x, norm_mix, w_in, b_forget, q_norm_fox, k_norm_fox, q_norm_dil, k_norm_dil,
           out_norm_fox, out_norm_dil, w_out, norm_ffn, w_router_group, b_router_group,
           w_router_expert, b_router_expert, w_gate, w_up, w_down):
    B, S, D = x.shape
    T = B * S
    n_main = 6 * D_GRP

    w_main = w_in.astype(BF16)
    w_vt = w_main[:, 2 * D_GRP:3 * D_GRP].T
    w_f = jnp.pad(w_main[:, n_main:], ((0, 0), (0, LANES - N_HEADS)))
    b_f = jnp.pad(b_forget, (0, LANES - N_HEADS))[None, :]
    per_head = lambda g: jnp.tile(g, N_HEADS)[None, :]
    bd = jnp.kron(jnp.eye(N_HEADS, dtype=F32), jnp.ones((HEAD_DIM, HEAD_DIM), F32)).astype(BF16)
    cos_t, sin_t = _rope_tables(S)
    tri = jnp.tril(jnp.ones((TRI_ROWS, TRI_ROWS), F32)).astype(BF16)
    upper = jnp.triu(jnp.ones((TM_OUT, TM_OUT), F32), k=1).astype(BF16)
    w_r = jnp.concatenate([
        jnp.pad(w_router_group.T, ((0, 8 - N_GROUPS), (0, 0))),
        w_router_expert.transpose(0, 2, 1).reshape(N_EXPERTS, D)], axis=0)
    b_r = jnp.concatenate([jnp.pad(b_router_group, (0, 8 - N_GROUPS)), b_router_expert.reshape(-1)])[:, None]

    bound = (HEAD_DIM / math.sqrt(HEAD_DIM)) * LOG2E * jnp.max(jnp.abs(q_norm_fox)) * jnp.max(jnp.abs(k_norm_fox))
    shift = FOX_BOUND_SLACK * bound + 1.0
    qa, ka, va, qb, kb, vb, c2 = _inproj(
        x, norm_mix[None, :], w_main, w_vt, w_f, b_f, jnp.full((1, LANES), shift, F32),
        per_head(q_norm_fox), per_head(k_norm_fox),
        per_head(q_norm_dil), per_head(k_norm_dil), bd, cos_t, sin_t, tri)
    n_q, n_k = S // TQ, S // TK_SUB
    c_first = c2[:, ::TQ, :].reshape(B, n_q, 1, N_PAIRS, HEADS_PER_TILE)
    c_last = c2[:, TK_SUB - 1::TK_SUB, :].reshape(B, 1, n_k, N_PAIRS, HEADS_PER_TILE)
    all_zero = jnp.all(c_first - c_last < -FOX_ZERO_BITS, axis=-1)
    live_so_far = jnp.einsum("bqkp,kj->bqjp", 1 - all_zero.astype(I32),
                             jnp.triu(jnp.ones((n_k, n_k), I32)))
    leading = (live_so_far == 0).astype(I32)
    before = (jnp.arange(n_k) * TK_SUB + TK_SUB <= jnp.arange(n_q)[:, None] * TQ)
    first_tile = jnp.sum(leading * before[None, :, :, None].astype(I32), axis=2)
    first_tile = first_tile.transpose(0, 2, 1).reshape(-1).astype(I32)
    oa = lax.cond(2.0 * shift <= FOX_SAFE_SPAN,
                  lambda: _fox(first_tile, qa, ka, va, online=False),
                  lambda: _fox(jnp.zeros_like(first_tile), qa, ka, va, online=True))
    dil_shift = (FOX_BOUND_SLACK * (HEAD_DIM / math.sqrt(HEAD_DIM))
                 * jnp.max(jnp.abs(q_norm_dil)) * jnp.max(jnp.abs(k_norm_dil)) + 1.0)
    dil_shift_arr = jnp.full((1, 1), dil_shift, F32)
    ob = lax.cond(2.0 * dil_shift <= DIL_SAFE_SPAN,
                  lambda: _dilated(dil_shift_arr, qb, kb, vb, bounded=True),
                  lambda: _dilated(dil_shift_arr, qb, kb, vb, bounded=False))

    h, hn, eid, gate, rank, cnt = _outproj(
        oa.reshape(T, D_GRP), ob.reshape(T, D_GRP), x.reshape(T, D), out_norm_fox[None, :],
        out_norm_dil[None, :], w_out.astype(BF16), norm_ffn[None, :], w_r, b_r, upper)

    counts = cnt[:, 0]
    padded = ((counts + TILE_E - 1) // TILE_E) * TILE_E
    ends = jnp.cumsum(padded)
    starts = ends - padded
    is_expert = eid[:, :, None] == jnp.arange(N_EXPERTS, dtype=I32)
    pos = jnp.sum(jnp.where(is_expert, starts, 0), axis=-1) + rank
    n_tiles = (2 * T) // TILE_E + N_EXPERTS
    tile_index = jnp.arange(n_tiles, dtype=I32)
    tile_valid = tile_index * TILE_E < ends[-1]
    tile_block = jnp.minimum(tile_index, ends[-1] // TILE_E - 1)
    in_region = ends[None, :] <= (tile_block * TILE_E)[:, None]
    tile_expert = jnp.sum(in_region.astype(I32), axis=1)
    expert_of_tile = tile_expert[:, None] == jnp.arange(N_EXPERTS, dtype=I32)
    rows_before = tile_block * TILE_E - jnp.sum(jnp.where(expert_of_tile, starts, 0), axis=1)
    tile_rows = jnp.clip(jnp.sum(jnp.where(expert_of_tile, counts, 0), axis=1) - rows_before, 0, TILE_E)
    tile_rows = jnp.where(tile_valid, tile_rows, 0).astype(I32)

    xs = _scatter(starts, counts, pos, hn, n_tiles * TILE_E)
    y = _experts(tile_expert, tile_block, tile_rows, xs, w_gate, w_up, w_down)
    out = _combine(pos, h, gate.T, y)
    return out.reshape(B, S, D)


def kernel(x, norm_mix, w_in, b_forget, q_norm_fox, k_norm_fox, q_norm_dil, k_norm_dil, out_norm_fox,
           out_norm_dil, w_out, norm_ffn, w_router_group, b_router_group, w_router_expert,
           b_router_expert, w_gate, w_up, w_down):
    h = x
    for l in range(norm_mix.shape[0]):
        h = _layer(h, norm_mix[l], w_in[l], b_forget[l], q_norm_fox[l], k_norm_fox[l], q_norm_dil[l],
                   k_norm_dil[l], out_norm_fox[l], out_norm_dil[l], w_out[l], norm_ffn[l],
                   w_router_group[l], b_router_group[l], w_router_expert[l], b_router_expert[l],
                   w_gate[l], w_up[l], w_down[l])
    return h
```

```python
import functools
import math

import jax
import jax.numpy as jnp
import numpy as np
from jax import lax
from jax.experimental import pallas as pl
from jax.experimental.pallas import tpu as pltpu

F32 = jnp.float32
BF16 = jnp.bfloat16
I32 = jnp.int32

D_MODEL = 1024
HEAD_DIM = 64
N_HEADS = 8
D_GRP = N_HEADS * HEAD_DIM
LANES = 128
HEADS_PER_TILE = LANES // HEAD_DIM
N_PAIRS = D_GRP // LANES
DIL_PATTERNS = ((128, 1), (512, 4), (2048, 16))
BLOCK = 128
ROPE_THETA = 10000.0
N_GROUPS = 4
EXPERTS_PER_GROUP = 8
N_EXPERTS = N_GROUPS * EXPERTS_PER_GROUP
D_EXPERT = 512
EPS = 1e-6
NEG = -1e30
LOG2E = 1.4426950408889634

TM_IN = 1024
TRI_ROWS = 256
TQ = 1024
TK_WIDE = 1024
TK_SUB = 256
FOX_AHEAD = 4
FOX_ZERO_BITS = 160.0
FOX_BOUND_SLACK = 1.02
FOX_SAFE_SPAN = 100.0
DIL_SAFE_SPAN = FOX_SAFE_SPAN * math.log(2.0)
DIL_SPAN = 2048
DIL_GROUP = 16
DIL_PRE = 4
TM_OUT = 512
TM_ROWS = 512
TILE_E = 512
ROUTER_ROWS = 8 + N_EXPERTS
ROW_CHUNKS = D_MODEL // LANES
DMA_UNROLL = 8
VMEM_LIMIT = 56 * 1024 * 1024


def _cparams(sem):
    return pltpu.CompilerParams(dimension_semantics=sem, vmem_limit_bytes=VMEM_LIMIT)


def _inproj_kernel(x_ref, gmix_ref, w_ref, wvt_ref, wf_ref, bf_ref, shift_ref, gqa_ref, gka_ref, gqb_ref, gkb_ref,
                   bd_ref, cos_ref, sin_ref, tri_ref,
                   qa_ref, ka_ref, va_ref, qb_ref, kb_ref, vb_ref, c_ref, carry_ref):
    @pl.when(pl.program_id(1) == 0)
    def _():
        carry_ref[...] = jnp.zeros_like(carry_ref)

    x = x_ref[0]
    ms = jnp.mean(x * x, axis=-1, keepdims=True)
    xn = (x * lax.rsqrt(ms + EPS) * gmix_ref[...]).astype(BF16)

    def seg(j):
        return jnp.dot(xn, w_ref[:, j * D_GRP:(j + 1) * D_GRP], preferred_element_type=F32)

    def head_norm(y, g_ref, scale):
        ss = jnp.dot((y * y).astype(BF16), bd_ref[...], preferred_element_type=F32) * (1.0 / HEAD_DIM)
        return y * lax.rsqrt(ss + EPS) * (g_ref[...] * scale)

    lane = lax.broadcasted_iota(I32, (x.shape[0], LANES), 1)
    first_half = (lane % HEAD_DIM) < (HEAD_DIM // 2)

    def spread(t):
        t = t + pltpu.roll(t, HEAD_DIM // 2, 1)
        return t + pltpu.roll(t, HEAD_DIM, 1)

    cos = spread(cos_ref[...])
    sin = jnp.where(first_half, -1.0, 1.0) * spread(sin_ref[...])

    def rope(y):
        outs = []
        for j in range(N_PAIRS):
            ys = y[:, j * LANES:(j + 1) * LANES]
            partner = jnp.where(first_half, pltpu.roll(ys, LANES - HEAD_DIM // 2, 1),
                                pltpu.roll(ys, HEAD_DIM // 2, 1))
            outs.append(ys * cos + partner * sin)
        return jnp.concatenate(outs, axis=1)

    scale = 1.0 / math.sqrt(HEAD_DIM)
    va_ref[0] = lax.dot_general(wvt_ref[...], xn, (((1,), (1,)), ((), ())),
                                preferred_element_type=F32).astype(va_ref.dtype)
    qb_ref[0] = rope(head_norm(seg(3), gqb_ref, scale)).astype(qb_ref.dtype)
    kb_ref[0] = rope(head_norm(seg(4), gkb_ref, 1.0)).astype(kb_ref.dtype)
    vb_ref[0] = seg(5).astype(vb_ref.dtype)

    fa = jnp.dot(xn, wf_ref[...], preferred_element_type=F32) + bf_ref[...]
    logf = jnp.minimum(fa, 0.0) - jnp.log1p(jnp.exp(-jnp.abs(fa)))
    hi = logf.astype(BF16)
    mid = (logf - hi.astype(F32)).astype(BF16)
    lo = (logf - hi.astype(F32) - mid.astype(F32)).astype(BF16)
    pieces = jnp.concatenate([hi, mid, lo], axis=1)
    carry = carry_ref[...]
    blocks = []
    for j in range(x.shape[0] // TRI_ROWS):
        parts = jnp.dot(tri_ref[...], pieces[j * TRI_ROWS:(j + 1) * TRI_ROWS, :], preferred_element_type=F32)
        blk = parts[:, :LANES] + parts[:, LANES:2 * LANES] + parts[:, 2 * LANES:] + carry
        carry = blk[TRI_ROWS - 1:, :]
        blocks.append(blk)
    c = jnp.concatenate(blocks, axis=0)
    carry_ref[...] = carry

    qa = head_norm(seg(0), gqa_ref, scale * LOG2E)
    ka = head_norm(seg(1), gka_ref, 1.0)
    c2 = c * LOG2E
    c_ref[0] = c2[:, :N_HEADS]
    ones = (jnp.where((lane >= HEAD_DIM + 3) & (lane < HEAD_DIM + 6), 1.0, 0.0)
            - jnp.where(lane == HEAD_DIM + 6, shift_ref[...], 0.0))
    for h in range(N_HEADS):
        cb = jnp.broadcast_to(c2[:, h:h + 1], (x.shape[0], LANES))
        hi = cb.astype(BF16).astype(F32)
        mid = (cb - hi).astype(BF16).astype(F32)
        lo = cb - hi - mid
        pieces = jnp.where(lane == HEAD_DIM, hi, jnp.where(lane == HEAD_DIM + 1, mid,
                           jnp.where(lane == HEAD_DIM + 2, lo, 0.0)))
        q_extra = pieces + ones
        k_extra = jnp.where(((lane >= HEAD_DIM) & (lane < HEAD_DIM + 3)) | (lane == HEAD_DIM + 6), 1.0, 0.0) \
            - pltpu.roll(pieces, 3, 1)
        j, odd = divmod(h, HEADS_PER_TILE)
        qp = qa[:, j * LANES:(j + 1) * LANES]
        kp = ka[:, j * LANES:(j + 1) * LANES]
        if odd:
            qp = pltpu.roll(qp, HEAD_DIM, 1)
            kp = pltpu.roll(kp, HEAD_DIM, 1)
        qa_ref[0, h] = jnp.where(lane < HEAD_DIM, qp, q_extra).astype(qa_ref.dtype)
        ka_ref[0, h] = jnp.where(lane < HEAD_DIM, kp, k_extra).astype(ka_ref.dtype)


def _inproj(x, gmix, w_main, w_vt, w_f, b_f, shift, gqa, gka, gqb, gkb, bd, cos_t, sin_t, tri):
    B, S, D = x.shape
    tm = TM_IN
    const = lambda shape: pl.BlockSpec(shape, lambda b, i: (0,) * len(shape))
    tok = lambda w, dt: jax.ShapeDtypeStruct((B, S, w), dt)
    tok_spec = lambda w: pl.BlockSpec((1, tm, w), lambda b, i: (b, i, 0))
    head_spec = pl.BlockSpec((1, N_HEADS, tm, LANES), lambda b, i: (b, 0, i, 0))
    head_shape = jax.ShapeDtypeStruct((B, N_HEADS, S, LANES), BF16)
    return pl.pallas_call(
        _inproj_kernel,
        grid=(B, S // tm),
        in_specs=[tok_spec(D), const((1, D)), const(w_main.shape), const(w_vt.shape), const(w_f.shape),
                  const((1, LANES)), const((1, LANES)),
                  const((1, D_GRP)), const((1, D_GRP)), const((1, D_GRP)), const((1, D_GRP)),
                  const((D_GRP, D_GRP)),
                  pl.BlockSpec((tm, LANES), lambda b, i: (i, 0)),
                  pl.BlockSpec((tm, LANES), lambda b, i: (i, 0)),
                  const(tri.shape)],
        out_specs=[head_spec, head_spec, pl.BlockSpec((1, D_GRP, tm), lambda b, i: (b, 0, i))]
        + [tok_spec(D_GRP)] * 3 + [tok_spec(N_HEADS)],
        out_shape=[head_shape, head_shape, jax.ShapeDtypeStruct((B, D_GRP, S), BF16),
                   tok(D_GRP, F32), tok(D_GRP, F32), tok(D_GRP, F32), tok(N_HEADS, F32)],
        scratch_shapes=[pltpu.VMEM((1, LANES), F32)],
        compiler_params=_cparams(("arbitrary", "arbitrary")),
        name="inproj",
    )(x, gmix, w_main, w_vt, w_f, b_f, shift, gqa, gka, gqb, gkb, bd, cos_t, sin_t, tri)


def _fox_kernel(first_ref, q_ref, k_ref, v_ref, o_ref, *, online):
    qi = pl.program_id(2)
    tq = q_ref.shape[2]

    def step(start, width, carry, diag):
        carry = list(carry)
        sub = min(TK_SUB, width)
        chunks = [(c, j) for c in range(width // sub) for j in range(HEADS_PER_TILE)]
        def first_query(c):
            return c * sub if diag else 0

        def score(c, j):
            k = k_ref[0, j, pl.ds(start + c * sub, sub), :]
            q = q_ref[0, j, first_query(c):, :]
            return lax.dot_general(k, q, (((1,), (1,)), ((), ())), preferred_element_type=F32)

        def tail(full, lo, new):
            return new if lo == 0 else jnp.concatenate([full[:, :lo], new], axis=1)

        scores = {cj: score(*cj) for cj in chunks[:FOX_AHEAD]}
        for n, (c, j) in enumerate(chunks):
                if n + FOX_AHEAD < len(chunks):
                    nxt = chunks[n + FOX_AHEAD]
                    scores[nxt] = score(*nxt)
                m, l, acc = carry[j]
                lo = first_query(c)
                vt = v_ref[0, :, pl.ds(start + c * sub, sub)]
                s = scores.pop((c, j))
                if diag:
                    key = lax.broadcasted_iota(I32, s.shape, 0)
                    qry = lax.broadcasted_iota(I32, s.shape, 1)
                    s = jnp.where(key <= qry, s, NEG)
                if online:
                    m_new = jnp.maximum(m[:, lo:], jnp.max(s, axis=0, keepdims=True))
                    alpha = jnp.exp2(m[:, lo:] - m_new)
                    p = jnp.exp2(s - m_new)
                    l_new = alpha * l[:, lo:] + jnp.sum(p, axis=0, keepdims=True)
                    acc_new = alpha * acc[:, lo:] + jnp.dot(vt, p.astype(BF16), preferred_element_type=F32)
                    m = tail(m, lo, m_new)
                else:
                    p = jnp.exp2(s)
                    l_new = l[:, lo:] + jnp.sum(p, axis=0, keepdims=True)
                    acc_new = acc[:, lo:] + jnp.dot(vt, p.astype(BF16), preferred_element_type=F32)
                carry[j] = (m, tail(l, lo, l_new), tail(acc, lo, acc_new))
        return tuple(carry)

    init = tuple((jnp.full((1, tq), NEG, F32), jnp.zeros((1, tq), F32), jnp.zeros((LANES, tq), F32))
                 for _ in range(HEADS_PER_TILE))
    assert TK_WIDE == tq and TK_WIDE % TK_SUB == 0
    per_wide = TK_WIDE // TK_SUB
    first = first_ref[(pl.program_id(0) * pl.num_programs(1) + pl.program_id(1)) * pl.num_programs(2) + qi]
    live = qi * per_wide - first
    n_narrow = lax.rem(live, per_wide)
    narrow_start = pl.multiple_of(first * TK_SUB, TK_SUB)
    carry = lax.switch(
        n_narrow,
        [lambda c: c] + [lambda c, w=w: step(narrow_start, w * TK_SUB, c, False) for w in range(1, per_wide)],
        init)
    wide_start = (first + n_narrow) * TK_SUB
    carry = lax.fori_loop(
        0, lax.div(live, per_wide),
        lambda i, c: step(pl.multiple_of(wide_start + i * TK_WIDE, TK_SUB), TK_WIDE, c, False), carry)
    carry = step(pl.multiple_of(qi * tq, tq), tq, carry, True)
    outs = [acc / l for (_, l, acc) in carry]
    feat = lax.broadcasted_iota(I32, (LANES, tq), 0)
    o_ref[0] = jnp.where(feat < HEAD_DIM, outs[0], outs[1]).T.astype(o_ref.dtype)


def _fox(first_tile, qa, ka, va_t, online):
    B, _, S, _ = qa.shape
    grid_spec = pltpu.PrefetchScalarGridSpec(
        num_scalar_prefetch=1,
        grid=(B, N_PAIRS, S // TQ),
        in_specs=[pl.BlockSpec((1, HEADS_PER_TILE, TQ, LANES), lambda b, hp, i, first: (b, hp, i, 0)),
                  pl.BlockSpec((1, HEADS_PER_TILE, S, LANES), lambda b, hp, i, first: (b, hp, 0, 0)),
                  pl.BlockSpec((1, LANES, S), lambda b, hp, i, first: (b, hp, 0))],
        out_specs=pl.BlockSpec((1, TQ, LANES), lambda b, hp, i, first: (b, i, hp)),
    )
    return pl.pallas_call(
        functools.partial(_fox_kernel, online=online),
        grid_spec=grid_spec,
        out_shape=jax.ShapeDtypeStruct((B, S, D_GRP), F32),
        compiler_params=_cparams(("arbitrary", "arbitrary", "arbitrary")),
        name="fox_online" if online else "fox",
    )(first_tile, qa, ka, va_t)


def _dilated_kernel(shift_ref, q_ref, kp_ref, kc_ref, vp_ref, vc_ref, o_ref, qq, kk, vv, qq4, kk4, vv4, osc, lsc,
                    *, bounded):
    u = pl.program_id(1)
    span = q_ref.shape[1]
    qq[...] = q_ref[0]
    kk[0:span, :] = kp_ref[0]
    kk[span:2 * span, :] = kc_ref[0]
    vv[0:span, :] = vp_ref[0]
    vv[span:2 * span, :] = vc_ref[0]
    for src, dst in ((qq, qq4), (kk, kk4), (vv, vv4)):
        part = src.shape[0] // DIL_PRE
        for a in range(DIL_PRE):
            dst[a * part:(a + 1) * part, :] = src[pl.ds(a, part, stride=DIL_PRE), :]

    def rows(buf, buf4, start, n, d):
        if d % DIL_PRE:
            return buf[pl.ds(start, n, stride=d), :]
        part = buf4.shape[0] // DIL_PRE
        a = lax.rem(start, DIL_PRE)
        return buf4[pl.ds(a * part + lax.div(start, DIL_PRE), n, stride=d // DIL_PRE), :]

    lane = lax.broadcasted_iota(I32, (BLOCK, LANES), 1)
    ql = lax.broadcasted_iota(I32, (BLOCK, 2 * BLOCK), 0)
    kl = lax.broadcasted_iota(I32, (BLOCK, 2 * BLOCK), 1)
    dist = ql + BLOCK - kl
    band = (dist >= 0) & (dist <= BLOCK)
    live = -shift_ref[0, 0] if bounded else 0.0
    bias = jnp.where(band, live, NEG)
    bias_first = jnp.where(band & (kl >= BLOCK), live, NEG)

    def scores(q_start, k_start, d, first):
        qs = rows(qq, qq4, q_start, BLOCK, d).astype(BF16)
        ks = rows(kk, kk4, k_start, 2 * BLOCK, d).astype(BF16)
        mask = jnp.where(first, bias_first, bias)
        out = []
        for j in range(HEADS_PER_TILE):
            qj = jnp.where(lane // HEAD_DIM == j, qs, jnp.zeros_like(qs))
            out.append(lax.dot_general(qj, ks, (((1,), (1,)), ((), ())), preferred_element_type=F32) + mask)
        return out

    def finish(s_heads, k_start, d):
        vs = rows(vv, vv4, k_start, 2 * BLOCK, d).astype(BF16)
        o_heads, lse_heads = [], []
        for s in s_heads:
            if bounded:
                p = jnp.exp(s)
                o_heads.append(jnp.dot(p.astype(BF16), vs, preferred_element_type=F32))
                lse_heads.append(jnp.sum(p, axis=-1, keepdims=True))
            else:
                m = jnp.max(s, axis=-1, keepdims=True)
                p = jnp.exp(s - m)
                l = jnp.sum(p, axis=-1, keepdims=True)
                o_heads.append(jnp.dot((p / l).astype(BF16), vs, preferred_element_type=F32))
                lse_heads.append(m + jnp.log(l))
        o = jnp.where(lane < HEAD_DIM, o_heads[0], o_heads[1])
        lse = jnp.where(lane < HEAD_DIM, lse_heads[0], lse_heads[1])
        return o, lse

    for pidx, (window, d) in enumerate(DIL_PATTERNS):
        assert window // d == BLOCK
        unit = d * BLOCK
        n_problems = (span // unit) * d
        assert n_problems % DIL_GROUP == 0

        def body(g, _, pidx=pidx, d=d, unit=unit):
            starts, s_all = [], []
            for t in range(DIL_GROUP):
                idx = g * DIL_GROUP + t
                w = idx // d
                q_start = w * unit + (idx - w * d)
                k_start = span - unit + q_start
                starts.append((q_start, k_start))
                s_all.append(scores(q_start, k_start, d, jnp.logical_and(u == 0, w == 0)))
            for (q_start, k_start), s_heads in zip(starts, s_all):
                o, lse = finish(s_heads, k_start, d)
                osc[pidx, pl.ds(q_start, BLOCK, stride=d), :] = o
                lsc[pidx, pl.ds(q_start, BLOCK, stride=d), :] = lse
            return 0

        lax.fori_loop(0, n_problems // DIL_GROUP, body, 0)

    if bounded:
        num = osc[0] + osc[1] + osc[2]
        den = lsc[0] + lsc[1] + lsc[2]
    else:
        mx = jnp.maximum(jnp.maximum(lsc[0], lsc[1]), lsc[2])
        num = jnp.zeros((span, LANES), F32)
        den = jnp.zeros((span, LANES), F32)
        for pidx in range(len(DIL_PATTERNS)):
            e = jnp.exp(lsc[pidx] - mx)
            num = num + e * osc[pidx]
            den = den + e
    o_ref[0] = (num / den).astype(o_ref.dtype)


def _dilated(shift, qb, kb, vb, bounded):
    B, S, _ = qb.shape
    span = DIL_SPAN
    cur = pl.BlockSpec((1, span, LANES), lambda b, u, hp: (b, u, hp))
    prev = pl.BlockSpec((1, span, LANES), lambda b, u, hp: (b, jnp.maximum(u - 1, 0), hp))
    return pl.pallas_call(
        functools.partial(_dilated_kernel, bounded=bounded),
        grid=(B, S // span, N_PAIRS),
        in_specs=[pl.BlockSpec(memory_space=pltpu.SMEM), cur, prev, cur, prev, cur],
        out_specs=cur,
        out_shape=jax.ShapeDtypeStruct((B, S, D_GRP), F32),
        scratch_shapes=[pltpu.VMEM((span, LANES), F32),
                        pltpu.VMEM((2 * span, LANES), F32), pltpu.VMEM((2 * span, LANES), F32),
                        pltpu.VMEM((span, LANES), F32),
                        pltpu.VMEM((2 * span, LANES), F32), pltpu.VMEM((2 * span, LANES), F32),
                        pltpu.VMEM((len(DIL_PATTERNS), span, LANES), F32),
                        pltpu.VMEM((len(DIL_PATTERNS), span, LANES), F32)],
        compiler_params=_cparams(("arbitrary", "arbitrary", "arbitrary")),
        name="dilated" if bounded else "dilated_exact",
    )(shift, qb, kb, kb, vb, vb)


def _store_row_tiles(ref, stage_ref, x, first_row=0):
    n = x.shape[0]
    target = ref if stage_ref is None else stage_ref
    for c in range(ROW_CHUNKS):
        target[pl.ds(first_row * ROW_CHUNKS + c, n, stride=ROW_CHUNKS), :] = x[:, c * LANES:(c + 1) * LANES]
    if stage_ref is not None:
        rows = slice(first_row * ROW_CHUNKS, (first_row + n) * ROW_CHUNKS)
        ref[rows, :] = stage_ref[rows, :].astype(ref.dtype)


def _load_row_tiles(ref, stage_ref, n, first_row=0):
    source = ref
    if stage_ref is not None:
        rows = slice(first_row * ROW_CHUNKS, (first_row + n) * ROW_CHUNKS)
        stage_ref[rows, :] = ref[rows, :].astype(F32)
        source = stage_ref
    return jnp.concatenate([source[pl.ds(first_row * ROW_CHUNKS + c, n, stride=ROW_CHUNKS), :]
                            for c in range(ROW_CHUNKS)], axis=1)


def _row_tile_copy(src_ref, src_row, dst_ref, dst_row, sem):
    src = src_ref.at[pl.ds(pl.multiple_of(src_row * ROW_CHUNKS, ROW_CHUNKS), ROW_CHUNKS), :]
    dst = dst_ref.at[pl.ds(pl.multiple_of(dst_row * ROW_CHUNKS, ROW_CHUNKS), ROW_CHUNKS), :]
    return pltpu.make_async_copy(src, dst, sem)


def _outproj_kernel(oa_ref, ob_ref, x_ref, gfox_ref, gdil_ref, wo_ref, gffn_ref, wr_ref, br_ref, upper_ref,
                    h_ref, hn_ref, eid_ref, gate_ref, rank_ref, cnt_ref, run_ref, stage_ref):
    @pl.when(pl.program_id(0) == 0)
    def _():
        run_ref[...] = jnp.zeros_like(run_ref)

    def norm(y, g):
        ms = jnp.mean(y * y, axis=-1, keepdims=True)
        return y * lax.rsqrt(ms + EPS) * g

    a = norm(oa_ref[...], gfox_ref[...]).astype(BF16)
    b = norm(ob_ref[...], gdil_ref[...]).astype(BF16)
    mix = (jnp.dot(a, wo_ref[0:D_GRP, :], preferred_element_type=F32)
           + jnp.dot(b, wo_ref[D_GRP:2 * D_GRP, :], preferred_element_type=F32))
    h = x_ref[...] + mix
    h_ref[...] = h
    hn = norm(h, gffn_ref[...])
    _store_row_tiles(hn_ref, stage_ref, hn)

    z = lax.dot_general(wr_ref[...], hn, (((1,), (1,)), ((), ())), preferred_element_type=F32,
                        precision=lax.Precision.HIGHEST) + br_ref[...]
    tm = z.shape[1]
    best = z[0:1, :]
    g_sel = jnp.zeros((1, tm), I32)
    for g in range(1, N_GROUPS):
        better = z[g:g + 1, :] > best
        g_sel = jnp.where(better, g, g_sel)
        best = jnp.maximum(best, z[g:g + 1, :])
    den = jnp.zeros((1, tm), F32)
    for g in range(N_GROUPS):
        den = den + jnp.exp(z[g:g + 1, :] - best)
    pg_top = 1.0 / den

    ze = jnp.zeros((EXPERTS_PER_GROUP, tm), F32)
    for g in range(N_GROUPS):
        ze = jnp.where(g_sel == g, z[8 + g * EXPERTS_PER_GROUP:8 + (g + 1) * EXPERTS_PER_GROUP, :], ze)
    e_iota = lax.broadcasted_iota(I32, ze.shape, 0)
    v1 = jnp.max(ze, axis=0, keepdims=True)
    i1 = jnp.min(jnp.where(ze == v1, e_iota, EXPERTS_PER_GROUP), axis=0, keepdims=True)
    ze2 = jnp.where(e_iota == i1, -jnp.inf, ze)
    v2 = jnp.max(ze2, axis=0, keepdims=True)
    i2 = jnp.min(jnp.where(ze2 == v2, e_iota, EXPERTS_PER_GROUP), axis=0, keepdims=True)
    e2 = jnp.exp(v2 - v1)
    inv = 1.0 / (1.0 + e2)
    gate1 = inv * pg_top
    gate2 = e2 * inv * pg_top
    eid1 = g_sel * EXPERTS_PER_GROUP + i1
    eid2 = g_sel * EXPERTS_PER_GROUP + i2

    x_iota = lax.broadcasted_iota(I32, (N_EXPERTS, tm), 0)
    hot1 = x_iota == eid1
    hot2 = x_iota == eid2
    multi = jnp.logical_or(hot1, hot2)
    before = jnp.dot(multi.astype(BF16), upper_ref[...], preferred_element_type=F32)
    slot = before + run_ref[:, 0:1]
    rank1 = jnp.sum(jnp.where(hot1, slot, 0.0), axis=0, keepdims=True)
    rank2 = jnp.sum(jnp.where(hot2, slot, 0.0), axis=0, keepdims=True)
    run_ref[...] = run_ref[...] + jnp.sum(multi.astype(F32), axis=1, keepdims=True)

    eid_ref[...] = jnp.concatenate([eid1, eid2], axis=0)
    gate_ref[...] = jnp.concatenate([gate1, gate2], axis=0)
    rank_ref[...] = jnp.concatenate([rank1, rank2], axis=0).astype(I32)
    cnt_ref[...] = run_ref[...].astype(I32)


def _outproj(oa, ob, x2, gfox, gdil, w_out, gffn, w_r, b_r, upper):
    T, D = x2.shape
    tm = TM_OUT
    const = lambda shape: pl.BlockSpec(shape, lambda i: (0,) * len(shape))
    tok = lambda w: pl.BlockSpec((tm, w), lambda i: (i, 0))
    lanes2 = pl.BlockSpec((2, tm), lambda i: (0, i))
    return pl.pallas_call(
        _outproj_kernel,
        grid=(T // tm,),
        in_specs=[tok(D_GRP), tok(D_GRP), tok(D), const((1, D_GRP)), const((1, D_GRP)), const((D, D)),
                  const((1, D)), const((ROUTER_ROWS, D)), const((ROUTER_ROWS, 1)), const((tm, tm))],
        out_specs=[tok(D), pl.BlockSpec((tm * ROW_CHUNKS, LANES), lambda i: (i, 0)),
                   lanes2, lanes2, lanes2, const((N_EXPERTS, LANES))],
        out_shape=[jax.ShapeDtypeStruct((T, D), F32), jax.ShapeDtypeStruct((T * ROW_CHUNKS, LANES), BF16),
                   jax.ShapeDtypeStruct((2, T), I32), jax.ShapeDtypeStruct((2, T), F32),
                   jax.ShapeDtypeStruct((2, T), I32), jax.ShapeDtypeStruct((N_EXPERTS, LANES), I32)],
        scratch_shapes=[pltpu.VMEM((N_EXPERTS, LANES), F32), pltpu.VMEM((tm * ROW_CHUNKS, LANES), F32)],
        compiler_params=_cparams(("arbitrary",)),
        name="outproj",
    )(oa, ob, x2, gfox, gdil, w_out, gffn, w_r, b_r, upper)


def _scatter_kernel(starts_ref, cnt_ref, pos0_ref, pos1_ref, hn_ref, xs_ref, ring, sems, zero_sem, *, n_steps):
    i = pl.program_id(0)
    tm = hn_ref.shape[0] // ROW_CHUNKS
    slot = lax.rem(i, 2)

    def wait_slot(s):
        for _ in range(2):
            pltpu.make_async_copy(ring.at[s], xs_ref.at[pl.ds(0, tm * ROW_CHUNKS), :], sems.at[s]).wait()

    @pl.when(i >= 2)
    def _():
        wait_slot(slot)

    ring[slot] = hn_ref[...]

    def start(r, _):
        for k, pos_ref in enumerate((pos0_ref, pos1_ref)):
            _row_tile_copy(ring.at[slot], r, xs_ref, pos_ref[r], sems.at[slot]).start(priority=k)
        return 0

    lax.fori_loop(0, tm, start, 0, unroll=DMA_UNROLL)

    @pl.when(i == n_steps - 1)
    def _():
        wait_slot(slot)
        if n_steps > 1:
            wait_slot(1 - slot)
        ring[0] = jnp.zeros((tm * ROW_CHUNKS, LANES), ring.dtype)

        def pad_expert(e, _, wait):
            n_pad = lax.rem(TILE_E - lax.rem(cnt_ref[e], TILE_E), TILE_E)
            first = starts_ref[e] + cnt_ref[e]
            size = TILE_E // 2
            while size >= 1:
                row0 = first + (n_pad & ~(2 * size - 1))

                @pl.when((n_pad & size) != 0)
                def _(size=size, row0=row0):
                    copy = pltpu.make_async_copy(
                        ring.at[0, pl.ds(0, size * ROW_CHUNKS), :],
                        xs_ref.at[pl.ds(pl.multiple_of(row0 * ROW_CHUNKS, ROW_CHUNKS), size * ROW_CHUNKS), :],
                        zero_sem)
                    copy.wait() if wait else copy.start()

                size //= 2
            return 0

        lax.fori_loop(0, N_EXPERTS, functools.partial(pad_expert, wait=False), 0)
        last = N_EXPERTS - 1
        used_rows = starts_ref[last] + cnt_ref[last] + lax.rem(TILE_E - lax.rem(cnt_ref[last], TILE_E), TILE_E)
        n_tail = xs_ref.shape[0] // (tm * ROW_CHUNKS) - used_rows // tm

        def tail_copy(t):
            row0 = pl.multiple_of((used_rows + t * tm) * ROW_CHUNKS, tm * ROW_CHUNKS)
            return pltpu.make_async_copy(ring.at[0], xs_ref.at[pl.ds(row0, tm * ROW_CHUNKS), :], zero_sem)

        lax.fori_loop(0, n_tail, lambda t, c: (tail_copy(t).start(), c)[1], 0)
        lax.fori_loop(0, N_EXPERTS, functools.partial(pad_expert, wait=True), 0)
        lax.fori_loop(0, n_tail, lambda t, c: (tail_copy(t).wait(), c)[1], 0)


def _scatter(starts, cnt, pos, hn, n_rows):
    T = hn.shape[0] // ROW_CHUNKS
    tm = TM_ROWS
    assert TILE_E % tm == 0 and TILE_E // 2 <= tm
    n_steps = T // tm
    grid_spec = pltpu.PrefetchScalarGridSpec(
        num_scalar_prefetch=2,
        grid=(n_steps,),
        in_specs=[pl.BlockSpec((tm,), lambda i, starts, cnt: (i,), memory_space=pltpu.SMEM),
                  pl.BlockSpec((tm,), lambda i, starts, cnt: (i,), memory_space=pltpu.SMEM),
                  pl.BlockSpec((tm * ROW_CHUNKS, LANES), lambda i, starts, cnt: (i, 0))],
        out_specs=pl.BlockSpec(memory_space=pl.ANY),
        scratch_shapes=[pltpu.VMEM((2, tm * ROW_CHUNKS, LANES), hn.dtype),
                        pltpu.SemaphoreType.DMA((2,)), pltpu.SemaphoreType.DMA(())],
    )
    return pl.pallas_call(
        functools.partial(_scatter_kernel, n_steps=n_steps),
        grid_spec=grid_spec,
        out_shape=jax.ShapeDtypeStruct((n_rows * ROW_CHUNKS, LANES), hn.dtype),
        compiler_params=_cparams(("arbitrary",)),
        name="scatter_rows",
    )(starts, cnt, pos[0], pos[1], hn)


def _experts_kernel(te_ref, tb_ref, tr_ref, xs_ref, wg_ref, wu_ref, wd_ref, y_ref, wgu_bf, wd_bf, stage_ref):
    del tb_ref
    i = pl.program_id(0)
    rows = tr_ref[i]
    new_expert = jnp.logical_or(i == 0, te_ref[i] != te_ref[jnp.maximum(i - 1, 0)])

    @pl.when(jnp.logical_and(rows > 0, new_expert))
    def _():
        wgu_bf[:, :D_EXPERT] = wg_ref[0].astype(BF16)
        wgu_bf[:, D_EXPERT:] = wu_ref[0].astype(BF16)
        wd_bf[...] = wd_ref[0].astype(BF16)

    half = TILE_E // 2
    for part in range(2):
        @pl.when(rows > part * half)
        def _(part=part):
            x = _load_row_tiles(xs_ref, stage_ref, half, part * half).astype(BF16)
            gu = jnp.dot(x, wgu_bf[...], preferred_element_type=F32)
            g, up = gu[:, :D_EXPERT], gu[:, D_EXPERT:]
            hmid = (g * jax.nn.sigmoid(g) * up).astype(BF16)
            _store_row_tiles(y_ref, None, jnp.dot(hmid, wd_bf[...], preferred_element_type=F32), part * half)

        @pl.when(rows <= part * half)
        def _(part=part):
            y_ref[part * half * ROW_CHUNKS:(part + 1) * half * ROW_CHUNKS, :] = (
                jnp.zeros((half * ROW_CHUNKS, LANES), y_ref.dtype))


def _experts(tile_expert, tile_block, tile_rows, xs, w_gate, w_up, w_down):
    n_tiles = tile_expert.shape[0]
    D = D_MODEL
    rows_spec = pl.BlockSpec((TILE_E * ROW_CHUNKS, LANES), lambda i, te, tb, tr: (tb[i], 0))
    weights = lambda shape: pl.BlockSpec(shape, lambda i, te, tb, tr: (te[i], 0, 0))
    grid_spec = pltpu.PrefetchScalarGridSpec(
        num_scalar_prefetch=3,
        grid=(n_tiles,),
        in_specs=[rows_spec, weights((1, D, D_EXPERT)), weights((1, D, D_EXPERT)), weights((1, D_EXPERT, D))],
        out_specs=pl.BlockSpec((TILE_E * ROW_CHUNKS, LANES), lambda i, te, tb, tr: (i, 0)),
        scratch_shapes=[pltpu.VMEM((D, 2 * D_EXPERT), BF16),
                        pltpu.VMEM((D_EXPERT, D), BF16), pltpu.VMEM((TILE_E * ROW_CHUNKS, LANES), F32)],
    )
    return pl.pallas_call(
        _experts_kernel,
        grid_spec=grid_spec,
        out_shape=jax.ShapeDtypeStruct(xs.shape, F32),
        compiler_params=_cparams(("arbitrary",)),
        name="experts",
    )(tile_expert, tile_block, tile_rows, xs, w_gate, w_up, w_down)


def _combine_kernel(pos0_ref, pos1_ref, next0_ref, next1_ref, h_ref, gate_ref, y_ref, o_ref, ybuf, sems, *, n_steps):
    i = pl.program_id(0)
    tm = h_ref.shape[0]
    slot = lax.rem(i, 2)

    def gather(p_refs, s):
        def start(r, _):
            for k, p_ref in enumerate(p_refs):
                _row_tile_copy(y_ref, p_ref[r], ybuf.at[s, k], r, sems.at[s]).start(priority=k)
            return 0

        lax.fori_loop(0, tm, start, 0, unroll=DMA_UNROLL)

    @pl.when(i == 0)
    def _():
        gather((pos0_ref, pos1_ref), slot)

    @pl.when(i + 1 < n_steps)
    def _():
        gather((next0_ref, next1_ref), 1 - slot)

    for k in range(2):
        pltpu.make_async_copy(y_ref.at[pl.ds(0, tm * ROW_CHUNKS), :], ybuf.at[slot, k], sems.at[slot]).wait()
    g = gate_ref[...]
    o_ref[...] = (h_ref[...] + g[:, 0:1] * _load_row_tiles(ybuf.at[slot, 0], None, tm)
                  + g[:, 1:2] * _load_row_tiles(ybuf.at[slot, 1], None, tm))


def _combine(pos, h, gate_t, y):
    T, D = h.shape
    tm = TM_ROWS
    n_steps = T // tm
    return pl.pallas_call(
        functools.partial(_combine_kernel, n_steps=n_steps),
        grid=(n_steps,),
        in_specs=[pl.BlockSpec((tm,), lambda i: (i,), memory_space=pltpu.SMEM),
                  pl.BlockSpec((tm,), lambda i: (i,), memory_space=pltpu.SMEM),
                  pl.BlockSpec((tm,), lambda i: (jnp.minimum(i + 1, n_steps - 1),), memory_space=pltpu.SMEM),
                  pl.BlockSpec((tm,), lambda i: (jnp.minimum(i + 1, n_steps - 1),), memory_space=pltpu.SMEM),
                  pl.BlockSpec((tm, D), lambda i: (i, 0)),
                  pl.BlockSpec((tm, 2), lambda i: (i, 0)),
                  pl.BlockSpec(memory_space=pl.ANY)],
        out_specs=pl.BlockSpec((tm, D), lambda i: (i, 0)),
        out_shape=jax.ShapeDtypeStruct((T, D), F32),
        scratch_shapes=[pltpu.VMEM((2, 2, tm * ROW_CHUNKS, LANES), y.dtype), pltpu.SemaphoreType.DMA((2,))],
        compiler_params=_cparams(("arbitrary",)),
        name="combine",
    )(pos[0], pos[1], pos[0], pos[1], h, gate_t, y)


def _rope_tables(S):
    inv_freq = 1.0 / (ROPE_THETA ** (np.arange(0, HEAD_DIM, 2, dtype=np.float64) / HEAD_DIM))
    ang = np.arange(S, dtype=np.float64)[:, None] * inv_freq[None, :]
    widen = lambda t: jnp.asarray(np.pad(t, ((0, 0), (0, LANES - HEAD_DIM // 2))).astype(np.float32))
    return widen(np.cos(ang)), widen(np.sin(ang))


def _layer(x, norm_mix, w_in, b_forget, q_norm_fox, k_norm_fox, q_norm_dil, k_norm_dil,
           out_norm_fox, out_norm_dil, w_out, norm_ffn, w_router_group, b_router_group,
           w_router_expert, b_router_expert, w_gate, w_up, w_down):
    B, S, D = x.shape
    T = B * S
    n_main = 6 * D_GRP

    w_main = w_in.astype(BF16)
    w_vt = w_main[:, 2 * D_GRP:3 * D_GRP].T
    w_f = jnp.pad(w_main[:, n_main:], ((0, 0), (0, LANES - N_HEADS)))
    b_f = jnp.pad(b_forget, (0, LANES - N_HEADS))[None, :]
    per_head = lambda g: jnp.tile(g, N_HEADS)[None, :]
    bd = jnp.kron(jnp.eye(N_HEADS, dtype=F32), jnp.ones((HEAD_DIM, HEAD_DIM), F32)).astype(BF16)
    cos_t, sin_t = _rope_tables(S)
    tri = jnp.tril(jnp.ones((TRI_ROWS, TRI_ROWS), F32)).astype(BF16)
    upper = jnp.triu(jnp.ones((TM_OUT, TM_OUT), F32), k=1).astype(BF16)
    w_r = jnp.concatenate([
        jnp.pad(w_router_group.T, ((0, 8 - N_GROUPS), (0, 0))),
        w_router_expert.transpose(0, 2, 1).reshape(N_EXPERTS, D)], axis=0)
    b_r = jnp.concatenate([jnp.pad(b_router_group, (0, 8 - N_GROUPS)), b_router_expert.reshape(-1)])[:, None]

    bound = (HEAD_DIM / math.sqrt(HEAD_DIM)) * LOG2E * jnp.max(jnp.abs(q_norm_fox)) * jnp.max(jnp.abs(k_norm_fox))
    shift = FOX_BOUND_SLACK * bound + 1.0
    qa, ka, va, qb, kb, vb, c2 = _inproj(
        x, norm_mix[None, :], w_main, w_vt, w_f, b_f, jnp.full((1, LANES), shift, F32),
        per_head(q_norm_fox), per_head(k_norm_fox),
        per_head(q_norm_dil), per_head(k_norm_dil), bd, cos_t, sin_t, tri)
    n_q, n_k = S // TQ, S // TK_SUB
    c_first = c2[:, ::TQ, :].reshape(B, n_q, 1, N_PAIRS, HEADS_PER_TILE)
    c_last = c2[:, TK_SUB - 1::TK_SUB, :].reshape(B, 1, n_k, N_PAIRS, HEADS_PER_TILE)
    all_zero = jnp.all(c_first - c_last < -FOX_ZERO_BITS, axis=-1)
    live_so_far = jnp.einsum("bqkp,kj->bqjp", 1 - all_zero.astype(I32),
                             jnp.triu(jnp.ones((n_k, n_k), I32)))
    leading = (live_so_far == 0).astype(I32)
    before = (jnp.arange(n_k) * TK_SUB + TK_SUB <= jnp.arange(n_q)[:, None] * TQ)
    first_tile = jnp.sum(leading * before[None, :, :, None].astype(I32), axis=2)
    first_tile = first_tile.transpose(0, 2, 1).reshape(-1).astype(I32)
    oa = lax.cond(2.0 * shift <= FOX_SAFE_SPAN,
                  lambda: _fox(first_tile, qa, ka, va, online=False),
                  lambda: _fox(jnp.zeros_like(first_tile), qa, ka, va, online=True))
    dil_shift = (FOX_BOUND_SLACK * (HEAD_DIM / math.sqrt(HEAD_DIM))
                 * jnp.max(jnp.abs(q_norm_dil)) * jnp.max(jnp.abs(k_norm_dil)) + 1.0)
    dil_shift_arr = jnp.full((1, 1), dil_shift, F32)
    ob = lax.cond(2.0 * dil_shift <= DIL_SAFE_SPAN,
                  lambda: _dilated(dil_shift_arr, qb, kb, vb, bounded=True),
                  lambda: _dilated(dil_shift_arr, qb, kb, vb, bounded=False))

    h, hn, eid, gate, rank, cnt = _outproj(
        oa.reshape(T, D_GRP), ob.reshape(T, D_GRP), x.reshape(T, D), out_norm_fox[None, :],
        out_norm_dil[None, :], w_out.astype(BF16), norm_ffn[None, :], w_r, b_r, upper)

    counts = cnt[:, 0]
    padded = ((counts + TILE_E - 1) // TILE_E) * TILE_E
    ends = jnp.cumsum(padded)
    starts = ends - padded
    is_expert = eid[:, :, None] == jnp.arange(N_EXPERTS, dtype=I32)
    pos = jnp.sum(jnp.where(is_expert, starts, 0), axis=-1) + rank
    n_tiles = (2 * T) // TILE_E + N_EXPERTS
    tile_index = jnp.arange(n_tiles, dtype=I32)
    tile_valid = tile_index * TILE_E < ends[-1]
    tile_block = jnp.minimum(tile_index, ends[-1] // TILE_E - 1)
    in_region = ends[None, :] <= (tile_block * TILE_E)[:, None]
    tile_expert = jnp.sum(in_region.astype(I32), axis=1)
    expert_of_tile = tile_expert[:, None] == jnp.arange(N_EXPERTS, dtype=I32)
    rows_before = tile_block * TILE_E - jnp.sum(jnp.where(expert_of_tile, starts, 0), axis=1)
    tile_rows = jnp.clip(jnp.sum(jnp.where(expert_of_tile, counts, 0), axis=1) - rows_before, 0, TILE_E)
    tile_rows = jnp.where(tile_valid, tile_rows, 0).astype(I32)

    xs = _scatter(starts, counts, pos, hn, n_tiles * TILE_E)
    y = _experts(tile_expert, tile_block, tile_rows, xs, w_gate, w_up, w_down)
    out = _combine(pos, h, gate.T, y)
    return out.reshape(B, S, D)


def kernel(x, norm_mix, w_in, b_forget, q_norm_fox, k_norm_fox, q_norm_dil, k_norm_dil, out_norm_fox,
           out_norm_dil, w_out, norm_ffn, w_router_group, b_router_group, w_router_expert,
           b_router_expert, w_gate, w_up, w_down):
    h = x
    for l in range(norm_mix.shape[0]):
        h = _layer(h, norm_mix[l], w_in[l], b_forget[l], q_norm_fox[l], k_norm_fox[l], q_norm_dil[l],
                   k_norm_dil[l], out_norm_fox[l], out_norm_dil[l], w_out[l], norm_ffn[l],
                   w_router_group[l], b_router_group[l], w_router_expert[l], b_router_expert[l],
                   w_gate[l], w_up[l], w_down[l])
    return h
```

```python
import functools
import math

import jax
import jax.numpy as jnp
import numpy as np
from jax import lax
from jax.experimental import pallas as pl
from jax.experimental.pallas import tpu as pltpu

F32 = jnp.float32
BF16 = jnp.bfloat16
I32 = jnp.int32

D_MODEL = 1024
HEAD_DIM = 64
N_HEADS = 8
D_GRP = N_HEADS * HEAD_DIM
LANES = 128
HEADS_PER_TILE = LANES // HEAD_DIM
N_PAIRS = D_GRP // LANES
DIL_PATTERNS = ((128, 1), (512, 4), (2048, 16))
BLOCK = 128
ROPE_THETA = 10000.0
N_GROUPS = 4
EXPERTS_PER_GROUP = 8
N_EXPERTS = N_GROUPS * EXPERTS_PER_GROUP
D_EXPERT = 512
EPS = 1e-6
NEG = -1e30
LOG2E = 1.4426950408889634

TM_IN = 1024
TRI_ROWS = 256
TQ = 1024
TK_WIDE = 1024
TK_SUB = 256
FOX_AHEAD = 4
FOX_ZERO_BITS = 160.0
FOX_BOUND_SLACK = 1.02
FOX_SAFE_SPAN = 100.0
DIL_SAFE_SPAN = FOX_SAFE_SPAN * math.log(2.0)
DIL_SPAN = 2048
DIL_GROUP = 16
DIL_PRE = 4
TM_OUT = 512
TM_ROWS = 512
TILE_E = 512
ROUTER_ROWS = 8 + N_EXPERTS
ROUTER_PAD = 48
ROW_CHUNKS = D_MODEL // LANES
DMA_UNROLL = 8
VMEM_LIMIT = 56 * 1024 * 1024


def _cparams(sem):
    return pltpu.CompilerParams(dimension_semantics=sem, vmem_limit_bytes=VMEM_LIMIT)


def _inproj_kernel(x_ref, gmix_ref, w_ref, wvt_ref, wf_ref, bf_ref, shift_ref, gqa_ref, gka_ref, gqb_ref, gkb_ref,
                   bd_ref, cos_ref, sin_ref, tri_ref,
                   qa_ref, ka_ref, va_ref, qb_ref, kb_ref, vb_ref, c_ref, carry_ref):
    @pl.when(pl.program_id(1) == 0)
    def _():
        carry_ref[...] = jnp.zeros_like(carry_ref)

    x = x_ref[0]
    ms = jnp.mean(x * x, axis=-1, keepdims=True)
    xn = (x * lax.rsqrt(ms + EPS) * gmix_ref[...]).astype(BF16)

    def seg(j):
        return jnp.dot(xn, w_ref[:, j * D_GRP:(j + 1) * D_GRP], preferred_element_type=F32)

    def head_norm(y, g_ref, scale):
        ss = jnp.dot((y * y).astype(BF16), bd_ref[...], preferred_element_type=F32) * (1.0 / HEAD_DIM)
        return y * lax.rsqrt(ss + EPS) * (g_ref[...] * scale)

    lane = lax.broadcasted_iota(I32, (x.shape[0], LANES), 1)
    first_half = (lane % HEAD_DIM) < (HEAD_DIM // 2)

    def spread(t):
        t = t + pltpu.roll(t, HEAD_DIM // 2, 1)
        return t + pltpu.roll(t, HEAD_DIM, 1)

    cos = spread(cos_ref[...])
    sin = jnp.where(first_half, -1.0, 1.0) * spread(sin_ref[...])

    def rope(y):
        outs = []
        for j in range(N_PAIRS):
            ys = y[:, j * LANES:(j + 1) * LANES]
            partner = jnp.where(first_half, pltpu.roll(ys, LANES - HEAD_DIM // 2, 1),
                                pltpu.roll(ys, HEAD_DIM // 2, 1))
            outs.append(ys * cos + partner * sin)
        return jnp.concatenate(outs, axis=1)

    scale = 1.0 / math.sqrt(HEAD_DIM)
    va_ref[0] = lax.dot_general(wvt_ref[...], xn, (((1,), (1,)), ((), ())),
                                preferred_element_type=F32).astype(va_ref.dtype)
    qb_ref[0] = rope(head_norm(seg(3), gqb_ref, scale)).astype(qb_ref.dtype)
    kb_ref[0] = rope(head_norm(seg(4), gkb_ref, 1.0)).astype(kb_ref.dtype)
    vb_ref[0] = seg(5).astype(vb_ref.dtype)

    fa = jnp.dot(xn, wf_ref[...], preferred_element_type=F32) + bf_ref[...]
    logf = jnp.minimum(fa, 0.0) - jnp.log1p(jnp.exp(-jnp.abs(fa)))
    hi = logf.astype(BF16)
    mid = (logf - hi.astype(F32)).astype(BF16)
    lo = (logf - hi.astype(F32) - mid.astype(F32)).astype(BF16)
    pieces = jnp.concatenate([hi, mid, lo], axis=1)
    carry = carry_ref[...]
    blocks = []
    for j in range(x.shape[0] // TRI_ROWS):
        parts = jnp.dot(tri_ref[...], pieces[j * TRI_ROWS:(j + 1) * TRI_ROWS, :], preferred_element_type=F32)
        blk = parts[:, :LANES] + parts[:, LANES:2 * LANES] + parts[:, 2 * LANES:] + carry
        carry = blk[TRI_ROWS - 1:, :]
        blocks.append(blk)
    c = jnp.concatenate(blocks, axis=0)
    carry_ref[...] = carry

    qa = head_norm(seg(0), gqa_ref, scale * LOG2E)
    ka = head_norm(seg(1), gka_ref, 1.0)
    c2 = c * LOG2E
    c_ref[0] = c2[:, :N_HEADS]
    ones = (jnp.where((lane >= HEAD_DIM + 3) & (lane < HEAD_DIM + 6), 1.0, 0.0)
            - jnp.where(lane == HEAD_DIM + 6, shift_ref[...], 0.0))
    for h in range(N_HEADS):
        cb = jnp.broadcast_to(c2[:, h:h + 1], (x.shape[0], LANES))
        hi = cb.astype(BF16).astype(F32)
        mid = (cb - hi).astype(BF16).astype(F32)
        lo = cb - hi - mid
        pieces = jnp.where(lane == HEAD_DIM, hi, jnp.where(lane == HEAD_DIM + 1, mid,
                           jnp.where(lane == HEAD_DIM + 2, lo, 0.0)))
        q_extra = pieces + ones
        k_extra = jnp.where(((lane >= HEAD_DIM) & (lane < HEAD_DIM + 3)) | (lane == HEAD_DIM + 6), 1.0, 0.0) \
            - pltpu.roll(pieces, 3, 1)
        j, odd = divmod(h, HEADS_PER_TILE)
        qp = qa[:, j * LANES:(j + 1) * LANES]
        kp = ka[:, j * LANES:(j + 1) * LANES]
        if odd:
            qp = pltpu.roll(qp, HEAD_DIM, 1)
            kp = pltpu.roll(kp, HEAD_DIM, 1)
        qa_ref[0, h] = jnp.where(lane < HEAD_DIM, qp, q_extra).astype(qa_ref.dtype)
        ka_ref[0, h] = jnp.where(lane < HEAD_DIM, kp, k_extra).astype(ka_ref.dtype)


def _inproj(x, gmix, w_main, w_vt, w_f, b_f, shift, gqa, gka, gqb, gkb, bd, cos_t, sin_t, tri):
    B, S, D = x.shape
    tm = TM_IN
    const = lambda shape: pl.BlockSpec(shape, lambda b, i: (0,) * len(shape))
    tok = lambda w, dt: jax.ShapeDtypeStruct((B, S, w), dt)
    tok_spec = lambda w: pl.BlockSpec((1, tm, w), lambda b, i: (b, i, 0))
    head_spec = pl.BlockSpec((1, N_HEADS, tm, LANES), lambda b, i: (b, 0, i, 0))
    head_shape = jax.ShapeDtypeStruct((B, N_HEADS, S, LANES), BF16)
    return pl.pallas_call(
        _inproj_kernel,
        grid=(B, S // tm),
        in_specs=[tok_spec(D), const((1, D)), const(w_main.shape), const(w_vt.shape), const(w_f.shape),
                  const((1, LANES)), const((1, LANES)),
                  const((1, D_GRP)), const((1, D_GRP)), const((1, D_GRP)), const((1, D_GRP)),
                  const((D_GRP, D_GRP)),
                  pl.BlockSpec((tm, LANES), lambda b, i: (i, 0)),
                  pl.BlockSpec((tm, LANES), lambda b, i: (i, 0)),
                  const(tri.shape)],
        out_specs=[head_spec, head_spec, pl.BlockSpec((1, D_GRP, tm), lambda b, i: (b, 0, i))]
        + [tok_spec(D_GRP)] * 3 + [tok_spec(N_HEADS)],
        out_shape=[head_shape, head_shape, jax.ShapeDtypeStruct((B, D_GRP, S), BF16),
                   tok(D_GRP, F32), tok(D_GRP, F32), tok(D_GRP, F32), tok(N_HEADS, F32)],
        scratch_shapes=[pltpu.VMEM((1, LANES), F32)],
        compiler_params=_cparams(("arbitrary", "arbitrary")),
        name="inproj",
    )(x, gmix, w_main, w_vt, w_f, b_f, shift, gqa, gka, gqb, gkb, bd, cos_t, sin_t, tri)


def _fox_kernel(first_ref, q_ref, k_ref, v_ref, o_ref, *, online):
    qi = pl.program_id(2)
    tq = q_ref.shape[2]

    def step(start, width, carry, diag):
        carry = list(carry)
        sub = min(TK_SUB, width)
        chunks = [(c, j) for c in range(width // sub) for j in range(HEADS_PER_TILE)]
        def first_query(c):
            return c * sub if diag else 0

        def score(c, j):
            k = k_ref[0, j, pl.ds(start + c * sub, sub), :]
            q = q_ref[0, j, first_query(c):, :]
            return lax.dot_general(k, q, (((1,), (1,)), ((), ())), preferred_element_type=F32)

        def tail(full, lo, new):
            return new if lo == 0 else jnp.concatenate([full[:, :lo], new], axis=1)

        scores = {cj: score(*cj) for cj in chunks[:FOX_AHEAD]}
        for n, (c, j) in enumerate(chunks):
                if n + FOX_AHEAD < len(chunks):
                    nxt = chunks[n + FOX_AHEAD]
                    scores[nxt] = score(*nxt)
                m, l, acc = carry[j]
                lo = first_query(c)
                vt = v_ref[0, :, pl.ds(start + c * sub, sub)]
                s = scores.pop((c, j))
                if diag:
                    key = lax.broadcasted_iota(I32, s.shape, 0)
                    qry = lax.broadcasted_iota(I32, s.shape, 1)
                    s = jnp.where(key <= qry, s, NEG)
                if online:
                    m_new = jnp.maximum(m[:, lo:], jnp.max(s, axis=0, keepdims=True))
                    alpha = jnp.exp2(m[:, lo:] - m_new)
                    p = jnp.exp2(s - m_new)
                    l_new = alpha * l[:, lo:] + jnp.sum(p, axis=0, keepdims=True)
                    acc_new = alpha * acc[:, lo:] + jnp.dot(vt, p.astype(BF16), preferred_element_type=F32)
                    m = tail(m, lo, m_new)
                else:
                    p = jnp.exp2(s)
                    l_new = l[:, lo:] + jnp.sum(p, axis=0, keepdims=True)
                    acc_new = acc[:, lo:] + jnp.dot(vt, p.astype(BF16), preferred_element_type=F32)
                carry[j] = (m, tail(l, lo, l_new), tail(acc, lo, acc_new))
        return tuple(carry)

    init = tuple((jnp.full((1, tq), NEG, F32), jnp.zeros((1, tq), F32), jnp.zeros((LANES, tq), F32))
                 for _ in range(HEADS_PER_TILE))
    assert TK_WIDE == tq and TK_WIDE % TK_SUB == 0
    per_wide = TK_WIDE // TK_SUB
    first = first_ref[(pl.program_id(0) * pl.num_programs(1) + pl.program_id(1)) * pl.num_programs(2) + qi]
    live = qi * per_wide - first
    n_narrow = lax.rem(live, per_wide)
    narrow_start = pl.multiple_of(first * TK_SUB, TK_SUB)
    carry = lax.switch(
        n_narrow,
        [lambda c: c] + [lambda c, w=w: step(narrow_start, w * TK_SUB, c, False) for w in range(1, per_wide)],
        init)
    wide_start = (first + n_narrow) * TK_SUB
    carry = lax.fori_loop(
        0, lax.div(live, per_wide),
        lambda i, c: step(pl.multiple_of(wide_start + i * TK_WIDE, TK_SUB), TK_WIDE, c, False), carry)
    carry = step(pl.multiple_of(qi * tq, tq), tq, carry, True)
    outs = [acc / l for (_, l, acc) in carry]
    feat = lax.broadcasted_iota(I32, (LANES, tq), 0)
    o_ref[0] = jnp.where(feat < HEAD_DIM, outs[0], outs[1]).T.astype(o_ref.dtype)


def _fox(first_tile, qa, ka, va_t, online):
    B, _, S, _ = qa.shape
    grid_spec = pltpu.PrefetchScalarGridSpec(
        num_scalar_prefetch=1,
        grid=(B, N_PAIRS, S // TQ),
        in_specs=[pl.BlockSpec((1, HEADS_PER_TILE, TQ, LANES), lambda b, hp, i, first: (b, hp, i, 0)),
                  pl.BlockSpec((1, HEADS_PER_TILE, S, LANES), lambda b, hp, i, first: (b, hp, 0, 0)),
                  pl.BlockSpec((1, LANES, S), lambda b, hp, i, first: (b, hp, 0))],
        out_specs=pl.BlockSpec((1, TQ, LANES), lambda b, hp, i, first: (b, i, hp)),
    )
    return pl.pallas_call(
        functools.partial(_fox_kernel, online=online),
        grid_spec=grid_spec,
        out_shape=jax.ShapeDtypeStruct((B, S, D_GRP), F32),
        compiler_params=_cparams(("arbitrary", "arbitrary", "arbitrary")),
        name="fox_online" if online else "fox",
    )(first_tile, qa, ka, va_t)


def _dilated_kernel(shift_ref, q_ref, kp_ref, kc_ref, vp_ref, vc_ref, o_ref, qq, kk, vv, qq4, kk4, vv4, osc, lsc,
                    *, bounded):
    u = pl.program_id(1)
    span = q_ref.shape[1]
    qq[...] = q_ref[0]
    kk[0:span, :] = kp_ref[0]
    kk[span:2 * span, :] = kc_ref[0]
    vv[0:span, :] = vp_ref[0]
    vv[span:2 * span, :] = vc_ref[0]
    for src, dst in ((qq, qq4), (kk, kk4), (vv, vv4)):
        part = src.shape[0] // DIL_PRE
        for a in range(DIL_PRE):
            dst[a * part:(a + 1) * part, :] = src[pl.ds(a, part, stride=DIL_PRE), :]

    def rows(buf, buf4, start, n, d):
        if d % DIL_PRE:
            return buf[pl.ds(start, n, stride=d), :]
        part = buf4.shape[0] // DIL_PRE
        a = lax.rem(start, DIL_PRE)
        return buf4[pl.ds(a * part + lax.div(start, DIL_PRE), n, stride=d // DIL_PRE), :]

    lane = lax.broadcasted_iota(I32, (BLOCK, LANES), 1)
    ql = lax.broadcasted_iota(I32, (BLOCK, 2 * BLOCK), 0)
    kl = lax.broadcasted_iota(I32, (BLOCK, 2 * BLOCK), 1)
    dist = ql + BLOCK - kl
    band = (dist >= 0) & (dist <= BLOCK)
    live = -shift_ref[0, 0] if bounded else 0.0
    bias = jnp.where(band, live, NEG)
    bias_first = jnp.where(band & (kl >= BLOCK), live, NEG)

    def scores(q_start, k_start, d, first):
        qs = rows(qq, qq4, q_start, BLOCK, d).astype(BF16)
        ks = rows(kk, kk4, k_start, 2 * BLOCK, d).astype(BF16)
        mask = jnp.where(first, bias_first, bias)
        out = []
        for j in range(HEADS_PER_TILE):
            qj = jnp.where(lane // HEAD_DIM == j, qs, jnp.zeros_like(qs))
            out.append(lax.dot_general(qj, ks, (((1,), (1,)), ((), ())), preferred_element_type=F32) + mask)
        return out

    def finish(s_heads, k_start, d):
        vs = rows(vv, vv4, k_start, 2 * BLOCK, d).astype(BF16)
        o_heads, lse_heads = [], []
        for s in s_heads:
            if bounded:
                p = jnp.exp(s)
                o_heads.append(jnp.dot(p.astype(BF16), vs, preferred_element_type=F32))
                lse_heads.append(jnp.sum(p, axis=-1, keepdims=True))
            else:
                m = jnp.max(s, axis=-1, keepdims=True)
                p = jnp.exp(s - m)
                l = jnp.sum(p, axis=-1, keepdims=True)
                o_heads.append(jnp.dot((p / l).astype(BF16), vs, preferred_element_type=F32))
                lse_heads.append(m + jnp.log(l))
        o = jnp.where(lane < HEAD_DIM, o_heads[0], o_heads[1])
        lse = jnp.where(lane < HEAD_DIM, lse_heads[0], lse_heads[1])
        return o, lse

    for pidx, (window, d) in enumerate(DIL_PATTERNS):
        assert window // d == BLOCK
        unit = d * BLOCK
        n_problems = (span // unit) * d
        assert n_problems % DIL_GROUP == 0

        def body(g, _, pidx=pidx, d=d, unit=unit):
            starts, s_all = [], []
            for t in range(DIL_GROUP):
                idx = g * DIL_GROUP + t
                w = idx // d
                q_start = w * unit + (idx - w * d)
                k_start = span - unit + q_start
                starts.append((q_start, k_start))
                s_all.append(scores(q_start, k_start, d, jnp.logical_and(u == 0, w == 0)))
            for (q_start, k_start), s_heads in zip(starts, s_all):
                o, lse = finish(s_heads, k_start, d)
                osc[pidx, pl.ds(q_start, BLOCK, stride=d), :] = o
                lsc[pidx, pl.ds(q_start, BLOCK, stride=d), :] = lse
            return 0

        lax.fori_loop(0, n_problems // DIL_GROUP, body, 0)

    if bounded:
        num = osc[0] + osc[1] + osc[2]
        den = lsc[0] + lsc[1] + lsc[2]
    else:
        mx = jnp.maximum(jnp.maximum(lsc[0], lsc[1]), lsc[2])
        num = jnp.zeros((span, LANES), F32)
        den = jnp.zeros((span, LANES), F32)
        for pidx in range(len(DIL_PATTERNS)):
            e = jnp.exp(lsc[pidx] - mx)
            num = num + e * osc[pidx]
            den = den + e
    o_ref[0] = (num / den).astype(o_ref.dtype)


def _dilated(shift, qb, kb, vb, bounded):
    B, S, _ = qb.shape
    span = DIL_SPAN
    cur = pl.BlockSpec((1, span, LANES), lambda b, u, hp: (b, u, hp))
    prev = pl.BlockSpec((1, span, LANES), lambda b, u, hp: (b, jnp.maximum(u - 1, 0), hp))
    return pl.pallas_call(
        functools.partial(_dilated_kernel, bounded=bounded),
        grid=(B, S // span, N_PAIRS),
        in_specs=[pl.BlockSpec(memory_space=pltpu.SMEM), cur, prev, cur, prev, cur],
        out_specs=cur,
        out_shape=jax.ShapeDtypeStruct((B, S, D_GRP), F32),
        scratch_shapes=[pltpu.VMEM((span, LANES), F32),
                        pltpu.VMEM((2 * span, LANES), F32), pltpu.VMEM((2 * span, LANES), F32),
                        pltpu.VMEM((span, LANES), F32),
                        pltpu.VMEM((2 * span, LANES), F32), pltpu.VMEM((2 * span, LANES), F32),
                        pltpu.VMEM((len(DIL_PATTERNS), span, LANES), F32),
                        pltpu.VMEM((len(DIL_PATTERNS), span, LANES), F32)],
        compiler_params=_cparams(("arbitrary", "arbitrary", "arbitrary")),
        name="dilated" if bounded else "dilated_exact",
    )(shift, qb, kb, kb, vb, vb)


def _store_row_tiles(ref, stage_ref, x, first_row=0):
    n = x.shape[0]
    target = ref if stage_ref is None else stage_ref
    for c in range(ROW_CHUNKS):
        target[pl.ds(first_row * ROW_CHUNKS + c, n, stride=ROW_CHUNKS), :] = x[:, c * LANES:(c + 1) * LANES]
    if stage_ref is not None:
        rows = slice(first_row * ROW_CHUNKS, (first_row + n) * ROW_CHUNKS)
        ref[rows, :] = stage_ref[rows, :].astype(ref.dtype)


def _load_row_tiles(ref, stage_ref, n, first_row=0):
    source = ref
    if stage_ref is not None:
        rows = slice(first_row * ROW_CHUNKS, (first_row + n) * ROW_CHUNKS)
        stage_ref[rows, :] = ref[rows, :].astype(F32)
        source = stage_ref
    return jnp.concatenate([source[pl.ds(first_row * ROW_CHUNKS + c, n, stride=ROW_CHUNKS), :]
                            for c in range(ROW_CHUNKS)], axis=1)


def _row_tile_copy(src_ref, src_row, dst_ref, dst_row, sem):
    src = src_ref.at[pl.ds(pl.multiple_of(src_row * ROW_CHUNKS, ROW_CHUNKS), ROW_CHUNKS), :]
    dst = dst_ref.at[pl.ds(pl.multiple_of(dst_row * ROW_CHUNKS, ROW_CHUNKS), ROW_CHUNKS), :]
    return pltpu.make_async_copy(src, dst, sem)


def _outproj_kernel(oa_ref, ob_ref, x_ref, gfox_ref, gdil_ref, wo_ref, gffn_ref, wr_ref, br_ref, upper_ref,
                    h_ref, hn_ref, eid_ref, gate_ref, rank_ref, cnt_ref, run_ref, stage_ref):
    @pl.when(pl.program_id(0) == 0)
    def _():
        run_ref[...] = jnp.zeros_like(run_ref)

    def norm(y, g):
        ms = jnp.mean(y * y, axis=-1, keepdims=True)
        return y * lax.rsqrt(ms + EPS) * g

    a = norm(oa_ref[...], gfox_ref[...]).astype(BF16)
    b = norm(ob_ref[...], gdil_ref[...]).astype(BF16)
    mix = (jnp.dot(a, wo_ref[0:D_GRP, :], preferred_element_type=F32)
           + jnp.dot(b, wo_ref[D_GRP:2 * D_GRP, :], preferred_element_type=F32))
    h = x_ref[...] + mix
    h_ref[...] = h
    hn = norm(h, gffn_ref[...])
    _store_row_tiles(hn_ref, stage_ref, hn)

    hn_hi = hn.astype(BF16)
    hn_lo = (hn - hn_hi.astype(F32)).astype(BF16)
    nt = (((1,), (1,)), ((), ()))
    r = lax.dot_general(wr_ref[...], hn_hi, nt, preferred_element_type=F32)
    r_lo = lax.dot_general(wr_ref[:ROUTER_PAD, :], hn_lo, nt, preferred_element_type=F32)
    z = (r[:ROUTER_ROWS, :] + r[ROUTER_PAD:ROUTER_PAD + ROUTER_ROWS, :] + r_lo[:ROUTER_ROWS, :] + br_ref[...])
    tm = z.shape[1]
    best = z[0:1, :]
    g_sel = jnp.zeros((1, tm), I32)
    for g in range(1, N_GROUPS):
        better = z[g:g + 1, :] > best
        g_sel = jnp.where(better, g, g_sel)
        best = jnp.maximum(best, z[g:g + 1, :])
    den = jnp.zeros((1, tm), F32)
    for g in range(N_GROUPS):
        den = den + jnp.exp(z[g:g + 1, :] - best)
    pg_top = 1.0 / den

    ze = jnp.zeros((EXPERTS_PER_GROUP, tm), F32)
    for g in range(N_GROUPS):
        ze = jnp.where(g_sel == g, z[8 + g * EXPERTS_PER_GROUP:8 + (g + 1) * EXPERTS_PER_GROUP, :], ze)
    e_iota = lax.broadcasted_iota(I32, ze.shape, 0)
    v1 = jnp.max(ze, axis=0, keepdims=True)
    i1 = jnp.min(jnp.where(ze == v1, e_iota, EXPERTS_PER_GROUP), axis=0, keepdims=True)
    ze2 = jnp.where(e_iota == i1, -jnp.inf, ze)
    v2 = jnp.max(ze2, axis=0, keepdims=True)
    i2 = jnp.min(jnp.where(ze2 == v2, e_iota, EXPERTS_PER_GROUP), axis=0, keepdims=True)
    e2 = jnp.exp(v2 - v1)
    inv = 1.0 / (1.0 + e2)
    gate1 = inv * pg_top
    gate2 = e2 * inv * pg_top
    eid1 = g_sel * EXPERTS_PER_GROUP + i1
    eid2 = g_sel * EXPERTS_PER_GROUP + i2

    x_iota = lax.broadcasted_iota(I32, (N_EXPERTS, tm), 0)
    hot1 = x_iota == eid1
    hot2 = x_iota == eid2
    multi = jnp.logical_or(hot1, hot2)
    before = jnp.dot(multi.astype(BF16), upper_ref[...], preferred_element_type=F32)
    slot = before + run_ref[:, 0:1]
    rank1 = jnp.sum(jnp.where(hot1, slot, 0.0), axis=0, keepdims=True)
    rank2 = jnp.sum(jnp.where(hot2, slot, 0.0), axis=0, keepdims=True)
    run_ref[...] = run_ref[...] + jnp.sum(multi.astype(F32), axis=1, keepdims=True)

    eid_ref[...] = jnp.concatenate([eid1, eid2], axis=0)
    gate_ref[...] = jnp.concatenate([gate1, gate2], axis=0)
    rank_ref[...] = jnp.concatenate([rank1, rank2], axis=0).astype(I32)
    cnt_ref[...] = run_ref[...].astype(I32)


def _outproj(oa, ob, x2, gfox, gdil, w_out, gffn, w_r, b_r, upper):
    T, D = x2.shape
    tm = TM_OUT
    const = lambda shape: pl.BlockSpec(shape, lambda i: (0,) * len(shape))
    tok = lambda w: pl.BlockSpec((tm, w), lambda i: (i, 0))
    lanes2 = pl.BlockSpec((2, tm), lambda i: (0, i))
    return pl.pallas_call(
        _outproj_kernel,
        grid=(T // tm,),
        in_specs=[tok(D_GRP), tok(D_GRP), tok(D), const((1, D_GRP)), const((1, D_GRP)), const((D, D)),
                  const((1, D)), const((2 * ROUTER_PAD, D)), const((ROUTER_ROWS, 1)), const((tm, tm))],
        out_specs=[tok(D), pl.BlockSpec((tm * ROW_CHUNKS, LANES), lambda i: (i, 0)),
                   lanes2, lanes2, lanes2, const((N_EXPERTS, LANES))],
        out_shape=[jax.ShapeDtypeStruct((T, D), F32), jax.ShapeDtypeStruct((T * ROW_CHUNKS, LANES), BF16),
                   jax.ShapeDtypeStruct((2, T), I32), jax.ShapeDtypeStruct((2, T), F32),
                   jax.ShapeDtypeStruct((2, T), I32), jax.ShapeDtypeStruct((N_EXPERTS, LANES), I32)],
        scratch_shapes=[pltpu.VMEM((N_EXPERTS, LANES), F32), pltpu.VMEM((tm * ROW_CHUNKS, LANES), F32)],
        compiler_params=_cparams(("arbitrary",)),
        name="outproj",
    )(oa, ob, x2, gfox, gdil, w_out, gffn, w_r, b_r, upper)


def _scatter_kernel(starts_ref, cnt_ref, pos0_ref, pos1_ref, hn_ref, xs_ref, ring, sems, zero_sem, *, n_steps):
    i = pl.program_id(0)
    tm = hn_ref.shape[0] // ROW_CHUNKS
    slot = lax.rem(i, 2)

    def wait_slot(s):
        for _ in range(2):
            pltpu.make_async_copy(ring.at[s], xs_ref.at[pl.ds(0, tm * ROW_CHUNKS), :], sems.at[s]).wait()

    @pl.when(i >= 2)
    def _():
        wait_slot(slot)

    ring[slot] = hn_ref[...]

    def start(r, _):
        for k, pos_ref in enumerate((pos0_ref, pos1_ref)):
            _row_tile_copy(ring.at[slot], r, xs_ref, pos_ref[r], sems.at[slot]).start(priority=k)
        return 0

    lax.fori_loop(0, tm, start, 0, unroll=DMA_UNROLL)

    @pl.when(i == n_steps - 1)
    def _():
        wait_slot(slot)
        if n_steps > 1:
            wait_slot(1 - slot)
        ring[0] = jnp.zeros((tm * ROW_CHUNKS, LANES), ring.dtype)

        def pad_expert(e, _, wait):
            n_pad = lax.rem(TILE_E - lax.rem(cnt_ref[e], TILE_E), TILE_E)
            first = starts_ref[e] + cnt_ref[e]
            size = TILE_E // 2
            while size >= 1:
                row0 = first + (n_pad & ~(2 * size - 1))

                @pl.when((n_pad & size) != 0)
                def _(size=size, row0=row0):
                    copy = pltpu.make_async_copy(
                        ring.at[0, pl.ds(0, size * ROW_CHUNKS), :],
                        xs_ref.at[pl.ds(pl.multiple_of(row0 * ROW_CHUNKS, ROW_CHUNKS), size * ROW_CHUNKS), :],
                        zero_sem)
                    copy.wait() if wait else copy.start()

                size //= 2
            return 0

        lax.fori_loop(0, N_EXPERTS, functools.partial(pad_expert, wait=False), 0)
        last = N_EXPERTS - 1
        used_rows = starts_ref[last] + cnt_ref[last] + lax.rem(TILE_E - lax.rem(cnt_ref[last], TILE_E), TILE_E)
        n_tail = xs_ref.shape[0] // (tm * ROW_CHUNKS) - used_rows // tm

        def tail_copy(t):
            row0 = pl.multiple_of((used_rows + t * tm) * ROW_CHUNKS, tm * ROW_CHUNKS)
            return pltpu.make_async_copy(ring.at[0], xs_ref.at[pl.ds(row0, tm * ROW_CHUNKS), :], zero_sem)

        lax.fori_loop(0, n_tail, lambda t, c: (tail_copy(t).start(), c)[1], 0)
        lax.fori_loop(0, N_EXPERTS, functools.partial(pad_expert, wait=True), 0)
        lax.fori_loop(0, n_tail, lambda t, c: (tail_copy(t).wait(), c)[1], 0)


def _scatter(starts, cnt, pos, hn, n_rows):
    T = hn.shape[0] // ROW_CHUNKS
    tm = TM_ROWS
    assert TILE_E % tm == 0 and TILE_E // 2 <= tm
    n_steps = T // tm
    grid_spec = pltpu.PrefetchScalarGridSpec(
        num_scalar_prefetch=2,
        grid=(n_steps,),
        in_specs=[pl.BlockSpec((tm,), lambda i, starts, cnt: (i,), memory_space=pltpu.SMEM),
                  pl.BlockSpec((tm,), lambda i, starts, cnt: (i,), memory_space=pltpu.SMEM),
                  pl.BlockSpec((tm * ROW_CHUNKS, LANES), lambda i, starts, cnt: (i, 0))],
        out_specs=pl.BlockSpec(memory_space=pl.ANY),
        scratch_shapes=[pltpu.VMEM((2, tm * ROW_CHUNKS, LANES), hn.dtype),
                        pltpu.SemaphoreType.DMA((2,)), pltpu.SemaphoreType.DMA(())],
    )
    return pl.pallas_call(
        functools.partial(_scatter_kernel, n_steps=n_steps),
        grid_spec=grid_spec,
        out_shape=jax.ShapeDtypeStruct((n_rows * ROW_CHUNKS, LANES), hn.dtype),
        compiler_params=_cparams(("arbitrary",)),
        name="scatter_rows",
    )(starts, cnt, pos[0], pos[1], hn)


def _experts_kernel(te_ref, tb_ref, tr_ref, xs_ref, wg_ref, wu_ref, wd_ref, y_ref, wgu_bf, wd_bf, stage_ref):
    del tb_ref
    i = pl.program_id(0)
    rows = tr_ref[i]
    new_expert = jnp.logical_or(i == 0, te_ref[i] != te_ref[jnp.maximum(i - 1, 0)])

    @pl.when(jnp.logical_and(rows > 0, new_expert))
    def _():
        wgu_bf[:, :D_EXPERT] = wg_ref[0].astype(BF16)
        wgu_bf[:, D_EXPERT:] = wu_ref[0].astype(BF16)
        wd_bf[...] = wd_ref[0].astype(BF16)

    half = TILE_E // 2
    for part in range(2):
        @pl.when(rows > part * half)
        def _(part=part):
            x = _load_row_tiles(xs_ref, stage_ref, half, part * half).astype(BF16)
            gu = jnp.dot(x, wgu_bf[...], preferred_element_type=F32)
            g, up = gu[:, :D_EXPERT], gu[:, D_EXPERT:]
            hmid = (g * jax.nn.sigmoid(g) * up).astype(BF16)
            _store_row_tiles(y_ref, None, jnp.dot(hmid, wd_bf[...], preferred_element_type=F32), part * half)

        @pl.when(rows <= part * half)
        def _(part=part):
            y_ref[part * half * ROW_CHUNKS:(part + 1) * half * ROW_CHUNKS, :] = (
                jnp.zeros((half * ROW_CHUNKS, LANES), y_ref.dtype))


def _experts(tile_expert, tile_block, tile_rows, xs, w_gate, w_up, w_down):
    n_tiles = tile_expert.shape[0]
    D = D_MODEL
    rows_spec = pl.BlockSpec((TILE_E * ROW_CHUNKS, LANES), lambda i, te, tb, tr: (tb[i], 0))
    weights = lambda shape: pl.BlockSpec(shape, lambda i, te, tb, tr: (te[i], 0, 0))
    grid_spec = pltpu.PrefetchScalarGridSpec(
        num_scalar_prefetch=3,
        grid=(n_tiles,),
        in_specs=[rows_spec, weights((1, D, D_EXPERT)), weights((1, D, D_EXPERT)), weights((1, D_EXPERT, D))],
        out_specs=pl.BlockSpec((TILE_E * ROW_CHUNKS, LANES), lambda i, te, tb, tr: (i, 0)),
        scratch_shapes=[pltpu.VMEM((D, 2 * D_EXPERT), BF16),
                        pltpu.VMEM((D_EXPERT, D), BF16), pltpu.VMEM((TILE_E * ROW_CHUNKS, LANES), F32)],
    )
    return pl.pallas_call(
        _experts_kernel,
        grid_spec=grid_spec,
        out_shape=jax.ShapeDtypeStruct(xs.shape, F32),
        compiler_params=_cparams(("arbitrary",)),
        name="experts",
    )(tile_expert, tile_block, tile_rows, xs, w_gate, w_up, w_down)


def _combine_kernel(pos0_ref, pos1_ref, next0_ref, next1_ref, h_ref, gate_ref, y_ref, o_ref, ybuf, sems, *, n_steps):
    i = pl.program_id(0)
    tm = h_ref.shape[0]
    slot = lax.rem(i, 2)

    def gather(p_refs, s):
        def start(r, _):
            for k, p_ref in enumerate(p_refs):
                _row_tile_copy(y_ref, p_ref[r], ybuf.at[s, k], r, sems.at[s]).start(priority=k)
            return 0

        lax.fori_loop(0, tm, start, 0, unroll=DMA_UNROLL)

    @pl.when(i == 0)
    def _():
        gather((pos0_ref, pos1_ref), slot)

    @pl.when(i + 1 < n_steps)
    def _():
        gather((next0_ref, next1_ref), 1 - slot)

    for k in range(2):
        pltpu.make_async_copy(y_ref.at[pl.ds(0, tm * ROW_CHUNKS), :], ybuf.at[slot, k], sems.at[slot]).wait()
    g = gate_ref[...]
    o_ref[...] = (h_ref[...] + g[:, 0:1] * _load_row_tiles(ybuf.at[slot, 0], None, tm)
                  + g[:, 1:2] * _load_row_tiles(ybuf.at[slot, 1], None, tm))


def _combine(pos, h, gate_t, y):
    T, D = h.shape
    tm = TM_ROWS
    n_steps = T // tm
    return pl.pallas_call(
        functools.partial(_combine_kernel, n_steps=n_steps),
        grid=(n_steps,),
        in_specs=[pl.BlockSpec((tm,), lambda i: (i,), memory_space=pltpu.SMEM),
                  pl.BlockSpec((tm,), lambda i: (i,), memory_space=pltpu.SMEM),
                  pl.BlockSpec((tm,), lambda i: (jnp.minimum(i + 1, n_steps - 1),), memory_space=pltpu.SMEM),
                  pl.BlockSpec((tm,), lambda i: (jnp.minimum(i + 1, n_steps - 1),), memory_space=pltpu.SMEM),
                  pl.BlockSpec((tm, D), lambda i: (i, 0)),
                  pl.BlockSpec((tm, 2), lambda i: (i, 0)),
                  pl.BlockSpec(memory_space=pl.ANY)],
        out_specs=pl.BlockSpec((tm, D), lambda i: (i, 0)),
        out_shape=jax.ShapeDtypeStruct((T, D), F32),
        scratch_shapes=[pltpu.VMEM((2, 2, tm * ROW_CHUNKS, LANES), y.dtype), pltpu.SemaphoreType.DMA((2,))],
        compiler_params=_cparams(("arbitrary",)),
        name="combine",
    )(pos[0], pos[1], pos[0], pos[1], h, gate_t, y)


def _rope_tables(S):
    inv_freq = 1.0 / (ROPE_THETA ** (np.arange(0, HEAD_DIM, 2, dtype=np.float64) / HEAD_DIM))
    ang = np.arange(S, dtype=np.float64)[:, None] * inv_freq[None, :]
    widen = lambda t: jnp.asarray(np.pad(t, ((0, 0), (0, LANES - HEAD_DIM // 2))).astype(np.float32))
    return widen(np.cos(ang)), widen(np.sin(ang))


def _layer(x, norm_mix, w_in, b_forget, q_norm_fox, k_norm_fox, q_norm_dil, k_norm_dil,
           out_norm_fox, out_norm_dil, w_out, norm_ffn, w_router_group, b_router_group,
           w_router_expert, b_router_expert, w_gate, w_up, w_down):
    B, S, D = x.shape
    T = B * S
    n_main = 6 * D_GRP

    w_main = w_in.astype(BF16)
    w_vt = w_main[:, 2 * D_GRP:3 * D_GRP].T
    w_f = jnp.pad(w_main[:, n_main:], ((0, 0), (0, LANES - N_HEADS)))
    b_f = jnp.pad(b_forget, (0, LANES - N_HEADS))[None, :]
    per_head = lambda g: jnp.tile(g, N_HEADS)[None, :]
    bd = jnp.kron(jnp.eye(N_HEADS, dtype=F32), jnp.ones((HEAD_DIM, HEAD_DIM), F32)).astype(BF16)
    cos_t, sin_t = _rope_tables(S)
    tri = jnp.tril(jnp.ones((TRI_ROWS, TRI_ROWS), F32)).astype(BF16)
    upper = jnp.triu(jnp.ones((TM_OUT, TM_OUT), F32), k=1).astype(BF16)
    w_r = jnp.concatenate([
        jnp.pad(w_router_group.T, ((0, 8 - N_GROUPS), (0, 0))),
        w_router_expert.transpose(0, 2, 1).reshape(N_EXPERTS, D)], axis=0)
    w_r = jnp.pad(w_r, ((0, ROUTER_PAD - ROUTER_ROWS), (0, 0)))
    w_r_hi = w_r.astype(BF16)
    w_r = jnp.concatenate([w_r_hi, (w_r - w_r_hi.astype(F32)).astype(BF16)], axis=0)
    b_r = jnp.concatenate([jnp.pad(b_router_group, (0, 8 - N_GROUPS)), b_router_expert.reshape(-1)])[:, None]

    bound = (HEAD_DIM / math.sqrt(HEAD_DIM)) * LOG2E * jnp.max(jnp.abs(q_norm_fox)) * jnp.max(jnp.abs(k_norm_fox))
    shift = FOX_BOUND_SLACK * bound + 1.0
    qa, ka, va, qb, kb, vb, c2 = _inproj(
        x, norm_mix[None, :], w_main, w_vt, w_f, b_f, jnp.full((1, LANES), shift, F32),
        per_head(q_norm_fox), per_head(k_norm_fox),
        per_head(q_norm_dil), per_head(k_norm_dil), bd, cos_t, sin_t, tri)
    n_q, n_k = S // TQ, S // TK_SUB
    c_first = c2[:, ::TQ, :].reshape(B, n_q, 1, N_PAIRS, HEADS_PER_TILE)
    c_last = c2[:, TK_SUB - 1::TK_SUB, :].reshape(B, 1, n_k, N_PAIRS, HEADS_PER_TILE)
    all_zero = jnp.all(c_first - c_last < -FOX_ZERO_BITS, axis=-1)
    live_so_far = jnp.einsum("bqkp,kj->bqjp", 1 - all_zero.astype(I32),
                             jnp.triu(jnp.ones((n_k, n_k), I32)))
    leading = (live_so_far == 0).astype(I32)
    before = (jnp.arange(n_k) * TK_SUB + TK_SUB <= jnp.arange(n_q)[:, None] * TQ)
    first_tile = jnp.sum(leading * before[None, :, :, None].astype(I32), axis=2)
    first_tile = first_tile.transpose(0, 2, 1).reshape(-1).astype(I32)
    oa = lax.cond(2.0 * shift <= FOX_SAFE_SPAN,
                  lambda: _fox(first_tile, qa, ka, va, online=False),
                  lambda: _fox(jnp.zeros_like(first_tile), qa, ka, va, online=True))
    dil_shift = (FOX_BOUND_SLACK * (HEAD_DIM / math.sqrt(HEAD_DIM))
                 * jnp.max(jnp.abs(q_norm_dil)) * jnp.max(jnp.abs(k_norm_dil)) + 1.0)
    dil_shift_arr = jnp.full((1, 1), dil_shift, F32)
    ob = lax.cond(2.0 * dil_shift <= DIL_SAFE_SPAN,
                  lambda: _dilated(dil_shift_arr, qb, kb, vb, bounded=True),
                  lambda: _dilated(dil_shift_arr, qb, kb, vb, bounded=False))

    h, hn, eid, gate, rank, cnt = _outproj(
        oa.reshape(T, D_GRP), ob.reshape(T, D_GRP), x.reshape(T, D), out_norm_fox[None, :],
        out_norm_dil[None, :], w_out.astype(BF16), norm_ffn[None, :], w_r, b_r, upper)

    counts = cnt[:, 0]
    padded = ((counts + TILE_E - 1) // TILE_E) * TILE_E
    ends = jnp.cumsum(padded)
    starts = ends - padded
    is_expert = eid[:, :, None] == jnp.arange(N_EXPERTS, dtype=I32)
    pos = jnp.sum(jnp.where(is_expert, starts, 0), axis=-1) + rank
    n_tiles = (2 * T) // TILE_E + N_EXPERTS
    tile_index = jnp.arange(n_tiles, dtype=I32)
    tile_valid = tile_index * TILE_E < ends[-1]
    tile_block = jnp.minimum(tile_index, ends[-1] // TILE_E - 1)
    in_region = ends[None, :] <= (tile_block * TILE_E)[:, None]
    tile_expert = jnp.sum(in_region.astype(I32), axis=1)
    expert_of_tile = tile_expert[:, None] == jnp.arange(N_EXPERTS, dtype=I32)
    rows_before = tile_block * TILE_E - jnp.sum(jnp.where(expert_of_tile, starts, 0), axis=1)
    tile_rows = jnp.clip(jnp.sum(jnp.where(expert_of_tile, counts, 0), axis=1) - rows_before, 0, TILE_E)
    tile_rows = jnp.where(tile_valid, tile_rows, 0).astype(I32)

    xs = _scatter(starts, counts, pos, hn, n_tiles * TILE_E)
    y = _experts(tile_expert, tile_block, tile_rows, xs, w_gate, w_up, w_down)
    out = _combine(pos, h, gate.T, y)
    return out.reshape(B, S, D)


def kernel(x, norm_mix, w_in, b_forget, q_norm_fox, k_norm_fox, q_norm_dil, k_norm_dil, out_norm_fox,
           out_norm_dil, w_out, norm_ffn, w_router_group, b_router_group, w_router_expert,
           b_router_expert, w_gate, w_up, w_down):
    h = x
    for l in range(norm_mix.shape[0]):
        h = _layer(h, norm_mix[l], w_in[l], b_forget[l], q_norm_fox[l], k_norm_fox[l], q_norm_dil[l],
                   k_norm_dil[l], out_norm_fox[l], out_norm_dil[l], w_out[l], norm_ffn[l],
                   w_router_group[l], b_router_group[l], w_router_expert[l], b_router_expert[l],
                   w_gate[l], w_up[l], w_down[l])
    return h
```

```python
import functools
import math

import jax
import jax.numpy as jnp
import numpy as np
from jax import lax
from jax.experimental import pallas as pl
from jax.experimental.pallas import tpu as pltpu

F32 = jnp.float32
BF16 = jnp.bfloat16
I32 = jnp.int32

D_MODEL = 1024
HEAD_DIM = 64
N_HEADS = 8
D_GRP = N_HEADS * HEAD_DIM
LANES = 128
HEADS_PER_TILE = LANES // HEAD_DIM
N_PAIRS = D_GRP // LANES
DIL_PATTERNS = ((128, 1), (512, 4), (2048, 16))
BLOCK = 128
ROPE_THETA = 10000.0
N_GROUPS = 4
EXPERTS_PER_GROUP = 8
N_EXPERTS = N_GROUPS * EXPERTS_PER_GROUP
D_EXPERT = 512
EPS = 1e-6
NEG = -1e30
LOG2E = 1.4426950408889634

TM_IN = 1024
TRI_ROWS = 256
TQ = 1024
TK_WIDE = 1024
TK_SUB = 256
FOX_AHEAD = 4
FOX_ZERO_BITS = 160.0
FOX_BOUND_SLACK = 1.02
FOX_SAFE_SPAN = 100.0
DIL_SAFE_SPAN = FOX_SAFE_SPAN * math.log(2.0)
DIL_SPAN = 2048
DIL_GROUP = 16
DIL_PRE = 4
TM_OUT = 512
TM_ROWS = 512
TILE_E = 1024
EXPERT_PART = 256
ROUTER_ROWS = 8 + N_EXPERTS
ROUTER_PAD = 48
ROW_CHUNKS = D_MODEL // LANES
DMA_UNROLL = 8
VMEM_LIMIT = 56 * 1024 * 1024


def _cparams(sem):
    return pltpu.CompilerParams(dimension_semantics=sem, vmem_limit_bytes=VMEM_LIMIT)


def _inproj_kernel(x_ref, gmix_ref, w_ref, wvt_ref, wf_ref, bf_ref, shift_ref, gqa_ref, gka_ref, gqb_ref, gkb_ref,
                   bd_ref, cos_ref, sin_ref, tri_ref,
                   qa_ref, ka_ref, va_ref, qb_ref, kb_ref, vb_ref, c_ref, carry_ref):
    @pl.when(pl.program_id(1) == 0)
    def _():
        carry_ref[...] = jnp.zeros_like(carry_ref)

    x = x_ref[0]
    ms = jnp.mean(x * x, axis=-1, keepdims=True)
    xn = (x * lax.rsqrt(ms + EPS) * gmix_ref[...]).astype(BF16)

    def seg(j):
        return jnp.dot(xn, w_ref[:, j * D_GRP:(j + 1) * D_GRP], preferred_element_type=F32)

    def head_norm(y, g_ref, scale):
        ss = jnp.dot((y * y).astype(BF16), bd_ref[...], preferred_element_type=F32) * (1.0 / HEAD_DIM)
        return y * lax.rsqrt(ss + EPS) * (g_ref[...] * scale)

    lane = lax.broadcasted_iota(I32, (x.shape[0], LANES), 1)
    first_half = (lane % HEAD_DIM) < (HEAD_DIM // 2)

    def spread(t):
        t = t + pltpu.roll(t, HEAD_DIM // 2, 1)
        return t + pltpu.roll(t, HEAD_DIM, 1)

    cos = spread(cos_ref[...])
    sin = jnp.where(first_half, -1.0, 1.0) * spread(sin_ref[...])

    def rope(y):
        outs = []
        for j in range(N_PAIRS):
            ys = y[:, j * LANES:(j + 1) * LANES]
            partner = jnp.where(first_half, pltpu.roll(ys, LANES - HEAD_DIM // 2, 1),
                                pltpu.roll(ys, HEAD_DIM // 2, 1))
            outs.append(ys * cos + partner * sin)
        return jnp.concatenate(outs, axis=1)

    scale = 1.0 / math.sqrt(HEAD_DIM)
    va_ref[0] = lax.dot_general(wvt_ref[...], xn, (((1,), (1,)), ((), ())),
                                preferred_element_type=F32).astype(va_ref.dtype)
    qb_ref[0] = rope(head_norm(seg(3), gqb_ref, scale)).astype(qb_ref.dtype)
    kb_ref[0] = rope(head_norm(seg(4), gkb_ref, 1.0)).astype(kb_ref.dtype)
    vb_ref[0] = seg(5).astype(vb_ref.dtype)

    fa = jnp.dot(xn, wf_ref[...], preferred_element_type=F32) + bf_ref[...]
    logf = jnp.minimum(fa, 0.0) - jnp.log1p(jnp.exp(-jnp.abs(fa)))
    hi = logf.astype(BF16)
    mid = (logf - hi.astype(F32)).astype(BF16)
    lo = (logf - hi.astype(F32) - mid.astype(F32)).astype(BF16)
    pieces = jnp.concatenate([hi, mid, lo], axis=1)
    carry = carry_ref[...]
    blocks = []
    for j in range(x.shape[0] // TRI_ROWS):
        parts = jnp.dot(tri_ref[...], pieces[j * TRI_ROWS:(j + 1) * TRI_ROWS, :], preferred_element_type=F32)
        blk = parts[:, :LANES] + parts[:, LANES:2 * LANES] + parts[:, 2 * LANES:] + carry
        carry = blk[TRI_ROWS - 1:, :]
        blocks.append(blk)
    c = jnp.concatenate(blocks, axis=0)
    carry_ref[...] = carry

    qa = head_norm(seg(0), gqa_ref, scale * LOG2E)
    ka = head_norm(seg(1), gka_ref, 1.0)
    c2 = c * LOG2E
    c_ref[0] = c2[:, :N_HEADS]
    ones = (jnp.where((lane >= HEAD_DIM + 3) & (lane < HEAD_DIM + 6), 1.0, 0.0)
            - jnp.where(lane == HEAD_DIM + 6, shift_ref[...], 0.0))
    for h in range(N_HEADS):
        cb = jnp.broadcast_to(c2[:, h:h + 1], (x.shape[0], LANES))
        hi = cb.astype(BF16).astype(F32)
        mid = (cb - hi).astype(BF16).astype(F32)
        lo = cb - hi - mid
        pieces = jnp.where(lane == HEAD_DIM, hi, jnp.where(lane == HEAD_DIM + 1, mid,
                           jnp.where(lane == HEAD_DIM + 2, lo, 0.0)))
        q_extra = pieces + ones
        k_extra = jnp.where(((lane >= HEAD_DIM) & (lane < HEAD_DIM + 3)) | (lane == HEAD_DIM + 6), 1.0, 0.0) \
            - pltpu.roll(pieces, 3, 1)
        j, odd = divmod(h, HEADS_PER_TILE)
        qp = qa[:, j * LANES:(j + 1) * LANES]
        kp = ka[:, j * LANES:(j + 1) * LANES]
        if odd:
            qp = pltpu.roll(qp, HEAD_DIM, 1)
            kp = pltpu.roll(kp, HEAD_DIM, 1)
        qa_ref[0, h] = jnp.where(lane < HEAD_DIM, qp, q_extra).astype(qa_ref.dtype)
        ka_ref[0, h] = jnp.where(lane < HEAD_DIM, kp, k_extra).astype(ka_ref.dtype)


def _inproj(x, gmix, w_main, w_vt, w_f, b_f, shift, gqa, gka, gqb, gkb, bd, cos_t, sin_t, tri):
    B, S, D = x.shape
    tm = TM_IN
    const = lambda shape: pl.BlockSpec(shape, lambda b, i: (0,) * len(shape))
    tok = lambda w, dt: jax.ShapeDtypeStruct((B, S, w), dt)
    tok_spec = lambda w: pl.BlockSpec((1, tm, w), lambda b, i: (b, i, 0))
    head_spec = pl.BlockSpec((1, N_HEADS, tm, LANES), lambda b, i: (b, 0, i, 0))
    head_shape = jax.ShapeDtypeStruct((B, N_HEADS, S, LANES), BF16)
    return pl.pallas_call(
        _inproj_kernel,
        grid=(B, S // tm),
        in_specs=[tok_spec(D), const((1, D)), const(w_main.shape), const(w_vt.shape), const(w_f.shape),
                  const((1, LANES)), const((1, LANES)),
                  const((1, D_GRP)), const((1, D_GRP)), const((1, D_GRP)), const((1, D_GRP)),
                  const((D_GRP, D_GRP)),
                  pl.BlockSpec((tm, LANES), lambda b, i: (i, 0)),
                  pl.BlockSpec((tm, LANES), lambda b, i: (i, 0)),
                  const(tri.shape)],
        out_specs=[head_spec, head_spec, pl.BlockSpec((1, D_GRP, tm), lambda b, i: (b, 0, i))]
        + [tok_spec(D_GRP)] * 3 + [tok_spec(N_HEADS)],
        out_shape=[head_shape, head_shape, jax.ShapeDtypeStruct((B, D_GRP, S), BF16),
                   tok(D_GRP, F32), tok(D_GRP, F32), tok(D_GRP, F32), tok(N_HEADS, F32)],
        scratch_shapes=[pltpu.VMEM((1, LANES), F32)],
        compiler_params=_cparams(("arbitrary", "arbitrary")),
        name="inproj",
    )(x, gmix, w_main, w_vt, w_f, b_f, shift, gqa, gka, gqb, gkb, bd, cos_t, sin_t, tri)


def _fox_kernel(first_ref, q_ref, k_ref, v_ref, o_ref, *, online):
    qi = pl.program_id(2)
    tq = q_ref.shape[2]

    def step(start, width, carry, diag):
        carry = list(carry)
        sub = min(TK_SUB, width)
        chunks = [(c, j) for c in range(width // sub) for j in range(HEADS_PER_TILE)]
        def first_query(c):
            return c * sub if diag else 0

        def score(c, j):
            k = k_ref[0, j, pl.ds(start + c * sub, sub), :]
            q = q_ref[0, j, first_query(c):, :]
            return lax.dot_general(k, q, (((1,), (1,)), ((), ())), preferred_element_type=F32)

        def tail(full, lo, new):
            return new if lo == 0 else jnp.concatenate([full[:, :lo], new], axis=1)

        scores = {cj: score(*cj) for cj in chunks[:FOX_AHEAD]}
        for n, (c, j) in enumerate(chunks):
                if n + FOX_AHEAD < len(chunks):
                    nxt = chunks[n + FOX_AHEAD]
                    scores[nxt] = score(*nxt)
                m, l, acc = carry[j]
                lo = first_query(c)
                vt = v_ref[0, :, pl.ds(start + c * sub, sub)]
                s = scores.pop((c, j))
                if diag:
                    key = lax.broadcasted_iota(I32, s.shape, 0)
                    qry = lax.broadcasted_iota(I32, s.shape, 1)
                    s = jnp.where(key <= qry, s, NEG)
                if online:
                    m_new = jnp.maximum(m[:, lo:], jnp.max(s, axis=0, keepdims=True))
                    alpha = jnp.exp2(m[:, lo:] - m_new)
                    p = jnp.exp2(s - m_new)
                    l_new = alpha * l[:, lo:] + jnp.sum(p, axis=0, keepdims=True)
                    acc_new = alpha * acc[:, lo:] + jnp.dot(vt, p.astype(BF16), preferred_element_type=F32)
                    m = tail(m, lo, m_new)
                else:
                    p = jnp.exp2(s)
                    l_new = l[:, lo:] + jnp.sum(p, axis=0, keepdims=True)
                    acc_new = acc[:, lo:] + jnp.dot(vt, p.astype(BF16), preferred_element_type=F32)
                carry[j] = (m, tail(l, lo, l_new), tail(acc, lo, acc_new))
        return tuple(carry)

    init = tuple((jnp.full((1, tq), NEG, F32), jnp.zeros((1, tq), F32), jnp.zeros((LANES, tq), F32))
                 for _ in range(HEADS_PER_TILE))
    assert TK_WIDE == tq and TK_WIDE % TK_SUB == 0
    per_wide = TK_WIDE // TK_SUB
    first = first_ref[(pl.program_id(0) * pl.num_programs(1) + pl.program_id(1)) * pl.num_programs(2) + qi]
    live = qi * per_wide - first
    n_narrow = lax.rem(live, per_wide)
    narrow_start = pl.multiple_of(first * TK_SUB, TK_SUB)
    carry = lax.switch(
        n_narrow,
        [lambda c: c] + [lambda c, w=w: step(narrow_start, w * TK_SUB, c, False) for w in range(1, per_wide)],
        init)
    wide_start = (first + n_narrow) * TK_SUB
    carry = lax.fori_loop(
        0, lax.div(live, per_wide),
        lambda i, c: step(pl.multiple_of(wide_start + i * TK_WIDE, TK_SUB), TK_WIDE, c, False), carry)
    carry = step(pl.multiple_of(qi * tq, tq), tq, carry, True)
    outs = [acc / l for (_, l, acc) in carry]
    feat = lax.broadcasted_iota(I32, (LANES, tq), 0)
    o_ref[0] = jnp.where(feat < HEAD_DIM, outs[0], outs[1]).T.astype(o_ref.dtype)


def _fox(first_tile, qa, ka, va_t, online):
    B, _, S, _ = qa.shape
    grid_spec = pltpu.PrefetchScalarGridSpec(
        num_scalar_prefetch=1,
        grid=(B, N_PAIRS, S // TQ),
        in_specs=[pl.BlockSpec((1, HEADS_PER_TILE, TQ, LANES), lambda b, hp, i, first: (b, hp, i, 0)),
                  pl.BlockSpec((1, HEADS_PER_TILE, S, LANES), lambda b, hp, i, first: (b, hp, 0, 0)),
                  pl.BlockSpec((1, LANES, S), lambda b, hp, i, first: (b, hp, 0))],
        out_specs=pl.BlockSpec((1, TQ, LANES), lambda b, hp, i, first: (b, i, hp)),
    )
    return pl.pallas_call(
        functools.partial(_fox_kernel, online=online),
        grid_spec=grid_spec,
        out_shape=jax.ShapeDtypeStruct((B, S, D_GRP), F32),
        compiler_params=_cparams(("arbitrary", "arbitrary", "arbitrary")),
        name="fox_online" if online else "fox",
    )(first_tile, qa, ka, va_t)


def _dilated_kernel(shift_ref, q_ref, kp_ref, kc_ref, vp_ref, vc_ref, o_ref, qq, kk, vv, qq4, kk4, vv4, osc, lsc,
                    *, bounded):
    u = pl.program_id(1)
    span = q_ref.shape[1]
    qq[...] = q_ref[0]
    kk[0:span, :] = kp_ref[0]
    kk[span:2 * span, :] = kc_ref[0]
    vv[0:span, :] = vp_ref[0]
    vv[span:2 * span, :] = vc_ref[0]
    for src, dst in ((qq, qq4), (kk, kk4), (vv, vv4)):
        part = src.shape[0] // DIL_PRE
        for a in range(DIL_PRE):
            dst[a * part:(a + 1) * part, :] = src[pl.ds(a, part, stride=DIL_PRE), :]

    def rows(buf, buf4, start, n, d):
        if d % DIL_PRE:
            return buf[pl.ds(start, n, stride=d), :]
        part = buf4.shape[0] // DIL_PRE
        a = lax.rem(start, DIL_PRE)
        return buf4[pl.ds(a * part + lax.div(start, DIL_PRE), n, stride=d // DIL_PRE), :]

    lane = lax.broadcasted_iota(I32, (BLOCK, LANES), 1)
    ql = lax.broadcasted_iota(I32, (BLOCK, 2 * BLOCK), 0)
    kl = lax.broadcasted_iota(I32, (BLOCK, 2 * BLOCK), 1)
    dist = ql + BLOCK - kl
    band = (dist >= 0) & (dist <= BLOCK)
    live = -shift_ref[0, 0] if bounded else 0.0
    bias = jnp.where(band, live, NEG)
    bias_first = jnp.where(band & (kl >= BLOCK), live, NEG)

    def scores(q_start, k_start, d, first):
        qs = rows(qq, qq4, q_start, BLOCK, d).astype(BF16)
        ks = rows(kk, kk4, k_start, 2 * BLOCK, d).astype(BF16)
        mask = jnp.where(first, bias_first, bias)
        out = []
        for j in range(HEADS_PER_TILE):
            qj = jnp.where(lane // HEAD_DIM == j, qs, jnp.zeros_like(qs))
            out.append(lax.dot_general(qj, ks, (((1,), (1,)), ((), ())), preferred_element_type=F32) + mask)
        return out

    def finish(s_heads, k_start, d):
        vs = rows(vv, vv4, k_start, 2 * BLOCK, d).astype(BF16)
        o_heads, lse_heads = [], []
        for s in s_heads:
            if bounded:
                p = jnp.exp(s)
                o_heads.append(jnp.dot(p.astype(BF16), vs, preferred_element_type=F32))
                lse_heads.append(jnp.sum(p, axis=-1, keepdims=True))
            else:
                m = jnp.max(s, axis=-1, keepdims=True)
                p = jnp.exp(s - m)
                l = jnp.sum(p, axis=-1, keepdims=True)
                o_heads.append(jnp.dot((p / l).astype(BF16), vs, preferred_element_type=F32))
                lse_heads.append(m + jnp.log(l))
        o = jnp.where(lane < HEAD_DIM, o_heads[0], o_heads[1])
        lse = jnp.where(lane < HEAD_DIM, lse_heads[0], lse_heads[1])
        return o, lse

    for pidx, (window, d) in enumerate(DIL_PATTERNS):
        assert window // d == BLOCK
        unit = d * BLOCK
        n_problems = (span // unit) * d
        assert n_problems % DIL_GROUP == 0

        def body(g, _, pidx=pidx, d=d, unit=unit):
            starts, s_all = [], []
            for t in range(DIL_GROUP):
                idx = g * DIL_GROUP + t
                w = idx // d
                q_start = w * unit + (idx - w * d)
                k_start = span - unit + q_start
                starts.append((q_start, k_start))
                s_all.append(scores(q_start, k_start, d, jnp.logical_and(u == 0, w == 0)))
            for (q_start, k_start), s_heads in zip(starts, s_all):
                o, lse = finish(s_heads, k_start, d)
                osc[pidx, pl.ds(q_start, BLOCK, stride=d), :] = o
                lsc[pidx, pl.ds(q_start, BLOCK, stride=d), :] = lse
            return 0

        lax.fori_loop(0, n_problems // DIL_GROUP, body, 0)

    if bounded:
        num = osc[0] + osc[1] + osc[2]
        den = lsc[0] + lsc[1] + lsc[2]
    else:
        mx = jnp.maximum(jnp.maximum(lsc[0], lsc[1]), lsc[2])
        num = jnp.zeros((span, LANES), F32)
        den = jnp.zeros((span, LANES), F32)
        for pidx in range(len(DIL_PATTERNS)):
            e = jnp.exp(lsc[pidx] - mx)
            num = num + e * osc[pidx]
            den = den + e
    o_ref[0] = (num / den).astype(o_ref.dtype)


def _dilated(shift, qb, kb, vb, bounded):
    B, S, _ = qb.shape
    span = DIL_SPAN
    cur = pl.BlockSpec((1, span, LANES), lambda b, u, hp: (b, u, hp))
    prev = pl.BlockSpec((1, span, LANES), lambda b, u, hp: (b, jnp.maximum(u - 1, 0), hp))
    return pl.pallas_call(
        functools.partial(_dilated_kernel, bounded=bounded),
        grid=(B, S // span, N_PAIRS),
        in_specs=[pl.BlockSpec(memory_space=pltpu.SMEM), cur, prev, cur, prev, cur],
        out_specs=cur,
        out_shape=jax.ShapeDtypeStruct((B, S, D_GRP), F32),
        scratch_shapes=[pltpu.VMEM((span, LANES), F32),
                        pltpu.VMEM((2 * span, LANES), F32), pltpu.VMEM((2 * span, LANES), F32),
                        pltpu.VMEM((span, LANES), F32),
                        pltpu.VMEM((2 * span, LANES), F32), pltpu.VMEM((2 * span, LANES), F32),
                        pltpu.VMEM((len(DIL_PATTERNS), span, LANES), F32),
                        pltpu.VMEM((len(DIL_PATTERNS), span, LANES), F32)],
        compiler_params=_cparams(("arbitrary", "arbitrary", "arbitrary")),
        name="dilated" if bounded else "dilated_exact",
    )(shift, qb, kb, kb, vb, vb)


def _store_row_tiles(ref, stage_ref, x, first_row=0):
    n = x.shape[0]
    target = ref if stage_ref is None else stage_ref
    for c in range(ROW_CHUNKS):
        target[pl.ds(first_row * ROW_CHUNKS + c, n, stride=ROW_CHUNKS), :] = x[:, c * LANES:(c + 1) * LANES]
    if stage_ref is not None:
        rows = slice(first_row * ROW_CHUNKS, (first_row + n) * ROW_CHUNKS)
        ref[rows, :] = stage_ref[rows, :].astype(ref.dtype)


def _load_row_tiles(ref, stage_ref, n, first_row=0):
    source = ref
    if stage_ref is not None:
        rows = slice(first_row * ROW_CHUNKS, (first_row + n) * ROW_CHUNKS)
        stage_ref[rows, :] = ref[rows, :].astype(F32)
        source = stage_ref
    return jnp.concatenate([source[pl.ds(first_row * ROW_CHUNKS + c, n, stride=ROW_CHUNKS), :]
                            for c in range(ROW_CHUNKS)], axis=1)


def _row_tile_copy(src_ref, src_row, dst_ref, dst_row, sem):
    src = src_ref.at[pl.ds(pl.multiple_of(src_row * ROW_CHUNKS, ROW_CHUNKS), ROW_CHUNKS), :]
    dst = dst_ref.at[pl.ds(pl.multiple_of(dst_row * ROW_CHUNKS, ROW_CHUNKS), ROW_CHUNKS), :]
    return pltpu.make_async_copy(src, dst, sem)


def _outproj_kernel(oa_ref, ob_ref, x_ref, gfox_ref, gdil_ref, wo_ref, gffn_ref, wr_ref, br_ref, upper_ref,
                    h_ref, hn_ref, eid_ref, gate_ref, rank_ref, cnt_ref, run_ref, stage_ref):
    @pl.when(pl.program_id(0) == 0)
    def _():
        run_ref[...] = jnp.zeros_like(run_ref)

    def norm(y, g):
        ms = jnp.mean(y * y, axis=-1, keepdims=True)
        return y * lax.rsqrt(ms + EPS) * g

    a = norm(oa_ref[...], gfox_ref[...]).astype(BF16)
    b = norm(ob_ref[...], gdil_ref[...]).astype(BF16)
    mix = (jnp.dot(a, wo_ref[0:D_GRP, :], preferred_element_type=F32)
           + jnp.dot(b, wo_ref[D_GRP:2 * D_GRP, :], preferred_element_type=F32))
    h = x_ref[...] + mix
    h_ref[...] = h
    hn = norm(h, gffn_ref[...])
    _store_row_tiles(hn_ref, stage_ref, hn)

    hn_hi = hn.astype(BF16)
    hn_lo = (hn - hn_hi.astype(F32)).astype(BF16)
    nt = (((1,), (1,)), ((), ()))
    r = lax.dot_general(wr_ref[...], hn_hi, nt, preferred_element_type=F32)
    r_lo = lax.dot_general(wr_ref[:ROUTER_PAD, :], hn_lo, nt, preferred_element_type=F32)
    z = (r[:ROUTER_ROWS, :] + r[ROUTER_PAD:ROUTER_PAD + ROUTER_ROWS, :] + r_lo[:ROUTER_ROWS, :] + br_ref[...])
    tm = z.shape[1]
    best = z[0:1, :]
    g_sel = jnp.zeros((1, tm), I32)
    for g in range(1, N_GROUPS):
        better = z[g:g + 1, :] > best
        g_sel = jnp.where(better, g, g_sel)
        best = jnp.maximum(best, z[g:g + 1, :])
    den = jnp.zeros((1, tm), F32)
    for g in range(N_GROUPS):
        den = den + jnp.exp(z[g:g + 1, :] - best)
    pg_top = 1.0 / den

    ze = jnp.zeros((EXPERTS_PER_GROUP, tm), F32)
    for g in range(N_GROUPS):
        ze = jnp.where(g_sel == g, z[8 + g * EXPERTS_PER_GROUP:8 + (g + 1) * EXPERTS_PER_GROUP, :], ze)
    e_iota = lax.broadcasted_iota(I32, ze.shape, 0)
    v1 = jnp.max(ze, axis=0, keepdims=True)
    i1 = jnp.min(jnp.where(ze == v1, e_iota, EXPERTS_PER_GROUP), axis=0, keepdims=True)
    ze2 = jnp.where(e_iota == i1, -jnp.inf, ze)
    v2 = jnp.max(ze2, axis=0, keepdims=True)
    i2 = jnp.min(jnp.where(ze2 == v2, e_iota, EXPERTS_PER_GROUP), axis=0, keepdims=True)
    e2 = jnp.exp(v2 - v1)
    inv = 1.0 / (1.0 + e2)
    gate1 = inv * pg_top
    gate2 = e2 * inv * pg_top
    eid1 = g_sel * EXPERTS_PER_GROUP + i1
    eid2 = g_sel * EXPERTS_PER_GROUP + i2

    x_iota = lax.broadcasted_iota(I32, (N_EXPERTS, tm), 0)
    hot1 = x_iota == eid1
    hot2 = x_iota == eid2
    multi = jnp.logical_or(hot1, hot2)
    before = jnp.dot(multi.astype(BF16), upper_ref[...], preferred_element_type=F32)
    slot = before + run_ref[:, 0:1]
    rank1 = jnp.sum(jnp.where(hot1, slot, 0.0), axis=0, keepdims=True)
    rank2 = jnp.sum(jnp.where(hot2, slot, 0.0), axis=0, keepdims=True)
    run_ref[...] = run_ref[...] + jnp.sum(multi.astype(F32), axis=1, keepdims=True)

    eid_ref[...] = jnp.concatenate([eid1, eid2], axis=0)
    gate_ref[...] = jnp.concatenate([gate1, gate2], axis=0)
    rank_ref[...] = jnp.concatenate([rank1, rank2], axis=0).astype(I32)
    cnt_ref[...] = run_ref[...].astype(I32)


def _outproj(oa, ob, x2, gfox, gdil, w_out, gffn, w_r, b_r, upper):
    T, D = x2.shape
    tm = TM_OUT
    const = lambda shape: pl.BlockSpec(shape, lambda i: (0,) * len(shape))
    tok = lambda w: pl.BlockSpec((tm, w), lambda i: (i, 0))
    lanes2 = pl.BlockSpec((2, tm), lambda i: (0, i))
    return pl.pallas_call(
        _outproj_kernel,
        grid=(T // tm,),
        in_specs=[tok(D_GRP), tok(D_GRP), tok(D), const((1, D_GRP)), const((1, D_GRP)), const((D, D)),
                  const((1, D)), const((2 * ROUTER_PAD, D)), const((ROUTER_ROWS, 1)), const((tm, tm))],
        out_specs=[tok(D), pl.BlockSpec((tm * ROW_CHUNKS, LANES), lambda i: (i, 0)),
                   lanes2, lanes2, lanes2, const((N_EXPERTS, LANES))],
        out_shape=[jax.ShapeDtypeStruct((T, D), F32), jax.ShapeDtypeStruct((T * ROW_CHUNKS, LANES), BF16),
                   jax.ShapeDtypeStruct((2, T), I32), jax.ShapeDtypeStruct((2, T), F32),
                   jax.ShapeDtypeStruct((2, T), I32), jax.ShapeDtypeStruct((N_EXPERTS, LANES), I32)],
        scratch_shapes=[pltpu.VMEM((N_EXPERTS, LANES), F32), pltpu.VMEM((tm * ROW_CHUNKS, LANES), F32)],
        compiler_params=_cparams(("arbitrary",)),
        name="outproj",
    )(oa, ob, x2, gfox, gdil, w_out, gffn, w_r, b_r, upper)


def _scatter_kernel(starts_ref, cnt_ref, pos0_ref, pos1_ref, hn_ref, xs_ref, ring, sems, zero_sem, *, n_steps):
    i = pl.program_id(0)
    tm = hn_ref.shape[0] // ROW_CHUNKS
    slot = lax.rem(i, 2)

    def wait_slot(s):
        for _ in range(2):
            pltpu.make_async_copy(ring.at[s], xs_ref.at[pl.ds(0, tm * ROW_CHUNKS), :], sems.at[s]).wait()

    @pl.when(i >= 2)
    def _():
        wait_slot(slot)

    ring[slot] = hn_ref[...]

    def start(r, _):
        for k, pos_ref in enumerate((pos0_ref, pos1_ref)):
            _row_tile_copy(ring.at[slot], r, xs_ref, pos_ref[r], sems.at[slot]).start(priority=k)
        return 0

    lax.fori_loop(0, tm, start, 0, unroll=DMA_UNROLL)

    @pl.when(i == n_steps - 1)
    def _():
        wait_slot(slot)
        if n_steps > 1:
            wait_slot(1 - slot)
        ring[0] = jnp.zeros((tm * ROW_CHUNKS, LANES), ring.dtype)

        def pad_expert(e, _, wait):
            n_pad = lax.rem(TILE_E - lax.rem(cnt_ref[e], TILE_E), TILE_E)
            first = starts_ref[e] + cnt_ref[e]
            size = TILE_E // 2
            while size >= 1:
                row0 = first + (n_pad & ~(2 * size - 1))

                @pl.when((n_pad & size) != 0)
                def _(size=size, row0=row0):
                    copy = pltpu.make_async_copy(
                        ring.at[0, pl.ds(0, size * ROW_CHUNKS), :],
                        xs_ref.at[pl.ds(pl.multiple_of(row0 * ROW_CHUNKS, ROW_CHUNKS), size * ROW_CHUNKS), :],
                        zero_sem)
                    copy.wait() if wait else copy.start()

                size //= 2
            return 0

        lax.fori_loop(0, N_EXPERTS, functools.partial(pad_expert, wait=False), 0)
        last = N_EXPERTS - 1
        used_rows = starts_ref[last] + cnt_ref[last] + lax.rem(TILE_E - lax.rem(cnt_ref[last], TILE_E), TILE_E)
        n_tail = xs_ref.shape[0] // (tm * ROW_CHUNKS) - used_rows // tm

        def tail_copy(t):
            row0 = pl.multiple_of((used_rows + t * tm) * ROW_CHUNKS, tm * ROW_CHUNKS)
            return pltpu.make_async_copy(ring.at[0], xs_ref.at[pl.ds(row0, tm * ROW_CHUNKS), :], zero_sem)

        lax.fori_loop(0, n_tail, lambda t, c: (tail_copy(t).start(), c)[1], 0)
        lax.fori_loop(0, N_EXPERTS, functools.partial(pad_expert, wait=True), 0)
        lax.fori_loop(0, n_tail, lambda t, c: (tail_copy(t).wait(), c)[1], 0)


def _scatter(starts, cnt, pos, hn, n_rows):
    T = hn.shape[0] // ROW_CHUNKS
    tm = TM_ROWS
    assert TILE_E % tm == 0 and TILE_E // 2 <= tm
    n_steps = T // tm
    grid_spec = pltpu.PrefetchScalarGridSpec(
        num_scalar_prefetch=2,
        grid=(n_steps,),
        in_specs=[pl.BlockSpec((tm,), lambda i, starts, cnt: (i,), memory_space=pltpu.SMEM),
                  pl.BlockSpec((tm,), lambda i, starts, cnt: (i,), memory_space=pltpu.SMEM),
                  pl.BlockSpec((tm * ROW_CHUNKS, LANES), lambda i, starts, cnt: (i, 0))],
        out_specs=pl.BlockSpec(memory_space=pl.ANY),
        scratch_shapes=[pltpu.VMEM((2, tm * ROW_CHUNKS, LANES), hn.dtype),
                        pltpu.SemaphoreType.DMA((2,)), pltpu.SemaphoreType.DMA(())],
    )
    return pl.pallas_call(
        functools.partial(_scatter_kernel, n_steps=n_steps),
        grid_spec=grid_spec,
        out_shape=jax.ShapeDtypeStruct((n_rows * ROW_CHUNKS, LANES), hn.dtype),
        compiler_params=_cparams(("arbitrary",)),
        name="scatter_rows",
    )(starts, cnt, pos[0], pos[1], hn)


def _experts_kernel(te_ref, tb_ref, tr_ref, xs_ref, wg_ref, wu_ref, wd_ref, y_ref, wgu_bf, wd_bf, stage_ref):
    del tb_ref
    i = pl.program_id(0)
    rows = tr_ref[i]
    new_expert = jnp.logical_or(i == 0, te_ref[i] != te_ref[jnp.maximum(i - 1, 0)])

    @pl.when(jnp.logical_and(rows > 0, new_expert))
    def _():
        wgu_bf[:, :D_EXPERT] = wg_ref[0].astype(BF16)
        wgu_bf[:, D_EXPERT:] = wu_ref[0].astype(BF16)
        wd_bf[...] = wd_ref[0].astype(BF16)

    half = EXPERT_PART
    for part in range(TILE_E // EXPERT_PART):
        @pl.when(rows > part * half)
        def _(part=part):
            x = _load_row_tiles(xs_ref, stage_ref, half, part * half).astype(BF16)
            gu = jnp.dot(x, wgu_bf[...], preferred_element_type=F32)
            g, up = gu[:, :D_EXPERT], gu[:, D_EXPERT:]
            hmid = (g * jax.nn.sigmoid(g) * up).astype(BF16)
            _store_row_tiles(y_ref, None, jnp.dot(hmid, wd_bf[...], preferred_element_type=F32), part * half)

        @pl.when(rows <= part * half)
        def _(part=part):
            y_ref[part * half * ROW_CHUNKS:(part + 1) * half * ROW_CHUNKS, :] = (
                jnp.zeros((half * ROW_CHUNKS, LANES), y_ref.dtype))


def _experts(tile_expert, tile_block, tile_rows, xs, w_gate, w_up, w_down):
    n_tiles = tile_expert.shape[0]
    D = D_MODEL
    rows_spec = pl.BlockSpec((TILE_E * ROW_CHUNKS, LANES), lambda i, te, tb, tr: (tb[i], 0))
    weights = lambda shape: pl.BlockSpec(shape, lambda i, te, tb, tr: (te[i], 0, 0))
    grid_spec = pltpu.PrefetchScalarGridSpec(
        num_scalar_prefetch=3,
        grid=(n_tiles,),
        in_specs=[rows_spec, weights((1, D, D_EXPERT)), weights((1, D, D_EXPERT)), weights((1, D_EXPERT, D))],
        out_specs=pl.BlockSpec((TILE_E * ROW_CHUNKS, LANES), lambda i, te, tb, tr: (i, 0)),
        scratch_shapes=[pltpu.VMEM((D, 2 * D_EXPERT), BF16),
                        pltpu.VMEM((D_EXPERT, D), BF16), pltpu.VMEM((TILE_E * ROW_CHUNKS, LANES), F32)],
    )
    return pl.pallas_call(
        _experts_kernel,
        grid_spec=grid_spec,
        out_shape=jax.ShapeDtypeStruct(xs.shape, F32),
        compiler_params=_cparams(("arbitrary",)),
        name="experts",
    )(tile_expert, tile_block, tile_rows, xs, w_gate, w_up, w_down)


def _combine_kernel(pos0_ref, pos1_ref, next0_ref, next1_ref, h_ref, gate_ref, y_ref, o_ref, ybuf, sems, *, n_steps):
    i = pl.program_id(0)
    tm = h_ref.shape[0]
    slot = lax.rem(i, 2)

    def gather(p_refs, s):
        def start(r, _):
            for k, p_ref in enumerate(p_refs):
                _row_tile_copy(y_ref, p_ref[r], ybuf.at[s, k], r, sems.at[s]).start(priority=k)
            return 0

        lax.fori_loop(0, tm, start, 0, unroll=DMA_UNROLL)

    @pl.when(i == 0)
    def _():
        gather((pos0_ref, pos1_ref), slot)

    @pl.when(i + 1 < n_steps)
    def _():
        gather((next0_ref, next1_ref), 1 - slot)

    for k in range(2):
        pltpu.make_async_copy(y_ref.at[pl.ds(0, tm * ROW_CHUNKS), :], ybuf.at[slot, k], sems.at[slot]).wait()
    g = gate_ref[...]
    o_ref[...] = (h_ref[...] + g[:, 0:1] * _load_row_tiles(ybuf.at[slot, 0], None, tm)
                  + g[:, 1:2] * _load_row_tiles(ybuf.at[slot, 1], None, tm))


def _combine(pos, h, gate_t, y):
    T, D = h.shape
    tm = TM_ROWS
    n_steps = T // tm
    return pl.pallas_call(
        functools.partial(_combine_kernel, n_steps=n_steps),
        grid=(n_steps,),
        in_specs=[pl.BlockSpec((tm,), lambda i: (i,), memory_space=pltpu.SMEM),
                  pl.BlockSpec((tm,), lambda i: (i,), memory_space=pltpu.SMEM),
                  pl.BlockSpec((tm,), lambda i: (jnp.minimum(i + 1, n_steps - 1),), memory_space=pltpu.SMEM),
                  pl.BlockSpec((tm,), lambda i: (jnp.minimum(i + 1, n_steps - 1),), memory_space=pltpu.SMEM),
                  pl.BlockSpec((tm, D), lambda i: (i, 0)),
                  pl.BlockSpec((tm, 2), lambda i: (i, 0)),
                  pl.BlockSpec(memory_space=pl.ANY)],
        out_specs=pl.BlockSpec((tm, D), lambda i: (i, 0)),
        out_shape=jax.ShapeDtypeStruct((T, D), F32),
        scratch_shapes=[pltpu.VMEM((2, 2, tm * ROW_CHUNKS, LANES), y.dtype), pltpu.SemaphoreType.DMA((2,))],
        compiler_params=_cparams(("arbitrary",)),
        name="combine",
    )(pos[0], pos[1], pos[0], pos[1], h, gate_t, y)


def _rope_tables(S):
    inv_freq = 1.0 / (ROPE_THETA ** (np.arange(0, HEAD_DIM, 2, dtype=np.float64) / HEAD_DIM))
    ang = np.arange(S, dtype=np.float64)[:, None] * inv_freq[None, :]
    widen = lambda t: jnp.asarray(np.pad(t, ((0, 0), (0, LANES - HEAD_DIM // 2))).astype(np.float32))
    return widen(np.cos(ang)), widen(np.sin(ang))


def _layer(x, norm_mix, w_in, b_forget, q_norm_fox, k_norm_fox, q_norm_dil, k_norm_dil,
           out_norm_fox, out_norm_dil, w_out, norm_ffn, w_router_group, b_router_group,
           w_router_expert, b_router_expert, w_gate, w_up, w_down):
    B, S, D = x.shape
    T = B * S
    n_main = 6 * D_GRP

    w_main = w_in.astype(BF16)
    w_vt = w_main[:, 2 * D_GRP:3 * D_GRP].T
    w_f = jnp.pad(w_main[:, n_main:], ((0, 0), (0, LANES - N_HEADS)))
    b_f = jnp.pad(b_forget, (0, LANES - N_HEADS))[None, :]
    per_head = lambda g: jnp.tile(g, N_HEADS)[None, :]
    bd = jnp.kron(jnp.eye(N_HEADS, dtype=F32), jnp.ones((HEAD_DIM, HEAD_DIM), F32)).astype(BF16)
    cos_t, sin_t = _rope_tables(S)
    tri = jnp.tril(jnp.ones((TRI_ROWS, TRI_ROWS), F32)).astype(BF16)
    upper = jnp.triu(jnp.ones((TM_OUT, TM_OUT), F32), k=1).astype(BF16)
    w_r = jnp.concatenate([
        jnp.pad(w_router_group.T, ((0, 8 - N_GROUPS), (0, 0))),
        w_router_expert.transpose(0, 2, 1).reshape(N_EXPERTS, D)], axis=0)
    w_r = jnp.pad(w_r, ((0, ROUTER_PAD - ROUTER_ROWS), (0, 0)))
    w_r_hi = w_r.astype(BF16)
    w_r = jnp.concatenate([w_r_hi, (w_r - w_r_hi.astype(F32)).astype(BF16)], axis=0)
    b_r = jnp.concatenate([jnp.pad(b_router_group, (0, 8 - N_GROUPS)), b_router_expert.reshape(-1)])[:, None]

    bound = (HEAD_DIM / math.sqrt(HEAD_DIM)) * LOG2E * jnp.max(jnp.abs(q_norm_fox)) * jnp.max(jnp.abs(k_norm_fox))
    shift = FOX_BOUND_SLACK * bound + 1.0
    qa, ka, va, qb, kb, vb, c2 = _inproj(
        x, norm_mix[None, :], w_main, w_vt, w_f, b_f, jnp.full((1, LANES), shift, F32),
        per_head(q_norm_fox), per_head(k_norm_fox),
        per_head(q_norm_dil), per_head(k_norm_dil), bd, cos_t, sin_t, tri)
    n_q, n_k = S // TQ, S // TK_SUB
    c_first = c2[:, ::TQ, :].reshape(B, n_q, 1, N_PAIRS, HEADS_PER_TILE)
    c_last = c2[:, TK_SUB - 1::TK_SUB, :].reshape(B, 1, n_k, N_PAIRS, HEADS_PER_TILE)
    all_zero = jnp.all(c_first - c_last < -FOX_ZERO_BITS, axis=-1)
    live_so_far = jnp.einsum("bqkp,kj->bqjp", 1 - all_zero.astype(I32),
                             jnp.triu(jnp.ones((n_k, n_k), I32)))
    leading = (live_so_far == 0).astype(I32)
    before = (jnp.arange(n_k) * TK_SUB + TK_SUB <= jnp.arange(n_q)[:, None] * TQ)
    first_tile = jnp.sum(leading * before[None, :, :, None].astype(I32), axis=2)
    first_tile = first_tile.transpose(0, 2, 1).reshape(-1).astype(I32)
    oa = lax.cond(2.0 * shift <= FOX_SAFE_SPAN,
                  lambda: _fox(first_tile, qa, ka, va, online=False),
                  lambda: _fox(jnp.zeros_like(first_tile), qa, ka, va, online=True))
    dil_shift = (FOX_BOUND_SLACK * (HEAD_DIM / math.sqrt(HEAD_DIM))
                 * jnp.max(jnp.abs(q_norm_dil)) * jnp.max(jnp.abs(k_norm_dil)) + 1.0)
    dil_shift_arr = jnp.full((1, 1), dil_shift, F32)
    ob = lax.cond(2.0 * dil_shift <= DIL_SAFE_SPAN,
                  lambda: _dilated(dil_shift_arr, qb, kb, vb, bounded=True),
                  lambda: _dilated(dil_shift_arr, qb, kb, vb, bounded=False))

    h, hn, eid, gate, rank, cnt = _outproj(
        oa.reshape(T, D_GRP), ob.reshape(T, D_GRP), x.reshape(T, D), out_norm_fox[None, :],
        out_norm_dil[None, :], w_out.astype(BF16), norm_ffn[None, :], w_r, b_r, upper)

    counts = cnt[:, 0]
    padded = ((counts + TILE_E - 1) // TILE_E) * TILE_E
    ends = jnp.cumsum(padded)
    starts = ends - padded
    is_expert = eid[:, :, None] == jnp.arange(N_EXPERTS, dtype=I32)
    pos = jnp.sum(jnp.where(is_expert, starts, 0), axis=-1) + rank
    n_tiles = (2 * T) // TILE_E + N_EXPERTS
    tile_index = jnp.arange(n_tiles, dtype=I32)
    tile_valid = tile_index * TILE_E < ends[-1]
    tile_block = jnp.minimum(tile_index, ends[-1] // TILE_E - 1)
    in_region = ends[None, :] <= (tile_block * TILE_E)[:, None]
    tile_expert = jnp.sum(in_region.astype(I32), axis=1)
    expert_of_tile = tile_expert[:, None] == jnp.arange(N_EXPERTS, dtype=I32)
    rows_before = tile_block * TILE_E - jnp.sum(jnp.where(expert_of_tile, starts, 0), axis=1)
    tile_rows = jnp.clip(jnp.sum(jnp.where(expert_of_tile, counts, 0), axis=1) - rows_before, 0, TILE_E)
    tile_rows = jnp.where(tile_valid, tile_rows, 0).astype(I32)

    xs = _scatter(starts, counts, pos, hn, n_tiles * TILE_E)
    y = _experts(tile_expert, tile_block, tile_rows, xs, w_gate, w_up, w_down)
    out = _combine(pos, h, gate.T, y)
    return out.reshape(B, S, D)


def kernel(x, norm_mix, w_in, b_forget, q_norm_fox, k_norm_fox, q_norm_dil, k_norm_dil, out_norm_fox,
           out_norm_dil, w_out, norm_ffn, w_router_group, b_router_group, w_router_expert,
           b_router_expert, w_gate, w_up, w_down):
    h = x
    for l in range(norm_mix.shape[0]):
        h = _layer(h, norm_mix[l], w_in[l], b_forget[l], q_norm_fox[l], k_norm_fox[l], q_norm_dil[l],
                   k_norm_dil[l], out_norm_fox[l], out_norm_dil[l], w_out[l], norm_ffn[l],
                   w_router_group[l], b_router_group[l], w_router_expert[l], b_router_expert[l],
                   w_gate[l], w_up[l], w_down[l])
    return h
```

```python
import functools
import math

import jax
import jax.numpy as jnp
import numpy as np
from jax import lax
from jax.experimental import pallas as pl
from jax.experimental.pallas import tpu as pltpu

F32 = jnp.float32
BF16 = jnp.bfloat16
I32 = jnp.int32

D_MODEL = 1024
HEAD_DIM = 64
N_HEADS = 8
D_GRP = N_HEADS * HEAD_DIM
LANES = 128
HEADS_PER_TILE = LANES // HEAD_DIM
N_PAIRS = D_GRP // LANES
DIL_PATTERNS = ((128, 1), (512, 4), (2048, 16))
BLOCK = 128
ROPE_THETA = 10000.0
N_GROUPS = 4
EXPERTS_PER_GROUP = 8
N_EXPERTS = N_GROUPS * EXPERTS_PER_GROUP
D_EXPERT = 512
EPS = 1e-6
NEG = -1e30
LOG2E = 1.4426950408889634

TM_IN = 1024
TRI_ROWS = 256
TQ = 1024
TK_WIDE = 1024
TK_SUB = 256
FOX_AHEAD = 4
FOX_ZERO_BITS = 160.0
FOX_BOUND_SLACK = 1.02
FOX_SAFE_SPAN = 100.0
DIL_SAFE_SPAN = FOX_SAFE_SPAN * math.log(2.0)
DIL_SPAN = 2048
DIL_GROUP = 16
DIL_PRE = 4
TM_OUT = 1024
TM_ROWS = 512
TILE_E = 512
ROUTER_ROWS = 8 + N_EXPERTS
ROUTER_PAD = 48
ROW_CHUNKS = D_MODEL // LANES
DMA_UNROLL = 8
VMEM_LIMIT = 56 * 1024 * 1024


def _cparams(sem):
    return pltpu.CompilerParams(dimension_semantics=sem, vmem_limit_bytes=VMEM_LIMIT)


def _inproj_kernel(x_ref, gmix_ref, w_ref, wvt_ref, wf_ref, bf_ref, shift_ref, gqa_ref, gka_ref, gqb_ref, gkb_ref,
                   bd_ref, cos_ref, sin_ref, tri_ref,
                   qa_ref, ka_ref, va_ref, qb_ref, kb_ref, vb_ref, c_ref, carry_ref):
    @pl.when(pl.program_id(1) == 0)
    def _():
        carry_ref[...] = jnp.zeros_like(carry_ref)

    x = x_ref[0]
    ms = jnp.mean(x * x, axis=-1, keepdims=True)
    xn = (x * lax.rsqrt(ms + EPS) * gmix_ref[...]).astype(BF16)

    def seg(j):
        return jnp.dot(xn, w_ref[:, j * D_GRP:(j + 1) * D_GRP], preferred_element_type=F32)

    def head_norm(y, g_ref, scale):
        ss = jnp.dot((y * y).astype(BF16), bd_ref[...], preferred_element_type=F32) * (1.0 / HEAD_DIM)
        return y * lax.rsqrt(ss + EPS) * (g_ref[...] * scale)

    lane = lax.broadcasted_iota(I32, (x.shape[0], LANES), 1)
    first_half = (lane % HEAD_DIM) < (HEAD_DIM // 2)

    def spread(t):
        t = t + pltpu.roll(t, HEAD_DIM // 2, 1)
        return t + pltpu.roll(t, HEAD_DIM, 1)

    cos = spread(cos_ref[...])
    sin = jnp.where(first_half, -1.0, 1.0) * spread(sin_ref[...])

    def rope(y):
        outs = []
        for j in range(N_PAIRS):
            ys = y[:, j * LANES:(j + 1) * LANES]
            partner = jnp.where(first_half, pltpu.roll(ys, LANES - HEAD_DIM // 2, 1),
                                pltpu.roll(ys, HEAD_DIM // 2, 1))
            outs.append(ys * cos + partner * sin)
        return jnp.concatenate(outs, axis=1)

    scale = 1.0 / math.sqrt(HEAD_DIM)
    va_ref[0] = lax.dot_general(wvt_ref[...], xn, (((1,), (1,)), ((), ())),
                                preferred_element_type=F32).astype(va_ref.dtype)
    qb_ref[0] = rope(head_norm(seg(3), gqb_ref, scale)).astype(qb_ref.dtype)
    kb_ref[0] = rope(head_norm(seg(4), gkb_ref, 1.0)).astype(kb_ref.dtype)
    vb_ref[0] = seg(5).astype(vb_ref.dtype)

    fa = jnp.dot(xn, wf_ref[...], preferred_element_type=F32) + bf_ref[...]
    logf = jnp.minimum(fa, 0.0) - jnp.log1p(jnp.exp(-jnp.abs(fa)))
    hi = logf.astype(BF16)
    mid = (logf - hi.astype(F32)).astype(BF16)
    lo = (logf - hi.astype(F32) - mid.astype(F32)).astype(BF16)
    pieces = jnp.concatenate([hi, mid, lo], axis=1)
    carry = carry_ref[...]
    blocks = []
    for j in range(x.shape[0] // TRI_ROWS):
        parts = jnp.dot(tri_ref[...], pieces[j * TRI_ROWS:(j + 1) * TRI_ROWS, :], preferred_element_type=F32)
        blk = parts[:, :LANES] + parts[:, LANES:2 * LANES] + parts[:, 2 * LANES:] + carry
        carry = blk[TRI_ROWS - 1:, :]
        blocks.append(blk)
    c = jnp.concatenate(blocks, axis=0)
    carry_ref[...] = carry

    qa = head_norm(seg(0), gqa_ref, scale * LOG2E)
    ka = head_norm(seg(1), gka_ref, 1.0)
    c2 = c * LOG2E
    c_ref[0] = c2[:, :N_HEADS]
    ones = (jnp.where((lane >= HEAD_DIM + 3) & (lane < HEAD_DIM + 6), 1.0, 0.0)
            - jnp.where(lane == HEAD_DIM + 6, shift_ref[...], 0.0))
    for h in range(N_HEADS):
        cb = jnp.broadcast_to(c2[:, h:h + 1], (x.shape[0], LANES))
        hi = cb.astype(BF16).astype(F32)
        mid = (cb - hi).astype(BF16).astype(F32)
        lo = cb - hi - mid
        pieces = jnp.where(lane == HEAD_DIM, hi, jnp.where(lane == HEAD_DIM + 1, mid,
                           jnp.where(lane == HEAD_DIM + 2, lo, 0.0)))
        q_extra = pieces + ones
        k_extra = jnp.where(((lane >= HEAD_DIM) & (lane < HEAD_DIM + 3)) | (lane == HEAD_DIM + 6), 1.0, 0.0) \
            - pltpu.roll(pieces, 3, 1)
        j, odd = divmod(h, HEADS_PER_TILE)
        qp = qa[:, j * LANES:(j + 1) * LANES]
        kp = ka[:, j * LANES:(j + 1) * LANES]
        if odd:
            qp = pltpu.roll(qp, HEAD_DIM, 1)
            kp = pltpu.roll(kp, HEAD_DIM, 1)
        qa_ref[0, h] = jnp.where(lane < HEAD_DIM, qp, q_extra).astype(qa_ref.dtype)
        ka_ref[0, h] = jnp.where(lane < HEAD_DIM, kp, k_extra).astype(ka_ref.dtype)


def _inproj(x, gmix, w_main, w_vt, w_f, b_f, shift, gqa, gka, gqb, gkb, bd, cos_t, sin_t, tri):
    B, S, D = x.shape
    tm = TM_IN
    const = lambda shape: pl.BlockSpec(shape, lambda b, i: (0,) * len(shape))
    tok = lambda w, dt: jax.ShapeDtypeStruct((B, S, w), dt)
    tok_spec = lambda w: pl.BlockSpec((1, tm, w), lambda b, i: (b, i, 0))
    head_spec = pl.BlockSpec((1, N_HEADS, tm, LANES), lambda b, i: (b, 0, i, 0))
    head_shape = jax.ShapeDtypeStruct((B, N_HEADS, S, LANES), BF16)
    return pl.pallas_call(
        _inproj_kernel,
        grid=(B, S // tm),
        in_specs=[tok_spec(D), const((1, D)), const(w_main.shape), const(w_vt.shape), const(w_f.shape),
                  const((1, LANES)), const((1, LANES)),
                  const((1, D_GRP)), const((1, D_GRP)), const((1, D_GRP)), const((1, D_GRP)),
                  const((D_GRP, D_GRP)),
                  pl.BlockSpec((tm, LANES), lambda b, i: (i, 0)),
                  pl.BlockSpec((tm, LANES), lambda b, i: (i, 0)),
                  const(tri.shape)],
        out_specs=[head_spec, head_spec, pl.BlockSpec((1, D_GRP, tm), lambda b, i: (b, 0, i))]
        + [tok_spec(D_GRP)] * 3 + [tok_spec(N_HEADS)],
        out_shape=[head_shape, head_shape, jax.ShapeDtypeStruct((B, D_GRP, S), BF16),
                   tok(D_GRP, F32), tok(D_GRP, F32), tok(D_GRP, F32), tok(N_HEADS, F32)],
        scratch_shapes=[pltpu.VMEM((1, LANES), F32)],
        compiler_params=_cparams(("arbitrary", "arbitrary")),
        name="inproj",
    )(x, gmix, w_main, w_vt, w_f, b_f, shift, gqa, gka, gqb, gkb, bd, cos_t, sin_t, tri)


def _fox_kernel(first_ref, q_ref, k_ref, v_ref, o_ref, *, online):
    qi = pl.program_id(2)
    tq = q_ref.shape[2]

    def step(start, width, carry, diag):
        carry = list(carry)
        sub = min(TK_SUB, width)
        chunks = [(c, j) for c in range(width // sub) for j in range(HEADS_PER_TILE)]
        def first_query(c):
            return c * sub if diag else 0

        def score(c, j):
            k = k_ref[0, j, pl.ds(start + c * sub, sub), :]
            q = q_ref[0, j, first_query(c):, :]
            return lax.dot_general(k, q, (((1,), (1,)), ((), ())), preferred_element_type=F32)

        def tail(full, lo, new):
            return new if lo == 0 else jnp.concatenate([full[:, :lo], new], axis=1)

        scores = {cj: score(*cj) for cj in chunks[:FOX_AHEAD]}
        for n, (c, j) in enumerate(chunks):
                if n + FOX_AHEAD < len(chunks):
                    nxt = chunks[n + FOX_AHEAD]
                    scores[nxt] = score(*nxt)
                m, l, acc = carry[j]
                lo = first_query(c)
                vt = v_ref[0, :, pl.ds(start + c * sub, sub)]
                s = scores.pop((c, j))
                if diag:
                    key = lax.broadcasted_iota(I32, s.shape, 0)
                    qry = lax.broadcasted_iota(I32, s.shape, 1)
                    s = jnp.where(key <= qry, s, NEG)
                if online:
                    m_new = jnp.maximum(m[:, lo:], jnp.max(s, axis=0, keepdims=True))
                    alpha = jnp.exp2(m[:, lo:] - m_new)
                    p = jnp.exp2(s - m_new)
                    l_new = alpha * l[:, lo:] + jnp.sum(p, axis=0, keepdims=True)
                    acc_new = alpha * acc[:, lo:] + jnp.dot(vt, p.astype(BF16), preferred_element_type=F32)
                    m = tail(m, lo, m_new)
                else:
                    p = jnp.exp2(s)
                    l_new = l[:, lo:] + jnp.sum(p, axis=0, keepdims=True)
                    acc_new = acc[:, lo:] + jnp.dot(vt, p.astype(BF16), preferred_element_type=F32)
                carry[j] = (m, tail(l, lo, l_new), tail(acc, lo, acc_new))
        return tuple(carry)

    init = tuple((jnp.full((1, tq), NEG, F32), jnp.zeros((1, tq), F32), jnp.zeros((LANES, tq), F32))
                 for _ in range(HEADS_PER_TILE))
    assert TK_WIDE == tq and TK_WIDE % TK_SUB == 0
    per_wide = TK_WIDE // TK_SUB
    first = first_ref[(pl.program_id(0) * pl.num_programs(1) + pl.program_id(1)) * pl.num_programs(2) + qi]
    live = qi * per_wide - first
    n_narrow = lax.rem(live, per_wide)
    narrow_start = pl.multiple_of(first * TK_SUB, TK_SUB)
    carry = lax.switch(
        n_narrow,
        [lambda c: c] + [lambda c, w=w: step(narrow_start, w * TK_SUB, c, False) for w in range(1, per_wide)],
        init)
    wide_start = (first + n_narrow) * TK_SUB
    carry = lax.fori_loop(
        0, lax.div(live, per_wide),
        lambda i, c: step(pl.multiple_of(wide_start + i * TK_WIDE, TK_SUB), TK_WIDE, c, False), carry)
    carry = step(pl.multiple_of(qi * tq, tq), tq, carry, True)
    outs = [acc / l for (_, l, acc) in carry]
    feat = lax.broadcasted_iota(I32, (LANES, tq), 0)
    o_ref[0] = jnp.where(feat < HEAD_DIM, outs[0], outs[1]).T.astype(o_ref.dtype)


def _fox(first_tile, qa, ka, va_t, online):
    B, _, S, _ = qa.shape
    grid_spec = pltpu.PrefetchScalarGridSpec(
        num_scalar_prefetch=1,
        grid=(B, N_PAIRS, S // TQ),
        in_specs=[pl.BlockSpec((1, HEADS_PER_TILE, TQ, LANES), lambda b, hp, i, first: (b, hp, i, 0)),
                  pl.BlockSpec((1, HEADS_PER_TILE, S, LANES), lambda b, hp, i, first: (b, hp, 0, 0)),
                  pl.BlockSpec((1, LANES, S), lambda b, hp, i, first: (b, hp, 0))],
        out_specs=pl.BlockSpec((1, TQ, LANES), lambda b, hp, i, first: (b, i, hp)),
    )
    return pl.pallas_call(
        functools.partial(_fox_kernel, online=online),
        grid_spec=grid_spec,
        out_shape=jax.ShapeDtypeStruct((B, S, D_GRP), F32),
        compiler_params=_cparams(("arbitrary", "arbitrary", "arbitrary")),
        name="fox_online" if online else "fox",
    )(first_tile, qa, ka, va_t)


def _dilated_kernel(shift_ref, q_ref, kp_ref, kc_ref, vp_ref, vc_ref, o_ref, qq, kk, vv, qq4, kk4, vv4, osc, lsc,
                    *, bounded):
    u = pl.program_id(1)
    span = q_ref.shape[1]
    qq[...] = q_ref[0]
    kk[0:span, :] = kp_ref[0]
    kk[span:2 * span, :] = kc_ref[0]
    vv[0:span, :] = vp_ref[0]
    vv[span:2 * span, :] = vc_ref[0]
    for src, dst in ((qq, qq4), (kk, kk4), (vv, vv4)):
        part = src.shape[0] // DIL_PRE
        for a in range(DIL_PRE):
            dst[a * part:(a + 1) * part, :] = src[pl.ds(a, part, stride=DIL_PRE), :]

    def rows(buf, buf4, start, n, d):
        if d % DIL_PRE:
            return buf[pl.ds(start, n, stride=d), :]
        part = buf4.shape[0] // DIL_PRE
        a = lax.rem(start, DIL_PRE)
        return buf4[pl.ds(a * part + lax.div(start, DIL_PRE), n, stride=d // DIL_PRE), :]

    lane = lax.broadcasted_iota(I32, (BLOCK, LANES), 1)
    ql = lax.broadcasted_iota(I32, (BLOCK, 2 * BLOCK), 0)
    kl = lax.broadcasted_iota(I32, (BLOCK, 2 * BLOCK), 1)
    dist = ql + BLOCK - kl
    band = (dist >= 0) & (dist <= BLOCK)
    live = -shift_ref[0, 0] if bounded else 0.0
    bias = jnp.where(band, live, NEG)
    bias_first = jnp.where(band & (kl >= BLOCK), live, NEG)

    def scores(q_start, k_start, d, first):
        qs = rows(qq, qq4, q_start, BLOCK, d).astype(BF16)
        ks = rows(kk, kk4, k_start, 2 * BLOCK, d).astype(BF16)
        mask = jnp.where(first, bias_first, bias)
        out = []
        for j in range(HEADS_PER_TILE):
            qj = jnp.where(lane // HEAD_DIM == j, qs, jnp.zeros_like(qs))
            out.append(lax.dot_general(qj, ks, (((1,), (1,)), ((), ())), preferred_element_type=F32) + mask)
        return out

    def finish(s_heads, k_start, d):
        vs = rows(vv, vv4, k_start, 2 * BLOCK, d).astype(BF16)
        o_heads, lse_heads = [], []
        for s in s_heads:
            if bounded:
                p = jnp.exp(s)
                o_heads.append(jnp.dot(p.astype(BF16), vs, preferred_element_type=F32))
                lse_heads.append(jnp.sum(p, axis=-1, keepdims=True))
            else:
                m = jnp.max(s, axis=-1, keepdims=True)
                p = jnp.exp(s - m)
                l = jnp.sum(p, axis=-1, keepdims=True)
                o_heads.append(jnp.dot((p / l).astype(BF16), vs, preferred_element_type=F32))
                lse_heads.append(m + jnp.log(l))
        o = jnp.where(lane < HEAD_DIM, o_heads[0], o_heads[1])
        lse = jnp.where(lane < HEAD_DIM, lse_heads[0], lse_heads[1])
        return o, lse

    for pidx, (window, d) in enumerate(DIL_PATTERNS):
        assert window // d == BLOCK
        unit = d * BLOCK
        n_problems = (span // unit) * d
        assert n_problems % DIL_GROUP == 0

        def body(g, _, pidx=pidx, d=d, unit=unit):
            starts, s_all = [], []
            for t in range(DIL_GROUP):
                idx = g * DIL_GROUP + t
                w = idx // d
                q_start = w * unit + (idx - w * d)
                k_start = span - unit + q_start
                starts.append((q_start, k_start))
                s_all.append(scores(q_start, k_start, d, jnp.logical_and(u == 0, w == 0)))
            for (q_start, k_start), s_heads in zip(starts, s_all):
                o, lse = finish(s_heads, k_start, d)
                osc[pidx, pl.ds(q_start, BLOCK, stride=d), :] = o
                lsc[pidx, pl.ds(q_start, BLOCK, stride=d), :] = lse
            return 0

        lax.fori_loop(0, n_problems // DIL_GROUP, body, 0)

    if bounded:
        num = osc[0] + osc[1] + osc[2]
        den = lsc[0] + lsc[1] + lsc[2]
    else:
        mx = jnp.maximum(jnp.maximum(lsc[0], lsc[1]), lsc[2])
        num = jnp.zeros((span, LANES), F32)
        den = jnp.zeros((span, LANES), F32)
        for pidx in range(len(DIL_PATTERNS)):
            e = jnp.exp(lsc[pidx] - mx)
            num = num + e * osc[pidx]
            den = den + e
    o_ref[0] = (num / den).astype(o_ref.dtype)


def _dilated(shift, qb, kb, vb, bounded):
    B, S, _ = qb.shape
    span = DIL_SPAN
    cur = pl.BlockSpec((1, span, LANES), lambda b, u, hp: (b, u, hp))
    prev = pl.BlockSpec((1, span, LANES), lambda b, u, hp: (b, jnp.maximum(u - 1, 0), hp))
    return pl.pallas_call(
        functools.partial(_dilated_kernel, bounded=bounded),
        grid=(B, S // span, N_PAIRS),
        in_specs=[pl.BlockSpec(memory_space=pltpu.SMEM), cur, prev, cur, prev, cur],
        out_specs=cur,
        out_shape=jax.ShapeDtypeStruct((B, S, D_GRP), F32),
        scratch_shapes=[pltpu.VMEM((span, LANES), F32),
                        pltpu.VMEM((2 * span, LANES), F32), pltpu.VMEM((2 * span, LANES), F32),
                        pltpu.VMEM((span, LANES), F32),
                        pltpu.VMEM((2 * span, LANES), F32), pltpu.VMEM((2 * span, LANES), F32),
                        pltpu.VMEM((len(DIL_PATTERNS), span, LANES), F32),
                        pltpu.VMEM((len(DIL_PATTERNS), span, LANES), F32)],
        compiler_params=_cparams(("arbitrary", "arbitrary", "arbitrary")),
        name="dilated" if bounded else "dilated_exact",
    )(shift, qb, kb, kb, vb, vb)


def _store_row_tiles(ref, stage_ref, x, first_row=0):
    n = x.shape[0]
    target = ref if stage_ref is None else stage_ref
    for c in range(ROW_CHUNKS):
        target[pl.ds(first_row * ROW_CHUNKS + c, n, stride=ROW_CHUNKS), :] = x[:, c * LANES:(c + 1) * LANES]
    if stage_ref is not None:
        rows = slice(first_row * ROW_CHUNKS, (first_row + n) * ROW_CHUNKS)
        ref[rows, :] = stage_ref[rows, :].astype(ref.dtype)


def _load_row_tiles(ref, stage_ref, n, first_row=0):
    source = ref
    if stage_ref is not None:
        rows = slice(first_row * ROW_CHUNKS, (first_row + n) * ROW_CHUNKS)
        stage_ref[rows, :] = ref[rows, :].astype(F32)
        source = stage_ref
    return jnp.concatenate([source[pl.ds(first_row * ROW_CHUNKS + c, n, stride=ROW_CHUNKS), :]
                            for c in range(ROW_CHUNKS)], axis=1)


def _row_tile_copy(src_ref, src_row, dst_ref, dst_row, sem):
    src = src_ref.at[pl.ds(pl.multiple_of(src_row * ROW_CHUNKS, ROW_CHUNKS), ROW_CHUNKS), :]
    dst = dst_ref.at[pl.ds(pl.multiple_of(dst_row * ROW_CHUNKS, ROW_CHUNKS), ROW_CHUNKS), :]
    return pltpu.make_async_copy(src, dst, sem)


def _outproj_kernel(oa_ref, ob_ref, x_ref, gfox_ref, gdil_ref, wo_ref, gffn_ref, wr_ref, br_ref, upper_ref,
                    h_ref, hn_ref, eid_ref, gate_ref, rank_ref, cnt_ref, run_ref, stage_ref):
    @pl.when(pl.program_id(0) == 0)
    def _():
        run_ref[...] = jnp.zeros_like(run_ref)

    def norm(y, g):
        ms = jnp.mean(y * y, axis=-1, keepdims=True)
        return y * lax.rsqrt(ms + EPS) * g

    a = norm(oa_ref[...], gfox_ref[...]).astype(BF16)
    b = norm(ob_ref[...], gdil_ref[...]).astype(BF16)
    mix = (jnp.dot(a, wo_ref[0:D_GRP, :], preferred_element_type=F32)
           + jnp.dot(b, wo_ref[D_GRP:2 * D_GRP, :], preferred_element_type=F32))
    h = x_ref[...] + mix
    h_ref[...] = h
    hn = norm(h, gffn_ref[...])
    _store_row_tiles(hn_ref, stage_ref, hn)

    hn_hi = hn.astype(BF16)
    hn_lo = (hn - hn_hi.astype(F32)).astype(BF16)
    nt = (((1,), (1,)), ((), ()))
    r = lax.dot_general(wr_ref[...], hn_hi, nt, preferred_element_type=F32)
    r_lo = lax.dot_general(wr_ref[:ROUTER_PAD, :], hn_lo, nt, preferred_element_type=F32)
    z = (r[:ROUTER_ROWS, :] + r[ROUTER_PAD:ROUTER_PAD + ROUTER_ROWS, :] + r_lo[:ROUTER_ROWS, :] + br_ref[...])
    tm = z.shape[1]
    best = z[0:1, :]
    g_sel = jnp.zeros((1, tm), I32)
    for g in range(1, N_GROUPS):
        better = z[g:g + 1, :] > best
        g_sel = jnp.where(better, g, g_sel)
        best = jnp.maximum(best, z[g:g + 1, :])
    den = jnp.zeros((1, tm), F32)
    for g in range(N_GROUPS):
        den = den + jnp.exp(z[g:g + 1, :] - best)
    pg_top = 1.0 / den

    ze = jnp.zeros((EXPERTS_PER_GROUP, tm), F32)
    for g in range(N_GROUPS):
        ze = jnp.where(g_sel == g, z[8 + g * EXPERTS_PER_GROUP:8 + (g + 1) * EXPERTS_PER_GROUP, :], ze)
    e_iota = lax.broadcasted_iota(I32, ze.shape, 0)
    v1 = jnp.max(ze, axis=0, keepdims=True)
    i1 = jnp.min(jnp.where(ze == v1, e_iota, EXPERTS_PER_GROUP), axis=0, keepdims=True)
    ze2 = jnp.where(e_iota == i1, -jnp.inf, ze)
    v2 = jnp.max(ze2, axis=0, keepdims=True)
    i2 = jnp.min(jnp.where(ze2 == v2, e_iota, EXPERTS_PER_GROUP), axis=0, keepdims=True)
    e2 = jnp.exp(v2 - v1)
    inv = 1.0 / (1.0 + e2)
    gate1 = inv * pg_top
    gate2 = e2 * inv * pg_top
    eid1 = g_sel * EXPERTS_PER_GROUP + i1
    eid2 = g_sel * EXPERTS_PER_GROUP + i2

    x_iota = lax.broadcasted_iota(I32, (N_EXPERTS, tm), 0)
    hot1 = x_iota == eid1
    hot2 = x_iota == eid2
    multi = jnp.logical_or(hot1, hot2)
    before = jnp.dot(multi.astype(BF16), upper_ref[...], preferred_element_type=F32)
    slot = before + run_ref[:, 0:1]
    rank1 = jnp.sum(jnp.where(hot1, slot, 0.0), axis=0, keepdims=True)
    rank2 = jnp.sum(jnp.where(hot2, slot, 0.0), axis=0, keepdims=True)
    run_ref[...] = run_ref[...] + jnp.sum(multi.astype(F32), axis=1, keepdims=True)

    eid_ref[...] = jnp.concatenate([eid1, eid2], axis=0)
    gate_ref[...] = jnp.concatenate([gate1, gate2], axis=0)
    rank_ref[...] = jnp.concatenate([rank1, rank2], axis=0).astype(I32)
    cnt_ref[...] = run_ref[...].astype(I32)


def _outproj(oa, ob, x2, gfox, gdil, w_out, gffn, w_r, b_r, upper):
    T, D = x2.shape
    tm = TM_OUT
    const = lambda shape: pl.BlockSpec(shape, lambda i: (0,) * len(shape))
    tok = lambda w: pl.BlockSpec((tm, w), lambda i: (i, 0))
    lanes2 = pl.BlockSpec((2, tm), lambda i: (0, i))
    return pl.pallas_call(
        _outproj_kernel,
        grid=(T // tm,),
        in_specs=[tok(D_GRP), tok(D_GRP), tok(D), const((1, D_GRP)), const((1, D_GRP)), const((D, D)),
                  const((1, D)), const((2 * ROUTER_PAD, D)), const((ROUTER_ROWS, 1)), const((tm, tm))],
        out_specs=[tok(D), pl.BlockSpec((tm * ROW_CHUNKS, LANES), lambda i: (i, 0)),
                   lanes2, lanes2, lanes2, const((N_EXPERTS, LANES))],
        out_shape=[jax.ShapeDtypeStruct((T, D), F32), jax.ShapeDtypeStruct((T * ROW_CHUNKS, LANES), BF16),
                   jax.ShapeDtypeStruct((2, T), I32), jax.ShapeDtypeStruct((2, T), F32),
                   jax.ShapeDtypeStruct((2, T), I32), jax.ShapeDtypeStruct((N_EXPERTS, LANES), I32)],
        scratch_shapes=[pltpu.VMEM((N_EXPERTS, LANES), F32), pltpu.VMEM((tm * ROW_CHUNKS, LANES), F32)],
        compiler_params=_cparams(("arbitrary",)),
        name="outproj",
    )(oa, ob, x2, gfox, gdil, w_out, gffn, w_r, b_r, upper)


def _scatter_kernel(starts_ref, cnt_ref, pos0_ref, pos1_ref, hn_ref, xs_ref, ring, sems, zero_sem, *, n_steps):
    i = pl.program_id(0)
    tm = hn_ref.shape[0] // ROW_CHUNKS
    slot = lax.rem(i, 2)

    def wait_slot(s):
        for _ in range(2):
            pltpu.make_async_copy(ring.at[s], xs_ref.at[pl.ds(0, tm * ROW_CHUNKS), :], sems.at[s]).wait()

    @pl.when(i >= 2)
    def _():
        wait_slot(slot)

    ring[slot] = hn_ref[...]

    def start(r, _):
        for k, pos_ref in enumerate((pos0_ref, pos1_ref)):
            _row_tile_copy(ring.at[slot], r, xs_ref, pos_ref[r], sems.at[slot]).start(priority=k)
        return 0

    lax.fori_loop(0, tm, start, 0, unroll=DMA_UNROLL)

    @pl.when(i == n_steps - 1)
    def _():
        wait_slot(slot)
        if n_steps > 1:
            wait_slot(1 - slot)
        ring[0] = jnp.zeros((tm * ROW_CHUNKS, LANES), ring.dtype)

        def pad_expert(e, _, wait):
            n_pad = lax.rem(TILE_E - lax.rem(cnt_ref[e], TILE_E), TILE_E)
            first = starts_ref[e] + cnt_ref[e]
            size = TILE_E // 2
            while size >= 1:
                row0 = first + (n_pad & ~(2 * size - 1))

                @pl.when((n_pad & size) != 0)
                def _(size=size, row0=row0):
                    copy = pltpu.make_async_copy(
                        ring.at[0, pl.ds(0, size * ROW_CHUNKS), :],
                        xs_ref.at[pl.ds(pl.multiple_of(row0 * ROW_CHUNKS, ROW_CHUNKS), size * ROW_CHUNKS), :],
                        zero_sem)
                    copy.wait() if wait else copy.start()

                size //= 2
            return 0

        lax.fori_loop(0, N_EXPERTS, functools.partial(pad_expert, wait=False), 0)
        last = N_EXPERTS - 1
        used_rows = starts_ref[last] + cnt_ref[last] + lax.rem(TILE_E - lax.rem(cnt_ref[last], TILE_E), TILE_E)
        n_tail = xs_ref.shape[0] // (tm * ROW_CHUNKS) - used_rows // tm

        def tail_copy(t):
            row0 = pl.multiple_of((used_rows + t * tm) * ROW_CHUNKS, tm * ROW_CHUNKS)
            return pltpu.make_async_copy(ring.at[0], xs_ref.at[pl.ds(row0, tm * ROW_CHUNKS), :], zero_sem)

        lax.fori_loop(0, n_tail, lambda t, c: (tail_copy(t).start(), c)[1], 0)
        lax.fori_loop(0, N_EXPERTS, functools.partial(pad_expert, wait=True), 0)
        lax.fori_loop(0, n_tail, lambda t, c: (tail_copy(t).wait(), c)[1], 0)


def _scatter(starts, cnt, pos, hn, n_rows):
    T = hn.shape[0] // ROW_CHUNKS
    tm = TM_ROWS
    assert TILE_E % tm == 0 and TILE_E // 2 <= tm
    n_steps = T // tm
    grid_spec = pltpu.PrefetchScalarGridSpec(
        num_scalar_prefetch=2,
        grid=(n_steps,),
        in_specs=[pl.BlockSpec((tm,), lambda i, starts, cnt: (i,), memory_space=pltpu.SMEM),
                  pl.BlockSpec((tm,), lambda i, starts, cnt: (i,), memory_space=pltpu.SMEM),
                  pl.BlockSpec((tm * ROW_CHUNKS, LANES), lambda i, starts, cnt: (i, 0))],
        out_specs=pl.BlockSpec(memory_space=pl.ANY),
        scratch_shapes=[pltpu.VMEM((2, tm * ROW_CHUNKS, LANES), hn.dtype),
                        pltpu.SemaphoreType.DMA((2,)), pltpu.SemaphoreType.DMA(())],
    )
    return pl.pallas_call(
        functools.partial(_scatter_kernel, n_steps=n_steps),
        grid_spec=grid_spec,
        out_shape=jax.ShapeDtypeStruct((n_rows * ROW_CHUNKS, LANES), hn.dtype),
        compiler_params=_cparams(("arbitrary",)),
        name="scatter_rows",
    )(starts, cnt, pos[0], pos[1], hn)


def _experts_kernel(te_ref, tb_ref, tr_ref, xs_ref, wg_ref, wu_ref, wd_ref, y_ref, wgu_bf, wd_bf, stage_ref):
    del tb_ref
    i = pl.program_id(0)
    rows = tr_ref[i]
    new_expert = jnp.logical_or(i == 0, te_ref[i] != te_ref[jnp.maximum(i - 1, 0)])

    @pl.when(jnp.logical_and(rows > 0, new_expert))
    def _():
        wgu_bf[:, :D_EXPERT] = wg_ref[0].astype(BF16)
        wgu_bf[:, D_EXPERT:] = wu_ref[0].astype(BF16)
        wd_bf[...] = wd_ref[0].astype(BF16)

    half = TILE_E // 2
    for part in range(2):
        @pl.when(rows > part * half)
        def _(part=part):
            x = _load_row_tiles(xs_ref, stage_ref, half, part * half).astype(BF16)
            gu = jnp.dot(x, wgu_bf[...], preferred_element_type=F32)
            g, up = gu[:, :D_EXPERT], gu[:, D_EXPERT:]
            hmid = (g * jax.nn.sigmoid(g) * up).astype(BF16)
            _store_row_tiles(y_ref, None, jnp.dot(hmid, wd_bf[...], preferred_element_type=F32), part * half)

        @pl.when(rows <= part * half)
        def _(part=part):
            y_ref[part * half * ROW_CHUNKS:(part + 1) * half * ROW_CHUNKS, :] = (
                jnp.zeros((half * ROW_CHUNKS, LANES), y_ref.dtype))


def _experts(tile_expert, tile_block, tile_rows, xs, w_gate, w_up, w_down):
    n_tiles = tile_expert.shape[0]
    D = D_MODEL
    rows_spec = pl.BlockSpec((TILE_E * ROW_CHUNKS, LANES), lambda i, te, tb, tr: (tb[i], 0))
    weights = lambda shape: pl.BlockSpec(shape, lambda i, te, tb, tr: (te[i], 0, 0))
    grid_spec = pltpu.PrefetchScalarGridSpec(
        num_scalar_prefetch=3,
        grid=(n_tiles,),
        in_specs=[rows_spec, weights((1, D, D_EXPERT)), weights((1, D, D_EXPERT)), weights((1, D_EXPERT, D))],
        out_specs=pl.BlockSpec((TILE_E * ROW_CHUNKS, LANES), lambda i, te, tb, tr: (i, 0)),
        scratch_shapes=[pltpu.VMEM((D, 2 * D_EXPERT), BF16),
                        pltpu.VMEM((D_EXPERT, D), BF16), pltpu.VMEM((TILE_E * ROW_CHUNKS, LANES), F32)],
    )
    return pl.pallas_call(
        _experts_kernel,
        grid_spec=grid_spec,
        out_shape=jax.ShapeDtypeStruct(xs.shape, F32),
        compiler_params=_cparams(("arbitrary",)),
        name="experts",
    )(tile_expert, tile_block, tile_rows, xs, w_gate, w_up, w_down)


def _combine_kernel(pos0_ref, pos1_ref, next0_ref, next1_ref, h_ref, gate_ref, y_ref, o_ref, ybuf, sems, *, n_steps):
    i = pl.program_id(0)
    tm = h_ref.shape[0]
    slot = lax.rem(i, 2)

    def gather(p_refs, s):
        def start(r, _):
            for k, p_ref in enumerate(p_refs):
                _row_tile_copy(y_ref, p_ref[r], ybuf.at[s, k], r, sems.at[s]).start(priority=k)
            return 0

        lax.fori_loop(0, tm, start, 0, unroll=DMA_UNROLL)

    @pl.when(i == 0)
    def _():
        gather((pos0_ref, pos1_ref), slot)

    @pl.when(i + 1 < n_steps)
    def _():
        gather((next0_ref, next1_ref), 1 - slot)

    for k in range(2):
        pltpu.make_async_copy(y_ref.at[pl.ds(0, tm * ROW_CHUNKS), :], ybuf.at[slot, k], sems.at[slot]).wait()
    g = gate_ref[...]
    o_ref[...] = (h_ref[...] + g[:, 0:1] * _load_row_tiles(ybuf.at[slot, 0], None, tm)
                  + g[:, 1:2] * _load_row_tiles(ybuf.at[slot, 1], None, tm))


def _combine(pos, h, gate_t, y):
    T, D = h.shape
    tm = TM_ROWS
    n_steps = T // tm
    return pl.pallas_call(
        functools.partial(_combine_kernel, n_steps=n_steps),
        grid=(n_steps,),
        in_specs=[pl.BlockSpec((tm,), lambda i: (i,), memory_space=pltpu.SMEM),
                  pl.BlockSpec((tm,), lambda i: (i,), memory_space=pltpu.SMEM),
                  pl.BlockSpec((tm,), lambda i: (jnp.minimum(i + 1, n_steps - 1),), memory_space=pltpu.SMEM),
                  pl.BlockSpec((tm,), lambda i: (jnp.minimum(i + 1, n_steps - 1),), memory_space=pltpu.SMEM),
                  pl.BlockSpec((tm, D), lambda i: (i, 0)),
                  pl.BlockSpec((tm, 2), lambda i: (i, 0)),
                  pl.BlockSpec(memory_space=pl.ANY)],
        out_specs=pl.BlockSpec((tm, D), lambda i: (i, 0)),
        out_shape=jax.ShapeDtypeStruct((T, D), F32),
        scratch_shapes=[pltpu.VMEM((2, 2, tm * ROW_CHUNKS, LANES), y.dtype), pltpu.SemaphoreType.DMA((2,))],
        compiler_params=_cparams(("arbitrary",)),
        name="combine",
    )(pos[0], pos[1], pos[0], pos[1], h, gate_t, y)


def _rope_tables(S):
    inv_freq = 1.0 / (ROPE_THETA ** (np.arange(0, HEAD_DIM, 2, dtype=np.float64) / HEAD_DIM))
    ang = np.arange(S, dtype=np.float64)[:, None] * inv_freq[None, :]
    widen = lambda t: jnp.asarray(np.pad(t, ((0, 0), (0, LANES - HEAD_DIM // 2))).astype(np.float32))
    return widen(np.cos(ang)), widen(np.sin(ang))


def _layer(x, norm_mix, w_in, b_forget, q_norm_fox, k_norm_fox, q_norm_dil, k_norm_dil,
           out_norm_fox, out_norm_dil, w_out, norm_ffn, w_router_group, b_router_group,
           w_router_expert, b_router_expert, w_gate, w_up, w_down):
    B, S, D = x.shape
    T = B * S
    n_main = 6 * D_GRP

    w_main = w_in.astype(BF16)
    w_vt = w_main[:, 2 * D_GRP:3 * D_GRP].T
    w_f = jnp.pad(w_main[:, n_main:], ((0, 0), (0, LANES - N_HEADS)))
    b_f = jnp.pad(b_forget, (0, LANES - N_HEADS))[None, :]
    per_head = lambda g: jnp.tile(g, N_HEADS)[None, :]
    bd = jnp.kron(jnp.eye(N_HEADS, dtype=F32), jnp.ones((HEAD_DIM, HEAD_DIM), F32)).astype(BF16)
    cos_t, sin_t = _rope_tables(S)
    tri = jnp.tril(jnp.ones((TRI_ROWS, TRI_ROWS), F32)).astype(BF16)
    upper = jnp.triu(jnp.ones((TM_OUT, TM_OUT), F32), k=1).astype(BF16)
    w_r = jnp.concatenate([
        jnp.pad(w_router_group.T, ((0, 8 - N_GROUPS), (0, 0))),
        w_router_expert.transpose(0, 2, 1).reshape(N_EXPERTS, D)], axis=0)
    w_r = jnp.pad(w_r, ((0, ROUTER_PAD - ROUTER_ROWS), (0, 0)))
    w_r_hi = w_r.astype(BF16)
    w_r = jnp.concatenate([w_r_hi, (w_r - w_r_hi.astype(F32)).astype(BF16)], axis=0)
    b_r = jnp.concatenate([jnp.pad(b_router_group, (0, 8 - N_GROUPS)), b_router_expert.reshape(-1)])[:, None]

    bound = (HEAD_DIM / math.sqrt(HEAD_DIM)) * LOG2E * jnp.max(jnp.abs(q_norm_fox)) * jnp.max(jnp.abs(k_norm_fox))
    shift = FOX_BOUND_SLACK * bound + 1.0
    qa, ka, va, qb, kb, vb, c2 = _inproj(
        x, norm_mix[None, :], w_main, w_vt, w_f, b_f, jnp.full((1, LANES), shift, F32),
        per_head(q_norm_fox), per_head(k_norm_fox),
        per_head(q_norm_dil), per_head(k_norm_dil), bd, cos_t, sin_t, tri)
    n_q, n_k = S // TQ, S // TK_SUB
    c_first = c2[:, ::TQ, :].reshape(B, n_q, 1, N_PAIRS, HEADS_PER_TILE)
    c_last = c2[:, TK_SUB - 1::TK_SUB, :].reshape(B, 1, n_k, N_PAIRS, HEADS_PER_TILE)
    all_zero = jnp.all(c_first - c_last < -FOX_ZERO_BITS, axis=-1)
    live_so_far = jnp.einsum("bqkp,kj->bqjp", 1 - all_zero.astype(I32),
                             jnp.triu(jnp.ones((n_k, n_k), I32)))
    leading = (live_so_far == 0).astype(I32)
    before = (jnp.arange(n_k) * TK_SUB + TK_SUB <= jnp.arange(n_q)[:, None] * TQ)
    first_tile = jnp.sum(leading * before[None, :, :, None].astype(I32), axis=2)
    first_tile = first_tile.transpose(0, 2, 1).reshape(-1).astype(I32)
    oa = lax.cond(2.0 * shift <= FOX_SAFE_SPAN,
                  lambda: _fox(first_tile, qa, ka, va, online=False),
                  lambda: _fox(jnp.zeros_like(first_tile), qa, ka, va, online=True))
    dil_shift = (FOX_BOUND_SLACK * (HEAD_DIM / math.sqrt(HEAD_DIM))
                 * jnp.max(jnp.abs(q_norm_dil)) * jnp.max(jnp.abs(k_norm_dil)) + 1.0)
    dil_shift_arr = jnp.full((1, 1), dil_shift, F32)
    ob = lax.cond(2.0 * dil_shift <= DIL_SAFE_SPAN,
                  lambda: _dilated(dil_shift_arr, qb, kb, vb, bounded=True),
                  lambda: _dilated(dil_shift_arr, qb, kb, vb, bounded=False))

    h, hn, eid, gate, rank, cnt = _outproj(
        oa.reshape(T, D_GRP), ob.reshape(T, D_GRP), x.reshape(T, D), out_norm_fox[None, :],
        out_norm_dil[None, :], w_out.astype(BF16), norm_ffn[None, :], w_r, b_r, upper)

    counts = cnt[:, 0]
    padded = ((counts + TILE_E - 1) // TILE_E) * TILE_E
    ends = jnp.cumsum(padded)
    starts = ends - padded
    is_expert = eid[:, :, None] == jnp.arange(N_EXPERTS, dtype=I32)
    pos = jnp.sum(jnp.where(is_expert, starts, 0), axis=-1) + rank
    n_tiles = (2 * T) // TILE_E + N_EXPERTS
    tile_index = jnp.arange(n_tiles, dtype=I32)
    tile_valid = tile_index * TILE_E < ends[-1]
    tile_block = jnp.minimum(tile_index, ends[-1] // TILE_E - 1)
    in_region = ends[None, :] <= (tile_block * TILE_E)[:, None]
    tile_expert = jnp.sum(in_region.astype(I32), axis=1)
    expert_of_tile = tile_expert[:, None] == jnp.arange(N_EXPERTS, dtype=I32)
    rows_before = tile_block * TILE_E - jnp.sum(jnp.where(expert_of_tile, starts, 0), axis=1)
    tile_rows = jnp.clip(jnp.sum(jnp.where(expert_of_tile, counts, 0), axis=1) - rows_before, 0, TILE_E)
    tile_rows = jnp.where(tile_valid, tile_rows, 0).astype(I32)

    xs = _scatter(starts, counts, pos, hn, n_tiles * TILE_E)
    y = _experts(tile_expert, tile_block, tile_rows, xs, w_gate, w_up, w_down)
    out = _combine(pos, h, gate.T, y)
    return out.reshape(B, S, D)


def kernel(x, norm_mix, w_in, b_forget, q_norm_fox, k_norm_fox, q_norm_dil, k_norm_dil, out_norm_fox,
           out_norm_dil, w_out, norm_ffn, w_router_group, b_router_group, w_router_expert,
           b_router_expert, w_gate, w_up, w_down):
    h = x
    for l in range(norm_mix.shape[0]):
        h = _layer(h, norm_mix[l], w_in[l], b_forget[l], q_norm_fox[l], k_norm_fox[l], q_norm_dil[l],
                   k_norm_dil[l], out_norm_fox[l], out_norm_dil[l], w_out[l], norm_ffn[l],
                   w_router_group[l], b_router_group[l], w_router_expert[l], b_router_expert[l],
                   w_gate[l], w_up[l], w_down[l])
    return h
```

```python
import functools
import math

import jax
import jax.numpy as jnp
import numpy as np
from jax import lax
from jax.experimental import pallas as pl
from jax.experimental.pallas import tpu as pltpu

F32 = jnp.float32
BF16 = jnp.bfloat16
I32 = jnp.int32

D_MODEL = 1024
HEAD_DIM = 64
N_HEADS = 8
D_GRP = N_HEADS * HEAD_DIM
LANES = 128
HEADS_PER_TILE = LANES // HEAD_DIM
N_PAIRS = D_GRP // LANES
DIL_PATTERNS = ((128, 1), (512, 4), (2048, 16))
BLOCK = 128
ROPE_THETA = 10000.0
N_GROUPS = 4
EXPERTS_PER_GROUP = 8
N_EXPERTS = N_GROUPS * EXPERTS_PER_GROUP
D_EXPERT = 512
EPS = 1e-6
NEG = -1e30
LOG2E = 1.4426950408889634

TM_IN = 1024
TRI_ROWS = 256
TQ = 1024
TK_WIDE = 1024
TK_SUB = 256
FOX_AHEAD = 4
FOX_ZERO_BITS = 160.0
FOX_BOUND_SLACK = 1.02
FOX_SAFE_SPAN = 100.0
DIL_SAFE_SPAN = FOX_SAFE_SPAN * math.log(2.0)
DIL_SPAN = 2048
DIL_GROUP = 16
DIL_PRE = 4
TM_OUT = 1024
TM_ROWS = 512
TILE_E = 512
ROUTER_ROWS = 8 + N_EXPERTS
ROUTER_PAD = 48
ROW_CHUNKS = D_MODEL // LANES
DMA_UNROLL = 8
VMEM_LIMIT = 56 * 1024 * 1024


def _cparams(sem):
    return pltpu.CompilerParams(dimension_semantics=sem, vmem_limit_bytes=VMEM_LIMIT)


def _inproj_kernel(x_ref, gmix_ref, w_ref, wvt_ref, wf_ref, bf_ref, shift_ref, gqa_ref, gka_ref, gqb_ref, gkb_ref,
                   bd_ref, cos_ref, sin_ref, tri_ref,
                   qa_ref, ka_ref, va_ref, qb_ref, kb_ref, vb_ref, c_ref, carry_ref):
    @pl.when(pl.program_id(1) == 0)
    def _():
        carry_ref[...] = jnp.zeros_like(carry_ref)

    x = x_ref[0]
    ms = jnp.mean(x * x, axis=-1, keepdims=True)
    xn = (x * lax.rsqrt(ms + EPS) * gmix_ref[...]).astype(BF16)

    def seg(j):
        return jnp.dot(xn, w_ref[:, j * D_GRP:(j + 1) * D_GRP], preferred_element_type=F32)

    def head_norm(y, g_ref, scale):
        ss = jnp.dot((y * y).astype(BF16), bd_ref[...], preferred_element_type=F32) * (1.0 / HEAD_DIM)
        return y * lax.rsqrt(ss + EPS) * (g_ref[...] * scale)

    lane = lax.broadcasted_iota(I32, (x.shape[0], LANES), 1)
    first_half = (lane % HEAD_DIM) < (HEAD_DIM // 2)

    def spread(t):
        t = t + pltpu.roll(t, HEAD_DIM // 2, 1)
        return t + pltpu.roll(t, HEAD_DIM, 1)

    cos = spread(cos_ref[...])
    sin = jnp.where(first_half, -1.0, 1.0) * spread(sin_ref[...])

    def rope(y):
        outs = []
        for j in range(N_PAIRS):
            ys = y[:, j * LANES:(j + 1) * LANES]
            partner = jnp.where(first_half, pltpu.roll(ys, LANES - HEAD_DIM // 2, 1),
                                pltpu.roll(ys, HEAD_DIM // 2, 1))
            outs.append(ys * cos + partner * sin)
        return jnp.concatenate(outs, axis=1)

    scale = 1.0 / math.sqrt(HEAD_DIM)
    va_ref[0] = lax.dot_general(wvt_ref[...], xn, (((1,), (1,)), ((), ())),
                                preferred_element_type=F32).astype(va_ref.dtype)
    qb_ref[0] = rope(head_norm(seg(3), gqb_ref, scale)).astype(qb_ref.dtype)
    kb_ref[0] = rope(head_norm(seg(4), gkb_ref, 1.0)).astype(kb_ref.dtype)
    vb_ref[0] = seg(5).astype(vb_ref.dtype)

    fa = jnp.dot(xn, wf_ref[...], preferred_element_type=F32) + bf_ref[...]
    logf = jnp.minimum(fa, 0.0) - jnp.log1p(jnp.exp(-jnp.abs(fa)))
    hi = logf.astype(BF16)
    mid = (logf - hi.astype(F32)).astype(BF16)
    lo = (logf - hi.astype(F32) - mid.astype(F32)).astype(BF16)
    pieces = jnp.concatenate([hi, mid, lo], axis=1)
    carry = carry_ref[...]
    blocks = []
    for j in range(x.shape[0] // TRI_ROWS):
        parts = jnp.dot(tri_ref[...], pieces[j * TRI_ROWS:(j + 1) * TRI_ROWS, :], preferred_element_type=F32)
        blk = parts[:, :LANES] + parts[:, LANES:2 * LANES] + parts[:, 2 * LANES:] + carry
        carry = blk[TRI_ROWS - 1:, :]
        blocks.append(blk)
    c = jnp.concatenate(blocks, axis=0)
    carry_ref[...] = carry

    qa = head_norm(seg(0), gqa_ref, scale * LOG2E)
    ka = head_norm(seg(1), gka_ref, 1.0)
    c2 = c * LOG2E
    c_ref[0] = c2[:, :N_HEADS]
    ones = (jnp.where((lane >= HEAD_DIM + 3) & (lane < HEAD_DIM + 6), 1.0, 0.0)
            - jnp.where(lane == HEAD_DIM + 6, shift_ref[...], 0.0))
    for h in range(N_HEADS):
        cb = jnp.broadcast_to(c2[:, h:h + 1], (x.shape[0], LANES))
        hi = cb.astype(BF16).astype(F32)
        mid = (cb - hi).astype(BF16).astype(F32)
        lo = cb - hi - mid
        pieces = jnp.where(lane == HEAD_DIM, hi, jnp.where(lane == HEAD_DIM + 1, mid,
                           jnp.where(lane == HEAD_DIM + 2, lo, 0.0)))
        q_extra = pieces + ones
        k_extra = jnp.where(((lane >= HEAD_DIM) & (lane < HEAD_DIM + 3)) | (lane == HEAD_DIM + 6), 1.0, 0.0) \
            - pltpu.roll(pieces, 3, 1)
        j, odd = divmod(h, HEADS_PER_TILE)
        qp = qa[:, j * LANES:(j + 1) * LANES]
        kp = ka[:, j * LANES:(j + 1) * LANES]
        if odd:
            qp = pltpu.roll(qp, HEAD_DIM, 1)
            kp = pltpu.roll(kp, HEAD_DIM, 1)
        qa_ref[0, h] = jnp.where(lane < HEAD_DIM, qp, q_extra).astype(qa_ref.dtype)
        ka_ref[0, h] = jnp.where(lane < HEAD_DIM, kp, k_extra).astype(ka_ref.dtype)


def _inproj(x, gmix, w_main, w_vt, w_f, b_f, shift, gqa, gka, gqb, gkb, bd, cos_t, sin_t, tri):
    B, S, D = x.shape
    tm = TM_IN
    const = lambda shape: pl.BlockSpec(shape, lambda b, i: (0,) * len(shape))
    tok = lambda w, dt: jax.ShapeDtypeStruct((B, S, w), dt)
    tok_spec = lambda w: pl.BlockSpec((1, tm, w), lambda b, i: (b, i, 0))
    head_spec = pl.BlockSpec((1, N_HEADS, tm, LANES), lambda b, i: (b, 0, i, 0))
    head_shape = jax.ShapeDtypeStruct((B, N_HEADS, S, LANES), BF16)
    return pl.pallas_call(
        _inproj_kernel,
        grid=(B, S // tm),
        in_specs=[tok_spec(D), const((1, D)), const(w_main.shape), const(w_vt.shape), const(w_f.shape),
                  const((1, LANES)), const((1, LANES)),
                  const((1, D_GRP)), const((1, D_GRP)), const((1, D_GRP)), const((1, D_GRP)),
                  const((D_GRP, D_GRP)),
                  pl.BlockSpec((tm, LANES), lambda b, i: (i, 0)),
                  pl.BlockSpec((tm, LANES), lambda b, i: (i, 0)),
                  const(tri.shape)],
        out_specs=[head_spec, head_spec, pl.BlockSpec((1, D_GRP, tm), lambda b, i: (b, 0, i))]
        + [tok_spec(D_GRP)] * 3 + [tok_spec(N_HEADS)],
        out_shape=[head_shape, head_shape, jax.ShapeDtypeStruct((B, D_GRP, S), BF16),
                   tok(D_GRP, F32), tok(D_GRP, F32), tok(D_GRP, F32), tok(N_HEADS, F32)],
        scratch_shapes=[pltpu.VMEM((1, LANES), F32)],
        compiler_params=_cparams(("arbitrary", "arbitrary")),
        name="inproj",
    )(x, gmix, w_main, w_vt, w_f, b_f, shift, gqa, gka, gqb, gkb, bd, cos_t, sin_t, tri)


def _fox_kernel(first_ref, q_ref, k_ref, v_ref, o_ref, *, online):
    qi = pl.program_id(2)
    tq = q_ref.shape[2]

    def step(start, width, carry, diag):
        carry = list(carry)
        sub = min(TK_SUB, width)
        chunks = [(c, j) for c in range(width // sub) for j in range(HEADS_PER_TILE)]
        def first_query(c):
            return c * sub if diag else 0

        def score(c, j):
            k = k_ref[0, j, pl.ds(start + c * sub, sub), :]
            q = q_ref[0, j, first_query(c):, :]
            return lax.dot_general(k, q, (((1,), (1,)), ((), ())), preferred_element_type=F32)

        def tail(full, lo, new):
            return new if lo == 0 else jnp.concatenate([full[:, :lo], new], axis=1)

        scores = {cj: score(*cj) for cj in chunks[:FOX_AHEAD]}
        for n, (c, j) in enumerate(chunks):
                if n + FOX_AHEAD < len(chunks):
                    nxt = chunks[n + FOX_AHEAD]
                    scores[nxt] = score(*nxt)
                m, l, acc = carry[j]
                lo = first_query(c)
                vt = v_ref[0, :, pl.ds(start + c * sub, sub)]
                s = scores.pop((c, j))
                if diag:
                    key = lax.broadcasted_iota(I32, s.shape, 0)
                    qry = lax.broadcasted_iota(I32, s.shape, 1)
                    s = jnp.where(key <= qry, s, NEG)
                if online:
                    m_new = jnp.maximum(m[:, lo:], jnp.max(s, axis=0, keepdims=True))
                    alpha = jnp.exp2(m[:, lo:] - m_new)
                    p = jnp.exp2(s - m_new)
                    l_new = alpha * l[:, lo:] + jnp.sum(p, axis=0, keepdims=True)
                    acc_new = alpha * acc[:, lo:] + jnp.dot(vt, p.astype(BF16), preferred_element_type=F32)
                    m = tail(m, lo, m_new)
                else:
                    p = jnp.exp2(s)
                    l_new = l[:, lo:] + jnp.sum(p, axis=0, keepdims=True)
                    acc_new = acc[:, lo:] + jnp.dot(vt, p.astype(BF16), preferred_element_type=F32)
                carry[j] = (m, tail(l, lo, l_new), tail(acc, lo, acc_new))
        return tuple(carry)

    init = tuple((jnp.full((1, tq), NEG, F32), jnp.zeros((1, tq), F32), jnp.zeros((LANES, tq), F32))
                 for _ in range(HEADS_PER_TILE))
    assert TK_WIDE == tq and TK_WIDE % TK_SUB == 0
    per_wide = TK_WIDE // TK_SUB
    first = first_ref[(pl.program_id(0) * pl.num_programs(1) + pl.program_id(1)) * pl.num_programs(2) + qi]
    live = qi * per_wide - first
    n_narrow = lax.rem(live, per_wide)
    narrow_start = pl.multiple_of(first * TK_SUB, TK_SUB)
    carry = lax.switch(
        n_narrow,
        [lambda c: c] + [lambda c, w=w: step(narrow_start, w * TK_SUB, c, False) for w in range(1, per_wide)],
        init)
    wide_start = (first + n_narrow) * TK_SUB
    carry = lax.fori_loop(
        0, lax.div(live, per_wide),
        lambda i, c: step(pl.multiple_of(wide_start + i * TK_WIDE, TK_SUB), TK_WIDE, c, False), carry)
    carry = step(pl.multiple_of(qi * tq, tq), tq, carry, True)
    outs = [acc / l for (_, l, acc) in carry]
    feat = lax.broadcasted_iota(I32, (LANES, tq), 0)
    o_ref[0] = jnp.where(feat < HEAD_DIM, outs[0], outs[1]).T.astype(o_ref.dtype)


def _fox(first_tile, qa, ka, va_t, online):
    B, _, S, _ = qa.shape
    grid_spec = pltpu.PrefetchScalarGridSpec(
        num_scalar_prefetch=1,
        grid=(B, N_PAIRS, S // TQ),
        in_specs=[pl.BlockSpec((1, HEADS_PER_TILE, TQ, LANES), lambda b, hp, i, first: (b, hp, i, 0)),
                  pl.BlockSpec((1, HEADS_PER_TILE, S, LANES), lambda b, hp, i, first: (b, hp, 0, 0)),
                  pl.BlockSpec((1, LANES, S), lambda b, hp, i, first: (b, hp, 0))],
        out_specs=pl.BlockSpec((1, TQ, LANES), lambda b, hp, i, first: (b, i, hp)),
    )
    return pl.pallas_call(
        functools.partial(_fox_kernel, online=online),
        grid_spec=grid_spec,
        out_shape=jax.ShapeDtypeStruct((B, S, D_GRP), F32),
        compiler_params=_cparams(("arbitrary", "arbitrary", "arbitrary")),
        name="fox_online" if online else "fox",
    )(first_tile, qa, ka, va_t)


def _dilated_kernel(shift_ref, q_ref, kp_ref, kc_ref, vp_ref, vc_ref, o_ref, qq, kk, vv, qq4, kk4, vv4, osc, lsc,
                    *, bounded):
    u = pl.program_id(1)
    span = q_ref.shape[1]
    qq[...] = q_ref[0]
    kk[0:span, :] = kp_ref[0]
    kk[span:2 * span, :] = kc_ref[0]
    vv[0:span, :] = vp_ref[0]
    vv[span:2 * span, :] = vc_ref[0]
    for src, dst in ((qq, qq4), (kk, kk4), (vv, vv4)):
        part = src.shape[0] // DIL_PRE
        for a in range(DIL_PRE):
            dst[a * part:(a + 1) * part, :] = src[pl.ds(a, part, stride=DIL_PRE), :]

    def rows(buf, buf4, start, n, d):
        if d % DIL_PRE:
            return buf[pl.ds(start, n, stride=d), :]
        part = buf4.shape[0] // DIL_PRE
        a = lax.rem(start, DIL_PRE)
        return buf4[pl.ds(a * part + lax.div(start, DIL_PRE), n, stride=d // DIL_PRE), :]

    lane = lax.broadcasted_iota(I32, (BLOCK, LANES), 1)
    ql = lax.broadcasted_iota(I32, (BLOCK, 2 * BLOCK), 0)
    kl = lax.broadcasted_iota(I32, (BLOCK, 2 * BLOCK), 1)
    dist = ql + BLOCK - kl
    band = (dist >= 0) & (dist <= BLOCK)
    live = -shift_ref[0, 0] if bounded else 0.0
    bias = jnp.where(band, live, NEG)
    bias_first = jnp.where(band & (kl >= BLOCK), live, NEG)

    def scores(q_start, k_start, d, first):
        qs = rows(qq, qq4, q_start, BLOCK, d).astype(BF16)
        ks = rows(kk, kk4, k_start, 2 * BLOCK, d).astype(BF16)
        mask = jnp.where(first, bias_first, bias)
        out = []
        for j in range(HEADS_PER_TILE):
            qj = jnp.where(lane // HEAD_DIM == j, qs, jnp.zeros_like(qs))
            out.append(lax.dot_general(qj, ks, (((1,), (1,)), ((), ())), preferred_element_type=F32) + mask)
        return out

    def finish(s_heads, k_start, d):
        vs = rows(vv, vv4, k_start, 2 * BLOCK, d).astype(BF16)
        o_heads, lse_heads = [], []
        for s in s_heads:
            if bounded:
                p = jnp.exp(s)
                o_heads.append(jnp.dot(p.astype(BF16), vs, preferred_element_type=F32))
                lse_heads.append(jnp.sum(p, axis=-1, keepdims=True))
            else:
                m = jnp.max(s, axis=-1, keepdims=True)
                p = jnp.exp(s - m)
                l = jnp.sum(p, axis=-1, keepdims=True)
                o_heads.append(jnp.dot((p / l).astype(BF16), vs, preferred_element_type=F32))
                lse_heads.append(m + jnp.log(l))
        o = jnp.where(lane < HEAD_DIM, o_heads[0], o_heads[1])
        lse = jnp.where(lane < HEAD_DIM, lse_heads[0], lse_heads[1])
        return o, lse

    for pidx, (window, d) in enumerate(DIL_PATTERNS):
        assert window // d == BLOCK
        unit = d * BLOCK
        n_problems = (span // unit) * d
        assert n_problems % DIL_GROUP == 0

        def body(g, _, pidx=pidx, d=d, unit=unit):
            starts, s_all = [], []
            for t in range(DIL_GROUP):
                idx = g * DIL_GROUP + t
                w = idx // d
                q_start = w * unit + (idx - w * d)
                k_start = span - unit + q_start
                starts.append((q_start, k_start))
                s_all.append(scores(q_start, k_start, d, jnp.logical_and(u == 0, w == 0)))
            for (q_start, k_start), s_heads in zip(starts, s_all):
                o, lse = finish(s_heads, k_start, d)
                osc[pidx, pl.ds(q_start, BLOCK, stride=d), :] = o
                lsc[pidx, pl.ds(q_start, BLOCK, stride=d), :] = lse
            return 0

        lax.fori_loop(0, n_problems // DIL_GROUP, body, 0)

    if bounded:
        num = osc[0] + osc[1] + osc[2]
        den = lsc[0] + lsc[1] + lsc[2]
    else:
        mx = jnp.maximum(jnp.maximum(lsc[0], lsc[1]), lsc[2])
        num = jnp.zeros((span, LANES), F32)
        den = jnp.zeros((span, LANES), F32)
        for pidx in range(len(DIL_PATTERNS)):
            e = jnp.exp(lsc[pidx] - mx)
            num = num + e * osc[pidx]
            den = den + e
    o_ref[0] = (num / den).astype(o_ref.dtype)


def _dilated(shift, qb, kb, vb, bounded):
    B, S, _ = qb.shape
    span = DIL_SPAN
    cur = pl.BlockSpec((1, span, LANES), lambda b, u, hp: (b, u, hp))
    prev = pl.BlockSpec((1, span, LANES), lambda b, u, hp: (b, jnp.maximum(u - 1, 0), hp))
    return pl.pallas_call(
        functools.partial(_dilated_kernel, bounded=bounded),
        grid=(B, S // span, N_PAIRS),
        in_specs=[pl.BlockSpec(memory_space=pltpu.SMEM), cur, prev, cur, prev, cur],
        out_specs=cur,
        out_shape=jax.ShapeDtypeStruct((B, S, D_GRP), F32),
        scratch_shapes=[pltpu.VMEM((span, LANES), F32),
                        pltpu.VMEM((2 * span, LANES), F32), pltpu.VMEM((2 * span, LANES), F32),
                        pltpu.VMEM((span, LANES), F32),
                        pltpu.VMEM((2 * span, LANES), F32), pltpu.VMEM((2 * span, LANES), F32),
                        pltpu.VMEM((len(DIL_PATTERNS), span, LANES), F32),
                        pltpu.VMEM((len(DIL_PATTERNS), span, LANES), F32)],
        compiler_params=_cparams(("arbitrary", "arbitrary", "arbitrary")),
        name="dilated" if bounded else "dilated_exact",
    )(shift, qb, kb, kb, vb, vb)


def _store_row_tiles(ref, stage_ref, x, first_row=0):
    n = x.shape[0]
    target = ref if stage_ref is None else stage_ref
    for c in range(ROW_CHUNKS):
        target[pl.ds(first_row * ROW_CHUNKS + c, n, stride=ROW_CHUNKS), :] = x[:, c * LANES:(c + 1) * LANES]
    if stage_ref is not None:
        rows = slice(first_row * ROW_CHUNKS, (first_row + n) * ROW_CHUNKS)
        ref[rows, :] = stage_ref[rows, :].astype(ref.dtype)


def _load_row_tiles(ref, stage_ref, n, first_row=0):
    source = ref
    if stage_ref is not None:
        rows = slice(first_row * ROW_CHUNKS, (first_row + n) * ROW_CHUNKS)
        stage_ref[rows, :] = ref[rows, :].astype(F32)
        source = stage_ref
    return jnp.concatenate([source[pl.ds(first_row * ROW_CHUNKS + c, n, stride=ROW_CHUNKS), :]
                            for c in range(ROW_CHUNKS)], axis=1)


def _row_tile_copy(src_ref, src_row, dst_ref, dst_row, sem):
    src = src_ref.at[pl.ds(pl.multiple_of(src_row * ROW_CHUNKS, ROW_CHUNKS), ROW_CHUNKS), :]
    dst = dst_ref.at[pl.ds(pl.multiple_of(dst_row * ROW_CHUNKS, ROW_CHUNKS), ROW_CHUNKS), :]
    return pltpu.make_async_copy(src, dst, sem)


def _outproj_kernel(oa_ref, ob_ref, x_ref, gfox_ref, gdil_ref, wo_ref, gffn_ref, wr_ref, br_ref, upper_ref,
                    h_ref, hn_ref, eid_ref, gate_ref, rank_ref, cnt_ref, run_ref, stage_ref):
    @pl.when(pl.program_id(0) == 0)
    def _():
        run_ref[...] = jnp.zeros_like(run_ref)

    def norm(y, g):
        ms = jnp.mean(y * y, axis=-1, keepdims=True)
        return y * lax.rsqrt(ms + EPS) * g

    a = norm(oa_ref[...], gfox_ref[...]).astype(BF16)
    b = norm(ob_ref[...], gdil_ref[...]).astype(BF16)
    mix = (jnp.dot(a, wo_ref[0:D_GRP, :], preferred_element_type=F32)
           + jnp.dot(b, wo_ref[D_GRP:2 * D_GRP, :], preferred_element_type=F32))
    h = x_ref[...] + mix
    h_ref[...] = h
    hn = norm(h, gffn_ref[...])
    _store_row_tiles(hn_ref, stage_ref, hn)

    hn_hi = hn.astype(BF16)
    hn_lo = (hn - hn_hi.astype(F32)).astype(BF16)
    nt = (((1,), (1,)), ((), ()))
    r = lax.dot_general(wr_ref[...], hn_hi, nt, preferred_element_type=F32)
    r_lo = lax.dot_general(wr_ref[:ROUTER_PAD, :], hn_lo, nt, preferred_element_type=F32)
    z = (r[:ROUTER_ROWS, :] + r[ROUTER_PAD:ROUTER_PAD + ROUTER_ROWS, :] + r_lo[:ROUTER_ROWS, :] + br_ref[...])
    tm = z.shape[1]
    best = z[0:1, :]
    g_sel = jnp.zeros((1, tm), I32)
    for g in range(1, N_GROUPS):
        better = z[g:g + 1, :] > best
        g_sel = jnp.where(better, g, g_sel)
        best = jnp.maximum(best, z[g:g + 1, :])
    den = jnp.zeros((1, tm), F32)
    for g in range(N_GROUPS):
        den = den + jnp.exp(z[g:g + 1, :] - best)
    pg_top = 1.0 / den

    ze = jnp.zeros((EXPERTS_PER_GROUP, tm), F32)
    for g in range(N_GROUPS):
        ze = jnp.where(g_sel == g, z[8 + g * EXPERTS_PER_GROUP:8 + (g + 1) * EXPERTS_PER_GROUP, :], ze)
    e_iota = lax.broadcasted_iota(I32, ze.shape, 0)
    v1 = jnp.max(ze, axis=0, keepdims=True)
    i1 = jnp.min(jnp.where(ze == v1, e_iota, EXPERTS_PER_GROUP), axis=0, keepdims=True)
    ze2 = jnp.where(e_iota == i1, -jnp.inf, ze)
    v2 = jnp.max(ze2, axis=0, keepdims=True)
    i2 = jnp.min(jnp.where(ze2 == v2, e_iota, EXPERTS_PER_GROUP), axis=0, keepdims=True)
    e2 = jnp.exp(v2 - v1)
    inv = 1.0 / (1.0 + e2)
    gate1 = inv * pg_top
    gate2 = e2 * inv * pg_top
    eid1 = g_sel * EXPERTS_PER_GROUP + i1
    eid2 = g_sel * EXPERTS_PER_GROUP + i2

    x_iota = lax.broadcasted_iota(I32, (N_EXPERTS, tm), 0)
    hot1 = x_iota == eid1
    hot2 = x_iota == eid2
    multi = jnp.logical_or(hot1, hot2)
    before = jnp.dot(multi.astype(BF16), upper_ref[...], preferred_element_type=F32)
    slot = before + run_ref[:, 0:1]
    rank1 = jnp.sum(jnp.where(hot1, slot, 0.0), axis=0, keepdims=True)
    rank2 = jnp.sum(jnp.where(hot2, slot, 0.0), axis=0, keepdims=True)
    run_ref[...] = run_ref[...] + jnp.sum(multi.astype(F32), axis=1, keepdims=True)

    eid_ref[...] = jnp.concatenate([eid1, eid2], axis=0)
    gate_ref[...] = jnp.concatenate([gate1, gate2], axis=0)
    rank_ref[...] = jnp.concatenate([rank1, rank2], axis=0).astype(I32)
    cnt_ref[...] = run_ref[...].astype(I32)


def _outproj(oa, ob, x2, gfox, gdil, w_out, gffn, w_r, b_r, upper):
    T, D = x2.shape
    tm = TM_OUT
    const = lambda shape: pl.BlockSpec(shape, lambda i: (0,) * len(shape))
    tok = lambda w: pl.BlockSpec((tm, w), lambda i: (i, 0))
    lanes2 = pl.BlockSpec((2, tm), lambda i: (0, i))
    return pl.pallas_call(
        _outproj_kernel,
        grid=(T // tm,),
        in_specs=[tok(D_GRP), tok(D_GRP), tok(D), const((1, D_GRP)), const((1, D_GRP)), const((D, D)),
                  const((1, D)), const((2 * ROUTER_PAD, D)), const((ROUTER_ROWS, 1)), const((tm, tm))],
        out_specs=[tok(D), pl.BlockSpec((tm * ROW_CHUNKS, LANES), lambda i: (i, 0)),
                   lanes2, lanes2, lanes2, const((N_EXPERTS, LANES))],
        out_shape=[jax.ShapeDtypeStruct((T, D), F32), jax.ShapeDtypeStruct((T * ROW_CHUNKS, LANES), BF16),
                   jax.ShapeDtypeStruct((2, T), I32), jax.ShapeDtypeStruct((2, T), F32),
                   jax.ShapeDtypeStruct((2, T), I32), jax.ShapeDtypeStruct((N_EXPERTS, LANES), I32)],
        scratch_shapes=[pltpu.VMEM((N_EXPERTS, LANES), F32), pltpu.VMEM((tm * ROW_CHUNKS, LANES), F32)],
        compiler_params=_cparams(("arbitrary",)),
        name="outproj",
    )(oa, ob, x2, gfox, gdil, w_out, gffn, w_r, b_r, upper)


def _scatter_kernel(starts_ref, cnt_ref, pos0_ref, pos1_ref, hn_ref, xs_ref, ring, sems, zero_sem, *, n_steps):
    i = pl.program_id(0)
    tm = hn_ref.shape[0] // ROW_CHUNKS
    slot = lax.rem(i, 2)

    def wait_slot(s):
        for _ in range(2):
            pltpu.make_async_copy(ring.at[s], xs_ref.at[pl.ds(0, tm * ROW_CHUNKS), :], sems.at[s]).wait()

    @pl.when(i >= 2)
    def _():
        wait_slot(slot)

    ring[slot] = hn_ref[...]

    def start(r, _):
        for k, pos_ref in enumerate((pos0_ref, pos1_ref)):
            _row_tile_copy(ring.at[slot], r, xs_ref, pos_ref[r], sems.at[slot]).start(priority=k)
        return 0

    lax.fori_loop(0, tm, start, 0, unroll=DMA_UNROLL)

    @pl.when(i == n_steps - 1)
    def _():
        wait_slot(slot)
        if n_steps > 1:
            wait_slot(1 - slot)
        ring[0] = jnp.zeros((tm * ROW_CHUNKS, LANES), ring.dtype)

        def pad_expert(e, _, wait):
            n_pad = lax.rem(TILE_E - lax.rem(cnt_ref[e], TILE_E), TILE_E)
            first = starts_ref[e] + cnt_ref[e]
            size = TILE_E // 2
            while size >= 1:
                row0 = first + (n_pad & ~(2 * size - 1))

                @pl.when((n_pad & size) != 0)
                def _(size=size, row0=row0):
                    copy = pltpu.make_async_copy(
                        ring.at[0, pl.ds(0, size * ROW_CHUNKS), :],
                        xs_ref.at[pl.ds(pl.multiple_of(row0 * ROW_CHUNKS, ROW_CHUNKS), size * ROW_CHUNKS), :],
                        zero_sem)
                    copy.wait() if wait else copy.start()

                size //= 2
            return 0

        lax.fori_loop(0, N_EXPERTS, functools.partial(pad_expert, wait=False), 0)
        last = N_EXPERTS - 1
        used_rows = starts_ref[last] + cnt_ref[last] + lax.rem(TILE_E - lax.rem(cnt_ref[last], TILE_E), TILE_E)
        n_tail = xs_ref.shape[0] // (tm * ROW_CHUNKS) - used_rows // tm

        def tail_copy(t):
            row0 = pl.multiple_of((used_rows + t * tm) * ROW_CHUNKS, tm * ROW_CHUNKS)
            return pltpu.make_async_copy(ring.at[0], xs_ref.at[pl.ds(row0, tm * ROW_CHUNKS), :], zero_sem)

        lax.fori_loop(0, n_tail, lambda t, c: (tail_copy(t).start(), c)[1], 0)
        lax.fori_loop(0, N_EXPERTS, functools.partial(pad_expert, wait=True), 0)
        lax.fori_loop(0, n_tail, lambda t, c: (tail_copy(t).wait(), c)[1], 0)


def _scatter(starts, cnt, pos, hn, n_rows):
    T = hn.shape[0] // ROW_CHUNKS
    tm = TM_ROWS
    assert TILE_E % tm == 0 and TILE_E // 2 <= tm
    n_steps = T // tm
    grid_spec = pltpu.PrefetchScalarGridSpec(
        num_scalar_prefetch=2,
        grid=(n_steps,),
        in_specs=[pl.BlockSpec((tm,), lambda i, starts, cnt: (i,), memory_space=pltpu.SMEM),
                  pl.BlockSpec((tm,), lambda i, starts, cnt: (i,), memory_space=pltpu.SMEM),
                  pl.BlockSpec((tm * ROW_CHUNKS, LANES), lambda i, starts, cnt: (i, 0))],
        out_specs=pl.BlockSpec(memory_space=pl.ANY),
        scratch_shapes=[pltpu.VMEM((2, tm * ROW_CHUNKS, LANES), hn.dtype),
                        pltpu.SemaphoreType.DMA((2,)), pltpu.SemaphoreType.DMA(())],
    )
    return pl.pallas_call(
        functools.partial(_scatter_kernel, n_steps=n_steps),
        grid_spec=grid_spec,
        out_shape=jax.ShapeDtypeStruct((n_rows * ROW_CHUNKS, LANES), hn.dtype),
        compiler_params=_cparams(("arbitrary",)),
        name="scatter_rows",
    )(starts, cnt, pos[0], pos[1], hn)


def _experts_kernel(te_ref, tb_ref, tr_ref, nxt_ref, xs_ref, wg_hbm, wu_hbm, wd_hbm, y_ref,
                    wgu_bf, wd_bf, stage_ref, wg_f, wu_f, wd_f, wsem):
    del tb_ref
    i = pl.program_id(0)
    rows = tr_ref[i]
    new_expert = jnp.logical_or(i == 0, te_ref[i] != te_ref[jnp.maximum(i - 1, 0)])

    def fetch(e):
        return [pltpu.make_async_copy(src.at[e], dst, wsem)
                for src, dst in ((wg_hbm, wg_f), (wu_hbm, wu_f), (wd_hbm, wd_f))]

    @pl.when(i == 0)
    def _():
        for cp in fetch(te_ref[0]):
            cp.start()

    @pl.when(jnp.logical_and(rows > 0, new_expert))
    def _():
        for cp in fetch(te_ref[i]):
            cp.wait()
        wgu_bf[:, :D_EXPERT] = wg_f[...].astype(BF16)
        wgu_bf[:, D_EXPERT:] = wu_f[...].astype(BF16)
        wd_bf[...] = wd_f[...].astype(BF16)

        @pl.when(nxt_ref[i] >= 0)
        def _():
            for cp in fetch(nxt_ref[i]):
                cp.start()

    half = TILE_E // 2
    for part in range(2):
        @pl.when(rows > part * half)
        def _(part=part):
            x = _load_row_tiles(xs_ref, stage_ref, half, part * half).astype(BF16)
            gu = jnp.dot(x, wgu_bf[...], preferred_element_type=F32)
            g, up = gu[:, :D_EXPERT], gu[:, D_EXPERT:]
            hmid = (g * jax.nn.sigmoid(g) * up).astype(BF16)
            _store_row_tiles(y_ref, None, jnp.dot(hmid, wd_bf[...], preferred_element_type=F32), part * half)

        @pl.when(rows <= part * half)
        def _(part=part):
            y_ref[part * half * ROW_CHUNKS:(part + 1) * half * ROW_CHUNKS, :] = (
                jnp.zeros((half * ROW_CHUNKS, LANES), y_ref.dtype))


def _experts(tile_expert, tile_block, tile_rows, tile_next, xs, w_gate, w_up, w_down):
    n_tiles = tile_expert.shape[0]
    D = D_MODEL
    rows_spec = pl.BlockSpec((TILE_E * ROW_CHUNKS, LANES), lambda i, te, tb, tr, nx: (tb[i], 0))
    in_hbm = pl.BlockSpec(memory_space=pl.ANY)
    grid_spec = pltpu.PrefetchScalarGridSpec(
        num_scalar_prefetch=4,
        grid=(n_tiles,),
        in_specs=[rows_spec, in_hbm, in_hbm, in_hbm],
        out_specs=pl.BlockSpec((TILE_E * ROW_CHUNKS, LANES), lambda i, te, tb, tr, nx: (i, 0)),
        scratch_shapes=[pltpu.VMEM((D, 2 * D_EXPERT), BF16),
                        pltpu.VMEM((D_EXPERT, D), BF16), pltpu.VMEM((TILE_E * ROW_CHUNKS, LANES), F32),
                        pltpu.VMEM((D, D_EXPERT), F32), pltpu.VMEM((D, D_EXPERT), F32),
                        pltpu.VMEM((D_EXPERT, D), F32), pltpu.SemaphoreType.DMA(())],
    )
    return pl.pallas_call(
        _experts_kernel,
        grid_spec=grid_spec,
        out_shape=jax.ShapeDtypeStruct(xs.shape, F32),
        compiler_params=_cparams(("arbitrary",)),
        name="experts",
    )(tile_expert, tile_block, tile_rows, tile_next, xs, w_gate, w_up, w_down)


def _combine_kernel(pos0_ref, pos1_ref, next0_ref, next1_ref, h_ref, gate_ref, y_ref, o_ref, ybuf, sems, *, n_steps):
    i = pl.program_id(0)
    tm = h_ref.shape[0]
    slot = lax.rem(i, 2)

    def gather(p_refs, s):
        def start(r, _):
            for k, p_ref in enumerate(p_refs):
                _row_tile_copy(y_ref, p_ref[r], ybuf.at[s, k], r, sems.at[s]).start(priority=k)
            return 0

        lax.fori_loop(0, tm, start, 0, unroll=DMA_UNROLL)

    @pl.when(i == 0)
    def _():
        gather((pos0_ref, pos1_ref), slot)

    @pl.when(i + 1 < n_steps)
    def _():
        gather((next0_ref, next1_ref), 1 - slot)

    for k in range(2):
        pltpu.make_async_copy(y_ref.at[pl.ds(0, tm * ROW_CHUNKS), :], ybuf.at[slot, k], sems.at[slot]).wait()
    g = gate_ref[...]
    o_ref[...] = (h_ref[...] + g[:, 0:1] * _load_row_tiles(ybuf.at[slot, 0], None, tm)
                  + g[:, 1:2] * _load_row_tiles(ybuf.at[slot, 1], None, tm))


def _combine(pos, h, gate_t, y):
    T, D = h.shape
    tm = TM_ROWS
    n_steps = T // tm
    return pl.pallas_call(
        functools.partial(_combine_kernel, n_steps=n_steps),
        grid=(n_steps,),
        in_specs=[pl.BlockSpec((tm,), lambda i: (i,), memory_space=pltpu.SMEM),
                  pl.BlockSpec((tm,), lambda i: (i,), memory_space=pltpu.SMEM),
                  pl.BlockSpec((tm,), lambda i: (jnp.minimum(i + 1, n_steps - 1),), memory_space=pltpu.SMEM),
                  pl.BlockSpec((tm,), lambda i: (jnp.minimum(i + 1, n_steps - 1),), memory_space=pltpu.SMEM),
                  pl.BlockSpec((tm, D), lambda i: (i, 0)),
                  pl.BlockSpec((tm, 2), lambda i: (i, 0)),
                  pl.BlockSpec(memory_space=pl.ANY)],
        out_specs=pl.BlockSpec((tm, D), lambda i: (i, 0)),
        out_shape=jax.ShapeDtypeStruct((T, D), F32),
        scratch_shapes=[pltpu.VMEM((2, 2, tm * ROW_CHUNKS, LANES), y.dtype), pltpu.SemaphoreType.DMA((2,))],
        compiler_params=_cparams(("arbitrary",)),
        name="combine",
    )(pos[0], pos[1], pos[0], pos[1], h, gate_t, y)


def _rope_tables(S):
    inv_freq = 1.0 / (ROPE_THETA ** (np.arange(0, HEAD_DIM, 2, dtype=np.float64) / HEAD_DIM))
    ang = np.arange(S, dtype=np.float64)[:, None] * inv_freq[None, :]
    widen = lambda t: jnp.asarray(np.pad(t, ((0, 0), (0, LANES - HEAD_DIM // 2))).astype(np.float32))
    return widen(np.cos(ang)), widen(np.sin(ang))


def _layer(x, norm_mix, w_in, b_forget, q_norm_fox, k_norm_fox, q_norm_dil, k_norm_dil,
           out_norm_fox, out_norm_dil, w_out, norm_ffn, w_router_group, b_router_group,
           w_router_expert, b_router_expert, w_gate, w_up, w_down):
    B, S, D = x.shape
    T = B * S
    n_main = 6 * D_GRP

    w_main = w_in.astype(BF16)
    w_vt = w_main[:, 2 * D_GRP:3 * D_GRP].T
    w_f = jnp.pad(w_main[:, n_main:], ((0, 0), (0, LANES - N_HEADS)))
    b_f = jnp.pad(b_forget, (0, LANES - N_HEADS))[None, :]
    per_head = lambda g: jnp.tile(g, N_HEADS)[None, :]
    bd = jnp.kron(jnp.eye(N_HEADS, dtype=F32), jnp.ones((HEAD_DIM, HEAD_DIM), F32)).astype(BF16)
    cos_t, sin_t = _rope_tables(S)
    tri = jnp.tril(jnp.ones((TRI_ROWS, TRI_ROWS), F32)).astype(BF16)
    upper = jnp.triu(jnp.ones((TM_OUT, TM_OUT), F32), k=1).astype(BF16)
    w_r = jnp.concatenate([
        jnp.pad(w_router_group.T, ((0, 8 - N_GROUPS), (0, 0))),
        w_router_expert.transpose(0, 2, 1).reshape(N_EXPERTS, D)], axis=0)
    w_r = jnp.pad(w_r, ((0, ROUTER_PAD - ROUTER_ROWS), (0, 0)))
    w_r_hi = w_r.astype(BF16)
    w_r = jnp.concatenate([w_r_hi, (w_r - w_r_hi.astype(F32)).astype(BF16)], axis=0)
    b_r = jnp.concatenate([jnp.pad(b_router_group, (0, 8 - N_GROUPS)), b_router_expert.reshape(-1)])[:, None]

    bound = (HEAD_DIM / math.sqrt(HEAD_DIM)) * LOG2E * jnp.max(jnp.abs(q_norm_fox)) * jnp.max(jnp.abs(k_norm_fox))
    shift = FOX_BOUND_SLACK * bound + 1.0
    qa, ka, va, qb, kb, vb, c2 = _inproj(
        x, norm_mix[None, :], w_main, w_vt, w_f, b_f, jnp.full((1, LANES), shift, F32),
        per_head(q_norm_fox), per_head(k_norm_fox),
        per_head(q_norm_dil), per_head(k_norm_dil), bd, cos_t, sin_t, tri)
    n_q, n_k = S // TQ, S // TK_SUB
    c_first = c2[:, ::TQ, :].reshape(B, n_q, 1, N_PAIRS, HEADS_PER_TILE)
    c_last = c2[:, TK_SUB - 1::TK_SUB, :].reshape(B, 1, n_k, N_PAIRS, HEADS_PER_TILE)
    all_zero = jnp.all(c_first - c_last < -FOX_ZERO_BITS, axis=-1)
    live_so_far = jnp.einsum("bqkp,kj->bqjp", 1 - all_zero.astype(I32),
                             jnp.triu(jnp.ones((n_k, n_k), I32)))
    leading = (live_so_far == 0).astype(I32)
    before = (jnp.arange(n_k) * TK_SUB + TK_SUB <= jnp.arange(n_q)[:, None] * TQ)
    first_tile = jnp.sum(leading * before[None, :, :, None].astype(I32), axis=2)
    first_tile = first_tile.transpose(0, 2, 1).reshape(-1).astype(I32)
    oa = lax.cond(2.0 * shift <= FOX_SAFE_SPAN,
                  lambda: _fox(first_tile, qa, ka, va, online=False),
                  lambda: _fox(jnp.zeros_like(first_tile), qa, ka, va, online=True))
    dil_shift = (FOX_BOUND_SLACK * (HEAD_DIM / math.sqrt(HEAD_DIM))
                 * jnp.max(jnp.abs(q_norm_dil)) * jnp.max(jnp.abs(k_norm_dil)) + 1.0)
    dil_shift_arr = jnp.full((1, 1), dil_shift, F32)
    ob = lax.cond(2.0 * dil_shift <= DIL_SAFE_SPAN,
                  lambda: _dilated(dil_shift_arr, qb, kb, vb, bounded=True),
                  lambda: _dilated(dil_shift_arr, qb, kb, vb, bounded=False))

    h, hn, eid, gate, rank, cnt = _outproj(
        oa.reshape(T, D_GRP), ob.reshape(T, D_GRP), x.reshape(T, D), out_norm_fox[None, :],
        out_norm_dil[None, :], w_out.astype(BF16), norm_ffn[None, :], w_r, b_r, upper)

    counts = cnt[:, 0]
    padded = ((counts + TILE_E - 1) // TILE_E) * TILE_E
    ends = jnp.cumsum(padded)
    starts = ends - padded
    is_expert = eid[:, :, None] == jnp.arange(N_EXPERTS, dtype=I32)
    pos = jnp.sum(jnp.where(is_expert, starts, 0), axis=-1) + rank
    n_tiles = (2 * T) // TILE_E + N_EXPERTS
    tile_index = jnp.arange(n_tiles, dtype=I32)
    tile_valid = tile_index * TILE_E < ends[-1]
    tile_block = jnp.minimum(tile_index, ends[-1] // TILE_E - 1)
    in_region = ends[None, :] <= (tile_block * TILE_E)[:, None]
    tile_expert = jnp.sum(in_region.astype(I32), axis=1)
    expert_of_tile = tile_expert[:, None] == jnp.arange(N_EXPERTS, dtype=I32)
    rows_before = tile_block * TILE_E - jnp.sum(jnp.where(expert_of_tile, starts, 0), axis=1)
    tile_rows = jnp.clip(jnp.sum(jnp.where(expert_of_tile, counts, 0), axis=1) - rows_before, 0, TILE_E)
    tile_rows = jnp.where(tile_valid, tile_rows, 0).astype(I32)

    xs = _scatter(starts, counts, pos, hn, n_tiles * TILE_E)
    experts = jnp.arange(N_EXPERTS, dtype=I32)
    later = jnp.logical_and(counts[None, :] > 0, experts[None, :] > experts[:, None])
    next_used = jnp.min(jnp.where(later, experts[None, :], N_EXPERTS), axis=1)
    next_used = jnp.where(next_used < N_EXPERTS, next_used, -1)
    tile_next = jnp.sum(jnp.where(expert_of_tile, next_used[None, :], 0), axis=1).astype(I32)
    y = _experts(tile_expert, tile_block, tile_rows, tile_next, xs, w_gate, w_up, w_down)
    out = _combine(pos, h, gate.T, y)
    return out.reshape(B, S, D)


def kernel(x, norm_mix, w_in, b_forget, q_norm_fox, k_norm_fox, q_norm_dil, k_norm_dil, out_norm_fox,
           out_norm_dil, w_out, norm_ffn, w_router_group, b_router_group, w_router_expert,
           b_router_expert, w_gate, w_up, w_down):
    h = x
    for l in range(norm_mix.shape[0]):
        h = _layer(h, norm_mix[l], w_in[l], b_forget[l], q_norm_fox[l], k_norm_fox[l], q_norm_dil[l],
                   k_norm_dil[l], out_norm_fox[l], out_norm_dil[l], w_out[l], norm_ffn[l],
                   w_router_group[l], b_router_group[l], w_router_expert[l], b_router_expert[l],
                   w_gate[l], w_up[l], w_down[l])
    return h
```

```python
import functools
import math

import jax
import jax.numpy as jnp
import numpy as np
from jax import lax
from jax.experimental import pallas as pl
from jax.experimental.pallas import tpu as pltpu

F32 = jnp.float32
BF16 = jnp.bfloat16
I32 = jnp.int32

D_MODEL = 1024
HEAD_DIM = 64
N_HEADS = 8
D_GRP = N_HEADS * HEAD_DIM
LANES = 128
HEADS_PER_TILE = LANES // HEAD_DIM
N_PAIRS = D_GRP // LANES
DIL_PATTERNS = ((128, 1), (512, 4), (2048, 16))
BLOCK = 128
ROPE_THETA = 10000.0
N_GROUPS = 4
EXPERTS_PER_GROUP = 8
N_EXPERTS = N_GROUPS * EXPERTS_PER_GROUP
D_EXPERT = 512
EPS = 1e-6
NEG = -1e30
LOG2E = 1.4426950408889634

TM_IN = 1024
TRI_ROWS = 256
TQ = 1024
TK_WIDE = 1024
TK_SUB = 256
FOX_AHEAD = 4
FOX_ZERO_BITS = 160.0
FOX_BOUND_SLACK = 1.02
FOX_SAFE_SPAN = 100.0
DIL_SAFE_SPAN = FOX_SAFE_SPAN * math.log(2.0)
DIL_SPAN = 2048
DIL_GROUP = 16
DIL_PRE = 4
TM_OUT = 1024
TM_ROWS = 512
TILE_E = 512
ROUTER_ROWS = 8 + N_EXPERTS
ROUTER_PAD = 48
ROW_CHUNKS = D_MODEL // LANES
DMA_UNROLL = 8
VMEM_LIMIT = 56 * 1024 * 1024


def _cparams(sem):
    return pltpu.CompilerParams(dimension_semantics=sem, vmem_limit_bytes=VMEM_LIMIT)


def _inproj_kernel(x_ref, gmix_ref, w_ref, wvt_ref, wf_ref, bf_ref, shift_ref, gqa_ref, gka_ref, gqb_ref, gkb_ref,
                   bd_ref, cos_ref, sin_ref, tri_ref,
                   qa_ref, ka_ref, va_ref, qb_ref, kb_ref, vb_ref, c_ref, carry_ref):
    @pl.when(pl.program_id(1) == 0)
    def _():
        carry_ref[...] = jnp.zeros_like(carry_ref)

    x = x_ref[0]
    ms = jnp.mean(x * x, axis=-1, keepdims=True)
    xn = (x * lax.rsqrt(ms + EPS) * gmix_ref[...]).astype(BF16)

    def seg(j):
        return jnp.dot(xn, w_ref[:, j * D_GRP:(j + 1) * D_GRP], preferred_element_type=F32)

    def head_norm(y, g_ref, scale):
        ss = jnp.dot((y * y).astype(BF16), bd_ref[...], preferred_element_type=F32) * (1.0 / HEAD_DIM)
        return y * lax.rsqrt(ss + EPS) * (g_ref[...] * scale)

    lane = lax.broadcasted_iota(I32, (x.shape[0], LANES), 1)
    first_half = (lane % HEAD_DIM) < (HEAD_DIM // 2)

    def spread(t):
        t = t + pltpu.roll(t, HEAD_DIM // 2, 1)
        return t + pltpu.roll(t, HEAD_DIM, 1)

    cos = spread(cos_ref[...])
    sin = jnp.where(first_half, -1.0, 1.0) * spread(sin_ref[...])

    def rope(y):
        outs = []
        for j in range(N_PAIRS):
            ys = y[:, j * LANES:(j + 1) * LANES]
            partner = jnp.where(first_half, pltpu.roll(ys, LANES - HEAD_DIM // 2, 1),
                                pltpu.roll(ys, HEAD_DIM // 2, 1))
            outs.append(ys * cos + partner * sin)
        return jnp.concatenate(outs, axis=1)

    scale = 1.0 / math.sqrt(HEAD_DIM)
    va_ref[0] = lax.dot_general(wvt_ref[...], xn, (((1,), (1,)), ((), ())),
                                preferred_element_type=F32).astype(va_ref.dtype)
    qb_ref[0] = rope(head_norm(seg(3), gqb_ref, scale)).astype(qb_ref.dtype)
    kb_ref[0] = rope(head_norm(seg(4), gkb_ref, 1.0)).astype(kb_ref.dtype)
    vb_ref[0] = seg(5).astype(vb_ref.dtype)

    fa = jnp.dot(xn, wf_ref[...], preferred_element_type=F32) + bf_ref[...]
    logf = jnp.minimum(fa, 0.0) - jnp.log1p(jnp.exp(-jnp.abs(fa)))
    hi = logf.astype(BF16)
    mid = (logf - hi.astype(F32)).astype(BF16)
    lo = (logf - hi.astype(F32) - mid.astype(F32)).astype(BF16)
    pieces = jnp.concatenate([hi, mid, lo], axis=1)
    carry = carry_ref[...]
    blocks = []
    for j in range(x.shape[0] // TRI_ROWS):
        parts = jnp.dot(tri_ref[...], pieces[j * TRI_ROWS:(j + 1) * TRI_ROWS, :], preferred_element_type=F32)
        blk = parts[:, :LANES] + parts[:, LANES:2 * LANES] + parts[:, 2 * LANES:] + carry
        carry = blk[TRI_ROWS - 1:, :]
        blocks.append(blk)
    c = jnp.concatenate(blocks, axis=0)
    carry_ref[...] = carry

    qa = head_norm(seg(0), gqa_ref, scale * LOG2E)
    ka = head_norm(seg(1), gka_ref, 1.0)
    c2 = c * LOG2E
    c_ref[0] = c2[:, :N_HEADS]
    ones = (jnp.where((lane >= HEAD_DIM + 3) & (lane < HEAD_DIM + 6), 1.0, 0.0)
            - jnp.where(lane == HEAD_DIM + 6, shift_ref[...], 0.0))
    for h in range(N_HEADS):
        cb = jnp.broadcast_to(c2[:, h:h + 1], (x.shape[0], LANES))
        hi = cb.astype(BF16).astype(F32)
        mid = (cb - hi).astype(BF16).astype(F32)
        lo = cb - hi - mid
        pieces = jnp.where(lane == HEAD_DIM, hi, jnp.where(lane == HEAD_DIM + 1, mid,
                           jnp.where(lane == HEAD_DIM + 2, lo, 0.0)))
        q_extra = pieces + ones
        k_extra = jnp.where(((lane >= HEAD_DIM) & (lane < HEAD_DIM + 3)) | (lane == HEAD_DIM + 6), 1.0, 0.0) \
            - pltpu.roll(pieces, 3, 1)
        j, odd = divmod(h, HEADS_PER_TILE)
        qp = qa[:, j * LANES:(j + 1) * LANES]
        kp = ka[:, j * LANES:(j + 1) * LANES]
        if odd:
            qp = pltpu.roll(qp, HEAD_DIM, 1)
            kp = pltpu.roll(kp, HEAD_DIM, 1)
        qa_ref[0, h] = jnp.where(lane < HEAD_DIM, qp, q_extra).astype(qa_ref.dtype)
        ka_ref[0, h] = jnp.where(lane < HEAD_DIM, kp, k_extra).astype(ka_ref.dtype)


def _inproj(x, gmix, w_main, w_vt, w_f, b_f, shift, gqa, gka, gqb, gkb, bd, cos_t, sin_t, tri):
    B, S, D = x.shape
    tm = TM_IN
    const = lambda shape: pl.BlockSpec(shape, lambda b, i: (0,) * len(shape))
    tok = lambda w, dt: jax.ShapeDtypeStruct((B, S, w), dt)
    tok_spec = lambda w: pl.BlockSpec((1, tm, w), lambda b, i: (b, i, 0))
    head_spec = pl.BlockSpec((1, N_HEADS, tm, LANES), lambda b, i: (b, 0, i, 0))
    head_shape = jax.ShapeDtypeStruct((B, N_HEADS, S, LANES), BF16)
    return pl.pallas_call(
        _inproj_kernel,
        grid=(B, S // tm),
        in_specs=[tok_spec(D), const((1, D)), const(w_main.shape), const(w_vt.shape), const(w_f.shape),
                  const((1, LANES)), const((1, LANES)),
                  const((1, D_GRP)), const((1, D_GRP)), const((1, D_GRP)), const((1, D_GRP)),
                  const((D_GRP, D_GRP)),
                  pl.BlockSpec((tm, LANES), lambda b, i: (i, 0)),
                  pl.BlockSpec((tm, LANES), lambda b, i: (i, 0)),
                  const(tri.shape)],
        out_specs=[head_spec, head_spec, pl.BlockSpec((1, D_GRP, tm), lambda b, i: (b, 0, i))]
        + [tok_spec(D_GRP)] * 3 + [tok_spec(N_HEADS)],
        out_shape=[head_shape, head_shape, jax.ShapeDtypeStruct((B, D_GRP, S), BF16),
                   tok(D_GRP, F32), tok(D_GRP, F32), tok(D_GRP, F32), tok(N_HEADS, F32)],
        scratch_shapes=[pltpu.VMEM((1, LANES), F32)],
        compiler_params=_cparams(("arbitrary", "arbitrary")),
        name="inproj",
    )(x, gmix, w_main, w_vt, w_f, b_f, shift, gqa, gka, gqb, gkb, bd, cos_t, sin_t, tri)


def _fox_kernel(first_ref, q_ref, k_ref, v_ref, o_ref, *, online):
    qi = pl.program_id(2)
    tq = q_ref.shape[2]

    def step(start, width, carry, diag):
        carry = list(carry)
        sub = min(TK_SUB, width)
        chunks = [(c, j) for c in range(width // sub) for j in range(HEADS_PER_TILE)]
        def first_query(c):
            return c * sub if diag else 0

        def score(c, j):
            k = k_ref[0, j, pl.ds(start + c * sub, sub), :]
            q = q_ref[0, j, first_query(c):, :]
            return lax.dot_general(k, q, (((1,), (1,)), ((), ())), preferred_element_type=F32)

        def tail(full, lo, new):
            return new if lo == 0 else jnp.concatenate([full[:, :lo], new], axis=1)

        scores = {cj: score(*cj) for cj in chunks[:FOX_AHEAD]}
        for n, (c, j) in enumerate(chunks):
                if n + FOX_AHEAD < len(chunks):
                    nxt = chunks[n + FOX_AHEAD]
                    scores[nxt] = score(*nxt)
                m, l, acc = carry[j]
                lo = first_query(c)
                vt = v_ref[0, :, pl.ds(start + c * sub, sub)]
                s = scores.pop((c, j))
                if diag:
                    key = lax.broadcasted_iota(I32, s.shape, 0)
                    qry = lax.broadcasted_iota(I32, s.shape, 1)
                    s = jnp.where(key <= qry, s, NEG)
                if online:
                    m_new = jnp.maximum(m[:, lo:], jnp.max(s, axis=0, keepdims=True))
                    alpha = jnp.exp2(m[:, lo:] - m_new)
                    p = jnp.exp2(s - m_new)
                    l_new = alpha * l[:, lo:] + jnp.sum(p, axis=0, keepdims=True)
                    acc_new = alpha * acc[:, lo:] + jnp.dot(vt, p.astype(BF16), preferred_element_type=F32)
                    m = tail(m, lo, m_new)
                else:
                    p = jnp.exp2(s)
                    l_new = l[:, lo:] + jnp.sum(p, axis=0, keepdims=True)
                    acc_new = acc[:, lo:] + jnp.dot(vt, p.astype(BF16), preferred_element_type=F32)
                carry[j] = (m, tail(l, lo, l_new), tail(acc, lo, acc_new))
        return tuple(carry)

    init = tuple((jnp.full((1, tq), NEG, F32), jnp.zeros((1, tq), F32), jnp.zeros((LANES, tq), F32))
                 for _ in range(HEADS_PER_TILE))
    assert TK_WIDE == tq and TK_WIDE % TK_SUB == 0
    per_wide = TK_WIDE // TK_SUB
    first = first_ref[(pl.program_id(0) * pl.num_programs(1) + pl.program_id(1)) * pl.num_programs(2) + qi]
    live = qi * per_wide - first
    n_narrow = lax.rem(live, per_wide)
    narrow_start = pl.multiple_of(first * TK_SUB, TK_SUB)
    carry = lax.switch(
        n_narrow,
        [lambda c: c] + [lambda c, w=w: step(narrow_start, w * TK_SUB, c, False) for w in range(1, per_wide)],
        init)
    wide_start = (first + n_narrow) * TK_SUB
    carry = lax.fori_loop(
        0, lax.div(live, per_wide),
        lambda i, c: step(pl.multiple_of(wide_start + i * TK_WIDE, TK_SUB), TK_WIDE, c, False), carry)
    carry = step(pl.multiple_of(qi * tq, tq), tq, carry, True)
    outs = [acc / l for (_, l, acc) in carry]
    feat = lax.broadcasted_iota(I32, (LANES, tq), 0)
    o_ref[0] = jnp.where(feat < HEAD_DIM, outs[0], outs[1]).T.astype(o_ref.dtype)


def _fox(first_tile, qa, ka, va_t, online):
    B, _, S, _ = qa.shape
    grid_spec = pltpu.PrefetchScalarGridSpec(
        num_scalar_prefetch=1,
        grid=(B, N_PAIRS, S // TQ),
        in_specs=[pl.BlockSpec((1, HEADS_PER_TILE, TQ, LANES), lambda b, hp, i, first: (b, hp, i, 0)),
                  pl.BlockSpec((1, HEADS_PER_TILE, S, LANES), lambda b, hp, i, first: (b, hp, 0, 0)),
                  pl.BlockSpec((1, LANES, S), lambda b, hp, i, first: (b, hp, 0))],
        out_specs=pl.BlockSpec((1, TQ, LANES), lambda b, hp, i, first: (b, i, hp)),
    )
    return pl.pallas_call(
        functools.partial(_fox_kernel, online=online),
        grid_spec=grid_spec,
        out_shape=jax.ShapeDtypeStruct((B, S, D_GRP), F32),
        compiler_params=_cparams(("arbitrary", "arbitrary", "arbitrary")),
        name="fox_online" if online else "fox",
    )(first_tile, qa, ka, va_t)


def _dilated_kernel(shift_ref, q_ref, kp_ref, kc_ref, vp_ref, vc_ref, o_ref, qq, kk, vv, qq4, kk4, vv4, osc, lsc,
                    *, bounded):
    u = pl.program_id(1)
    span = q_ref.shape[1]
    qq[...] = q_ref[0]
    kk[0:span, :] = kp_ref[0]
    kk[span:2 * span, :] = kc_ref[0]
    vv[0:span, :] = vp_ref[0]
    vv[span:2 * span, :] = vc_ref[0]
    for src, dst in ((qq, qq4), (kk, kk4), (vv, vv4)):
        part = src.shape[0] // DIL_PRE
        for a in range(DIL_PRE):
            dst[a * part:(a + 1) * part, :] = src[pl.ds(a, part, stride=DIL_PRE), :]

    def rows(buf, buf4, start, n, d):
        if d % DIL_PRE:
            return buf[pl.ds(start, n, stride=d), :]
        part = buf4.shape[0] // DIL_PRE
        a = lax.rem(start, DIL_PRE)
        return buf4[pl.ds(a * part + lax.div(start, DIL_PRE), n, stride=d // DIL_PRE), :]

    lane = lax.broadcasted_iota(I32, (BLOCK, LANES), 1)
    ql = lax.broadcasted_iota(I32, (BLOCK, 2 * BLOCK), 0)
    kl = lax.broadcasted_iota(I32, (BLOCK, 2 * BLOCK), 1)
    dist = ql + BLOCK - kl
    band = (dist >= 0) & (dist <= BLOCK)
    live = -shift_ref[0, 0] if bounded else 0.0
    bias = jnp.where(band, live, NEG)
    bias_first = jnp.where(band & (kl >= BLOCK), live, NEG)

    def scores(q_start, k_start, d, first):
        qs = rows(qq, qq4, q_start, BLOCK, d).astype(BF16)
        ks = rows(kk, kk4, k_start, 2 * BLOCK, d).astype(BF16)
        mask = jnp.where(first, bias_first, bias)
        out = []
        for j in range(HEADS_PER_TILE):
            qj = jnp.where(lane // HEAD_DIM == j, qs, jnp.zeros_like(qs))
            out.append(lax.dot_general(qj, ks, (((1,), (1,)), ((), ())), preferred_element_type=F32) + mask)
        return out

    def finish(s_heads, k_start, d):
        vs = rows(vv, vv4, k_start, 2 * BLOCK, d).astype(BF16)
        o_heads, lse_heads = [], []
        for s in s_heads:
            if bounded:
                p = jnp.exp(s)
                o_heads.append(jnp.dot(p.astype(BF16), vs, preferred_element_type=F32))
                lse_heads.append(jnp.sum(p, axis=-1, keepdims=True))
            else:
                m = jnp.max(s, axis=-1, keepdims=True)
                p = jnp.exp(s - m)
                l = jnp.sum(p, axis=-1, keepdims=True)
                o_heads.append(jnp.dot((p / l).astype(BF16), vs, preferred_element_type=F32))
                lse_heads.append(m + jnp.log(l))
        o = jnp.where(lane < HEAD_DIM, o_heads[0], o_heads[1])
        lse = jnp.where(lane < HEAD_DIM, lse_heads[0], lse_heads[1])
        return o, lse

    for pidx, (window, d) in enumerate(DIL_PATTERNS):
        assert window // d == BLOCK
        unit = d * BLOCK
        n_problems = (span // unit) * d
        assert n_problems % DIL_GROUP == 0

        def body(g, _, pidx=pidx, d=d, unit=unit):
            starts, s_all = [], []
            for t in range(DIL_GROUP):
                idx = g * DIL_GROUP + t
                w = idx // d
                q_start = w * unit + (idx - w * d)
                k_start = span - unit + q_start
                starts.append((q_start, k_start))
                s_all.append(scores(q_start, k_start, d, jnp.logical_and(u == 0, w == 0)))
            for (q_start, k_start), s_heads in zip(starts, s_all):
                o, lse = finish(s_heads, k_start, d)
                osc[pidx, pl.ds(q_start, BLOCK, stride=d), :] = o
                lsc[pidx, pl.ds(q_start, BLOCK, stride=d), :] = lse
            return 0

        lax.fori_loop(0, n_problems // DIL_GROUP, body, 0)

    if bounded:
        num = osc[0] + osc[1] + osc[2]
        den = lsc[0] + lsc[1] + lsc[2]
    else:
        mx = jnp.maximum(jnp.maximum(lsc[0], lsc[1]), lsc[2])
        num = jnp.zeros((span, LANES), F32)
        den = jnp.zeros((span, LANES), F32)
        for pidx in range(len(DIL_PATTERNS)):
            e = jnp.exp(lsc[pidx] - mx)
            num = num + e * osc[pidx]
            den = den + e
    o_ref[0] = (num / den).astype(o_ref.dtype)


def _dilated(shift, qb, kb, vb, bounded):
    B, S, _ = qb.shape
    span = DIL_SPAN
    cur = pl.BlockSpec((1, span, LANES), lambda b, u, hp: (b, u, hp))
    prev = pl.BlockSpec((1, span, LANES), lambda b, u, hp: (b, jnp.maximum(u - 1, 0), hp))
    return pl.pallas_call(
        functools.partial(_dilated_kernel, bounded=bounded),
        grid=(B, S // span, N_PAIRS),
        in_specs=[pl.BlockSpec(memory_space=pltpu.SMEM), cur, prev, cur, prev, cur],
        out_specs=cur,
        out_shape=jax.ShapeDtypeStruct((B, S, D_GRP), F32),
        scratch_shapes=[pltpu.VMEM((span, LANES), F32),
                        pltpu.VMEM((2 * span, LANES), F32), pltpu.VMEM((2 * span, LANES), F32),
                        pltpu.VMEM((span, LANES), F32),
                        pltpu.VMEM((2 * span, LANES), F32), pltpu.VMEM((2 * span, LANES), F32),
                        pltpu.VMEM((len(DIL_PATTERNS), span, LANES), F32),
                        pltpu.VMEM((len(DIL_PATTERNS), span, LANES), F32)],
        compiler_params=_cparams(("arbitrary", "arbitrary", "arbitrary")),
        name="dilated" if bounded else "dilated_exact",
    )(shift, qb, kb, kb, vb, vb)


def _store_row_tiles(ref, stage_ref, x, first_row=0):
    n = x.shape[0]
    target = ref if stage_ref is None else stage_ref
    for c in range(ROW_CHUNKS):
        target[pl.ds(first_row * ROW_CHUNKS + c, n, stride=ROW_CHUNKS), :] = x[:, c * LANES:(c + 1) * LANES]
    if stage_ref is not None:
        rows = slice(first_row * ROW_CHUNKS, (first_row + n) * ROW_CHUNKS)
        ref[rows, :] = stage_ref[rows, :].astype(ref.dtype)


def _load_row_tiles(ref, stage_ref, n, first_row=0):
    source = ref
    if stage_ref is not None:
        rows = slice(first_row * ROW_CHUNKS, (first_row + n) * ROW_CHUNKS)
        stage_ref[rows, :] = ref[rows, :].astype(F32)
        source = stage_ref
    return jnp.concatenate([source[pl.ds(first_row * ROW_CHUNKS + c, n, stride=ROW_CHUNKS), :]
                            for c in range(ROW_CHUNKS)], axis=1)


def _row_tile_copy(src_ref, src_row, dst_ref, dst_row, sem):
    src = src_ref.at[pl.ds(pl.multiple_of(src_row * ROW_CHUNKS, ROW_CHUNKS), ROW_CHUNKS), :]
    dst = dst_ref.at[pl.ds(pl.multiple_of(dst_row * ROW_CHUNKS, ROW_CHUNKS), ROW_CHUNKS), :]
    return pltpu.make_async_copy(src, dst, sem)


def _outproj_kernel(oa_ref, ob_ref, x_ref, gfox_ref, gdil_ref, wo_ref, gffn_ref, wr_ref, br_ref, upper_ref,
                    h_ref, hn_ref, eid_ref, gate_ref, rank_ref, cnt_ref, run_ref, stage_ref):
    @pl.when(pl.program_id(0) == 0)
    def _():
        run_ref[...] = jnp.zeros_like(run_ref)

    def norm(y, g):
        ms = jnp.mean(y * y, axis=-1, keepdims=True)
        return y * lax.rsqrt(ms + EPS) * g

    a = norm(oa_ref[...], gfox_ref[...]).astype(BF16)
    b = norm(ob_ref[...], gdil_ref[...]).astype(BF16)
    mix = (jnp.dot(a, wo_ref[0:D_GRP, :], preferred_element_type=F32)
           + jnp.dot(b, wo_ref[D_GRP:2 * D_GRP, :], preferred_element_type=F32))
    h = x_ref[...] + mix
    h_ref[...] = h
    hn = norm(h, gffn_ref[...])
    _store_row_tiles(hn_ref, stage_ref, hn)

    hn_hi = hn.astype(BF16)
    hn_lo = (hn - hn_hi.astype(F32)).astype(BF16)
    nt = (((1,), (1,)), ((), ()))
    r = lax.dot_general(wr_ref[...], hn_hi, nt, preferred_element_type=F32)
    r_lo = lax.dot_general(wr_ref[:ROUTER_PAD, :], hn_lo, nt, preferred_element_type=F32)
    z = (r[:ROUTER_ROWS, :] + r[ROUTER_PAD:ROUTER_PAD + ROUTER_ROWS, :] + r_lo[:ROUTER_ROWS, :] + br_ref[...])
    tm = z.shape[1]
    best = z[0:1, :]
    g_sel = jnp.zeros((1, tm), I32)
    for g in range(1, N_GROUPS):
        better = z[g:g + 1, :] > best
        g_sel = jnp.where(better, g, g_sel)
        best = jnp.maximum(best, z[g:g + 1, :])
    den = jnp.zeros((1, tm), F32)
    for g in range(N_GROUPS):
        den = den + jnp.exp(z[g:g + 1, :] - best)
    pg_top = 1.0 / den

    ze = jnp.zeros((EXPERTS_PER_GROUP, tm), F32)
    for g in range(N_GROUPS):
        ze = jnp.where(g_sel == g, z[8 + g * EXPERTS_PER_GROUP:8 + (g + 1) * EXPERTS_PER_GROUP, :], ze)
    e_iota = lax.broadcasted_iota(I32, ze.shape, 0)
    v1 = jnp.max(ze, axis=0, keepdims=True)
    i1 = jnp.min(jnp.where(ze == v1, e_iota, EXPERTS_PER_GROUP), axis=0, keepdims=True)
    ze2 = jnp.where(e_iota == i1, -jnp.inf, ze)
    v2 = jnp.max(ze2, axis=0, keepdims=True)
    i2 = jnp.min(jnp.where(ze2 == v2, e_iota, EXPERTS_PER_GROUP), axis=0, keepdims=True)
    e2 = jnp.exp(v2 - v1)
    inv = 1.0 / (1.0 + e2)
    gate1 = inv * pg_top
    gate2 = e2 * inv * pg_top
    eid1 = g_sel * EXPERTS_PER_GROUP + i1
    eid2 = g_sel * EXPERTS_PER_GROUP + i2

    x_iota = lax.broadcasted_iota(I32, (N_EXPERTS, tm), 0)
    hot1 = x_iota == eid1
    hot2 = x_iota == eid2
    multi = jnp.logical_or(hot1, hot2)
    before = jnp.dot(multi.astype(BF16), upper_ref[...], preferred_element_type=F32)
    slot = before + run_ref[:, 0:1]
    rank1 = jnp.sum(jnp.where(hot1, slot, 0.0), axis=0, keepdims=True)
    rank2 = jnp.sum(jnp.where(hot2, slot, 0.0), axis=0, keepdims=True)
    run_ref[...] = run_ref[...] + jnp.sum(multi.astype(F32), axis=1, keepdims=True)

    eid_ref[...] = jnp.concatenate([eid1, eid2], axis=0)
    gate_ref[...] = jnp.concatenate([gate1, gate2], axis=0)
    rank_ref[...] = jnp.concatenate([rank1, rank2], axis=0).astype(I32)
    cnt_ref[...] = run_ref[...].astype(I32)


def _outproj(oa, ob, x2, gfox, gdil, w_out, gffn, w_r, b_r, upper):
    T, D = x2.shape
    tm = TM_OUT
    const = lambda shape: pl.BlockSpec(shape, lambda i: (0,) * len(shape))
    tok = lambda w: pl.BlockSpec((tm, w), lambda i: (i, 0))
    lanes2 = pl.BlockSpec((2, tm), lambda i: (0, i))
    return pl.pallas_call(
        _outproj_kernel,
        grid=(T // tm,),
        in_specs=[tok(D_GRP), tok(D_GRP), tok(D), const((1, D_GRP)), const((1, D_GRP)), const((D, D)),
                  const((1, D)), const((2 * ROUTER_PAD, D)), const((ROUTER_ROWS, 1)), const((tm, tm))],
        out_specs=[tok(D), pl.BlockSpec((tm * ROW_CHUNKS, LANES), lambda i: (i, 0)),
                   lanes2, lanes2, lanes2, const((N_EXPERTS, LANES))],
        out_shape=[jax.ShapeDtypeStruct((T, D), F32), jax.ShapeDtypeStruct((T * ROW_CHUNKS, LANES), BF16),
                   jax.ShapeDtypeStruct((2, T), I32), jax.ShapeDtypeStruct((2, T), F32),
                   jax.ShapeDtypeStruct((2, T), I32), jax.ShapeDtypeStruct((N_EXPERTS, LANES), I32)],
        scratch_shapes=[pltpu.VMEM((N_EXPERTS, LANES), F32), pltpu.VMEM((tm * ROW_CHUNKS, LANES), F32)],
        compiler_params=_cparams(("arbitrary",)),
        name="outproj",
    )(oa, ob, x2, gfox, gdil, w_out, gffn, w_r, b_r, upper)


def _scatter_kernel(starts_ref, cnt_ref, pos0_ref, pos1_ref, hn_ref, xs_ref, ring, sems, zero_sem, *, n_steps):
    i = pl.program_id(0)
    tm = hn_ref.shape[0] // ROW_CHUNKS
    slot = lax.rem(i, 2)

    def wait_slot(s):
        for _ in range(2):
            pltpu.make_async_copy(ring.at[s], xs_ref.at[pl.ds(0, tm * ROW_CHUNKS), :], sems.at[s]).wait()

    @pl.when(i >= 2)
    def _():
        wait_slot(slot)

    ring[slot] = hn_ref[...]

    def start(r, _):
        for k, pos_ref in enumerate((pos0_ref, pos1_ref)):
            _row_tile_copy(ring.at[slot], r, xs_ref, pos_ref[r], sems.at[slot]).start(priority=k)
        return 0

    lax.fori_loop(0, tm, start, 0, unroll=DMA_UNROLL)

    @pl.when(i == n_steps - 1)
    def _():
        wait_slot(slot)
        if n_steps > 1:
            wait_slot(1 - slot)
        ring[0] = jnp.zeros((tm * ROW_CHUNKS, LANES), ring.dtype)

        def pad_expert(e, _, wait):
            n_pad = lax.rem(TILE_E - lax.rem(cnt_ref[e], TILE_E), TILE_E)
            first = starts_ref[e] + cnt_ref[e]
            size = TILE_E // 2
            while size >= 1:
                row0 = first + (n_pad & ~(2 * size - 1))

                @pl.when((n_pad & size) != 0)
                def _(size=size, row0=row0):
                    copy = pltpu.make_async_copy(
                        ring.at[0, pl.ds(0, size * ROW_CHUNKS), :],
                        xs_ref.at[pl.ds(pl.multiple_of(row0 * ROW_CHUNKS, ROW_CHUNKS), size * ROW_CHUNKS), :],
                        zero_sem)
                    copy.wait() if wait else copy.start()

                size //= 2
            return 0

        lax.fori_loop(0, N_EXPERTS, functools.partial(pad_expert, wait=False), 0)
        last = N_EXPERTS - 1
        used_rows = starts_ref[last] + cnt_ref[last] + lax.rem(TILE_E - lax.rem(cnt_ref[last], TILE_E), TILE_E)
        n_tail = xs_ref.shape[0] // (tm * ROW_CHUNKS) - used_rows // tm

        def tail_copy(t):
            row0 = pl.multiple_of((used_rows + t * tm) * ROW_CHUNKS, tm * ROW_CHUNKS)
            return pltpu.make_async_copy(ring.at[0], xs_ref.at[pl.ds(row0, tm * ROW_CHUNKS), :], zero_sem)

        lax.fori_loop(0, n_tail, lambda t, c: (tail_copy(t).start(), c)[1], 0)
        lax.fori_loop(0, N_EXPERTS, functools.partial(pad_expert, wait=True), 0)
        lax.fori_loop(0, n_tail, lambda t, c: (tail_copy(t).wait(), c)[1], 0)


def _scatter(starts, cnt, pos, hn, n_rows):
    T = hn.shape[0] // ROW_CHUNKS
    tm = TM_ROWS
    assert TILE_E % tm == 0 and TILE_E // 2 <= tm
    n_steps = T // tm
    grid_spec = pltpu.PrefetchScalarGridSpec(
        num_scalar_prefetch=2,
        grid=(n_steps,),
        in_specs=[pl.BlockSpec((tm,), lambda i, starts, cnt: (i,), memory_space=pltpu.SMEM),
                  pl.BlockSpec((tm,), lambda i, starts, cnt: (i,), memory_space=pltpu.SMEM),
                  pl.BlockSpec((tm * ROW_CHUNKS, LANES), lambda i, starts, cnt: (i, 0))],
        out_specs=pl.BlockSpec(memory_space=pl.ANY),
        scratch_shapes=[pltpu.VMEM((2, tm * ROW_CHUNKS, LANES), hn.dtype),
                        pltpu.SemaphoreType.DMA((2,)), pltpu.SemaphoreType.DMA(())],
    )
    return pl.pallas_call(
        functools.partial(_scatter_kernel, n_steps=n_steps),
        grid_spec=grid_spec,
        out_shape=jax.ShapeDtypeStruct((n_rows * ROW_CHUNKS, LANES), hn.dtype),
        compiler_params=_cparams(("arbitrary",)),
        name="scatter_rows",
    )(starts, cnt, pos[0], pos[1], hn)


def _experts_kernel(te_ref, tb_ref, tr_ref, nxt_ref, xs_ref, wg_hbm, wu_hbm, wd_hbm, y_ref,
                    wgu_bf, wd_bf, stage_ref, wg_f, wu_f, wd_f, wsem):
    del tb_ref
    i = pl.program_id(0)
    rows = tr_ref[i]
    new_expert = jnp.logical_or(i == 0, te_ref[i] != te_ref[jnp.maximum(i - 1, 0)])

    def fetch(e):
        return [pltpu.make_async_copy(src.at[e], dst, wsem)
                for src, dst in ((wg_hbm, wg_f), (wu_hbm, wu_f), (wd_hbm, wd_f))]

    @pl.when(i == 0)
    def _():
        for cp in fetch(te_ref[0]):
            cp.start()

    @pl.when(jnp.logical_and(rows > 0, new_expert))
    def _():
        for cp in fetch(te_ref[i]):
            cp.wait()
        wgu_bf[:, :D_EXPERT] = wg_f[...].astype(BF16)
        wgu_bf[:, D_EXPERT:] = wu_f[...].astype(BF16)
        wd_bf[...] = wd_f[...].astype(BF16)

        @pl.when(nxt_ref[i] >= 0)
        def _():
            for cp in fetch(nxt_ref[i]):
                cp.start(priority=1)

    half = TILE_E // 2
    for part in range(2):
        @pl.when(rows > part * half)
        def _(part=part):
            x = _load_row_tiles(xs_ref, stage_ref, half, part * half).astype(BF16)
            gu = jnp.dot(x, wgu_bf[...], preferred_element_type=F32)
            g, up = gu[:, :D_EXPERT], gu[:, D_EXPERT:]
            hmid = (g * jax.nn.sigmoid(g) * up).astype(BF16)
            _store_row_tiles(y_ref, None, jnp.dot(hmid, wd_bf[...], preferred_element_type=F32), part * half)

        @pl.when(rows <= part * half)
        def _(part=part):
            y_ref[part * half * ROW_CHUNKS:(part + 1) * half * ROW_CHUNKS, :] = (
                jnp.zeros((half * ROW_CHUNKS, LANES), y_ref.dtype))


def _experts(tile_expert, tile_block, tile_rows, tile_next, xs, w_gate, w_up, w_down):
    n_tiles = tile_expert.shape[0]
    D = D_MODEL
    rows_spec = pl.BlockSpec((TILE_E * ROW_CHUNKS, LANES), lambda i, te, tb, tr, nx: (tb[i], 0))
    in_hbm = pl.BlockSpec(memory_space=pl.ANY)
    grid_spec = pltpu.PrefetchScalarGridSpec(
        num_scalar_prefetch=4,
        grid=(n_tiles,),
        in_specs=[rows_spec, in_hbm, in_hbm, in_hbm],
        out_specs=pl.BlockSpec((TILE_E * ROW_CHUNKS, LANES), lambda i, te, tb, tr, nx: (i, 0)),
        scratch_shapes=[pltpu.VMEM((D, 2 * D_EXPERT), BF16),
                        pltpu.VMEM((D_EXPERT, D), BF16), pltpu.VMEM((TILE_E * ROW_CHUNKS, LANES), F32),
                        pltpu.VMEM((D, D_EXPERT), F32), pltpu.VMEM((D, D_EXPERT), F32),
                        pltpu.VMEM((D_EXPERT, D), F32), pltpu.SemaphoreType.DMA(())],
    )
    return pl.pallas_call(
        _experts_kernel,
        grid_spec=grid_spec,
        out_shape=jax.ShapeDtypeStruct(xs.shape, F32),
        compiler_params=_cparams(("arbitrary",)),
        name="experts",
    )(tile_expert, tile_block, tile_rows, tile_next, xs, w_gate, w_up, w_down)


def _combine_kernel(pos0_ref, pos1_ref, next0_ref, next1_ref, h_ref, gate_ref, y_ref, o_ref, ybuf, sems, *, n_steps):
    i = pl.program_id(0)
    tm = h_ref.shape[0]
    slot = lax.rem(i, 2)

    def gather(p_refs, s):
        def start(r, _):
            for k, p_ref in enumerate(p_refs):
                _row_tile_copy(y_ref, p_ref[r], ybuf.at[s, k], r, sems.at[s]).start(priority=k)
            return 0

        lax.fori_loop(0, tm, start, 0, unroll=DMA_UNROLL)

    @pl.when(i == 0)
    def _():
        gather((pos0_ref, pos1_ref), slot)

    @pl.when(i + 1 < n_steps)
    def _():
        gather((next0_ref, next1_ref), 1 - slot)

    for k in range(2):
        pltpu.make_async_copy(y_ref.at[pl.ds(0, tm * ROW_CHUNKS), :], ybuf.at[slot, k], sems.at[slot]).wait()
    g = gate_ref[...]
    o_ref[...] = (h_ref[...] + g[:, 0:1] * _load_row_tiles(ybuf.at[slot, 0], None, tm)
                  + g[:, 1:2] * _load_row_tiles(ybuf.at[slot, 1], None, tm))


def _combine(pos, h, gate_t, y):
    T, D = h.shape
    tm = TM_ROWS
    n_steps = T // tm
    return pl.pallas_call(
        functools.partial(_combine_kernel, n_steps=n_steps),
        grid=(n_steps,),
        in_specs=[pl.BlockSpec((tm,), lambda i: (i,), memory_space=pltpu.SMEM),
                  pl.BlockSpec((tm,), lambda i: (i,), memory_space=pltpu.SMEM),
                  pl.BlockSpec((tm,), lambda i: (jnp.minimum(i + 1, n_steps - 1),), memory_space=pltpu.SMEM),
                  pl.BlockSpec((tm,), lambda i: (jnp.minimum(i + 1, n_steps - 1),), memory_space=pltpu.SMEM),
                  pl.BlockSpec((tm, D), lambda i: (i, 0)),
                  pl.BlockSpec((tm, 2), lambda i: (i, 0)),
                  pl.BlockSpec(memory_space=pl.ANY)],
        out_specs=pl.BlockSpec((tm, D), lambda i: (i, 0)),
        out_shape=jax.ShapeDtypeStruct((T, D), F32),
        scratch_shapes=[pltpu.VMEM((2, 2, tm * ROW_CHUNKS, LANES), y.dtype), pltpu.SemaphoreType.DMA((2,))],
        compiler_params=_cparams(("arbitrary",)),
        name="combine",
    )(pos[0], pos[1], pos[0], pos[1], h, gate_t, y)


def _rope_tables(S):
    inv_freq = 1.0 / (ROPE_THETA ** (np.arange(0, HEAD_DIM, 2, dtype=np.float64) / HEAD_DIM))
    ang = np.arange(S, dtype=np.float64)[:, None] * inv_freq[None, :]
    widen = lambda t: jnp.asarray(np.pad(t, ((0, 0), (0, LANES - HEAD_DIM // 2))).astype(np.float32))
    return widen(np.cos(ang)), widen(np.sin(ang))


def _layer(x, norm_mix, w_in, b_forget, q_norm_fox, k_norm_fox, q_norm_dil, k_norm_dil,
           out_norm_fox, out_norm_dil, w_out, norm_ffn, w_router_group, b_router_group,
           w_router_expert, b_router_expert, w_gate, w_up, w_down):
    B, S, D = x.shape
    T = B * S
    n_main = 6 * D_GRP

    w_main = w_in.astype(BF16)
    w_vt = w_main[:, 2 * D_GRP:3 * D_GRP].T
    w_f = jnp.pad(w_main[:, n_main:], ((0, 0), (0, LANES - N_HEADS)))
    b_f = jnp.pad(b_forget, (0, LANES - N_HEADS))[None, :]
    per_head = lambda g: jnp.tile(g, N_HEADS)[None, :]
    bd = jnp.kron(jnp.eye(N_HEADS, dtype=F32), jnp.ones((HEAD_DIM, HEAD_DIM), F32)).astype(BF16)
    cos_t, sin_t = _rope_tables(S)
    tri = jnp.tril(jnp.ones((TRI_ROWS, TRI_ROWS), F32)).astype(BF16)
    upper = jnp.triu(jnp.ones((TM_OUT, TM_OUT), F32), k=1).astype(BF16)
    w_r = jnp.concatenate([
        jnp.pad(w_router_group.T, ((0, 8 - N_GROUPS), (0, 0))),
        w_router_expert.transpose(0, 2, 1).reshape(N_EXPERTS, D)], axis=0)
    w_r = jnp.pad(w_r, ((0, ROUTER_PAD - ROUTER_ROWS), (0, 0)))
    w_r_hi = w_r.astype(BF16)
    w_r = jnp.concatenate([w_r_hi, (w_r - w_r_hi.astype(F32)).astype(BF16)], axis=0)
    b_r = jnp.concatenate([jnp.pad(b_router_group, (0, 8 - N_GROUPS)), b_router_expert.reshape(-1)])[:, None]

    bound = (HEAD_DIM / math.sqrt(HEAD_DIM)) * LOG2E * jnp.max(jnp.abs(q_norm_fox)) * jnp.max(jnp.abs(k_norm_fox))
    shift = FOX_BOUND_SLACK * bound + 1.0
    qa, ka, va, qb, kb, vb, c2 = _inproj(
        x, norm_mix[None, :], w_main, w_vt, w_f, b_f, jnp.full((1, LANES), shift, F32),
        per_head(q_norm_fox), per_head(k_norm_fox),
        per_head(q_norm_dil), per_head(k_norm_dil), bd, cos_t, sin_t, tri)
    n_q, n_k = S // TQ, S // TK_SUB
    c_first = c2[:, ::TQ, :].reshape(B, n_q, 1, N_PAIRS, HEADS_PER_TILE)
    c_last = c2[:, TK_SUB - 1::TK_SUB, :].reshape(B, 1, n_k, N_PAIRS, HEADS_PER_TILE)
    all_zero = jnp.all(c_first - c_last < -FOX_ZERO_BITS, axis=-1)
    live_so_far = jnp.einsum("bqkp,kj->bqjp", 1 - all_zero.astype(I32),
                             jnp.triu(jnp.ones((n_k, n_k), I32)))
    leading = (live_so_far == 0).astype(I32)
    before = (jnp.arange(n_k) * TK_SUB + TK_SUB <= jnp.arange(n_q)[:, None] * TQ)
    first_tile = jnp.sum(leading * before[None, :, :, None].astype(I32), axis=2)
    first_tile = first_tile.transpose(0, 2, 1).reshape(-1).astype(I32)
    oa = lax.cond(2.0 * shift <= FOX_SAFE_SPAN,
                  lambda: _fox(first_tile, qa, ka, va, online=False),
                  lambda: _fox(jnp.zeros_like(first_tile), qa, ka, va, online=True))
    dil_shift = (FOX_BOUND_SLACK * (HEAD_DIM / math.sqrt(HEAD_DIM))
                 * jnp.max(jnp.abs(q_norm_dil)) * jnp.max(jnp.abs(k_norm_dil)) + 1.0)
    dil_shift_arr = jnp.full((1, 1), dil_shift, F32)
    ob = lax.cond(2.0 * dil_shift <= DIL_SAFE_SPAN,
                  lambda: _dilated(dil_shift_arr, qb, kb, vb, bounded=True),
                  lambda: _dilated(dil_shift_arr, qb, kb, vb, bounded=False))

    h, hn, eid, gate, rank, cnt = _outproj(
        oa.reshape(T, D_GRP), ob.reshape(T, D_GRP), x.reshape(T, D), out_norm_fox[None, :],
        out_norm_dil[None, :], w_out.astype(BF16), norm_ffn[None, :], w_r, b_r, upper)

    counts = cnt[:, 0]
    padded = ((counts + TILE_E - 1) // TILE_E) * TILE_E
    ends = jnp.cumsum(padded)
    starts = ends - padded
    is_expert = eid[:, :, None] == jnp.arange(N_EXPERTS, dtype=I32)
    pos = jnp.sum(jnp.where(is_expert, starts, 0), axis=-1) + rank
    n_tiles = (2 * T) // TILE_E + N_EXPERTS
    tile_index = jnp.arange(n_tiles, dtype=I32)
    tile_valid = tile_index * TILE_E < ends[-1]
    tile_block = jnp.minimum(tile_index, ends[-1] // TILE_E - 1)
    in_region = ends[None, :] <= (tile_block * TILE_E)[:, None]
    tile_expert = jnp.sum(in_region.astype(I32), axis=1)
    expert_of_tile = tile_expert[:, None] == jnp.arange(N_EXPERTS, dtype=I32)
    rows_before = tile_block * TILE_E - jnp.sum(jnp.where(expert_of_tile, starts, 0), axis=1)
    tile_rows = jnp.clip(jnp.sum(jnp.where(expert_of_tile, counts, 0), axis=1) - rows_before, 0, TILE_E)
    tile_rows = jnp.where(tile_valid, tile_rows, 0).astype(I32)

    xs = _scatter(starts, counts, pos, hn, n_tiles * TILE_E)
    experts = jnp.arange(N_EXPERTS, dtype=I32)
    later = jnp.logical_and(counts[None, :] > 0, experts[None, :] > experts[:, None])
    next_used = jnp.min(jnp.where(later, experts[None, :], N_EXPERTS), axis=1)
    next_used = jnp.where(next_used < N_EXPERTS, next_used, -1)
    tile_next = jnp.sum(jnp.where(expert_of_tile, next_used[None, :], 0), axis=1).astype(I32)
    y = _experts(tile_expert, tile_block, tile_rows, tile_next, xs, w_gate, w_up, w_down)
    out = _combine(pos, h, gate.T, y)
    return out.reshape(B, S, D)


def kernel(x, norm_mix, w_in, b_forget, q_norm_fox, k_norm_fox, q_norm_dil, k_norm_dil, out_norm_fox,
           out_norm_dil, w_out, norm_ffn, w_router_group, b_router_group, w_router_expert,
           b_router_expert, w_gate, w_up, w_down):
    h = x
    for l in range(norm_mix.shape[0]):
        h = _layer(h, norm_mix[l], w_in[l], b_forget[l], q_norm_fox[l], k_norm_fox[l], q_norm_dil[l],
                   k_norm_dil[l], out_norm_fox[l], out_norm_dil[l], w_out[l], norm_ffn[l],
                   w_router_group[l], b_router_group[l], w_router_expert[l], b_router_expert[l],
                   w_gate[l], w_up[l], w_down[l])
    return h
```

```python
import functools
import math

import jax
import jax.numpy as jnp
import numpy as np
from jax import lax
from jax.experimental import pallas as pl
from jax.experimental.pallas import tpu as pltpu

F32 = jnp.float32
BF16 = jnp.bfloat16
I32 = jnp.int32

D_MODEL = 1024
HEAD_DIM = 64
N_HEADS = 8
D_GRP = N_HEADS * HEAD_DIM
LANES = 128
HEADS_PER_TILE = LANES // HEAD_DIM
N_PAIRS = D_GRP // LANES
DIL_PATTERNS = ((128, 1), (512, 4), (2048, 16))
BLOCK = 128
ROPE_THETA = 10000.0
N_GROUPS = 4
EXPERTS_PER_GROUP = 8
N_EXPERTS = N_GROUPS * EXPERTS_PER_GROUP
D_EXPERT = 512
EPS = 1e-6
NEG = -1e30
LOG2E = 1.4426950408889634

TM_IN = 1024
TRI_ROWS = 256
TQ = 1024
TK_WIDE = 1024
TK_SUB = 256
FOX_AHEAD = 4
FOX_ZERO_BITS = 160.0
FOX_BOUND_SLACK = 1.02
FOX_SAFE_SPAN = 100.0
DIL_SAFE_SPAN = FOX_SAFE_SPAN * math.log(2.0)
DIL_SPAN = 2048
DIL_GROUP = 16
DIL_PRE = 4
TM_OUT = 1024
TM_ROWS = 512
TILE_E = 512
ROUTER_ROWS = 8 + N_EXPERTS
ROUTER_PAD = 48
ROW_CHUNKS = D_MODEL // LANES
DMA_UNROLL = 8
VMEM_LIMIT = 56 * 1024 * 1024


def _cparams(sem):
    return pltpu.CompilerParams(dimension_semantics=sem, vmem_limit_bytes=VMEM_LIMIT)


def _inproj_kernel(x_ref, gmix_ref, w_ref, wvt_ref, wf_ref, bf_ref, shift_ref, gqa_ref, gka_ref, gqb_ref, gkb_ref,
                   bd_ref, cos_ref, sin_ref, tri_ref,
                   qa_ref, ka_ref, va_ref, qb_ref, kb_ref, vb_ref, c_ref, carry_ref):
    @pl.when(pl.program_id(1) == 0)
    def _():
        carry_ref[...] = jnp.zeros_like(carry_ref)

    x = x_ref[0]
    ms = jnp.mean(x * x, axis=-1, keepdims=True)
    xn = (x * lax.rsqrt(ms + EPS) * gmix_ref[...]).astype(BF16)

    def seg(j):
        return jnp.dot(xn, w_ref[:, j * D_GRP:(j + 1) * D_GRP], preferred_element_type=F32)

    def head_norm(y, g_ref, scale):
        ss = jnp.dot((y * y).astype(BF16), bd_ref[...], preferred_element_type=F32) * (1.0 / HEAD_DIM)
        return y * lax.rsqrt(ss + EPS) * (g_ref[...] * scale)

    lane = lax.broadcasted_iota(I32, (x.shape[0], LANES), 1)
    first_half = (lane % HEAD_DIM) < (HEAD_DIM // 2)

    def spread(t):
        t = t + pltpu.roll(t, HEAD_DIM // 2, 1)
        return t + pltpu.roll(t, HEAD_DIM, 1)

    cos = spread(cos_ref[...])
    sin = jnp.where(first_half, -1.0, 1.0) * spread(sin_ref[...])

    def rope(y):
        outs = []
        for j in range(N_PAIRS):
            ys = y[:, j * LANES:(j + 1) * LANES]
            partner = jnp.where(first_half, pltpu.roll(ys, LANES - HEAD_DIM // 2, 1),
                                pltpu.roll(ys, HEAD_DIM // 2, 1))
            outs.append(ys * cos + partner * sin)
        return jnp.concatenate(outs, axis=1)

    scale = 1.0 / math.sqrt(HEAD_DIM)
    va_ref[0] = lax.dot_general(wvt_ref[...], xn, (((1,), (1,)), ((), ())),
                                preferred_element_type=F32).astype(va_ref.dtype)
    qb_ref[0] = rope(head_norm(seg(3), gqb_ref, scale)).astype(qb_ref.dtype)
    kb_ref[0] = rope(head_norm(seg(4), gkb_ref, 1.0)).astype(kb_ref.dtype)
    vb_ref[0] = seg(5).astype(vb_ref.dtype)

    fa = jnp.dot(xn, wf_ref[...], preferred_element_type=F32) + bf_ref[...]
    logf = jnp.minimum(fa, 0.0) - jnp.log1p(jnp.exp(-jnp.abs(fa)))
    hi = logf.astype(BF16)
    mid = (logf - hi.astype(F32)).astype(BF16)
    lo = (logf - hi.astype(F32) - mid.astype(F32)).astype(BF16)
    pieces = jnp.concatenate([hi, mid, lo], axis=1)
    carry = carry_ref[...]
    blocks = []
    for j in range(x.shape[0] // TRI_ROWS):
        parts = jnp.dot(tri_ref[...], pieces[j * TRI_ROWS:(j + 1) * TRI_ROWS, :], preferred_element_type=F32)
        blk = parts[:, :LANES] + parts[:, LANES:2 * LANES] + parts[:, 2 * LANES:] + carry
        carry = blk[TRI_ROWS - 1:, :]
        blocks.append(blk)
    c = jnp.concatenate(blocks, axis=0)
    carry_ref[...] = carry

    qa = head_norm(seg(0), gqa_ref, scale * LOG2E)
    ka = head_norm(seg(1), gka_ref, 1.0)
    c2 = c * LOG2E
    c_ref[0] = c2[:, :N_HEADS]
    ones = (jnp.where((lane >= HEAD_DIM + 3) & (lane < HEAD_DIM + 6), 1.0, 0.0)
            - jnp.where(lane == HEAD_DIM + 6, shift_ref[...], 0.0))
    for h in range(N_HEADS):
        cb = jnp.broadcast_to(c2[:, h:h + 1], (x.shape[0], LANES))
        hi = cb.astype(BF16).astype(F32)
        mid = (cb - hi).astype(BF16).astype(F32)
        lo = cb - hi - mid
        pieces = jnp.where(lane == HEAD_DIM, hi, jnp.where(lane == HEAD_DIM + 1, mid,
                           jnp.where(lane == HEAD_DIM + 2, lo, 0.0)))
        q_extra = pieces + ones
        k_extra = jnp.where(((lane >= HEAD_DIM) & (lane < HEAD_DIM + 3)) | (lane == HEAD_DIM + 6), 1.0, 0.0) \
            - pltpu.roll(pieces, 3, 1)
        j, odd = divmod(h, HEADS_PER_TILE)
        qp = qa[:, j * LANES:(j + 1) * LANES]
        kp = ka[:, j * LANES:(j + 1) * LANES]
        if odd:
            qp = pltpu.roll(qp, HEAD_DIM, 1)
            kp = pltpu.roll(kp, HEAD_DIM, 1)
        qa_ref[0, h] = jnp.where(lane < HEAD_DIM, qp, q_extra).astype(qa_ref.dtype)
        ka_ref[0, h] = jnp.where(lane < HEAD_DIM, kp, k_extra).astype(ka_ref.dtype)


def _inproj(x, gmix, w_main, w_vt, w_f, b_f, shift, gqa, gka, gqb, gkb, bd, cos_t, sin_t, tri):
    B, S, D = x.shape
    tm = TM_IN
    const = lambda shape: pl.BlockSpec(shape, lambda b, i: (0,) * len(shape))
    tok = lambda w, dt: jax.ShapeDtypeStruct((B, S, w), dt)
    tok_spec = lambda w: pl.BlockSpec((1, tm, w), lambda b, i: (b, i, 0))
    head_spec = pl.BlockSpec((1, N_HEADS, tm, LANES), lambda b, i: (b, 0, i, 0))
    head_shape = jax.ShapeDtypeStruct((B, N_HEADS, S, LANES), BF16)
    return pl.pallas_call(
        _inproj_kernel,
        grid=(B, S // tm),
        in_specs=[tok_spec(D), const((1, D)), const(w_main.shape), const(w_vt.shape), const(w_f.shape),
                  const((1, LANES)), const((1, LANES)),
                  const((1, D_GRP)), const((1, D_GRP)), const((1, D_GRP)), const((1, D_GRP)),
                  const((D_GRP, D_GRP)),
                  pl.BlockSpec((tm, LANES), lambda b, i: (i, 0)),
                  pl.BlockSpec((tm, LANES), lambda b, i: (i, 0)),
                  const(tri.shape)],
        out_specs=[head_spec, head_spec, pl.BlockSpec((1, D_GRP, tm), lambda b, i: (b, 0, i))]
        + [tok_spec(D_GRP)] * 3 + [tok_spec(N_HEADS)],
        out_shape=[head_shape, head_shape, jax.ShapeDtypeStruct((B, D_GRP, S), BF16),
                   tok(D_GRP, F32), tok(D_GRP, F32), tok(D_GRP, F32), tok(N_HEADS, F32)],
        scratch_shapes=[pltpu.VMEM((1, LANES), F32)],
        compiler_params=_cparams(("arbitrary", "arbitrary")),
        name="inproj",
    )(x, gmix, w_main, w_vt, w_f, b_f, shift, gqa, gka, gqb, gkb, bd, cos_t, sin_t, tri)


def _fox_kernel(first_ref, q_ref, k_ref, v_ref, o_ref, *, online):
    qi = pl.program_id(2)
    tq = q_ref.shape[2]

    def step(start, width, carry, diag):
        carry = list(carry)
        sub = min(TK_SUB, width)
        chunks = [(c, j) for c in range(width // sub) for j in range(HEADS_PER_TILE)]
        def first_query(c):
            return c * sub if diag else 0

        def score(c, j):
            k = k_ref[0, j, pl.ds(start + c * sub, sub), :]
            q = q_ref[0, j, first_query(c):, :]
            return lax.dot_general(k, q, (((1,), (1,)), ((), ())), preferred_element_type=F32)

        def tail(full, lo, new):
            return new if lo == 0 else jnp.concatenate([full[:, :lo], new], axis=1)

        scores = {cj: score(*cj) for cj in chunks[:FOX_AHEAD]}
        for n, (c, j) in enumerate(chunks):
                if n + FOX_AHEAD < len(chunks):
                    nxt = chunks[n + FOX_AHEAD]
                    scores[nxt] = score(*nxt)
                m, l, acc = carry[j]
                lo = first_query(c)
                vt = v_ref[0, :, pl.ds(start + c * sub, sub)]
                s = scores.pop((c, j))
                if diag:
                    key = lax.broadcasted_iota(I32, s.shape, 0)
                    qry = lax.broadcasted_iota(I32, s.shape, 1)
                    s = jnp.where(key <= qry, s, NEG)
                if online:
                    m_new = jnp.maximum(m[:, lo:], jnp.max(s, axis=0, keepdims=True))
                    alpha = jnp.exp2(m[:, lo:] - m_new)
                    p = jnp.exp2(s - m_new)
                    l_new = alpha * l[:, lo:] + jnp.sum(p, axis=0, keepdims=True)
                    acc_new = alpha * acc[:, lo:] + jnp.dot(vt, p.astype(BF16), preferred_element_type=F32)
                    m = tail(m, lo, m_new)
                else:
                    p = jnp.exp2(s)
                    l_new = l[:, lo:] + jnp.sum(p, axis=0, keepdims=True)
                    acc_new = acc[:, lo:] + jnp.dot(vt, p.astype(BF16), preferred_element_type=F32)
                carry[j] = (m, tail(l, lo, l_new), tail(acc, lo, acc_new))
        return tuple(carry)

    init = tuple((jnp.full((1, tq), NEG, F32), jnp.zeros((1, tq), F32), jnp.zeros((LANES, tq), F32))
                 for _ in range(HEADS_PER_TILE))
    assert TK_WIDE == tq and TK_WIDE % TK_SUB == 0
    per_wide = TK_WIDE // TK_SUB
    first = first_ref[(pl.program_id(0) * pl.num_programs(1) + pl.program_id(1)) * pl.num_programs(2) + qi]
    live = qi * per_wide - first
    n_narrow = lax.rem(live, per_wide)
    narrow_start = pl.multiple_of(first * TK_SUB, TK_SUB)
    carry = lax.switch(
        n_narrow,
        [lambda c: c] + [lambda c, w=w: step(narrow_start, w * TK_SUB, c, False) for w in range(1, per_wide)],
        init)
    wide_start = (first + n_narrow) * TK_SUB
    carry = lax.fori_loop(
        0, lax.div(live, per_wide),
        lambda i, c: step(pl.multiple_of(wide_start + i * TK_WIDE, TK_SUB), TK_WIDE, c, False), carry)
    carry = step(pl.multiple_of(qi * tq, tq), tq, carry, True)
    outs = [acc / l for (_, l, acc) in carry]
    feat = lax.broadcasted_iota(I32, (LANES, tq), 0)
    o_ref[0] = jnp.where(feat < HEAD_DIM, outs[0], outs[1]).T.astype(o_ref.dtype)


def _fox(first_tile, qa, ka, va_t, online):
    B, _, S, _ = qa.shape
    grid_spec = pltpu.PrefetchScalarGridSpec(
        num_scalar_prefetch=1,
        grid=(B, N_PAIRS, S // TQ),
        in_specs=[pl.BlockSpec((1, HEADS_PER_TILE, TQ, LANES), lambda b, hp, i, first: (b, hp, i, 0)),
                  pl.BlockSpec((1, HEADS_PER_TILE, S, LANES), lambda b, hp, i, first: (b, hp, 0, 0)),
                  pl.BlockSpec((1, LANES, S), lambda b, hp, i, first: (b, hp, 0))],
        out_specs=pl.BlockSpec((1, TQ, LANES), lambda b, hp, i, first: (b, i, hp)),
    )
    return pl.pallas_call(
        functools.partial(_fox_kernel, online=online),
        grid_spec=grid_spec,
        out_shape=jax.ShapeDtypeStruct((B, S, D_GRP), BF16),
        compiler_params=_cparams(("arbitrary", "arbitrary", "arbitrary")),
        name="fox_online" if online else "fox",
    )(first_tile, qa, ka, va_t)


def _dilated_kernel(shift_ref, q_ref, kp_ref, kc_ref, vp_ref, vc_ref, o_ref, qq, kk, vv, qq4, kk4, vv4, osc, lsc,
                    *, bounded):
    u = pl.program_id(1)
    span = q_ref.shape[1]
    qq[...] = q_ref[0]
    kk[0:span, :] = kp_ref[0]
    kk[span:2 * span, :] = kc_ref[0]
    vv[0:span, :] = vp_ref[0]
    vv[span:2 * span, :] = vc_ref[0]
    for src, dst in ((qq, qq4), (kk, kk4), (vv, vv4)):
        part = src.shape[0] // DIL_PRE
        for a in range(DIL_PRE):
            dst[a * part:(a + 1) * part, :] = src[pl.ds(a, part, stride=DIL_PRE), :]

    def rows(buf, buf4, start, n, d):
        if d % DIL_PRE:
            return buf[pl.ds(start, n, stride=d), :]
        part = buf4.shape[0] // DIL_PRE
        a = lax.rem(start, DIL_PRE)
        return buf4[pl.ds(a * part + lax.div(start, DIL_PRE), n, stride=d // DIL_PRE), :]

    lane = lax.broadcasted_iota(I32, (BLOCK, LANES), 1)
    ql = lax.broadcasted_iota(I32, (BLOCK, 2 * BLOCK), 0)
    kl = lax.broadcasted_iota(I32, (BLOCK, 2 * BLOCK), 1)
    dist = ql + BLOCK - kl
    band = (dist >= 0) & (dist <= BLOCK)
    live = -shift_ref[0, 0] if bounded else 0.0
    bias = jnp.where(band, live, NEG)
    bias_first = jnp.where(band & (kl >= BLOCK), live, NEG)

    def scores(q_start, k_start, d, first):
        qs = rows(qq, qq4, q_start, BLOCK, d).astype(BF16)
        ks = rows(kk, kk4, k_start, 2 * BLOCK, d).astype(BF16)
        mask = jnp.where(first, bias_first, bias)
        out = []
        for j in range(HEADS_PER_TILE):
            qj = jnp.where(lane // HEAD_DIM == j, qs, jnp.zeros_like(qs))
            out.append(lax.dot_general(qj, ks, (((1,), (1,)), ((), ())), preferred_element_type=F32) + mask)
        return out

    def finish(s_heads, k_start, d):
        vs = rows(vv, vv4, k_start, 2 * BLOCK, d).astype(BF16)
        o_heads, lse_heads = [], []
        for s in s_heads:
            if bounded:
                p = jnp.exp(s)
                o_heads.append(jnp.dot(p.astype(BF16), vs, preferred_element_type=F32))
                lse_heads.append(jnp.sum(p, axis=-1, keepdims=True))
            else:
                m = jnp.max(s, axis=-1, keepdims=True)
                p = jnp.exp(s - m)
                l = jnp.sum(p, axis=-1, keepdims=True)
                o_heads.append(jnp.dot((p / l).astype(BF16), vs, preferred_element_type=F32))
                lse_heads.append(m + jnp.log(l))
        o = jnp.where(lane < HEAD_DIM, o_heads[0], o_heads[1])
        lse = jnp.where(lane < HEAD_DIM, lse_heads[0], lse_heads[1])
        return o, lse

    for pidx, (window, d) in enumerate(DIL_PATTERNS):
        assert window // d == BLOCK
        unit = d * BLOCK
        n_problems = (span // unit) * d
        assert n_problems % DIL_GROUP == 0

        def body(g, _, pidx=pidx, d=d, unit=unit):
            starts, s_all = [], []
            for t in range(DIL_GROUP):
                idx = g * DIL_GROUP + t
                w = idx // d
                q_start = w * unit + (idx - w * d)
                k_start = span - unit + q_start
                starts.append((q_start, k_start))
                s_all.append(scores(q_start, k_start, d, jnp.logical_and(u == 0, w == 0)))
            for (q_start, k_start), s_heads in zip(starts, s_all):
                o, lse = finish(s_heads, k_start, d)
                osc[pidx, pl.ds(q_start, BLOCK, stride=d), :] = o
                lsc[pidx, pl.ds(q_start, BLOCK, stride=d), :] = lse
            return 0

        lax.fori_loop(0, n_problems // DIL_GROUP, body, 0)

    if bounded:
        num = osc[0] + osc[1] + osc[2]
        den = lsc[0] + lsc[1] + lsc[2]
    else:
        mx = jnp.maximum(jnp.maximum(lsc[0], lsc[1]), lsc[2])
        num = jnp.zeros((span, LANES), F32)
        den = jnp.zeros((span, LANES), F32)
        for pidx in range(len(DIL_PATTERNS)):
            e = jnp.exp(lsc[pidx] - mx)
            num = num + e * osc[pidx]
            den = den + e
    o_ref[0] = (num / den).astype(o_ref.dtype)


def _dilated(shift, qb, kb, vb, bounded):
    B, S, _ = qb.shape
    span = DIL_SPAN
    cur = pl.BlockSpec((1, span, LANES), lambda b, u, hp: (b, u, hp))
    prev = pl.BlockSpec((1, span, LANES), lambda b, u, hp: (b, jnp.maximum(u - 1, 0), hp))
    return pl.pallas_call(
        functools.partial(_dilated_kernel, bounded=bounded),
        grid=(B, S // span, N_PAIRS),
        in_specs=[pl.BlockSpec(memory_space=pltpu.SMEM), cur, prev, cur, prev, cur],
        out_specs=cur,
        out_shape=jax.ShapeDtypeStruct((B, S, D_GRP), BF16),
        scratch_shapes=[pltpu.VMEM((span, LANES), F32),
                        pltpu.VMEM((2 * span, LANES), F32), pltpu.VMEM((2 * span, LANES), F32),
                        pltpu.VMEM((span, LANES), F32),
                        pltpu.VMEM((2 * span, LANES), F32), pltpu.VMEM((2 * span, LANES), F32),
                        pltpu.VMEM((len(DIL_PATTERNS), span, LANES), F32),
                        pltpu.VMEM((len(DIL_PATTERNS), span, LANES), F32)],
        compiler_params=_cparams(("arbitrary", "arbitrary", "arbitrary")),
        name="dilated" if bounded else "dilated_exact",
    )(shift, qb, kb, kb, vb, vb)


def _store_row_tiles(ref, stage_ref, x, first_row=0):
    n = x.shape[0]
    target = ref if stage_ref is None else stage_ref
    for c in range(ROW_CHUNKS):
        target[pl.ds(first_row * ROW_CHUNKS + c, n, stride=ROW_CHUNKS), :] = x[:, c * LANES:(c + 1) * LANES]
    if stage_ref is not None:
        rows = slice(first_row * ROW_CHUNKS, (first_row + n) * ROW_CHUNKS)
        ref[rows, :] = stage_ref[rows, :].astype(ref.dtype)


def _load_row_tiles(ref, stage_ref, n, first_row=0):
    source = ref
    if stage_ref is not None:
        rows = slice(first_row * ROW_CHUNKS, (first_row + n) * ROW_CHUNKS)
        stage_ref[rows, :] = ref[rows, :].astype(F32)
        source = stage_ref
    return jnp.concatenate([source[pl.ds(first_row * ROW_CHUNKS + c, n, stride=ROW_CHUNKS), :]
                            for c in range(ROW_CHUNKS)], axis=1)


def _row_tile_copy(src_ref, src_row, dst_ref, dst_row, sem):
    src = src_ref.at[pl.ds(pl.multiple_of(src_row * ROW_CHUNKS, ROW_CHUNKS), ROW_CHUNKS), :]
    dst = dst_ref.at[pl.ds(pl.multiple_of(dst_row * ROW_CHUNKS, ROW_CHUNKS), ROW_CHUNKS), :]
    return pltpu.make_async_copy(src, dst, sem)


def _outproj_kernel(oa_ref, ob_ref, x_ref, gfox_ref, gdil_ref, wo_ref, gffn_ref, wr_ref, br_ref, upper_ref,
                    h_ref, hn_ref, eid_ref, gate_ref, rank_ref, cnt_ref, run_ref, stage_ref):
    @pl.when(pl.program_id(0) == 0)
    def _():
        run_ref[...] = jnp.zeros_like(run_ref)

    def norm(y, g):
        ms = jnp.mean(y * y, axis=-1, keepdims=True)
        return y * lax.rsqrt(ms + EPS) * g

    a = norm(oa_ref[...].astype(F32), gfox_ref[...]).astype(BF16)
    b = norm(ob_ref[...].astype(F32), gdil_ref[...]).astype(BF16)
    mix = (jnp.dot(a, wo_ref[0:D_GRP, :], preferred_element_type=F32)
           + jnp.dot(b, wo_ref[D_GRP:2 * D_GRP, :], preferred_element_type=F32))
    h = x_ref[...] + mix
    h_ref[...] = h
    hn = norm(h, gffn_ref[...])
    _store_row_tiles(hn_ref, stage_ref, hn)

    hn_hi = hn.astype(BF16)
    hn_lo = (hn - hn_hi.astype(F32)).astype(BF16)
    nt = (((1,), (1,)), ((), ()))
    r = lax.dot_general(wr_ref[...], hn_hi, nt, preferred_element_type=F32)
    r_lo = lax.dot_general(wr_ref[:ROUTER_PAD, :], hn_lo, nt, preferred_element_type=F32)
    z = (r[:ROUTER_ROWS, :] + r[ROUTER_PAD:ROUTER_PAD + ROUTER_ROWS, :] + r_lo[:ROUTER_ROWS, :] + br_ref[...])
    tm = z.shape[1]
    best = z[0:1, :]
    g_sel = jnp.zeros((1, tm), I32)
    for g in range(1, N_GROUPS):
        better = z[g:g + 1, :] > best
        g_sel = jnp.where(better, g, g_sel)
        best = jnp.maximum(best, z[g:g + 1, :])
    den = jnp.zeros((1, tm), F32)
    for g in range(N_GROUPS):
        den = den + jnp.exp(z[g:g + 1, :] - best)
    pg_top = 1.0 / den

    ze = jnp.zeros((EXPERTS_PER_GROUP, tm), F32)
    for g in range(N_GROUPS):
        ze = jnp.where(g_sel == g, z[8 + g * EXPERTS_PER_GROUP:8 + (g + 1) * EXPERTS_PER_GROUP, :], ze)
    e_iota = lax.broadcasted_iota(I32, ze.shape, 0)
    v1 = jnp.max(ze, axis=0, keepdims=True)
    i1 = jnp.min(jnp.where(ze == v1, e_iota, EXPERTS_PER_GROUP), axis=0, keepdims=True)
    ze2 = jnp.where(e_iota == i1, -jnp.inf, ze)
    v2 = jnp.max(ze2, axis=0, keepdims=True)
    i2 = jnp.min(jnp.where(ze2 == v2, e_iota, EXPERTS_PER_GROUP), axis=0, keepdims=True)
    e2 = jnp.exp(v2 - v1)
    inv = 1.0 / (1.0 + e2)
    gate1 = inv * pg_top
    gate2 = e2 * inv * pg_top
    eid1 = g_sel * EXPERTS_PER_GROUP + i1
    eid2 = g_sel * EXPERTS_PER_GROUP + i2

    x_iota = lax.broadcasted_iota(I32, (N_EXPERTS, tm), 0)
    hot1 = x_iota == eid1
    hot2 = x_iota == eid2
    multi = jnp.logical_or(hot1, hot2)
    before = jnp.dot(multi.astype(BF16), upper_ref[...], preferred_element_type=F32)
    slot = before + run_ref[:, 0:1]
    rank1 = jnp.sum(jnp.where(hot1, slot, 0.0), axis=0, keepdims=True)
    rank2 = jnp.sum(jnp.where(hot2, slot, 0.0), axis=0, keepdims=True)
    run_ref[...] = run_ref[...] + jnp.sum(multi.astype(F32), axis=1, keepdims=True)

    eid_ref[...] = jnp.concatenate([eid1, eid2], axis=0)
    gate_ref[...] = jnp.concatenate([gate1, gate2], axis=0)
    rank_ref[...] = jnp.concatenate([rank1, rank2], axis=0).astype(I32)
    cnt_ref[...] = run_ref[...].astype(I32)


def _outproj(oa, ob, x2, gfox, gdil, w_out, gffn, w_r, b_r, upper):
    T, D = x2.shape
    tm = TM_OUT
    const = lambda shape: pl.BlockSpec(shape, lambda i: (0,) * len(shape))
    tok = lambda w: pl.BlockSpec((tm, w), lambda i: (i, 0))
    lanes2 = pl.BlockSpec((2, tm), lambda i: (0, i))
    return pl.pallas_call(
        _outproj_kernel,
        grid=(T // tm,),
        in_specs=[tok(D_GRP), tok(D_GRP), tok(D), const((1, D_GRP)), const((1, D_GRP)), const((D, D)),
                  const((1, D)), const((2 * ROUTER_PAD, D)), const((ROUTER_ROWS, 1)), const((tm, tm))],
        out_specs=[tok(D), pl.BlockSpec((tm * ROW_CHUNKS, LANES), lambda i: (i, 0)),
                   lanes2, lanes2, lanes2, const((N_EXPERTS, LANES))],
        out_shape=[jax.ShapeDtypeStruct((T, D), F32), jax.ShapeDtypeStruct((T * ROW_CHUNKS, LANES), BF16),
                   jax.ShapeDtypeStruct((2, T), I32), jax.ShapeDtypeStruct((2, T), F32),
                   jax.ShapeDtypeStruct((2, T), I32), jax.ShapeDtypeStruct((N_EXPERTS, LANES), I32)],
        scratch_shapes=[pltpu.VMEM((N_EXPERTS, LANES), F32), pltpu.VMEM((tm * ROW_CHUNKS, LANES), F32)],
        compiler_params=_cparams(("arbitrary",)),
        name="outproj",
    )(oa, ob, x2, gfox, gdil, w_out, gffn, w_r, b_r, upper)


def _scatter_kernel(starts_ref, cnt_ref, pos0_ref, pos1_ref, hn_ref, xs_ref, ring, sems, zero_sem, *, n_steps):
    i = pl.program_id(0)
    tm = hn_ref.shape[0] // ROW_CHUNKS
    slot = lax.rem(i, 2)

    def wait_slot(s):
        for _ in range(2):
            pltpu.make_async_copy(ring.at[s], xs_ref.at[pl.ds(0, tm * ROW_CHUNKS), :], sems.at[s]).wait()

    @pl.when(i >= 2)
    def _():
        wait_slot(slot)

    ring[slot] = hn_ref[...]

    def start(r, _):
        for k, pos_ref in enumerate((pos0_ref, pos1_ref)):
            _row_tile_copy(ring.at[slot], r, xs_ref, pos_ref[r], sems.at[slot]).start(priority=k)
        return 0

    lax.fori_loop(0, tm, start, 0, unroll=DMA_UNROLL)

    @pl.when(i == n_steps - 1)
    def _():
        wait_slot(slot)
        if n_steps > 1:
            wait_slot(1 - slot)
        ring[0] = jnp.zeros((tm * ROW_CHUNKS, LANES), ring.dtype)

        def pad_expert(e, _, wait):
            n_pad = lax.rem(TILE_E - lax.rem(cnt_ref[e], TILE_E), TILE_E)
            first = starts_ref[e] + cnt_ref[e]
            size = TILE_E // 2
            while size >= 1:
                row0 = first + (n_pad & ~(2 * size - 1))

                @pl.when((n_pad & size) != 0)
                def _(size=size, row0=row0):
                    copy = pltpu.make_async_copy(
                        ring.at[0, pl.ds(0, size * ROW_CHUNKS), :],
                        xs_ref.at[pl.ds(pl.multiple_of(row0 * ROW_CHUNKS, ROW_CHUNKS), size * ROW_CHUNKS), :],
                        zero_sem)
                    copy.wait() if wait else copy.start()

                size //= 2
            return 0

        lax.fori_loop(0, N_EXPERTS, functools.partial(pad_expert, wait=False), 0)
        last = N_EXPERTS - 1
        used_rows = starts_ref[last] + cnt_ref[last] + lax.rem(TILE_E - lax.rem(cnt_ref[last], TILE_E), TILE_E)
        n_tail = xs_ref.shape[0] // (tm * ROW_CHUNKS) - used_rows // tm

        def tail_copy(t):
            row0 = pl.multiple_of((used_rows + t * tm) * ROW_CHUNKS, tm * ROW_CHUNKS)
            return pltpu.make_async_copy(ring.at[0], xs_ref.at[pl.ds(row0, tm * ROW_CHUNKS), :], zero_sem)

        lax.fori_loop(0, n_tail, lambda t, c: (tail_copy(t).start(), c)[1], 0)
        lax.fori_loop(0, N_EXPERTS, functools.partial(pad_expert, wait=True), 0)
        lax.fori_loop(0, n_tail, lambda t, c: (tail_copy(t).wait(), c)[1], 0)


def _scatter(starts, cnt, pos, hn, n_rows):
    T = hn.shape[0] // ROW_CHUNKS
    tm = TM_ROWS
    assert TILE_E % tm == 0 and TILE_E // 2 <= tm
    n_steps = T // tm
    grid_spec = pltpu.PrefetchScalarGridSpec(
        num_scalar_prefetch=2,
        grid=(n_steps,),
        in_specs=[pl.BlockSpec((tm,), lambda i, starts, cnt: (i,), memory_space=pltpu.SMEM),
                  pl.BlockSpec((tm,), lambda i, starts, cnt: (i,), memory_space=pltpu.SMEM),
                  pl.BlockSpec((tm * ROW_CHUNKS, LANES), lambda i, starts, cnt: (i, 0))],
        out_specs=pl.BlockSpec(memory_space=pl.ANY),
        scratch_shapes=[pltpu.VMEM((2, tm * ROW_CHUNKS, LANES), hn.dtype),
                        pltpu.SemaphoreType.DMA((2,)), pltpu.SemaphoreType.DMA(())],
    )
    return pl.pallas_call(
        functools.partial(_scatter_kernel, n_steps=n_steps),
        grid_spec=grid_spec,
        out_shape=jax.ShapeDtypeStruct((n_rows * ROW_CHUNKS, LANES), hn.dtype),
        compiler_params=_cparams(("arbitrary",)),
        name="scatter_rows",
    )(starts, cnt, pos[0], pos[1], hn)


def _experts_kernel(te_ref, tb_ref, tr_ref, nxt_ref, xs_ref, wg_hbm, wu_hbm, wd_hbm, y_ref,
                    wgu_bf, wd_bf, stage_ref, wg_f, wu_f, wd_f, wsem):
    del tb_ref
    i = pl.program_id(0)
    rows = tr_ref[i]
    new_expert = jnp.logical_or(i == 0, te_ref[i] != te_ref[jnp.maximum(i - 1, 0)])

    def fetch(e):
        return [pltpu.make_async_copy(src.at[e], dst, wsem)
                for src, dst in ((wg_hbm, wg_f), (wu_hbm, wu_f), (wd_hbm, wd_f))]

    @pl.when(i == 0)
    def _():
        for cp in fetch(te_ref[0]):
            cp.start()

    @pl.when(jnp.logical_and(rows > 0, new_expert))
    def _():
        for cp in fetch(te_ref[i]):
            cp.wait()
        wgu_bf[:, :D_EXPERT] = wg_f[...].astype(BF16)
        wgu_bf[:, D_EXPERT:] = wu_f[...].astype(BF16)
        wd_bf[...] = wd_f[...].astype(BF16)

        @pl.when(nxt_ref[i] >= 0)
        def _():
            for cp in fetch(nxt_ref[i]):
                cp.start(priority=1)

    half = TILE_E // 2
    for part in range(2):
        @pl.when(rows > part * half)
        def _(part=part):
            x = _load_row_tiles(xs_ref, stage_ref, half, part * half).astype(BF16)
            gu = jnp.dot(x, wgu_bf[...], preferred_element_type=F32)
            g, up = gu[:, :D_EXPERT], gu[:, D_EXPERT:]
            hmid = (g * jax.nn.sigmoid(g) * up).astype(BF16)
            _store_row_tiles(y_ref, None, jnp.dot(hmid, wd_bf[...], preferred_element_type=F32), part * half)

        @pl.when(rows <= part * half)
        def _(part=part):
            y_ref[part * half * ROW_CHUNKS:(part + 1) * half * ROW_CHUNKS, :] = (
                jnp.zeros((half * ROW_CHUNKS, LANES), y_ref.dtype))


def _experts(tile_expert, tile_block, tile_rows, tile_next, xs, w_gate, w_up, w_down):
    n_tiles = tile_expert.shape[0]
    D = D_MODEL
    rows_spec = pl.BlockSpec((TILE_E * ROW_CHUNKS, LANES), lambda i, te, tb, tr, nx: (tb[i], 0))
    in_hbm = pl.BlockSpec(memory_space=pl.ANY)
    grid_spec = pltpu.PrefetchScalarGridSpec(
        num_scalar_prefetch=4,
        grid=(n_tiles,),
        in_specs=[rows_spec, in_hbm, in_hbm, in_hbm],
        out_specs=pl.BlockSpec((TILE_E * ROW_CHUNKS, LANES), lambda i, te, tb, tr, nx: (i, 0)),
        scratch_shapes=[pltpu.VMEM((D, 2 * D_EXPERT), BF16),
                        pltpu.VMEM((D_EXPERT, D), BF16), pltpu.VMEM((TILE_E * ROW_CHUNKS, LANES), F32),
                        pltpu.VMEM((D, D_EXPERT), F32), pltpu.VMEM((D, D_EXPERT), F32),
                        pltpu.VMEM((D_EXPERT, D), F32), pltpu.SemaphoreType.DMA(())],
    )
    return pl.pallas_call(
        _experts_kernel,
        grid_spec=grid_spec,
        out_shape=jax.ShapeDtypeStruct(xs.shape, F32),
        compiler_params=_cparams(("arbitrary",)),
        name="experts",
    )(tile_expert, tile_block, tile_rows, tile_next, xs, w_gate, w_up, w_down)


def _combine_kernel(pos0_ref, pos1_ref, next0_ref, next1_ref, h_ref, gate_ref, y_ref, o_ref, ybuf, sems, *, n_steps):
    i = pl.program_id(0)
    tm = h_ref.shape[0]
    slot = lax.rem(i, 2)

    def gather(p_refs, s):
        def start(r, _):
            for k, p_ref in enumerate(p_refs):
                _row_tile_copy(y_ref, p_ref[r], ybuf.at[s, k], r, sems.at[s]).start(priority=k)
            return 0

        lax.fori_loop(0, tm, start, 0, unroll=DMA_UNROLL)

    @pl.when(i == 0)
    def _():
        gather((pos0_ref, pos1_ref), slot)

    @pl.when(i + 1 < n_steps)
    def _():
        gather((next0_ref, next1_ref), 1 - slot)

    for k in range(2):
        pltpu.make_async_copy(y_ref.at[pl.ds(0, tm * ROW_CHUNKS), :], ybuf.at[slot, k], sems.at[slot]).wait()
    g = gate_ref[...]
    o_ref[...] = (h_ref[...] + g[:, 0:1] * _load_row_tiles(ybuf.at[slot, 0], None, tm)
                  + g[:, 1:2] * _load_row_tiles(ybuf.at[slot, 1], None, tm))


def _combine(pos, h, gate_t, y):
    T, D = h.shape
    tm = TM_ROWS
    n_steps = T // tm
    return pl.pallas_call(
        functools.partial(_combine_kernel, n_steps=n_steps),
        grid=(n_steps,),
        in_specs=[pl.BlockSpec((tm,), lambda i: (i,), memory_space=pltpu.SMEM),
                  pl.BlockSpec((tm,), lambda i: (i,), memory_space=pltpu.SMEM),
                  pl.BlockSpec((tm,), lambda i: (jnp.minimum(i + 1, n_steps - 1),), memory_space=pltpu.SMEM),
                  pl.BlockSpec((tm,), lambda i: (jnp.minimum(i + 1, n_steps - 1),), memory_space=pltpu.SMEM),
                  pl.BlockSpec((tm, D), lambda i: (i, 0)),
                  pl.BlockSpec((tm, 2), lambda i: (i, 0)),
                  pl.BlockSpec(memory_space=pl.ANY)],
        out_specs=pl.BlockSpec((tm, D), lambda i: (i, 0)),
        out_shape=jax.ShapeDtypeStruct((T, D), F32),
        scratch_shapes=[pltpu.VMEM((2, 2, tm * ROW_CHUNKS, LANES), y.dtype), pltpu.SemaphoreType.DMA((2,))],
        compiler_params=_cparams(("arbitrary",)),
        name="combine",
    )(pos[0], pos[1], pos[0], pos[1], h, gate_t, y)


def _rope_tables(S):
    inv_freq = 1.0 / (ROPE_THETA ** (np.arange(0, HEAD_DIM, 2, dtype=np.float64) / HEAD_DIM))
    ang = np.arange(S, dtype=np.float64)[:, None] * inv_freq[None, :]
    widen = lambda t: jnp.asarray(np.pad(t, ((0, 0), (0, LANES - HEAD_DIM // 2))).astype(np.float32))
    return widen(np.cos(ang)), widen(np.sin(ang))


def _layer(x, norm_mix, w_in, b_forget, q_norm_fox, k_norm_fox, q_norm_dil, k_norm_dil,
           out_norm_fox, out_norm_dil, w_out, norm_ffn, w_router_group, b_router_group,
           w_router_expert, b_router_expert, w_gate, w_up, w_down):
    B, S, D = x.shape
    T = B * S
    n_main = 6 * D_GRP

    w_main = w_in.astype(BF16)
    w_vt = w_main[:, 2 * D_GRP:3 * D_GRP].T
    w_f = jnp.pad(w_main[:, n_main:], ((0, 0), (0, LANES - N_HEADS)))
    b_f = jnp.pad(b_forget, (0, LANES - N_HEADS))[None, :]
    per_head = lambda g: jnp.tile(g, N_HEADS)[None, :]
    bd = jnp.kron(jnp.eye(N_HEADS, dtype=F32), jnp.ones((HEAD_DIM, HEAD_DIM), F32)).astype(BF16)
    cos_t, sin_t = _rope_tables(S)
    tri = jnp.tril(jnp.ones((TRI_ROWS, TRI_ROWS), F32)).astype(BF16)
    upper = jnp.triu(jnp.ones((TM_OUT, TM_OUT), F32), k=1).astype(BF16)
    w_r = jnp.concatenate([
        jnp.pad(w_router_group.T, ((0, 8 - N_GROUPS), (0, 0))),
        w_router_expert.transpose(0, 2, 1).reshape(N_EXPERTS, D)], axis=0)
    w_r = jnp.pad(w_r, ((0, ROUTER_PAD - ROUTER_ROWS), (0, 0)))
    w_r_hi = w_r.astype(BF16)
    w_r = jnp.concatenate([w_r_hi, (w_r - w_r_hi.astype(F32)).astype(BF16)], axis=0)
    b_r = jnp.concatenate([jnp.pad(b_router_group, (0, 8 - N_GROUPS)), b_router_expert.reshape(-1)])[:, None]

    bound = (HEAD_DIM / math.sqrt(HEAD_DIM)) * LOG2E * jnp.max(jnp.abs(q_norm_fox)) * jnp.max(jnp.abs(k_norm_fox))
    shift = FOX_BOUND_SLACK * bound + 1.0
    qa, ka, va, qb, kb, vb, c2 = _inproj(
        x, norm_mix[None, :], w_main, w_vt, w_f, b_f, jnp.full((1, LANES), shift, F32),
        per_head(q_norm_fox), per_head(k_norm_fox),
        per_head(q_norm_dil), per_head(k_norm_dil), bd, cos_t, sin_t, tri)
    n_q, n_k = S // TQ, S // TK_SUB
    c_first = c2[:, ::TQ, :].reshape(B, n_q, 1, N_PAIRS, HEADS_PER_TILE)
    c_last = c2[:, TK_SUB - 1::TK_SUB, :].reshape(B, 1, n_k, N_PAIRS, HEADS_PER_TILE)
    all_zero = jnp.all(c_first - c_last < -FOX_ZERO_BITS, axis=-1)
    live_so_far = jnp.einsum("bqkp,kj->bqjp", 1 - all_zero.astype(I32),
                             jnp.triu(jnp.ones((n_k, n_k), I32)))
    leading = (live_so_far == 0).astype(I32)
    before = (jnp.arange(n_k) * TK_SUB + TK_SUB <= jnp.arange(n_q)[:, None] * TQ)
    first_tile = jnp.sum(leading * before[None, :, :, None].astype(I32), axis=2)
    first_tile = first_tile.transpose(0, 2, 1).reshape(-1).astype(I32)
    oa = lax.cond(2.0 * shift <= FOX_SAFE_SPAN,
                  lambda: _fox(first_tile, qa, ka, va, online=False),
                  lambda: _fox(jnp.zeros_like(first_tile), qa, ka, va, online=True))
    dil_shift = (FOX_BOUND_SLACK * (HEAD_DIM / math.sqrt(HEAD_DIM))
                 * jnp.max(jnp.abs(q_norm_dil)) * jnp.max(jnp.abs(k_norm_dil)) + 1.0)
    dil_shift_arr = jnp.full((1, 1), dil_shift, F32)
    ob = lax.cond(2.0 * dil_shift <= DIL_SAFE_SPAN,
                  lambda: _dilated(dil_shift_arr, qb, kb, vb, bounded=True),
                  lambda: _dilated(dil_shift_arr, qb, kb, vb, bounded=False))

    h, hn, eid, gate, rank, cnt = _outproj(
        oa.reshape(T, D_GRP), ob.reshape(T, D_GRP), x.reshape(T, D), out_norm_fox[None, :],
        out_norm_dil[None, :], w_out.astype(BF16), norm_ffn[None, :], w_r, b_r, upper)

    counts = cnt[:, 0]
    padded = ((counts + TILE_E - 1) // TILE_E) * TILE_E
    ends = jnp.cumsum(padded)
    starts = ends - padded
    is_expert = eid[:, :, None] == jnp.arange(N_EXPERTS, dtype=I32)
    pos = jnp.sum(jnp.where(is_expert, starts, 0), axis=-1) + rank
    n_tiles = (2 * T) // TILE_E + N_EXPERTS
    tile_index = jnp.arange(n_tiles, dtype=I32)
    tile_valid = tile_index * TILE_E < ends[-1]
    tile_block = jnp.minimum(tile_index, ends[-1] // TILE_E - 1)
    in_region = ends[None, :] <= (tile_block * TILE_E)[:, None]
    tile_expert = jnp.sum(in_region.astype(I32), axis=1)
    expert_of_tile = tile_expert[:, None] == jnp.arange(N_EXPERTS, dtype=I32)
    rows_before = tile_block * TILE_E - jnp.sum(jnp.where(expert_of_tile, starts, 0), axis=1)
    tile_rows = jnp.clip(jnp.sum(jnp.where(expert_of_tile, counts, 0), axis=1) - rows_before, 0, TILE_E)
    tile_rows = jnp.where(tile_valid, tile_rows, 0).astype(I32)

    xs = _scatter(starts, counts, pos, hn, n_tiles * TILE_E)
    experts = jnp.arange(N_EXPERTS, dtype=I32)
    later = jnp.logical_and(counts[None, :] > 0, experts[None, :] > experts[:, None])
    next_used = jnp.min(jnp.where(later, experts[None, :], N_EXPERTS), axis=1)
    next_used = jnp.where(next_used < N_EXPERTS, next_used, -1)
    tile_next = jnp.sum(jnp.where(expert_of_tile, next_used[None, :], 0), axis=1).astype(I32)
    y = _experts(tile_expert, tile_block, tile_rows, tile_next, xs, w_gate, w_up, w_down)
    out = _combine(pos, h, gate.T, y)
    return out.reshape(B, S, D)


def kernel(x, norm_mix, w_in, b_forget, q_norm_fox, k_norm_fox, q_norm_dil, k_norm_dil, out_norm_fox,
           out_norm_dil, w_out, norm_ffn, w_router_group, b_router_group, w_router_expert,
           b_router_expert, w_gate, w_up, w_down):
    h = x
    for l in range(norm_mix.shape[0]):
        h = _layer(h, norm_mix[l], w_in[l], b_forget[l], q_norm_fox[l], k_norm_fox[l], q_norm_dil[l],
                   k_norm_dil[l], out_norm_fox[l], out_norm_dil[l], w_out[l], norm_ffn[l],
                   w_router_group[l], b_router_group[l], w_router_expert[l], b_router_expert[l],
                   w_gate[l], w_up[l], w_down[l])
    return h
```
